```python
import math
import jax, jax.numpy as jnp
from jax import lax
import numpy as np

D_MODEL = 1024
BATCH = 8
SEQ = 2048
DEPTH = 2

HEAD_DIM = 64
SB_WIDTH = D_MODEL // 2
N_HEADS_SB = SB_WIDTH // HEAD_DIM
DSA_WIDTH = D_MODEL // 2
N_HEADS_DSA = DSA_WIDTH // HEAD_DIM
KV_DIM = HEAD_DIM
IDX_HEADS = 8
IDX_DIM = 64
INDEX_TOPK_MAX = 256
BLOCK_Q = 128
REL_BUCKETS = 32
REL_MAX_DIST = 128
N_EXPERTS = 16
N_GROUPS = 4
EXPERTS_PER_GROUP = N_EXPERTS // N_GROUPS
TOP_K = 2
D_EXPERT = D_MODEL // 2
N_MOD = 6
EPS = 1e-6

IN_SIZES = [SB_WIDTH, SB_WIDTH, SB_WIDTH,
            DSA_WIDTH, KV_DIM, KV_DIM,
            IDX_HEADS * IDX_DIM, IDX_DIM, IDX_HEADS,
            D_MODEL, D_MODEL]
N_IN = int(sum(IN_SIZES))
IN_OFFSETS = [int(o) for o in np.cumsum(IN_SIZES)[:-1]]

kernel_name = "hybrid_stickbreak_dsa_grouped_moe_adaln"


def rmsnorm(x, g):
    xf = x.astype(jnp.float32)
    y = xf * lax.rsqrt(jnp.mean(xf * xf, axis=-1, keepdims=True) + EPS)
    return (y * g.astype(jnp.float32)).astype(x.dtype)


def t5_bucket(dist):
    max_exact = REL_BUCKETS // 2
    is_small = dist < max_exact
    d_f = jnp.maximum(dist, 1).astype(jnp.float32)
    large = max_exact + (jnp.log(d_f / max_exact) / math.log(REL_MAX_DIST / max_exact)
                         * (REL_BUCKETS - max_exact)).astype(jnp.int32)
    large = jnp.minimum(large, REL_BUCKETS - 1)
    return jnp.where(is_small, dist, large)


def stick_breaking_attention(q, k, v):
    B, S, H, dh = q.shape
    outs = []
    for start in range(0, S, BLOCK_Q):
        end = start + BLOCK_Q
        z = jnp.einsum('bqhd,bkhd->bhqk', q[:, start:end], k[:, :end]).astype(jnp.float32) * dh ** -0.5
        t_pos = start + jnp.arange(BLOCK_Q)[:, None]
        s_pos = jnp.arange(end)[None, :]
        before = s_pos < t_pos
        log_1m = jnp.where(before, jax.nn.log_sigmoid(-z), 0.0)
        csum = jnp.cumsum(log_1m, axis=-1)
        log_a = jax.nn.log_sigmoid(z) + (csum[..., -1:] - csum)
        a = jnp.where(before, jnp.exp(log_a), 0.0)
        outs.append(jnp.einsum('bhqk,bkhd->bqhd', a.astype(v.dtype), v[:, :end]))
    return jnp.concatenate(outs, axis=1)


def dsa_sparse_attention(q, k, v, q_idx, k_idx, w_idx, rel_bias):
    B, S, H, dh = q.shape
    ktop = min(INDEX_TOPK_MAX, S // 4)
    gather = jax.vmap(lambda arr, idx: arr[idx])
    outs = []
    for start in range(0, S, BLOCK_Q):
        end = start + BLOCK_Q
        kl = min(S, max(end, ktop))
        t_pos = start + jnp.arange(BLOCK_Q)[:, None]
        s_pos = jnp.arange(kl)[None, :]
        dots = jnp.einsum('bqhd,bkd->bqhk', q_idx[:, start:end], k_idx[:, :kl]).astype(jnp.float32)
        score = jnp.einsum('bqh,bqhk->bqk', w_idx[:, start:end].astype(jnp.float32), jax.nn.relu(dots))
        score = jnp.where(s_pos <= t_pos, score, -jnp.inf)
        _, sel = lax.top_k(score, ktop)
        k_sel = gather(k[:, :kl], sel)
        v_sel = gather(v[:, :kl], sel)
        dist = t_pos[None] - sel
        valid = dist >= 0
        bias = rel_bias[t5_bucket(jnp.maximum(dist, 0))]
        logits = (jnp.einsum('bqhd,bqkd->bhqk', q[:, start:end], k_sel).astype(jnp.float32) * dh ** -0.5
                  + jnp.transpose(bias, (0, 3, 1, 2)).astype(jnp.float32))
        logits = jnp.where(valid[:, None], logits, -jnp.inf)
        p = jax.nn.softmax(logits, axis=-1)
        outs.append(jnp.einsum('bhqk,bqkd->bqhd', p.astype(v.dtype), v_sel))
    return jnp.concatenate(outs, axis=1)


def grouped_moe(h, router_w, router_b, w_gate, w_up, w_down):
    B, S, D = h.shape
    t = h.reshape(-1, D)
    logits = (t @ router_w).astype(jnp.float32) + router_b.astype(jnp.float32)
    probs = jax.nn.softmax(logits, axis=-1)
    pg = probs.reshape(-1, N_GROUPS, EXPERTS_PER_GROUP)
    group_score = jnp.sum(lax.top_k(pg, TOP_K)[0], axis=-1)
    g = jnp.argmax(group_score, axis=-1)
    in_group = jnp.take_along_axis(pg, g[:, None, None], axis=1)[:, 0]
    vals, local = lax.top_k(in_group, TOP_K)
    experts = g[:, None] * EXPERTS_PER_GROUP + local
    weights = vals / jnp.sum(vals, axis=-1, keepdims=True)
    combine = jnp.sum(jax.nn.one_hot(experts, N_EXPERTS, dtype=jnp.float32) * weights[..., None], axis=1)
    hidden = jax.nn.silu(jnp.einsum('td,edf->tef', t, w_gate)) * jnp.einsum('td,edf->tef', t, w_up)
    hidden = hidden * combine[:, :, None].astype(hidden.dtype)
    y = jnp.einsum('tef,efd->td', hidden, w_down)
    return y.reshape(B, S, D)


def setup_inputs(seed: int = 0) -> dict:
    key = jax.random.key(seed)
    ks = jax.random.split(key, 20)
    f32 = jnp.float32
    nrm = lambda k, shape, s: jax.random.normal(k, shape, f32) * s
    return {
        "x": nrm(ks[0], (BATCH, SEQ, D_MODEL), 1.0),
        "c": nrm(ks[1], (BATCH, D_MODEL), 1.0),
        "w_ada": nrm(ks[2], (DEPTH, D_MODEL, N_MOD * D_MODEL), 0.5 * D_MODEL ** -0.5),
        "b_ada": nrm(ks[3], (DEPTH, N_MOD * D_MODEL), 0.02),
        "norm1_g": 1.0 + nrm(ks[4], (DEPTH, D_MODEL), 0.02),
        "w_in": nrm(ks[5], (DEPTH, D_MODEL, N_IN), D_MODEL ** -0.5),
        "q_norm_g": 1.0 + nrm(ks[6], (DEPTH, HEAD_DIM), 0.02),
        "k_norm_g": 1.0 + nrm(ks[7], (DEPTH, KV_DIM), 0.02),
        "rel_bias": nrm(ks[8], (REL_BUCKETS, N_HEADS_DSA), 0.5),
        "w_o_a": nrm(ks[9], (DEPTH, SB_WIDTH, D_MODEL), SB_WIDTH ** -0.5),
        "w_o_b": nrm(ks[10], (DEPTH, DSA_WIDTH, D_MODEL), DSA_WIDTH ** -0.5),
        "w_out": nrm(ks[11], (DEPTH, D_MODEL, D_MODEL), D_MODEL ** -0.5),
        "norm2_g": 1.0 + nrm(ks[12], (DEPTH, D_MODEL), 0.02),
        "router_w": nrm(ks[13], (D_MODEL, N_EXPERTS), D_MODEL ** -0.5),
        "router_b": nrm(ks[14], (N_EXPERTS,), 0.01),
        "w_gate": nrm(ks[15], (DEPTH, N_EXPERTS, D_MODEL, D_EXPERT), D_MODEL ** -0.5),
        "w_up": nrm(ks[16], (DEPTH, N_EXPERTS, D_MODEL, D_EXPERT), D_MODEL ** -0.5),
        "w_down": nrm(ks[17], (DEPTH, N_EXPERTS, D_EXPERT, D_MODEL), D_EXPERT ** -0.5),
    }


def reference(x, c, w_ada, b_ada, norm1_g, w_in, q_norm_g, k_norm_g, rel_bias,
              w_o_a, w_o_b, w_out, norm2_g, router_w, router_b, w_gate, w_up, w_down):
    B, S, D = x.shape
    for l in range(DEPTH):
        mod = jax.nn.silu(c) @ w_ada[l] + b_ada[l]
        shift1, scale1, gate1, shift2, scale2, gate2 = [m[:, None, :] for m in jnp.split(mod, N_MOD, axis=-1)]

        h = rmsnorm(x, norm1_g[l]) * (1.0 + scale1) + shift1
        proj = h @ w_in[l]
        (q_a, k_a, v_a, q_b, k_b, v_b, q_i, k_i, w_i, g_a, g_b) = jnp.split(proj, IN_OFFSETS, axis=-1)

        q_a = q_a.reshape(B, S, N_HEADS_SB, HEAD_DIM)
        k_a = k_a.reshape(B, S, N_HEADS_SB, HEAD_DIM)
        v_a = v_a.reshape(B, S, N_HEADS_SB, HEAD_DIM)
        o_a = stick_breaking_attention(q_a, k_a, v_a).reshape(B, S, SB_WIDTH)

        q_b = rmsnorm(q_b.reshape(B, S, N_HEADS_DSA, HEAD_DIM), q_norm_g[l])
        k_b = rmsnorm(k_b, k_norm_g[l])
        q_i = q_i.reshape(B, S, IDX_HEADS, IDX_DIM)
        o_b = dsa_sparse_attention(q_b, k_b, v_b, q_i, k_i, w_i, rel_bias).reshape(B, S, DSA_WIDTH)

        merged = jax.nn.sigmoid(g_a) * (o_a @ w_o_a[l]) + jax.nn.sigmoid(g_b) * (o_b @ w_o_b[l])
        x = x + gate1 * (merged @ w_out[l])

        h2 = rmsnorm(x, norm2_g[l]) * (1.0 + scale2) + shift2
        x = x + gate2 * grouped_moe(h2, router_w, router_b, w_gate[l], w_up[l], w_down[l])
    return x
```

```python
import functools
import math

import numpy as np
import jax
import jax.numpy as jnp
from jax import lax
from jax.experimental import pallas as pl
from jax.experimental.pallas import tpu as pltpu

F32 = jnp.float32
BF16 = jnp.bfloat16
I32 = jnp.int32

HEAD_DIM = 64
N_HEADS = 8
N_EXPERTS = 16
N_GROUPS = 4
EXPERTS_PER_GROUP = 4
N_MOD = 6
EPS = 1e-6
INDEX_TOPK_MAX = 256
REL_BUCKETS = 32
REL_MAX_DIST = 128
LANES = 128
ATT_BLOCK = 256
NEG_BIG = -1e30
INT_MIN = -2 ** 31

VMEM_LIMIT = 56 * 1024 * 1024


def _nt_dot(a, b):
    return lax.dot_general(a, b, (((1,), (1,)), ((), ())), preferred_element_type=F32)


def _dot(a, b):
    return jnp.dot(a, b, preferred_element_type=F32)


def _mod_kernel(c_ref, w_ref, b_ref, o_ref):
    c = c_ref[...]
    sc = c * jax.nn.sigmoid(c)
    o_ref[...] = jnp.dot(sc, w_ref[...], preferred_element_type=F32,
                         precision=lax.Precision.HIGHEST) + b_ref[...]


def _adaln(c, w_ada, b_ada):
    depth, d, nd = w_ada.shape
    b = c.shape[0]
    out = pl.pallas_call(
        _mod_kernel,
        grid=(depth, nd // d),
        in_specs=[pl.BlockSpec((b, d), lambda l, j: (0, 0)),
                  pl.BlockSpec((None, d, d), lambda l, j: (l, 0, j)),
                  pl.BlockSpec((None, 1, d), lambda l, j: (l, 0, j))],
        out_specs=pl.BlockSpec((None, b, d), lambda l, j: (l, 0, j)),
        out_shape=jax.ShapeDtypeStruct((depth, b, nd), F32),
        compiler_params=pltpu.CompilerParams(vmem_limit_bytes=VMEM_LIMIT),
        name="adaln",
    )(c, w_ada, b_ada.reshape(depth, 1, nd))
    return out.reshape(depth, b, N_MOD, d)


def _rms(x):
    return x * lax.rsqrt(jnp.mean(x * x, axis=-1, keepdims=True) + EPS)


def _in_kernel(x_ref, mod_ref, g1_ref, w_ref, qng_ref, kng_ref,
               qa_ref, ka_ref, va_ref, qb_ref, qi_ref, kb_ref, vb_ref, ki_ref, wi_ref,
               sga_ref, sgb_ref):
    d = x_ref.shape[-1]
    hw = N_HEADS * HEAD_DIM
    x = x_ref[...]
    h = (_rms(x) * g1_ref[...]) * (1.0 + mod_ref[1:2, :]) + mod_ref[0:1, :]
    hb = h.astype(BF16)

    def proj(lo, n):
        return _dot(hb, w_ref[:, lo:lo + n])

    def heads(r, ref):
        for hh in range(N_HEADS):
            ref[hh] = r[:, hh * HEAD_DIM:(hh + 1) * HEAD_DIM].astype(ref.dtype)

    scale = HEAD_DIM ** -0.5
    heads(proj(0, hw) * scale, qa_ref)
    heads(proj(hw, hw), ka_ref)
    heads(proj(2 * hw, hw), va_ref)

    r = proj(3 * hw, hw)
    for hh in range(N_HEADS):
        rh = r[:, hh * HEAD_DIM:(hh + 1) * HEAD_DIM]
        qb_ref[hh] = ((_rms(rh) * qng_ref[...]) * scale).astype(BF16)

    r = proj(4 * hw, 2 * HEAD_DIM)
    kb_ref[...] = (_rms(r[:, :HEAD_DIM]) * kng_ref[...]).astype(BF16)
    vb_ref[...] = r[:, HEAD_DIM:].astype(BF16)

    heads(proj(4 * hw + LANES, hw), qi_ref)

    r = proj(5 * hw + LANES, LANES)
    ki_ref[...] = r[:, :HEAD_DIM].astype(BF16)
    wi_ref[...] = r[:, HEAD_DIM:HEAD_DIM + N_HEADS]

    off = 5 * hw + 2 * LANES
    sga_ref[...] = jax.nn.sigmoid(proj(off, d)).astype(BF16)
    sgb_ref[...] = jax.nn.sigmoid(proj(off + d, d)).astype(BF16)


def _pack_w_in(w_in_l):
    hw = N_HEADS * HEAD_DIM
    a = 4 * hw + 2 * HEAD_DIM + hw
    small = w_in_l[:, a:a + HEAD_DIM + N_HEADS]
    small = jnp.pad(small, ((0, 0), (0, LANES - small.shape[1])))
    gates = w_in_l[:, a + HEAD_DIM + N_HEADS:]
    return jnp.concatenate([w_in_l[:, :a], small, gates], axis=1).astype(BF16)


def _in_proj(x, mod_l, g1, w_packed, qng, kng, tm):
    b, s, d = x.shape
    hm = jax.ShapeDtypeStruct((b, N_HEADS, s, HEAD_DIM), BF16)
    tok64 = jax.ShapeDtypeStruct((b, s, HEAD_DIM), BF16)
    hm_spec = pl.BlockSpec((None, N_HEADS, tm, HEAD_DIM), lambda bb, i: (bb, 0, i, 0))
    tok = lambda n: pl.BlockSpec((None, tm, n), lambda bb, i: (bb, i, 0))
    nw = w_packed.shape[1]
    return pl.pallas_call(
        _in_kernel,
        grid=(b, s // tm),
        in_specs=[tok(d),
                  pl.BlockSpec((None, N_MOD, d), lambda bb, i: (bb, 0, 0)),
                  pl.BlockSpec((1, d), lambda bb, i: (0, 0)),
                  pl.BlockSpec((d, nw), lambda bb, i: (0, 0)),
                  pl.BlockSpec((1, HEAD_DIM), lambda bb, i: (0, 0)),
                  pl.BlockSpec((1, HEAD_DIM), lambda bb, i: (0, 0))],
        out_specs=[hm_spec] * 5 + [tok(HEAD_DIM)] * 3 + [tok(N_HEADS), tok(d), tok(d)],
        out_shape=[hm] * 5 + [tok64] * 3
                  + [jax.ShapeDtypeStruct((b, s, N_HEADS), F32),
                     jax.ShapeDtypeStruct((b, s, d), BF16),
                     jax.ShapeDtypeStruct((b, s, d), BF16)],
        compiler_params=pltpu.CompilerParams(
            dimension_semantics=("parallel", "parallel"), vmem_limit_bytes=VMEM_LIMIT),
        name="in_proj",
    )(x, mod_l, g1, w_packed, qng, kng)


def _softplus(z):
    return jnp.maximum(z, 0.0) + jnp.log(1.0 + jnp.exp(-jnp.abs(z)))


def _sb_kernel(q_ref, k_ref, v_ref, o_ref):
    tq = q_ref.shape[1]
    tk = tq
    i = pl.program_id(1)
    row = lax.broadcasted_iota(I32, (tq, tk), 0)
    col = lax.broadcasted_iota(I32, (tq, tk), 1)
    before = col < row
    suffix = (row > col).astype(BF16)

    def block(q, ks, vs, csum, diag):
        z = _nt_dot(q, ks)
        sp = _softplus(z)
        l1m = -sp
        if diag:
            l1m = jnp.where(before, l1m, 0.0)
        hi = l1m.astype(BF16)
        lo = (l1m - hi.astype(F32)).astype(BF16)
        suf = _dot(hi, suffix) + _dot(lo, suffix)
        a = jnp.exp((z - sp) + suf + csum)
        if diag:
            a = jnp.where(before, a, 0.0)
        contrib = _dot(a.astype(BF16), vs)
        return contrib, csum + jnp.sum(l1m, axis=-1, keepdims=True)

    for hh in range(N_HEADS):
        q = q_ref[hh]
        start = pl.multiple_of(i * tk, tk)
        acc, csum = block(q, k_ref[hh, pl.ds(start, tk), :], v_ref[hh, pl.ds(start, tk), :],
                          jnp.zeros((tq, 1), F32), True)

        def body(jj, carry, q=q, hh=hh):
            acc, csum = carry
            st = pl.multiple_of((i - 1 - jj) * tk, tk)
            c, csum = block(q, k_ref[hh, pl.ds(st, tk), :], v_ref[hh, pl.ds(st, tk), :], csum, False)
            return acc + c, csum

        acc, _ = lax.fori_loop(0, i, body, (acc, csum))
        o_ref[:, hh * HEAD_DIM:(hh + 1) * HEAD_DIM] = acc.astype(o_ref.dtype)


def _sb_attention(qa, ka, va):
    b, nh, s, dh = qa.shape
    tq = ATT_BLOCK
    return pl.pallas_call(
        _sb_kernel,
        grid=(b, s // tq),
        in_specs=[pl.BlockSpec((None, nh, tq, dh), lambda bb, i: (bb, 0, i, 0)),
                  pl.BlockSpec((None, nh, s, dh), lambda bb, i: (bb, 0, 0, 0)),
                  pl.BlockSpec((None, nh, s, dh), lambda bb, i: (bb, 0, 0, 0))],
        out_specs=pl.BlockSpec((None, tq, nh * dh), lambda bb, i: (bb, i, 0)),
        out_shape=jax.ShapeDtypeStruct((b, s, nh * dh), BF16),
        compiler_params=pltpu.CompilerParams(
            dimension_semantics=("parallel", "parallel"), vmem_limit_bytes=VMEM_LIMIT),
        name="stick_breaking",
    )(qa, ka, va)


def _t5_bucket_np(dist):
    max_exact = REL_BUCKETS // 2
    d_f = np.maximum(dist, 1).astype(np.float32)
    large = max_exact + (np.log(d_f / np.float32(max_exact))
                         / np.float32(math.log(REL_MAX_DIST / max_exact))
                         * np.float32(REL_BUCKETS - max_exact)).astype(np.int32)
    large = np.minimum(large, REL_BUCKETS - 1)
    return np.where(dist < max_exact, dist, large).astype(np.int32)


def _bias_tiles(rel_bias, t):
    r = np.arange(t)[:, None]
    c = np.arange(t)[None, :]
    bk = np.stack([_t5_bucket_np(np.maximum(r - c, 0)), _t5_bucket_np(t + r - c)])
    far = int(_t5_bucket_np(np.array([REL_MAX_DIST]))[0])
    tiles = rel_bias[bk] - rel_bias[far]
    return jnp.transpose(tiles, (0, 3, 1, 2)).astype(F32)


def _sort_key(x):
    bits = pltpu.bitcast(x, I32)
    return bits ^ ((bits >> 31) & 0x7FFFFFFF)


def _dsa_kernel(ktop, qi_ref, ki_ref, wi_ref, qb_ref, kb_ref, vb_ref, bias_ref, o_ref,
                key_scr, m_scr, l_scr, acc_scr):
    tq = qi_ref.shape[1]
    tk = tq
    i = pl.program_id(1)
    row = lax.broadcasted_iota(I32, (tq, tk), 0)
    col = lax.broadcasted_iota(I32, (tq, tk), 1)
    causal = col <= row

    qi = qi_ref[...].reshape(N_HEADS * tq, HEAD_DIM)
    wi = wi_ref[...]

    def score_block(j, diag):
        st = pl.multiple_of(j * tk, tk)
        dots = _nt_dot(qi, ki_ref[pl.ds(st, tk), :])
        score = jnp.zeros((tq, tk), F32)
        for hh in range(N_HEADS):
            score = score + wi[:, hh:hh + 1] * jnp.maximum(dots[hh * tq:(hh + 1) * tq], 0.0)
        keys = _sort_key(score)
        if diag:
            keys = jnp.where(causal, keys, INT_MIN)
        key_scr[j] = keys

    def score_body(j, c):
        score_block(j, False)
        return c

    lax.fori_loop(0, i, score_body, 0)
    score_block(i, True)

    def count(pred):
        def body(j, part):
            k = key_scr[j]
            hit = pred(k).astype(I32)
            for g in range(tk // LANES):
                part = part + hit[:, g * LANES:(g + 1) * LANES]
            return part
        part = lax.fori_loop(0, i + 1, body, jnp.zeros((tq, LANES), I32))
        return jnp.sum(part, axis=-1, keepdims=True)

    def bit_body(bi, t_u):
        bit = jnp.left_shift(jnp.int32(1), 31 - bi)
        cand_u = t_u | bit
        cand = cand_u ^ INT_MIN
        n = count(lambda k: k >= cand)
        return jnp.where(n >= ktop, cand_u, t_u)

    t_u = lax.fori_loop(0, 32, bit_body, jnp.zeros((tq, 1), I32))
    thr = t_u ^ INT_MIN
    n_ge = count(lambda k: k >= thr)

    @pl.when(jnp.max(n_ge) > ktop)
    def _():
        n_gt = count(lambda k: k > thr)
        need = ktop - n_gt
        strict_lower = (row < col).astype(BF16)

        def body(j, seen):
            k = key_scr[j]
            eq = k == thr
            eqf = eq.astype(BF16)
            rank = seen + _dot(eqf, strict_lower)
            drop = eq & (rank >= need.astype(F32)) & (n_ge > ktop)
            key_scr[j] = jnp.where(drop, thr - 1, k)
            return seen + jnp.sum(eqf.astype(F32), axis=-1, keepdims=True)

        lax.fori_loop(0, i + 1, body, jnp.zeros((tq, 1), F32))

    m_scr[...] = jnp.full(m_scr.shape, NEG_BIG, F32)
    l_scr[...] = jnp.zeros(l_scr.shape, F32)
    acc_scr[...] = jnp.zeros(acc_scr.shape, F32)

    def att_block(j, bias_idx, diag):
        st = pl.multiple_of(j * tk, tk)
        sel = key_scr[j] >= thr
        if diag:
            sel = sel & causal
        kb = kb_ref[pl.ds(st, tk), :]
        vb = vb_ref[pl.ds(st, tk), :]
        for hh in range(N_HEADS):
            s = _nt_dot(qb_ref[hh], kb)
            if bias_idx is not None:
                s = s + bias_ref[bias_idx, hh]
            s = jnp.where(sel, s, NEG_BIG)
            m_old = m_scr[hh]
            m_new = jnp.maximum(m_old, jnp.max(s, axis=-1, keepdims=True))
            p = jnp.where(sel, jnp.exp(s - m_new), 0.0)
            alpha = jnp.exp(m_old - m_new)
            l_scr[hh] = alpha * l_scr[hh] + jnp.sum(p, axis=-1, keepdims=True)
            acc_scr[hh] = alpha * acc_scr[hh] + _dot(p.astype(BF16), vb)
            m_scr[hh] = m_new

    def far_body(j, c):
        att_block(j, None, False)
        return c

    lax.fori_loop(0, i - 1, far_body, 0)

    @pl.when(i >= 1)
    def _():
        att_block(i - 1, 1, False)

    att_block(i, 0, True)

    for hh in range(N_HEADS):
        o_ref[:, hh * HEAD_DIM:(hh + 1) * HEAD_DIM] = (acc_scr[hh] / l_scr[hh]).astype(o_ref.dtype)


def _dsa_attention(qi, ki, wi, qb, kb, vb, bias):
    b, nh, s, dh = qi.shape
    tq = ATT_BLOCK
    ktop = min(INDEX_TOPK_MAX, s // 4)
    hm_q = pl.BlockSpec((None, nh, tq, dh), lambda bb, i: (bb, 0, i, 0))
    full64 = pl.BlockSpec((None, s, dh), lambda bb, i: (bb, 0, 0))
    return pl.pallas_call(
        functools.partial(_dsa_kernel, ktop),
        grid=(b, s // tq),
        in_specs=[hm_q, full64,
                  pl.BlockSpec((None, tq, nh), lambda bb, i: (bb, i, 0)),
                  hm_q, full64, full64,
                  pl.BlockSpec((2, nh, tq, tq), lambda bb, i: (0, 0, 0, 0))],
        out_specs=pl.BlockSpec((None, tq, nh * dh), lambda bb, i: (bb, i, 0)),
        out_shape=jax.ShapeDtypeStruct((b, s, nh * dh), BF16),
        scratch_shapes=[pltpu.VMEM((s // tq, tq, tq), I32),
                        pltpu.VMEM((nh, tq, 1), F32),
                        pltpu.VMEM((nh, tq, 1), F32),
                        pltpu.VMEM((nh, tq, dh), F32)],
        compiler_params=pltpu.CompilerParams(
            dimension_semantics=("parallel", "parallel"), vmem_limit_bytes=VMEM_LIMIT),
        name="sparse_attention",
    )(qi, ki, wi, qb, kb, vb, bias)


def _first_max4(a):
    m1 = jnp.maximum(jnp.maximum(a[0], a[1]), jnp.maximum(a[2], a[3]))
    i1 = jnp.where(a[0] == m1, 0, jnp.where(a[1] == m1, 1, jnp.where(a[2] == m1, 2, 3)))
    rest = [jnp.where(i1 == j, -1.0, a[j]) for j in range(4)]
    m2 = jnp.maximum(jnp.maximum(rest[0], rest[1]), jnp.maximum(rest[2], rest[3]))
    i2 = jnp.where(rest[0] == m2, 0, jnp.where(rest[1] == m2, 1, jnp.where(rest[2] == m2, 2, 3)))
    return m1, i1, m2, i2


def _route(lt):
    lg = [lt[e:e + 1, :] for e in range(N_EXPERTS)]
    mx = functools.reduce(jnp.maximum, lg)
    ex = [jnp.exp(v - mx) for v in lg]
    tot = functools.reduce(lambda u, v: u + v, ex)
    p = [v / tot for v in ex]
    gs = []
    for g in range(N_GROUPS):
        m1, _, m2, _ = _first_max4(p[4 * g:4 * g + 4])
        gs.append(m1 + m2)
    _, gbest, _, _ = _first_max4(gs)
    chosen = [jnp.where(gbest == 0, p[j], jnp.where(gbest == 1, p[4 + j],
              jnp.where(gbest == 2, p[8 + j], p[12 + j]))) for j in range(4)]
    m1, i1, m2, i2 = _first_max4(chosen)
    den = m1 + m2
    w1 = m1 / den
    w2 = m2 / den
    rows = []
    for e in range(N_EXPERTS):
        g, j = divmod(e, EXPERTS_PER_GROUP)
        rows.append(jnp.where(gbest == g, jnp.where(i1 == j, w1, 0.0) + jnp.where(i2 == j, w2, 0.0), 0.0))
    return rows


def _merge_kernel(oa_ref, ob_ref, sga_ref, sgb_ref, x_ref, mod_ref, woa_ref, wob_ref, wout_ref,
                  g2_ref, rwh_ref, rwl_ref, rb_ref, x1_ref, h2_ref, comb_ref):
    tm = x_ref.shape[0]
    merged = (sga_ref[...].astype(F32) * _dot(oa_ref[...], woa_ref[...])
              + sgb_ref[...].astype(F32) * _dot(ob_ref[...], wob_ref[...]))
    x1 = x_ref[...] + mod_ref[2:3, :] * _dot(merged.astype(BF16), wout_ref[...])
    x1_ref[...] = x1
    h2 = (_rms(x1) * g2_ref[...]) * (1.0 + mod_ref[4:5, :]) + mod_ref[3:4, :]
    hi = h2.astype(BF16)
    h2_ref[...] = hi
    lo = (h2 - hi.astype(F32)).astype(BF16)
    logits = (_dot(hi, rwh_ref[...]) + _dot(lo, rwh_ref[...]) + _dot(hi, rwl_ref[...])) + rb_ref[...]
    rows = _route(logits.T)
    sub = lax.broadcasted_iota(I32, (LANES, tm), 0)
    comb_t = jnp.zeros((LANES, tm), F32)
    for e in range(N_EXPERTS):
        comb_t = jnp.where(sub == e, jnp.broadcast_to(rows[e], (LANES, tm)), comb_t)
    comb_ref[...] = comb_t.T


def _merge(oa, ob, sga, sgb, x, mod_l, woa, wob, wout, g2, rwh, rwl, rb, tm):
    b, s, d = x.shape
    hw = oa.shape[-1]
    tok = lambda n: pl.BlockSpec((None, tm, n), lambda bb, i: (bb, i, 0))
    const = lambda r, c: pl.BlockSpec((r, c), lambda bb, i: (0, 0))
    return pl.pallas_call(
        _merge_kernel,
        grid=(b, s // tm),
        in_specs=[tok(hw), tok(hw), tok(d), tok(d), tok(d),
                  pl.BlockSpec((None, N_MOD, d), lambda bb, i: (bb, 0, 0)),
                  const(hw, d), const(hw, d), const(d, d), const(1, d),
                  const(d, LANES), const(d, LANES), const(1, LANES)],
        out_specs=[tok(d), tok(d), tok(LANES)],
        out_shape=[jax.ShapeDtypeStruct((b, s, d), F32),
                   jax.ShapeDtypeStruct((b, s, d), BF16),
                   jax.ShapeDtypeStruct((b, s, LANES), F32)],
        compiler_params=pltpu.CompilerParams(
            dimension_semantics=("parallel", "parallel"), vmem_limit_bytes=VMEM_LIMIT),
        name="merge_route",
    )(oa, ob, sga, sgb, x, mod_l, woa, wob, wout, g2, rwh, rwl, rb)


def _moe_kernel(h2_ref, comb_ref, wg_ref, wu_ref, wd_ref, x1_ref, mod_ref, o_ref, acc_ref):
    e = pl.program_id(2)

    @pl.when(e == 0)
    def _():
        acc_ref[...] = jnp.zeros(acc_ref.shape, F32)

    h = h2_ref[...]
    lane = lax.broadcasted_iota(I32, comb_ref.shape, 1)
    cw = jnp.sum(jnp.where(lane == e, comb_ref[...], 0.0), axis=-1, keepdims=True)
    gt = _dot(h, wg_ref[...])
    hidden = (gt * jax.nn.sigmoid(gt)) * _dot(h, wu_ref[...]) * cw
    acc_ref[...] += _dot(hidden.astype(BF16), wd_ref[...])

    @pl.when(e == pl.num_programs(2) - 1)
    def _():
        o_ref[...] = x1_ref[...] + mod_ref[5:6, :] * acc_ref[...]


def _moe(h2, comb, wg, wu, wd, x1, mod_l, tm):
    b, s, d = x1.shape
    ne, _, f = wg.shape
    tok = lambda n: pl.BlockSpec((None, tm, n), lambda bb, i, e: (bb, i, 0))
    return pl.pallas_call(
        _moe_kernel,
        grid=(b, s // tm, ne),
        in_specs=[tok(d), tok(LANES),
                  pl.BlockSpec((None, d, f), lambda bb, i, e: (e, 0, 0)),
                  pl.BlockSpec((None, d, f), lambda bb, i, e: (e, 0, 0)),
                  pl.BlockSpec((None, f, d), lambda bb, i, e: (e, 0, 0)),
                  tok(d),
                  pl.BlockSpec((None, N_MOD, d), lambda bb, i, e: (bb, 0, 0))],
        out_specs=tok(d),
        out_shape=jax.ShapeDtypeStruct((b, s, d), F32),
        scratch_shapes=[pltpu.VMEM((tm, d), F32)],
        compiler_params=pltpu.CompilerParams(
            dimension_semantics=("parallel", "parallel", "arbitrary"), vmem_limit_bytes=VMEM_LIMIT),
        name="experts",
    )(h2, comb, wg, wu, wd, x1, mod_l)


def kernel(x, c, w_ada, b_ada, norm1_g, w_in, q_norm_g, k_norm_g, rel_bias, w_o_a, w_o_b, w_out,
           norm2_g, router_w, router_b, w_gate, w_up, w_down):
    b, s, d = x.shape
    depth = w_ada.shape[0]
    assert s % ATT_BLOCK == 0 and d % LANES == 0
    tm = min(512, s)

    mod = _adaln(c, w_ada, b_ada)
    bias = _bias_tiles(rel_bias, ATT_BLOCK)
    rw = jnp.pad(router_w, ((0, 0), (0, LANES - N_EXPERTS)))
    rwh = rw.astype(BF16)
    rwl = (rw - rwh.astype(F32)).astype(BF16)
    rb = jnp.pad(router_b, (0, LANES - N_EXPERTS)).reshape(1, LANES)

    for l in range(depth):
        qa, ka, va, qb, qi, kb, vb, ki, wi, sga, sgb = _in_proj(
            x, mod[l], norm1_g[l].reshape(1, d), _pack_w_in(w_in[l]),
            q_norm_g[l].reshape(1, HEAD_DIM), k_norm_g[l].reshape(1, HEAD_DIM), tm)
        oa = _sb_attention(qa, ka, va)
        ob = _dsa_attention(qi, ki, wi, qb, kb, vb, bias)
        x1, h2, comb = _merge(oa, ob, sga, sgb, x, mod[l], w_o_a[l].astype(BF16),
                              w_o_b[l].astype(BF16), w_out[l].astype(BF16),
                              norm2_g[l].reshape(1, d), rwh, rwl, rb, tm)
        x = _moe(h2, comb, w_gate[l].astype(BF16), w_up[l].astype(BF16), w_down[l].astype(BF16),
                 x1, mod[l], tm)
    return x
```

```python
import functools
import math

import numpy as np
import jax
import jax.numpy as jnp
from jax import lax
from jax.experimental import pallas as pl
from jax.experimental.pallas import tpu as pltpu

F32 = jnp.float32
BF16 = jnp.bfloat16
I32 = jnp.int32

HEAD_DIM = 64
N_HEADS = 8
N_EXPERTS = 16
N_GROUPS = 4
EXPERTS_PER_GROUP = 4
N_MOD = 6
EPS = 1e-6
INDEX_TOPK_MAX = 256
REL_BUCKETS = 32
REL_MAX_DIST = 128
LANES = 128
ATT_BLOCK = 256
SB_QUERY_BLOCK = 128
SB_KEY_BLOCK = 256
LOG2E = 1.4426950408889634
NEG_BIG = -1e30
INT_MIN = -2 ** 31

VMEM_LIMIT = 56 * 1024 * 1024


def _nt_dot(a, b):
    return lax.dot_general(a, b, (((1,), (1,)), ((), ())), preferred_element_type=F32)


def _dot(a, b):
    return jnp.dot(a, b, preferred_element_type=F32)


def _mod_kernel(c_ref, w_ref, b_ref, o_ref):
    c = c_ref[...]
    sc = c * jax.nn.sigmoid(c)
    o_ref[...] = jnp.dot(sc, w_ref[...], preferred_element_type=F32,
                         precision=lax.Precision.HIGHEST) + b_ref[...]


def _adaln(c, w_ada, b_ada):
    depth, d, nd = w_ada.shape
    b = c.shape[0]
    out = pl.pallas_call(
        _mod_kernel,
        grid=(depth, nd // d),
        in_specs=[pl.BlockSpec((b, d), lambda l, j: (0, 0)),
                  pl.BlockSpec((None, d, d), lambda l, j: (l, 0, j)),
                  pl.BlockSpec((None, 1, d), lambda l, j: (l, 0, j))],
        out_specs=pl.BlockSpec((None, b, d), lambda l, j: (l, 0, j)),
        out_shape=jax.ShapeDtypeStruct((depth, b, nd), F32),
        compiler_params=pltpu.CompilerParams(vmem_limit_bytes=VMEM_LIMIT),
        name="adaln",
    )(c, w_ada, b_ada.reshape(depth, 1, nd))
    return out.reshape(depth, b, N_MOD, d)


def _rms(x):
    return x * lax.rsqrt(jnp.mean(x * x, axis=-1, keepdims=True) + EPS)


def _in_kernel(x_ref, mod_ref, g1_ref, w_ref, qng_ref, kng_ref,
               qa_ref, ka_ref, va_ref, qb_ref, qi_ref, kb_ref, vb_ref, ki_ref, wi_ref,
               sga_ref, sgb_ref):
    d = x_ref.shape[-1]
    hw = N_HEADS * HEAD_DIM
    x = x_ref[...]
    h = (_rms(x) * g1_ref[...]) * (1.0 + mod_ref[1:2, :]) + mod_ref[0:1, :]
    hb = h.astype(BF16)

    def proj(lo, n):
        return _dot(hb, w_ref[:, lo:lo + n])

    def heads(r, ref):
        for hh in range(N_HEADS):
            ref[hh] = r[:, hh * HEAD_DIM:(hh + 1) * HEAD_DIM].astype(ref.dtype)

    scale = HEAD_DIM ** -0.5
    heads(proj(0, hw) * (scale * LOG2E), qa_ref)
    heads(proj(hw, hw), ka_ref)
    heads(proj(2 * hw, hw), va_ref)

    r = proj(3 * hw, hw)
    for hh in range(N_HEADS):
        rh = r[:, hh * HEAD_DIM:(hh + 1) * HEAD_DIM]
        qb_ref[hh] = ((_rms(rh) * qng_ref[...]) * scale).astype(BF16)

    r = proj(4 * hw, 2 * HEAD_DIM)
    kb_ref[...] = (_rms(r[:, :HEAD_DIM]) * kng_ref[...]).astype(BF16)
    vb_ref[...] = r[:, HEAD_DIM:].astype(BF16)

    heads(proj(4 * hw + LANES, hw), qi_ref)

    r = proj(5 * hw + LANES, LANES)
    ki_ref[...] = r[:, :HEAD_DIM].astype(BF16)
    wi_ref[...] = r[:, HEAD_DIM:HEAD_DIM + N_HEADS]

    off = 5 * hw + 2 * LANES
    sga_ref[...] = jax.nn.sigmoid(proj(off, d)).astype(BF16)
    sgb_ref[...] = jax.nn.sigmoid(proj(off + d, d)).astype(BF16)


def _pack_w_in(w_in_l):
    hw = N_HEADS * HEAD_DIM
    a = 4 * hw + 2 * HEAD_DIM + hw
    small = w_in_l[:, a:a + HEAD_DIM + N_HEADS]
    small = jnp.pad(small, ((0, 0), (0, LANES - small.shape[1])))
    gates = w_in_l[:, a + HEAD_DIM + N_HEADS:]
    return jnp.concatenate([w_in_l[:, :a], small, gates], axis=1).astype(BF16)


def _in_proj(x, mod_l, g1, w_packed, qng, kng, tm):
    b, s, d = x.shape
    hm = jax.ShapeDtypeStruct((b, N_HEADS, s, HEAD_DIM), BF16)
    tok64 = jax.ShapeDtypeStruct((b, s, HEAD_DIM), BF16)
    hm_spec = pl.BlockSpec((None, N_HEADS, tm, HEAD_DIM), lambda bb, i: (bb, 0, i, 0))
    tok = lambda n: pl.BlockSpec((None, tm, n), lambda bb, i: (bb, i, 0))
    nw = w_packed.shape[1]
    return pl.pallas_call(
        _in_kernel,
        grid=(b, s // tm),
        in_specs=[tok(d),
                  pl.BlockSpec((None, N_MOD, d), lambda bb, i: (bb, 0, 0)),
                  pl.BlockSpec((1, d), lambda bb, i: (0, 0)),
                  pl.BlockSpec((d, nw), lambda bb, i: (0, 0)),
                  pl.BlockSpec((1, HEAD_DIM), lambda bb, i: (0, 0)),
                  pl.BlockSpec((1, HEAD_DIM), lambda bb, i: (0, 0))],
        out_specs=[hm_spec] * 5 + [tok(HEAD_DIM)] * 3 + [tok(N_HEADS), tok(d), tok(d)],
        out_shape=[hm] * 5 + [tok64] * 3
                  + [jax.ShapeDtypeStruct((b, s, N_HEADS), F32),
                     jax.ShapeDtypeStruct((b, s, d), BF16),
                     jax.ShapeDtypeStruct((b, s, d), BF16)],
        compiler_params=pltpu.CompilerParams(
            dimension_semantics=("parallel", "parallel"), vmem_limit_bytes=VMEM_LIMIT),
        name="in_proj",
    )(x, mod_l, g1, w_packed, qng, kng)


def _sb_kernel(q_ref, k_ref, v_ref, o_ref, acc_scr, csum_scr):
    tq = q_ref.shape[1]
    tk = SB_KEY_BLOCK
    i = pl.program_id(1)
    nfull = (i * tq) // tk
    row = lax.broadcasted_iota(I32, (tq, tk), 0)
    col = lax.broadcasted_iota(I32, (tq, tk), 1)
    before = (col - row) < (i * tq - nfull * tk)
    krow = lax.broadcasted_iota(I32, (tk, tk), 0)
    kcol = lax.broadcasted_iota(I32, (tk, tk), 1)
    suffix = (krow > kcol).astype(BF16)

    def block(q, ks, vs, csum, diag):
        z2 = _nt_dot(q, ks)
        sp2 = jnp.maximum(z2, 0.0) + jnp.log2(1.0 + jnp.exp2(-jnp.abs(z2)))
        if diag:
            sp2 = jnp.where(before, sp2, 0.0)
        suf = _dot(sp2.astype(BF16), suffix)
        a = jnp.exp2((z2 - sp2) - suf - csum)
        if diag:
            a = jnp.where(before, a, 0.0)
        contrib = _dot(a.astype(BF16), vs)
        return contrib, csum + jnp.sum(sp2, axis=-1, keepdims=True)

    start = pl.multiple_of(nfull * tk, tk)
    for hh in range(N_HEADS):
        acc, csum = block(q_ref[hh], k_ref[hh, pl.ds(start, tk), :], v_ref[hh, pl.ds(start, tk), :],
                          jnp.zeros((tq, 1), F32), True)
        acc_scr[hh] = acc
        csum_scr[hh] = csum

    def body(jj, carry):
        st = pl.multiple_of((nfull - 1 - jj) * tk, tk)
        for hh in range(N_HEADS):
            c, csum = block(q_ref[hh], k_ref[hh, pl.ds(st, tk), :], v_ref[hh, pl.ds(st, tk), :],
                            csum_scr[hh], False)
            acc_scr[hh] += c
            csum_scr[hh] = csum
        return carry

    lax.fori_loop(0, nfull, body, 0)
    for hh in range(N_HEADS):
        o_ref[:, hh * HEAD_DIM:(hh + 1) * HEAD_DIM] = acc_scr[hh].astype(o_ref.dtype)


def _sb_attention(qa, ka, va):
    b, nh, s, dh = qa.shape
    tq = SB_QUERY_BLOCK
    assert SB_KEY_BLOCK % tq == 0 and s % SB_KEY_BLOCK == 0
    return pl.pallas_call(
        _sb_kernel,
        grid=(b, s // tq),
        in_specs=[pl.BlockSpec((None, nh, tq, dh), lambda bb, i: (bb, 0, i, 0)),
                  pl.BlockSpec((None, nh, s, dh), lambda bb, i: (bb, 0, 0, 0)),
                  pl.BlockSpec((None, nh, s, dh), lambda bb, i: (bb, 0, 0, 0))],
        out_specs=pl.BlockSpec((None, tq, nh * dh), lambda bb, i: (bb, i, 0)),
        out_shape=jax.ShapeDtypeStruct((b, s, nh * dh), BF16),
        scratch_shapes=[pltpu.VMEM((nh, tq, dh), F32), pltpu.VMEM((nh, tq, 1), F32)],
        compiler_params=pltpu.CompilerParams(
            dimension_semantics=("parallel", "parallel"), vmem_limit_bytes=VMEM_LIMIT),
        name="stick_breaking",
    )(qa, ka, va)


def _t5_bucket_np(dist):
    max_exact = REL_BUCKETS // 2
    d_f = np.maximum(dist, 1).astype(np.float32)
    large = max_exact + (np.log(d_f / np.float32(max_exact))
                         / np.float32(math.log(REL_MAX_DIST / max_exact))
                         * np.float32(REL_BUCKETS - max_exact)).astype(np.int32)
    large = np.minimum(large, REL_BUCKETS - 1)
    return np.where(dist < max_exact, dist, large).astype(np.int32)


def _bias_tiles(rel_bias, t):
    r = np.arange(t)[:, None]
    c = np.arange(t)[None, :]
    bk = np.stack([_t5_bucket_np(np.maximum(r - c, 0)), _t5_bucket_np(t + r - c)])
    far = int(_t5_bucket_np(np.array([REL_MAX_DIST]))[0])
    nh = rel_bias.shape[1]

    def body(rb_ref, bk_ref, o_ref):
        hh = pl.program_id(1)
        buckets = bk_ref[...]
        tile = jnp.zeros((t, t), F32)
        for bucket in range(REL_BUCKETS):
            tile = jnp.where(buckets == bucket, rb_ref[hh, bucket] - rb_ref[hh, far], tile)
        o_ref[...] = tile

    return pl.pallas_call(
        body,
        grid=(2, nh),
        in_specs=[pl.BlockSpec(memory_space=pltpu.SMEM),
                  pl.BlockSpec((None, t, t), lambda d, hh: (d, 0, 0))],
        out_specs=pl.BlockSpec((None, None, t, t), lambda d, hh: (d, hh, 0, 0)),
        out_shape=jax.ShapeDtypeStruct((2, nh, t, t), F32),
        name="bias_tiles",
    )(rel_bias.T, jnp.asarray(bk))


def _sort_key(x):
    bits = pltpu.bitcast(x, I32)
    return bits ^ ((bits >> 31) & 0x7FFFFFFF)


def _dsa_kernel(ktop, qi_ref, ki_ref, wi_ref, qb_ref, kb_ref, vb_ref, bias_ref, o_ref,
                key_scr, m_scr, l_scr, acc_scr):
    tq = qi_ref.shape[1]
    tk = tq
    i = pl.program_id(1)
    row = lax.broadcasted_iota(I32, (tq, tk), 0)
    col = lax.broadcasted_iota(I32, (tq, tk), 1)
    causal = col <= row

    qi = qi_ref[...].reshape(N_HEADS * tq, HEAD_DIM)
    wi = wi_ref[...]

    def score_block(j, diag):
        st = pl.multiple_of(j * tk, tk)
        dots = _nt_dot(qi, ki_ref[pl.ds(st, tk), :])
        score = jnp.zeros((tq, tk), F32)
        for hh in range(N_HEADS):
            score = score + wi[:, hh:hh + 1] * jnp.maximum(dots[hh * tq:(hh + 1) * tq], 0.0)
        keys = _sort_key(score)
        if diag:
            keys = jnp.where(causal, keys, INT_MIN)
        key_scr[j] = keys

    def score_body(j, c):
        score_block(j, False)
        return c

    lax.fori_loop(0, i, score_body, 0)
    score_block(i, True)

    def count(pred):
        def body(j, part):
            k = key_scr[j]
            hit = pred(k).astype(I32)
            for g in range(tk // LANES):
                part = part + hit[:, g * LANES:(g + 1) * LANES]
            return part
        part = lax.fori_loop(0, i + 1, body, jnp.zeros((tq, LANES), I32))
        return jnp.sum(part, axis=-1, keepdims=True)

    def bit_body(bi, t_u):
        bit = jnp.left_shift(jnp.int32(1), 31 - bi)
        cand_u = t_u | bit
        cand = cand_u ^ INT_MIN
        n = count(lambda k: k >= cand)
        return jnp.where(n >= ktop, cand_u, t_u)

    t_u = lax.fori_loop(0, 32, bit_body, jnp.zeros((tq, 1), I32))
    thr = t_u ^ INT_MIN
    n_ge = count(lambda k: k >= thr)

    @pl.when(jnp.max(n_ge) > ktop)
    def _():
        n_gt = count(lambda k: k > thr)
        need = ktop - n_gt
        strict_lower = (row < col).astype(BF16)

        def body(j, seen):
            k = key_scr[j]
            eq = k == thr
            eqf = eq.astype(BF16)
            rank = seen + _dot(eqf, strict_lower)
            drop = eq & (rank >= need.astype(F32)) & (n_ge > ktop)
            key_scr[j] = jnp.where(drop, thr - 1, k)
            return seen + jnp.sum(eqf.astype(F32), axis=-1, keepdims=True)

        lax.fori_loop(0, i + 1, body, jnp.zeros((tq, 1), F32))

    m_scr[...] = jnp.full(m_scr.shape, NEG_BIG, F32)
    l_scr[...] = jnp.zeros(l_scr.shape, F32)
    acc_scr[...] = jnp.zeros(acc_scr.shape, F32)

    def att_block(j, bias_idx, diag):
        st = pl.multiple_of(j * tk, tk)
        sel = key_scr[j] >= thr
        if diag:
            sel = sel & causal
        kb = kb_ref[pl.ds(st, tk), :]
        vb = vb_ref[pl.ds(st, tk), :]
        for hh in range(N_HEADS):
            s = _nt_dot(qb_ref[hh], kb)
            if bias_idx is not None:
                s = s + bias_ref[bias_idx, hh]
            s = jnp.where(sel, s, NEG_BIG)
            m_old = m_scr[hh]
            m_new = jnp.maximum(m_old, jnp.max(s, axis=-1, keepdims=True))
            p = jnp.where(sel, jnp.exp(s - m_new), 0.0)
            alpha = jnp.exp(m_old - m_new)
            l_scr[hh] = alpha * l_scr[hh] + jnp.sum(p, axis=-1, keepdims=True)
            acc_scr[hh] = alpha * acc_scr[hh] + _dot(p.astype(BF16), vb)
            m_scr[hh] = m_new

    def far_body(j, c):
        att_block(j, None, False)
        return c

    lax.fori_loop(0, i - 1, far_body, 0)

    @pl.when(i >= 1)
    def _():
        att_block(i - 1, 1, False)

    att_block(i, 0, True)

    for hh in range(N_HEADS):
        o_ref[:, hh * HEAD_DIM:(hh + 1) * HEAD_DIM] = (acc_scr[hh] / l_scr[hh]).astype(o_ref.dtype)


def _dsa_attention(qi, ki, wi, qb, kb, vb, bias):
    b, nh, s, dh = qi.shape
    tq = ATT_BLOCK
    ktop = min(INDEX_TOPK_MAX, s // 4)
    hm_q = pl.BlockSpec((None, nh, tq, dh), lambda bb, i: (bb, 0, i, 0))
    full64 = pl.BlockSpec((None, s, dh), lambda bb, i: (bb, 0, 0))
    return pl.pallas_call(
        functools.partial(_dsa_kernel, ktop),
        grid=(b, s // tq),
        in_specs=[hm_q, full64,
                  pl.BlockSpec((None, tq, nh), lambda bb, i: (bb, i, 0)),
                  hm_q, full64, full64,
                  pl.BlockSpec((2, nh, tq, tq), lambda bb, i: (0, 0, 0, 0))],
        out_specs=pl.BlockSpec((None, tq, nh * dh), lambda bb, i: (bb, i, 0)),
        out_shape=jax.ShapeDtypeStruct((b, s, nh * dh), BF16),
        scratch_shapes=[pltpu.VMEM((s // tq, tq, tq), I32),
                        pltpu.VMEM((nh, tq, 1), F32),
                        pltpu.VMEM((nh, tq, 1), F32),
                        pltpu.VMEM((nh, tq, dh), F32)],
        compiler_params=pltpu.CompilerParams(
            dimension_semantics=("parallel", "parallel"), vmem_limit_bytes=VMEM_LIMIT),
        name="sparse_attention",
    )(qi, ki, wi, qb, kb, vb, bias)


def _first_max4(a):
    m1 = jnp.maximum(jnp.maximum(a[0], a[1]), jnp.maximum(a[2], a[3]))
    i1 = jnp.where(a[0] == m1, 0, jnp.where(a[1] == m1, 1, jnp.where(a[2] == m1, 2, 3)))
    rest = [jnp.where(i1 == j, -1.0, a[j]) for j in range(4)]
    m2 = jnp.maximum(jnp.maximum(rest[0], rest[1]), jnp.maximum(rest[2], rest[3]))
    i2 = jnp.where(rest[0] == m2, 0, jnp.where(rest[1] == m2, 1, jnp.where(rest[2] == m2, 2, 3)))
    return m1, i1, m2, i2


def _route(lt):
    lg = [lt[e:e + 1, :] for e in range(N_EXPERTS)]
    mx = functools.reduce(jnp.maximum, lg)
    ex = [jnp.exp(v - mx) for v in lg]
    tot = functools.reduce(lambda u, v: u + v, ex)
    p = [v / tot for v in ex]
    gs = []
    for g in range(N_GROUPS):
        m1, _, m2, _ = _first_max4(p[4 * g:4 * g + 4])
        gs.append(m1 + m2)
    _, gbest, _, _ = _first_max4(gs)
    chosen = [jnp.where(gbest == 0, p[j], jnp.where(gbest == 1, p[4 + j],
              jnp.where(gbest == 2, p[8 + j], p[12 + j]))) for j in range(4)]
    m1, i1, m2, i2 = _first_max4(chosen)
    den = m1 + m2
    w1 = m1 / den
    w2 = m2 / den
    rows = []
    for e in range(N_EXPERTS):
        g, j = divmod(e, EXPERTS_PER_GROUP)
        rows.append(jnp.where(gbest == g, jnp.where(i1 == j, w1, 0.0) + jnp.where(i2 == j, w2, 0.0), 0.0))
    return rows


def _merge_kernel(oa_ref, ob_ref, sga_ref, sgb_ref, x_ref, mod_ref, woa_ref, wob_ref, wout_ref,
                  g2_ref, rwh_ref, rwl_ref, rb_ref, x1_ref, h2_ref, comb_ref):
    tm = x_ref.shape[0]
    merged = (sga_ref[...].astype(F32) * _dot(oa_ref[...], woa_ref[...])
              + sgb_ref[...].astype(F32) * _dot(ob_ref[...], wob_ref[...]))
    x1 = x_ref[...] + mod_ref[2:3, :] * _dot(merged.astype(BF16), wout_ref[...])
    x1_ref[...] = x1
    h2 = (_rms(x1) * g2_ref[...]) * (1.0 + mod_ref[4:5, :]) + mod_ref[3:4, :]
    hi = h2.astype(BF16)
    h2_ref[...] = hi
    lo = (h2 - hi.astype(F32)).astype(BF16)
    logits = (_dot(hi, rwh_ref[...]) + _dot(lo, rwh_ref[...]) + _dot(hi, rwl_ref[...])) + rb_ref[...]
    rows = _route(logits.T)
    sub = lax.broadcasted_iota(I32, (LANES, tm), 0)
    comb_t = jnp.zeros((LANES, tm), F32)
    for e in range(N_EXPERTS):
        comb_t = jnp.where(sub == e, jnp.broadcast_to(rows[e], (LANES, tm)), comb_t)
    comb_ref[...] = comb_t.T


def _merge(oa, ob, sga, sgb, x, mod_l, woa, wob, wout, g2, rwh, rwl, rb, tm):
    b, s, d = x.shape
    hw = oa.shape[-1]
    tok = lambda n: pl.BlockSpec((None, tm, n), lambda bb, i: (bb, i, 0))
    const = lambda r, c: pl.BlockSpec((r, c), lambda bb, i: (0, 0))
    return pl.pallas_call(
        _merge_kernel,
        grid=(b, s // tm),
        in_specs=[tok(hw), tok(hw), tok(d), tok(d), tok(d),
                  pl.BlockSpec((None, N_MOD, d), lambda bb, i: (bb, 0, 0)),
                  const(hw, d), const(hw, d), const(d, d), const(1, d),
                  const(d, LANES), const(d, LANES), const(1, LANES)],
        out_specs=[tok(d), tok(d), tok(LANES)],
        out_shape=[jax.ShapeDtypeStruct((b, s, d), F32),
                   jax.ShapeDtypeStruct((b, s, d), BF16),
                   jax.ShapeDtypeStruct((b, s, LANES), F32)],
        compiler_params=pltpu.CompilerParams(
            dimension_semantics=("parallel", "parallel"), vmem_limit_bytes=VMEM_LIMIT),
        name="merge_route",
    )(oa, ob, sga, sgb, x, mod_l, woa, wob, wout, g2, rwh, rwl, rb)


def _moe_kernel(h2_ref, comb_ref, wg_ref, wu_ref, wd_ref, x1_ref, mod_ref, o_ref, acc_ref):
    e = pl.program_id(2)

    @pl.when(e == 0)
    def _():
        acc_ref[...] = jnp.zeros(acc_ref.shape, F32)

    h = h2_ref[...]
    lane = lax.broadcasted_iota(I32, comb_ref.shape, 1)
    cw = jnp.sum(jnp.where(lane == e, comb_ref[...], 0.0), axis=-1, keepdims=True)
    gt = _dot(h, wg_ref[...])
    hidden = (gt * jax.nn.sigmoid(gt)) * _dot(h, wu_ref[...]) * cw
    acc_ref[...] += _dot(hidden.astype(BF16), wd_ref[...])

    @pl.when(e == pl.num_programs(2) - 1)
    def _():
        o_ref[...] = x1_ref[...] + mod_ref[5:6, :] * acc_ref[...]


def _moe(h2, comb, wg, wu, wd, x1, mod_l, tm):
    b, s, d = x1.shape
    ne, _, f = wg.shape
    tok = lambda n: pl.BlockSpec((None, tm, n), lambda bb, i, e: (bb, i, 0))
    return pl.pallas_call(
        _moe_kernel,
        grid=(b, s // tm, ne),
        in_specs=[tok(d), tok(LANES),
                  pl.BlockSpec((None, d, f), lambda bb, i, e: (e, 0, 0)),
                  pl.BlockSpec((None, d, f), lambda bb, i, e: (e, 0, 0)),
                  pl.BlockSpec((None, f, d), lambda bb, i, e: (e, 0, 0)),
                  tok(d),
                  pl.BlockSpec((None, N_MOD, d), lambda bb, i, e: (bb, 0, 0))],
        out_specs=tok(d),
        out_shape=jax.ShapeDtypeStruct((b, s, d), F32),
        scratch_shapes=[pltpu.VMEM((tm, d), F32)],
        compiler_params=pltpu.CompilerParams(
            dimension_semantics=("parallel", "parallel", "arbitrary"), vmem_limit_bytes=VMEM_LIMIT),
        name="experts",
    )(h2, comb, wg, wu, wd, x1, mod_l)


def kernel(x, c, w_ada, b_ada, norm1_g, w_in, q_norm_g, k_norm_g, rel_bias, w_o_a, w_o_b, w_out,
           norm2_g, router_w, router_b, w_gate, w_up, w_down):
    b, s, d = x.shape
    depth = w_ada.shape[0]
    assert s % ATT_BLOCK == 0 and d % LANES == 0
    tm = min(512, s)

    mod = _adaln(c, w_ada, b_ada)
    bias = _bias_tiles(rel_bias, ATT_BLOCK)
    rw = jnp.pad(router_w, ((0, 0), (0, LANES - N_EXPERTS)))
    rwh = rw.astype(BF16)
    rwl = (rw - rwh.astype(F32)).astype(BF16)
    rb = jnp.pad(router_b, (0, LANES - N_EXPERTS)).reshape(1, LANES)

    for l in range(depth):
        qa, ka, va, qb, qi, kb, vb, ki, wi, sga, sgb = _in_proj(
            x, mod[l], norm1_g[l].reshape(1, d), _pack_w_in(w_in[l]),
            q_norm_g[l].reshape(1, HEAD_DIM), k_norm_g[l].reshape(1, HEAD_DIM), tm)
        oa = _sb_attention(qa, ka, va)
        ob = _dsa_attention(qi, ki, wi, qb, kb, vb, bias)
        x1, h2, comb = _merge(oa, ob, sga, sgb, x, mod[l], w_o_a[l].astype(BF16),
                              w_o_b[l].astype(BF16), w_out[l].astype(BF16),
                              norm2_g[l].reshape(1, d), rwh, rwl, rb, tm)
        x = _moe(h2, comb, w_gate[l].astype(BF16), w_up[l].astype(BF16), w_down[l].astype(BF16),
                 x1, mod[l], tm)
    return x
```

```python
import functools
import math

import numpy as np
import jax
import jax.numpy as jnp
from jax import lax
from jax.experimental import pallas as pl
from jax.experimental.pallas import tpu as pltpu

F32 = jnp.float32
BF16 = jnp.bfloat16
I32 = jnp.int32

HEAD_DIM = 64
N_HEADS = 8
N_EXPERTS = 16
N_GROUPS = 4
EXPERTS_PER_GROUP = 4
N_MOD = 6
EPS = 1e-6
INDEX_TOPK_MAX = 256
REL_BUCKETS = 32
REL_MAX_DIST = 128
LANES = 128
DSA_QUERY_BLOCK = 256
DSA_KEY_BLOCK = 128
SB_QUERY_BLOCK = 128
SB_KEY_BLOCK = 256
LOG2E = 1.4426950408889634
NEG_BIG = -1e30
INT_MIN = -2 ** 31

VMEM_LIMIT = 56 * 1024 * 1024


def _nt_dot(a, b):
    return lax.dot_general(a, b, (((1,), (1,)), ((), ())), preferred_element_type=F32)


def _dot(a, b):
    return jnp.dot(a, b, preferred_element_type=F32)


def _mod_kernel(c_ref, w_ref, b_ref, o_ref):
    c = c_ref[...]
    sc = c * jax.nn.sigmoid(c)
    o_ref[...] = jnp.dot(sc, w_ref[...], preferred_element_type=F32,
                         precision=lax.Precision.HIGHEST) + b_ref[...]


def _adaln(c, w_ada, b_ada):
    depth, d, nd = w_ada.shape
    b = c.shape[0]
    out = pl.pallas_call(
        _mod_kernel,
        grid=(depth, nd // d),
        in_specs=[pl.BlockSpec((b, d), lambda l, j: (0, 0)),
                  pl.BlockSpec((None, d, d), lambda l, j: (l, 0, j)),
                  pl.BlockSpec((None, 1, d), lambda l, j: (l, 0, j))],
        out_specs=pl.BlockSpec((None, b, d), lambda l, j: (l, 0, j)),
        out_shape=jax.ShapeDtypeStruct((depth, b, nd), F32),
        compiler_params=pltpu.CompilerParams(vmem_limit_bytes=VMEM_LIMIT),
        name="adaln",
    )(c, w_ada, b_ada.reshape(depth, 1, nd))
    return out.reshape(depth, b, N_MOD, d)


def _rms(x):
    return x * lax.rsqrt(jnp.mean(x * x, axis=-1, keepdims=True) + EPS)


def _in_kernel(x_ref, mod_ref, g1_ref, w_ref, qng_ref, kng_ref,
               qa_ref, ka_ref, va_ref, qbt_ref, qit_ref, kb_ref, vt_ref, ki_ref, wit_ref,
               sga_ref, sgb_ref):
    d = x_ref.shape[-1]
    tm = x_ref.shape[0]
    hw = N_HEADS * HEAD_DIM
    x = x_ref[...]
    h = (_rms(x) * g1_ref[...]) * (1.0 + mod_ref[1:2, :]) + mod_ref[0:1, :]
    hb = h.astype(BF16)

    def proj(lo, n):
        return _dot(hb, w_ref[:, lo:lo + n])

    def heads(r, ref):
        for hh in range(N_HEADS):
            ref[hh] = r[:, hh * HEAD_DIM:(hh + 1) * HEAD_DIM].astype(ref.dtype)

    scale = HEAD_DIM ** -0.5
    heads(proj(0, hw) * (scale * LOG2E), qa_ref)
    heads(proj(hw, hw), ka_ref)
    heads(proj(2 * hw, hw), va_ref)

    rt = proj(3 * hw, hw).T
    for hh in range(N_HEADS):
        slab = rt[hh * HEAD_DIM:(hh + 1) * HEAD_DIM, :]
        inv = lax.rsqrt(jnp.mean(slab * slab, axis=0, keepdims=True) + EPS)
        qbt_ref[hh] = (((slab * inv) * qng_ref[...]) * scale).astype(BF16)

    r = proj(4 * hw, 2 * HEAD_DIM)
    kb_ref[...] = (_rms(r[:, :HEAD_DIM]) * kng_ref[...]).astype(BF16)
    vt = r.T[HEAD_DIM:, :]
    ones_row = (lax.broadcasted_iota(I32, (HEAD_DIM, LANES), 0) == 0).astype(BF16)
    for cc in range(tm // LANES):
        vt_ref[cc, :HEAD_DIM, :] = vt[:, cc * LANES:(cc + 1) * LANES].astype(BF16)
        vt_ref[cc, HEAD_DIM:, :] = ones_row

    rt = proj(4 * hw + LANES, hw).T
    for hh in range(N_HEADS):
        qit_ref[hh] = rt[hh * HEAD_DIM:(hh + 1) * HEAD_DIM, :].astype(BF16)

    r = proj(5 * hw + LANES, LANES)
    ki_ref[...] = r[:, :HEAD_DIM].astype(BF16)
    wit_ref[...] = r.T[HEAD_DIM:HEAD_DIM + N_HEADS, :]

    off = 5 * hw + 2 * LANES
    sga_ref[...] = jax.nn.sigmoid(proj(off, d)).astype(BF16)
    sgb_ref[...] = jax.nn.sigmoid(proj(off + d, d)).astype(BF16)


def _pack_w_in(w_in_l):
    hw = N_HEADS * HEAD_DIM
    a = 4 * hw + 2 * HEAD_DIM + hw
    small = w_in_l[:, a:a + HEAD_DIM + N_HEADS]
    small = jnp.pad(small, ((0, 0), (0, LANES - small.shape[1])))
    gates = w_in_l[:, a + HEAD_DIM + N_HEADS:]
    return jnp.concatenate([w_in_l[:, :a], small, gates], axis=1).astype(BF16)


def _in_proj(x, mod_l, g1, w_packed, qng, kng, tm):
    b, s, d = x.shape
    hm = jax.ShapeDtypeStruct((b, N_HEADS, s, HEAD_DIM), BF16)
    hmt = jax.ShapeDtypeStruct((b, N_HEADS, HEAD_DIM, s), BF16)
    tok64 = jax.ShapeDtypeStruct((b, s, HEAD_DIM), BF16)
    hm_spec = pl.BlockSpec((None, N_HEADS, tm, HEAD_DIM), lambda bb, i: (bb, 0, i, 0))
    hmt_spec = pl.BlockSpec((None, N_HEADS, HEAD_DIM, tm), lambda bb, i: (bb, 0, 0, i))
    tok = lambda n: pl.BlockSpec((None, tm, n), lambda bb, i: (bb, i, 0))
    nw = w_packed.shape[1]
    return pl.pallas_call(
        _in_kernel,
        grid=(b, s // tm),
        in_specs=[tok(d),
                  pl.BlockSpec((None, N_MOD, d), lambda bb, i: (bb, 0, 0)),
                  pl.BlockSpec((1, d), lambda bb, i: (0, 0)),
                  pl.BlockSpec((d, nw), lambda bb, i: (0, 0)),
                  pl.BlockSpec((HEAD_DIM, 1), lambda bb, i: (0, 0)),
                  pl.BlockSpec((1, HEAD_DIM), lambda bb, i: (0, 0))],
        out_specs=[hm_spec] * 3 + [hmt_spec] * 2
                  + [tok(HEAD_DIM),
                     pl.BlockSpec((None, tm // LANES, 2 * HEAD_DIM, LANES), lambda bb, i: (bb, i, 0, 0)),
                     tok(HEAD_DIM),
                     pl.BlockSpec((None, N_HEADS, tm), lambda bb, i: (bb, 0, i)),
                     tok(d), tok(d)],
        out_shape=[hm] * 3 + [hmt] * 2
                  + [tok64,
                     jax.ShapeDtypeStruct((b, s // LANES, 2 * HEAD_DIM, LANES), BF16),
                     tok64,
                     jax.ShapeDtypeStruct((b, N_HEADS, s), F32),
                     jax.ShapeDtypeStruct((b, s, d), BF16),
                     jax.ShapeDtypeStruct((b, s, d), BF16)],
        compiler_params=pltpu.CompilerParams(
            dimension_semantics=("parallel", "parallel"), vmem_limit_bytes=VMEM_LIMIT),
        name="in_proj",
    )(x, mod_l, g1, w_packed, qng, kng)


def _sb_kernel(q_ref, k_ref, v_ref, o_ref, acc_scr, csum_scr):
    tq = q_ref.shape[1]
    tk = SB_KEY_BLOCK
    i = pl.program_id(1)
    nfull = (i * tq) // tk
    row = lax.broadcasted_iota(I32, (tq, tk), 0)
    col = lax.broadcasted_iota(I32, (tq, tk), 1)
    before = (col - row) < (i * tq - nfull * tk)
    krow = lax.broadcasted_iota(I32, (tk, tk), 0)
    kcol = lax.broadcasted_iota(I32, (tk, tk), 1)
    suffix = (krow > kcol).astype(BF16)

    def block(q, ks, vs, csum, diag):
        z2 = _nt_dot(q, ks)
        sp2 = jnp.maximum(z2, 0.0) + jnp.log2(1.0 + jnp.exp2(-jnp.abs(z2)))
        if diag:
            sp2 = jnp.where(before, sp2, 0.0)
        suf = _dot(sp2.astype(BF16), suffix)
        a = jnp.exp2((z2 - sp2) - suf - csum)
        if diag:
            a = jnp.where(before, a, 0.0)
        contrib = _dot(a.astype(BF16), vs)
        return contrib, csum + jnp.sum(sp2, axis=-1, keepdims=True)

    start = pl.multiple_of(nfull * tk, tk)
    for hh in range(N_HEADS):
        acc, csum = block(q_ref[hh], k_ref[hh, pl.ds(start, tk), :], v_ref[hh, pl.ds(start, tk), :],
                          jnp.zeros((tq, 1), F32), True)
        acc_scr[hh] = acc
        csum_scr[hh] = csum

    def body(jj, carry):
        st = pl.multiple_of((nfull - 1 - jj) * tk, tk)
        for hh in range(N_HEADS):
            c, csum = block(q_ref[hh], k_ref[hh, pl.ds(st, tk), :], v_ref[hh, pl.ds(st, tk), :],
                            csum_scr[hh], False)
            acc_scr[hh] += c
            csum_scr[hh] = csum
        return carry

    lax.fori_loop(0, nfull, body, 0)
    for hh in range(N_HEADS):
        o_ref[:, hh * HEAD_DIM:(hh + 1) * HEAD_DIM] = acc_scr[hh].astype(o_ref.dtype)


def _sb_attention(qa, ka, va):
    b, nh, s, dh = qa.shape
    tq = SB_QUERY_BLOCK
    assert SB_KEY_BLOCK % tq == 0 and s % SB_KEY_BLOCK == 0
    return pl.pallas_call(
        _sb_kernel,
        grid=(b, s // tq),
        in_specs=[pl.BlockSpec((None, nh, tq, dh), lambda bb, i: (bb, 0, i, 0)),
                  pl.BlockSpec((None, nh, s, dh), lambda bb, i: (bb, 0, 0, 0)),
                  pl.BlockSpec((None, nh, s, dh), lambda bb, i: (bb, 0, 0, 0))],
        out_specs=pl.BlockSpec((None, tq, nh * dh), lambda bb, i: (bb, i, 0)),
        out_shape=jax.ShapeDtypeStruct((b, s, nh * dh), BF16),
        scratch_shapes=[pltpu.VMEM((nh, tq, dh), F32), pltpu.VMEM((nh, tq, 1), F32)],
        compiler_params=pltpu.CompilerParams(
            dimension_semantics=("parallel", "parallel"), vmem_limit_bytes=VMEM_LIMIT),
        name="stick_breaking",
    )(qa, ka, va)


def _t5_bucket_np(dist):
    max_exact = REL_BUCKETS // 2
    d_f = np.maximum(dist, 1).astype(np.float32)
    large = max_exact + (np.log(d_f / np.float32(max_exact))
                         / np.float32(math.log(REL_MAX_DIST / max_exact))
                         * np.float32(REL_BUCKETS - max_exact)).astype(np.int32)
    large = np.minimum(large, REL_BUCKETS - 1)
    return np.where(dist < max_exact, dist, large).astype(np.int32)


def _bias_tiles(rel_bias, tq, tk):
    n_near = tq // tk + 1
    key = np.arange(tk)[:, None]
    query = np.arange(tq)[None, :]
    bk = np.stack([_t5_bucket_np(np.maximum(query - key - (n - 1) * tk, 0)) for n in range(n_near)])
    far = int(_t5_bucket_np(np.array([REL_MAX_DIST]))[0])
    nh = rel_bias.shape[1]

    def body(rb_ref, bk_ref, o_ref):
        hh = pl.program_id(1)
        buckets = bk_ref[...]
        tile = jnp.zeros((tk, tq), F32)
        for bucket in range(REL_BUCKETS):
            tile = jnp.where(buckets == bucket, rb_ref[hh, bucket] - rb_ref[hh, far], tile)
        o_ref[...] = tile

    return pl.pallas_call(
        body,
        grid=(n_near, nh),
        in_specs=[pl.BlockSpec(memory_space=pltpu.SMEM),
                  pl.BlockSpec((None, tk, tq), lambda n, hh: (n, 0, 0))],
        out_specs=pl.BlockSpec((None, None, tk, tq), lambda n, hh: (n, hh, 0, 0)),
        out_shape=jax.ShapeDtypeStruct((n_near, nh, tk, tq), F32),
        name="bias_tiles",
    )(rel_bias.T, jnp.asarray(bk))


def _sort_key(x):
    bits = pltpu.bitcast(x, I32)
    return bits ^ ((bits >> 31) & 0x7FFFFFFF)


def _fold_rows(x, op):
    return functools.reduce(op, [x[g * 8:(g + 1) * 8, :] for g in range(x.shape[0] // 8)])


def _dsa_kernel(ktop, qit_ref, ki_ref, wit_ref, qbt_ref, kb_ref, vt_ref, bias_ref, o_ref,
                key_scr, m_scr, acc_scr):
    tq = qit_ref.shape[2]
    tk = DSA_KEY_BLOCK
    per_q = tq // tk
    i = pl.program_id(1)
    first_own = per_q * i
    nchunks = per_q * (i + 1)
    key_i = lax.broadcasted_iota(I32, (tk, tq), 0)
    qry_i = lax.broadcasted_iota(I32, (tk, tq), 1)

    def causal(rel):
        return (key_i - qry_i) <= (-rel * tk)

    def for_chunks(fn):
        def far(c, carry):
            fn(c, None)
            return carry
        lax.fori_loop(0, first_own - 1, far, 0)

        @pl.when(i >= 1)
        def _():
            fn(first_own - 1, -1)

        for rel in range(per_q):
            fn(first_own + rel, rel)

    def score_chunk(c, rel):
        kic = ki_ref[pl.ds(pl.multiple_of(c * tk, tk), tk), :]
        score = jnp.zeros((tk, tq), F32)
        for hh in range(N_HEADS):
            score = score + wit_ref[hh:hh + 1, :] * jnp.maximum(_dot(kic, qit_ref[hh]), 0.0)
        keys = _sort_key(score)
        if rel is not None and rel >= 0:
            keys = jnp.where(causal(rel), keys, INT_MIN)
        key_scr[c] = keys

    for_chunks(score_chunk)

    def count(pred):
        def body(c, part):
            return part + _fold_rows(jnp.where(pred(key_scr[c]), 1, 0), jnp.add)
        part = lax.fori_loop(0, nchunks, body, jnp.zeros((8, tq), I32))
        return jnp.sum(part, axis=0, keepdims=True)

    def bit_body(bi, t_u):
        cand_u = t_u | jnp.left_shift(jnp.int32(1), 31 - bi)
        cand = cand_u ^ INT_MIN
        return jnp.where(count(lambda k: k >= cand) >= ktop, cand_u, t_u)

    thr = lax.fori_loop(0, 32, bit_body, jnp.zeros((1, tq), I32)) ^ INT_MIN
    n_ge = count(lambda k: k >= thr)

    @pl.when(jnp.max(n_ge) > ktop)
    def _():
        need = (ktop - count(lambda k: k > thr)).astype(F32)
        lower = (lax.broadcasted_iota(I32, (tk, tk), 1)
                 < lax.broadcasted_iota(I32, (tk, tk), 0)).astype(BF16)

        def body(c, seen):
            k = key_scr[c]
            eq = k == thr
            eqf = eq.astype(BF16)
            rank = seen + _dot(lower, eqf)
            drop = eq & (rank >= need) & (n_ge > ktop)
            key_scr[c] = jnp.where(drop, thr - 1, k)
            return seen + jnp.sum(eqf.astype(F32), axis=0, keepdims=True)

        lax.fori_loop(0, nchunks, body, jnp.zeros((1, tq), F32))

    def logits(c, rel):
        sel = key_scr[c] >= thr
        if rel is not None and rel >= 0:
            sel = sel & causal(rel)
        kbc = kb_ref[pl.ds(pl.multiple_of(c * tk, tk), tk), :]

        def head(hh):
            s = _dot(kbc, qbt_ref[hh])
            if rel is not None:
                s = s + bias_ref[rel + 1, hh]
            return s
        return sel, head

    m_scr[...] = jnp.full(m_scr.shape, NEG_BIG, F32)

    def max_chunk(c, rel):
        sel, head = logits(c, rel)
        for hh in range(N_HEADS):
            s = jnp.where(sel, head(hh), NEG_BIG)
            m_scr[hh] = jnp.maximum(m_scr[hh], _fold_rows(s, jnp.maximum))

    for_chunks(max_chunk)
    m = [jnp.max(m_scr[hh], axis=0, keepdims=True) for hh in range(N_HEADS)]

    acc_scr[...] = jnp.zeros(acc_scr.shape, F32)

    def att_chunk(c, rel):
        sel, head = logits(c, rel)
        vtc = vt_ref[c]
        for hh in range(N_HEADS):
            p = jnp.exp(jnp.where(sel, head(hh) - m[hh], NEG_BIG))
            acc_scr[hh] += _dot(vtc, p.astype(BF16))

    for_chunks(att_chunk)

    out_t = jnp.concatenate(
        [acc_scr[hh, :HEAD_DIM, :] / acc_scr[hh, HEAD_DIM:HEAD_DIM + 1, :] for hh in range(N_HEADS)],
        axis=0)
    o_ref[...] = out_t.T.astype(o_ref.dtype)


def _dsa_attention(qit, ki, wit, qbt, kb, vt, bias):
    b, nh, dh, s = qit.shape
    tq = DSA_QUERY_BLOCK
    tk = DSA_KEY_BLOCK
    assert tk == REL_MAX_DIST and tk == LANES and tq % tk == 0 and s % tq == 0
    ktop = min(INDEX_TOPK_MAX, s // 4)
    hmt_q = pl.BlockSpec((None, nh, dh, tq), lambda bb, i: (bb, 0, 0, i))
    full64 = pl.BlockSpec((None, s, dh), lambda bb, i: (bb, 0, 0))
    return pl.pallas_call(
        functools.partial(_dsa_kernel, ktop),
        grid=(b, s // tq),
        in_specs=[hmt_q, full64,
                  pl.BlockSpec((None, nh, tq), lambda bb, i: (bb, 0, i)),
                  hmt_q, full64,
                  pl.BlockSpec((None, s // tk, 2 * dh, tk), lambda bb, i: (bb, 0, 0, 0)),
                  pl.BlockSpec(bias.shape, lambda bb, i: (0, 0, 0, 0))],
        out_specs=pl.BlockSpec((None, tq, nh * dh), lambda bb, i: (bb, i, 0)),
        out_shape=jax.ShapeDtypeStruct((b, s, nh * dh), BF16),
        scratch_shapes=[pltpu.VMEM((s // tk, tk, tq), I32),
                        pltpu.VMEM((nh, 8, tq), F32),
                        pltpu.VMEM((nh, 2 * dh, tq), F32)],
        compiler_params=pltpu.CompilerParams(
            dimension_semantics=("parallel", "parallel"), vmem_limit_bytes=VMEM_LIMIT),
        name="sparse_attention",
    )(qit, ki, wit, qbt, kb, vt, bias)


def _first_max4(a):
    m1 = jnp.maximum(jnp.maximum(a[0], a[1]), jnp.maximum(a[2], a[3]))
    i1 = jnp.where(a[0] == m1, 0, jnp.where(a[1] == m1, 1, jnp.where(a[2] == m1, 2, 3)))
    rest = [jnp.where(i1 == j, -1.0, a[j]) for j in range(4)]
    m2 = jnp.maximum(jnp.maximum(rest[0], rest[1]), jnp.maximum(rest[2], rest[3]))
    i2 = jnp.where(rest[0] == m2, 0, jnp.where(rest[1] == m2, 1, jnp.where(rest[2] == m2, 2, 3)))
    return m1, i1, m2, i2


def _route(lt):
    lg = [lt[e:e + 1, :] for e in range(N_EXPERTS)]
    mx = functools.reduce(jnp.maximum, lg)
    ex = [jnp.exp(v - mx) for v in lg]
    tot = functools.reduce(lambda u, v: u + v, ex)
    p = [v / tot for v in ex]
    gs = []
    for g in range(N_GROUPS):
        m1, _, m2, _ = _first_max4(p[4 * g:4 * g + 4])
        gs.append(m1 + m2)
    _, gbest, _, _ = _first_max4(gs)
    chosen = [jnp.where(gbest == 0, p[j], jnp.where(gbest == 1, p[4 + j],
              jnp.where(gbest == 2, p[8 + j], p[12 + j]))) for j in range(4)]
    m1, i1, m2, i2 = _first_max4(chosen)
    den = m1 + m2
    w1 = m1 / den
    w2 = m2 / den
    rows = []
    for e in range(N_EXPERTS):
        g, j = divmod(e, EXPERTS_PER_GROUP)
        rows.append(jnp.where(gbest == g, jnp.where(i1 == j, w1, 0.0) + jnp.where(i2 == j, w2, 0.0), 0.0))
    return rows


def _merge_kernel(oa_ref, ob_ref, sga_ref, sgb_ref, x_ref, mod_ref, woa_ref, wob_ref, wout_ref,
                  g2_ref, rwh_ref, rwl_ref, rb_ref, x1_ref, h2_ref, comb_ref):
    tm = x_ref.shape[0]
    merged = (sga_ref[...].astype(F32) * _dot(oa_ref[...], woa_ref[...])
              + sgb_ref[...].astype(F32) * _dot(ob_ref[...], wob_ref[...]))
    x1 = x_ref[...] + mod_ref[2:3, :] * _dot(merged.astype(BF16), wout_ref[...])
    x1_ref[...] = x1
    h2 = (_rms(x1) * g2_ref[...]) * (1.0 + mod_ref[4:5, :]) + mod_ref[3:4, :]
    hi = h2.astype(BF16)
    h2_ref[...] = hi
    lo = (h2 - hi.astype(F32)).astype(BF16)
    logits = (_dot(hi, rwh_ref[...]) + _dot(lo, rwh_ref[...]) + _dot(hi, rwl_ref[...])) + rb_ref[...]
    rows = _route(logits.T)
    sub = lax.broadcasted_iota(I32, (LANES, tm), 0)
    comb_t = jnp.zeros((LANES, tm), F32)
    for e in range(N_EXPERTS):
        comb_t = jnp.where(sub == e, jnp.broadcast_to(rows[e], (LANES, tm)), comb_t)
    comb_ref[...] = comb_t.T


def _merge(oa, ob, sga, sgb, x, mod_l, woa, wob, wout, g2, rwh, rwl, rb, tm):
    b, s, d = x.shape
    hw = oa.shape[-1]
    tok = lambda n: pl.BlockSpec((None, tm, n), lambda bb, i: (bb, i, 0))
    const = lambda r, c: pl.BlockSpec((r, c), lambda bb, i: (0, 0))
    return pl.pallas_call(
        _merge_kernel,
        grid=(b, s // tm),
        in_specs=[tok(hw), tok(hw), tok(d), tok(d), tok(d),
                  pl.BlockSpec((None, N_MOD, d), lambda bb, i: (bb, 0, 0)),
                  const(hw, d), const(hw, d), const(d, d), const(1, d),
                  const(d, LANES), const(d, LANES), const(1, LANES)],
        out_specs=[tok(d), tok(d), tok(LANES)],
        out_shape=[jax.ShapeDtypeStruct((b, s, d), F32),
                   jax.ShapeDtypeStruct((b, s, d), BF16),
                   jax.ShapeDtypeStruct((b, s, LANES), F32)],
        compiler_params=pltpu.CompilerParams(
            dimension_semantics=("parallel", "parallel"), vmem_limit_bytes=VMEM_LIMIT),
        name="merge_route",
    )(oa, ob, sga, sgb, x, mod_l, woa, wob, wout, g2, rwh, rwl, rb)


def _moe_kernel(h2_ref, comb_ref, wg_ref, wu_ref, wd_ref, x1_ref, mod_ref, o_ref, acc_ref):
    e = pl.program_id(2)

    @pl.when(e == 0)
    def _():
        acc_ref[...] = jnp.zeros(acc_ref.shape, F32)

    h = h2_ref[...]
    lane = lax.broadcasted_iota(I32, comb_ref.shape, 1)
    cw = jnp.sum(jnp.where(lane == e, comb_ref[...], 0.0), axis=-1, keepdims=True)
    gt = _dot(h, wg_ref[...])
    hidden = (gt * jax.nn.sigmoid(gt)) * _dot(h, wu_ref[...]) * cw
    acc_ref[...] += _dot(hidden.astype(BF16), wd_ref[...])

    @pl.when(e == pl.num_programs(2) - 1)
    def _():
        o_ref[...] = x1_ref[...] + mod_ref[5:6, :] * acc_ref[...]


def _moe(h2, comb, wg, wu, wd, x1, mod_l, tm):
    b, s, d = x1.shape
    ne, _, f = wg.shape
    tok = lambda n: pl.BlockSpec((None, tm, n), lambda bb, i, e: (bb, i, 0))
    return pl.pallas_call(
        _moe_kernel,
        grid=(b, s // tm, ne),
        in_specs=[tok(d), tok(LANES),
                  pl.BlockSpec((None, d, f), lambda bb, i, e: (e, 0, 0)),
                  pl.BlockSpec((None, d, f), lambda bb, i, e: (e, 0, 0)),
                  pl.BlockSpec((None, f, d), lambda bb, i, e: (e, 0, 0)),
                  tok(d),
                  pl.BlockSpec((None, N_MOD, d), lambda bb, i, e: (bb, 0, 0))],
        out_specs=tok(d),
        out_shape=jax.ShapeDtypeStruct((b, s, d), F32),
        scratch_shapes=[pltpu.VMEM((tm, d), F32)],
        compiler_params=pltpu.CompilerParams(
            dimension_semantics=("parallel", "parallel", "arbitrary"), vmem_limit_bytes=VMEM_LIMIT),
        name="experts",
    )(h2, comb, wg, wu, wd, x1, mod_l)


def kernel(x, c, w_ada, b_ada, norm1_g, w_in, q_norm_g, k_norm_g, rel_bias, w_o_a, w_o_b, w_out,
           norm2_g, router_w, router_b, w_gate, w_up, w_down):
    b, s, d = x.shape
    depth = w_ada.shape[0]
    assert d % LANES == 0
    tm = min(512, s)

    mod = _adaln(c, w_ada, b_ada)
    bias = _bias_tiles(rel_bias, DSA_QUERY_BLOCK, DSA_KEY_BLOCK)
    rw = jnp.pad(router_w, ((0, 0), (0, LANES - N_EXPERTS)))
    rwh = rw.astype(BF16)
    rwl = (rw - rwh.astype(F32)).astype(BF16)
    rb = jnp.pad(router_b, (0, LANES - N_EXPERTS)).reshape(1, LANES)

    for l in range(depth):
        qa, ka, va, qbt, qit, kb, vt, ki, wit, sga, sgb = _in_proj(
            x, mod[l], norm1_g[l].reshape(1, d), _pack_w_in(w_in[l]),
            q_norm_g[l].reshape(HEAD_DIM, 1), k_norm_g[l].reshape(1, HEAD_DIM), tm)
        oa = _sb_attention(qa, ka, va)
        ob = _dsa_attention(qit, ki, wit, qbt, kb, vt, bias)
        x1, h2, comb = _merge(oa, ob, sga, sgb, x, mod[l], w_o_a[l].astype(BF16),
                              w_o_b[l].astype(BF16), w_out[l].astype(BF16),
                              norm2_g[l].reshape(1, d), rwh, rwl, rb, tm)
        x = _moe(h2, comb, w_gate[l].astype(BF16), w_up[l].astype(BF16), w_down[l].astype(BF16),
                 x1, mod[l], tm)
    return x
```

```python
import functools
import math

import numpy as np
import jax
import jax.numpy as jnp
from jax import lax
from jax.experimental import pallas as pl
from jax.experimental.pallas import tpu as pltpu

F32 = jnp.float32
BF16 = jnp.bfloat16
I32 = jnp.int32

HEAD_DIM = 64
N_HEADS = 8
N_EXPERTS = 16
N_GROUPS = 4
EXPERTS_PER_GROUP = 4
_PAIRS = ((0, 1), (0, 2), (0, 3), (1, 2), (1, 3), (2, 3))
N_PAIRS = len(_PAIRS)
N_CLASSES = N_GROUPS * N_PAIRS
N_CLASS_PAD = 32
MOE_ROW_TILE = 256
MOE_DMA_TILE = 512
N_MOD = 6
EPS = 1e-6
INDEX_TOPK_MAX = 256
REL_BUCKETS = 32
REL_MAX_DIST = 128
LANES = 128
DSA_QUERY_BLOCK = 256
DSA_KEY_BLOCK = 128
SB_QUERY_BLOCK = 128
SB_KEY_BLOCK = 256
LOG2E = 1.4426950408889634
NEG_BIG = -1e30
INT_MIN = -2 ** 31

VMEM_LIMIT = 56 * 1024 * 1024


def _nt_dot(a, b):
    return lax.dot_general(a, b, (((1,), (1,)), ((), ())), preferred_element_type=F32)


def _dot(a, b):
    return jnp.dot(a, b, preferred_element_type=F32)


def _mod_kernel(c_ref, w_ref, b_ref, o_ref):
    c = c_ref[...]
    sc = c * jax.nn.sigmoid(c)
    o_ref[...] = jnp.dot(sc, w_ref[...], preferred_element_type=F32,
                         precision=lax.Precision.HIGHEST) + b_ref[...]


def _adaln(c, w_ada, b_ada):
    depth, d, nd = w_ada.shape
    b = c.shape[0]
    out = pl.pallas_call(
        _mod_kernel,
        grid=(depth, nd // d),
        in_specs=[pl.BlockSpec((b, d), lambda l, j: (0, 0)),
                  pl.BlockSpec((None, d, d), lambda l, j: (l, 0, j)),
                  pl.BlockSpec((None, 1, d), lambda l, j: (l, 0, j))],
        out_specs=pl.BlockSpec((None, b, d), lambda l, j: (l, 0, j)),
        out_shape=jax.ShapeDtypeStruct((depth, b, nd), F32),
        compiler_params=pltpu.CompilerParams(vmem_limit_bytes=VMEM_LIMIT),
        name="adaln",
    )(c, w_ada, b_ada.reshape(depth, 1, nd))
    return out.reshape(depth, b, N_MOD, d)


def _rms(x):
    return x * lax.rsqrt(jnp.mean(x * x, axis=-1, keepdims=True) + EPS)


def _in_kernel(x_ref, mod_ref, g1_ref, w_ref, qng_ref, kng_ref,
               qa_ref, ka_ref, va_ref, qbt_ref, qit_ref, kb_ref, vt_ref, ki_ref, wit_ref,
               sga_ref, sgb_ref):
    d = x_ref.shape[-1]
    tm = x_ref.shape[0]
    hw = N_HEADS * HEAD_DIM
    x = x_ref[...]
    h = (_rms(x) * g1_ref[...]) * (1.0 + mod_ref[1:2, :]) + mod_ref[0:1, :]
    hb = h.astype(BF16)

    def proj(lo, n):
        return _dot(hb, w_ref[:, lo:lo + n])

    def heads(r, ref):
        for hh in range(N_HEADS):
            ref[hh] = r[:, hh * HEAD_DIM:(hh + 1) * HEAD_DIM].astype(ref.dtype)

    scale = HEAD_DIM ** -0.5
    heads(proj(0, hw) * (scale * LOG2E), qa_ref)
    heads(proj(hw, hw), ka_ref)
    heads(proj(2 * hw, hw), va_ref)

    rt = proj(3 * hw, hw).T
    for hh in range(N_HEADS):
        slab = rt[hh * HEAD_DIM:(hh + 1) * HEAD_DIM, :]
        inv = lax.rsqrt(jnp.mean(slab * slab, axis=0, keepdims=True) + EPS)
        qbt_ref[hh] = (((slab * inv) * qng_ref[...]) * scale).astype(BF16)

    r = proj(4 * hw, 2 * HEAD_DIM)
    kb_ref[...] = (_rms(r[:, :HEAD_DIM]) * kng_ref[...]).astype(BF16)
    vt = r.T[HEAD_DIM:, :]
    ones_row = (lax.broadcasted_iota(I32, (HEAD_DIM, LANES), 0) == 0).astype(BF16)
    for cc in range(tm // LANES):
        vt_ref[cc, :HEAD_DIM, :] = vt[:, cc * LANES:(cc + 1) * LANES].astype(BF16)
        vt_ref[cc, HEAD_DIM:, :] = ones_row

    rt = proj(4 * hw + LANES, hw).T
    for hh in range(N_HEADS):
        qit_ref[hh] = rt[hh * HEAD_DIM:(hh + 1) * HEAD_DIM, :].astype(BF16)

    r = proj(5 * hw + LANES, LANES)
    ki_ref[...] = r[:, :HEAD_DIM].astype(BF16)
    wit_ref[...] = r.T[HEAD_DIM:HEAD_DIM + N_HEADS, :]

    off = 5 * hw + 2 * LANES
    sga_ref[...] = jax.nn.sigmoid(proj(off, d)).astype(BF16)
    sgb_ref[...] = jax.nn.sigmoid(proj(off + d, d)).astype(BF16)


def _pack_w_in(w_in_l):
    hw = N_HEADS * HEAD_DIM
    a = 4 * hw + 2 * HEAD_DIM + hw
    small = w_in_l[:, a:a + HEAD_DIM + N_HEADS]
    small = jnp.pad(small, ((0, 0), (0, LANES - small.shape[1])))
    gates = w_in_l[:, a + HEAD_DIM + N_HEADS:]
    return jnp.concatenate([w_in_l[:, :a], small, gates], axis=1).astype(BF16)


def _in_proj(x, mod_l, g1, w_packed, qng, kng, tm):
    b, s, d = x.shape
    hm = jax.ShapeDtypeStruct((b, N_HEADS, s, HEAD_DIM), BF16)
    hmt = jax.ShapeDtypeStruct((b, N_HEADS, HEAD_DIM, s), BF16)
    tok64 = jax.ShapeDtypeStruct((b, s, HEAD_DIM), BF16)
    hm_spec = pl.BlockSpec((None, N_HEADS, tm, HEAD_DIM), lambda bb, i: (bb, 0, i, 0))
    hmt_spec = pl.BlockSpec((None, N_HEADS, HEAD_DIM, tm), lambda bb, i: (bb, 0, 0, i))
    tok = lambda n: pl.BlockSpec((None, tm, n), lambda bb, i: (bb, i, 0))
    nw = w_packed.shape[1]
    return pl.pallas_call(
        _in_kernel,
        grid=(b, s // tm),
        in_specs=[tok(d),
                  pl.BlockSpec((None, N_MOD, d), lambda bb, i: (bb, 0, 0)),
                  pl.BlockSpec((1, d), lambda bb, i: (0, 0)),
                  pl.BlockSpec((d, nw), lambda bb, i: (0, 0)),
                  pl.BlockSpec((HEAD_DIM, 1), lambda bb, i: (0, 0)),
                  pl.BlockSpec((1, HEAD_DIM), lambda bb, i: (0, 0))],
        out_specs=[hm_spec] * 3 + [hmt_spec] * 2
                  + [tok(HEAD_DIM),
                     pl.BlockSpec((None, tm // LANES, 2 * HEAD_DIM, LANES), lambda bb, i: (bb, i, 0, 0)),
                     tok(HEAD_DIM),
                     pl.BlockSpec((None, N_HEADS, tm), lambda bb, i: (bb, 0, i)),
                     tok(d), tok(d)],
        out_shape=[hm] * 3 + [hmt] * 2
                  + [tok64,
                     jax.ShapeDtypeStruct((b, s // LANES, 2 * HEAD_DIM, LANES), BF16),
                     tok64,
                     jax.ShapeDtypeStruct((b, N_HEADS, s), F32),
                     jax.ShapeDtypeStruct((b, s, d), BF16),
                     jax.ShapeDtypeStruct((b, s, d), BF16)],
        compiler_params=pltpu.CompilerParams(
            dimension_semantics=("parallel", "parallel"), vmem_limit_bytes=VMEM_LIMIT),
        name="in_proj",
    )(x, mod_l, g1, w_packed, qng, kng)


def _sb_kernel(q_ref, k_ref, v_ref, o_ref, acc_scr, csum_scr):
    tq = q_ref.shape[1]
    tk = SB_KEY_BLOCK
    i = pl.program_id(1)
    nfull = (i * tq) // tk
    row = lax.broadcasted_iota(I32, (tq, tk), 0)
    col = lax.broadcasted_iota(I32, (tq, tk), 1)
    before = (col - row) < (i * tq - nfull * tk)
    krow = lax.broadcasted_iota(I32, (tk, tk), 0)
    kcol = lax.broadcasted_iota(I32, (tk, tk), 1)
    suffix = (krow > kcol).astype(BF16)

    def block(q, ks, vs, csum, diag):
        z2 = _nt_dot(q, ks)
        sp2 = jnp.maximum(z2, 0.0) + jnp.log2(1.0 + jnp.exp2(-jnp.abs(z2)))
        if diag:
            sp2 = jnp.where(before, sp2, 0.0)
        suf = _dot(sp2.astype(BF16), suffix)
        a = jnp.exp2((z2 - sp2) - suf - csum)
        if diag:
            a = jnp.where(before, a, 0.0)
        contrib = _dot(a.astype(BF16), vs)
        return contrib, csum + jnp.sum(sp2, axis=-1, keepdims=True)

    start = pl.multiple_of(nfull * tk, tk)
    for hh in range(N_HEADS):
        acc, csum = block(q_ref[hh], k_ref[hh, pl.ds(start, tk), :], v_ref[hh, pl.ds(start, tk), :],
                          jnp.zeros((tq, 1), F32), True)
        acc_scr[hh] = acc
        csum_scr[hh] = csum

    def body(jj, carry):
        st = pl.multiple_of((nfull - 1 - jj) * tk, tk)
        for hh in range(N_HEADS):
            c, csum = block(q_ref[hh], k_ref[hh, pl.ds(st, tk), :], v_ref[hh, pl.ds(st, tk), :],
                            csum_scr[hh], False)
            acc_scr[hh] += c
            csum_scr[hh] = csum
        return carry

    lax.fori_loop(0, nfull, body, 0)
    for hh in range(N_HEADS):
        o_ref[:, hh * HEAD_DIM:(hh + 1) * HEAD_DIM] = acc_scr[hh].astype(o_ref.dtype)


def _sb_attention(qa, ka, va):
    b, nh, s, dh = qa.shape
    tq = SB_QUERY_BLOCK
    assert SB_KEY_BLOCK % tq == 0 and s % SB_KEY_BLOCK == 0
    return pl.pallas_call(
        _sb_kernel,
        grid=(b, s // tq),
        in_specs=[pl.BlockSpec((None, nh, tq, dh), lambda bb, i: (bb, 0, i, 0)),
                  pl.BlockSpec((None, nh, s, dh), lambda bb, i: (bb, 0, 0, 0)),
                  pl.BlockSpec((None, nh, s, dh), lambda bb, i: (bb, 0, 0, 0))],
        out_specs=pl.BlockSpec((None, tq, nh * dh), lambda bb, i: (bb, i, 0)),
        out_shape=jax.ShapeDtypeStruct((b, s, nh * dh), BF16),
        scratch_shapes=[pltpu.VMEM((nh, tq, dh), F32), pltpu.VMEM((nh, tq, 1), F32)],
        compiler_params=pltpu.CompilerParams(
            dimension_semantics=("parallel", "parallel"), vmem_limit_bytes=VMEM_LIMIT),
        name="stick_breaking",
    )(qa, ka, va)


def _t5_bucket_np(dist):
    max_exact = REL_BUCKETS // 2
    d_f = np.maximum(dist, 1).astype(np.float32)
    large = max_exact + (np.log(d_f / np.float32(max_exact))
                         / np.float32(math.log(REL_MAX_DIST / max_exact))
                         * np.float32(REL_BUCKETS - max_exact)).astype(np.int32)
    large = np.minimum(large, REL_BUCKETS - 1)
    return np.where(dist < max_exact, dist, large).astype(np.int32)


def _bias_tiles(rel_bias, tq, tk):
    n_near = tq // tk + 1
    key = np.arange(tk)[:, None]
    query = np.arange(tq)[None, :]
    bk = np.stack([_t5_bucket_np(np.maximum(query - key - (n - 1) * tk, 0)) for n in range(n_near)])
    far = int(_t5_bucket_np(np.array([REL_MAX_DIST]))[0])
    nh = rel_bias.shape[1]

    def body(rb_ref, bk_ref, o_ref):
        hh = pl.program_id(1)
        buckets = bk_ref[...]
        tile = jnp.zeros((tk, tq), F32)
        for bucket in range(REL_BUCKETS):
            tile = jnp.where(buckets == bucket, rb_ref[hh, bucket] - rb_ref[hh, far], tile)
        o_ref[...] = tile

    return pl.pallas_call(
        body,
        grid=(n_near, nh),
        in_specs=[pl.BlockSpec(memory_space=pltpu.SMEM),
                  pl.BlockSpec((None, tk, tq), lambda n, hh: (n, 0, 0))],
        out_specs=pl.BlockSpec((None, None, tk, tq), lambda n, hh: (n, hh, 0, 0)),
        out_shape=jax.ShapeDtypeStruct((n_near, nh, tk, tq), F32),
        name="bias_tiles",
    )(rel_bias.T, jnp.asarray(bk))


def _sort_key(x):
    bits = pltpu.bitcast(x, I32)
    return bits ^ ((bits >> 31) & 0x7FFFFFFF)


def _fold_rows(x, op):
    return functools.reduce(op, [x[g * 8:(g + 1) * 8, :] for g in range(x.shape[0] // 8)])


def _dsa_kernel(ktop, qit_ref, ki_ref, wit_ref, qbt_ref, kb_ref, vt_ref, bias_ref, o_ref,
                key_scr, m_scr, acc_scr):
    tq = qit_ref.shape[2]
    tk = DSA_KEY_BLOCK
    per_q = tq // tk
    i = pl.program_id(1)
    first_own = per_q * i
    nchunks = per_q * (i + 1)
    key_i = lax.broadcasted_iota(I32, (tk, tq), 0)
    qry_i = lax.broadcasted_iota(I32, (tk, tq), 1)

    def causal(rel):
        return (key_i - qry_i) <= (-rel * tk)

    def for_chunks(fn):
        def far(j, carry):
            for u in range(per_q):
                fn(j * per_q + u, None)
            return carry
        lax.fori_loop(0, i - 1, far, 0)

        @pl.when(i >= 1)
        def _():
            for rel in range(-per_q, -1):
                fn(first_own + rel, None)
            fn(first_own - 1, -1)

        for rel in range(per_q):
            fn(first_own + rel, rel)

    def score_chunk(c, rel):
        kic = ki_ref[pl.ds(pl.multiple_of(c * tk, tk), tk), :]
        score = jnp.zeros((tk, tq), F32)
        for hh in range(N_HEADS):
            score = score + wit_ref[hh:hh + 1, :] * jnp.maximum(_dot(kic, qit_ref[hh]), 0.0)
        keys = _sort_key(score)
        if rel is not None and rel >= 0:
            keys = jnp.where(causal(rel), keys, INT_MIN)
        key_scr[c] = keys

    for_chunks(score_chunk)

    def count(pred):
        def body(j, part):
            for u in range(per_q):
                part = part + _fold_rows(jnp.where(pred(key_scr[j * per_q + u]), 1, 0), jnp.add)
            return part
        part = lax.fori_loop(0, i + 1, body, jnp.zeros((8, tq), I32))
        return jnp.sum(part, axis=0, keepdims=True)

    def bit_body(bi, t_u):
        cand_u = t_u | jnp.left_shift(jnp.int32(1), 31 - bi)
        cand = cand_u ^ INT_MIN
        return jnp.where(count(lambda k: k >= cand) >= ktop, cand_u, t_u)

    thr = lax.fori_loop(0, 32, bit_body, jnp.zeros((1, tq), I32)) ^ INT_MIN
    n_ge = count(lambda k: k >= thr)

    @pl.when(jnp.max(n_ge) > ktop)
    def _():
        need = (ktop - count(lambda k: k > thr)).astype(F32)
        lower = (lax.broadcasted_iota(I32, (tk, tk), 1)
                 < lax.broadcasted_iota(I32, (tk, tk), 0)).astype(BF16)

        def body(c, seen):
            k = key_scr[c]
            eq = k == thr
            eqf = eq.astype(BF16)
            rank = seen + _dot(lower, eqf)
            drop = eq & (rank >= need) & (n_ge > ktop)
            key_scr[c] = jnp.where(drop, thr - 1, k)
            return seen + jnp.sum(eqf.astype(F32), axis=0, keepdims=True)

        lax.fori_loop(0, nchunks, body, jnp.zeros((1, tq), F32))

    def logits(c, rel):
        sel = key_scr[c] >= thr
        if rel is not None and rel >= 0:
            sel = sel & causal(rel)
        kbc = kb_ref[pl.ds(pl.multiple_of(c * tk, tk), tk), :]

        def head(hh):
            s = _dot(kbc, qbt_ref[hh])
            if rel is not None:
                s = s + bias_ref[rel + 1, hh]
            return s
        return sel, head

    m_scr[...] = jnp.full(m_scr.shape, NEG_BIG, F32)

    def max_chunk(c, rel):
        sel, head = logits(c, rel)
        for hh in range(N_HEADS):
            s = jnp.where(sel, head(hh), NEG_BIG)
            m_scr[hh] = jnp.maximum(m_scr[hh], _fold_rows(s, jnp.maximum))

    for_chunks(max_chunk)
    m = [jnp.max(m_scr[hh], axis=0, keepdims=True) for hh in range(N_HEADS)]

    acc_scr[...] = jnp.zeros(acc_scr.shape, F32)

    def att_chunk(c, rel):
        sel, head = logits(c, rel)
        vtc = vt_ref[c]
        for hh in range(N_HEADS):
            p = jnp.exp(jnp.where(sel, head(hh) - m[hh], NEG_BIG))
            acc_scr[hh] += _dot(vtc, p.astype(BF16))

    for_chunks(att_chunk)

    out_t = jnp.concatenate(
        [acc_scr[hh, :HEAD_DIM, :] / acc_scr[hh, HEAD_DIM:HEAD_DIM + 1, :] for hh in range(N_HEADS)],
        axis=0)
    o_ref[...] = out_t.T.astype(o_ref.dtype)


def _dsa_attention(qit, ki, wit, qbt, kb, vt, bias):
    b, nh, dh, s = qit.shape
    tq = DSA_QUERY_BLOCK
    tk = DSA_KEY_BLOCK
    assert tk == REL_MAX_DIST and tk == LANES and tq % tk == 0 and s % tq == 0
    ktop = min(INDEX_TOPK_MAX, s // 4)
    hmt_q = pl.BlockSpec((None, nh, dh, tq), lambda bb, i: (bb, 0, 0, i))
    full64 = pl.BlockSpec((None, s, dh), lambda bb, i: (bb, 0, 0))
    return pl.pallas_call(
        functools.partial(_dsa_kernel, ktop),
        grid=(b, s // tq),
        in_specs=[hmt_q, full64,
                  pl.BlockSpec((None, nh, tq), lambda bb, i: (bb, 0, i)),
                  hmt_q, full64,
                  pl.BlockSpec((None, s // tk, 2 * dh, tk), lambda bb, i: (bb, 0, 0, 0)),
                  pl.BlockSpec(bias.shape, lambda bb, i: (0, 0, 0, 0))],
        out_specs=pl.BlockSpec((None, tq, nh * dh), lambda bb, i: (bb, i, 0)),
        out_shape=jax.ShapeDtypeStruct((b, s, nh * dh), BF16),
        scratch_shapes=[pltpu.VMEM((s // tk, tk, tq), I32),
                        pltpu.VMEM((nh, 8, tq), F32),
                        pltpu.VMEM((nh, 2 * dh, tq), F32)],
        compiler_params=pltpu.CompilerParams(
            dimension_semantics=("parallel", "parallel"), vmem_limit_bytes=VMEM_LIMIT),
        name="sparse_attention",
    )(qit, ki, wit, qbt, kb, vt, bias)


def _first_max4(a):
    m1 = jnp.maximum(jnp.maximum(a[0], a[1]), jnp.maximum(a[2], a[3]))
    i1 = jnp.where(a[0] == m1, 0, jnp.where(a[1] == m1, 1, jnp.where(a[2] == m1, 2, 3)))
    rest = [jnp.where(i1 == j, -1.0, a[j]) for j in range(4)]
    m2 = jnp.maximum(jnp.maximum(rest[0], rest[1]), jnp.maximum(rest[2], rest[3]))
    i2 = jnp.where(rest[0] == m2, 0, jnp.where(rest[1] == m2, 1, jnp.where(rest[2] == m2, 2, 3)))
    return m1, i1, m2, i2


def _route(lt):
    lg = [lt[e:e + 1, :] for e in range(N_EXPERTS)]
    mx = functools.reduce(jnp.maximum, lg)
    ex = [jnp.exp(v - mx) for v in lg]
    tot = functools.reduce(lambda u, v: u + v, ex)
    p = [v / tot for v in ex]
    gs = []
    for g in range(N_GROUPS):
        m1, _, m2, _ = _first_max4(p[4 * g:4 * g + 4])
        gs.append(m1 + m2)
    _, gbest, _, _ = _first_max4(gs)
    chosen = [jnp.where(gbest == 0, p[j], jnp.where(gbest == 1, p[4 + j],
              jnp.where(gbest == 2, p[8 + j], p[12 + j]))) for j in range(4)]
    m1, i1, m2, i2 = _first_max4(chosen)
    den = m1 + m2
    lo = jnp.minimum(i1, i2)
    hi = jnp.maximum(i1, i2)
    pair = jnp.where(lo == 0, hi - 1, jnp.where(lo == 1, hi + 1, N_PAIRS - 1))
    first_is_lo = i1 < i2
    w1 = m1 / den
    w2 = m2 / den
    return gbest * N_PAIRS + pair, jnp.where(first_is_lo, w1, w2), jnp.where(first_is_lo, w2, w1)


def _merge_kernel(oa_ref, ob_ref, sga_ref, sgb_ref, x_ref, mod_ref, woa_ref, wob_ref, wout_ref,
                  g2_ref, rwh_ref, rwl_ref, rb_ref, x1_ref, h2x_ref, info_ref, cnt_ref, carry_scr):
    tm, d = x_ref.shape

    @pl.when((pl.program_id(0) == 0) & (pl.program_id(1) == 0))
    def _():
        carry_scr[...] = jnp.zeros(carry_scr.shape, F32)

    merged = (sga_ref[...].astype(F32) * _dot(oa_ref[...], woa_ref[...])
              + sgb_ref[...].astype(F32) * _dot(ob_ref[...], wob_ref[...]))
    x1 = x_ref[...] + mod_ref[2:3, :] * _dot(merged.astype(BF16), wout_ref[...])
    x1_ref[...] = x1
    h2 = (_rms(x1) * g2_ref[...]) * (1.0 + mod_ref[4:5, :]) + mod_ref[3:4, :]
    h2x_ref[:, :d] = h2
    hi = h2.astype(BF16)
    lo = (h2 - hi.astype(F32)).astype(BF16)
    logits = (_dot(hi, rwh_ref[...]) + _dot(lo, rwh_ref[...]) + _dot(hi, rwl_ref[...])) + rb_ref[...]
    cls, cw_lo, cw_hi = _route(logits.T)

    ncls = carry_scr.shape[0]
    onehot = (lax.broadcasted_iota(I32, (ncls, tm), 0) == cls).astype(F32)
    earlier = (lax.broadcasted_iota(I32, (tm, tm), 0)
               < lax.broadcasted_iota(I32, (tm, tm), 1)).astype(BF16)
    before = _dot(onehot.astype(BF16), earlier)
    carry = carry_scr[...]
    carry_t = jnp.concatenate([carry] * (tm // LANES), axis=1)
    rank = jnp.sum(onehot * (carry_t + before), axis=0, keepdims=True).astype(I32)
    carry = carry + jnp.sum(onehot, axis=1, keepdims=True)
    carry_scr[...] = carry
    cnt_ref[...] = carry.astype(I32)

    sub = lax.broadcasted_iota(I32, (8, tm), 0)
    info_ref[...] = jnp.where(sub == 0, cls, jnp.where(sub == 1, rank, 0))
    sub = lax.broadcasted_iota(I32, (LANES, tm), 0)
    weights_t = jnp.where(sub == 0, cw_lo, jnp.where(sub == 1, cw_hi, 0.0))
    h2x_ref[:, d:] = weights_t.T


def _merge(oa, ob, sga, sgb, x, mod_l, woa, wob, wout, g2, rwh, rwl, rb, tm):
    b, s, d = x.shape
    hw = oa.shape[-1]
    nt = s // tm
    tok = lambda n: pl.BlockSpec((None, tm, n), lambda bb, i: (bb, i, 0))
    const = lambda r, c: pl.BlockSpec((r, c), lambda bb, i: (0, 0))
    return pl.pallas_call(
        _merge_kernel,
        grid=(b, nt),
        in_specs=[tok(hw), tok(hw), tok(d), tok(d), tok(d),
                  pl.BlockSpec((None, N_MOD, d), lambda bb, i: (bb, 0, 0)),
                  const(hw, d), const(hw, d), const(d, d), const(1, d),
                  const(d, LANES), const(d, LANES), const(1, LANES)],
        out_specs=[tok(d), tok(d + LANES),
                   pl.BlockSpec((8, tm), lambda bb, i: (0, bb * nt + i)),
                   const(N_CLASS_PAD, LANES)],
        out_shape=[jax.ShapeDtypeStruct((b, s, d), F32),
                   jax.ShapeDtypeStruct((b, s, d + LANES), F32),
                   jax.ShapeDtypeStruct((8, b * s), I32),
                   jax.ShapeDtypeStruct((N_CLASS_PAD, LANES), I32)],
        scratch_shapes=[pltpu.VMEM((N_CLASS_PAD, LANES), F32)],
        compiler_params=pltpu.CompilerParams(
            dimension_semantics=("arbitrary", "arbitrary"), vmem_limit_bytes=VMEM_LIMIT),
        name="merge_route",
    )(oa, ob, sga, sgb, x, mod_l, woa, wob, wout, g2, rwh, rwl, rb)


def _row_position(offs_ref, cls_ref, rank_ref, r):
    return offs_ref[cls_ref[0, r]] + rank_ref[0, r]


def _dispatch_kernel(offs_ref, cls_ref, rank_ref, src_ref, init_ref, dst_ref, sem):
    del init_ref
    tm = cls_ref.shape[1]
    base = pl.program_id(0) * tm

    def issue(r, carry):
        p = _row_position(offs_ref, cls_ref, rank_ref, r)
        pltpu.make_async_copy(src_ref.at[pl.ds(base + r, 1), :], dst_ref.at[pl.ds(p, 1), :], sem).start()
        return carry

    lax.fori_loop(0, tm, issue, 0, unroll=8)
    pltpu.make_async_copy(src_ref.at[pl.ds(0, tm), :], dst_ref.at[pl.ds(0, tm), :], sem).wait()


def _dispatch(h2x, cls, rank, offs, n_rows):
    t, w = h2x.shape
    tm = MOE_DMA_TILE
    smem_row = pl.BlockSpec((None, 1, tm), lambda i: (i, 0, 0), memory_space=pltpu.SMEM)
    return pl.pallas_call(
        _dispatch_kernel,
        grid=(t // tm,),
        in_specs=[pl.BlockSpec(memory_space=pltpu.SMEM), smem_row, smem_row,
                  pl.BlockSpec(memory_space=pl.ANY), pl.BlockSpec(memory_space=pl.ANY)],
        out_specs=pl.BlockSpec(memory_space=pl.ANY),
        out_shape=jax.ShapeDtypeStruct((n_rows, w), F32),
        scratch_shapes=[pltpu.SemaphoreType.DMA],
        input_output_aliases={4: 0},
        compiler_params=pltpu.CompilerParams(dimension_semantics=("arbitrary",)),
        name="dispatch",
    )(offs, cls, rank, h2x, jnp.zeros((n_rows, w), F32))


def _experts_kernel(elo_ref, ehi_ref, live_ref, xs_ref, wg_lo, wg_hi, wu_lo, wu_hi, wd_lo, wd_hi, ys_ref):
    del elo_ref, ehi_ref
    d = ys_ref.shape[1]
    j = pl.program_id(0)

    @pl.when(live_ref[j] == 1)
    def _():
        h = xs_ref[:, :d].astype(BF16)

        def expert(wg, wu, wd, weight):
            gate = _dot(h, wg[...])
            hidden = ((gate * jax.nn.sigmoid(gate)) * _dot(h, wu[...])) * weight
            return _dot(hidden.astype(BF16), wd[...])

        ys_ref[...] = (expert(wg_lo, wu_lo, wd_lo, xs_ref[:, d:d + 1])
                       + expert(wg_hi, wu_hi, wd_hi, xs_ref[:, d + 1:d + 2]))

    @pl.when(live_ref[j] == 0)
    def _():
        ys_ref[...] = jnp.zeros(ys_ref.shape, F32)


def _experts(xs, e_lo, e_hi, live, wg, wu, wd):
    n_rows, w = xs.shape
    ne, d, f = wg.shape
    tm = MOE_ROW_TILE
    lo = lambda j, elo, ehi, lv: (elo[j], 0, 0)
    hi = lambda j, elo, ehi, lv: (ehi[j], 0, 0)
    grid_spec = pltpu.PrefetchScalarGridSpec(
        num_scalar_prefetch=3,
        grid=(n_rows // tm,),
        in_specs=[pl.BlockSpec((tm, w), lambda j, elo, ehi, lv: (j, 0)),
                  pl.BlockSpec((None, d, f), lo), pl.BlockSpec((None, d, f), hi),
                  pl.BlockSpec((None, d, f), lo), pl.BlockSpec((None, d, f), hi),
                  pl.BlockSpec((None, f, d), lo), pl.BlockSpec((None, f, d), hi)],
        out_specs=pl.BlockSpec((tm, d), lambda j, elo, ehi, lv: (j, 0)),
    )
    return pl.pallas_call(
        _experts_kernel,
        grid_spec=grid_spec,
        out_shape=jax.ShapeDtypeStruct((n_rows, d), F32),
        compiler_params=pltpu.CompilerParams(
            dimension_semantics=("arbitrary",), vmem_limit_bytes=VMEM_LIMIT),
        name="experts",
    )(e_lo, e_hi, live, xs, wg, wg, wu, wu, wd, wd)


def _combine_kernel(offs_ref, cls_ref, rank_ref, ys_ref, x1_ref, mod_ref, o_ref, buf, sem):
    tm = o_ref.shape[0]

    def issue(r, carry):
        p = _row_position(offs_ref, cls_ref, rank_ref, r)
        pltpu.make_async_copy(ys_ref.at[pl.ds(p, 1), :], buf.at[pl.ds(r, 1), :], sem).start()
        return carry

    lax.fori_loop(0, tm, issue, 0, unroll=8)
    pltpu.make_async_copy(ys_ref.at[pl.ds(0, tm), :], buf, sem).wait()
    o_ref[...] = x1_ref[...] + mod_ref[5:6, :] * buf[...]


def _combine(ys, cls, rank, offs, x1, mod_l):
    b, s, d = x1.shape
    tm = MOE_DMA_TILE
    nt = s // tm
    smem_row = pl.BlockSpec((None, 1, tm), lambda bb, i: (bb * nt + i, 0, 0), memory_space=pltpu.SMEM)
    return pl.pallas_call(
        _combine_kernel,
        grid=(b, nt),
        in_specs=[pl.BlockSpec(memory_space=pltpu.SMEM), smem_row, smem_row,
                  pl.BlockSpec(memory_space=pl.ANY),
                  pl.BlockSpec((None, tm, d), lambda bb, i: (bb, i, 0)),
                  pl.BlockSpec((None, N_MOD, d), lambda bb, i: (bb, 0, 0))],
        out_specs=pl.BlockSpec((None, tm, d), lambda bb, i: (bb, i, 0)),
        out_shape=jax.ShapeDtypeStruct((b, s, d), F32),
        scratch_shapes=[pltpu.VMEM((tm, d), F32), pltpu.SemaphoreType.DMA],
        compiler_params=pltpu.CompilerParams(
            dimension_semantics=("arbitrary", "arbitrary"), vmem_limit_bytes=VMEM_LIMIT),
        name="combine",
    )(offs, cls, rank, ys, x1, mod_l)


def _moe(h2x, info, counts, wg, wu, wd, x1, mod_l):
    b, s, d = x1.shape
    t = b * s
    tm = MOE_ROW_TILE
    n_tiles = t // tm + N_CLASSES
    counts = counts[:N_CLASSES, 0]
    padded = (counts + tm - 1) // tm * tm
    ends = jnp.cumsum(padded)
    offs = jnp.pad(ends - padded, (0, N_CLASS_PAD - N_CLASSES)).astype(I32)
    tile_cls = jnp.sum(jnp.arange(n_tiles, dtype=I32)[:, None] * tm >= ends[None, :], axis=1)
    live = (tile_cls < N_CLASSES).astype(I32)
    tile_cls = jnp.minimum(tile_cls, N_CLASSES - 1)
    group, pair = tile_cls // N_PAIRS, tile_cls % N_PAIRS
    pairs = jnp.asarray(_PAIRS, I32)
    e_lo = group * EXPERTS_PER_GROUP + pairs[pair, 0]
    e_hi = group * EXPERTS_PER_GROUP + pairs[pair, 1]

    nblk = t // MOE_DMA_TILE
    cls = info[0].reshape(nblk, 1, MOE_DMA_TILE)
    rank = info[1].reshape(nblk, 1, MOE_DMA_TILE)
    xs = _dispatch(h2x.reshape(t, d + LANES), cls, rank, offs, n_tiles * tm)
    ys = _experts(xs, e_lo, e_hi, live, wg, wu, wd)
    return _combine(ys, cls, rank, offs, x1, mod_l)


def kernel(x, c, w_ada, b_ada, norm1_g, w_in, q_norm_g, k_norm_g, rel_bias, w_o_a, w_o_b, w_out,
           norm2_g, router_w, router_b, w_gate, w_up, w_down):
    b, s, d = x.shape
    depth = w_ada.shape[0]
    assert d % LANES == 0
    tm = min(512, s)

    mod = _adaln(c, w_ada, b_ada)
    bias = _bias_tiles(rel_bias, DSA_QUERY_BLOCK, DSA_KEY_BLOCK)
    rw = jnp.pad(router_w, ((0, 0), (0, LANES - N_EXPERTS)))
    rwh = rw.astype(BF16)
    rwl = (rw - rwh.astype(F32)).astype(BF16)
    rb = jnp.pad(router_b, (0, LANES - N_EXPERTS)).reshape(1, LANES)

    for l in range(depth):
        qa, ka, va, qbt, qit, kb, vt, ki, wit, sga, sgb = _in_proj(
            x, mod[l], norm1_g[l].reshape(1, d), _pack_w_in(w_in[l]),
            q_norm_g[l].reshape(HEAD_DIM, 1), k_norm_g[l].reshape(1, HEAD_DIM), tm)
        oa = _sb_attention(qa, ka, va)
        ob = _dsa_attention(qit, ki, wit, qbt, kb, vt, bias)
        x1, h2x, info, counts = _merge(oa, ob, sga, sgb, x, mod[l], w_o_a[l].astype(BF16),
                                       w_o_b[l].astype(BF16), w_out[l].astype(BF16),
                                       norm2_g[l].reshape(1, d), rwh, rwl, rb, tm)
        x = _moe(h2x, info, counts, w_gate[l].astype(BF16), w_up[l].astype(BF16),
                 w_down[l].astype(BF16), x1, mod[l])
    return x
```

```python
import functools
import math

import numpy as np
import jax
import jax.numpy as jnp
from jax import lax
from jax.experimental import pallas as pl
from jax.experimental.pallas import tpu as pltpu

F32 = jnp.float32
BF16 = jnp.bfloat16
I32 = jnp.int32

HEAD_DIM = 64
N_HEADS = 8
N_EXPERTS = 16
N_GROUPS = 4
EXPERTS_PER_GROUP = 4
_PAIRS = ((0, 1), (0, 2), (0, 3), (1, 2), (1, 3), (2, 3))
N_PAIRS = len(_PAIRS)
N_CLASSES = N_GROUPS * N_PAIRS
N_CLASS_PAD = 32
MOE_ROW_TILE = 256
MOE_DMA_TILE = 512
N_MOD = 6
EPS = 1e-6
INDEX_TOPK_MAX = 256
REL_BUCKETS = 32
REL_MAX_DIST = 128
LANES = 128
DSA_QUERY_BLOCK = 256
DSA_KEY_BLOCK = 128
SB_QUERY_BLOCK = 128
SB_KEY_BLOCK = 256
LOG2E = 1.4426950408889634
NEG_BIG = -1e30
INT_MIN = -2 ** 31

VMEM_LIMIT = 56 * 1024 * 1024


def _nt_dot(a, b):
    return lax.dot_general(a, b, (((1,), (1,)), ((), ())), preferred_element_type=F32)


def _dot(a, b):
    return jnp.dot(a, b, preferred_element_type=F32)


def _mod_kernel(c_ref, w_ref, b_ref, o_ref):
    c = c_ref[...]
    sc = c * jax.nn.sigmoid(c)
    o_ref[...] = jnp.dot(sc, w_ref[...], preferred_element_type=F32,
                         precision=lax.Precision.HIGHEST) + b_ref[...]


def _adaln(c, w_ada, b_ada):
    depth, d, nd = w_ada.shape
    b = c.shape[0]
    out = pl.pallas_call(
        _mod_kernel,
        grid=(depth, nd // d),
        in_specs=[pl.BlockSpec((b, d), lambda l, j: (0, 0)),
                  pl.BlockSpec((None, d, d), lambda l, j: (l, 0, j)),
                  pl.BlockSpec((None, 1, d), lambda l, j: (l, 0, j))],
        out_specs=pl.BlockSpec((None, b, d), lambda l, j: (l, 0, j)),
        out_shape=jax.ShapeDtypeStruct((depth, b, nd), F32),
        compiler_params=pltpu.CompilerParams(vmem_limit_bytes=VMEM_LIMIT),
        name="adaln",
    )(c, w_ada, b_ada.reshape(depth, 1, nd))
    return out.reshape(depth, b, N_MOD, d)


def _rms(x):
    return x * lax.rsqrt(jnp.mean(x * x, axis=-1, keepdims=True) + EPS)


def _in_kernel(x_ref, mod_ref, g1_ref, w_ref, qng_ref, kng_ref,
               qa_ref, ka_ref, va_ref, qbt_ref, qit_ref, kb_ref, vt_ref, ki_ref, wit_ref,
               sga_ref, sgb_ref):
    d = x_ref.shape[-1]
    tm = x_ref.shape[0]
    hw = N_HEADS * HEAD_DIM
    x = x_ref[...]
    h = (_rms(x) * g1_ref[...]) * (1.0 + mod_ref[1:2, :]) + mod_ref[0:1, :]
    hb = h.astype(BF16)

    def proj(lo, n):
        return _dot(hb, w_ref[:, lo:lo + n])

    def heads(r, ref):
        for hh in range(N_HEADS):
            ref[hh] = r[:, hh * HEAD_DIM:(hh + 1) * HEAD_DIM].astype(ref.dtype)

    scale = HEAD_DIM ** -0.5
    heads(proj(0, hw) * (scale * LOG2E), qa_ref)
    heads(proj(hw, hw), ka_ref)
    heads(proj(2 * hw, hw), va_ref)

    rt = proj(3 * hw, hw).T
    for hh in range(N_HEADS):
        slab = rt[hh * HEAD_DIM:(hh + 1) * HEAD_DIM, :]
        inv = lax.rsqrt(jnp.mean(slab * slab, axis=0, keepdims=True) + EPS)
        qbt_ref[hh] = (((slab * inv) * qng_ref[...]) * scale).astype(BF16)

    r = proj(4 * hw, 2 * HEAD_DIM)
    kb_ref[...] = (_rms(r[:, :HEAD_DIM]) * kng_ref[...]).astype(BF16)
    vt = r.T[HEAD_DIM:, :]
    ones_row = (lax.broadcasted_iota(I32, (HEAD_DIM, LANES), 0) == 0).astype(BF16)
    for cc in range(tm // LANES):
        vt_ref[cc, :HEAD_DIM, :] = vt[:, cc * LANES:(cc + 1) * LANES].astype(BF16)
        vt_ref[cc, HEAD_DIM:, :] = ones_row

    rt = proj(4 * hw + LANES, hw).T
    for hh in range(N_HEADS):
        qit_ref[hh] = rt[hh * HEAD_DIM:(hh + 1) * HEAD_DIM, :].astype(BF16)

    r = proj(5 * hw + LANES, LANES)
    ki_ref[...] = r[:, :HEAD_DIM].astype(BF16)
    wit_ref[...] = r.T[HEAD_DIM:HEAD_DIM + N_HEADS, :]

    off = 5 * hw + 2 * LANES
    sga_ref[...] = jax.nn.sigmoid(proj(off, d)).astype(BF16)
    sgb_ref[...] = jax.nn.sigmoid(proj(off + d, d)).astype(BF16)


def _pack_w_in(w_in_l):
    hw = N_HEADS * HEAD_DIM
    a = 4 * hw + 2 * HEAD_DIM + hw
    small = w_in_l[:, a:a + HEAD_DIM + N_HEADS]
    small = jnp.pad(small, ((0, 0), (0, LANES - small.shape[1])))
    gates = w_in_l[:, a + HEAD_DIM + N_HEADS:]
    return jnp.concatenate([w_in_l[:, :a], small, gates], axis=1).astype(BF16)


def _in_proj(x, mod_l, g1, w_packed, qng, kng, tm):
    b, s, d = x.shape
    hm = jax.ShapeDtypeStruct((b, N_HEADS, s, HEAD_DIM), BF16)
    hmt = jax.ShapeDtypeStruct((b, N_HEADS, HEAD_DIM, s), BF16)
    tok64 = jax.ShapeDtypeStruct((b, s, HEAD_DIM), BF16)
    hm_spec = pl.BlockSpec((None, N_HEADS, tm, HEAD_DIM), lambda bb, i: (bb, 0, i, 0))
    hmt_spec = pl.BlockSpec((None, N_HEADS, HEAD_DIM, tm), lambda bb, i: (bb, 0, 0, i))
    tok = lambda n: pl.BlockSpec((None, tm, n), lambda bb, i: (bb, i, 0))
    nw = w_packed.shape[1]
    return pl.pallas_call(
        _in_kernel,
        grid=(b, s // tm),
        in_specs=[tok(d),
                  pl.BlockSpec((None, N_MOD, d), lambda bb, i: (bb, 0, 0)),
                  pl.BlockSpec((1, d), lambda bb, i: (0, 0)),
                  pl.BlockSpec((d, nw), lambda bb, i: (0, 0)),
                  pl.BlockSpec((HEAD_DIM, 1), lambda bb, i: (0, 0)),
                  pl.BlockSpec((1, HEAD_DIM), lambda bb, i: (0, 0))],
        out_specs=[hm_spec] * 3 + [hmt_spec] * 2
                  + [tok(HEAD_DIM),
                     pl.BlockSpec((None, tm // LANES, 2 * HEAD_DIM, LANES), lambda bb, i: (bb, i, 0, 0)),
                     tok(HEAD_DIM),
                     pl.BlockSpec((None, N_HEADS, tm), lambda bb, i: (bb, 0, i)),
                     tok(d), tok(d)],
        out_shape=[hm] * 3 + [hmt] * 2
                  + [tok64,
                     jax.ShapeDtypeStruct((b, s // LANES, 2 * HEAD_DIM, LANES), BF16),
                     tok64,
                     jax.ShapeDtypeStruct((b, N_HEADS, s), F32),
                     jax.ShapeDtypeStruct((b, s, d), BF16),
                     jax.ShapeDtypeStruct((b, s, d), BF16)],
        compiler_params=pltpu.CompilerParams(
            dimension_semantics=("parallel", "parallel"), vmem_limit_bytes=VMEM_LIMIT),
        name="in_proj",
    )(x, mod_l, g1, w_packed, qng, kng)


def _sb_kernel(q_ref, k_ref, v_ref, o_ref, acc_scr, csum_scr):
    tq = q_ref.shape[1]
    tk = SB_KEY_BLOCK
    i = pl.program_id(1)
    nfull = (i * tq) // tk
    row = lax.broadcasted_iota(I32, (tq, tk), 0)
    col = lax.broadcasted_iota(I32, (tq, tk), 1)
    before = (col - row) < (i * tq - nfull * tk)
    krow = lax.broadcasted_iota(I32, (tk, tk), 0)
    kcol = lax.broadcasted_iota(I32, (tk, tk), 1)
    suffix = (krow > kcol).astype(BF16)

    def block(q, ks, vs, csum, diag):
        z2 = _nt_dot(q, ks)
        sp2 = jnp.maximum(z2, 0.0) + jnp.log2(1.0 + jnp.exp2(-jnp.abs(z2)))
        if diag:
            sp2 = jnp.where(before, sp2, 0.0)
        suf = _dot(sp2.astype(BF16), suffix)
        a = jnp.exp2((z2 - sp2) - suf - csum)
        if diag:
            a = jnp.where(before, a, 0.0)
        contrib = _dot(a.astype(BF16), vs)
        return contrib, csum + jnp.sum(sp2, axis=-1, keepdims=True)

    start = pl.multiple_of(nfull * tk, tk)
    for hh in range(N_HEADS):
        acc, csum = block(q_ref[hh], k_ref[hh, pl.ds(start, tk), :], v_ref[hh, pl.ds(start, tk), :],
                          jnp.zeros((tq, 1), F32), True)
        acc_scr[hh] = acc
        csum_scr[hh] = csum

    def body(jj, carry):
        st = pl.multiple_of((nfull - 1 - jj) * tk, tk)
        for hh in range(N_HEADS):
            c, csum = block(q_ref[hh], k_ref[hh, pl.ds(st, tk), :], v_ref[hh, pl.ds(st, tk), :],
                            csum_scr[hh], False)
            acc_scr[hh] += c
            csum_scr[hh] = csum
        return carry

    lax.fori_loop(0, nfull, body, 0)
    for hh in range(N_HEADS):
        o_ref[:, hh * HEAD_DIM:(hh + 1) * HEAD_DIM] = acc_scr[hh].astype(o_ref.dtype)


def _sb_attention(qa, ka, va):
    b, nh, s, dh = qa.shape
    tq = SB_QUERY_BLOCK
    assert SB_KEY_BLOCK % tq == 0 and s % SB_KEY_BLOCK == 0
    return pl.pallas_call(
        _sb_kernel,
        grid=(b, s // tq),
        in_specs=[pl.BlockSpec((None, nh, tq, dh), lambda bb, i: (bb, 0, i, 0)),
                  pl.BlockSpec((None, nh, s, dh), lambda bb, i: (bb, 0, 0, 0)),
                  pl.BlockSpec((None, nh, s, dh), lambda bb, i: (bb, 0, 0, 0))],
        out_specs=pl.BlockSpec((None, tq, nh * dh), lambda bb, i: (bb, i, 0)),
        out_shape=jax.ShapeDtypeStruct((b, s, nh * dh), BF16),
        scratch_shapes=[pltpu.VMEM((nh, tq, dh), F32), pltpu.VMEM((nh, tq, 1), F32)],
        compiler_params=pltpu.CompilerParams(
            dimension_semantics=("parallel", "parallel"), vmem_limit_bytes=VMEM_LIMIT),
        name="stick_breaking",
    )(qa, ka, va)


def _t5_bucket_np(dist):
    max_exact = REL_BUCKETS // 2
    d_f = np.maximum(dist, 1).astype(np.float32)
    large = max_exact + (np.log(d_f / np.float32(max_exact))
                         / np.float32(math.log(REL_MAX_DIST / max_exact))
                         * np.float32(REL_BUCKETS - max_exact)).astype(np.int32)
    large = np.minimum(large, REL_BUCKETS - 1)
    return np.where(dist < max_exact, dist, large).astype(np.int32)


def _bias_tiles(rel_bias, tq, tk):
    n_near = tq // tk + 1
    key = np.arange(tk)[:, None]
    query = np.arange(tq)[None, :]
    bk = np.stack([_t5_bucket_np(np.maximum(query - key - (n - 1) * tk, 0)) for n in range(n_near)])
    far = int(_t5_bucket_np(np.array([REL_MAX_DIST]))[0])
    nh = rel_bias.shape[1]

    def body(rb_ref, bk_ref, o_ref):
        hh = pl.program_id(1)
        buckets = bk_ref[...]
        tile = jnp.zeros((tk, tq), F32)
        for bucket in range(REL_BUCKETS):
            tile = jnp.where(buckets == bucket, rb_ref[hh, bucket] - rb_ref[hh, far], tile)
        o_ref[...] = tile

    return pl.pallas_call(
        body,
        grid=(n_near, nh),
        in_specs=[pl.BlockSpec(memory_space=pltpu.SMEM),
                  pl.BlockSpec((None, tk, tq), lambda n, hh: (n, 0, 0))],
        out_specs=pl.BlockSpec((None, None, tk, tq), lambda n, hh: (n, hh, 0, 0)),
        out_shape=jax.ShapeDtypeStruct((n_near, nh, tk, tq), F32),
        name="bias_tiles",
    )(rel_bias.T, jnp.asarray(bk))


def _sort_key(x):
    bits = pltpu.bitcast(x, I32)
    return bits ^ ((bits >> 31) & 0x7FFFFFFF)


def _fold_rows(x, op):
    return functools.reduce(op, [x[g * 8:(g + 1) * 8, :] for g in range(x.shape[0] // 8)])


def _dsa_kernel(ktop, qit_ref, ki_ref, wit_ref, qbt_ref, kb_ref, vt_ref, bias_ref, o_ref,
                key_scr, m_scr, acc_scr):
    tq = qit_ref.shape[2]
    tk = DSA_KEY_BLOCK
    per_q = tq // tk
    i = pl.program_id(1)
    first_own = per_q * i
    nchunks = per_q * (i + 1)
    key_i = lax.broadcasted_iota(I32, (tk, tq), 0)
    qry_i = lax.broadcasted_iota(I32, (tk, tq), 1)

    def causal(rel):
        return (key_i - qry_i) <= (-rel * tk)

    def for_chunks(fn):
        def far(j, carry):
            for u in range(per_q):
                fn(j * per_q + u, None)
            return carry
        lax.fori_loop(0, i - 1, far, 0)

        @pl.when(i >= 1)
        def _():
            for rel in range(-per_q, -1):
                fn(first_own + rel, None)
            fn(first_own - 1, -1)

        for rel in range(per_q):
            fn(first_own + rel, rel)

    def score_chunk(c, rel):
        kic = ki_ref[pl.ds(pl.multiple_of(c * tk, tk), tk), :]
        score = jnp.zeros((tk, tq), F32)
        for hh in range(N_HEADS):
            score = score + wit_ref[hh:hh + 1, :] * jnp.maximum(_dot(kic, qit_ref[hh]), 0.0)
        keys = _sort_key(score)
        if rel is not None and rel >= 0:
            keys = jnp.where(causal(rel), keys, INT_MIN)
        key_scr[c] = keys

    for_chunks(score_chunk)

    def count(pred):
        def body(j, part):
            for u in range(per_q):
                part = part + _fold_rows(jnp.where(pred(key_scr[j * per_q + u]), 1, 0), jnp.add)
            return part
        part = lax.fori_loop(0, i + 1, body, jnp.zeros((8, tq), I32))
        return jnp.sum(part, axis=0, keepdims=True)

    def bit_body(bi, t_u):
        cand_u = t_u | jnp.left_shift(jnp.int32(1), 31 - bi)
        cand = cand_u ^ INT_MIN
        return jnp.where(count(lambda k: k >= cand) >= ktop, cand_u, t_u)

    thr = lax.fori_loop(0, 32, bit_body, jnp.zeros((1, tq), I32)) ^ INT_MIN
    n_ge = count(lambda k: k >= thr)

    @pl.when(jnp.max(n_ge) > ktop)
    def _():
        need = (ktop - count(lambda k: k > thr)).astype(F32)
        lower = (lax.broadcasted_iota(I32, (tk, tk), 1)
                 < lax.broadcasted_iota(I32, (tk, tk), 0)).astype(BF16)

        def body(c, seen):
            k = key_scr[c]
            eq = k == thr
            eqf = eq.astype(BF16)
            rank = seen + _dot(lower, eqf)
            drop = eq & (rank >= need) & (n_ge > ktop)
            key_scr[c] = jnp.where(drop, thr - 1, k)
            return seen + jnp.sum(eqf.astype(F32), axis=0, keepdims=True)

        lax.fori_loop(0, nchunks, body, jnp.zeros((1, tq), F32))

    def logits(c, rel):
        sel = key_scr[c] >= thr
        if rel is not None and rel >= 0:
            sel = sel & causal(rel)
        kbc = kb_ref[pl.ds(pl.multiple_of(c * tk, tk), tk), :]

        def head(hh):
            s = _dot(kbc, qbt_ref[hh])
            if rel is not None:
                s = s + bias_ref[rel + 1, hh]
            return s
        return sel, head

    m_scr[...] = jnp.full(m_scr.shape, NEG_BIG, F32)

    def max_chunk(c, rel):
        sel, head = logits(c, rel)
        for hh in range(N_HEADS):
            s = jnp.where(sel, head(hh), NEG_BIG)
            m_scr[hh] = jnp.maximum(m_scr[hh], _fold_rows(s, jnp.maximum))

    for_chunks(max_chunk)
    m = [jnp.max(m_scr[hh], axis=0, keepdims=True) for hh in range(N_HEADS)]

    acc_scr[...] = jnp.zeros(acc_scr.shape, F32)

    def att_chunk(c, rel):
        sel, head = logits(c, rel)
        vtc = vt_ref[c]
        for hh in range(N_HEADS):
            p = jnp.exp(jnp.where(sel, head(hh) - m[hh], NEG_BIG))
            acc_scr[hh] += _dot(vtc, p.astype(BF16))

    for_chunks(att_chunk)

    out_t = jnp.concatenate(
        [acc_scr[hh, :HEAD_DIM, :] / acc_scr[hh, HEAD_DIM:HEAD_DIM + 1, :] for hh in range(N_HEADS)],
        axis=0)
    o_ref[...] = out_t.T.astype(o_ref.dtype)


def _dsa_attention(qit, ki, wit, qbt, kb, vt, bias):
    b, nh, dh, s = qit.shape
    tq = DSA_QUERY_BLOCK
    tk = DSA_KEY_BLOCK
    assert tk == REL_MAX_DIST and tk == LANES and tq % tk == 0 and s % tq == 0
    ktop = min(INDEX_TOPK_MAX, s // 4)
    hmt_q = pl.BlockSpec((None, nh, dh, tq), lambda bb, i: (bb, 0, 0, i))
    full64 = pl.BlockSpec((None, s, dh), lambda bb, i: (bb, 0, 0))
    return pl.pallas_call(
        functools.partial(_dsa_kernel, ktop),
        grid=(b, s // tq),
        in_specs=[hmt_q, full64,
                  pl.BlockSpec((None, nh, tq), lambda bb, i: (bb, 0, i)),
                  hmt_q, full64,
                  pl.BlockSpec((None, s // tk, 2 * dh, tk), lambda bb, i: (bb, 0, 0, 0)),
                  pl.BlockSpec(bias.shape, lambda bb, i: (0, 0, 0, 0))],
        out_specs=pl.BlockSpec((None, tq, nh * dh), lambda bb, i: (bb, i, 0)),
        out_shape=jax.ShapeDtypeStruct((b, s, nh * dh), BF16),
        scratch_shapes=[pltpu.VMEM((s // tk, tk, tq), I32),
                        pltpu.VMEM((nh, 8, tq), F32),
                        pltpu.VMEM((nh, 2 * dh, tq), F32)],
        compiler_params=pltpu.CompilerParams(
            dimension_semantics=("parallel", "parallel"), vmem_limit_bytes=VMEM_LIMIT),
        name="sparse_attention",
    )(qit, ki, wit, qbt, kb, vt, bias)


def _first_max4(a):
    m1 = jnp.maximum(jnp.maximum(a[0], a[1]), jnp.maximum(a[2], a[3]))
    i1 = jnp.where(a[0] == m1, 0, jnp.where(a[1] == m1, 1, jnp.where(a[2] == m1, 2, 3)))
    rest = [jnp.where(i1 == j, -1.0, a[j]) for j in range(4)]
    m2 = jnp.maximum(jnp.maximum(rest[0], rest[1]), jnp.maximum(rest[2], rest[3]))
    i2 = jnp.where(rest[0] == m2, 0, jnp.where(rest[1] == m2, 1, jnp.where(rest[2] == m2, 2, 3)))
    return m1, i1, m2, i2


def _route(lt):
    lg = [lt[e:e + 1, :] for e in range(N_EXPERTS)]
    mx = functools.reduce(jnp.maximum, lg)
    ex = [jnp.exp(v - mx) for v in lg]
    tot = functools.reduce(lambda u, v: u + v, ex)
    p = [v / tot for v in ex]
    gs = []
    for g in range(N_GROUPS):
        m1, _, m2, _ = _first_max4(p[4 * g:4 * g + 4])
        gs.append(m1 + m2)
    _, gbest, _, _ = _first_max4(gs)
    chosen = [jnp.where(gbest == 0, p[j], jnp.where(gbest == 1, p[4 + j],
              jnp.where(gbest == 2, p[8 + j], p[12 + j]))) for j in range(4)]
    m1, i1, m2, i2 = _first_max4(chosen)
    den = m1 + m2
    lo = jnp.minimum(i1, i2)
    hi = jnp.maximum(i1, i2)
    pair = jnp.where(lo == 0, hi - 1, jnp.where(lo == 1, hi + 1, N_PAIRS - 1))
    first_is_lo = i1 < i2
    w1 = m1 / den
    w2 = m2 / den
    return gbest * N_PAIRS + pair, jnp.where(first_is_lo, w1, w2), jnp.where(first_is_lo, w2, w1)


def _merge_kernel(oa_ref, ob_ref, sga_ref, sgb_ref, x_ref, mod_ref, woa_ref, wob_ref, wout_ref,
                  g2_ref, rwh_ref, rwl_ref, rb_ref, x1_ref, h2x_ref, info_ref, cnt_ref, carry_scr):
    tm, d = x_ref.shape

    @pl.when((pl.program_id(0) == 0) & (pl.program_id(1) == 0))
    def _():
        carry_scr[...] = jnp.zeros(carry_scr.shape, F32)

    merged = (sga_ref[...].astype(F32) * _dot(oa_ref[...], woa_ref[...])
              + sgb_ref[...].astype(F32) * _dot(ob_ref[...], wob_ref[...]))
    x1 = x_ref[...] + mod_ref[2:3, :] * _dot(merged.astype(BF16), wout_ref[...])
    x1_ref[...] = x1
    h2 = (_rms(x1) * g2_ref[...]) * (1.0 + mod_ref[4:5, :]) + mod_ref[3:4, :]
    h2x_ref[:, :d] = h2
    hi = h2.astype(BF16)
    lo = (h2 - hi.astype(F32)).astype(BF16)
    logits = (_dot(hi, rwh_ref[...]) + _dot(lo, rwh_ref[...]) + _dot(hi, rwl_ref[...])) + rb_ref[...]
    cls, cw_lo, cw_hi = _route(logits.T)

    ncls = carry_scr.shape[0]
    onehot = (lax.broadcasted_iota(I32, (ncls, tm), 0) == cls).astype(F32)
    earlier = (lax.broadcasted_iota(I32, (tm, tm), 0)
               < lax.broadcasted_iota(I32, (tm, tm), 1)).astype(BF16)
    before = _dot(onehot.astype(BF16), earlier)
    carry = carry_scr[...]
    carry_t = jnp.concatenate([carry] * (tm // LANES), axis=1)
    rank = jnp.sum(onehot * (carry_t + before), axis=0, keepdims=True).astype(I32)
    carry = carry + jnp.sum(onehot, axis=1, keepdims=True)
    carry_scr[...] = carry
    cnt_ref[...] = carry.astype(I32)

    sub = lax.broadcasted_iota(I32, (8, tm), 0)
    info_ref[...] = jnp.where(sub == 0, cls, jnp.where(sub == 1, rank, 0))
    sub = lax.broadcasted_iota(I32, (LANES, tm), 0)
    weights_t = jnp.where(sub == 0, cw_lo, jnp.where(sub == 1, cw_hi, 0.0))
    h2x_ref[:, d:] = weights_t.T


def _merge(oa, ob, sga, sgb, x, mod_l, woa, wob, wout, g2, rwh, rwl, rb, tm):
    b, s, d = x.shape
    hw = oa.shape[-1]
    nt = s // tm
    tok = lambda n: pl.BlockSpec((None, tm, n), lambda bb, i: (bb, i, 0))
    const = lambda r, c: pl.BlockSpec((r, c), lambda bb, i: (0, 0))
    return pl.pallas_call(
        _merge_kernel,
        grid=(b, nt),
        in_specs=[tok(hw), tok(hw), tok(d), tok(d), tok(d),
                  pl.BlockSpec((None, N_MOD, d), lambda bb, i: (bb, 0, 0)),
                  const(hw, d), const(hw, d), const(d, d), const(1, d),
                  const(d, LANES), const(d, LANES), const(1, LANES)],
        out_specs=[tok(d), tok(d + LANES),
                   pl.BlockSpec((8, tm), lambda bb, i: (0, bb * nt + i)),
                   const(N_CLASS_PAD, LANES)],
        out_shape=[jax.ShapeDtypeStruct((b, s, d), F32),
                   jax.ShapeDtypeStruct((b, s, d + LANES), F32),
                   jax.ShapeDtypeStruct((8, b * s), I32),
                   jax.ShapeDtypeStruct((N_CLASS_PAD, LANES), I32)],
        scratch_shapes=[pltpu.VMEM((N_CLASS_PAD, LANES), F32)],
        compiler_params=pltpu.CompilerParams(
            dimension_semantics=("arbitrary", "arbitrary"), vmem_limit_bytes=VMEM_LIMIT),
        name="merge_route",
    )(oa, ob, sga, sgb, x, mod_l, woa, wob, wout, g2, rwh, rwl, rb)


def _row_position(offs_ref, cls_ref, rank_ref, r):
    return offs_ref[cls_ref[0, r]] + rank_ref[0, r]


def _dispatch_kernel(offs_ref, cls_ref, rank_ref, src_ref, init_ref, dst_ref, sem):
    del init_ref
    tm = cls_ref.shape[1]

    def issue(r, carry):
        p = _row_position(offs_ref, cls_ref, rank_ref, r)
        pltpu.make_async_copy(src_ref.at[pl.ds(r, 1), :], dst_ref.at[pl.ds(p, 1), :], sem).start()
        return carry

    lax.fori_loop(0, tm, issue, 0, unroll=8)
    pltpu.make_async_copy(src_ref, dst_ref.at[pl.ds(0, tm), :], sem).wait()


def _dispatch(h2x, cls, rank, offs, n_rows):
    t, w = h2x.shape
    tm = MOE_DMA_TILE
    smem_row = pl.BlockSpec((None, 1, tm), lambda i: (i, 0, 0), memory_space=pltpu.SMEM)
    return pl.pallas_call(
        _dispatch_kernel,
        grid=(t // tm,),
        in_specs=[pl.BlockSpec(memory_space=pltpu.SMEM), smem_row, smem_row,
                  pl.BlockSpec((tm, w), lambda i: (i, 0)), pl.BlockSpec(memory_space=pl.ANY)],
        out_specs=pl.BlockSpec(memory_space=pl.ANY),
        out_shape=jax.ShapeDtypeStruct((n_rows, w), F32),
        scratch_shapes=[pltpu.SemaphoreType.DMA],
        input_output_aliases={4: 0},
        compiler_params=pltpu.CompilerParams(
            dimension_semantics=("arbitrary",), vmem_limit_bytes=VMEM_LIMIT),
        name="dispatch",
    )(offs, cls, rank, h2x, jnp.zeros((n_rows, w), F32))


def _experts_kernel(elo_ref, ehi_ref, live_ref, xs_ref, wg_lo, wg_hi, wu_lo, wu_hi, wd_lo, wd_hi, ys_ref):
    del elo_ref, ehi_ref
    d = ys_ref.shape[1]
    j = pl.program_id(0)

    @pl.when(live_ref[j] == 1)
    def _():
        h = xs_ref[:, :d].astype(BF16)

        def expert(wg, wu, wd, weight):
            gate = _dot(h, wg[...])
            hidden = ((gate * jax.nn.sigmoid(gate)) * _dot(h, wu[...])) * weight
            return _dot(hidden.astype(BF16), wd[...])

        ys_ref[...] = (expert(wg_lo, wu_lo, wd_lo, xs_ref[:, d:d + 1])
                       + expert(wg_hi, wu_hi, wd_hi, xs_ref[:, d + 1:d + 2]))

    @pl.when(live_ref[j] == 0)
    def _():
        ys_ref[...] = jnp.zeros(ys_ref.shape, F32)


def _experts(xs, e_lo, e_hi, live, wg, wu, wd):
    n_rows, w = xs.shape
    ne, d, f = wg.shape
    tm = MOE_ROW_TILE
    lo = lambda j, elo, ehi, lv: (elo[j], 0, 0)
    hi = lambda j, elo, ehi, lv: (ehi[j], 0, 0)
    grid_spec = pltpu.PrefetchScalarGridSpec(
        num_scalar_prefetch=3,
        grid=(n_rows // tm,),
        in_specs=[pl.BlockSpec((tm, w), lambda j, elo, ehi, lv: (j, 0)),
                  pl.BlockSpec((None, d, f), lo), pl.BlockSpec((None, d, f), hi),
                  pl.BlockSpec((None, d, f), lo), pl.BlockSpec((None, d, f), hi),
                  pl.BlockSpec((None, f, d), lo), pl.BlockSpec((None, f, d), hi)],
        out_specs=pl.BlockSpec((tm, d), lambda j, elo, ehi, lv: (j, 0)),
    )
    return pl.pallas_call(
        _experts_kernel,
        grid_spec=grid_spec,
        out_shape=jax.ShapeDtypeStruct((n_rows, d), F32),
        compiler_params=pltpu.CompilerParams(
            dimension_semantics=("arbitrary",), vmem_limit_bytes=VMEM_LIMIT),
        name="experts",
    )(e_lo, e_hi, live, xs, wg, wg, wu, wu, wd, wd)


def _combine_kernel(offs_ref, cls_ref, rank_ref, ys_ref, x1_ref, mod_ref, o_ref, buf, sem):
    tm = o_ref.shape[0]

    def issue(r, carry):
        p = _row_position(offs_ref, cls_ref, rank_ref, r)
        pltpu.make_async_copy(ys_ref.at[pl.ds(p, 1), :], buf.at[pl.ds(r, 1), :], sem).start()
        return carry

    lax.fori_loop(0, tm, issue, 0, unroll=8)
    pltpu.make_async_copy(ys_ref.at[pl.ds(0, tm), :], buf, sem).wait()
    o_ref[...] = x1_ref[...] + mod_ref[5:6, :] * buf[...]


def _combine(ys, cls, rank, offs, x1, mod_l):
    b, s, d = x1.shape
    tm = MOE_DMA_TILE
    nt = s // tm
    smem_row = pl.BlockSpec((None, 1, tm), lambda bb, i: (bb * nt + i, 0, 0), memory_space=pltpu.SMEM)
    return pl.pallas_call(
        _combine_kernel,
        grid=(b, nt),
        in_specs=[pl.BlockSpec(memory_space=pltpu.SMEM), smem_row, smem_row,
                  pl.BlockSpec(memory_space=pl.ANY),
                  pl.BlockSpec((None, tm, d), lambda bb, i: (bb, i, 0)),
                  pl.BlockSpec((None, N_MOD, d), lambda bb, i: (bb, 0, 0))],
        out_specs=pl.BlockSpec((None, tm, d), lambda bb, i: (bb, i, 0)),
        out_shape=jax.ShapeDtypeStruct((b, s, d), F32),
        scratch_shapes=[pltpu.VMEM((tm, d), F32), pltpu.SemaphoreType.DMA],
        compiler_params=pltpu.CompilerParams(
            dimension_semantics=("arbitrary", "arbitrary"), vmem_limit_bytes=VMEM_LIMIT),
        name="combine",
    )(offs, cls, rank, ys, x1, mod_l)


def _moe(h2x, info, counts, wg, wu, wd, x1, mod_l):
    b, s, d = x1.shape
    t = b * s
    tm = MOE_ROW_TILE
    n_tiles = t // tm + N_CLASSES
    counts = counts[:N_CLASSES, 0]
    padded = (counts + tm - 1) // tm * tm
    ends = jnp.cumsum(padded)
    offs = jnp.pad(ends - padded, (0, N_CLASS_PAD - N_CLASSES)).astype(I32)
    tile_cls = jnp.sum(jnp.arange(n_tiles, dtype=I32)[:, None] * tm >= ends[None, :], axis=1)
    live = (tile_cls < N_CLASSES).astype(I32)
    tile_cls = jnp.minimum(tile_cls, N_CLASSES - 1)
    group, pair = tile_cls // N_PAIRS, tile_cls % N_PAIRS
    pairs = jnp.asarray(_PAIRS, I32)
    e_lo = group * EXPERTS_PER_GROUP + pairs[pair, 0]
    e_hi = group * EXPERTS_PER_GROUP + pairs[pair, 1]

    nblk = t // MOE_DMA_TILE
    cls = info[0].reshape(nblk, 1, MOE_DMA_TILE)
    rank = info[1].reshape(nblk, 1, MOE_DMA_TILE)
    xs = _dispatch(h2x.reshape(t, d + LANES), cls, rank, offs, n_tiles * tm)
    ys = _experts(xs, e_lo, e_hi, live, wg, wu, wd)
    return _combine(ys, cls, rank, offs, x1, mod_l)


def kernel(x, c, w_ada, b_ada, norm1_g, w_in, q_norm_g, k_norm_g, rel_bias, w_o_a, w_o_b, w_out,
           norm2_g, router_w, router_b, w_gate, w_up, w_down):
    b, s, d = x.shape
    depth = w_ada.shape[0]
    assert d % LANES == 0
    tm = min(512, s)

    mod = _adaln(c, w_ada, b_ada)
    bias = _bias_tiles(rel_bias, DSA_QUERY_BLOCK, DSA_KEY_BLOCK)
    rw = jnp.pad(router_w, ((0, 0), (0, LANES - N_EXPERTS)))
    rwh = rw.astype(BF16)
    rwl = (rw - rwh.astype(F32)).astype(BF16)
    rb = jnp.pad(router_b, (0, LANES - N_EXPERTS)).reshape(1, LANES)

    for l in range(depth):
        qa, ka, va, qbt, qit, kb, vt, ki, wit, sga, sgb = _in_proj(
            x, mod[l], norm1_g[l].reshape(1, d), _pack_w_in(w_in[l]),
            q_norm_g[l].reshape(HEAD_DIM, 1), k_norm_g[l].reshape(1, HEAD_DIM), tm)
        oa = _sb_attention(qa, ka, va)
        ob = _dsa_attention(qit, ki, wit, qbt, kb, vt, bias)
        x1, h2x, info, counts = _merge(oa, ob, sga, sgb, x, mod[l], w_o_a[l].astype(BF16),
                                       w_o_b[l].astype(BF16), w_out[l].astype(BF16),
                                       norm2_g[l].reshape(1, d), rwh, rwl, rb, tm)
        x = _moe(h2x, info, counts, w_gate[l].astype(BF16), w_up[l].astype(BF16),
                 w_down[l].astype(BF16), x1, mod[l])
    return x
```

```python
import functools
import math

import numpy as np
import jax
import jax.numpy as jnp
from jax import lax
from jax.experimental import pallas as pl
from jax.experimental.pallas import tpu as pltpu

F32 = jnp.float32
BF16 = jnp.bfloat16
I32 = jnp.int32

HEAD_DIM = 64
N_HEADS = 8
N_EXPERTS = 16
N_GROUPS = 4
EXPERTS_PER_GROUP = 4
_PAIRS = ((0, 1), (0, 2), (0, 3), (1, 2), (1, 3), (2, 3))
N_PAIRS = len(_PAIRS)
N_CLASSES = N_GROUPS * N_PAIRS
N_CLASS_PAD = 32
MOE_ROW_TILE = 256
MOE_DMA_TILE = 512
N_MOD = 6
EPS = 1e-6
INDEX_TOPK_MAX = 256
REL_BUCKETS = 32
REL_MAX_DIST = 128
LANES = 128
DSA_QUERY_BLOCK = 256
DSA_KEY_BLOCK = 128
SB_QUERY_BLOCK = 256
SB_KEY_BLOCK = 256
LOG2E = 1.4426950408889634
NEG_BIG = -1e30
INT_MIN = -2 ** 31

VMEM_LIMIT = 56 * 1024 * 1024


def _nt_dot(a, b):
    return lax.dot_general(a, b, (((1,), (1,)), ((), ())), preferred_element_type=F32)


def _dot(a, b):
    return jnp.dot(a, b, preferred_element_type=F32)


def _mod_kernel(c_ref, w_ref, b_ref, o_ref):
    c = c_ref[...]
    sc = c * jax.nn.sigmoid(c)
    o_ref[...] = jnp.dot(sc, w_ref[...], preferred_element_type=F32,
                         precision=lax.Precision.HIGHEST) + b_ref[...]


def _adaln(c, w_ada, b_ada):
    depth, d, nd = w_ada.shape
    b = c.shape[0]
    out = pl.pallas_call(
        _mod_kernel,
        grid=(depth, nd // d),
        in_specs=[pl.BlockSpec((b, d), lambda l, j: (0, 0)),
                  pl.BlockSpec((None, d, d), lambda l, j: (l, 0, j)),
                  pl.BlockSpec((None, 1, d), lambda l, j: (l, 0, j))],
        out_specs=pl.BlockSpec((None, b, d), lambda l, j: (l, 0, j)),
        out_shape=jax.ShapeDtypeStruct((depth, b, nd), F32),
        compiler_params=pltpu.CompilerParams(vmem_limit_bytes=VMEM_LIMIT),
        name="adaln",
    )(c, w_ada, b_ada.reshape(depth, 1, nd))
    return out.reshape(depth, b, N_MOD, d)


def _rms(x):
    return x * lax.rsqrt(jnp.mean(x * x, axis=-1, keepdims=True) + EPS)


def _in_kernel(x_ref, mod_ref, g1_ref, w_ref, qng_ref, kng_ref,
               qa_ref, ka_ref, va_ref, qbt_ref, qit_ref, kb_ref, vt_ref, ki_ref, wit_ref,
               sga_ref, sgb_ref):
    d = x_ref.shape[-1]
    tm = x_ref.shape[0]
    hw = N_HEADS * HEAD_DIM
    x = x_ref[...]
    h = (_rms(x) * g1_ref[...]) * (1.0 + mod_ref[1:2, :]) + mod_ref[0:1, :]
    hb = h.astype(BF16)

    def proj(lo, n):
        return _dot(hb, w_ref[:, lo:lo + n])

    def heads(r, ref):
        for hh in range(N_HEADS):
            ref[hh] = r[:, hh * HEAD_DIM:(hh + 1) * HEAD_DIM].astype(ref.dtype)

    scale = HEAD_DIM ** -0.5
    heads(proj(0, hw) * (scale * LOG2E), qa_ref)
    heads(proj(hw, hw), ka_ref)
    heads(proj(2 * hw, hw), va_ref)

    rt = proj(3 * hw, hw).T
    for hh in range(N_HEADS):
        slab = rt[hh * HEAD_DIM:(hh + 1) * HEAD_DIM, :]
        inv = lax.rsqrt(jnp.mean(slab * slab, axis=0, keepdims=True) + EPS)
        qbt_ref[hh] = (((slab * inv) * qng_ref[...]) * scale).astype(BF16)

    r = proj(4 * hw, 2 * HEAD_DIM)
    kb_ref[...] = (_rms(r[:, :HEAD_DIM]) * kng_ref[...]).astype(BF16)
    vt = r.T[HEAD_DIM:, :]
    ones_row = (lax.broadcasted_iota(I32, (HEAD_DIM, LANES), 0) == 0).astype(BF16)
    for cc in range(tm // LANES):
        vt_ref[cc, :HEAD_DIM, :] = vt[:, cc * LANES:(cc + 1) * LANES].astype(BF16)
        vt_ref[cc, HEAD_DIM:, :] = ones_row

    rt = proj(4 * hw + LANES, hw).T
    for hh in range(N_HEADS):
        qit_ref[hh] = rt[hh * HEAD_DIM:(hh + 1) * HEAD_DIM, :].astype(BF16)

    r = proj(5 * hw + LANES, LANES)
    ki_ref[...] = r[:, :HEAD_DIM].astype(BF16)
    wit_ref[...] = r.T[HEAD_DIM:HEAD_DIM + N_HEADS, :]

    off = 5 * hw + 2 * LANES
    sga_ref[...] = jax.nn.sigmoid(proj(off, d)).astype(BF16)
    sgb_ref[...] = jax.nn.sigmoid(proj(off + d, d)).astype(BF16)


def _pack_w_in(w_in_l):
    hw = N_HEADS * HEAD_DIM
    a = 4 * hw + 2 * HEAD_DIM + hw
    small = w_in_l[:, a:a + HEAD_DIM + N_HEADS]
    small = jnp.pad(small, ((0, 0), (0, LANES - small.shape[1])))
    gates = w_in_l[:, a + HEAD_DIM + N_HEADS:]
    return jnp.concatenate([w_in_l[:, :a], small, gates], axis=1).astype(BF16)


def _in_proj(x, mod_l, g1, w_packed, qng, kng, tm):
    b, s, d = x.shape
    hm = jax.ShapeDtypeStruct((b, N_HEADS, s, HEAD_DIM), BF16)
    hmt = jax.ShapeDtypeStruct((b, N_HEADS, HEAD_DIM, s), BF16)
    tok64 = jax.ShapeDtypeStruct((b, s, HEAD_DIM), BF16)
    hm_spec = pl.BlockSpec((None, N_HEADS, tm, HEAD_DIM), lambda bb, i: (bb, 0, i, 0))
    hmt_spec = pl.BlockSpec((None, N_HEADS, HEAD_DIM, tm), lambda bb, i: (bb, 0, 0, i))
    tok = lambda n: pl.BlockSpec((None, tm, n), lambda bb, i: (bb, i, 0))
    nw = w_packed.shape[1]
    return pl.pallas_call(
        _in_kernel,
        grid=(b, s // tm),
        in_specs=[tok(d),
                  pl.BlockSpec((None, N_MOD, d), lambda bb, i: (bb, 0, 0)),
                  pl.BlockSpec((1, d), lambda bb, i: (0, 0)),
                  pl.BlockSpec((d, nw), lambda bb, i: (0, 0)),
                  pl.BlockSpec((HEAD_DIM, 1), lambda bb, i: (0, 0)),
                  pl.BlockSpec((1, HEAD_DIM), lambda bb, i: (0, 0))],
        out_specs=[hm_spec] * 3 + [hmt_spec] * 2
                  + [tok(HEAD_DIM),
                     pl.BlockSpec((None, tm // LANES, 2 * HEAD_DIM, LANES), lambda bb, i: (bb, i, 0, 0)),
                     tok(HEAD_DIM),
                     pl.BlockSpec((None, N_HEADS, tm), lambda bb, i: (bb, 0, i)),
                     tok(d), tok(d)],
        out_shape=[hm] * 3 + [hmt] * 2
                  + [tok64,
                     jax.ShapeDtypeStruct((b, s // LANES, 2 * HEAD_DIM, LANES), BF16),
                     tok64,
                     jax.ShapeDtypeStruct((b, N_HEADS, s), F32),
                     jax.ShapeDtypeStruct((b, s, d), BF16),
                     jax.ShapeDtypeStruct((b, s, d), BF16)],
        compiler_params=pltpu.CompilerParams(
            dimension_semantics=("parallel", "parallel"), vmem_limit_bytes=VMEM_LIMIT),
        name="in_proj",
    )(x, mod_l, g1, w_packed, qng, kng)


def _sb_kernel(q_ref, k_ref, v_ref, o_ref, acc_scr, csum_scr, sp_scr, ls_scr):
    tq = q_ref.shape[1]
    tk = SB_KEY_BLOCK
    i = pl.program_id(1)
    nfull = (i * tq) // tk
    row = lax.broadcasted_iota(I32, (tq, tk), 0)
    col = lax.broadcasted_iota(I32, (tq, tk), 1)
    before = (col - row) < (i * tq - nfull * tk)
    krow = lax.broadcasted_iota(I32, (tk, tk), 0)
    kcol = lax.broadcasted_iota(I32, (tk, tk), 1)
    suffix = (krow > kcol).astype(BF16)

    def stage_a(hh, st, first):
        z2 = _nt_dot(q_ref[hh], k_ref[hh, pl.ds(st, tk), :])
        sp2 = jnp.maximum(jnp.log2(1.0 + jnp.exp2(jnp.minimum(z2, 126.0))), z2)
        if first:
            sp2 = jnp.where(before, sp2, 0.0)
        sp_scr[hh] = sp2.astype(BF16)
        ls_scr[hh] = z2 - sp2

    def stage_b(hh, st, first):
        sp = sp_scr[hh]
        suf = _dot(sp, suffix)
        if first:
            a = jnp.where(before, jnp.exp2(ls_scr[hh] - suf), 0.0)
            csum = jnp.zeros((tq, 1), F32)
        else:
            csum = csum_scr[hh]
            a = jnp.exp2((ls_scr[hh] - suf) - csum)
        c = _dot(a.astype(BF16), v_ref[hh, pl.ds(st, tk), :])
        if first:
            acc_scr[hh] = c
        else:
            acc_scr[hh] += c
        csum_scr[hh] = csum + (suf[:, 0:1] + sp[:, 0:1].astype(F32))

    def key_block(st, first):
        for hh in range(N_HEADS):
            stage_a(hh, st, first)
        for hh in range(N_HEADS):
            stage_b(hh, st, first)

    key_block(pl.multiple_of(nfull * tk, tk), True)

    def body(jj, carry):
        key_block(pl.multiple_of((nfull - 1 - jj) * tk, tk), False)
        return carry

    lax.fori_loop(0, nfull, body, 0)
    for hh in range(N_HEADS):
        o_ref[:, hh * HEAD_DIM:(hh + 1) * HEAD_DIM] = acc_scr[hh].astype(o_ref.dtype)


def _sb_attention(qa, ka, va):
    b, nh, s, dh = qa.shape
    tq = SB_QUERY_BLOCK
    assert SB_KEY_BLOCK % tq == 0 and s % SB_KEY_BLOCK == 0
    return pl.pallas_call(
        _sb_kernel,
        grid=(b, s // tq),
        in_specs=[pl.BlockSpec((None, nh, tq, dh), lambda bb, i: (bb, 0, i, 0)),
                  pl.BlockSpec((None, nh, s, dh), lambda bb, i: (bb, 0, 0, 0)),
                  pl.BlockSpec((None, nh, s, dh), lambda bb, i: (bb, 0, 0, 0))],
        out_specs=pl.BlockSpec((None, tq, nh * dh), lambda bb, i: (bb, i, 0)),
        out_shape=jax.ShapeDtypeStruct((b, s, nh * dh), BF16),
        scratch_shapes=[pltpu.VMEM((nh, tq, dh), F32), pltpu.VMEM((nh, tq, 1), F32),
                        pltpu.VMEM((nh, tq, SB_KEY_BLOCK), BF16), pltpu.VMEM((nh, tq, SB_KEY_BLOCK), F32)],
        compiler_params=pltpu.CompilerParams(
            dimension_semantics=("parallel", "parallel"), vmem_limit_bytes=VMEM_LIMIT),
        name="stick_breaking",
    )(qa, ka, va)


def _t5_bucket_np(dist):
    max_exact = REL_BUCKETS // 2
    d_f = np.maximum(dist, 1).astype(np.float32)
    large = max_exact + (np.log(d_f / np.float32(max_exact))
                         / np.float32(math.log(REL_MAX_DIST / max_exact))
                         * np.float32(REL_BUCKETS - max_exact)).astype(np.int32)
    large = np.minimum(large, REL_BUCKETS - 1)
    return np.where(dist < max_exact, dist, large).astype(np.int32)


def _bias_tiles(rel_bias, tq, tk):
    n_near = tq // tk + 1
    key = np.arange(tk)[:, None]
    query = np.arange(tq)[None, :]
    bk = np.stack([_t5_bucket_np(np.maximum(query - key - (n - 1) * tk, 0)) for n in range(n_near)])
    far = int(_t5_bucket_np(np.array([REL_MAX_DIST]))[0])
    nh = rel_bias.shape[1]

    def body(rb_ref, bk_ref, o_ref):
        hh = pl.program_id(1)
        buckets = bk_ref[...]
        tile = jnp.zeros((tk, tq), F32)
        for bucket in range(REL_BUCKETS):
            tile = jnp.where(buckets == bucket, rb_ref[hh, bucket] - rb_ref[hh, far], tile)
        o_ref[...] = tile

    return pl.pallas_call(
        body,
        grid=(n_near, nh),
        in_specs=[pl.BlockSpec(memory_space=pltpu.SMEM),
                  pl.BlockSpec((None, tk, tq), lambda n, hh: (n, 0, 0))],
        out_specs=pl.BlockSpec((None, None, tk, tq), lambda n, hh: (n, hh, 0, 0)),
        out_shape=jax.ShapeDtypeStruct((n_near, nh, tk, tq), F32),
        name="bias_tiles",
    )(rel_bias.T, jnp.asarray(bk))


def _sort_key(x):
    bits = pltpu.bitcast(x, I32)
    return bits ^ ((bits >> 31) & 0x7FFFFFFF)


def _fold_rows(x, op):
    return functools.reduce(op, [x[g * 8:(g + 1) * 8, :] for g in range(x.shape[0] // 8)])


def _dsa_kernel(ktop, qit_ref, ki_ref, wit_ref, qbt_ref, kb_ref, vt_ref, bias_ref, o_ref,
                key_scr, m_scr, acc_scr, s_scr):
    tq = qit_ref.shape[2]
    tk = DSA_KEY_BLOCK
    per_q = tq // tk
    i = pl.program_id(1)
    first_own = per_q * i
    nchunks = per_q * (i + 1)
    key_i = lax.broadcasted_iota(I32, (tk, tq), 0)
    qry_i = lax.broadcasted_iota(I32, (tk, tq), 1)

    def causal(rel):
        return (key_i - qry_i) <= (-rel * tk)

    def for_chunks(fn):
        def far(j, carry):
            for u in range(per_q):
                fn(j * per_q + u, None)
            return carry
        lax.fori_loop(0, i - 1, far, 0)

        @pl.when(i >= 1)
        def _():
            for rel in range(-per_q, -1):
                fn(first_own + rel, None)
            fn(first_own - 1, -1)

        for rel in range(per_q):
            fn(first_own + rel, rel)

    def score_chunk(c, rel):
        kic = ki_ref[pl.ds(pl.multiple_of(c * tk, tk), tk), :]
        score = jnp.zeros((tk, tq), F32)
        for hh in range(N_HEADS):
            score = score + wit_ref[hh:hh + 1, :] * jnp.maximum(_dot(kic, qit_ref[hh]), 0.0)
        keys = _sort_key(score)
        if rel is not None and rel >= 0:
            keys = jnp.where(causal(rel), keys, INT_MIN)
        key_scr[c] = keys

    for_chunks(score_chunk)

    def count(pred):
        def body(j, part):
            for u in range(per_q):
                part = part + _fold_rows(jnp.where(pred(key_scr[j * per_q + u]), 1, 0), jnp.add)
            return part
        part = lax.fori_loop(0, i + 1, body, jnp.zeros((8, tq), I32))
        return jnp.sum(part, axis=0, keepdims=True)

    def bit_body(bi, t_u):
        cand_u = t_u | jnp.left_shift(jnp.int32(1), 31 - bi)
        cand = cand_u ^ INT_MIN
        return jnp.where(count(lambda k: k >= cand) >= ktop, cand_u, t_u)

    thr = lax.fori_loop(0, 32, bit_body, jnp.zeros((1, tq), I32)) ^ INT_MIN
    n_ge = count(lambda k: k >= thr)

    @pl.when(jnp.max(n_ge) > ktop)
    def _():
        need = (ktop - count(lambda k: k > thr)).astype(F32)
        lower = (lax.broadcasted_iota(I32, (tk, tk), 1)
                 < lax.broadcasted_iota(I32, (tk, tk), 0)).astype(BF16)

        def body(c, seen):
            k = key_scr[c]
            eq = k == thr
            eqf = eq.astype(BF16)
            rank = seen + _dot(lower, eqf)
            drop = eq & (rank >= need) & (n_ge > ktop)
            key_scr[c] = jnp.where(drop, thr - 1, k)
            return seen + jnp.sum(eqf.astype(F32), axis=0, keepdims=True)

        lax.fori_loop(0, nchunks, body, jnp.zeros((1, tq), F32))

    def logits(c, rel):
        sel = key_scr[c] >= thr
        if rel is not None and rel >= 0:
            sel = sel & causal(rel)
        kbc = kb_ref[pl.ds(pl.multiple_of(c * tk, tk), tk), :]

        def head(hh):
            s = _dot(kbc, qbt_ref[hh])
            if rel is not None:
                s = s + bias_ref[rel + 1, hh]
            return s
        return sel, head

    m_scr[...] = jnp.full(m_scr.shape, NEG_BIG, F32)

    def max_chunk(c, rel):
        sel, head = logits(c, rel)
        for hh in range(N_HEADS):
            s = jnp.where(sel, head(hh), NEG_BIG)
            s_scr[hh, c] = s
            m_scr[hh] = jnp.maximum(m_scr[hh], _fold_rows(s, jnp.maximum))

    for_chunks(max_chunk)
    m = [jnp.max(m_scr[hh], axis=0, keepdims=True) for hh in range(N_HEADS)]

    acc_scr[...] = jnp.zeros(acc_scr.shape, F32)

    def att_block(j, carry):
        for u in range(per_q):
            c = j * per_q + u
            vtc = vt_ref[c]
            for hh in range(N_HEADS):
                p = jnp.exp(s_scr[hh, c] - m[hh])
                acc_scr[hh] += _dot(vtc, p.astype(BF16))
        return carry

    lax.fori_loop(0, i + 1, att_block, 0)

    out_t = jnp.concatenate(
        [acc_scr[hh, :HEAD_DIM, :] / acc_scr[hh, HEAD_DIM:HEAD_DIM + 1, :] for hh in range(N_HEADS)],
        axis=0)
    o_ref[...] = out_t.T.astype(o_ref.dtype)


def _dsa_attention(qit, ki, wit, qbt, kb, vt, bias):
    b, nh, dh, s = qit.shape
    tq = DSA_QUERY_BLOCK
    tk = DSA_KEY_BLOCK
    assert tk == REL_MAX_DIST and tk == LANES and tq % tk == 0 and s % tq == 0
    ktop = min(INDEX_TOPK_MAX, s // 4)
    hmt_q = pl.BlockSpec((None, nh, dh, tq), lambda bb, i: (bb, 0, 0, i))
    full64 = pl.BlockSpec((None, s, dh), lambda bb, i: (bb, 0, 0))
    return pl.pallas_call(
        functools.partial(_dsa_kernel, ktop),
        grid=(b, s // tq),
        in_specs=[hmt_q, full64,
                  pl.BlockSpec((None, nh, tq), lambda bb, i: (bb, 0, i)),
                  hmt_q, full64,
                  pl.BlockSpec((None, s // tk, 2 * dh, tk), lambda bb, i: (bb, 0, 0, 0)),
                  pl.BlockSpec(bias.shape, lambda bb, i: (0, 0, 0, 0))],
        out_specs=pl.BlockSpec((None, tq, nh * dh), lambda bb, i: (bb, i, 0)),
        out_shape=jax.ShapeDtypeStruct((b, s, nh * dh), BF16),
        scratch_shapes=[pltpu.VMEM((s // tk, tk, tq), I32),
                        pltpu.VMEM((nh, 8, tq), F32),
                        pltpu.VMEM((nh, 2 * dh, tq), F32),
                        pltpu.VMEM((nh, s // tk, tk, tq), F32)],
        compiler_params=pltpu.CompilerParams(
            dimension_semantics=("parallel", "parallel"), vmem_limit_bytes=VMEM_LIMIT),
        name="sparse_attention",
    )(qit, ki, wit, qbt, kb, vt, bias)


def _first_max4(a):
    m1 = jnp.maximum(jnp.maximum(a[0], a[1]), jnp.maximum(a[2], a[3]))
    i1 = jnp.where(a[0] == m1, 0, jnp.where(a[1] == m1, 1, jnp.where(a[2] == m1, 2, 3)))
    rest = [jnp.where(i1 == j, -1.0, a[j]) for j in range(4)]
    m2 = jnp.maximum(jnp.maximum(rest[0], rest[1]), jnp.maximum(rest[2], rest[3]))
    i2 = jnp.where(rest[0] == m2, 0, jnp.where(rest[1] == m2, 1, jnp.where(rest[2] == m2, 2, 3)))
    return m1, i1, m2, i2


def _route(lt):
    lg = [lt[e:e + 1, :] for e in range(N_EXPERTS)]
    mx = functools.reduce(jnp.maximum, lg)
    ex = [jnp.exp(v - mx) for v in lg]
    tot = functools.reduce(lambda u, v: u + v, ex)
    p = [v / tot for v in ex]
    gs = []
    for g in range(N_GROUPS):
        m1, _, m2, _ = _first_max4(p[4 * g:4 * g + 4])
        gs.append(m1 + m2)
    _, gbest, _, _ = _first_max4(gs)
    chosen = [jnp.where(gbest == 0, p[j], jnp.where(gbest == 1, p[4 + j],
              jnp.where(gbest == 2, p[8 + j], p[12 + j]))) for j in range(4)]
    m1, i1, m2, i2 = _first_max4(chosen)
    den = m1 + m2
    lo = jnp.minimum(i1, i2)
    hi = jnp.maximum(i1, i2)
    pair = jnp.where(lo == 0, hi - 1, jnp.where(lo == 1, hi + 1, N_PAIRS - 1))
    first_is_lo = i1 < i2
    w1 = m1 / den
    w2 = m2 / den
    return gbest * N_PAIRS + pair, jnp.where(first_is_lo, w1, w2), jnp.where(first_is_lo, w2, w1)


def _merge_kernel(oa_ref, ob_ref, sga_ref, sgb_ref, x_ref, mod_ref, woa_ref, wob_ref, wout_ref,
                  g2_ref, rwh_ref, rwl_ref, rb_ref, x1_ref, h2x_ref, info_ref, cnt_ref, carry_scr):
    tm, d = x_ref.shape

    @pl.when((pl.program_id(0) == 0) & (pl.program_id(1) == 0))
    def _():
        carry_scr[...] = jnp.zeros(carry_scr.shape, F32)

    merged = (sga_ref[...].astype(F32) * _dot(oa_ref[...], woa_ref[...])
              + sgb_ref[...].astype(F32) * _dot(ob_ref[...], wob_ref[...]))
    x1 = x_ref[...] + mod_ref[2:3, :] * _dot(merged.astype(BF16), wout_ref[...])
    x1_ref[...] = x1
    h2 = (_rms(x1) * g2_ref[...]) * (1.0 + mod_ref[4:5, :]) + mod_ref[3:4, :]
    h2x_ref[:, :d] = h2
    hi = h2.astype(BF16)
    lo = (h2 - hi.astype(F32)).astype(BF16)
    logits = (_dot(hi, rwh_ref[...]) + _dot(lo, rwh_ref[...]) + _dot(hi, rwl_ref[...])) + rb_ref[...]
    cls, cw_lo, cw_hi = _route(logits.T)

    ncls = carry_scr.shape[0]
    onehot = (lax.broadcasted_iota(I32, (ncls, tm), 0) == cls).astype(F32)
    earlier = (lax.broadcasted_iota(I32, (tm, tm), 0)
               < lax.broadcasted_iota(I32, (tm, tm), 1)).astype(BF16)
    before = _dot(onehot.astype(BF16), earlier)
    carry = carry_scr[...]
    carry_t = jnp.concatenate([carry] * (tm // LANES), axis=1)
    rank = jnp.sum(onehot * (carry_t + before), axis=0, keepdims=True).astype(I32)
    carry = carry + jnp.sum(onehot, axis=1, keepdims=True)
    carry_scr[...] = carry
    cnt_ref[...] = carry.astype(I32)

    sub = lax.broadcasted_iota(I32, (8, tm), 0)
    info_ref[...] = jnp.where(sub == 0, cls, jnp.where(sub == 1, rank, 0))
    sub = lax.broadcasted_iota(I32, (LANES, tm), 0)
    weights_t = jnp.where(sub == 0, cw_lo, jnp.where(sub == 1, cw_hi, 0.0))
    h2x_ref[:, d:] = weights_t.T


def _merge(oa, ob, sga, sgb, x, mod_l, woa, wob, wout, g2, rwh, rwl, rb, tm):
    b, s, d = x.shape
    hw = oa.shape[-1]
    nt = s // tm
    tok = lambda n: pl.BlockSpec((None, tm, n), lambda bb, i: (bb, i, 0))
    const = lambda r, c: pl.BlockSpec((r, c), lambda bb, i: (0, 0))
    return pl.pallas_call(
        _merge_kernel,
        grid=(b, nt),
        in_specs=[tok(hw), tok(hw), tok(d), tok(d), tok(d),
                  pl.BlockSpec((None, N_MOD, d), lambda bb, i: (bb, 0, 0)),
                  const(hw, d), const(hw, d), const(d, d), const(1, d),
                  const(d, LANES), const(d, LANES), const(1, LANES)],
        out_specs=[tok(d), tok(d + LANES),
                   pl.BlockSpec((8, tm), lambda bb, i: (0, bb * nt + i)),
                   const(N_CLASS_PAD, LANES)],
        out_shape=[jax.ShapeDtypeStruct((b, s, d), F32),
                   jax.ShapeDtypeStruct((b, s, d + LANES), F32),
                   jax.ShapeDtypeStruct((8, b * s), I32),
                   jax.ShapeDtypeStruct((N_CLASS_PAD, LANES), I32)],
        scratch_shapes=[pltpu.VMEM((N_CLASS_PAD, LANES), F32)],
        compiler_params=pltpu.CompilerParams(
            dimension_semantics=("arbitrary", "arbitrary"), vmem_limit_bytes=VMEM_LIMIT),
        name="merge_route",
    )(oa, ob, sga, sgb, x, mod_l, woa, wob, wout, g2, rwh, rwl, rb)


def _row_position(offs_ref, cls_ref, rank_ref, r):
    return offs_ref[cls_ref[0, r]] + rank_ref[0, r]


def _dispatch_kernel(offs_ref, cls_ref, rank_ref, src_ref, init_ref, dst_ref, sem):
    del init_ref
    tm = cls_ref.shape[1]

    def issue(r, carry):
        p = _row_position(offs_ref, cls_ref, rank_ref, r)
        pltpu.make_async_copy(src_ref.at[pl.ds(r, 1), :], dst_ref.at[pl.ds(p, 1), :], sem).start()
        return carry

    lax.fori_loop(0, tm, issue, 0, unroll=8)
    pltpu.make_async_copy(src_ref, dst_ref.at[pl.ds(0, tm), :], sem).wait()


def _dispatch(h2x, cls, rank, offs, n_rows):
    t, w = h2x.shape
    tm = MOE_DMA_TILE
    smem_row = pl.BlockSpec((None, 1, tm), lambda i: (i, 0, 0), memory_space=pltpu.SMEM)
    return pl.pallas_call(
        _dispatch_kernel,
        grid=(t // tm,),
        in_specs=[pl.BlockSpec(memory_space=pltpu.SMEM), smem_row, smem_row,
                  pl.BlockSpec((tm, w), lambda i: (i, 0)), pl.BlockSpec(memory_space=pl.ANY)],
        out_specs=pl.BlockSpec(memory_space=pl.ANY),
        out_shape=jax.ShapeDtypeStruct((n_rows, w), F32),
        scratch_shapes=[pltpu.SemaphoreType.DMA],
        input_output_aliases={4: 0},
        compiler_params=pltpu.CompilerParams(
            dimension_semantics=("arbitrary",), vmem_limit_bytes=VMEM_LIMIT),
        name="dispatch",
    )(offs, cls, rank, h2x, jnp.zeros((n_rows, w), F32))


def _experts_kernel(elo_ref, ehi_ref, live_ref, xs_ref, wg_lo, wg_hi, wu_lo, wu_hi, wd_lo, wd_hi, ys_ref):
    del elo_ref, ehi_ref
    d = ys_ref.shape[1]
    j = pl.program_id(0)

    @pl.when(live_ref[j] == 1)
    def _():
        h = xs_ref[:, :d].astype(BF16)

        def expert(wg, wu, wd, weight):
            gate = _dot(h, wg[...])
            hidden = ((gate * jax.nn.sigmoid(gate)) * _dot(h, wu[...])) * weight
            return _dot(hidden.astype(BF16), wd[...])

        ys_ref[...] = (expert(wg_lo, wu_lo, wd_lo, xs_ref[:, d:d + 1])
                       + expert(wg_hi, wu_hi, wd_hi, xs_ref[:, d + 1:d + 2]))

    @pl.when(live_ref[j] == 0)
    def _():
        ys_ref[...] = jnp.zeros(ys_ref.shape, F32)


def _experts(xs, e_lo, e_hi, live, wg, wu, wd):
    n_rows, w = xs.shape
    ne, d, f = wg.shape
    tm = MOE_ROW_TILE
    lo = lambda j, elo, ehi, lv: (elo[j], 0, 0)
    hi = lambda j, elo, ehi, lv: (ehi[j], 0, 0)
    grid_spec = pltpu.PrefetchScalarGridSpec(
        num_scalar_prefetch=3,
        grid=(n_rows // tm,),
        in_specs=[pl.BlockSpec((tm, w), lambda j, elo, ehi, lv: (j, 0)),
                  pl.BlockSpec((None, d, f), lo), pl.BlockSpec((None, d, f), hi),
                  pl.BlockSpec((None, d, f), lo), pl.BlockSpec((None, d, f), hi),
                  pl.BlockSpec((None, f, d), lo), pl.BlockSpec((None, f, d), hi)],
        out_specs=pl.BlockSpec((tm, d), lambda j, elo, ehi, lv: (j, 0)),
    )
    return pl.pallas_call(
        _experts_kernel,
        grid_spec=grid_spec,
        out_shape=jax.ShapeDtypeStruct((n_rows, d), F32),
        compiler_params=pltpu.CompilerParams(
            dimension_semantics=("arbitrary",), vmem_limit_bytes=VMEM_LIMIT),
        name="experts",
    )(e_lo, e_hi, live, xs, wg, wg, wu, wu, wd, wd)


def _combine_kernel(offs_ref, cls_ref, rank_ref, ys_ref, x1_ref, mod_ref, o_ref, buf, sem):
    tm = o_ref.shape[0]

    def issue(r, carry):
        p = _row_position(offs_ref, cls_ref, rank_ref, r)
        pltpu.make_async_copy(ys_ref.at[pl.ds(p, 1), :], buf.at[pl.ds(r, 1), :], sem).start()
        return carry

    lax.fori_loop(0, tm, issue, 0, unroll=8)
    pltpu.make_async_copy(ys_ref.at[pl.ds(0, tm), :], buf, sem).wait()
    o_ref[...] = x1_ref[...] + mod_ref[5:6, :] * buf[...]


def _combine(ys, cls, rank, offs, x1, mod_l):
    b, s, d = x1.shape
    tm = MOE_DMA_TILE
    nt = s // tm
    smem_row = pl.BlockSpec((None, 1, tm), lambda bb, i: (bb * nt + i, 0, 0), memory_space=pltpu.SMEM)
    return pl.pallas_call(
        _combine_kernel,
        grid=(b, nt),
        in_specs=[pl.BlockSpec(memory_space=pltpu.SMEM), smem_row, smem_row,
                  pl.BlockSpec(memory_space=pl.ANY),
                  pl.BlockSpec((None, tm, d), lambda bb, i: (bb, i, 0)),
                  pl.BlockSpec((None, N_MOD, d), lambda bb, i: (bb, 0, 0))],
        out_specs=pl.BlockSpec((None, tm, d), lambda bb, i: (bb, i, 0)),
        out_shape=jax.ShapeDtypeStruct((b, s, d), F32),
        scratch_shapes=[pltpu.VMEM((tm, d), F32), pltpu.SemaphoreType.DMA],
        compiler_params=pltpu.CompilerParams(
            dimension_semantics=("arbitrary", "arbitrary"), vmem_limit_bytes=VMEM_LIMIT),
        name="combine",
    )(offs, cls, rank, ys, x1, mod_l)


def _moe(h2x, info, counts, wg, wu, wd, x1, mod_l):
    b, s, d = x1.shape
    t = b * s
    tm = MOE_ROW_TILE
    n_tiles = t // tm + N_CLASSES
    counts = counts[:N_CLASSES, 0]
    padded = (counts + tm - 1) // tm * tm
    ends = jnp.cumsum(padded)
    offs = jnp.pad(ends - padded, (0, N_CLASS_PAD - N_CLASSES)).astype(I32)
    tile_cls = jnp.sum(jnp.arange(n_tiles, dtype=I32)[:, None] * tm >= ends[None, :], axis=1)
    live = (tile_cls < N_CLASSES).astype(I32)
    tile_cls = jnp.minimum(tile_cls, N_CLASSES - 1)
    group, pair = tile_cls // N_PAIRS, tile_cls % N_PAIRS
    pairs = jnp.asarray(_PAIRS, I32)
    e_lo = group * EXPERTS_PER_GROUP + pairs[pair, 0]
    e_hi = group * EXPERTS_PER_GROUP + pairs[pair, 1]

    nblk = t // MOE_DMA_TILE
    cls = info[0].reshape(nblk, 1, MOE_DMA_TILE)
    rank = info[1].reshape(nblk, 1, MOE_DMA_TILE)
    xs = _dispatch(h2x.reshape(t, d + LANES), cls, rank, offs, n_tiles * tm)
    ys = _experts(xs, e_lo, e_hi, live, wg, wu, wd)
    return _combine(ys, cls, rank, offs, x1, mod_l)


def kernel(x, c, w_ada, b_ada, norm1_g, w_in, q_norm_g, k_norm_g, rel_bias, w_o_a, w_o_b, w_out,
           norm2_g, router_w, router_b, w_gate, w_up, w_down):
    b, s, d = x.shape
    depth = w_ada.shape[0]
    assert d % LANES == 0
    tm = min(512, s)

    mod = _adaln(c, w_ada, b_ada)
    bias = _bias_tiles(rel_bias, DSA_QUERY_BLOCK, DSA_KEY_BLOCK)
    rw = jnp.pad(router_w, ((0, 0), (0, LANES - N_EXPERTS)))
    rwh = rw.astype(BF16)
    rwl = (rw - rwh.astype(F32)).astype(BF16)
    rb = jnp.pad(router_b, (0, LANES - N_EXPERTS)).reshape(1, LANES)

    for l in range(depth):
        qa, ka, va, qbt, qit, kb, vt, ki, wit, sga, sgb = _in_proj(
            x, mod[l], norm1_g[l].reshape(1, d), _pack_w_in(w_in[l]),
            q_norm_g[l].reshape(HEAD_DIM, 1), k_norm_g[l].reshape(1, HEAD_DIM), tm)
        oa = _sb_attention(qa, ka, va)
        ob = _dsa_attention(qit, ki, wit, qbt, kb, vt, bias)
        x1, h2x, info, counts = _merge(oa, ob, sga, sgb, x, mod[l], w_o_a[l].astype(BF16),
                                       w_o_b[l].astype(BF16), w_out[l].astype(BF16),
                                       norm2_g[l].reshape(1, d), rwh, rwl, rb, tm)
        x = _moe(h2x, info, counts, w_gate[l].astype(BF16), w_up[l].astype(BF16),
                 w_down[l].astype(BF16), x1, mod[l])
    return x
```

```python
import functools
import math

import numpy as np
import jax
import jax.numpy as jnp
from jax import lax
from jax.experimental import pallas as pl
from jax.experimental.pallas import tpu as pltpu

F32 = jnp.float32
BF16 = jnp.bfloat16
I32 = jnp.int32
I16 = jnp.int16

HEAD_DIM = 64
N_HEADS = 8
N_EXPERTS = 16
N_GROUPS = 4
EXPERTS_PER_GROUP = 4
_PAIRS = ((0, 1), (0, 2), (0, 3), (1, 2), (1, 3), (2, 3))
N_PAIRS = len(_PAIRS)
N_CLASSES = N_GROUPS * N_PAIRS
N_CLASS_PAD = 32
MOE_ROW_TILE = 256
MOE_DMA_TILE = 512
N_MOD = 6
EPS = 1e-6
INDEX_TOPK_MAX = 256
REL_BUCKETS = 32
REL_MAX_DIST = 128
LANES = 128
DSA_QUERY_BLOCK = 256
DSA_KEY_BLOCK = 128
SB_QUERY_BLOCK = 256
SB_KEY_BLOCK = 256
LOG2E = 1.4426950408889634
NEG_BIG = -1e30
INT_MIN = -2 ** 31
HALF_BIAS = 2 ** 15

VMEM_LIMIT = 56 * 1024 * 1024


def _nt_dot(a, b):
    return lax.dot_general(a, b, (((1,), (1,)), ((), ())), preferred_element_type=F32)


def _dot(a, b):
    return jnp.dot(a, b, preferred_element_type=F32)


def _mod_kernel(c_ref, w_ref, b_ref, o_ref):
    c = c_ref[...]
    sc = c * jax.nn.sigmoid(c)
    o_ref[...] = jnp.dot(sc, w_ref[...], preferred_element_type=F32,
                         precision=lax.Precision.HIGHEST) + b_ref[...]


def _adaln(c, w_ada, b_ada):
    depth, d, nd = w_ada.shape
    b = c.shape[0]
    out = pl.pallas_call(
        _mod_kernel,
        grid=(depth, nd // d),
        in_specs=[pl.BlockSpec((b, d), lambda l, j: (0, 0)),
                  pl.BlockSpec((None, d, d), lambda l, j: (l, 0, j)),
                  pl.BlockSpec((None, 1, d), lambda l, j: (l, 0, j))],
        out_specs=pl.BlockSpec((None, b, d), lambda l, j: (l, 0, j)),
        out_shape=jax.ShapeDtypeStruct((depth, b, nd), F32),
        compiler_params=pltpu.CompilerParams(vmem_limit_bytes=VMEM_LIMIT),
        name="adaln",
    )(c, w_ada, b_ada.reshape(depth, 1, nd))
    return out.reshape(depth, b, N_MOD, d)


def _rms(x):
    return x * lax.rsqrt(jnp.mean(x * x, axis=-1, keepdims=True) + EPS)


def _in_kernel(x_ref, mod_ref, g1_ref, w_ref, qng_ref, kng_ref,
               qa_ref, ka_ref, va_ref, qbt_ref, qit_ref, kb_ref, vt_ref, ki_ref, wit_ref,
               sga_ref, sgb_ref):
    d = x_ref.shape[-1]
    tm = x_ref.shape[0]
    hw = N_HEADS * HEAD_DIM
    x = x_ref[...]
    h = (_rms(x) * g1_ref[...]) * (1.0 + mod_ref[1:2, :]) + mod_ref[0:1, :]
    hb = h.astype(BF16)

    def proj(lo, n):
        return _dot(hb, w_ref[:, lo:lo + n])

    def heads(r, ref):
        for hh in range(N_HEADS):
            ref[hh] = r[:, hh * HEAD_DIM:(hh + 1) * HEAD_DIM].astype(ref.dtype)

    scale = HEAD_DIM ** -0.5
    heads(proj(0, hw) * (scale * LOG2E), qa_ref)
    heads(proj(hw, hw), ka_ref)
    heads(proj(2 * hw, hw), va_ref)

    rt = proj(3 * hw, hw).T
    for hh in range(N_HEADS):
        slab = rt[hh * HEAD_DIM:(hh + 1) * HEAD_DIM, :]
        inv = lax.rsqrt(jnp.mean(slab * slab, axis=0, keepdims=True) + EPS)
        qbt_ref[hh] = (((slab * inv) * qng_ref[...]) * scale).astype(BF16)

    r = proj(4 * hw, 2 * HEAD_DIM)
    kb_ref[...] = (_rms(r[:, :HEAD_DIM]) * kng_ref[...]).astype(BF16)
    vt = r.T[HEAD_DIM:, :]
    ones_row = (lax.broadcasted_iota(I32, (HEAD_DIM, LANES), 0) == 0).astype(BF16)
    for cc in range(tm // LANES):
        vt_ref[cc, :HEAD_DIM, :] = vt[:, cc * LANES:(cc + 1) * LANES].astype(BF16)
        vt_ref[cc, HEAD_DIM:, :] = ones_row

    rt = proj(4 * hw + LANES, hw).T
    for hh in range(N_HEADS):
        qit_ref[hh] = rt[hh * HEAD_DIM:(hh + 1) * HEAD_DIM, :].astype(BF16)

    r = proj(5 * hw + LANES, LANES)
    ki_ref[...] = r[:, :HEAD_DIM].astype(BF16)
    wit_ref[...] = r.T[HEAD_DIM:HEAD_DIM + N_HEADS, :]

    off = 5 * hw + 2 * LANES
    sga_ref[...] = jax.nn.sigmoid(proj(off, d)).astype(BF16)
    sgb_ref[...] = jax.nn.sigmoid(proj(off + d, d)).astype(BF16)


def _pack_w_in(w_in_l):
    hw = N_HEADS * HEAD_DIM
    a = 4 * hw + 2 * HEAD_DIM + hw
    small = w_in_l[:, a:a + HEAD_DIM + N_HEADS]
    small = jnp.pad(small, ((0, 0), (0, LANES - small.shape[1])))
    gates = w_in_l[:, a + HEAD_DIM + N_HEADS:]
    return jnp.concatenate([w_in_l[:, :a], small, gates], axis=1).astype(BF16)


def _in_proj(x, mod_l, g1, w_packed, qng, kng, tm):
    b, s, d = x.shape
    hm = jax.ShapeDtypeStruct((b, N_HEADS, s, HEAD_DIM), BF16)
    hmt = jax.ShapeDtypeStruct((b, N_HEADS, HEAD_DIM, s), BF16)
    tok64 = jax.ShapeDtypeStruct((b, s, HEAD_DIM), BF16)
    hm_spec = pl.BlockSpec((None, N_HEADS, tm, HEAD_DIM), lambda bb, i: (bb, 0, i, 0))
    hmt_spec = pl.BlockSpec((None, N_HEADS, HEAD_DIM, tm), lambda bb, i: (bb, 0, 0, i))
    tok = lambda n: pl.BlockSpec((None, tm, n), lambda bb, i: (bb, i, 0))
    nw = w_packed.shape[1]
    return pl.pallas_call(
        _in_kernel,
        grid=(b, s // tm),
        in_specs=[tok(d),
                  pl.BlockSpec((None, N_MOD, d), lambda bb, i: (bb, 0, 0)),
                  pl.BlockSpec((1, d), lambda bb, i: (0, 0)),
                  pl.BlockSpec((d, nw), lambda bb, i: (0, 0)),
                  pl.BlockSpec((HEAD_DIM, 1), lambda bb, i: (0, 0)),
                  pl.BlockSpec((1, HEAD_DIM), lambda bb, i: (0, 0))],
        out_specs=[hm_spec] * 3 + [hmt_spec] * 2
                  + [tok(HEAD_DIM),
                     pl.BlockSpec((None, tm // LANES, 2 * HEAD_DIM, LANES), lambda bb, i: (bb, i, 0, 0)),
                     tok(HEAD_DIM),
                     pl.BlockSpec((None, N_HEADS, tm), lambda bb, i: (bb, 0, i)),
                     tok(d), tok(d)],
        out_shape=[hm] * 3 + [hmt] * 2
                  + [tok64,
                     jax.ShapeDtypeStruct((b, s // LANES, 2 * HEAD_DIM, LANES), BF16),
                     tok64,
                     jax.ShapeDtypeStruct((b, N_HEADS, s), F32),
                     jax.ShapeDtypeStruct((b, s, d), BF16),
                     jax.ShapeDtypeStruct((b, s, d), BF16)],
        compiler_params=pltpu.CompilerParams(
            dimension_semantics=("parallel", "parallel"), vmem_limit_bytes=VMEM_LIMIT),
        name="in_proj",
    )(x, mod_l, g1, w_packed, qng, kng)


def _sb_kernel(q_ref, k_ref, v_ref, o_ref, acc_scr, csum_scr, sp_scr, ls_scr):
    tq = q_ref.shape[1]
    tk = SB_KEY_BLOCK
    i = pl.program_id(1)
    nfull = (i * tq) // tk
    row = lax.broadcasted_iota(I32, (tq, tk), 0)
    col = lax.broadcasted_iota(I32, (tq, tk), 1)
    before = (col - row) < (i * tq - nfull * tk)
    krow = lax.broadcasted_iota(I32, (tk, tk), 0)
    kcol = lax.broadcasted_iota(I32, (tk, tk), 1)
    suffix = (krow > kcol).astype(BF16)

    def stage_a(hh, st, first):
        z2 = _nt_dot(q_ref[hh], k_ref[hh, pl.ds(st, tk), :])
        sp2 = jnp.maximum(jnp.log2(1.0 + jnp.exp2(jnp.minimum(z2, 126.0))), z2)
        if first:
            sp2 = jnp.where(before, sp2, 0.0)
        sp_scr[hh] = sp2.astype(BF16)
        ls_scr[hh] = z2 - sp2

    def stage_b(hh, st, first):
        sp = sp_scr[hh]
        suf = _dot(sp, suffix)
        if first:
            a = jnp.where(before, jnp.exp2(ls_scr[hh] - suf), 0.0)
            csum = jnp.zeros((tq, 1), F32)
        else:
            csum = csum_scr[hh]
            a = jnp.exp2((ls_scr[hh] - suf) - csum)
        c = _dot(a.astype(BF16), v_ref[hh, pl.ds(st, tk), :])
        if first:
            acc_scr[hh] = c
        else:
            acc_scr[hh] += c
        csum_scr[hh] = csum + (suf[:, 0:1] + sp[:, 0:1].astype(F32))

    def key_block(st, first):
        for hh in range(N_HEADS):
            stage_a(hh, st, first)
        for hh in range(N_HEADS):
            stage_b(hh, st, first)

    key_block(pl.multiple_of(nfull * tk, tk), True)

    def body(jj, carry):
        key_block(pl.multiple_of((nfull - 1 - jj) * tk, tk), False)
        return carry

    lax.fori_loop(0, nfull, body, 0)
    for hh in range(N_HEADS):
        o_ref[:, hh * HEAD_DIM:(hh + 1) * HEAD_DIM] = acc_scr[hh].astype(o_ref.dtype)


def _sb_attention(qa, ka, va):
    b, nh, s, dh = qa.shape
    tq = SB_QUERY_BLOCK
    assert SB_KEY_BLOCK % tq == 0 and s % SB_KEY_BLOCK == 0
    return pl.pallas_call(
        _sb_kernel,
        grid=(b, s // tq),
        in_specs=[pl.BlockSpec((None, nh, tq, dh), lambda bb, i: (bb, 0, i, 0)),
                  pl.BlockSpec((None, nh, s, dh), lambda bb, i: (bb, 0, 0, 0)),
                  pl.BlockSpec((None, nh, s, dh), lambda bb, i: (bb, 0, 0, 0))],
        out_specs=pl.BlockSpec((None, tq, nh * dh), lambda bb, i: (bb, i, 0)),
        out_shape=jax.ShapeDtypeStruct((b, s, nh * dh), BF16),
        scratch_shapes=[pltpu.VMEM((nh, tq, dh), F32), pltpu.VMEM((nh, tq, 1), F32),
                        pltpu.VMEM((nh, tq, SB_KEY_BLOCK), BF16), pltpu.VMEM((nh, tq, SB_KEY_BLOCK), F32)],
        compiler_params=pltpu.CompilerParams(
            dimension_semantics=("parallel", "parallel"), vmem_limit_bytes=VMEM_LIMIT),
        name="stick_breaking",
    )(qa, ka, va)


def _t5_bucket_np(dist):
    max_exact = REL_BUCKETS // 2
    d_f = np.maximum(dist, 1).astype(np.float32)
    large = max_exact + (np.log(d_f / np.float32(max_exact))
                         / np.float32(math.log(REL_MAX_DIST / max_exact))
                         * np.float32(REL_BUCKETS - max_exact)).astype(np.int32)
    large = np.minimum(large, REL_BUCKETS - 1)
    return np.where(dist < max_exact, dist, large).astype(np.int32)


def _bias_tiles(rel_bias, tq, tk):
    n_near = tq // tk + 1
    key = np.arange(tk)[:, None]
    query = np.arange(tq)[None, :]
    bk = np.stack([_t5_bucket_np(np.maximum(query - key - (n - 1) * tk, 0)) for n in range(n_near)])
    far = int(_t5_bucket_np(np.array([REL_MAX_DIST]))[0])
    nh = rel_bias.shape[1]

    def body(rb_ref, bk_ref, o_ref):
        hh = pl.program_id(1)
        buckets = bk_ref[...]
        tile = jnp.zeros((tk, tq), F32)
        for bucket in range(REL_BUCKETS):
            tile = jnp.where(buckets == bucket, rb_ref[hh, bucket] - rb_ref[hh, far], tile)
        o_ref[...] = tile

    return pl.pallas_call(
        body,
        grid=(n_near, nh),
        in_specs=[pl.BlockSpec(memory_space=pltpu.SMEM),
                  pl.BlockSpec((None, tk, tq), lambda n, hh: (n, 0, 0))],
        out_specs=pl.BlockSpec((None, None, tk, tq), lambda n, hh: (n, hh, 0, 0)),
        out_shape=jax.ShapeDtypeStruct((n_near, nh, tk, tq), F32),
        name="bias_tiles",
    )(rel_bias.T, jnp.asarray(bk))


def _sort_key(x):
    bits = pltpu.bitcast(x, I32)
    return bits ^ ((bits >> 31) & 0x7FFFFFFF)


def _fold_rows(x, op):
    return functools.reduce(op, [x[g * 8:(g + 1) * 8, :] for g in range(x.shape[0] // 8)])


def _dsa_kernel(ktop, qit_ref, ki_ref, wit_ref, qbt_ref, kb_ref, vt_ref, bias_ref, o_ref,
                key_scr, hi_scr, lo_scr, m_scr, acc_scr, s_scr):
    tq = qit_ref.shape[2]
    tk = DSA_KEY_BLOCK
    per_q = tq // tk
    i = pl.program_id(1)
    first_own = per_q * i
    nchunks = per_q * (i + 1)
    key_i = lax.broadcasted_iota(I32, (tk, tq), 0)
    qry_i = lax.broadcasted_iota(I32, (tk, tq), 1)

    def causal(rel):
        return (key_i - qry_i) <= (-rel * tk)

    def for_chunks(fn):
        def far(j, carry):
            for u in range(per_q):
                fn(j * per_q + u, None)
            return carry
        lax.fori_loop(0, i - 1, far, 0)

        @pl.when(i >= 1)
        def _():
            for rel in range(-per_q, -1):
                fn(first_own + rel, None)
            fn(first_own - 1, -1)

        for rel in range(per_q):
            fn(first_own + rel, rel)

    def score_chunk(c, rel):
        kic = ki_ref[pl.ds(pl.multiple_of(c * tk, tk), tk), :]
        score = jnp.zeros((tk, tq), F32)
        for hh in range(N_HEADS):
            score = score + wit_ref[hh:hh + 1, :] * jnp.maximum(_dot(kic, qit_ref[hh]), 0.0)
        keys = _sort_key(score)
        if rel is not None and rel >= 0:
            keys = jnp.where(causal(rel), keys, INT_MIN)
        key_scr[c] = keys
        hi_scr[c] = (keys >> 16).astype(I16)
        lo_scr[c] = ((keys & 0xFFFF) - HALF_BIAS).astype(I16)

    for_chunks(score_chunk)

    def count(pred):
        def body(j, part):
            for u in range(per_q):
                part = part + _fold_rows(jnp.where(pred(key_scr[j * per_q + u]), 1, 0), jnp.add)
            return part
        part = lax.fori_loop(0, i + 1, body, jnp.zeros((8, tq), I32))
        return jnp.sum(part, axis=0, keepdims=True)

    def count16(scr, pred):
        one, zero = jnp.int16(1), jnp.int16(0)

        def body(j, part):
            for u in range(per_q):
                hit = jnp.where(pred(scr[j * per_q + u]), one, zero)
                part = part + functools.reduce(
                    jnp.add, [hit[g * 16:(g + 1) * 16, :] for g in range(tk // 16)])
            return part
        part = lax.fori_loop(0, i + 1, body, jnp.zeros((16, tq), I16))
        return jnp.sum(part.astype(I32), axis=0, keepdims=True)

    def search16(scr, base):
        def bit_body(bi, t_u):
            cand_u = t_u | jnp.left_shift(jnp.int32(1), 15 - bi)
            cand = (cand_u - HALF_BIAS).astype(I16)
            return jnp.where(base + count16(scr, lambda k: k >= cand) >= ktop, cand_u, t_u)
        return lax.fori_loop(0, 16, bit_body, jnp.zeros((1, tq), I32))

    t_hi = search16(hi_scr, 0) - HALF_BIAS
    t_hi16 = t_hi.astype(I16)
    above = count16(hi_scr, lambda k: k > t_hi16)

    def keep_band(j, carry):
        for u in range(per_q):
            c = j * per_q + u
            lo_scr[c] = jnp.where(hi_scr[c] == t_hi16, lo_scr[c], jnp.int16(-HALF_BIAS))
        return carry

    lax.fori_loop(0, i + 1, keep_band, 0)
    thr = t_hi * (2 * HALF_BIAS) + search16(lo_scr, above)
    n_ge = count(lambda k: k >= thr)

    @pl.when(jnp.max(n_ge) > ktop)
    def _():
        need = (ktop - count(lambda k: k > thr)).astype(F32)
        lower = (lax.broadcasted_iota(I32, (tk, tk), 1)
                 < lax.broadcasted_iota(I32, (tk, tk), 0)).astype(BF16)

        def body(c, seen):
            k = key_scr[c]
            eq = k == thr
            eqf = eq.astype(BF16)
            rank = seen + _dot(lower, eqf)
            drop = eq & (rank >= need) & (n_ge > ktop)
            key_scr[c] = jnp.where(drop, thr - 1, k)
            return seen + jnp.sum(eqf.astype(F32), axis=0, keepdims=True)

        lax.fori_loop(0, nchunks, body, jnp.zeros((1, tq), F32))

    def logits(c, rel):
        sel = key_scr[c] >= thr
        if rel is not None and rel >= 0:
            sel = sel & causal(rel)
        kbc = kb_ref[pl.ds(pl.multiple_of(c * tk, tk), tk), :]

        def head(hh):
            s = _dot(kbc, qbt_ref[hh])
            if rel is not None:
                s = s + bias_ref[rel + 1, hh]
            return s
        return sel, head

    m_scr[...] = jnp.full(m_scr.shape, NEG_BIG, F32)

    def max_chunk(c, rel):
        sel, head = logits(c, rel)
        for hh in range(N_HEADS):
            s = jnp.where(sel, head(hh), NEG_BIG)
            s_scr[hh, c] = s
            m_scr[hh] = jnp.maximum(m_scr[hh], _fold_rows(s, jnp.maximum))

    for_chunks(max_chunk)
    m = [jnp.max(m_scr[hh], axis=0, keepdims=True) for hh in range(N_HEADS)]

    acc_scr[...] = jnp.zeros(acc_scr.shape, F32)

    def att_block(j, carry):
        for u in range(per_q):
            c = j * per_q + u
            vtc = vt_ref[c]
            for hh in range(N_HEADS):
                p = jnp.exp(s_scr[hh, c] - m[hh])
                acc_scr[hh] += _dot(vtc, p.astype(BF16))
        return carry

    lax.fori_loop(0, i + 1, att_block, 0)

    out_t = jnp.concatenate(
        [acc_scr[hh, :HEAD_DIM, :] / acc_scr[hh, HEAD_DIM:HEAD_DIM + 1, :] for hh in range(N_HEADS)],
        axis=0)
    o_ref[...] = out_t.T.astype(o_ref.dtype)


def _dsa_attention(qit, ki, wit, qbt, kb, vt, bias):
    b, nh, dh, s = qit.shape
    tq = DSA_QUERY_BLOCK
    tk = DSA_KEY_BLOCK
    assert tk == REL_MAX_DIST and tk == LANES and tq % tk == 0 and s % tq == 0
    ktop = min(INDEX_TOPK_MAX, s // 4)
    hmt_q = pl.BlockSpec((None, nh, dh, tq), lambda bb, i: (bb, 0, 0, i))
    full64 = pl.BlockSpec((None, s, dh), lambda bb, i: (bb, 0, 0))
    return pl.pallas_call(
        functools.partial(_dsa_kernel, ktop),
        grid=(b, s // tq),
        in_specs=[hmt_q, full64,
                  pl.BlockSpec((None, nh, tq), lambda bb, i: (bb, 0, i)),
                  hmt_q, full64,
                  pl.BlockSpec((None, s // tk, 2 * dh, tk), lambda bb, i: (bb, 0, 0, 0)),
                  pl.BlockSpec(bias.shape, lambda bb, i: (0, 0, 0, 0))],
        out_specs=pl.BlockSpec((None, tq, nh * dh), lambda bb, i: (bb, i, 0)),
        out_shape=jax.ShapeDtypeStruct((b, s, nh * dh), BF16),
        scratch_shapes=[pltpu.VMEM((s // tk, tk, tq), I32),
                        pltpu.VMEM((s // tk, tk, tq), I16),
                        pltpu.VMEM((s // tk, tk, tq), I16),
                        pltpu.VMEM((nh, 8, tq), F32),
                        pltpu.VMEM((nh, 2 * dh, tq), F32),
                        pltpu.VMEM((nh, s // tk, tk, tq), F32)],
        compiler_params=pltpu.CompilerParams(
            dimension_semantics=("parallel", "parallel"), vmem_limit_bytes=VMEM_LIMIT),
        name="sparse_attention",
    )(qit, ki, wit, qbt, kb, vt, bias)


def _first_max4(a):
    m1 = jnp.maximum(jnp.maximum(a[0], a[1]), jnp.maximum(a[2], a[3]))
    i1 = jnp.where(a[0] == m1, 0, jnp.where(a[1] == m1, 1, jnp.where(a[2] == m1, 2, 3)))
    rest = [jnp.where(i1 == j, -1.0, a[j]) for j in range(4)]
    m2 = jnp.maximum(jnp.maximum(rest[0], rest[1]), jnp.maximum(rest[2], rest[3]))
    i2 = jnp.where(rest[0] == m2, 0, jnp.where(rest[1] == m2, 1, jnp.where(rest[2] == m2, 2, 3)))
    return m1, i1, m2, i2


def _route(lt):
    lg = [lt[e:e + 1, :] for e in range(N_EXPERTS)]
    mx = functools.reduce(jnp.maximum, lg)
    ex = [jnp.exp(v - mx) for v in lg]
    tot = functools.reduce(lambda u, v: u + v, ex)
    p = [v / tot for v in ex]
    gs = []
    for g in range(N_GROUPS):
        m1, _, m2, _ = _first_max4(p[4 * g:4 * g + 4])
        gs.append(m1 + m2)
    _, gbest, _, _ = _first_max4(gs)
    chosen = [jnp.where(gbest == 0, p[j], jnp.where(gbest == 1, p[4 + j],
              jnp.where(gbest == 2, p[8 + j], p[12 + j]))) for j in range(4)]
    m1, i1, m2, i2 = _first_max4(chosen)
    den = m1 + m2
    lo = jnp.minimum(i1, i2)
    hi = jnp.maximum(i1, i2)
    pair = jnp.where(lo == 0, hi - 1, jnp.where(lo == 1, hi + 1, N_PAIRS - 1))
    first_is_lo = i1 < i2
    w1 = m1 / den
    w2 = m2 / den
    return gbest * N_PAIRS + pair, jnp.where(first_is_lo, w1, w2), jnp.where(first_is_lo, w2, w1)


def _merge_kernel(oa_ref, ob_ref, sga_ref, sgb_ref, x_ref, mod_ref, woa_ref, wob_ref, wout_ref,
                  g2_ref, rwh_ref, rwl_ref, rb_ref, x1_ref, h2x_ref, info_ref, cnt_ref, carry_scr):
    tm, d = x_ref.shape

    @pl.when((pl.program_id(0) == 0) & (pl.program_id(1) == 0))
    def _():
        carry_scr[...] = jnp.zeros(carry_scr.shape, F32)

    merged = (sga_ref[...].astype(F32) * _dot(oa_ref[...], woa_ref[...])
              + sgb_ref[...].astype(F32) * _dot(ob_ref[...], wob_ref[...]))
    x1 = x_ref[...] + mod_ref[2:3, :] * _dot(merged.astype(BF16), wout_ref[...])
    x1_ref[...] = x1
    h2 = (_rms(x1) * g2_ref[...]) * (1.0 + mod_ref[4:5, :]) + mod_ref[3:4, :]
    h2x_ref[:, :d] = h2
    hi = h2.astype(BF16)
    lo = (h2 - hi.astype(F32)).astype(BF16)
    logits = (_dot(hi, rwh_ref[...]) + _dot(lo, rwh_ref[...]) + _dot(hi, rwl_ref[...])) + rb_ref[...]
    cls, cw_lo, cw_hi = _route(logits.T)

    ncls = carry_scr.shape[0]
    onehot = (lax.broadcasted_iota(I32, (ncls, tm), 0) == cls).astype(F32)
    earlier = (lax.broadcasted_iota(I32, (tm, tm), 0)
               < lax.broadcasted_iota(I32, (tm, tm), 1)).astype(BF16)
    before = _dot(onehot.astype(BF16), earlier)
    carry = carry_scr[...]
    carry_t = jnp.concatenate([carry] * (tm // LANES), axis=1)
    rank = jnp.sum(onehot * (carry_t + before), axis=0, keepdims=True).astype(I32)
    carry = carry + jnp.sum(onehot, axis=1, keepdims=True)
    carry_scr[...] = carry
    cnt_ref[...] = carry.astype(I32)

    sub = lax.broadcasted_iota(I32, (8, tm), 0)
    info_ref[...] = jnp.where(sub == 0, cls, jnp.where(sub == 1, rank, 0))
    sub = lax.broadcasted_iota(I32, (LANES, tm), 0)
    weights_t = jnp.where(sub == 0, cw_lo, jnp.where(sub == 1, cw_hi, 0.0))
    h2x_ref[:, d:] = weights_t.T


def _merge(oa, ob, sga, sgb, x, mod_l, woa, wob, wout, g2, rwh, rwl, rb, tm):
    b, s, d = x.shape
    hw = oa.shape[-1]
    nt = s // tm
    tok = lambda n: pl.BlockSpec((None, tm, n), lambda bb, i: (bb, i, 0))
    const = lambda r, c: pl.BlockSpec((r, c), lambda bb, i: (0, 0))
    return pl.pallas_call(
        _merge_kernel,
        grid=(b, nt),
        in_specs=[tok(hw), tok(hw), tok(d), tok(d), tok(d),
                  pl.BlockSpec((None, N_MOD, d), lambda bb, i: (bb, 0, 0)),
                  const(hw, d), const(hw, d), const(d, d), const(1, d),
                  const(d, LANES), const(d, LANES), const(1, LANES)],
        out_specs=[tok(d), tok(d + LANES),
                   pl.BlockSpec((8, tm), lambda bb, i: (0, bb * nt + i)),
                   const(N_CLASS_PAD, LANES)],
        out_shape=[jax.ShapeDtypeStruct((b, s, d), F32),
                   jax.ShapeDtypeStruct((b, s, d + LANES), F32),
                   jax.ShapeDtypeStruct((8, b * s), I32),
                   jax.ShapeDtypeStruct((N_CLASS_PAD, LANES), I32)],
        scratch_shapes=[pltpu.VMEM((N_CLASS_PAD, LANES), F32)],
        compiler_params=pltpu.CompilerParams(
            dimension_semantics=("arbitrary", "arbitrary"), vmem_limit_bytes=VMEM_LIMIT),
        name="merge_route",
    )(oa, ob, sga, sgb, x, mod_l, woa, wob, wout, g2, rwh, rwl, rb)


def _positions_kernel(offs_ref, info_ref, pos_ref):
    cls = info_ref[0:1, :]
    pos = info_ref[1:2, :]
    for c in range(N_CLASSES):
        pos = pos + jnp.where(cls == c, offs_ref[c], 0)
    pos_ref[...] = jnp.broadcast_to(pos, pos_ref.shape)


def _positions(info, offs):
    return pl.pallas_call(
        _positions_kernel,
        in_specs=[pl.BlockSpec(memory_space=pltpu.SMEM), pl.BlockSpec(info.shape, lambda: (0, 0))],
        out_specs=pl.BlockSpec(info.shape, lambda: (0, 0)),
        out_shape=jax.ShapeDtypeStruct(info.shape, I32),
        name="positions",
    )(offs, info)[0]


def _dispatch_kernel(pos_ref, src_ref, init_ref, dst_ref, sem):
    del init_ref
    tm = pos_ref.shape[1]

    def issue(r, carry):
        pltpu.make_async_copy(src_ref.at[pl.ds(r, 1), :], dst_ref.at[pl.ds(pos_ref[0, r], 1), :], sem).start()
        return carry

    lax.fori_loop(0, tm, issue, 0, unroll=8)
    pltpu.make_async_copy(src_ref, dst_ref.at[pl.ds(0, tm), :], sem).wait()


def _dispatch(h2x, pos, n_rows):
    t, w = h2x.shape
    tm = MOE_DMA_TILE
    smem_row = pl.BlockSpec((None, 1, tm), lambda i: (i, 0, 0), memory_space=pltpu.SMEM)
    return pl.pallas_call(
        _dispatch_kernel,
        grid=(t // tm,),
        in_specs=[smem_row, pl.BlockSpec((tm, w), lambda i: (i, 0)), pl.BlockSpec(memory_space=pl.ANY)],
        out_specs=pl.BlockSpec(memory_space=pl.ANY),
        out_shape=jax.ShapeDtypeStruct((n_rows, w), F32),
        scratch_shapes=[pltpu.SemaphoreType.DMA],
        input_output_aliases={2: 0},
        compiler_params=pltpu.CompilerParams(
            dimension_semantics=("arbitrary",), vmem_limit_bytes=VMEM_LIMIT),
        name="dispatch",
    )(pos, h2x, jnp.zeros((n_rows, w), F32))


def _experts_kernel(elo_ref, ehi_ref, live_ref, xs_ref, wg_lo, wg_hi, wu_lo, wu_hi, wd_lo, wd_hi, ys_ref):
    del elo_ref, ehi_ref
    d = ys_ref.shape[1]
    j = pl.program_id(0)

    @pl.when(live_ref[j] == 1)
    def _():
        h = xs_ref[:, :d].astype(BF16)

        def expert(wg, wu, wd, weight):
            gate = _dot(h, wg[...])
            hidden = ((gate * jax.nn.sigmoid(gate)) * _dot(h, wu[...])) * weight
            return _dot(hidden.astype(BF16), wd[...])

        ys_ref[...] = (expert(wg_lo, wu_lo, wd_lo, xs_ref[:, d:d + 1])
                       + expert(wg_hi, wu_hi, wd_hi, xs_ref[:, d + 1:d + 2]))

    @pl.when(live_ref[j] == 0)
    def _():
        ys_ref[...] = jnp.zeros(ys_ref.shape, F32)


def _experts(xs, e_lo, e_hi, live, wg, wu, wd):
    n_rows, w = xs.shape
    ne, d, f = wg.shape
    tm = MOE_ROW_TILE
    lo = lambda j, elo, ehi, lv: (elo[j], 0, 0)
    hi = lambda j, elo, ehi, lv: (ehi[j], 0, 0)
    grid_spec = pltpu.PrefetchScalarGridSpec(
        num_scalar_prefetch=3,
        grid=(n_rows // tm,),
        in_specs=[pl.BlockSpec((tm, w), lambda j, elo, ehi, lv: (j, 0)),
                  pl.BlockSpec((None, d, f), lo), pl.BlockSpec((None, d, f), hi),
                  pl.BlockSpec((None, d, f), lo), pl.BlockSpec((None, d, f), hi),
                  pl.BlockSpec((None, f, d), lo), pl.BlockSpec((None, f, d), hi)],
        out_specs=pl.BlockSpec((tm, d), lambda j, elo, ehi, lv: (j, 0)),
    )
    return pl.pallas_call(
        _experts_kernel,
        grid_spec=grid_spec,
        out_shape=jax.ShapeDtypeStruct((n_rows, d), F32),
        compiler_params=pltpu.CompilerParams(
            dimension_semantics=("arbitrary",), vmem_limit_bytes=VMEM_LIMIT),
        name="experts",
    )(e_lo, e_hi, live, xs, wg, wg, wu, wu, wd, wd)


def _combine_kernel(pos_ref, ys_ref, x1_ref, mod_ref, o_ref, buf, sem):
    tm = o_ref.shape[0]

    def issue(r, carry):
        pltpu.make_async_copy(ys_ref.at[pl.ds(pos_ref[0, r], 1), :], buf.at[pl.ds(r, 1), :], sem).start()
        return carry

    lax.fori_loop(0, tm, issue, 0, unroll=8)
    pltpu.make_async_copy(ys_ref.at[pl.ds(0, tm), :], buf, sem).wait()
    o_ref[...] = x1_ref[...] + mod_ref[5:6, :] * buf[...]


def _combine(ys, pos, x1, mod_l):
    b, s, d = x1.shape
    tm = MOE_DMA_TILE
    nt = s // tm
    smem_row = pl.BlockSpec((None, 1, tm), lambda bb, i: (bb * nt + i, 0, 0), memory_space=pltpu.SMEM)
    return pl.pallas_call(
        _combine_kernel,
        grid=(b, nt),
        in_specs=[smem_row,
                  pl.BlockSpec(memory_space=pl.ANY),
                  pl.BlockSpec((None, tm, d), lambda bb, i: (bb, i, 0)),
                  pl.BlockSpec((None, N_MOD, d), lambda bb, i: (bb, 0, 0))],
        out_specs=pl.BlockSpec((None, tm, d), lambda bb, i: (bb, i, 0)),
        out_shape=jax.ShapeDtypeStruct((b, s, d), F32),
        scratch_shapes=[pltpu.VMEM((tm, d), F32), pltpu.SemaphoreType.DMA],
        compiler_params=pltpu.CompilerParams(
            dimension_semantics=("arbitrary", "arbitrary"), vmem_limit_bytes=VMEM_LIMIT),
        name="combine",
    )(pos, ys, x1, mod_l)


def _moe(h2x, info, counts, wg, wu, wd, x1, mod_l):
    b, s, d = x1.shape
    t = b * s
    tm = MOE_ROW_TILE
    n_tiles = t // tm + N_CLASSES
    counts = counts[:N_CLASSES, 0]
    padded = (counts + tm - 1) // tm * tm
    ends = jnp.cumsum(padded)
    offs = jnp.pad(ends - padded, (0, N_CLASS_PAD - N_CLASSES)).astype(I32)
    tile_cls = jnp.sum(jnp.arange(n_tiles, dtype=I32)[:, None] * tm >= ends[None, :], axis=1)
    live = (tile_cls < N_CLASSES).astype(I32)
    tile_cls = jnp.minimum(tile_cls, N_CLASSES - 1)
    group, pair = tile_cls // N_PAIRS, tile_cls % N_PAIRS
    pairs = jnp.asarray(_PAIRS, I32)
    e_lo = group * EXPERTS_PER_GROUP + pairs[pair, 0]
    e_hi = group * EXPERTS_PER_GROUP + pairs[pair, 1]

    pos = _positions(info, offs).reshape(t // MOE_DMA_TILE, 1, MOE_DMA_TILE)
    xs = _dispatch(h2x.reshape(t, d + LANES), pos, n_tiles * tm)
    ys = _experts(xs, e_lo, e_hi, live, wg, wu, wd)
    return _combine(ys, pos, x1, mod_l)


def kernel(x, c, w_ada, b_ada, norm1_g, w_in, q_norm_g, k_norm_g, rel_bias, w_o_a, w_o_b, w_out,
           norm2_g, router_w, router_b, w_gate, w_up, w_down):
    b, s, d = x.shape
    depth = w_ada.shape[0]
    assert d % LANES == 0
    tm = min(512, s)

    mod = _adaln(c, w_ada, b_ada)
    bias = _bias_tiles(rel_bias, DSA_QUERY_BLOCK, DSA_KEY_BLOCK)
    rw = jnp.pad(router_w, ((0, 0), (0, LANES - N_EXPERTS)))
    rwh = rw.astype(BF16)
    rwl = (rw - rwh.astype(F32)).astype(BF16)
    rb = jnp.pad(router_b, (0, LANES - N_EXPERTS)).reshape(1, LANES)

    for l in range(depth):
        qa, ka, va, qbt, qit, kb, vt, ki, wit, sga, sgb = _in_proj(
            x, mod[l], norm1_g[l].reshape(1, d), _pack_w_in(w_in[l]),
            q_norm_g[l].reshape(HEAD_DIM, 1), k_norm_g[l].reshape(1, HEAD_DIM), tm)
        oa = _sb_attention(qa, ka, va)
        ob = _dsa_attention(qit, ki, wit, qbt, kb, vt, bias)
        x1, h2x, info, counts = _merge(oa, ob, sga, sgb, x, mod[l], w_o_a[l].astype(BF16),
                                       w_o_b[l].astype(BF16), w_out[l].astype(BF16),
                                       norm2_g[l].reshape(1, d), rwh, rwl, rb, tm)
        x = _moe(h2x, info, counts, w_gate[l].astype(BF16), w_up[l].astype(BF16),
                 w_down[l].astype(BF16), x1, mod[l])
    return x
```

```python
import functools
import math

import numpy as np
import jax
import jax.numpy as jnp
from jax import lax
from jax.experimental import pallas as pl
from jax.experimental.pallas import tpu as pltpu

F32 = jnp.float32
BF16 = jnp.bfloat16
I32 = jnp.int32
I16 = jnp.int16

HEAD_DIM = 64
N_HEADS = 8
N_EXPERTS = 16
N_GROUPS = 4
EXPERTS_PER_GROUP = 4
_PAIRS = ((0, 1), (0, 2), (0, 3), (1, 2), (1, 3), (2, 3))
N_PAIRS = len(_PAIRS)
N_CLASSES = N_GROUPS * N_PAIRS
N_CLASS_PAD = 32
MOE_ROW_TILE = 256
MOE_DMA_TILE = 512
N_MOD = 6
EPS = 1e-6
INDEX_TOPK_MAX = 256
REL_BUCKETS = 32
REL_MAX_DIST = 128
LANES = 128
DSA_QUERY_BLOCK = 256
DSA_KEY_BLOCK = 128
SB_QUERY_BLOCK = 256
SB_KEY_BLOCK = 256
SB_DEAD_LOG2 = 150.0
LOG2E = 1.4426950408889634
NEG_BIG = -1e30
INT_MIN = -2 ** 31
HALF_BIAS = 2 ** 15

VMEM_LIMIT = 56 * 1024 * 1024


def _nt_dot(a, b):
    return lax.dot_general(a, b, (((1,), (1,)), ((), ())), preferred_element_type=F32)


def _dot(a, b):
    return jnp.dot(a, b, preferred_element_type=F32)


def _mod_kernel(c_ref, w_ref, b_ref, o_ref):
    c = c_ref[...]
    sc = c * jax.nn.sigmoid(c)
    o_ref[...] = jnp.dot(sc, w_ref[...], preferred_element_type=F32,
                         precision=lax.Precision.HIGHEST) + b_ref[...]


def _adaln(c, w_ada, b_ada):
    depth, d, nd = w_ada.shape
    b = c.shape[0]
    out = pl.pallas_call(
        _mod_kernel,
        grid=(depth, nd // d),
        in_specs=[pl.BlockSpec((b, d), lambda l, j: (0, 0)),
                  pl.BlockSpec((None, d, d), lambda l, j: (l, 0, j)),
                  pl.BlockSpec((None, 1, d), lambda l, j: (l, 0, j))],
        out_specs=pl.BlockSpec((None, b, d), lambda l, j: (l, 0, j)),
        out_shape=jax.ShapeDtypeStruct((depth, b, nd), F32),
        compiler_params=pltpu.CompilerParams(vmem_limit_bytes=VMEM_LIMIT),
        name="adaln",
    )(c, w_ada, b_ada.reshape(depth, 1, nd))
    return out.reshape(depth, b, N_MOD, d)


def _rms(x):
    return x * lax.rsqrt(jnp.mean(x * x, axis=-1, keepdims=True) + EPS)


def _in_kernel(x_ref, mod_ref, g1_ref, w_ref, qng_ref, kng_ref,
               qa_ref, ka_ref, va_ref, qbt_ref, qit_ref, kb_ref, vt_ref, ki_ref, wit_ref,
               sga_ref, sgb_ref):
    d = x_ref.shape[-1]
    tm = x_ref.shape[0]
    hw = N_HEADS * HEAD_DIM
    x = x_ref[...]
    h = (_rms(x) * g1_ref[...]) * (1.0 + mod_ref[1:2, :]) + mod_ref[0:1, :]
    hb = h.astype(BF16)

    def proj(lo, n):
        return _dot(hb, w_ref[:, lo:lo + n])

    def heads(r, ref):
        for hh in range(N_HEADS):
            ref[hh] = r[:, hh * HEAD_DIM:(hh + 1) * HEAD_DIM].astype(ref.dtype)

    scale = HEAD_DIM ** -0.5
    heads(proj(0, hw) * (scale * LOG2E), qa_ref)
    heads(proj(hw, hw), ka_ref)
    heads(proj(2 * hw, hw), va_ref)

    rt = proj(3 * hw, hw).T
    for hh in range(N_HEADS):
        slab = rt[hh * HEAD_DIM:(hh + 1) * HEAD_DIM, :]
        inv = lax.rsqrt(jnp.mean(slab * slab, axis=0, keepdims=True) + EPS)
        qbt_ref[hh] = (((slab * inv) * qng_ref[...]) * scale).astype(BF16)

    r = proj(4 * hw, 2 * HEAD_DIM)
    kb_ref[...] = (_rms(r[:, :HEAD_DIM]) * kng_ref[...]).astype(BF16)
    vt = r.T[HEAD_DIM:, :]
    ones_row = (lax.broadcasted_iota(I32, (HEAD_DIM, LANES), 0) == 0).astype(BF16)
    for cc in range(tm // LANES):
        vt_ref[cc, :HEAD_DIM, :] = vt[:, cc * LANES:(cc + 1) * LANES].astype(BF16)
        vt_ref[cc, HEAD_DIM:, :] = ones_row

    rt = proj(4 * hw + LANES, hw).T
    for hh in range(N_HEADS):
        qit_ref[hh] = rt[hh * HEAD_DIM:(hh + 1) * HEAD_DIM, :].astype(BF16)

    r = proj(5 * hw + LANES, LANES)
    ki_ref[...] = r[:, :HEAD_DIM].astype(BF16)
    wit_ref[...] = r.T[HEAD_DIM:HEAD_DIM + N_HEADS, :]

    off = 5 * hw + 2 * LANES
    sga_ref[...] = jax.nn.sigmoid(proj(off, d)).astype(BF16)
    sgb_ref[...] = jax.nn.sigmoid(proj(off + d, d)).astype(BF16)


def _pack_w_in(w_in_l):
    hw = N_HEADS * HEAD_DIM
    a = 4 * hw + 2 * HEAD_DIM + hw
    small = w_in_l[:, a:a + HEAD_DIM + N_HEADS]
    small = jnp.pad(small, ((0, 0), (0, LANES - small.shape[1])))
    gates = w_in_l[:, a + HEAD_DIM + N_HEADS:]
    return jnp.concatenate([w_in_l[:, :a], small, gates], axis=1).astype(BF16)


def _in_proj(x, mod_l, g1, w_packed, qng, kng, tm):
    b, s, d = x.shape
    hm = jax.ShapeDtypeStruct((b, N_HEADS, s, HEAD_DIM), BF16)
    hmt = jax.ShapeDtypeStruct((b, N_HEADS, HEAD_DIM, s), BF16)
    tok64 = jax.ShapeDtypeStruct((b, s, HEAD_DIM), BF16)
    hm_spec = pl.BlockSpec((None, N_HEADS, tm, HEAD_DIM), lambda bb, i: (bb, 0, i, 0))
    hmt_spec = pl.BlockSpec((None, N_HEADS, HEAD_DIM, tm), lambda bb, i: (bb, 0, 0, i))
    tok = lambda n: pl.BlockSpec((None, tm, n), lambda bb, i: (bb, i, 0))
    nw = w_packed.shape[1]
    return pl.pallas_call(
        _in_kernel,
        grid=(b, s // tm),
        in_specs=[tok(d),
                  pl.BlockSpec((None, N_MOD, d), lambda bb, i: (bb, 0, 0)),
                  pl.BlockSpec((1, d), lambda bb, i: (0, 0)),
                  pl.BlockSpec((d, nw), lambda bb, i: (0, 0)),
                  pl.BlockSpec((HEAD_DIM, 1), lambda bb, i: (0, 0)),
                  pl.BlockSpec((1, HEAD_DIM), lambda bb, i: (0, 0))],
        out_specs=[hm_spec] * 3 + [hmt_spec] * 2
                  + [tok(HEAD_DIM),
                     pl.BlockSpec((None, tm // LANES, 2 * HEAD_DIM, LANES), lambda bb, i: (bb, i, 0, 0)),
                     tok(HEAD_DIM),
                     pl.BlockSpec((None, N_HEADS, tm), lambda bb, i: (bb, 0, i)),
                     tok(d), tok(d)],
        out_shape=[hm] * 3 + [hmt] * 2
                  + [tok64,
                     jax.ShapeDtypeStruct((b, s // LANES, 2 * HEAD_DIM, LANES), BF16),
                     tok64,
                     jax.ShapeDtypeStruct((b, N_HEADS, s), F32),
                     jax.ShapeDtypeStruct((b, s, d), BF16),
                     jax.ShapeDtypeStruct((b, s, d), BF16)],
        compiler_params=pltpu.CompilerParams(
            dimension_semantics=("parallel", "parallel"), vmem_limit_bytes=VMEM_LIMIT),
        name="in_proj",
    )(x, mod_l, g1, w_packed, qng, kng)


def _sb_kernel(q_ref, k_ref, v_ref, o_ref, acc_scr, csum_scr, sp_scr, ls_scr):
    tq = q_ref.shape[1]
    tk = SB_KEY_BLOCK
    i = pl.program_id(1)
    nfull = (i * tq) // tk
    row = lax.broadcasted_iota(I32, (tq, tk), 0)
    col = lax.broadcasted_iota(I32, (tq, tk), 1)
    before = (col - row) < (i * tq - nfull * tk)
    krow = lax.broadcasted_iota(I32, (tk, tk), 0)
    kcol = lax.broadcasted_iota(I32, (tk, tk), 1)
    suffix = (krow > kcol).astype(BF16)

    def stage_a(hh, st, first):
        z2 = _nt_dot(q_ref[hh], k_ref[hh, pl.ds(st, tk), :])
        sp2 = jnp.maximum(jnp.log2(1.0 + jnp.exp2(jnp.minimum(z2, 126.0))), z2)
        if first:
            sp2 = jnp.where(before, sp2, 0.0)
        sp_scr[hh] = sp2.astype(BF16)
        ls_scr[hh] = z2 - sp2

    def stage_b(hh, st, first):
        sp = sp_scr[hh]
        suf = _dot(sp, suffix)
        if first:
            a = jnp.where(before, jnp.exp2(ls_scr[hh] - suf), 0.0)
            csum = jnp.zeros((tq, 1), F32)
        else:
            csum = csum_scr[hh]
            a = jnp.exp2((ls_scr[hh] - suf) - csum)
        c = _dot(a.astype(BF16), v_ref[hh, pl.ds(st, tk), :])
        if first:
            acc_scr[hh] = c
        else:
            acc_scr[hh] += c
        csum_scr[hh] = csum + (suf[:, 0:1] + sp[:, 0:1].astype(F32))

    def key_block(st, first):
        for hh in range(N_HEADS):
            stage_a(hh, st, first)
        for hh in range(N_HEADS):
            stage_b(hh, st, first)

    key_block(pl.multiple_of(nfull * tk, tk), True)

    def live():
        return jnp.min(csum_scr[...]) < SB_DEAD_LOG2

    def body(carry):
        jj, _ = carry
        key_block(pl.multiple_of((nfull - 1 - jj) * tk, tk), False)
        return jj + 1, live()

    lax.while_loop(lambda carry: (carry[0] < nfull) & carry[1], body, (jnp.int32(0), live()))
    for hh in range(N_HEADS):
        o_ref[:, hh * HEAD_DIM:(hh + 1) * HEAD_DIM] = acc_scr[hh].astype(o_ref.dtype)


def _sb_attention(qa, ka, va):
    b, nh, s, dh = qa.shape
    tq = SB_QUERY_BLOCK
    assert SB_KEY_BLOCK % tq == 0 and s % SB_KEY_BLOCK == 0
    return pl.pallas_call(
        _sb_kernel,
        grid=(b, s // tq),
        in_specs=[pl.BlockSpec((None, nh, tq, dh), lambda bb, i: (bb, 0, i, 0)),
                  pl.BlockSpec((None, nh, s, dh), lambda bb, i: (bb, 0, 0, 0)),
                  pl.BlockSpec((None, nh, s, dh), lambda bb, i: (bb, 0, 0, 0))],
        out_specs=pl.BlockSpec((None, tq, nh * dh), lambda bb, i: (bb, i, 0)),
        out_shape=jax.ShapeDtypeStruct((b, s, nh * dh), BF16),
        scratch_shapes=[pltpu.VMEM((nh, tq, dh), F32), pltpu.VMEM((nh, tq, 1), F32),
                        pltpu.VMEM((nh, tq, SB_KEY_BLOCK), BF16), pltpu.VMEM((nh, tq, SB_KEY_BLOCK), F32)],
        compiler_params=pltpu.CompilerParams(
            dimension_semantics=("parallel", "parallel"), vmem_limit_bytes=VMEM_LIMIT),
        name="stick_breaking",
    )(qa, ka, va)


def _t5_bucket_np(dist):
    max_exact = REL_BUCKETS // 2
    d_f = np.maximum(dist, 1).astype(np.float32)
    large = max_exact + (np.log(d_f / np.float32(max_exact))
                         / np.float32(math.log(REL_MAX_DIST / max_exact))
                         * np.float32(REL_BUCKETS - max_exact)).astype(np.int32)
    large = np.minimum(large, REL_BUCKETS - 1)
    return np.where(dist < max_exact, dist, large).astype(np.int32)


def _bias_tiles(rel_bias, tq, tk):
    n_near = tq // tk + 1
    key = np.arange(tk)[:, None]
    query = np.arange(tq)[None, :]
    bk = np.stack([_t5_bucket_np(np.maximum(query - key - (n - 1) * tk, 0)) for n in range(n_near)])
    far = int(_t5_bucket_np(np.array([REL_MAX_DIST]))[0])
    nh = rel_bias.shape[1]

    def body(rb_ref, bk_ref, o_ref):
        hh = pl.program_id(1)
        buckets = bk_ref[...]
        tile = jnp.zeros((tk, tq), F32)
        for bucket in range(REL_BUCKETS):
            tile = jnp.where(buckets == bucket, rb_ref[hh, bucket] - rb_ref[hh, far], tile)
        o_ref[...] = tile

    return pl.pallas_call(
        body,
        grid=(n_near, nh),
        in_specs=[pl.BlockSpec(memory_space=pltpu.SMEM),
                  pl.BlockSpec((None, tk, tq), lambda n, hh: (n, 0, 0))],
        out_specs=pl.BlockSpec((None, None, tk, tq), lambda n, hh: (n, hh, 0, 0)),
        out_shape=jax.ShapeDtypeStruct((n_near, nh, tk, tq), F32),
        name="bias_tiles",
    )(rel_bias.T, jnp.asarray(bk))


def _sort_key(x):
    bits = pltpu.bitcast(x, I32)
    return bits ^ ((bits >> 31) & 0x7FFFFFFF)


def _fold_rows(x, op):
    return functools.reduce(op, [x[g * 8:(g + 1) * 8, :] for g in range(x.shape[0] // 8)])


def _dsa_kernel(ktop, qit_ref, ki_ref, wit_ref, qbt_ref, kb_ref, vt_ref, bias_ref, o_ref,
                key_scr, hi_scr, lo_scr, m_scr, acc_scr, s_scr):
    tq = qit_ref.shape[2]
    tk = DSA_KEY_BLOCK
    per_q = tq // tk
    i = pl.program_id(1)
    first_own = per_q * i
    nchunks = per_q * (i + 1)
    key_i = lax.broadcasted_iota(I32, (tk, tq), 0)
    qry_i = lax.broadcasted_iota(I32, (tk, tq), 1)

    def causal(rel):
        return (key_i - qry_i) <= (-rel * tk)

    def for_chunks(fn):
        def far(j, carry):
            for u in range(per_q):
                fn(j * per_q + u, None)
            return carry
        lax.fori_loop(0, i - 1, far, 0)

        @pl.when(i >= 1)
        def _():
            for rel in range(-per_q, -1):
                fn(first_own + rel, None)
            fn(first_own - 1, -1)

        for rel in range(per_q):
            fn(first_own + rel, rel)

    def score_chunk(c, rel):
        kic = ki_ref[pl.ds(pl.multiple_of(c * tk, tk), tk), :]
        score = jnp.zeros((tk, tq), F32)
        for hh in range(N_HEADS):
            score = score + wit_ref[hh:hh + 1, :] * jnp.maximum(_dot(kic, qit_ref[hh]), 0.0)
        keys = _sort_key(score)
        if rel is not None and rel >= 0:
            keys = jnp.where(causal(rel), keys, INT_MIN)
        key_scr[c] = keys
        hi_scr[c] = (keys >> 16).astype(I16)
        lo_scr[c] = ((keys & 0xFFFF) - HALF_BIAS).astype(I16)

    for_chunks(score_chunk)

    def count(pred):
        def body(j, part):
            for u in range(per_q):
                part = part + _fold_rows(jnp.where(pred(key_scr[j * per_q + u]), 1, 0), jnp.add)
            return part
        part = lax.fori_loop(0, i + 1, body, jnp.zeros((8, tq), I32))
        return jnp.sum(part, axis=0, keepdims=True)

    def count16(scr, pred):
        one, zero = jnp.int16(1), jnp.int16(0)

        def body(j, part):
            for u in range(per_q):
                hit = jnp.where(pred(scr[j * per_q + u]), one, zero)
                part = part + functools.reduce(
                    jnp.add, [hit[g * 16:(g + 1) * 16, :] for g in range(tk // 16)])
            return part
        part = lax.fori_loop(0, i + 1, body, jnp.zeros((16, tq), I16))
        return jnp.sum(part.astype(I32), axis=0, keepdims=True)

    def search16(scr, base):
        def bit_body(bi, t_u):
            cand_u = t_u | jnp.left_shift(jnp.int32(1), 15 - bi)
            cand = (cand_u - HALF_BIAS).astype(I16)
            return jnp.where(base + count16(scr, lambda k: k >= cand) >= ktop, cand_u, t_u)
        return lax.fori_loop(0, 16, bit_body, jnp.zeros((1, tq), I32))

    t_hi = search16(hi_scr, 0) - HALF_BIAS
    t_hi16 = t_hi.astype(I16)
    above = count16(hi_scr, lambda k: k > t_hi16)

    def keep_band(j, carry):
        for u in range(per_q):
            c = j * per_q + u
            lo_scr[c] = jnp.where(hi_scr[c] == t_hi16, lo_scr[c], jnp.int16(-HALF_BIAS))
        return carry

    lax.fori_loop(0, i + 1, keep_band, 0)
    thr = t_hi * (2 * HALF_BIAS) + search16(lo_scr, above)
    n_ge = count(lambda k: k >= thr)

    @pl.when(jnp.max(n_ge) > ktop)
    def _():
        need = (ktop - count(lambda k: k > thr)).astype(F32)
        lower = (lax.broadcasted_iota(I32, (tk, tk), 1)
                 < lax.broadcasted_iota(I32, (tk, tk), 0)).astype(BF16)

        def body(c, seen):
            k = key_scr[c]
            eq = k == thr
            eqf = eq.astype(BF16)
            rank = seen + _dot(lower, eqf)
            drop = eq & (rank >= need) & (n_ge > ktop)
            key_scr[c] = jnp.where(drop, thr - 1, k)
            return seen + jnp.sum(eqf.astype(F32), axis=0, keepdims=True)

        lax.fori_loop(0, nchunks, body, jnp.zeros((1, tq), F32))

    def logits(c, rel):
        sel = key_scr[c] >= thr
        if rel is not None and rel >= 0:
            sel = sel & causal(rel)
        kbc = kb_ref[pl.ds(pl.multiple_of(c * tk, tk), tk), :]

        def head(hh):
            s = _dot(kbc, qbt_ref[hh])
            if rel is not None:
                s = s + bias_ref[rel + 1, hh]
            return s
        return sel, head

    m_scr[...] = jnp.full(m_scr.shape, NEG_BIG, F32)

    def max_chunk(c, rel):
        sel, head = logits(c, rel)
        for hh in range(N_HEADS):
            s = jnp.where(sel, head(hh), NEG_BIG)
            s_scr[hh, c] = s
            m_scr[hh] = jnp.maximum(m_scr[hh], _fold_rows(s, jnp.maximum))

    for_chunks(max_chunk)
    m = [jnp.max(m_scr[hh], axis=0, keepdims=True) for hh in range(N_HEADS)]

    acc_scr[...] = jnp.zeros(acc_scr.shape, F32)

    def att_block(j, carry):
        for u in range(per_q):
            c = j * per_q + u
            vtc = vt_ref[c]
            for hh in range(N_HEADS):
                p = jnp.exp(s_scr[hh, c] - m[hh])
                acc_scr[hh] += _dot(vtc, p.astype(BF16))
        return carry

    lax.fori_loop(0, i + 1, att_block, 0)

    out_t = jnp.concatenate(
        [acc_scr[hh, :HEAD_DIM, :] / acc_scr[hh, HEAD_DIM:HEAD_DIM + 1, :] for hh in range(N_HEADS)],
        axis=0)
    o_ref[...] = out_t.T.astype(o_ref.dtype)


def _dsa_attention(qit, ki, wit, qbt, kb, vt, bias):
    b, nh, dh, s = qit.shape
    tq = DSA_QUERY_BLOCK
    tk = DSA_KEY_BLOCK
    assert tk == REL_MAX_DIST and tk == LANES and tq % tk == 0 and s % tq == 0
    ktop = min(INDEX_TOPK_MAX, s // 4)
    hmt_q = pl.BlockSpec((None, nh, dh, tq), lambda bb, i: (bb, 0, 0, i))
    full64 = pl.BlockSpec((None, s, dh), lambda bb, i: (bb, 0, 0))
    return pl.pallas_call(
        functools.partial(_dsa_kernel, ktop),
        grid=(b, s // tq),
        in_specs=[hmt_q, full64,
                  pl.BlockSpec((None, nh, tq), lambda bb, i: (bb, 0, i)),
                  hmt_q, full64,
                  pl.BlockSpec((None, s // tk, 2 * dh, tk), lambda bb, i: (bb, 0, 0, 0)),
                  pl.BlockSpec(bias.shape, lambda bb, i: (0, 0, 0, 0))],
        out_specs=pl.BlockSpec((None, tq, nh * dh), lambda bb, i: (bb, i, 0)),
        out_shape=jax.ShapeDtypeStruct((b, s, nh * dh), BF16),
        scratch_shapes=[pltpu.VMEM((s // tk, tk, tq), I32),
                        pltpu.VMEM((s // tk, tk, tq), I16),
                        pltpu.VMEM((s // tk, tk, tq), I16),
                        pltpu.VMEM((nh, 8, tq), F32),
                        pltpu.VMEM((nh, 2 * dh, tq), F32),
                        pltpu.VMEM((nh, s // tk, tk, tq), F32)],
        compiler_params=pltpu.CompilerParams(
            dimension_semantics=("parallel", "parallel"), vmem_limit_bytes=VMEM_LIMIT),
        name="sparse_attention",
    )(qit, ki, wit, qbt, kb, vt, bias)


def _first_max4(a):
    m1 = jnp.maximum(jnp.maximum(a[0], a[1]), jnp.maximum(a[2], a[3]))
    i1 = jnp.where(a[0] == m1, 0, jnp.where(a[1] == m1, 1, jnp.where(a[2] == m1, 2, 3)))
    rest = [jnp.where(i1 == j, -1.0, a[j]) for j in range(4)]
    m2 = jnp.maximum(jnp.maximum(rest[0], rest[1]), jnp.maximum(rest[2], rest[3]))
    i2 = jnp.where(rest[0] == m2, 0, jnp.where(rest[1] == m2, 1, jnp.where(rest[2] == m2, 2, 3)))
    return m1, i1, m2, i2


def _route(lt):
    lg = [lt[e:e + 1, :] for e in range(N_EXPERTS)]
    mx = functools.reduce(jnp.maximum, lg)
    ex = [jnp.exp(v - mx) for v in lg]
    tot = functools.reduce(lambda u, v: u + v, ex)
    p = [v / tot for v in ex]
    gs = []
    for g in range(N_GROUPS):
        m1, _, m2, _ = _first_max4(p[4 * g:4 * g + 4])
        gs.append(m1 + m2)
    _, gbest, _, _ = _first_max4(gs)
    chosen = [jnp.where(gbest == 0, p[j], jnp.where(gbest == 1, p[4 + j],
              jnp.where(gbest == 2, p[8 + j], p[12 + j]))) for j in range(4)]
    m1, i1, m2, i2 = _first_max4(chosen)
    den = m1 + m2
    lo = jnp.minimum(i1, i2)
    hi = jnp.maximum(i1, i2)
    pair = jnp.where(lo == 0, hi - 1, jnp.where(lo == 1, hi + 1, N_PAIRS - 1))
    first_is_lo = i1 < i2
    w1 = m1 / den
    w2 = m2 / den
    return gbest * N_PAIRS + pair, jnp.where(first_is_lo, w1, w2), jnp.where(first_is_lo, w2, w1)


def _merge_kernel(oa_ref, ob_ref, sga_ref, sgb_ref, x_ref, mod_ref, woa_ref, wob_ref, wout_ref,
                  g2_ref, rwh_ref, rwl_ref, rb_ref, x1_ref, h2x_ref, info_ref, cnt_ref, carry_scr):
    tm, d = x_ref.shape

    @pl.when((pl.program_id(0) == 0) & (pl.program_id(1) == 0))
    def _():
        carry_scr[...] = jnp.zeros(carry_scr.shape, F32)

    merged = (sga_ref[...].astype(F32) * _dot(oa_ref[...], woa_ref[...])
              + sgb_ref[...].astype(F32) * _dot(ob_ref[...], wob_ref[...]))
    x1 = x_ref[...] + mod_ref[2:3, :] * _dot(merged.astype(BF16), wout_ref[...])
    x1_ref[...] = x1
    h2 = (_rms(x1) * g2_ref[...]) * (1.0 + mod_ref[4:5, :]) + mod_ref[3:4, :]
    h2x_ref[:, :d] = h2
    hi = h2.astype(BF16)
    lo = (h2 - hi.astype(F32)).astype(BF16)
    logits = (_dot(hi, rwh_ref[...]) + _dot(lo, rwh_ref[...]) + _dot(hi, rwl_ref[...])) + rb_ref[...]
    cls, cw_lo, cw_hi = _route(logits.T)

    ncls = carry_scr.shape[0]
    onehot = (lax.broadcasted_iota(I32, (ncls, tm), 0) == cls).astype(F32)
    earlier = (lax.broadcasted_iota(I32, (tm, tm), 0)
               < lax.broadcasted_iota(I32, (tm, tm), 1)).astype(BF16)
    before = _dot(onehot.astype(BF16), earlier)
    carry = carry_scr[...]
    carry_t = jnp.concatenate([carry] * (tm // LANES), axis=1)
    rank = jnp.sum(onehot * (carry_t + before), axis=0, keepdims=True).astype(I32)
    carry = carry + jnp.sum(onehot, axis=1, keepdims=True)
    carry_scr[...] = carry
    cnt_ref[...] = carry.astype(I32)

    sub = lax.broadcasted_iota(I32, (8, tm), 0)
    info_ref[...] = jnp.where(sub == 0, cls, jnp.where(sub == 1, rank, 0))
    sub = lax.broadcasted_iota(I32, (LANES, tm), 0)
    weights_t = jnp.where(sub == 0, cw_lo, jnp.where(sub == 1, cw_hi, 0.0))
    h2x_ref[:, d:] = weights_t.T


def _merge(oa, ob, sga, sgb, x, mod_l, woa, wob, wout, g2, rwh, rwl, rb, tm):
    b, s, d = x.shape
    hw = oa.shape[-1]
    nt = s // tm
    tok = lambda n: pl.BlockSpec((None, tm, n), lambda bb, i: (bb, i, 0))
    const = lambda r, c: pl.BlockSpec((r, c), lambda bb, i: (0, 0))
    return pl.pallas_call(
        _merge_kernel,
        grid=(b, nt),
        in_specs=[tok(hw), tok(hw), tok(d), tok(d), tok(d),
                  pl.BlockSpec((None, N_MOD, d), lambda bb, i: (bb, 0, 0)),
                  const(hw, d), const(hw, d), const(d, d), const(1, d),
                  const(d, LANES), const(d, LANES), const(1, LANES)],
        out_specs=[tok(d), tok(d + LANES),
                   pl.BlockSpec((8, tm), lambda bb, i: (0, bb * nt + i)),
                   const(N_CLASS_PAD, LANES)],
        out_shape=[jax.ShapeDtypeStruct((b, s, d), F32),
                   jax.ShapeDtypeStruct((b, s, d + LANES), F32),
                   jax.ShapeDtypeStruct((8, b * s), I32),
                   jax.ShapeDtypeStruct((N_CLASS_PAD, LANES), I32)],
        scratch_shapes=[pltpu.VMEM((N_CLASS_PAD, LANES), F32)],
        compiler_params=pltpu.CompilerParams(
            dimension_semantics=("arbitrary", "arbitrary"), vmem_limit_bytes=VMEM_LIMIT),
        name="merge_route",
    )(oa, ob, sga, sgb, x, mod_l, woa, wob, wout, g2, rwh, rwl, rb)


def _positions_kernel(offs_ref, info_ref, pos_ref):
    cls = info_ref[0:1, :]
    pos = info_ref[1:2, :]
    for c in range(N_CLASSES):
        pos = pos + jnp.where(cls == c, offs_ref[c], 0)
    pos_ref[...] = jnp.broadcast_to(pos, pos_ref.shape)


def _positions(info, offs):
    return pl.pallas_call(
        _positions_kernel,
        in_specs=[pl.BlockSpec(memory_space=pltpu.SMEM), pl.BlockSpec(info.shape, lambda: (0, 0))],
        out_specs=pl.BlockSpec(info.shape, lambda: (0, 0)),
        out_shape=jax.ShapeDtypeStruct(info.shape, I32),
        name="positions",
    )(offs, info)[0]


def _dispatch_kernel(pos_ref, src_ref, init_ref, dst_ref, sem):
    del init_ref
    tm = pos_ref.shape[1]

    def issue(r, carry):
        pltpu.make_async_copy(src_ref.at[pl.ds(r, 1), :], dst_ref.at[pl.ds(pos_ref[0, r], 1), :], sem).start()
        return carry

    lax.fori_loop(0, tm, issue, 0, unroll=8)
    pltpu.make_async_copy(src_ref, dst_ref.at[pl.ds(0, tm), :], sem).wait()


def _dispatch(h2x, pos, n_rows):
    t, w = h2x.shape
    tm = MOE_DMA_TILE
    smem_row = pl.BlockSpec((None, 1, tm), lambda i: (i, 0, 0), memory_space=pltpu.SMEM)
    return pl.pallas_call(
        _dispatch_kernel,
        grid=(t // tm,),
        in_specs=[smem_row, pl.BlockSpec((tm, w), lambda i: (i, 0)), pl.BlockSpec(memory_space=pl.ANY)],
        out_specs=pl.BlockSpec(memory_space=pl.ANY),
        out_shape=jax.ShapeDtypeStruct((n_rows, w), F32),
        scratch_shapes=[pltpu.SemaphoreType.DMA],
        input_output_aliases={2: 0},
        compiler_params=pltpu.CompilerParams(
            dimension_semantics=("arbitrary",), vmem_limit_bytes=VMEM_LIMIT),
        name="dispatch",
    )(pos, h2x, jnp.zeros((n_rows, w), F32))


def _experts_kernel(elo_ref, ehi_ref, live_ref, xs_ref, wg_lo, wg_hi, wu_lo, wu_hi, wd_lo, wd_hi, ys_ref):
    del elo_ref, ehi_ref
    d = ys_ref.shape[1]
    j = pl.program_id(0)

    @pl.when(live_ref[j] == 1)
    def _():
        h = xs_ref[:, :d].astype(BF16)

        def expert(wg, wu, wd, weight):
            gate = _dot(h, wg[...])
            hidden = ((gate * jax.nn.sigmoid(gate)) * _dot(h, wu[...])) * weight
            return _dot(hidden.astype(BF16), wd[...])

        ys_ref[...] = (expert(wg_lo, wu_lo, wd_lo, xs_ref[:, d:d + 1])
                       + expert(wg_hi, wu_hi, wd_hi, xs_ref[:, d + 1:d + 2]))

    @pl.when(live_ref[j] == 0)
    def _():
        ys_ref[...] = jnp.zeros(ys_ref.shape, F32)


def _experts(xs, e_lo, e_hi, live, wg, wu, wd):
    n_rows, w = xs.shape
    ne, d, f = wg.shape
    tm = MOE_ROW_TILE
    lo = lambda j, elo, ehi, lv: (elo[j], 0, 0)
    hi = lambda j, elo, ehi, lv: (ehi[j], 0, 0)
    grid_spec = pltpu.PrefetchScalarGridSpec(
        num_scalar_prefetch=3,
        grid=(n_rows // tm,),
        in_specs=[pl.BlockSpec((tm, w), lambda j, elo, ehi, lv: (j, 0)),
                  pl.BlockSpec((None, d, f), lo), pl.BlockSpec((None, d, f), hi),
                  pl.BlockSpec((None, d, f), lo), pl.BlockSpec((None, d, f), hi),
                  pl.BlockSpec((None, f, d), lo), pl.BlockSpec((None, f, d), hi)],
        out_specs=pl.BlockSpec((tm, d), lambda j, elo, ehi, lv: (j, 0)),
    )
    return pl.pallas_call(
        _experts_kernel,
        grid_spec=grid_spec,
        out_shape=jax.ShapeDtypeStruct((n_rows, d), F32),
        compiler_params=pltpu.CompilerParams(
            dimension_semantics=("arbitrary",), vmem_limit_bytes=VMEM_LIMIT),
        name="experts",
    )(e_lo, e_hi, live, xs, wg, wg, wu, wu, wd, wd)


def _combine_kernel(pos_ref, ys_ref, x1_ref, mod_ref, o_ref, buf, sem):
    tm = o_ref.shape[0]

    def issue(r, carry):
        pltpu.make_async_copy(ys_ref.at[pl.ds(pos_ref[0, r], 1), :], buf.at[pl.ds(r, 1), :], sem).start()
        return carry

    lax.fori_loop(0, tm, issue, 0, unroll=8)
    pltpu.make_async_copy(ys_ref.at[pl.ds(0, tm), :], buf, sem).wait()
    o_ref[...] = x1_ref[...] + mod_ref[5:6, :] * buf[...]


def _combine(ys, pos, x1, mod_l):
    b, s, d = x1.shape
    tm = MOE_DMA_TILE
    nt = s // tm
    smem_row = pl.BlockSpec((None, 1, tm), lambda bb, i: (bb * nt + i, 0, 0), memory_space=pltpu.SMEM)
    return pl.pallas_call(
        _combine_kernel,
        grid=(b, nt),
        in_specs=[smem_row,
                  pl.BlockSpec(memory_space=pl.ANY),
                  pl.BlockSpec((None, tm, d), lambda bb, i: (bb, i, 0)),
                  pl.BlockSpec((None, N_MOD, d), lambda bb, i: (bb, 0, 0))],
        out_specs=pl.BlockSpec((None, tm, d), lambda bb, i: (bb, i, 0)),
        out_shape=jax.ShapeDtypeStruct((b, s, d), F32),
        scratch_shapes=[pltpu.VMEM((tm, d), F32), pltpu.SemaphoreType.DMA],
        compiler_params=pltpu.CompilerParams(
            dimension_semantics=("arbitrary", "arbitrary"), vmem_limit_bytes=VMEM_LIMIT),
        name="combine",
    )(pos, ys, x1, mod_l)


def _moe(h2x, info, counts, wg, wu, wd, x1, mod_l):
    b, s, d = x1.shape
    t = b * s
    tm = MOE_ROW_TILE
    n_tiles = t // tm + N_CLASSES
    counts = counts[:N_CLASSES, 0]
    padded = (counts + tm - 1) // tm * tm
    ends = jnp.cumsum(padded)
    offs = jnp.pad(ends - padded, (0, N_CLASS_PAD - N_CLASSES)).astype(I32)
    tile_cls = jnp.sum(jnp.arange(n_tiles, dtype=I32)[:, None] * tm >= ends[None, :], axis=1)
    live = (tile_cls < N_CLASSES).astype(I32)
    tile_cls = jnp.minimum(tile_cls, N_CLASSES - 1)
    group, pair = tile_cls // N_PAIRS, tile_cls % N_PAIRS
    pairs = jnp.asarray(_PAIRS, I32)
    e_lo = group * EXPERTS_PER_GROUP + pairs[pair, 0]
    e_hi = group * EXPERTS_PER_GROUP + pairs[pair, 1]

    pos = _positions(info, offs).reshape(t // MOE_DMA_TILE, 1, MOE_DMA_TILE)
    xs = _dispatch(h2x.reshape(t, d + LANES), pos, n_tiles * tm)
    ys = _experts(xs, e_lo, e_hi, live, wg, wu, wd)
    return _combine(ys, pos, x1, mod_l)


def kernel(x, c, w_ada, b_ada, norm1_g, w_in, q_norm_g, k_norm_g, rel_bias, w_o_a, w_o_b, w_out,
           norm2_g, router_w, router_b, w_gate, w_up, w_down):
    b, s, d = x.shape
    depth = w_ada.shape[0]
    assert d % LANES == 0
    tm = min(512, s)

    mod = _adaln(c, w_ada, b_ada)
    bias = _bias_tiles(rel_bias, DSA_QUERY_BLOCK, DSA_KEY_BLOCK)
    rw = jnp.pad(router_w, ((0, 0), (0, LANES - N_EXPERTS)))
    rwh = rw.astype(BF16)
    rwl = (rw - rwh.astype(F32)).astype(BF16)
    rb = jnp.pad(router_b, (0, LANES - N_EXPERTS)).reshape(1, LANES)

    for l in range(depth):
        qa, ka, va, qbt, qit, kb, vt, ki, wit, sga, sgb = _in_proj(
            x, mod[l], norm1_g[l].reshape(1, d), _pack_w_in(w_in[l]),
            q_norm_g[l].reshape(HEAD_DIM, 1), k_norm_g[l].reshape(1, HEAD_DIM), tm)
        oa = _sb_attention(qa, ka, va)
        ob = _dsa_attention(qit, ki, wit, qbt, kb, vt, bias)
        x1, h2x, info, counts = _merge(oa, ob, sga, sgb, x, mod[l], w_o_a[l].astype(BF16),
                                       w_o_b[l].astype(BF16), w_out[l].astype(BF16),
                                       norm2_g[l].reshape(1, d), rwh, rwl, rb, tm)
        x = _moe(h2x, info, counts, w_gate[l].astype(BF16), w_up[l].astype(BF16),
                 w_down[l].astype(BF16), x1, mod[l])
    return x
```

```python
import functools
import math

import numpy as np
import jax
import jax.numpy as jnp
from jax import lax
from jax.experimental import pallas as pl
from jax.experimental.pallas import tpu as pltpu

F32 = jnp.float32
BF16 = jnp.bfloat16
I32 = jnp.int32
I16 = jnp.int16

HEAD_DIM = 64
N_HEADS = 8
N_EXPERTS = 16
N_GROUPS = 4
EXPERTS_PER_GROUP = 4
_PAIRS = ((0, 1), (0, 2), (0, 3), (1, 2), (1, 3), (2, 3))
N_PAIRS = len(_PAIRS)
N_CLASSES = N_GROUPS * N_PAIRS
N_CLASS_PAD = 32
MOE_ROW_TILE = 256
MOE_DMA_TILE = 512
N_MOD = 6
EPS = 1e-6
INDEX_TOPK_MAX = 256
REL_BUCKETS = 32
REL_MAX_DIST = 128
LANES = 128
DSA_QUERY_BLOCK = 256
DSA_KEY_BLOCK = 128
SB_QUERY_BLOCK = 256
SB_KEY_BLOCK = 256
SB_DEAD_LOG2 = 150.0
LOG2E = 1.4426950408889634
NEG_BIG = -1e30
INT_MIN = -2 ** 31
HALF_BIAS = 2 ** 15

VMEM_LIMIT = 56 * 1024 * 1024


def _nt_dot(a, b):
    return lax.dot_general(a, b, (((1,), (1,)), ((), ())), preferred_element_type=F32)


def _dot(a, b):
    return jnp.dot(a, b, preferred_element_type=F32)


def _mod_kernel(c_ref, w_ref, b_ref, o_ref):
    c = c_ref[...]
    sc = c * jax.nn.sigmoid(c)
    o_ref[...] = jnp.dot(sc, w_ref[...], preferred_element_type=F32,
                         precision=lax.Precision.HIGHEST) + b_ref[...]


def _adaln(c, w_ada, b_ada):
    depth, d, nd = w_ada.shape
    b = c.shape[0]
    out = pl.pallas_call(
        _mod_kernel,
        grid=(depth, nd // d),
        in_specs=[pl.BlockSpec((b, d), lambda l, j: (0, 0)),
                  pl.BlockSpec((None, d, d), lambda l, j: (l, 0, j)),
                  pl.BlockSpec((None, 1, d), lambda l, j: (l, 0, j))],
        out_specs=pl.BlockSpec((None, b, d), lambda l, j: (l, 0, j)),
        out_shape=jax.ShapeDtypeStruct((depth, b, nd), F32),
        compiler_params=pltpu.CompilerParams(vmem_limit_bytes=VMEM_LIMIT),
        name="adaln",
    )(c, w_ada, b_ada.reshape(depth, 1, nd))
    return out.reshape(depth, b, N_MOD, d)


def _rms(x):
    return x * lax.rsqrt(jnp.mean(x * x, axis=-1, keepdims=True) + EPS)


def _in_kernel(x_ref, mod_ref, g1_ref, w_ref, qng_ref, kng_ref,
               qa_ref, ka_ref, va_ref, qbt_ref, qit_ref, kb_ref, vt_ref, ki_ref, wit_ref,
               sga_ref, sgb_ref):
    d = x_ref.shape[-1]
    tm = x_ref.shape[0]
    hw = N_HEADS * HEAD_DIM
    x = x_ref[...]
    h = (_rms(x) * g1_ref[...]) * (1.0 + mod_ref[1:2, :]) + mod_ref[0:1, :]
    hb = h.astype(BF16)

    def proj(lo, n):
        return _dot(hb, w_ref[:, lo:lo + n])

    def heads(r, ref):
        for hh in range(N_HEADS):
            ref[hh] = r[:, hh * HEAD_DIM:(hh + 1) * HEAD_DIM].astype(ref.dtype)

    scale = HEAD_DIM ** -0.5
    heads(proj(0, hw) * (scale * LOG2E), qa_ref)
    heads(proj(hw, hw), ka_ref)
    heads(proj(2 * hw, hw), va_ref)

    rt = proj(3 * hw, hw).T
    for hh in range(N_HEADS):
        slab = rt[hh * HEAD_DIM:(hh + 1) * HEAD_DIM, :]
        inv = lax.rsqrt(jnp.mean(slab * slab, axis=0, keepdims=True) + EPS)
        qbt_ref[hh] = (((slab * inv) * qng_ref[...]) * scale).astype(BF16)

    r = proj(4 * hw, 2 * HEAD_DIM)
    kb_ref[...] = (_rms(r[:, :HEAD_DIM]) * kng_ref[...]).astype(BF16)
    vt = r.T[HEAD_DIM:, :]
    ones_row = (lax.broadcasted_iota(I32, (HEAD_DIM, LANES), 0) == 0).astype(BF16)
    for cc in range(tm // LANES):
        vt_ref[cc, :HEAD_DIM, :] = vt[:, cc * LANES:(cc + 1) * LANES].astype(BF16)
        vt_ref[cc, HEAD_DIM:, :] = ones_row

    rt = proj(4 * hw + LANES, hw).T
    for hh in range(N_HEADS):
        qit_ref[hh] = rt[hh * HEAD_DIM:(hh + 1) * HEAD_DIM, :].astype(BF16)

    r = proj(5 * hw + LANES, LANES)
    ki_ref[...] = r[:, :HEAD_DIM].astype(BF16)
    wit_ref[...] = r.T[HEAD_DIM:HEAD_DIM + N_HEADS, :]

    off = 5 * hw + 2 * LANES
    sga_ref[...] = jax.nn.sigmoid(proj(off, d)).astype(BF16)
    sgb_ref[...] = jax.nn.sigmoid(proj(off + d, d)).astype(BF16)


def _pack_w_in(w_in_l):
    hw = N_HEADS * HEAD_DIM
    a = 4 * hw + 2 * HEAD_DIM + hw
    small = w_in_l[:, a:a + HEAD_DIM + N_HEADS]
    small = jnp.pad(small, ((0, 0), (0, LANES - small.shape[1])))
    gates = w_in_l[:, a + HEAD_DIM + N_HEADS:]
    return jnp.concatenate([w_in_l[:, :a], small, gates], axis=1).astype(BF16)


def _in_proj(x, mod_l, g1, w_packed, qng, kng, tm):
    b, s, d = x.shape
    hm = jax.ShapeDtypeStruct((b, N_HEADS, s, HEAD_DIM), BF16)
    hmt = jax.ShapeDtypeStruct((b, N_HEADS, HEAD_DIM, s), BF16)
    tok64 = jax.ShapeDtypeStruct((b, s, HEAD_DIM), BF16)
    hm_spec = pl.BlockSpec((None, N_HEADS, tm, HEAD_DIM), lambda bb, i: (bb, 0, i, 0))
    hmt_spec = pl.BlockSpec((None, N_HEADS, HEAD_DIM, tm), lambda bb, i: (bb, 0, 0, i))
    tok = lambda n: pl.BlockSpec((None, tm, n), lambda bb, i: (bb, i, 0))
    nw = w_packed.shape[1]
    return pl.pallas_call(
        _in_kernel,
        grid=(b, s // tm),
        in_specs=[tok(d),
                  pl.BlockSpec((None, N_MOD, d), lambda bb, i: (bb, 0, 0)),
                  pl.BlockSpec((1, d), lambda bb, i: (0, 0)),
                  pl.BlockSpec((d, nw), lambda bb, i: (0, 0)),
                  pl.BlockSpec((HEAD_DIM, 1), lambda bb, i: (0, 0)),
                  pl.BlockSpec((1, HEAD_DIM), lambda bb, i: (0, 0))],
        out_specs=[hm_spec] * 3 + [hmt_spec] * 2
                  + [tok(HEAD_DIM),
                     pl.BlockSpec((None, tm // LANES, 2 * HEAD_DIM, LANES), lambda bb, i: (bb, i, 0, 0)),
                     tok(HEAD_DIM),
                     pl.BlockSpec((None, N_HEADS, tm), lambda bb, i: (bb, 0, i)),
                     tok(d), tok(d)],
        out_shape=[hm] * 3 + [hmt] * 2
                  + [tok64,
                     jax.ShapeDtypeStruct((b, s // LANES, 2 * HEAD_DIM, LANES), BF16),
                     tok64,
                     jax.ShapeDtypeStruct((b, N_HEADS, s), F32),
                     jax.ShapeDtypeStruct((b, s, d), BF16),
                     jax.ShapeDtypeStruct((b, s, d), BF16)],
        compiler_params=pltpu.CompilerParams(
            dimension_semantics=("parallel", "parallel"), vmem_limit_bytes=VMEM_LIMIT),
        name="in_proj",
    )(x, mod_l, g1, w_packed, qng, kng)


def _sb_kernel(q_ref, k_ref, v_ref, o_ref, acc_scr, csum_scr, sp_scr, ls_scr):
    tq = q_ref.shape[1]
    tk = SB_KEY_BLOCK
    i = pl.program_id(1)
    nfull = (i * tq) // tk
    row = lax.broadcasted_iota(I32, (tq, tk), 0)
    col = lax.broadcasted_iota(I32, (tq, tk), 1)
    before = (col - row) < (i * tq - nfull * tk)
    krow = lax.broadcasted_iota(I32, (tk, tk), 0)
    kcol = lax.broadcasted_iota(I32, (tk, tk), 1)
    suffix = (krow > kcol).astype(BF16)

    def stage_a(hh, st, first):
        z2 = _nt_dot(q_ref[hh], k_ref[hh, pl.ds(st, tk), :])
        sp2 = jnp.maximum(jnp.log2(1.0 + jnp.exp2(jnp.minimum(z2, 126.0))), z2)
        if first:
            sp2 = jnp.where(before, sp2, 0.0)
        sp_scr[hh] = sp2.astype(BF16)
        ls_scr[hh] = z2 - sp2

    def stage_b(hh, st, first):
        sp = sp_scr[hh]
        suf = _dot(sp, suffix)
        if first:
            a = jnp.where(before, jnp.exp2(ls_scr[hh] - suf), 0.0)
            csum = jnp.zeros((tq, 1), F32)
        else:
            csum = csum_scr[hh]
            a = jnp.exp2((ls_scr[hh] - suf) - csum)
        c = _dot(a.astype(BF16), v_ref[hh, pl.ds(st, tk), :])
        if first:
            acc_scr[hh] = c
        else:
            acc_scr[hh] += c
        csum_scr[hh] = csum + (suf[:, 0:1] + sp[:, 0:1].astype(F32))

    def key_block(st, first):
        for hh in range(N_HEADS):
            stage_a(hh, st, first)
        for hh in range(N_HEADS):
            stage_b(hh, st, first)

    key_block(pl.multiple_of(nfull * tk, tk), True)

    def live():
        return jnp.min(csum_scr[...]) < SB_DEAD_LOG2

    def body(carry):
        jj, _ = carry
        key_block(pl.multiple_of((nfull - 1 - jj) * tk, tk), False)
        return jj + 1, live()

    lax.while_loop(lambda carry: (carry[0] < nfull) & carry[1], body, (jnp.int32(0), live()))
    for hh in range(N_HEADS):
        o_ref[:, hh * HEAD_DIM:(hh + 1) * HEAD_DIM] = acc_scr[hh].astype(o_ref.dtype)


def _sb_attention(qa, ka, va):
    b, nh, s, dh = qa.shape
    tq = SB_QUERY_BLOCK
    assert SB_KEY_BLOCK % tq == 0 and s % SB_KEY_BLOCK == 0
    return pl.pallas_call(
        _sb_kernel,
        grid=(b, s // tq),
        in_specs=[pl.BlockSpec((None, nh, tq, dh), lambda bb, i: (bb, 0, i, 0)),
                  pl.BlockSpec((None, nh, s, dh), lambda bb, i: (bb, 0, 0, 0)),
                  pl.BlockSpec((None, nh, s, dh), lambda bb, i: (bb, 0, 0, 0))],
        out_specs=pl.BlockSpec((None, tq, nh * dh), lambda bb, i: (bb, i, 0)),
        out_shape=jax.ShapeDtypeStruct((b, s, nh * dh), BF16),
        scratch_shapes=[pltpu.VMEM((nh, tq, dh), F32), pltpu.VMEM((nh, tq, 1), F32),
                        pltpu.VMEM((nh, tq, SB_KEY_BLOCK), BF16), pltpu.VMEM((nh, tq, SB_KEY_BLOCK), F32)],
        compiler_params=pltpu.CompilerParams(
            dimension_semantics=("parallel", "parallel"), vmem_limit_bytes=VMEM_LIMIT),
        name="stick_breaking",
    )(qa, ka, va)


def _t5_bucket_np(dist):
    max_exact = REL_BUCKETS // 2
    d_f = np.maximum(dist, 1).astype(np.float32)
    large = max_exact + (np.log(d_f / np.float32(max_exact))
                         / np.float32(math.log(REL_MAX_DIST / max_exact))
                         * np.float32(REL_BUCKETS - max_exact)).astype(np.int32)
    large = np.minimum(large, REL_BUCKETS - 1)
    return np.where(dist < max_exact, dist, large).astype(np.int32)


def _bias_tiles(rel_bias, tq, tk):
    n_near = tq // tk + 1
    key = np.arange(tk)[:, None]
    query = np.arange(tq)[None, :]
    bk = np.stack([_t5_bucket_np(np.maximum(query - key - (n - 1) * tk, 0)) for n in range(n_near)])
    far = int(_t5_bucket_np(np.array([REL_MAX_DIST]))[0])
    nh = rel_bias.shape[1]

    def body(rb_ref, bk_ref, o_ref):
        hh = pl.program_id(1)
        buckets = bk_ref[...]
        tile = jnp.zeros((tk, tq), F32)
        for bucket in range(REL_BUCKETS):
            tile = jnp.where(buckets == bucket, rb_ref[hh, bucket] - rb_ref[hh, far], tile)
        o_ref[...] = tile

    return pl.pallas_call(
        body,
        grid=(n_near, nh),
        in_specs=[pl.BlockSpec(memory_space=pltpu.SMEM),
                  pl.BlockSpec((None, tk, tq), lambda n, hh: (n, 0, 0))],
        out_specs=pl.BlockSpec((None, None, tk, tq), lambda n, hh: (n, hh, 0, 0)),
        out_shape=jax.ShapeDtypeStruct((n_near, nh, tk, tq), F32),
        name="bias_tiles",
    )(rel_bias.T, jnp.asarray(bk))


def _sort_key(x):
    bits = pltpu.bitcast(x, I32)
    return bits ^ ((bits >> 31) & 0x7FFFFFFF)


def _fold_rows(x, op):
    return functools.reduce(op, [x[g * 8:(g + 1) * 8, :] for g in range(x.shape[0] // 8)])


def _dsa_kernel(ktop, qit_ref, ki_ref, wit_ref, qbt_ref, kb_ref, vt_ref, bias_ref, o_ref,
                key_scr, hi_scr, lo_scr, m_scr, acc_scr, s_scr):
    tq = qit_ref.shape[2]
    tk = DSA_KEY_BLOCK
    per_q = tq // tk
    i = pl.program_id(1)
    first_own = per_q * i
    nchunks = per_q * (i + 1)
    key_i = lax.broadcasted_iota(I32, (tk, tq), 0)
    qry_i = lax.broadcasted_iota(I32, (tk, tq), 1)

    def causal(rel):
        return (key_i - qry_i) <= (-rel * tk)

    def for_chunks(fn):
        def far(j, carry):
            for u in range(per_q):
                fn(j * per_q + u, None)
            return carry
        lax.fori_loop(0, i - 1, far, 0)

        @pl.when(i >= 1)
        def _():
            for rel in range(-per_q, -1):
                fn(first_own + rel, None)
            fn(first_own - 1, -1)

        for rel in range(per_q):
            fn(first_own + rel, rel)

    def score_chunk(c, rel):
        kic = ki_ref[pl.ds(pl.multiple_of(c * tk, tk), tk), :]
        score = jnp.zeros((tk, tq), F32)
        for hh in range(N_HEADS):
            score = score + wit_ref[hh:hh + 1, :] * jnp.maximum(_dot(kic, qit_ref[hh]), 0.0)
        keys = _sort_key(score)
        if rel is not None and rel >= 0:
            keys = jnp.where(causal(rel), keys, INT_MIN)
        key_scr[c] = keys
        hi_scr[c] = (keys >> 16).astype(I16)
        lo_scr[c] = ((keys & 0xFFFF) - HALF_BIAS).astype(I16)

    for_chunks(score_chunk)

    def count(pred):
        def body(j, part):
            for u in range(per_q):
                part = part + _fold_rows(jnp.where(pred(key_scr[j * per_q + u]), 1, 0), jnp.add)
            return part
        part = lax.fori_loop(0, i + 1, body, jnp.zeros((8, tq), I32))
        return jnp.sum(part, axis=0, keepdims=True)

    def count16(scr, pred):
        one, zero = jnp.int16(1), jnp.int16(0)

        def body(j, part):
            for u in range(per_q):
                hit = jnp.where(pred(scr[j * per_q + u]), one, zero)
                part = part + functools.reduce(
                    jnp.add, [hit[g * 16:(g + 1) * 16, :] for g in range(tk // 16)])
            return part
        part = lax.fori_loop(0, i + 1, body, jnp.zeros((16, tq), I16))
        return jnp.sum(part.astype(I32), axis=0, keepdims=True)

    def search16(scr, base):
        def bit_body(bi, t_u):
            cand_u = t_u | jnp.left_shift(jnp.int32(1), 15 - bi)
            cand = (cand_u - HALF_BIAS).astype(I16)
            return jnp.where(base + count16(scr, lambda k: k >= cand) >= ktop, cand_u, t_u)
        return lax.fori_loop(0, 16, bit_body, jnp.zeros((1, tq), I32))

    t_hi = search16(hi_scr, 0) - HALF_BIAS
    t_hi16 = t_hi.astype(I16)
    above = count16(hi_scr, lambda k: k > t_hi16)

    def keep_band(j, carry):
        for u in range(per_q):
            c = j * per_q + u
            lo_scr[c] = jnp.where(hi_scr[c] == t_hi16, lo_scr[c], jnp.int16(-HALF_BIAS))
        return carry

    lax.fori_loop(0, i + 1, keep_band, 0)
    thr = t_hi * (2 * HALF_BIAS) + search16(lo_scr, above)
    n_ge = count(lambda k: k >= thr)

    @pl.when(jnp.max(n_ge) > ktop)
    def _():
        need = (ktop - count(lambda k: k > thr)).astype(F32)
        lower = (lax.broadcasted_iota(I32, (tk, tk), 1)
                 < lax.broadcasted_iota(I32, (tk, tk), 0)).astype(BF16)

        def body(c, seen):
            k = key_scr[c]
            eq = k == thr
            eqf = eq.astype(BF16)
            rank = seen + _dot(lower, eqf)
            drop = eq & (rank >= need) & (n_ge > ktop)
            key_scr[c] = jnp.where(drop, thr - 1, k)
            return seen + jnp.sum(eqf.astype(F32), axis=0, keepdims=True)

        lax.fori_loop(0, nchunks, body, jnp.zeros((1, tq), F32))

    def logits(c, rel):
        sel = key_scr[c] >= thr
        if rel is not None and rel >= 0:
            sel = sel & causal(rel)
        kbc = kb_ref[pl.ds(pl.multiple_of(c * tk, tk), tk), :]

        def head(hh):
            s = _dot(kbc, qbt_ref[hh])
            if rel is not None:
                s = s + bias_ref[rel + 1, hh]
            return s
        return sel, head

    m_scr[...] = jnp.full(m_scr.shape, NEG_BIG, F32)

    def max_chunk(c, rel):
        sel, head = logits(c, rel)
        for hh in range(N_HEADS):
            s = jnp.where(sel, head(hh), NEG_BIG)
            s_scr[hh, c] = s
            m_scr[hh] = jnp.maximum(m_scr[hh], _fold_rows(s, jnp.maximum))

    for_chunks(max_chunk)
    m = [jnp.max(m_scr[hh], axis=0, keepdims=True) for hh in range(N_HEADS)]

    acc_scr[...] = jnp.zeros(acc_scr.shape, F32)

    def att_block(j, carry):
        for u in range(per_q):
            c = j * per_q + u
            vtc = vt_ref[c]
            for hh in range(N_HEADS):
                p = jnp.exp(s_scr[hh, c] - m[hh])
                acc_scr[hh] += _dot(vtc, p.astype(BF16))
        return carry

    lax.fori_loop(0, i + 1, att_block, 0)

    out_t = jnp.concatenate(
        [acc_scr[hh, :HEAD_DIM, :] / acc_scr[hh, HEAD_DIM:HEAD_DIM + 1, :] for hh in range(N_HEADS)],
        axis=0)
    o_ref[...] = out_t.T.astype(o_ref.dtype)


def _dsa_attention(qit, ki, wit, qbt, kb, vt, bias):
    b, nh, dh, s = qit.shape
    tq = DSA_QUERY_BLOCK
    tk = DSA_KEY_BLOCK
    assert tk == REL_MAX_DIST and tk == LANES and tq % tk == 0 and s % tq == 0
    ktop = min(INDEX_TOPK_MAX, s // 4)
    hmt_q = pl.BlockSpec((None, nh, dh, tq), lambda bb, i: (bb, 0, 0, i))
    full64 = pl.BlockSpec((None, s, dh), lambda bb, i: (bb, 0, 0))
    return pl.pallas_call(
        functools.partial(_dsa_kernel, ktop),
        grid=(b, s // tq),
        in_specs=[hmt_q, full64,
                  pl.BlockSpec((None, nh, tq), lambda bb, i: (bb, 0, i)),
                  hmt_q, full64,
                  pl.BlockSpec((None, s // tk, 2 * dh, tk), lambda bb, i: (bb, 0, 0, 0)),
                  pl.BlockSpec(bias.shape, lambda bb, i: (0, 0, 0, 0))],
        out_specs=pl.BlockSpec((None, tq, nh * dh), lambda bb, i: (bb, i, 0)),
        out_shape=jax.ShapeDtypeStruct((b, s, nh * dh), BF16),
        scratch_shapes=[pltpu.VMEM((s // tk, tk, tq), I32),
                        pltpu.VMEM((s // tk, tk, tq), I16),
                        pltpu.VMEM((s // tk, tk, tq), I16),
                        pltpu.VMEM((nh, 8, tq), F32),
                        pltpu.VMEM((nh, 2 * dh, tq), F32),
                        pltpu.VMEM((nh, s // tk, tk, tq), F32)],
        compiler_params=pltpu.CompilerParams(
            dimension_semantics=("parallel", "parallel"), vmem_limit_bytes=VMEM_LIMIT),
        name="sparse_attention",
    )(qit, ki, wit, qbt, kb, vt, bias)


def _first_max4(a):
    m1 = jnp.maximum(jnp.maximum(a[0], a[1]), jnp.maximum(a[2], a[3]))
    i1 = jnp.where(a[0] == m1, 0, jnp.where(a[1] == m1, 1, jnp.where(a[2] == m1, 2, 3)))
    rest = [jnp.where(i1 == j, -1.0, a[j]) for j in range(4)]
    m2 = jnp.maximum(jnp.maximum(rest[0], rest[1]), jnp.maximum(rest[2], rest[3]))
    i2 = jnp.where(rest[0] == m2, 0, jnp.where(rest[1] == m2, 1, jnp.where(rest[2] == m2, 2, 3)))
    return m1, i1, m2, i2


def _route(lt):
    lg = [lt[e:e + 1, :] for e in range(N_EXPERTS)]
    mx = functools.reduce(jnp.maximum, lg)
    ex = [jnp.exp(v - mx) for v in lg]
    tot = functools.reduce(lambda u, v: u + v, ex)
    p = [v / tot for v in ex]
    gs = []
    for g in range(N_GROUPS):
        m1, _, m2, _ = _first_max4(p[4 * g:4 * g + 4])
        gs.append(m1 + m2)
    _, gbest, _, _ = _first_max4(gs)
    chosen = [jnp.where(gbest == 0, p[j], jnp.where(gbest == 1, p[4 + j],
              jnp.where(gbest == 2, p[8 + j], p[12 + j]))) for j in range(4)]
    m1, i1, m2, i2 = _first_max4(chosen)
    den = m1 + m2
    lo = jnp.minimum(i1, i2)
    hi = jnp.maximum(i1, i2)
    pair = jnp.where(lo == 0, hi - 1, jnp.where(lo == 1, hi + 1, N_PAIRS - 1))
    first_is_lo = i1 < i2
    w1 = m1 / den
    w2 = m2 / den
    return gbest * N_PAIRS + pair, jnp.where(first_is_lo, w1, w2), jnp.where(first_is_lo, w2, w1)


def _merge_kernel(oa_ref, ob_ref, sga_ref, sgb_ref, x_ref, mod_ref, woa_ref, wob_ref, wout_ref,
                  g2_ref, rwh_ref, rwl_ref, rb_ref, x1_ref, h2x_ref, info_ref, cnt_ref, carry_scr):
    tm, d = x_ref.shape

    @pl.when((pl.program_id(0) == 0) & (pl.program_id(1) == 0))
    def _():
        carry_scr[...] = jnp.zeros(carry_scr.shape, F32)

    merged = (sga_ref[...].astype(F32) * _dot(oa_ref[...], woa_ref[...])
              + sgb_ref[...].astype(F32) * _dot(ob_ref[...], wob_ref[...]))
    x1 = x_ref[...] + mod_ref[2:3, :] * _dot(merged.astype(BF16), wout_ref[...])
    x1_ref[...] = x1
    h2 = (_rms(x1) * g2_ref[...]) * (1.0 + mod_ref[4:5, :]) + mod_ref[3:4, :]
    h2x_ref[:, :d] = h2
    hi = h2.astype(BF16)
    lo = (h2 - hi.astype(F32)).astype(BF16)
    logits = (_dot(hi, rwh_ref[...]) + _dot(lo, rwh_ref[...]) + _dot(hi, rwl_ref[...])) + rb_ref[...]
    cls, cw_lo, cw_hi = _route(logits.T)

    ncls = carry_scr.shape[0]
    onehot = (lax.broadcasted_iota(I32, (ncls, tm), 0) == cls).astype(F32)
    earlier = (lax.broadcasted_iota(I32, (tm, tm), 0)
               < lax.broadcasted_iota(I32, (tm, tm), 1)).astype(BF16)
    before = _dot(onehot.astype(BF16), earlier)
    carry = carry_scr[...]
    carry_t = jnp.concatenate([carry] * (tm // LANES), axis=1)
    rank = jnp.sum(onehot * (carry_t + before), axis=0, keepdims=True).astype(I32)
    carry = carry + jnp.sum(onehot, axis=1, keepdims=True)
    carry_scr[...] = carry
    cnt_ref[...] = carry.astype(I32)

    sub = lax.broadcasted_iota(I32, (8, tm), 0)
    info_ref[...] = jnp.where(sub == 0, cls, jnp.where(sub == 1, rank, 0))
    sub = lax.broadcasted_iota(I32, (LANES, tm), 0)
    weights_t = jnp.where(sub == 0, cw_lo, jnp.where(sub == 1, cw_hi, 0.0))
    h2x_ref[:, d:] = weights_t.T


def _merge(oa, ob, sga, sgb, x, mod_l, woa, wob, wout, g2, rwh, rwl, rb, tm):
    b, s, d = x.shape
    hw = oa.shape[-1]
    nt = s // tm
    tok = lambda n: pl.BlockSpec((None, tm, n), lambda bb, i: (bb, i, 0))
    const = lambda r, c: pl.BlockSpec((r, c), lambda bb, i: (0, 0))
    return pl.pallas_call(
        _merge_kernel,
        grid=(b, nt),
        in_specs=[tok(hw), tok(hw), tok(d), tok(d), tok(d),
                  pl.BlockSpec((None, N_MOD, d), lambda bb, i: (bb, 0, 0)),
                  const(hw, d), const(hw, d), const(d, d), const(1, d),
                  const(d, LANES), const(d, LANES), const(1, LANES)],
        out_specs=[tok(d), tok(d + LANES),
                   pl.BlockSpec((8, tm), lambda bb, i: (0, bb * nt + i)),
                   const(N_CLASS_PAD, LANES)],
        out_shape=[jax.ShapeDtypeStruct((b, s, d), F32),
                   jax.ShapeDtypeStruct((b, s, d + LANES), F32),
                   jax.ShapeDtypeStruct((8, b * s), I32),
                   jax.ShapeDtypeStruct((N_CLASS_PAD, LANES), I32)],
        scratch_shapes=[pltpu.VMEM((N_CLASS_PAD, LANES), F32)],
        compiler_params=pltpu.CompilerParams(
            dimension_semantics=("arbitrary", "arbitrary"), vmem_limit_bytes=VMEM_LIMIT),
        name="merge_route",
    )(oa, ob, sga, sgb, x, mod_l, woa, wob, wout, g2, rwh, rwl, rb)


def _positions_kernel(offs_ref, info_ref, pos_ref):
    cls = info_ref[0:1, :]
    pos = info_ref[1:2, :]
    for c in range(N_CLASSES):
        pos = pos + jnp.where(cls == c, offs_ref[c], 0)
    pos_ref[...] = jnp.broadcast_to(pos, pos_ref.shape)


def _positions(info, offs):
    return pl.pallas_call(
        _positions_kernel,
        in_specs=[pl.BlockSpec(memory_space=pltpu.SMEM), pl.BlockSpec(info.shape, lambda: (0, 0))],
        out_specs=pl.BlockSpec(info.shape, lambda: (0, 0)),
        out_shape=jax.ShapeDtypeStruct(info.shape, I32),
        name="positions",
    )(offs, info)[0]


def _dispatch_kernel(ends_ref, pos_ref, src_ref, dst_ref, zero_scr, sem, zero_sem):
    tm = pos_ref.shape[1]
    row_tile = zero_scr.shape[0]

    @pl.when(pl.program_id(0) == 0)
    def _():
        zero_scr[...] = jnp.zeros(zero_scr.shape, F32)

        def zero_tail(c):
            start = pl.multiple_of(ends_ref[c] - row_tile, row_tile)
            return pltpu.make_async_copy(zero_scr, dst_ref.at[pl.ds(start, row_tile), :], zero_sem)

        def nonempty(c):
            return ends_ref[c] > (ends_ref[c - 1] if c else 0)

        for c in range(N_CLASSES):
            @pl.when(nonempty(c))
            def _(c=c):
                zero_tail(c).start()
        for c in range(N_CLASSES):
            @pl.when(nonempty(c))
            def _(c=c):
                zero_tail(c).wait()

        def unused_tile(j):
            start = pl.multiple_of(ends_ref[N_CLASSES - 1] + j * row_tile, row_tile)
            return pltpu.make_async_copy(zero_scr, dst_ref.at[pl.ds(start, row_tile), :], zero_sem)

        n_unused = (dst_ref.shape[0] - ends_ref[N_CLASSES - 1]) // row_tile
        lax.fori_loop(0, n_unused, lambda j, carry: (unused_tile(j).start(), carry)[1], 0)
        lax.fori_loop(0, n_unused, lambda j, carry: (unused_tile(j).wait(), carry)[1], 0)

    for r in range(tm):
        pltpu.make_async_copy(src_ref.at[pl.ds(r, 1), :], dst_ref.at[pl.ds(pos_ref[0, r], 1), :], sem).start()
    pltpu.make_async_copy(src_ref, dst_ref.at[pl.ds(0, tm), :], sem).wait()


def _dispatch(h2x, pos, ends, n_rows):
    t, w = h2x.shape
    tm = MOE_DMA_TILE
    smem_row = pl.BlockSpec((None, 1, tm), lambda i: (i, 0, 0), memory_space=pltpu.SMEM)
    return pl.pallas_call(
        _dispatch_kernel,
        grid=(t // tm,),
        in_specs=[pl.BlockSpec(memory_space=pltpu.SMEM), smem_row, pl.BlockSpec((tm, w), lambda i: (i, 0))],
        out_specs=pl.BlockSpec(memory_space=pl.ANY),
        out_shape=jax.ShapeDtypeStruct((n_rows, w), F32),
        scratch_shapes=[pltpu.VMEM((MOE_ROW_TILE, w), F32), pltpu.SemaphoreType.DMA, pltpu.SemaphoreType.DMA],
        compiler_params=pltpu.CompilerParams(
            dimension_semantics=("arbitrary",), vmem_limit_bytes=VMEM_LIMIT),
        name="dispatch",
    )(ends, pos, h2x)


def _experts_kernel(elo_ref, ehi_ref, live_ref, xs_ref, wg_lo, wg_hi, wu_lo, wu_hi, wd_lo, wd_hi, ys_ref):
    del elo_ref, ehi_ref
    d = ys_ref.shape[1]
    j = pl.program_id(0)

    @pl.when(live_ref[j] == 1)
    def _():
        h = xs_ref[:, :d].astype(BF16)

        def expert(wg, wu, wd, weight):
            gate = _dot(h, wg[...])
            hidden = ((gate * jax.nn.sigmoid(gate)) * _dot(h, wu[...])) * weight
            return _dot(hidden.astype(BF16), wd[...])

        ys_ref[...] = (expert(wg_lo, wu_lo, wd_lo, xs_ref[:, d:d + 1])
                       + expert(wg_hi, wu_hi, wd_hi, xs_ref[:, d + 1:d + 2]))

    @pl.when(live_ref[j] == 0)
    def _():
        ys_ref[...] = jnp.zeros(ys_ref.shape, F32)


def _experts(xs, e_lo, e_hi, live, wg, wu, wd):
    n_rows, w = xs.shape
    ne, d, f = wg.shape
    tm = MOE_ROW_TILE
    lo = lambda j, elo, ehi, lv: (elo[j], 0, 0)
    hi = lambda j, elo, ehi, lv: (ehi[j], 0, 0)
    grid_spec = pltpu.PrefetchScalarGridSpec(
        num_scalar_prefetch=3,
        grid=(n_rows // tm,),
        in_specs=[pl.BlockSpec((tm, w), lambda j, elo, ehi, lv: (j, 0)),
                  pl.BlockSpec((None, d, f), lo), pl.BlockSpec((None, d, f), hi),
                  pl.BlockSpec((None, d, f), lo), pl.BlockSpec((None, d, f), hi),
                  pl.BlockSpec((None, f, d), lo), pl.BlockSpec((None, f, d), hi)],
        out_specs=pl.BlockSpec((tm, d), lambda j, elo, ehi, lv: (j, 0)),
    )
    return pl.pallas_call(
        _experts_kernel,
        grid_spec=grid_spec,
        out_shape=jax.ShapeDtypeStruct((n_rows, d), F32),
        compiler_params=pltpu.CompilerParams(
            dimension_semantics=("arbitrary",), vmem_limit_bytes=VMEM_LIMIT),
        name="experts",
    )(e_lo, e_hi, live, xs, wg, wg, wu, wu, wd, wd)


def _combine_kernel(pos_ref, ys_ref, x1_ref, mod_ref, o_ref, buf, sem):
    tm = o_ref.shape[0]

    for r in range(tm):
        pltpu.make_async_copy(ys_ref.at[pl.ds(pos_ref[0, r], 1), :], buf.at[pl.ds(r, 1), :], sem).start()
    pltpu.make_async_copy(ys_ref.at[pl.ds(0, tm), :], buf, sem).wait()
    o_ref[...] = x1_ref[...] + mod_ref[5:6, :] * buf[...]


def _combine(ys, pos, x1, mod_l):
    b, s, d = x1.shape
    tm = MOE_DMA_TILE
    nt = s // tm
    smem_row = pl.BlockSpec((None, 1, tm), lambda bb, i: (bb * nt + i, 0, 0), memory_space=pltpu.SMEM)
    return pl.pallas_call(
        _combine_kernel,
        grid=(b, nt),
        in_specs=[smem_row,
                  pl.BlockSpec(memory_space=pl.ANY),
                  pl.BlockSpec((None, tm, d), lambda bb, i: (bb, i, 0)),
                  pl.BlockSpec((None, N_MOD, d), lambda bb, i: (bb, 0, 0))],
        out_specs=pl.BlockSpec((None, tm, d), lambda bb, i: (bb, i, 0)),
        out_shape=jax.ShapeDtypeStruct((b, s, d), F32),
        scratch_shapes=[pltpu.VMEM((tm, d), F32), pltpu.SemaphoreType.DMA],
        compiler_params=pltpu.CompilerParams(
            dimension_semantics=("arbitrary", "arbitrary"), vmem_limit_bytes=VMEM_LIMIT),
        name="combine",
    )(pos, ys, x1, mod_l)


def _moe(h2x, info, counts, wg, wu, wd, x1, mod_l):
    b, s, d = x1.shape
    t = b * s
    tm = MOE_ROW_TILE
    n_tiles = t // tm + N_CLASSES
    counts = counts[:N_CLASSES, 0]
    padded = (counts + tm - 1) // tm * tm
    ends = jnp.cumsum(padded)
    offs = jnp.pad(ends - padded, (0, N_CLASS_PAD - N_CLASSES)).astype(I32)
    tile_cls = jnp.sum(jnp.arange(n_tiles, dtype=I32)[:, None] * tm >= ends[None, :], axis=1)
    live = (tile_cls < N_CLASSES).astype(I32)
    tile_cls = jnp.minimum(tile_cls, N_CLASSES - 1)
    group, pair = tile_cls // N_PAIRS, tile_cls % N_PAIRS
    pairs = jnp.asarray(_PAIRS, I32)
    e_lo = group * EXPERTS_PER_GROUP + pairs[pair, 0]
    e_hi = group * EXPERTS_PER_GROUP + pairs[pair, 1]

    pos = _positions(info, offs).reshape(t // MOE_DMA_TILE, 1, MOE_DMA_TILE)
    xs = _dispatch(h2x.reshape(t, d + LANES), pos, ends.astype(I32), n_tiles * tm)
    ys = _experts(xs, e_lo, e_hi, live, wg, wu, wd)
    return _combine(ys, pos, x1, mod_l)


def kernel(x, c, w_ada, b_ada, norm1_g, w_in, q_norm_g, k_norm_g, rel_bias, w_o_a, w_o_b, w_out,
           norm2_g, router_w, router_b, w_gate, w_up, w_down):
    b, s, d = x.shape
    depth = w_ada.shape[0]
    assert d % LANES == 0
    tm = min(512, s)

    mod = _adaln(c, w_ada, b_ada)
    bias = _bias_tiles(rel_bias, DSA_QUERY_BLOCK, DSA_KEY_BLOCK)
    rw = jnp.pad(router_w, ((0, 0), (0, LANES - N_EXPERTS)))
    rwh = rw.astype(BF16)
    rwl = (rw - rwh.astype(F32)).astype(BF16)
    rb = jnp.pad(router_b, (0, LANES - N_EXPERTS)).reshape(1, LANES)

    for l in range(depth):
        qa, ka, va, qbt, qit, kb, vt, ki, wit, sga, sgb = _in_proj(
            x, mod[l], norm1_g[l].reshape(1, d), _pack_w_in(w_in[l]),
            q_norm_g[l].reshape(HEAD_DIM, 1), k_norm_g[l].reshape(1, HEAD_DIM), tm)
        oa = _sb_attention(qa, ka, va)
        ob = _dsa_attention(qit, ki, wit, qbt, kb, vt, bias)
        x1, h2x, info, counts = _merge(oa, ob, sga, sgb, x, mod[l], w_o_a[l].astype(BF16),
                                       w_o_b[l].astype(BF16), w_out[l].astype(BF16),
                                       norm2_g[l].reshape(1, d), rwh, rwl, rb, tm)
        x = _moe(h2x, info, counts, w_gate[l].astype(BF16), w_up[l].astype(BF16),
                 w_down[l].astype(BF16), x1, mod[l])
    return x
```

```python
import functools
import math

import numpy as np
import jax
import jax.numpy as jnp
from jax import lax
from jax.experimental import pallas as pl
from jax.experimental.pallas import tpu as pltpu

F32 = jnp.float32
BF16 = jnp.bfloat16
I32 = jnp.int32
I16 = jnp.int16

HEAD_DIM = 64
N_HEADS = 8
N_EXPERTS = 16
N_GROUPS = 4
EXPERTS_PER_GROUP = 4
_PAIRS = ((0, 1), (0, 2), (0, 3), (1, 2), (1, 3), (2, 3))
N_PAIRS = len(_PAIRS)
N_CLASSES = N_GROUPS * N_PAIRS
N_CLASS_PAD = 32
MOE_ROW_TILE = 256
MOE_DMA_TILE = 512
N_MOD = 6
EPS = 1e-6
INDEX_TOPK_MAX = 256
REL_BUCKETS = 32
REL_MAX_DIST = 128
LANES = 128
DSA_QUERY_BLOCK = 256
DSA_KEY_BLOCK = 128
SB_QUERY_BLOCK = 256
SB_KEY_BLOCK = 256
SB_DEAD_LOG2 = 150.0
LOG2E = 1.4426950408889634
NEG_BIG = -1e30
INT_MIN = -2 ** 31
HALF_BIAS = 2 ** 15

VMEM_LIMIT = 56 * 1024 * 1024


def _nt_dot(a, b):
    return lax.dot_general(a, b, (((1,), (1,)), ((), ())), preferred_element_type=F32)


def _dot(a, b):
    return jnp.dot(a, b, preferred_element_type=F32)


def _mod_kernel(c_ref, w_ref, b_ref, o_ref):
    c = c_ref[...]
    sc = c * jax.nn.sigmoid(c)
    o_ref[...] = jnp.dot(sc, w_ref[...], preferred_element_type=F32,
                         precision=lax.Precision.HIGHEST) + b_ref[...]


def _adaln(c, w_ada, b_ada):
    depth, d, nd = w_ada.shape
    b = c.shape[0]
    out = pl.pallas_call(
        _mod_kernel,
        grid=(depth, nd // d),
        in_specs=[pl.BlockSpec((b, d), lambda l, j: (0, 0)),
                  pl.BlockSpec((None, d, d), lambda l, j: (l, 0, j)),
                  pl.BlockSpec((None, 1, d), lambda l, j: (l, 0, j))],
        out_specs=pl.BlockSpec((None, b, d), lambda l, j: (l, 0, j)),
        out_shape=jax.ShapeDtypeStruct((depth, b, nd), F32),
        compiler_params=pltpu.CompilerParams(vmem_limit_bytes=VMEM_LIMIT),
        name="adaln",
    )(c, w_ada, b_ada.reshape(depth, 1, nd))
    return out.reshape(depth, b, N_MOD, d)


def _rms(x):
    return x * lax.rsqrt(jnp.mean(x * x, axis=-1, keepdims=True) + EPS)


def _in_kernel(x_ref, mod_ref, g1_ref, w_ref, qng_ref, kng_ref,
               qa_ref, ka_ref, va_ref, qbt_ref, qit_ref, kb_ref, vt_ref, ki_ref, wit_ref,
               sga_ref, sgb_ref):
    d = x_ref.shape[-1]
    tm = x_ref.shape[0]
    hw = N_HEADS * HEAD_DIM
    x = x_ref[...]
    h = (_rms(x) * g1_ref[...]) * (1.0 + mod_ref[1:2, :]) + mod_ref[0:1, :]
    hb = h.astype(BF16)

    def proj(lo, n):
        return _dot(hb, w_ref[:, lo:lo + n])

    def heads(r, ref):
        for hh in range(N_HEADS):
            ref[hh] = r[:, hh * HEAD_DIM:(hh + 1) * HEAD_DIM].astype(ref.dtype)

    scale = HEAD_DIM ** -0.5
    heads(proj(0, hw) * (scale * LOG2E), qa_ref)
    heads(proj(hw, hw), ka_ref)
    heads(proj(2 * hw, hw), va_ref)

    rt = proj(3 * hw, hw).T
    for hh in range(N_HEADS):
        slab = rt[hh * HEAD_DIM:(hh + 1) * HEAD_DIM, :]
        inv = lax.rsqrt(jnp.mean(slab * slab, axis=0, keepdims=True) + EPS)
        qbt_ref[hh] = (((slab * inv) * qng_ref[...]) * scale).astype(BF16)

    r = proj(4 * hw, 2 * HEAD_DIM)
    kb_ref[...] = (_rms(r[:, :HEAD_DIM]) * kng_ref[...]).astype(BF16)
    vt = r.T[HEAD_DIM:, :]
    ones_row = (lax.broadcasted_iota(I32, (HEAD_DIM, LANES), 0) == 0).astype(BF16)
    for cc in range(tm // LANES):
        vt_ref[cc, :HEAD_DIM, :] = vt[:, cc * LANES:(cc + 1) * LANES].astype(BF16)
        vt_ref[cc, HEAD_DIM:, :] = ones_row

    rt = proj(4 * hw + LANES, hw).T
    for hh in range(N_HEADS):
        qit_ref[hh] = rt[hh * HEAD_DIM:(hh + 1) * HEAD_DIM, :].astype(BF16)

    r = proj(5 * hw + LANES, LANES)
    ki_ref[...] = r[:, :HEAD_DIM].astype(BF16)
    wit_ref[...] = r.T[HEAD_DIM:HEAD_DIM + N_HEADS, :]

    off = 5 * hw + 2 * LANES
    sga_ref[...] = jax.nn.sigmoid(proj(off, d)).astype(BF16)
    sgb_ref[...] = jax.nn.sigmoid(proj(off + d, d)).astype(BF16)


def _pack_w_in(w_in_l):
    hw = N_HEADS * HEAD_DIM
    a = 4 * hw + 2 * HEAD_DIM + hw
    small = w_in_l[:, a:a + HEAD_DIM + N_HEADS]
    small = jnp.pad(small, ((0, 0), (0, LANES - small.shape[1])))
    gates = w_in_l[:, a + HEAD_DIM + N_HEADS:]
    return jnp.concatenate([w_in_l[:, :a], small, gates], axis=1).astype(BF16)


def _in_proj(x, mod_l, g1, w_packed, qng, kng, tm):
    b, s, d = x.shape
    hm = jax.ShapeDtypeStruct((b, N_HEADS, s, HEAD_DIM), BF16)
    hmt = jax.ShapeDtypeStruct((b, N_HEADS, HEAD_DIM, s), BF16)
    tok64 = jax.ShapeDtypeStruct((b, s, HEAD_DIM), BF16)
    hm_spec = pl.BlockSpec((None, N_HEADS, tm, HEAD_DIM), lambda bb, i: (bb, 0, i, 0))
    hmt_spec = pl.BlockSpec((None, N_HEADS, HEAD_DIM, tm), lambda bb, i: (bb, 0, 0, i))
    tok = lambda n: pl.BlockSpec((None, tm, n), lambda bb, i: (bb, i, 0))
    nw = w_packed.shape[1]
    return pl.pallas_call(
        _in_kernel,
        grid=(b, s // tm),
        in_specs=[tok(d),
                  pl.BlockSpec((None, N_MOD, d), lambda bb, i: (bb, 0, 0)),
                  pl.BlockSpec((1, d), lambda bb, i: (0, 0)),
                  pl.BlockSpec((d, nw), lambda bb, i: (0, 0)),
                  pl.BlockSpec((HEAD_DIM, 1), lambda bb, i: (0, 0)),
                  pl.BlockSpec((1, HEAD_DIM), lambda bb, i: (0, 0))],
        out_specs=[hm_spec] * 3 + [hmt_spec] * 2
                  + [tok(HEAD_DIM),
                     pl.BlockSpec((None, tm // LANES, 2 * HEAD_DIM, LANES), lambda bb, i: (bb, i, 0, 0)),
                     tok(HEAD_DIM),
                     pl.BlockSpec((None, N_HEADS, tm), lambda bb, i: (bb, 0, i)),
                     tok(d), tok(d)],
        out_shape=[hm] * 3 + [hmt] * 2
                  + [tok64,
                     jax.ShapeDtypeStruct((b, s // LANES, 2 * HEAD_DIM, LANES), BF16),
                     tok64,
                     jax.ShapeDtypeStruct((b, N_HEADS, s), F32),
                     jax.ShapeDtypeStruct((b, s, d), BF16),
                     jax.ShapeDtypeStruct((b, s, d), BF16)],
        compiler_params=pltpu.CompilerParams(
            dimension_semantics=("parallel", "parallel"), vmem_limit_bytes=VMEM_LIMIT),
        name="in_proj",
    )(x, mod_l, g1, w_packed, qng, kng)


def _sb_kernel(q_ref, k_ref, v_ref, o_ref, acc_scr, csum_scr, sp_scr, ls_scr):
    tq = q_ref.shape[1]
    tk = SB_KEY_BLOCK
    i = pl.program_id(1)
    nfull = (i * tq) // tk
    row = lax.broadcasted_iota(I32, (tq, tk), 0)
    col = lax.broadcasted_iota(I32, (tq, tk), 1)
    before = (col - row) < (i * tq - nfull * tk)
    krow = lax.broadcasted_iota(I32, (tk, tk), 0)
    kcol = lax.broadcasted_iota(I32, (tk, tk), 1)
    suffix = (krow > kcol).astype(BF16)

    def stage_a(hh, st, first):
        z2 = _nt_dot(q_ref[hh], k_ref[hh, pl.ds(st, tk), :])
        sp2 = jnp.maximum(jnp.log2(1.0 + jnp.exp2(jnp.minimum(z2, 126.0))), z2)
        if first:
            sp2 = jnp.where(before, sp2, 0.0)
        sp_scr[hh] = sp2.astype(BF16)
        ls_scr[hh] = z2 - sp2

    def stage_b(hh, st, first):
        sp = sp_scr[hh]
        suf = _dot(sp, suffix)
        if first:
            a = jnp.where(before, jnp.exp2(ls_scr[hh] - suf), 0.0)
            csum = jnp.zeros((tq, 1), F32)
        else:
            csum = csum_scr[hh]
            a = jnp.exp2((ls_scr[hh] - suf) - csum)
        c = _dot(a.astype(BF16), v_ref[hh, pl.ds(st, tk), :])
        if first:
            acc_scr[hh] = c
        else:
            acc_scr[hh] += c
        csum_scr[hh] = csum + (suf[:, 0:1] + sp[:, 0:1].astype(F32))

    def key_block(st, first):
        for hh in range(N_HEADS):
            stage_a(hh, st, first)
        for hh in range(N_HEADS):
            stage_b(hh, st, first)

    key_block(pl.multiple_of(nfull * tk, tk), True)

    def live():
        return jnp.min(csum_scr[...]) < SB_DEAD_LOG2

    def body(carry):
        jj, _ = carry
        key_block(pl.multiple_of((nfull - 1 - jj) * tk, tk), False)
        return jj + 1, live()

    lax.while_loop(lambda carry: (carry[0] < nfull) & carry[1], body, (jnp.int32(0), live()))
    for hh in range(N_HEADS):
        o_ref[:, hh * HEAD_DIM:(hh + 1) * HEAD_DIM] = acc_scr[hh].astype(o_ref.dtype)


def _sb_attention(qa, ka, va):
    b, nh, s, dh = qa.shape
    tq = SB_QUERY_BLOCK
    assert SB_KEY_BLOCK % tq == 0 and s % SB_KEY_BLOCK == 0
    return pl.pallas_call(
        _sb_kernel,
        grid=(b, s // tq),
        in_specs=[pl.BlockSpec((None, nh, tq, dh), lambda bb, i: (bb, 0, i, 0)),
                  pl.BlockSpec((None, nh, s, dh), lambda bb, i: (bb, 0, 0, 0)),
                  pl.BlockSpec((None, nh, s, dh), lambda bb, i: (bb, 0, 0, 0))],
        out_specs=pl.BlockSpec((None, tq, nh * dh), lambda bb, i: (bb, i, 0)),
        out_shape=jax.ShapeDtypeStruct((b, s, nh * dh), BF16),
        scratch_shapes=[pltpu.VMEM((nh, tq, dh), F32), pltpu.VMEM((nh, tq, 1), F32),
                        pltpu.VMEM((nh, tq, SB_KEY_BLOCK), BF16), pltpu.VMEM((nh, tq, SB_KEY_BLOCK), F32)],
        compiler_params=pltpu.CompilerParams(
            dimension_semantics=("parallel", "parallel"), vmem_limit_bytes=VMEM_LIMIT),
        name="stick_breaking",
    )(qa, ka, va)


def _t5_bucket_np(dist):
    max_exact = REL_BUCKETS // 2
    d_f = np.maximum(dist, 1).astype(np.float32)
    large = max_exact + (np.log(d_f / np.float32(max_exact))
                         / np.float32(math.log(REL_MAX_DIST / max_exact))
                         * np.float32(REL_BUCKETS - max_exact)).astype(np.int32)
    large = np.minimum(large, REL_BUCKETS - 1)
    return np.where(dist < max_exact, dist, large).astype(np.int32)


def _bias_tiles(rel_bias, tq, tk):
    n_near = tq // tk + 1
    key = np.arange(tk)[:, None]
    query = np.arange(tq)[None, :]
    bk = np.stack([_t5_bucket_np(np.maximum(query - key - (n - 1) * tk, 0)) for n in range(n_near)])
    far = int(_t5_bucket_np(np.array([REL_MAX_DIST]))[0])
    nh = rel_bias.shape[1]

    def body(rb_ref, bk_ref, o_ref):
        hh = pl.program_id(1)
        buckets = bk_ref[...]
        tile = jnp.zeros((tk, tq), F32)
        for bucket in range(REL_BUCKETS):
            tile = jnp.where(buckets == bucket, rb_ref[hh, bucket] - rb_ref[hh, far], tile)
        o_ref[...] = tile

    return pl.pallas_call(
        body,
        grid=(n_near, nh),
        in_specs=[pl.BlockSpec(memory_space=pltpu.SMEM),
                  pl.BlockSpec((None, tk, tq), lambda n, hh: (n, 0, 0))],
        out_specs=pl.BlockSpec((None, None, tk, tq), lambda n, hh: (n, hh, 0, 0)),
        out_shape=jax.ShapeDtypeStruct((n_near, nh, tk, tq), F32),
        name="bias_tiles",
    )(rel_bias.T, jnp.asarray(bk))


def _sort_key(x):
    bits = pltpu.bitcast(x, I32)
    return bits ^ ((bits >> 31) & 0x7FFFFFFF)


def _fold_rows(x, op):
    return functools.reduce(op, [x[g * 8:(g + 1) * 8, :] for g in range(x.shape[0] // 8)])


def _dsa_kernel(ktop, qit_ref, ki_ref, wit_ref, qbt_ref, kb_ref, vt_ref, bias_ref, o_ref,
                key_scr, hi_scr, lo_scr, m_scr, acc_scr, s_scr):
    tq = qit_ref.shape[2]
    tk = DSA_KEY_BLOCK
    per_q = tq // tk
    i = pl.program_id(1)
    first_own = per_q * i
    nchunks = per_q * (i + 1)
    key_i = lax.broadcasted_iota(I32, (tk, tq), 0)
    qry_i = lax.broadcasted_iota(I32, (tk, tq), 1)

    def causal(rel):
        return (key_i - qry_i) <= (-rel * tk)

    def for_chunks(fn):
        def far(j, carry):
            for u in range(per_q):
                fn(j * per_q + u, None)
            return carry
        lax.fori_loop(0, i - 1, far, 0)

        @pl.when(i >= 1)
        def _():
            for rel in range(-per_q, -1):
                fn(first_own + rel, None)
            fn(first_own - 1, -1)

        for rel in range(per_q):
            fn(first_own + rel, rel)

    def score_chunk(c, rel):
        kic = ki_ref[pl.ds(pl.multiple_of(c * tk, tk), tk), :]
        score = jnp.zeros((tk, tq), F32)
        for hh in range(N_HEADS):
            score = score + wit_ref[hh:hh + 1, :] * jnp.maximum(_dot(kic, qit_ref[hh]), 0.0)
        keys = _sort_key(score)
        if rel is not None and rel >= 0:
            keys = jnp.where(causal(rel), keys, INT_MIN)
        key_scr[c] = keys
        hi_scr[c] = (keys >> 16).astype(I16)
        lo_scr[c] = ((keys & 0xFFFF) - HALF_BIAS).astype(I16)

    for_chunks(score_chunk)

    def count(pred):
        def body(j, part):
            for u in range(per_q):
                part = part + _fold_rows(jnp.where(pred(key_scr[j * per_q + u]), 1, 0), jnp.add)
            return part
        part = lax.fori_loop(0, i + 1, body, jnp.zeros((8, tq), I32))
        return jnp.sum(part, axis=0, keepdims=True)

    def count16(scr, pred):
        one, zero = jnp.int16(1), jnp.int16(0)

        def body(j, part):
            for u in range(per_q):
                hit = jnp.where(pred(scr[j * per_q + u]), one, zero)
                part = part + functools.reduce(
                    jnp.add, [hit[g * 16:(g + 1) * 16, :] for g in range(tk // 16)])
            return part
        part = lax.fori_loop(0, i + 1, body, jnp.zeros((16, tq), I16))
        return jnp.sum(part.astype(I32), axis=0, keepdims=True)

    def search16(scr, base):
        def bit_body(bi, t_u):
            cand_u = t_u | jnp.left_shift(jnp.int32(1), 15 - bi)
            cand = (cand_u - HALF_BIAS).astype(I16)
            return jnp.where(base + count16(scr, lambda k: k >= cand) >= ktop, cand_u, t_u)
        return lax.fori_loop(0, 16, bit_body, jnp.zeros((1, tq), I32))

    t_hi = search16(hi_scr, 0) - HALF_BIAS
    t_hi16 = t_hi.astype(I16)
    above = count16(hi_scr, lambda k: k > t_hi16)

    def keep_band(j, carry):
        for u in range(per_q):
            c = j * per_q + u
            lo_scr[c] = jnp.where(hi_scr[c] == t_hi16, lo_scr[c], jnp.int16(-HALF_BIAS))
        return carry

    lax.fori_loop(0, i + 1, keep_band, 0)
    thr = t_hi * (2 * HALF_BIAS) + search16(lo_scr, above)
    n_ge = count(lambda k: k >= thr)

    @pl.when(jnp.max(n_ge) > ktop)
    def _():
        need = (ktop - count(lambda k: k > thr)).astype(F32)
        lower = (lax.broadcasted_iota(I32, (tk, tk), 1)
                 < lax.broadcasted_iota(I32, (tk, tk), 0)).astype(BF16)

        def body(c, seen):
            k = key_scr[c]
            eq = k == thr
            eqf = eq.astype(BF16)
            rank = seen + _dot(lower, eqf)
            drop = eq & (rank >= need) & (n_ge > ktop)
            key_scr[c] = jnp.where(drop, thr - 1, k)
            return seen + jnp.sum(eqf.astype(F32), axis=0, keepdims=True)

        lax.fori_loop(0, nchunks, body, jnp.zeros((1, tq), F32))

    def logits(c, rel):
        sel = key_scr[c] >= thr
        if rel is not None and rel >= 0:
            sel = sel & causal(rel)
        kbc = kb_ref[pl.ds(pl.multiple_of(c * tk, tk), tk), :]

        def head(hh):
            s = _dot(kbc, qbt_ref[hh])
            if rel is not None:
                s = s + bias_ref[rel + 1, hh]
            return s
        return sel, head

    m_scr[...] = jnp.full(m_scr.shape, NEG_BIG, F32)

    def max_chunk(c, rel):
        sel, head = logits(c, rel)
        for hh in range(N_HEADS):
            s = jnp.where(sel, head(hh), NEG_BIG)
            s_scr[hh, c] = s
            m_scr[hh] = jnp.maximum(m_scr[hh], _fold_rows(s, jnp.maximum))

    for_chunks(max_chunk)
    m = [jnp.max(m_scr[hh], axis=0, keepdims=True) for hh in range(N_HEADS)]

    acc_scr[...] = jnp.zeros(acc_scr.shape, F32)

    def att_block(j, carry):
        for u in range(per_q):
            c = j * per_q + u
            vtc = vt_ref[c]
            for hh in range(N_HEADS):
                p = jnp.exp(s_scr[hh, c] - m[hh])
                acc_scr[hh] += _dot(vtc, p.astype(BF16))
        return carry

    lax.fori_loop(0, i + 1, att_block, 0)

    out_t = jnp.concatenate(
        [acc_scr[hh, :HEAD_DIM, :] / acc_scr[hh, HEAD_DIM:HEAD_DIM + 1, :] for hh in range(N_HEADS)],
        axis=0)
    o_ref[...] = out_t.T.astype(o_ref.dtype)


def _dsa_attention(qit, ki, wit, qbt, kb, vt, bias):
    b, nh, dh, s = qit.shape
    tq = DSA_QUERY_BLOCK
    tk = DSA_KEY_BLOCK
    assert tk == REL_MAX_DIST and tk == LANES and tq % tk == 0 and s % tq == 0
    ktop = min(INDEX_TOPK_MAX, s // 4)
    hmt_q = pl.BlockSpec((None, nh, dh, tq), lambda bb, i: (bb, 0, 0, i))
    full64 = pl.BlockSpec((None, s, dh), lambda bb, i: (bb, 0, 0))
    return pl.pallas_call(
        functools.partial(_dsa_kernel, ktop),
        grid=(b, s // tq),
        in_specs=[hmt_q, full64,
                  pl.BlockSpec((None, nh, tq), lambda bb, i: (bb, 0, i)),
                  hmt_q, full64,
                  pl.BlockSpec((None, s // tk, 2 * dh, tk), lambda bb, i: (bb, 0, 0, 0)),
                  pl.BlockSpec(bias.shape, lambda bb, i: (0, 0, 0, 0))],
        out_specs=pl.BlockSpec((None, tq, nh * dh), lambda bb, i: (bb, i, 0)),
        out_shape=jax.ShapeDtypeStruct((b, s, nh * dh), BF16),
        scratch_shapes=[pltpu.VMEM((s // tk, tk, tq), I32),
                        pltpu.VMEM((s // tk, tk, tq), I16),
                        pltpu.VMEM((s // tk, tk, tq), I16),
                        pltpu.VMEM((nh, 8, tq), F32),
                        pltpu.VMEM((nh, 2 * dh, tq), F32),
                        pltpu.VMEM((nh, s // tk, tk, tq), F32)],
        compiler_params=pltpu.CompilerParams(
            dimension_semantics=("parallel", "parallel"), vmem_limit_bytes=VMEM_LIMIT),
        name="sparse_attention",
    )(qit, ki, wit, qbt, kb, vt, bias)


def _first_max4(a):
    m1 = jnp.maximum(jnp.maximum(a[0], a[1]), jnp.maximum(a[2], a[3]))
    i1 = jnp.where(a[0] == m1, 0, jnp.where(a[1] == m1, 1, jnp.where(a[2] == m1, 2, 3)))
    rest = [jnp.where(i1 == j, -1.0, a[j]) for j in range(4)]
    m2 = jnp.maximum(jnp.maximum(rest[0], rest[1]), jnp.maximum(rest[2], rest[3]))
    i2 = jnp.where(rest[0] == m2, 0, jnp.where(rest[1] == m2, 1, jnp.where(rest[2] == m2, 2, 3)))
    return m1, i1, m2, i2


def _route(lt):
    lg = [lt[e:e + 1, :] for e in range(N_EXPERTS)]
    mx = functools.reduce(jnp.maximum, lg)
    ex = [jnp.exp(v - mx) for v in lg]
    tot = functools.reduce(lambda u, v: u + v, ex)
    p = [v / tot for v in ex]
    gs = []
    for g in range(N_GROUPS):
        m1, _, m2, _ = _first_max4(p[4 * g:4 * g + 4])
        gs.append(m1 + m2)
    _, gbest, _, _ = _first_max4(gs)
    chosen = [jnp.where(gbest == 0, p[j], jnp.where(gbest == 1, p[4 + j],
              jnp.where(gbest == 2, p[8 + j], p[12 + j]))) for j in range(4)]
    m1, i1, m2, i2 = _first_max4(chosen)
    den = m1 + m2
    lo = jnp.minimum(i1, i2)
    hi = jnp.maximum(i1, i2)
    pair = jnp.where(lo == 0, hi - 1, jnp.where(lo == 1, hi + 1, N_PAIRS - 1))
    first_is_lo = i1 < i2
    w1 = m1 / den
    w2 = m2 / den
    return gbest * N_PAIRS + pair, jnp.where(first_is_lo, w1, w2), jnp.where(first_is_lo, w2, w1)


def _merge_kernel(oa_ref, ob_ref, sga_ref, sgb_ref, x_ref, mod_ref, woa_ref, wob_ref, wout_ref,
                  g2_ref, rwh_ref, rwl_ref, rb_ref, x1_ref, h2x_ref, info_ref, cnt_ref, carry_scr):
    tm, d = x_ref.shape

    @pl.when((pl.program_id(0) == 0) & (pl.program_id(1) == 0))
    def _():
        carry_scr[...] = jnp.zeros(carry_scr.shape, F32)

    merged = (sga_ref[...].astype(F32) * _dot(oa_ref[...], woa_ref[...])
              + sgb_ref[...].astype(F32) * _dot(ob_ref[...], wob_ref[...]))
    x1 = x_ref[...] + mod_ref[2:3, :] * _dot(merged.astype(BF16), wout_ref[...])
    x1_ref[...] = x1
    h2 = (_rms(x1) * g2_ref[...]) * (1.0 + mod_ref[4:5, :]) + mod_ref[3:4, :]
    h2x_ref[:, :d] = h2
    hi = h2.astype(BF16)
    lo = (h2 - hi.astype(F32)).astype(BF16)
    logits = (_dot(hi, rwh_ref[...]) + _dot(lo, rwh_ref[...]) + _dot(hi, rwl_ref[...])) + rb_ref[...]
    cls, cw_lo, cw_hi = _route(logits.T)

    ncls = carry_scr.shape[0]
    onehot = (lax.broadcasted_iota(I32, (ncls, tm), 0) == cls).astype(F32)
    earlier = (lax.broadcasted_iota(I32, (tm, tm), 0)
               < lax.broadcasted_iota(I32, (tm, tm), 1)).astype(BF16)
    before = _dot(onehot.astype(BF16), earlier)
    carry = carry_scr[...]
    carry_t = jnp.concatenate([carry] * (tm // LANES), axis=1)
    rank = jnp.sum(onehot * (carry_t + before), axis=0, keepdims=True).astype(I32)
    carry = carry + jnp.sum(onehot, axis=1, keepdims=True)
    carry_scr[...] = carry
    cnt_ref[...] = carry.astype(I32)

    sub = lax.broadcasted_iota(I32, (8, tm), 0)
    info_ref[...] = jnp.where(sub == 0, cls, jnp.where(sub == 1, rank, 0))
    sub = lax.broadcasted_iota(I32, (LANES, tm), 0)
    weights_t = jnp.where(sub == 0, cw_lo, jnp.where(sub == 1, cw_hi, 0.0))
    h2x_ref[:, d:] = weights_t.T


def _merge(oa, ob, sga, sgb, x, mod_l, woa, wob, wout, g2, rwh, rwl, rb, tm):
    b, s, d = x.shape
    hw = oa.shape[-1]
    nt = s // tm
    tok = lambda n: pl.BlockSpec((None, tm, n), lambda bb, i: (bb, i, 0))
    const = lambda r, c: pl.BlockSpec((r, c), lambda bb, i: (0, 0))
    return pl.pallas_call(
        _merge_kernel,
        grid=(b, nt),
        in_specs=[tok(hw), tok(hw), tok(d), tok(d), tok(d),
                  pl.BlockSpec((None, N_MOD, d), lambda bb, i: (bb, 0, 0)),
                  const(hw, d), const(hw, d), const(d, d), const(1, d),
                  const(d, LANES), const(d, LANES), const(1, LANES)],
        out_specs=[tok(d), tok(d + LANES),
                   pl.BlockSpec((8, tm), lambda bb, i: (0, bb * nt + i)),
                   const(N_CLASS_PAD, LANES)],
        out_shape=[jax.ShapeDtypeStruct((b, s, d), F32),
                   jax.ShapeDtypeStruct((b, s, d + LANES), F32),
                   jax.ShapeDtypeStruct((8, b * s), I32),
                   jax.ShapeDtypeStruct((N_CLASS_PAD, LANES), I32)],
        scratch_shapes=[pltpu.VMEM((N_CLASS_PAD, LANES), F32)],
        compiler_params=pltpu.CompilerParams(
            dimension_semantics=("arbitrary", "arbitrary"), vmem_limit_bytes=VMEM_LIMIT),
        name="merge_route",
    )(oa, ob, sga, sgb, x, mod_l, woa, wob, wout, g2, rwh, rwl, rb)


def _positions_kernel(offs_ref, info_ref, pos_ref):
    cls = info_ref[0:1, :]
    pos = info_ref[1:2, :]
    for c in range(N_CLASSES):
        pos = pos + jnp.where(cls == c, offs_ref[c], 0)
    pos_ref[...] = jnp.broadcast_to(pos, pos_ref.shape)


def _positions(info, offs):
    return pl.pallas_call(
        _positions_kernel,
        in_specs=[pl.BlockSpec(memory_space=pltpu.SMEM), pl.BlockSpec(info.shape, lambda: (0, 0))],
        out_specs=pl.BlockSpec(info.shape, lambda: (0, 0)),
        out_shape=jax.ShapeDtypeStruct(info.shape, I32),
        name="positions",
    )(offs, info)[0]


def _dispatch_kernel(ends_ref, pos_ref, src_ref, dst_ref, zero_scr, sem, zero_sem):
    tm = pos_ref.shape[1]
    row_tile = zero_scr.shape[0]

    @pl.when(pl.program_id(0) == 0)
    def _():
        zero_scr[...] = jnp.zeros(zero_scr.shape, F32)

        def zero_tail(c):
            start = pl.multiple_of(ends_ref[c] - row_tile, row_tile)
            return pltpu.make_async_copy(zero_scr, dst_ref.at[pl.ds(start, row_tile), :], zero_sem)

        def nonempty(c):
            return ends_ref[c] > (ends_ref[c - 1] if c else 0)

        for c in range(N_CLASSES):
            @pl.when(nonempty(c))
            def _(c=c):
                zero_tail(c).start()
        for c in range(N_CLASSES):
            @pl.when(nonempty(c))
            def _(c=c):
                zero_tail(c).wait()

        def unused_tile(j):
            start = pl.multiple_of(ends_ref[N_CLASSES - 1] + j * row_tile, row_tile)
            return pltpu.make_async_copy(zero_scr, dst_ref.at[pl.ds(start, row_tile), :], zero_sem)

        n_unused = (dst_ref.shape[0] - ends_ref[N_CLASSES - 1]) // row_tile
        lax.fori_loop(0, n_unused, lambda j, carry: (unused_tile(j).start(), carry)[1], 0)
        lax.fori_loop(0, n_unused, lambda j, carry: (unused_tile(j).wait(), carry)[1], 0)

    for r in range(tm):
        pltpu.make_async_copy(src_ref.at[pl.ds(r, 1), :], dst_ref.at[pl.ds(pos_ref[0, r], 1), :],
                              sem).start(priority=r % 2)
    pltpu.make_async_copy(src_ref, dst_ref.at[pl.ds(0, tm), :], sem).wait()


def _dispatch(h2x, pos, ends, n_rows):
    t, w = h2x.shape
    tm = MOE_DMA_TILE
    smem_row = pl.BlockSpec((None, 1, tm), lambda i: (i, 0, 0), memory_space=pltpu.SMEM)
    return pl.pallas_call(
        _dispatch_kernel,
        grid=(t // tm,),
        in_specs=[pl.BlockSpec(memory_space=pltpu.SMEM), smem_row, pl.BlockSpec((tm, w), lambda i: (i, 0))],
        out_specs=pl.BlockSpec(memory_space=pl.ANY),
        out_shape=jax.ShapeDtypeStruct((n_rows, w), F32),
        scratch_shapes=[pltpu.VMEM((MOE_ROW_TILE, w), F32), pltpu.SemaphoreType.DMA, pltpu.SemaphoreType.DMA],
        compiler_params=pltpu.CompilerParams(
            dimension_semantics=("arbitrary",), vmem_limit_bytes=VMEM_LIMIT),
        name="dispatch",
    )(ends, pos, h2x)


def _experts_kernel(elo_ref, ehi_ref, live_ref, xs_ref, wg_lo, wg_hi, wu_lo, wu_hi, wd_lo, wd_hi, ys_ref):
    del elo_ref, ehi_ref
    d = ys_ref.shape[1]
    j = pl.program_id(0)

    @pl.when(live_ref[j] == 1)
    def _():
        h = xs_ref[:, :d].astype(BF16)

        def expert(wg, wu, wd, weight):
            gate = _dot(h, wg[...])
            hidden = ((gate * jax.nn.sigmoid(gate)) * _dot(h, wu[...])) * weight
            return _dot(hidden.astype(BF16), wd[...])

        ys_ref[...] = (expert(wg_lo, wu_lo, wd_lo, xs_ref[:, d:d + 1])
                       + expert(wg_hi, wu_hi, wd_hi, xs_ref[:, d + 1:d + 2]))

    @pl.when(live_ref[j] == 0)
    def _():
        ys_ref[...] = jnp.zeros(ys_ref.shape, F32)


def _experts(xs, e_lo, e_hi, live, wg, wu, wd):
    n_rows, w = xs.shape
    ne, d, f = wg.shape
    tm = MOE_ROW_TILE
    lo = lambda j, elo, ehi, lv: (elo[j], 0, 0)
    hi = lambda j, elo, ehi, lv: (ehi[j], 0, 0)
    grid_spec = pltpu.PrefetchScalarGridSpec(
        num_scalar_prefetch=3,
        grid=(n_rows // tm,),
        in_specs=[pl.BlockSpec((tm, w), lambda j, elo, ehi, lv: (j, 0)),
                  pl.BlockSpec((None, d, f), lo), pl.BlockSpec((None, d, f), hi),
                  pl.BlockSpec((None, d, f), lo), pl.BlockSpec((None, d, f), hi),
                  pl.BlockSpec((None, f, d), lo), pl.BlockSpec((None, f, d), hi)],
        out_specs=pl.BlockSpec((tm, d), lambda j, elo, ehi, lv: (j, 0)),
    )
    return pl.pallas_call(
        _experts_kernel,
        grid_spec=grid_spec,
        out_shape=jax.ShapeDtypeStruct((n_rows, d), F32),
        compiler_params=pltpu.CompilerParams(
            dimension_semantics=("arbitrary",), vmem_limit_bytes=VMEM_LIMIT),
        name="experts",
    )(e_lo, e_hi, live, xs, wg, wg, wu, wu, wd, wd)


def _combine_kernel(pos_ref, ys_ref, x1_ref, mod_ref, o_ref, buf, sem):
    tm = o_ref.shape[0]

    for r in range(tm):
        pltpu.make_async_copy(ys_ref.at[pl.ds(pos_ref[0, r], 1), :], buf.at[pl.ds(r, 1), :],
                              sem).start(priority=r % 2)
    pltpu.make_async_copy(ys_ref.at[pl.ds(0, tm), :], buf, sem).wait()
    o_ref[...] = x1_ref[...] + mod_ref[5:6, :] * buf[...]


def _combine(ys, pos, x1, mod_l):
    b, s, d = x1.shape
    tm = MOE_DMA_TILE
    nt = s // tm
    smem_row = pl.BlockSpec((None, 1, tm), lambda bb, i: (bb * nt + i, 0, 0), memory_space=pltpu.SMEM)
    return pl.pallas_call(
        _combine_kernel,
        grid=(b, nt),
        in_specs=[smem_row,
                  pl.BlockSpec(memory_space=pl.ANY),
                  pl.BlockSpec((None, tm, d), lambda bb, i: (bb, i, 0)),
                  pl.BlockSpec((None, N_MOD, d), lambda bb, i: (bb, 0, 0))],
        out_specs=pl.BlockSpec((None, tm, d), lambda bb, i: (bb, i, 0)),
        out_shape=jax.ShapeDtypeStruct((b, s, d), F32),
        scratch_shapes=[pltpu.VMEM((tm, d), F32), pltpu.SemaphoreType.DMA],
        compiler_params=pltpu.CompilerParams(
            dimension_semantics=("arbitrary", "arbitrary"), vmem_limit_bytes=VMEM_LIMIT),
        name="combine",
    )(pos, ys, x1, mod_l)


def _moe(h2x, info, counts, wg, wu, wd, x1, mod_l):
    b, s, d = x1.shape
    t = b * s
    tm = MOE_ROW_TILE
    n_tiles = t // tm + N_CLASSES
    counts = counts[:N_CLASSES, 0]
    padded = (counts + tm - 1) // tm * tm
    ends = jnp.cumsum(padded)
    offs = jnp.pad(ends - padded, (0, N_CLASS_PAD - N_CLASSES)).astype(I32)
    tile_cls = jnp.sum(jnp.arange(n_tiles, dtype=I32)[:, None] * tm >= ends[None, :], axis=1)
    live = (tile_cls < N_CLASSES).astype(I32)
    tile_cls = jnp.minimum(tile_cls, N_CLASSES - 1)
    group, pair = tile_cls // N_PAIRS, tile_cls % N_PAIRS
    pairs = jnp.asarray(_PAIRS, I32)
    e_lo = group * EXPERTS_PER_GROUP + pairs[pair, 0]
    e_hi = group * EXPERTS_PER_GROUP + pairs[pair, 1]

    pos = _positions(info, offs).reshape(t // MOE_DMA_TILE, 1, MOE_DMA_TILE)
    xs = _dispatch(h2x.reshape(t, d + LANES), pos, ends.astype(I32), n_tiles * tm)
    ys = _experts(xs, e_lo, e_hi, live, wg, wu, wd)
    return _combine(ys, pos, x1, mod_l)


def kernel(x, c, w_ada, b_ada, norm1_g, w_in, q_norm_g, k_norm_g, rel_bias, w_o_a, w_o_b, w_out,
           norm2_g, router_w, router_b, w_gate, w_up, w_down):
    b, s, d = x.shape
    depth = w_ada.shape[0]
    assert d % LANES == 0
    tm = min(512, s)

    mod = _adaln(c, w_ada, b_ada)
    bias = _bias_tiles(rel_bias, DSA_QUERY_BLOCK, DSA_KEY_BLOCK)
    rw = jnp.pad(router_w, ((0, 0), (0, LANES - N_EXPERTS)))
    rwh = rw.astype(BF16)
    rwl = (rw - rwh.astype(F32)).astype(BF16)
    rb = jnp.pad(router_b, (0, LANES - N_EXPERTS)).reshape(1, LANES)

    for l in range(depth):
        qa, ka, va, qbt, qit, kb, vt, ki, wit, sga, sgb = _in_proj(
            x, mod[l], norm1_g[l].reshape(1, d), _pack_w_in(w_in[l]),
            q_norm_g[l].reshape(HEAD_DIM, 1), k_norm_g[l].reshape(1, HEAD_DIM), tm)
        oa = _sb_attention(qa, ka, va)
        ob = _dsa_attention(qit, ki, wit, qbt, kb, vt, bias)
        x1, h2x, info, counts = _merge(oa, ob, sga, sgb, x, mod[l], w_o_a[l].astype(BF16),
                                       w_o_b[l].astype(BF16), w_out[l].astype(BF16),
                                       norm2_g[l].reshape(1, d), rwh, rwl, rb, tm)
        x = _moe(h2x, info, counts, w_gate[l].astype(BF16), w_up[l].astype(BF16),
                 w_down[l].astype(BF16), x1, mod[l])
    return x
```

```python
import functools
import math

import numpy as np
import jax
import jax.numpy as jnp
from jax import lax
from jax.experimental import pallas as pl
from jax.experimental.pallas import tpu as pltpu

F32 = jnp.float32
BF16 = jnp.bfloat16
I32 = jnp.int32
I16 = jnp.int16

HEAD_DIM = 64
N_HEADS = 8
N_EXPERTS = 16
N_GROUPS = 4
EXPERTS_PER_GROUP = 4
_PAIRS = ((0, 1), (0, 2), (0, 3), (1, 2), (1, 3), (2, 3))
N_PAIRS = len(_PAIRS)
N_CLASSES = N_GROUPS * N_PAIRS
N_CLASS_PAD = 32
MOE_ROW_TILE = 256
MOE_DMA_TILE = 512
N_MOD = 6
EPS = 1e-6
INDEX_TOPK_MAX = 256
REL_BUCKETS = 32
REL_MAX_DIST = 128
LANES = 128
DSA_QUERY_BLOCK = 256
DSA_KEY_BLOCK = 128
SB_QUERY_BLOCK = 256
SB_KEY_BLOCK = 256
SB_DEAD_LOG2 = 150.0
LOG2E = 1.4426950408889634
NEG_BIG = -1e30
INT_MIN = -2 ** 31
HALF_BIAS = 2 ** 15

VMEM_LIMIT = 56 * 1024 * 1024


def _nt_dot(a, b):
    return lax.dot_general(a, b, (((1,), (1,)), ((), ())), preferred_element_type=F32)


def _dot(a, b):
    return jnp.dot(a, b, preferred_element_type=F32)


def _mod_kernel(c_ref, w_ref, b_ref, o_ref):
    c = c_ref[...]
    sc = c * jax.nn.sigmoid(c)
    o_ref[...] = jnp.dot(sc, w_ref[...], preferred_element_type=F32,
                         precision=lax.Precision.HIGHEST) + b_ref[...]


def _adaln(c, w_ada, b_ada):
    depth, d, nd = w_ada.shape
    b = c.shape[0]
    out = pl.pallas_call(
        _mod_kernel,
        grid=(depth, nd // d),
        in_specs=[pl.BlockSpec((b, d), lambda l, j: (0, 0)),
                  pl.BlockSpec((None, d, d), lambda l, j: (l, 0, j)),
                  pl.BlockSpec((None, 1, d), lambda l, j: (l, 0, j))],
        out_specs=pl.BlockSpec((None, b, d), lambda l, j: (l, 0, j)),
        out_shape=jax.ShapeDtypeStruct((depth, b, nd), F32),
        compiler_params=pltpu.CompilerParams(vmem_limit_bytes=VMEM_LIMIT),
        name="adaln",
    )(c, w_ada, b_ada.reshape(depth, 1, nd))
    return out.reshape(depth, b, N_MOD, d)


def _rms(x):
    return x * lax.rsqrt(jnp.mean(x * x, axis=-1, keepdims=True) + EPS)


def _in_kernel(x_ref, mod_ref, g1_ref, w_ref, qng_ref, kng_ref,
               qa_ref, ka_ref, va_ref, qbt_ref, qit_ref, kb_ref, vt_ref, ki_ref, wit_ref,
               sga_ref, sgb_ref):
    d = x_ref.shape[-1]
    tm = x_ref.shape[0]
    hw = N_HEADS * HEAD_DIM
    x = x_ref[...]
    h = (_rms(x) * g1_ref[...]) * (1.0 + mod_ref[1:2, :]) + mod_ref[0:1, :]
    hb = h.astype(BF16)

    def proj(lo, n):
        return _dot(hb, w_ref[:, lo:lo + n])

    def heads(r, ref):
        for hh in range(N_HEADS):
            ref[hh] = r[:, hh * HEAD_DIM:(hh + 1) * HEAD_DIM].astype(ref.dtype)

    scale = HEAD_DIM ** -0.5
    heads(proj(0, hw) * (scale * LOG2E), qa_ref)
    heads(proj(hw, hw), ka_ref)
    heads(proj(2 * hw, hw), va_ref)

    rt = proj(3 * hw, hw).T
    for hh in range(N_HEADS):
        slab = rt[hh * HEAD_DIM:(hh + 1) * HEAD_DIM, :]
        inv = lax.rsqrt(jnp.mean(slab * slab, axis=0, keepdims=True) + EPS)
        qbt_ref[hh] = (((slab * inv) * qng_ref[...]) * scale).astype(BF16)

    r = proj(4 * hw, 2 * HEAD_DIM)
    kb_ref[...] = (_rms(r[:, :HEAD_DIM]) * kng_ref[...]).astype(BF16)
    vt = r.T[HEAD_DIM:, :]
    ones_row = (lax.broadcasted_iota(I32, (HEAD_DIM, LANES), 0) == 0).astype(BF16)
    for cc in range(tm // LANES):
        vt_ref[cc, :HEAD_DIM, :] = vt[:, cc * LANES:(cc + 1) * LANES].astype(BF16)
        vt_ref[cc, HEAD_DIM:, :] = ones_row

    rt = proj(4 * hw + LANES, hw).T
    for hh in range(N_HEADS):
        qit_ref[hh] = rt[hh * HEAD_DIM:(hh + 1) * HEAD_DIM, :].astype(BF16)

    r = proj(5 * hw + LANES, LANES)
    ki_ref[...] = r[:, :HEAD_DIM].astype(BF16)
    wit_ref[...] = r.T[HEAD_DIM:HEAD_DIM + N_HEADS, :]

    off = 5 * hw + 2 * LANES
    sga_ref[...] = jax.nn.sigmoid(proj(off, d)).astype(BF16)
    sgb_ref[...] = jax.nn.sigmoid(proj(off + d, d)).astype(BF16)


def _pack_w_in(w_in_l):
    hw = N_HEADS * HEAD_DIM
    a = 4 * hw + 2 * HEAD_DIM + hw
    small = w_in_l[:, a:a + HEAD_DIM + N_HEADS]
    small = jnp.pad(small, ((0, 0), (0, LANES - small.shape[1])))
    gates = w_in_l[:, a + HEAD_DIM + N_HEADS:]
    return jnp.concatenate([w_in_l[:, :a], small, gates], axis=1).astype(BF16)


def _in_proj(x, mod_l, g1, w_packed, qng, kng, tm):
    b, s, d = x.shape
    hm = jax.ShapeDtypeStruct((b, N_HEADS, s, HEAD_DIM), BF16)
    hmt = jax.ShapeDtypeStruct((b, N_HEADS, HEAD_DIM, s), BF16)
    tok64 = jax.ShapeDtypeStruct((b, s, HEAD_DIM), BF16)
    hm_spec = pl.BlockSpec((None, N_HEADS, tm, HEAD_DIM), lambda bb, i: (bb, 0, i, 0))
    hmt_spec = pl.BlockSpec((None, N_HEADS, HEAD_DIM, tm), lambda bb, i: (bb, 0, 0, i))
    tok = lambda n: pl.BlockSpec((None, tm, n), lambda bb, i: (bb, i, 0))
    nw = w_packed.shape[1]
    return pl.pallas_call(
        _in_kernel,
        grid=(b, s // tm),
        in_specs=[tok(d),
                  pl.BlockSpec((None, N_MOD, d), lambda bb, i: (bb, 0, 0)),
                  pl.BlockSpec((1, d), lambda bb, i: (0, 0)),
                  pl.BlockSpec((d, nw), lambda bb, i: (0, 0)),
                  pl.BlockSpec((HEAD_DIM, 1), lambda bb, i: (0, 0)),
                  pl.BlockSpec((1, HEAD_DIM), lambda bb, i: (0, 0))],
        out_specs=[hm_spec] * 3 + [hmt_spec] * 2
                  + [tok(HEAD_DIM),
                     pl.BlockSpec((None, tm // LANES, 2 * HEAD_DIM, LANES), lambda bb, i: (bb, i, 0, 0)),
                     tok(HEAD_DIM),
                     pl.BlockSpec((None, N_HEADS, tm), lambda bb, i: (bb, 0, i)),
                     tok(d), tok(d)],
        out_shape=[hm] * 3 + [hmt] * 2
                  + [tok64,
                     jax.ShapeDtypeStruct((b, s // LANES, 2 * HEAD_DIM, LANES), BF16),
                     tok64,
                     jax.ShapeDtypeStruct((b, N_HEADS, s), F32),
                     jax.ShapeDtypeStruct((b, s, d), BF16),
                     jax.ShapeDtypeStruct((b, s, d), BF16)],
        compiler_params=pltpu.CompilerParams(
            dimension_semantics=("parallel", "parallel"), vmem_limit_bytes=VMEM_LIMIT),
        name="in_proj",
    )(x, mod_l, g1, w_packed, qng, kng)


def _sb_kernel(q_ref, k_ref, v_ref, o_ref, acc_scr, csum_scr, sp_scr, ls_scr):
    tq = q_ref.shape[1]
    tk = SB_KEY_BLOCK
    i = pl.program_id(1)
    nfull = (i * tq) // tk
    row = lax.broadcasted_iota(I32, (tq, tk), 0)
    col = lax.broadcasted_iota(I32, (tq, tk), 1)
    before = (col - row) < (i * tq - nfull * tk)
    krow = lax.broadcasted_iota(I32, (tk, tk), 0)
    kcol = lax.broadcasted_iota(I32, (tk, tk), 1)
    suffix = (krow > kcol).astype(BF16)

    def stage_a(hh, st, first):
        z2 = _nt_dot(q_ref[hh], k_ref[hh, pl.ds(st, tk), :])
        sp2 = jnp.maximum(jnp.log2(1.0 + jnp.exp2(jnp.minimum(z2, 126.0))), z2)
        if first:
            sp2 = jnp.where(before, sp2, 0.0)
        sp_scr[hh] = sp2.astype(BF16)
        ls_scr[hh] = z2 - sp2

    def stage_b(hh, st, first):
        sp = sp_scr[hh]
        suf = _dot(sp, suffix)
        if first:
            a = jnp.where(before, jnp.exp2(ls_scr[hh] - suf), 0.0)
            csum = jnp.zeros((tq, 1), F32)
        else:
            csum = csum_scr[hh]
            a = jnp.exp2((ls_scr[hh] - suf) - csum)
        c = _dot(a.astype(BF16), v_ref[hh, pl.ds(st, tk), :])
        if first:
            acc_scr[hh] = c
        else:
            acc_scr[hh] += c
        csum_scr[hh] = csum + (suf[:, 0:1] + sp[:, 0:1].astype(F32))

    def key_block(st, first):
        for hh in range(N_HEADS):
            stage_a(hh, st, first)
        for hh in range(N_HEADS):
            stage_b(hh, st, first)

    key_block(pl.multiple_of(nfull * tk, tk), True)

    def live():
        return jnp.min(csum_scr[...]) < SB_DEAD_LOG2

    def body(carry):
        jj, _ = carry
        key_block(pl.multiple_of((nfull - 1 - jj) * tk, tk), False)
        return jj + 1, live()

    lax.while_loop(lambda carry: (carry[0] < nfull) & carry[1], body, (jnp.int32(0), live()))
    for hh in range(N_HEADS):
        o_ref[:, hh * HEAD_DIM:(hh + 1) * HEAD_DIM] = acc_scr[hh].astype(o_ref.dtype)


def _sb_attention(qa, ka, va):
    b, nh, s, dh = qa.shape
    tq = SB_QUERY_BLOCK
    assert SB_KEY_BLOCK % tq == 0 and s % SB_KEY_BLOCK == 0
    return pl.pallas_call(
        _sb_kernel,
        grid=(b, s // tq),
        in_specs=[pl.BlockSpec((None, nh, tq, dh), lambda bb, i: (bb, 0, i, 0)),
                  pl.BlockSpec((None, nh, s, dh), lambda bb, i: (bb, 0, 0, 0)),
                  pl.BlockSpec((None, nh, s, dh), lambda bb, i: (bb, 0, 0, 0))],
        out_specs=pl.BlockSpec((None, tq, nh * dh), lambda bb, i: (bb, i, 0)),
        out_shape=jax.ShapeDtypeStruct((b, s, nh * dh), BF16),
        scratch_shapes=[pltpu.VMEM((nh, tq, dh), F32), pltpu.VMEM((nh, tq, 1), F32),
                        pltpu.VMEM((nh, tq, SB_KEY_BLOCK), BF16), pltpu.VMEM((nh, tq, SB_KEY_BLOCK), F32)],
        compiler_params=pltpu.CompilerParams(
            dimension_semantics=("parallel", "parallel"), vmem_limit_bytes=VMEM_LIMIT),
        name="stick_breaking",
    )(qa, ka, va)


def _t5_bucket_np(dist):
    max_exact = REL_BUCKETS // 2
    d_f = np.maximum(dist, 1).astype(np.float32)
    large = max_exact + (np.log(d_f / np.float32(max_exact))
                         / np.float32(math.log(REL_MAX_DIST / max_exact))
                         * np.float32(REL_BUCKETS - max_exact)).astype(np.int32)
    large = np.minimum(large, REL_BUCKETS - 1)
    return np.where(dist < max_exact, dist, large).astype(np.int32)


def _bias_tiles(rel_bias, tq, tk):
    n_near = tq // tk + 1
    key = np.arange(tk)[:, None]
    query = np.arange(tq)[None, :]
    bk = np.stack([_t5_bucket_np(np.maximum(query - key - (n - 1) * tk, 0)) for n in range(n_near)])
    far = int(_t5_bucket_np(np.array([REL_MAX_DIST]))[0])
    nh = rel_bias.shape[1]

    def body(rb_ref, bk_ref, o_ref):
        hh = pl.program_id(1)
        buckets = bk_ref[...]
        tile = jnp.zeros((tk, tq), F32)
        for bucket in range(REL_BUCKETS):
            tile = jnp.where(buckets == bucket, rb_ref[hh, bucket] - rb_ref[hh, far], tile)
        o_ref[...] = tile

    return pl.pallas_call(
        body,
        grid=(n_near, nh),
        in_specs=[pl.BlockSpec(memory_space=pltpu.SMEM),
                  pl.BlockSpec((None, tk, tq), lambda n, hh: (n, 0, 0))],
        out_specs=pl.BlockSpec((None, None, tk, tq), lambda n, hh: (n, hh, 0, 0)),
        out_shape=jax.ShapeDtypeStruct((n_near, nh, tk, tq), F32),
        name="bias_tiles",
    )(rel_bias.T, jnp.asarray(bk))


def _sort_key(x):
    bits = pltpu.bitcast(x, I32)
    return bits ^ ((bits >> 31) & 0x7FFFFFFF)


def _fold_rows(x, op):
    return functools.reduce(op, [x[g * 8:(g + 1) * 8, :] for g in range(x.shape[0] // 8)])


def _dsa_kernel(ktop, qit_ref, ki_ref, wit_ref, qbt_ref, kb_ref, vt_ref, bias_ref, o_ref,
                key_scr, hi_scr, lo_scr, m_scr, acc_scr, s_scr):
    tq = qit_ref.shape[2]
    tk = DSA_KEY_BLOCK
    per_q = tq // tk
    i = pl.program_id(1)
    first_own = per_q * i
    nchunks = per_q * (i + 1)
    key_i = lax.broadcasted_iota(I32, (tk, tq), 0)
    qry_i = lax.broadcasted_iota(I32, (tk, tq), 1)

    def causal(rel):
        return (key_i - qry_i) <= (-rel * tk)

    def for_chunks(fn):
        def far(j, carry):
            for u in range(per_q):
                fn(j * per_q + u, None)
            return carry
        lax.fori_loop(0, i - 1, far, 0)

        @pl.when(i >= 1)
        def _():
            for rel in range(-per_q, -1):
                fn(first_own + rel, None)
            fn(first_own - 1, -1)

        for rel in range(per_q):
            fn(first_own + rel, rel)

    def score_chunk(c, rel):
        kic = ki_ref[pl.ds(pl.multiple_of(c * tk, tk), tk), :]
        score = jnp.zeros((tk, tq), F32)
        for hh in range(N_HEADS):
            score = score + wit_ref[hh:hh + 1, :] * jnp.maximum(_dot(kic, qit_ref[hh]), 0.0)
        keys = _sort_key(score)
        if rel is not None and rel >= 0:
            keys = jnp.where(causal(rel), keys, INT_MIN)
        key_scr[c] = keys
        hi_scr[c] = (keys >> 16).astype(I16)
        lo_scr[c] = ((keys & 0xFFFF) - HALF_BIAS).astype(I16)

    for_chunks(score_chunk)

    def count(pred):
        def body(j, part):
            for u in range(per_q):
                part = part + _fold_rows(jnp.where(pred(key_scr[j * per_q + u]), 1, 0), jnp.add)
            return part
        part = lax.fori_loop(0, i + 1, body, jnp.zeros((8, tq), I32))
        return jnp.sum(part, axis=0, keepdims=True)

    def count16(scr, pred):
        one, zero = jnp.int16(1), jnp.int16(0)

        def body(j, part):
            for u in range(per_q):
                hit = jnp.where(pred(scr[j * per_q + u]), one, zero)
                part = part + functools.reduce(
                    jnp.add, [hit[g * 16:(g + 1) * 16, :] for g in range(tk // 16)])
            return part
        part = lax.fori_loop(0, i + 1, body, jnp.zeros((16, tq), I16))
        return jnp.sum(part.astype(I32), axis=0, keepdims=True)

    def search16(scr, base):
        def bit_body(bi, t_u):
            cand_u = t_u | jnp.left_shift(jnp.int32(1), 15 - bi)
            cand = (cand_u - HALF_BIAS).astype(I16)
            return jnp.where(base + count16(scr, lambda k: k >= cand) >= ktop, cand_u, t_u)
        return lax.fori_loop(0, 16, bit_body, jnp.zeros((1, tq), I32))

    t_hi = search16(hi_scr, 0) - HALF_BIAS
    t_hi16 = t_hi.astype(I16)
    above = count16(hi_scr, lambda k: k > t_hi16)

    def keep_band(j, carry):
        for u in range(per_q):
            c = j * per_q + u
            lo_scr[c] = jnp.where(hi_scr[c] == t_hi16, lo_scr[c], jnp.int16(-HALF_BIAS))
        return carry

    lax.fori_loop(0, i + 1, keep_band, 0)
    thr = t_hi * (2 * HALF_BIAS) + search16(lo_scr, above)
    n_ge = count(lambda k: k >= thr)

    @pl.when(jnp.max(n_ge) > ktop)
    def _():
        need = (ktop - count(lambda k: k > thr)).astype(F32)
        lower = (lax.broadcasted_iota(I32, (tk, tk), 1)
                 < lax.broadcasted_iota(I32, (tk, tk), 0)).astype(BF16)

        def body(c, seen):
            k = key_scr[c]
            eq = k == thr
            eqf = eq.astype(BF16)
            rank = seen + _dot(lower, eqf)
            drop = eq & (rank >= need) & (n_ge > ktop)
            key_scr[c] = jnp.where(drop, thr - 1, k)
            return seen + jnp.sum(eqf.astype(F32), axis=0, keepdims=True)

        lax.fori_loop(0, nchunks, body, jnp.zeros((1, tq), F32))

    def logits(c, rel):
        sel = key_scr[c] >= thr
        if rel is not None and rel >= 0:
            sel = sel & causal(rel)
        kbc = kb_ref[pl.ds(pl.multiple_of(c * tk, tk), tk), :]

        def head(hh):
            s = _dot(kbc, qbt_ref[hh])
            if rel is not None:
                s = s + bias_ref[rel + 1, hh]
            return s
        return sel, head

    m_scr[...] = jnp.full(m_scr.shape, NEG_BIG, F32)

    def max_chunk(c, rel):
        sel, head = logits(c, rel)
        for hh in range(N_HEADS):
            s = jnp.where(sel, head(hh), NEG_BIG)
            s_scr[hh, c] = s
            m_scr[hh] = jnp.maximum(m_scr[hh], _fold_rows(s, jnp.maximum))

    for_chunks(max_chunk)
    m = [jnp.max(m_scr[hh], axis=0, keepdims=True) for hh in range(N_HEADS)]

    acc_scr[...] = jnp.zeros(acc_scr.shape, F32)

    def att_block(j, carry):
        for u in range(per_q):
            c = j * per_q + u
            vtc = vt_ref[c]
            for hh in range(N_HEADS):
                p = jnp.exp(s_scr[hh, c] - m[hh])
                acc_scr[hh] += _dot(vtc, p.astype(BF16))
        return carry

    lax.fori_loop(0, i + 1, att_block, 0)

    out_t = jnp.concatenate(
        [acc_scr[hh, :HEAD_DIM, :] / acc_scr[hh, HEAD_DIM:HEAD_DIM + 1, :] for hh in range(N_HEADS)],
        axis=0)
    o_ref[...] = out_t.T.astype(o_ref.dtype)


def _dsa_attention(qit, ki, wit, qbt, kb, vt, bias):
    b, nh, dh, s = qit.shape
    tq = DSA_QUERY_BLOCK
    tk = DSA_KEY_BLOCK
    assert tk == REL_MAX_DIST and tk == LANES and tq % tk == 0 and s % tq == 0
    ktop = min(INDEX_TOPK_MAX, s // 4)
    hmt_q = pl.BlockSpec((None, nh, dh, tq), lambda bb, i: (bb, 0, 0, i))
    full64 = pl.BlockSpec((None, s, dh), lambda bb, i: (bb, 0, 0))
    return pl.pallas_call(
        functools.partial(_dsa_kernel, ktop),
        grid=(b, s // tq),
        in_specs=[hmt_q, full64,
                  pl.BlockSpec((None, nh, tq), lambda bb, i: (bb, 0, i)),
                  hmt_q, full64,
                  pl.BlockSpec((None, s // tk, 2 * dh, tk), lambda bb, i: (bb, 0, 0, 0)),
                  pl.BlockSpec(bias.shape, lambda bb, i: (0, 0, 0, 0))],
        out_specs=pl.BlockSpec((None, tq, nh * dh), lambda bb, i: (bb, i, 0)),
        out_shape=jax.ShapeDtypeStruct((b, s, nh * dh), BF16),
        scratch_shapes=[pltpu.VMEM((s // tk, tk, tq), I32),
                        pltpu.VMEM((s // tk, tk, tq), I16),
                        pltpu.VMEM((s // tk, tk, tq), I16),
                        pltpu.VMEM((nh, 8, tq), F32),
                        pltpu.VMEM((nh, 2 * dh, tq), F32),
                        pltpu.VMEM((nh, s // tk, tk, tq), F32)],
        compiler_params=pltpu.CompilerParams(
            dimension_semantics=("parallel", "parallel"), vmem_limit_bytes=VMEM_LIMIT),
        name="sparse_attention",
    )(qit, ki, wit, qbt, kb, vt, bias)


def _first_max4(a):
    m1 = jnp.maximum(jnp.maximum(a[0], a[1]), jnp.maximum(a[2], a[3]))
    i1 = jnp.where(a[0] == m1, 0, jnp.where(a[1] == m1, 1, jnp.where(a[2] == m1, 2, 3)))
    rest = [jnp.where(i1 == j, -1.0, a[j]) for j in range(4)]
    m2 = jnp.maximum(jnp.maximum(rest[0], rest[1]), jnp.maximum(rest[2], rest[3]))
    i2 = jnp.where(rest[0] == m2, 0, jnp.where(rest[1] == m2, 1, jnp.where(rest[2] == m2, 2, 3)))
    return m1, i1, m2, i2


def _route(lt):
    lg = [lt[e:e + 1, :] for e in range(N_EXPERTS)]
    mx = functools.reduce(jnp.maximum, lg)
    ex = [jnp.exp(v - mx) for v in lg]
    tot = functools.reduce(lambda u, v: u + v, ex)
    p = [v / tot for v in ex]
    gs = []
    for g in range(N_GROUPS):
        m1, _, m2, _ = _first_max4(p[4 * g:4 * g + 4])
        gs.append(m1 + m2)
    _, gbest, _, _ = _first_max4(gs)
    chosen = [jnp.where(gbest == 0, p[j], jnp.where(gbest == 1, p[4 + j],
              jnp.where(gbest == 2, p[8 + j], p[12 + j]))) for j in range(4)]
    m1, i1, m2, i2 = _first_max4(chosen)
    den = m1 + m2
    lo = jnp.minimum(i1, i2)
    hi = jnp.maximum(i1, i2)
    pair = jnp.where(lo == 0, hi - 1, jnp.where(lo == 1, hi + 1, N_PAIRS - 1))
    first_is_lo = i1 < i2
    w1 = m1 / den
    w2 = m2 / den
    return gbest * N_PAIRS + pair, jnp.where(first_is_lo, w1, w2), jnp.where(first_is_lo, w2, w1)


def _merge_kernel(oa_ref, ob_ref, sga_ref, sgb_ref, x_ref, mod_ref, woa_ref, wob_ref, wout_ref,
                  g2_ref, rwh_ref, rwl_ref, rb_ref, x1_ref, h2x_ref, info_ref, cnt_ref, carry_scr):
    tm, d = x_ref.shape

    @pl.when((pl.program_id(0) == 0) & (pl.program_id(1) == 0))
    def _():
        carry_scr[...] = jnp.zeros(carry_scr.shape, F32)

    merged = (sga_ref[...].astype(F32) * _dot(oa_ref[...], woa_ref[...])
              + sgb_ref[...].astype(F32) * _dot(ob_ref[...], wob_ref[...]))
    x1 = x_ref[...] + mod_ref[2:3, :] * _dot(merged.astype(BF16), wout_ref[...])
    x1_ref[...] = x1
    h2 = (_rms(x1) * g2_ref[...]) * (1.0 + mod_ref[4:5, :]) + mod_ref[3:4, :]
    h2x_ref[:, :d] = h2
    hi = h2.astype(BF16)
    lo = (h2 - hi.astype(F32)).astype(BF16)
    logits = (_dot(hi, rwh_ref[...]) + _dot(lo, rwh_ref[...]) + _dot(hi, rwl_ref[...])) + rb_ref[...]
    cls, cw_lo, cw_hi = _route(logits.T)

    ncls = carry_scr.shape[0]
    onehot = (lax.broadcasted_iota(I32, (ncls, tm), 0) == cls).astype(F32)
    earlier = (lax.broadcasted_iota(I32, (tm, tm), 0)
               < lax.broadcasted_iota(I32, (tm, tm), 1)).astype(BF16)
    before = _dot(onehot.astype(BF16), earlier)
    carry = carry_scr[...]
    carry_t = jnp.concatenate([carry] * (tm // LANES), axis=1)
    rank = jnp.sum(onehot * (carry_t + before), axis=0, keepdims=True).astype(I32)
    carry = carry + jnp.sum(onehot, axis=1, keepdims=True)
    carry_scr[...] = carry
    cnt_ref[...] = carry.astype(I32)

    sub = lax.broadcasted_iota(I32, (8, tm), 0)
    info_ref[...] = jnp.where(sub == 0, cls, jnp.where(sub == 1, rank, 0))
    sub = lax.broadcasted_iota(I32, (LANES, tm), 0)
    weights_t = jnp.where(sub == 0, cw_lo, jnp.where(sub == 1, cw_hi, 0.0))
    h2x_ref[:, d:] = weights_t.T


def _merge(oa, ob, sga, sgb, x, mod_l, woa, wob, wout, g2, rwh, rwl, rb, tm):
    b, s, d = x.shape
    hw = oa.shape[-1]
    nt = s // tm
    tok = lambda n: pl.BlockSpec((None, tm, n), lambda bb, i: (bb, i, 0))
    const = lambda r, c: pl.BlockSpec((r, c), lambda bb, i: (0, 0))
    return pl.pallas_call(
        _merge_kernel,
        grid=(b, nt),
        in_specs=[tok(hw), tok(hw), tok(d), tok(d), tok(d),
                  pl.BlockSpec((None, N_MOD, d), lambda bb, i: (bb, 0, 0)),
                  const(hw, d), const(hw, d), const(d, d), const(1, d),
                  const(d, LANES), const(d, LANES), const(1, LANES)],
        out_specs=[tok(d), tok(d + LANES),
                   pl.BlockSpec((8, tm), lambda bb, i: (0, bb * nt + i)),
                   const(N_CLASS_PAD, LANES)],
        out_shape=[jax.ShapeDtypeStruct((b, s, d), F32),
                   jax.ShapeDtypeStruct((b, s, d + LANES), F32),
                   jax.ShapeDtypeStruct((8, b * s), I32),
                   jax.ShapeDtypeStruct((N_CLASS_PAD, LANES), I32)],
        scratch_shapes=[pltpu.VMEM((N_CLASS_PAD, LANES), F32)],
        compiler_params=pltpu.CompilerParams(
            dimension_semantics=("arbitrary", "arbitrary"), vmem_limit_bytes=VMEM_LIMIT),
        name="merge_route",
    )(oa, ob, sga, sgb, x, mod_l, woa, wob, wout, g2, rwh, rwl, rb)


def _positions_kernel(offs_ref, info_ref, pos_ref):
    cls = info_ref[0:1, :]
    pos = info_ref[1:2, :]
    for c in range(N_CLASSES):
        pos = pos + jnp.where(cls == c, offs_ref[c], 0)
    pos_ref[...] = jnp.broadcast_to(pos, pos_ref.shape)


def _positions(info, offs):
    return pl.pallas_call(
        _positions_kernel,
        in_specs=[pl.BlockSpec(memory_space=pltpu.SMEM), pl.BlockSpec(info.shape, lambda: (0, 0))],
        out_specs=pl.BlockSpec(info.shape, lambda: (0, 0)),
        out_shape=jax.ShapeDtypeStruct(info.shape, I32),
        name="positions",
    )(offs, info)[0]


def _dispatch_kernel(ends_ref, pos_ref, src_ref, dst_ref, zero_scr, sem, zero_sem):
    tm = pos_ref.shape[1]
    row_tile = zero_scr.shape[0]

    @pl.when(pl.program_id(0) == 0)
    def _():
        zero_scr[...] = jnp.zeros(zero_scr.shape, F32)

        def zero_tail(c):
            start = pl.multiple_of(ends_ref[c] - row_tile, row_tile)
            return pltpu.make_async_copy(zero_scr, dst_ref.at[pl.ds(start, row_tile), :], zero_sem)

        def nonempty(c):
            return ends_ref[c] > (ends_ref[c - 1] if c else 0)

        for c in range(N_CLASSES):
            @pl.when(nonempty(c))
            def _(c=c):
                zero_tail(c).start()
        for c in range(N_CLASSES):
            @pl.when(nonempty(c))
            def _(c=c):
                zero_tail(c).wait()

        def unused_tile(j):
            start = pl.multiple_of(ends_ref[N_CLASSES - 1] + j * row_tile, row_tile)
            return pltpu.make_async_copy(zero_scr, dst_ref.at[pl.ds(start, row_tile), :], zero_sem)

        n_unused = (dst_ref.shape[0] - ends_ref[N_CLASSES - 1]) // row_tile
        lax.fori_loop(0, n_unused, lambda j, carry: (unused_tile(j).start(), carry)[1], 0)
        lax.fori_loop(0, n_unused, lambda j, carry: (unused_tile(j).wait(), carry)[1], 0)

    for r in range(tm):
        pltpu.make_async_copy(src_ref.at[pl.ds(r, 1), :], dst_ref.at[pl.ds(pos_ref[0, r], 1), :],
                              sem).start(priority=r % 2)
    pltpu.make_async_copy(src_ref, dst_ref.at[pl.ds(0, tm), :], sem).wait()


def _dispatch(h2x, pos, ends, n_rows):
    t, w = h2x.shape
    tm = MOE_DMA_TILE
    smem_row = pl.BlockSpec((None, 1, tm), lambda i: (i, 0, 0), memory_space=pltpu.SMEM)
    return pl.pallas_call(
        _dispatch_kernel,
        grid=(t // tm,),
        in_specs=[pl.BlockSpec(memory_space=pltpu.SMEM), smem_row, pl.BlockSpec((tm, w), lambda i: (i, 0))],
        out_specs=pl.BlockSpec(memory_space=pl.ANY),
        out_shape=jax.ShapeDtypeStruct((n_rows, w), F32),
        scratch_shapes=[pltpu.VMEM((MOE_ROW_TILE, w), F32), pltpu.SemaphoreType.DMA, pltpu.SemaphoreType.DMA],
        compiler_params=pltpu.CompilerParams(
            dimension_semantics=("arbitrary",), vmem_limit_bytes=VMEM_LIMIT),
        name="dispatch",
    )(ends, pos, h2x)


def _experts_kernel(elo_ref, ehi_ref, live_ref, xs_ref, wg_lo, wg_hi, wu_lo, wu_hi, wd_lo, wd_hi, ys_ref):
    del elo_ref, ehi_ref
    d = ys_ref.shape[1]
    j = pl.program_id(0)

    @pl.when(live_ref[j] == 1)
    def _():
        h = xs_ref[:, :d].astype(BF16)

        def expert(wg, wu, wd, weight):
            gate = _dot(h, wg[...])
            hidden = ((gate * jax.nn.sigmoid(gate)) * _dot(h, wu[...])) * weight
            return _dot(hidden.astype(BF16), wd[...])

        ys_ref[...] = (expert(wg_lo, wu_lo, wd_lo, xs_ref[:, d:d + 1])
                       + expert(wg_hi, wu_hi, wd_hi, xs_ref[:, d + 1:d + 2]))

    @pl.when(live_ref[j] == 0)
    def _():
        ys_ref[...] = jnp.zeros(ys_ref.shape, F32)


def _experts(xs, e_lo, e_hi, live, wg, wu, wd):
    n_rows, w = xs.shape
    ne, d, f = wg.shape
    tm = MOE_ROW_TILE
    lo = lambda j, elo, ehi, lv: (elo[j], 0, 0)
    hi = lambda j, elo, ehi, lv: (ehi[j], 0, 0)
    grid_spec = pltpu.PrefetchScalarGridSpec(
        num_scalar_prefetch=3,
        grid=(n_rows // tm,),
        in_specs=[pl.BlockSpec((tm, w), lambda j, elo, ehi, lv: (j, 0)),
                  pl.BlockSpec((None, d, f), lo), pl.BlockSpec((None, d, f), hi),
                  pl.BlockSpec((None, d, f), lo), pl.BlockSpec((None, d, f), hi),
                  pl.BlockSpec((None, f, d), lo), pl.BlockSpec((None, f, d), hi)],
        out_specs=pl.BlockSpec((tm, d), lambda j, elo, ehi, lv: (j, 0)),
    )
    return pl.pallas_call(
        _experts_kernel,
        grid_spec=grid_spec,
        out_shape=jax.ShapeDtypeStruct((n_rows, d), F32),
        compiler_params=pltpu.CompilerParams(
            dimension_semantics=("arbitrary",), vmem_limit_bytes=VMEM_LIMIT),
        name="experts",
    )(e_lo, e_hi, live, xs, wg, wg, wu, wu, wd, wd)


def _combine_kernel(pos_ref, next_ref, ys_ref, x1_ref, mod_ref, o_ref, buf, sem):
    tm = buf.shape[1]
    step = pl.program_id(0) * pl.num_programs(1) + pl.program_id(1)
    last = pl.num_programs(0) * pl.num_programs(1) - 1

    def issue(p_ref, slot):
        for r in range(tm):
            pltpu.make_async_copy(ys_ref.at[pl.ds(p_ref[slot, r], 1), :], buf.at[slot, pl.ds(r, 1), :],
                                  sem.at[slot]).start(priority=r % 2)

    def finish(slot):
        pltpu.make_async_copy(ys_ref.at[pl.ds(0, tm), :], buf.at[slot], sem.at[slot]).wait()
        rows = pl.ds(slot * tm, tm)
        o_ref[rows, :] = x1_ref[rows, :] + mod_ref[5:6, :] * buf[slot]

    @pl.when(step == 0)
    def _():
        issue(pos_ref, 0)

    issue(pos_ref, 1)
    finish(0)

    @pl.when(step < last)
    def _():
        issue(next_ref, 0)

    finish(1)


def _combine(ys, pos, x1, mod_l):
    b, s, d = x1.shape
    tm = MOE_DMA_TILE
    nt = s // (2 * tm)
    n_steps = b * nt
    pos2 = pos.reshape(n_steps, 2, tm)
    this_step = pl.BlockSpec((None, 2, tm), lambda bb, i: (bb * nt + i, 0, 0), memory_space=pltpu.SMEM)
    next_step = pl.BlockSpec((None, 2, tm), lambda bb, i: (jnp.minimum(bb * nt + i + 1, n_steps - 1), 0, 0),
                             memory_space=pltpu.SMEM)
    return pl.pallas_call(
        _combine_kernel,
        grid=(b, nt),
        in_specs=[this_step, next_step,
                  pl.BlockSpec(memory_space=pl.ANY),
                  pl.BlockSpec((None, 2 * tm, d), lambda bb, i: (bb, i, 0)),
                  pl.BlockSpec((None, N_MOD, d), lambda bb, i: (bb, 0, 0))],
        out_specs=pl.BlockSpec((None, 2 * tm, d), lambda bb, i: (bb, i, 0)),
        out_shape=jax.ShapeDtypeStruct((b, s, d), F32),
        scratch_shapes=[pltpu.VMEM((2, tm, d), F32), pltpu.SemaphoreType.DMA((2,))],
        compiler_params=pltpu.CompilerParams(
            dimension_semantics=("arbitrary", "arbitrary"), vmem_limit_bytes=VMEM_LIMIT),
        name="combine",
    )(pos2, pos2, ys, x1, mod_l)


def _moe(h2x, info, counts, wg, wu, wd, x1, mod_l):
    b, s, d = x1.shape
    t = b * s
    tm = MOE_ROW_TILE
    n_tiles = t // tm + N_CLASSES
    counts = counts[:N_CLASSES, 0]
    padded = (counts + tm - 1) // tm * tm
    ends = jnp.cumsum(padded)
    offs = jnp.pad(ends - padded, (0, N_CLASS_PAD - N_CLASSES)).astype(I32)
    tile_cls = jnp.sum(jnp.arange(n_tiles, dtype=I32)[:, None] * tm >= ends[None, :], axis=1)
    live = (tile_cls < N_CLASSES).astype(I32)
    tile_cls = jnp.minimum(tile_cls, N_CLASSES - 1)
    group, pair = tile_cls // N_PAIRS, tile_cls % N_PAIRS
    pairs = jnp.asarray(_PAIRS, I32)
    e_lo = group * EXPERTS_PER_GROUP + pairs[pair, 0]
    e_hi = group * EXPERTS_PER_GROUP + pairs[pair, 1]

    pos = _positions(info, offs).reshape(t // MOE_DMA_TILE, 1, MOE_DMA_TILE)
    xs = _dispatch(h2x.reshape(t, d + LANES), pos, ends.astype(I32), n_tiles * tm)
    ys = _experts(xs, e_lo, e_hi, live, wg, wu, wd)
    return _combine(ys, pos, x1, mod_l)


def kernel(x, c, w_ada, b_ada, norm1_g, w_in, q_norm_g, k_norm_g, rel_bias, w_o_a, w_o_b, w_out,
           norm2_g, router_w, router_b, w_gate, w_up, w_down):
    b, s, d = x.shape
    depth = w_ada.shape[0]
    assert d % LANES == 0
    tm = min(512, s)

    mod = _adaln(c, w_ada, b_ada)
    bias = _bias_tiles(rel_bias, DSA_QUERY_BLOCK, DSA_KEY_BLOCK)
    rw = jnp.pad(router_w, ((0, 0), (0, LANES - N_EXPERTS)))
    rwh = rw.astype(BF16)
    rwl = (rw - rwh.astype(F32)).astype(BF16)
    rb = jnp.pad(router_b, (0, LANES - N_EXPERTS)).reshape(1, LANES)

    for l in range(depth):
        qa, ka, va, qbt, qit, kb, vt, ki, wit, sga, sgb = _in_proj(
            x, mod[l], norm1_g[l].reshape(1, d), _pack_w_in(w_in[l]),
            q_norm_g[l].reshape(HEAD_DIM, 1), k_norm_g[l].reshape(1, HEAD_DIM), tm)
        oa = _sb_attention(qa, ka, va)
        ob = _dsa_attention(qit, ki, wit, qbt, kb, vt, bias)
        x1, h2x, info, counts = _merge(oa, ob, sga, sgb, x, mod[l], w_o_a[l].astype(BF16),
                                       w_o_b[l].astype(BF16), w_out[l].astype(BF16),
                                       norm2_g[l].reshape(1, d), rwh, rwl, rb, tm)
        x = _moe(h2x, info, counts, w_gate[l].astype(BF16), w_up[l].astype(BF16),
                 w_down[l].astype(BF16), x1, mod[l])
    return x
```

```python
import functools
import math

import numpy as np
import jax
import jax.numpy as jnp
from jax import lax
from jax.experimental import pallas as pl
from jax.experimental.pallas import tpu as pltpu

F32 = jnp.float32
BF16 = jnp.bfloat16
I32 = jnp.int32
I16 = jnp.int16

HEAD_DIM = 64
N_HEADS = 8
N_EXPERTS = 16
N_GROUPS = 4
EXPERTS_PER_GROUP = 4
_PAIRS = ((0, 1), (0, 2), (0, 3), (1, 2), (1, 3), (2, 3))
N_PAIRS = len(_PAIRS)
N_CLASSES = N_GROUPS * N_PAIRS
N_CLASS_PAD = 32
MOE_ROW_TILE = 256
MOE_DMA_TILE = 512
N_MOD = 6
EPS = 1e-6
INDEX_TOPK_MAX = 256
REL_BUCKETS = 32
REL_MAX_DIST = 128
LANES = 128
DSA_QUERY_BLOCK = 256
DSA_KEY_BLOCK = 128
SB_QUERY_BLOCK = 256
SB_KEY_BLOCK = 256
SB_DEAD_LOG2 = 150.0
LOG2E = 1.4426950408889634
NEG_BIG = -1e30
INT_MIN = -2 ** 31
HALF_BIAS = 2 ** 15

VMEM_LIMIT = 56 * 1024 * 1024


def _nt_dot(a, b):
    return lax.dot_general(a, b, (((1,), (1,)), ((), ())), preferred_element_type=F32)


def _dot(a, b):
    return jnp.dot(a, b, preferred_element_type=F32)


def _mod_kernel(c_ref, w_ref, b_ref, o_ref):
    c = c_ref[...]
    sc = c * jax.nn.sigmoid(c)
    o_ref[...] = jnp.dot(sc, w_ref[...], preferred_element_type=F32,
                         precision=lax.Precision.HIGHEST) + b_ref[...]


def _adaln(c, w_ada, b_ada):
    depth, d, nd = w_ada.shape
    b = c.shape[0]
    out = pl.pallas_call(
        _mod_kernel,
        grid=(depth, nd // d),
        in_specs=[pl.BlockSpec((b, d), lambda l, j: (0, 0)),
                  pl.BlockSpec((None, d, d), lambda l, j: (l, 0, j)),
                  pl.BlockSpec((None, 1, d), lambda l, j: (l, 0, j))],
        out_specs=pl.BlockSpec((None, b, d), lambda l, j: (l, 0, j)),
        out_shape=jax.ShapeDtypeStruct((depth, b, nd), F32),
        compiler_params=pltpu.CompilerParams(vmem_limit_bytes=VMEM_LIMIT),
        name="adaln",
    )(c, w_ada, b_ada.reshape(depth, 1, nd))
    return out.reshape(depth, b, N_MOD, d)


def _rms(x):
    return x * lax.rsqrt(jnp.mean(x * x, axis=-1, keepdims=True) + EPS)


def _in_kernel(x_ref, mod_ref, g1_ref, w_ref, qng_ref, kng_ref,
               qa_ref, ka_ref, va_ref, qbt_ref, qit_ref, kb_ref, vt_ref, ki_ref, wit_ref,
               sga_ref, sgb_ref):
    d = x_ref.shape[-1]
    tm = x_ref.shape[0]
    hw = N_HEADS * HEAD_DIM
    x = x_ref[...]
    h = (_rms(x) * g1_ref[...]) * (1.0 + mod_ref[1:2, :]) + mod_ref[0:1, :]
    hb = h.astype(BF16)

    def proj(lo, n):
        return _dot(hb, w_ref[:, lo:lo + n])

    def heads(r, ref):
        for hh in range(N_HEADS):
            ref[hh] = r[:, hh * HEAD_DIM:(hh + 1) * HEAD_DIM].astype(ref.dtype)

    scale = HEAD_DIM ** -0.5
    heads(proj(0, hw) * (scale * LOG2E), qa_ref)
    heads(proj(hw, hw), ka_ref)
    heads(proj(2 * hw, hw), va_ref)

    rt = proj(3 * hw, hw).T
    for hh in range(N_HEADS):
        slab = rt[hh * HEAD_DIM:(hh + 1) * HEAD_DIM, :]
        inv = lax.rsqrt(jnp.mean(slab * slab, axis=0, keepdims=True) + EPS)
        qbt_ref[hh] = (((slab * inv) * qng_ref[...]) * scale).astype(BF16)

    r = proj(4 * hw, 2 * HEAD_DIM)
    kb_ref[...] = (_rms(r[:, :HEAD_DIM]) * kng_ref[...]).astype(BF16)
    vt = r.T[HEAD_DIM:, :]
    ones_row = (lax.broadcasted_iota(I32, (HEAD_DIM, LANES), 0) == 0).astype(BF16)
    for cc in range(tm // LANES):
        vt_ref[cc, :HEAD_DIM, :] = vt[:, cc * LANES:(cc + 1) * LANES].astype(BF16)
        vt_ref[cc, HEAD_DIM:, :] = ones_row

    rt = proj(4 * hw + LANES, hw).T
    for hh in range(N_HEADS):
        qit_ref[hh] = rt[hh * HEAD_DIM:(hh + 1) * HEAD_DIM, :].astype(BF16)

    r = proj(5 * hw + LANES, LANES)
    ki_ref[...] = r[:, :HEAD_DIM].astype(BF16)
    wit_ref[...] = r.T[HEAD_DIM:HEAD_DIM + N_HEADS, :]

    off = 5 * hw + 2 * LANES
    sga_ref[...] = jax.nn.sigmoid(proj(off, d)).astype(BF16)
    sgb_ref[...] = jax.nn.sigmoid(proj(off + d, d)).astype(BF16)


def _pack_w_in(w_in_l):
    hw = N_HEADS * HEAD_DIM
    a = 4 * hw + 2 * HEAD_DIM + hw
    small = w_in_l[:, a:a + HEAD_DIM + N_HEADS]
    small = jnp.pad(small, ((0, 0), (0, LANES - small.shape[1])))
    gates = w_in_l[:, a + HEAD_DIM + N_HEADS:]
    return jnp.concatenate([w_in_l[:, :a], small, gates], axis=1).astype(BF16)


def _in_proj(x, mod_l, g1, w_packed, qng, kng, tm):
    b, s, d = x.shape
    hm = jax.ShapeDtypeStruct((b, N_HEADS, s, HEAD_DIM), BF16)
    hmt = jax.ShapeDtypeStruct((b, N_HEADS, HEAD_DIM, s), BF16)
    tok64 = jax.ShapeDtypeStruct((b, s, HEAD_DIM), BF16)
    hm_spec = pl.BlockSpec((None, N_HEADS, tm, HEAD_DIM), lambda bb, i: (bb, 0, i, 0))
    hmt_spec = pl.BlockSpec((None, N_HEADS, HEAD_DIM, tm), lambda bb, i: (bb, 0, 0, i))
    tok = lambda n: pl.BlockSpec((None, tm, n), lambda bb, i: (bb, i, 0))
    nw = w_packed.shape[1]
    return pl.pallas_call(
        _in_kernel,
        grid=(b, s // tm),
        in_specs=[tok(d),
                  pl.BlockSpec((None, N_MOD, d), lambda bb, i: (bb, 0, 0)),
                  pl.BlockSpec((1, d), lambda bb, i: (0, 0)),
                  pl.BlockSpec((d, nw), lambda bb, i: (0, 0)),
                  pl.BlockSpec((HEAD_DIM, 1), lambda bb, i: (0, 0)),
                  pl.BlockSpec((1, HEAD_DIM), lambda bb, i: (0, 0))],
        out_specs=[hm_spec] * 3 + [hmt_spec] * 2
                  + [tok(HEAD_DIM),
                     pl.BlockSpec((None, tm // LANES, 2 * HEAD_DIM, LANES), lambda bb, i: (bb, i, 0, 0)),
                     tok(HEAD_DIM),
                     pl.BlockSpec((None, N_HEADS, tm), lambda bb, i: (bb, 0, i)),
                     tok(d), tok(d)],
        out_shape=[hm] * 3 + [hmt] * 2
                  + [tok64,
                     jax.ShapeDtypeStruct((b, s // LANES, 2 * HEAD_DIM, LANES), BF16),
                     tok64,
                     jax.ShapeDtypeStruct((b, N_HEADS, s), F32),
                     jax.ShapeDtypeStruct((b, s, d), BF16),
                     jax.ShapeDtypeStruct((b, s, d), BF16)],
        compiler_params=pltpu.CompilerParams(
            dimension_semantics=("parallel", "parallel"), vmem_limit_bytes=VMEM_LIMIT),
        name="in_proj",
    )(x, mod_l, g1, w_packed, qng, kng)


def _sb_kernel(q_ref, k_ref, v_ref, o_ref, acc_scr, csum_scr, sp_scr, ls_scr):
    tq = q_ref.shape[1]
    tk = SB_KEY_BLOCK
    i = pl.program_id(1)
    nfull = (i * tq) // tk
    row = lax.broadcasted_iota(I32, (tq, tk), 0)
    col = lax.broadcasted_iota(I32, (tq, tk), 1)
    before = (col - row) < (i * tq - nfull * tk)
    krow = lax.broadcasted_iota(I32, (tk, tk), 0)
    kcol = lax.broadcasted_iota(I32, (tk, tk), 1)
    suffix = (krow > kcol).astype(BF16)

    def stage_a(hh, st, first):
        z2 = _nt_dot(q_ref[hh], k_ref[hh, pl.ds(st, tk), :])
        sp2 = jnp.maximum(jnp.log2(1.0 + jnp.exp2(jnp.minimum(z2, 126.0))), z2)
        if first:
            sp2 = jnp.where(before, sp2, 0.0)
        sp_scr[hh] = sp2.astype(BF16)
        ls_scr[hh] = z2 - sp2

    def stage_b(hh, st, first):
        sp = sp_scr[hh]
        suf = _dot(sp, suffix)
        if first:
            a = jnp.where(before, jnp.exp2(ls_scr[hh] - suf), 0.0)
            csum = jnp.zeros((tq, 1), F32)
        else:
            csum = csum_scr[hh]
            a = jnp.exp2((ls_scr[hh] - suf) - csum)
        c = _dot(a.astype(BF16), v_ref[hh, pl.ds(st, tk), :])
        if first:
            acc_scr[hh] = c
        else:
            acc_scr[hh] += c
        csum_scr[hh] = csum + (suf[:, 0:1] + sp[:, 0:1].astype(F32))

    def key_block(st, first):
        for hh in range(N_HEADS):
            stage_a(hh, st, first)
        for hh in range(N_HEADS):
            stage_b(hh, st, first)

    key_block(pl.multiple_of(nfull * tk, tk), True)

    def live():
        return jnp.min(csum_scr[...]) < SB_DEAD_LOG2

    def body(carry):
        jj, _ = carry
        key_block(pl.multiple_of((nfull - 1 - jj) * tk, tk), False)
        return jj + 1, live()

    lax.while_loop(lambda carry: (carry[0] < nfull) & carry[1], body, (jnp.int32(0), live()))
    for hh in range(N_HEADS):
        o_ref[:, hh * HEAD_DIM:(hh + 1) * HEAD_DIM] = acc_scr[hh].astype(o_ref.dtype)


def _sb_attention(qa, ka, va):
    b, nh, s, dh = qa.shape
    tq = SB_QUERY_BLOCK
    assert SB_KEY_BLOCK % tq == 0 and s % SB_KEY_BLOCK == 0
    return pl.pallas_call(
        _sb_kernel,
        grid=(b, s // tq),
        in_specs=[pl.BlockSpec((None, nh, tq, dh), lambda bb, i: (bb, 0, i, 0)),
                  pl.BlockSpec((None, nh, s, dh), lambda bb, i: (bb, 0, 0, 0)),
                  pl.BlockSpec((None, nh, s, dh), lambda bb, i: (bb, 0, 0, 0))],
        out_specs=pl.BlockSpec((None, tq, nh * dh), lambda bb, i: (bb, i, 0)),
        out_shape=jax.ShapeDtypeStruct((b, s, nh * dh), BF16),
        scratch_shapes=[pltpu.VMEM((nh, tq, dh), F32), pltpu.VMEM((nh, tq, 1), F32),
                        pltpu.VMEM((nh, tq, SB_KEY_BLOCK), BF16), pltpu.VMEM((nh, tq, SB_KEY_BLOCK), F32)],
        compiler_params=pltpu.CompilerParams(
            dimension_semantics=("parallel", "parallel"), vmem_limit_bytes=VMEM_LIMIT),
        name="stick_breaking",
    )(qa, ka, va)


def _t5_bucket_np(dist):
    max_exact = REL_BUCKETS // 2
    d_f = np.maximum(dist, 1).astype(np.float32)
    large = max_exact + (np.log(d_f / np.float32(max_exact))
                         / np.float32(math.log(REL_MAX_DIST / max_exact))
                         * np.float32(REL_BUCKETS - max_exact)).astype(np.int32)
    large = np.minimum(large, REL_BUCKETS - 1)
    return np.where(dist < max_exact, dist, large).astype(np.int32)


def _bias_tiles(rel_bias, tq, tk):
    n_near = tq // tk + 1
    key = np.arange(tk)[:, None]
    query = np.arange(tq)[None, :]
    bk = np.stack([_t5_bucket_np(np.maximum(query - key - (n - 1) * tk, 0)) for n in range(n_near)])
    far = int(_t5_bucket_np(np.array([REL_MAX_DIST]))[0])
    nh = rel_bias.shape[1]

    def body(rb_ref, bk_ref, o_ref):
        hh = pl.program_id(1)
        buckets = bk_ref[...]
        tile = jnp.zeros((tk, tq), F32)
        for bucket in range(REL_BUCKETS):
            tile = jnp.where(buckets == bucket, rb_ref[hh, bucket] - rb_ref[hh, far], tile)
        o_ref[...] = tile

    return pl.pallas_call(
        body,
        grid=(n_near, nh),
        in_specs=[pl.BlockSpec(memory_space=pltpu.SMEM),
                  pl.BlockSpec((None, tk, tq), lambda n, hh: (n, 0, 0))],
        out_specs=pl.BlockSpec((None, None, tk, tq), lambda n, hh: (n, hh, 0, 0)),
        out_shape=jax.ShapeDtypeStruct((n_near, nh, tk, tq), F32),
        name="bias_tiles",
    )(rel_bias.T, jnp.asarray(bk))


def _sort_key(x):
    bits = pltpu.bitcast(x, I32)
    return bits ^ ((bits >> 31) & 0x7FFFFFFF)


def _fold_rows(x, op):
    return functools.reduce(op, [x[g * 8:(g + 1) * 8, :] for g in range(x.shape[0] // 8)])


def _dsa_kernel(ktop, qit_ref, ki_ref, wit_ref, qbt_ref, kb_ref, vt_ref, bias_ref, o_ref,
                key_scr, hi_scr, lo_scr, m_scr, acc_scr, s_scr):
    tq = qit_ref.shape[2]
    tk = DSA_KEY_BLOCK
    per_q = tq // tk
    i = pl.program_id(1)
    first_own = per_q * i
    nchunks = per_q * (i + 1)
    key_i = lax.broadcasted_iota(I32, (tk, tq), 0)
    qry_i = lax.broadcasted_iota(I32, (tk, tq), 1)

    def causal(rel):
        return (key_i - qry_i) <= (-rel * tk)

    def for_chunks(fn):
        def far(j, carry):
            for u in range(per_q):
                fn(j * per_q + u, None)
            return carry
        lax.fori_loop(0, i - 1, far, 0)

        @pl.when(i >= 1)
        def _():
            for rel in range(-per_q, -1):
                fn(first_own + rel, None)
            fn(first_own - 1, -1)

        for rel in range(per_q):
            fn(first_own + rel, rel)

    def score_chunk(c, rel):
        kic = ki_ref[pl.ds(pl.multiple_of(c * tk, tk), tk), :]
        score = jnp.zeros((tk, tq), F32)
        for hh in range(N_HEADS):
            score = score + wit_ref[hh:hh + 1, :] * jnp.maximum(_dot(kic, qit_ref[hh]), 0.0)
        keys = _sort_key(score)
        if rel is not None and rel >= 0:
            keys = jnp.where(causal(rel), keys, INT_MIN)
        key_scr[c] = keys
        hi_scr[c] = (keys >> 16).astype(I16)
        lo_scr[c] = ((keys & 0xFFFF) - HALF_BIAS).astype(I16)

    for_chunks(score_chunk)

    def count(pred):
        def body(j, part):
            for u in range(per_q):
                part = part + _fold_rows(jnp.where(pred(key_scr[j * per_q + u]), 1, 0), jnp.add)
            return part
        part = lax.fori_loop(0, i + 1, body, jnp.zeros((8, tq), I32))
        return jnp.sum(part, axis=0, keepdims=True)

    def count16(scr, pred):
        one, zero = jnp.int16(1), jnp.int16(0)

        def block(j, part):
            for u in range(per_q):
                hit = jnp.where(pred(scr[j * per_q + u]), one, zero)
                part = part + functools.reduce(
                    jnp.add, [hit[g * 16:(g + 1) * 16, :] for g in range(tk // 16)])
            return part

        nblk = i + 1
        part = lax.fori_loop(0, nblk // 2, lambda jj, part: block(2 * jj + 1, block(2 * jj, part)),
                             jnp.zeros((16, tq), I16))
        part = lax.cond(nblk % 2 == 1, lambda p: block(nblk - 1, p), lambda p: p, part)
        return jnp.sum(part.astype(I32), axis=0, keepdims=True)

    def search16(scr, base):
        def bit_body(bi, t_u):
            cand_u = t_u | jnp.left_shift(jnp.int32(1), 15 - bi)
            cand = (cand_u - HALF_BIAS).astype(I16)
            return jnp.where(base + count16(scr, lambda k: k >= cand) >= ktop, cand_u, t_u)
        return lax.fori_loop(0, 16, bit_body, jnp.zeros((1, tq), I32))

    t_hi = search16(hi_scr, 0) - HALF_BIAS
    t_hi16 = t_hi.astype(I16)
    above = count16(hi_scr, lambda k: k > t_hi16)

    def keep_band(j, carry):
        for u in range(per_q):
            c = j * per_q + u
            lo_scr[c] = jnp.where(hi_scr[c] == t_hi16, lo_scr[c], jnp.int16(-HALF_BIAS))
        return carry

    lax.fori_loop(0, i + 1, keep_band, 0)
    thr = t_hi * (2 * HALF_BIAS) + search16(lo_scr, above)
    n_ge = count(lambda k: k >= thr)

    @pl.when(jnp.max(n_ge) > ktop)
    def _():
        need = (ktop - count(lambda k: k > thr)).astype(F32)
        lower = (lax.broadcasted_iota(I32, (tk, tk), 1)
                 < lax.broadcasted_iota(I32, (tk, tk), 0)).astype(BF16)

        def body(c, seen):
            k = key_scr[c]
            eq = k == thr
            eqf = eq.astype(BF16)
            rank = seen + _dot(lower, eqf)
            drop = eq & (rank >= need) & (n_ge > ktop)
            key_scr[c] = jnp.where(drop, thr - 1, k)
            return seen + jnp.sum(eqf.astype(F32), axis=0, keepdims=True)

        lax.fori_loop(0, nchunks, body, jnp.zeros((1, tq), F32))

    def logits(c, rel):
        sel = key_scr[c] >= thr
        if rel is not None and rel >= 0:
            sel = sel & causal(rel)
        kbc = kb_ref[pl.ds(pl.multiple_of(c * tk, tk), tk), :]

        def head(hh):
            s = _dot(kbc, qbt_ref[hh])
            if rel is not None:
                s = s + bias_ref[rel + 1, hh]
            return s
        return sel, head

    m_scr[...] = jnp.full(m_scr.shape, NEG_BIG, F32)

    def max_chunk(c, rel):
        sel, head = logits(c, rel)
        for hh in range(N_HEADS):
            s = jnp.where(sel, head(hh), NEG_BIG)
            s_scr[hh, c] = s
            m_scr[hh] = jnp.maximum(m_scr[hh], _fold_rows(s, jnp.maximum))

    for_chunks(max_chunk)
    m = [jnp.max(m_scr[hh], axis=0, keepdims=True) for hh in range(N_HEADS)]

    acc_scr[...] = jnp.zeros(acc_scr.shape, F32)

    def att_block(j, carry):
        for u in range(per_q):
            c = j * per_q + u
            vtc = vt_ref[c]
            for hh in range(N_HEADS):
                p = jnp.exp(s_scr[hh, c] - m[hh])
                acc_scr[hh] += _dot(vtc, p.astype(BF16))
        return carry

    lax.fori_loop(0, i + 1, att_block, 0)

    out_t = jnp.concatenate(
        [acc_scr[hh, :HEAD_DIM, :] / acc_scr[hh, HEAD_DIM:HEAD_DIM + 1, :] for hh in range(N_HEADS)],
        axis=0)
    o_ref[...] = out_t.T.astype(o_ref.dtype)


def _dsa_attention(qit, ki, wit, qbt, kb, vt, bias):
    b, nh, dh, s = qit.shape
    tq = DSA_QUERY_BLOCK
    tk = DSA_KEY_BLOCK
    assert tk == REL_MAX_DIST and tk == LANES and tq % tk == 0 and s % tq == 0
    ktop = min(INDEX_TOPK_MAX, s // 4)
    hmt_q = pl.BlockSpec((None, nh, dh, tq), lambda bb, i: (bb, 0, 0, i))
    full64 = pl.BlockSpec((None, s, dh), lambda bb, i: (bb, 0, 0))
    return pl.pallas_call(
        functools.partial(_dsa_kernel, ktop),
        grid=(b, s // tq),
        in_specs=[hmt_q, full64,
                  pl.BlockSpec((None, nh, tq), lambda bb, i: (bb, 0, i)),
                  hmt_q, full64,
                  pl.BlockSpec((None, s // tk, 2 * dh, tk), lambda bb, i: (bb, 0, 0, 0)),
                  pl.BlockSpec(bias.shape, lambda bb, i: (0, 0, 0, 0))],
        out_specs=pl.BlockSpec((None, tq, nh * dh), lambda bb, i: (bb, i, 0)),
        out_shape=jax.ShapeDtypeStruct((b, s, nh * dh), BF16),
        scratch_shapes=[pltpu.VMEM((s // tk, tk, tq), I32),
                        pltpu.VMEM((s // tk, tk, tq), I16),
                        pltpu.VMEM((s // tk, tk, tq), I16),
                        pltpu.VMEM((nh, 8, tq), F32),
                        pltpu.VMEM((nh, 2 * dh, tq), F32),
                        pltpu.VMEM((nh, s // tk, tk, tq), F32)],
        compiler_params=pltpu.CompilerParams(
            dimension_semantics=("parallel", "parallel"), vmem_limit_bytes=VMEM_LIMIT),
        name="sparse_attention",
    )(qit, ki, wit, qbt, kb, vt, bias)


def _first_max4(a):
    m1 = jnp.maximum(jnp.maximum(a[0], a[1]), jnp.maximum(a[2], a[3]))
    i1 = jnp.where(a[0] == m1, 0, jnp.where(a[1] == m1, 1, jnp.where(a[2] == m1, 2, 3)))
    rest = [jnp.where(i1 == j, -1.0, a[j]) for j in range(4)]
    m2 = jnp.maximum(jnp.maximum(rest[0], rest[1]), jnp.maximum(rest[2], rest[3]))
    i2 = jnp.where(rest[0] == m2, 0, jnp.where(rest[1] == m2, 1, jnp.where(rest[2] == m2, 2, 3)))
    return m1, i1, m2, i2


def _route(lt):
    lg = [lt[e:e + 1, :] for e in range(N_EXPERTS)]
    mx = functools.reduce(jnp.maximum, lg)
    ex = [jnp.exp(v - mx) for v in lg]
    tot = functools.reduce(lambda u, v: u + v, ex)
    p = [v / tot for v in ex]
    gs = []
    for g in range(N_GROUPS):
        m1, _, m2, _ = _first_max4(p[4 * g:4 * g + 4])
        gs.append(m1 + m2)
    _, gbest, _, _ = _first_max4(gs)
    chosen = [jnp.where(gbest == 0, p[j], jnp.where(gbest == 1, p[4 + j],
              jnp.where(gbest == 2, p[8 + j], p[12 + j]))) for j in range(4)]
    m1, i1, m2, i2 = _first_max4(chosen)
    den = m1 + m2
    lo = jnp.minimum(i1, i2)
    hi = jnp.maximum(i1, i2)
    pair = jnp.where(lo == 0, hi - 1, jnp.where(lo == 1, hi + 1, N_PAIRS - 1))
    first_is_lo = i1 < i2
    w1 = m1 / den
    w2 = m2 / den
    return gbest * N_PAIRS + pair, jnp.where(first_is_lo, w1, w2), jnp.where(first_is_lo, w2, w1)


def _merge_kernel(oa_ref, ob_ref, sga_ref, sgb_ref, x_ref, mod_ref, woa_ref, wob_ref, wout_ref,
                  g2_ref, rwh_ref, rwl_ref, rb_ref, x1_ref, h2x_ref, info_ref, cnt_ref, carry_scr):
    tm, d = x_ref.shape

    @pl.when((pl.program_id(0) == 0) & (pl.program_id(1) == 0))
    def _():
        carry_scr[...] = jnp.zeros(carry_scr.shape, F32)

    merged = (sga_ref[...].astype(F32) * _dot(oa_ref[...], woa_ref[...])
              + sgb_ref[...].astype(F32) * _dot(ob_ref[...], wob_ref[...]))
    x1 = x_ref[...] + mod_ref[2:3, :] * _dot(merged.astype(BF16), wout_ref[...])
    x1_ref[...] = x1
    h2 = (_rms(x1) * g2_ref[...]) * (1.0 + mod_ref[4:5, :]) + mod_ref[3:4, :]
    h2x_ref[:, :d] = h2
    hi = h2.astype(BF16)
    lo = (h2 - hi.astype(F32)).astype(BF16)
    logits = (_dot(hi, rwh_ref[...]) + _dot(lo, rwh_ref[...]) + _dot(hi, rwl_ref[...])) + rb_ref[...]
    cls, cw_lo, cw_hi = _route(logits.T)

    ncls = carry_scr.shape[0]
    onehot = (lax.broadcasted_iota(I32, (ncls, tm), 0) == cls).astype(F32)
    earlier = (lax.broadcasted_iota(I32, (tm, tm), 0)
               < lax.broadcasted_iota(I32, (tm, tm), 1)).astype(BF16)
    before = _dot(onehot.astype(BF16), earlier)
    carry = carry_scr[...]
    carry_t = jnp.concatenate([carry] * (tm // LANES), axis=1)
    rank = jnp.sum(onehot * (carry_t + before), axis=0, keepdims=True).astype(I32)
    carry = carry + jnp.sum(onehot, axis=1, keepdims=True)
    carry_scr[...] = carry
    cnt_ref[...] = carry.astype(I32)

    sub = lax.broadcasted_iota(I32, (8, tm), 0)
    info_ref[...] = jnp.where(sub == 0, cls, jnp.where(sub == 1, rank, 0))
    sub = lax.broadcasted_iota(I32, (LANES, tm), 0)
    weights_t = jnp.where(sub == 0, cw_lo, jnp.where(sub == 1, cw_hi, 0.0))
    h2x_ref[:, d:] = weights_t.T


def _merge(oa, ob, sga, sgb, x, mod_l, woa, wob, wout, g2, rwh, rwl, rb, tm):
    b, s, d = x.shape
    hw = oa.shape[-1]
    nt = s // tm
    tok = lambda n: pl.BlockSpec((None, tm, n), lambda bb, i: (bb, i, 0))
    const = lambda r, c: pl.BlockSpec((r, c), lambda bb, i: (0, 0))
    return pl.pallas_call(
        _merge_kernel,
        grid=(b, nt),
        in_specs=[tok(hw), tok(hw), tok(d), tok(d), tok(d),
                  pl.BlockSpec((None, N_MOD, d), lambda bb, i: (bb, 0, 0)),
                  const(hw, d), const(hw, d), const(d, d), const(1, d),
                  const(d, LANES), const(d, LANES), const(1, LANES)],
        out_specs=[tok(d), tok(d + LANES),
                   pl.BlockSpec((8, tm), lambda bb, i: (0, bb * nt + i)),
                   const(N_CLASS_PAD, LANES)],
        out_shape=[jax.ShapeDtypeStruct((b, s, d), F32),
                   jax.ShapeDtypeStruct((b, s, d + LANES), F32),
                   jax.ShapeDtypeStruct((8, b * s), I32),
                   jax.ShapeDtypeStruct((N_CLASS_PAD, LANES), I32)],
        scratch_shapes=[pltpu.VMEM((N_CLASS_PAD, LANES), F32)],
        compiler_params=pltpu.CompilerParams(
            dimension_semantics=("arbitrary", "arbitrary"), vmem_limit_bytes=VMEM_LIMIT),
        name="merge_route",
    )(oa, ob, sga, sgb, x, mod_l, woa, wob, wout, g2, rwh, rwl, rb)


def _positions_kernel(offs_ref, info_ref, pos_ref):
    cls = info_ref[0:1, :]
    pos = info_ref[1:2, :]
    for c in range(N_CLASSES):
        pos = pos + jnp.where(cls == c, offs_ref[c], 0)
    pos_ref[...] = jnp.broadcast_to(pos, pos_ref.shape)


def _positions(info, offs):
    return pl.pallas_call(
        _positions_kernel,
        in_specs=[pl.BlockSpec(memory_space=pltpu.SMEM), pl.BlockSpec(info.shape, lambda: (0, 0))],
        out_specs=pl.BlockSpec(info.shape, lambda: (0, 0)),
        out_shape=jax.ShapeDtypeStruct(info.shape, I32),
        name="positions",
    )(offs, info)[0]


def _dispatch_kernel(ends_ref, pos_ref, src_ref, dst_ref, zero_scr, sem, zero_sem):
    tm = pos_ref.shape[1]
    row_tile = zero_scr.shape[0]

    @pl.when(pl.program_id(0) == 0)
    def _():
        zero_scr[...] = jnp.zeros(zero_scr.shape, F32)

        def zero_tail(c):
            start = pl.multiple_of(ends_ref[c] - row_tile, row_tile)
            return pltpu.make_async_copy(zero_scr, dst_ref.at[pl.ds(start, row_tile), :], zero_sem)

        def nonempty(c):
            return ends_ref[c] > (ends_ref[c - 1] if c else 0)

        for c in range(N_CLASSES):
            @pl.when(nonempty(c))
            def _(c=c):
                zero_tail(c).start()
        for c in range(N_CLASSES):
            @pl.when(nonempty(c))
            def _(c=c):
                zero_tail(c).wait()

        def unused_tile(j):
            start = pl.multiple_of(ends_ref[N_CLASSES - 1] + j * row_tile, row_tile)
            return pltpu.make_async_copy(zero_scr, dst_ref.at[pl.ds(start, row_tile), :], zero_sem)

        n_unused = (dst_ref.shape[0] - ends_ref[N_CLASSES - 1]) // row_tile
        lax.fori_loop(0, n_unused, lambda j, carry: (unused_tile(j).start(), carry)[1], 0)
        lax.fori_loop(0, n_unused, lambda j, carry: (unused_tile(j).wait(), carry)[1], 0)

    for r in range(tm):
        pltpu.make_async_copy(src_ref.at[pl.ds(r, 1), :], dst_ref.at[pl.ds(pos_ref[0, r], 1), :],
                              sem).start(priority=r % 2)
    pltpu.make_async_copy(src_ref, dst_ref.at[pl.ds(0, tm), :], sem).wait()


def _dispatch(h2x, pos, ends, n_rows):
    t, w = h2x.shape
    tm = MOE_DMA_TILE
    smem_row = pl.BlockSpec((None, 1, tm), lambda i: (i, 0, 0), memory_space=pltpu.SMEM)
    return pl.pallas_call(
        _dispatch_kernel,
        grid=(t // tm,),
        in_specs=[pl.BlockSpec(memory_space=pltpu.SMEM), smem_row, pl.BlockSpec((tm, w), lambda i: (i, 0))],
        out_specs=pl.BlockSpec(memory_space=pl.ANY),
        out_shape=jax.ShapeDtypeStruct((n_rows, w), F32),
        scratch_shapes=[pltpu.VMEM((MOE_ROW_TILE, w), F32), pltpu.SemaphoreType.DMA, pltpu.SemaphoreType.DMA],
        compiler_params=pltpu.CompilerParams(
            dimension_semantics=("arbitrary",), vmem_limit_bytes=VMEM_LIMIT),
        name="dispatch",
    )(ends, pos, h2x)


def _experts_kernel(elo_ref, ehi_ref, live_ref, xs_ref, wg_lo, wg_hi, wu_lo, wu_hi, wd_lo, wd_hi, ys_ref):
    del elo_ref, ehi_ref
    d = ys_ref.shape[1]
    j = pl.program_id(0)

    @pl.when(live_ref[j] == 1)
    def _():
        h = xs_ref[:, :d].astype(BF16)

        def expert(wg, wu, wd, weight):
            gate = _dot(h, wg[...])
            hidden = ((gate * jax.nn.sigmoid(gate)) * _dot(h, wu[...])) * weight
            return _dot(hidden.astype(BF16), wd[...])

        ys_ref[...] = (expert(wg_lo, wu_lo, wd_lo, xs_ref[:, d:d + 1])
                       + expert(wg_hi, wu_hi, wd_hi, xs_ref[:, d + 1:d + 2]))

    @pl.when(live_ref[j] == 0)
    def _():
        ys_ref[...] = jnp.zeros(ys_ref.shape, F32)


def _experts(xs, e_lo, e_hi, live, wg, wu, wd):
    n_rows, w = xs.shape
    ne, d, f = wg.shape
    tm = MOE_ROW_TILE
    lo = lambda j, elo, ehi, lv: (elo[j], 0, 0)
    hi = lambda j, elo, ehi, lv: (ehi[j], 0, 0)
    grid_spec = pltpu.PrefetchScalarGridSpec(
        num_scalar_prefetch=3,
        grid=(n_rows // tm,),
        in_specs=[pl.BlockSpec((tm, w), lambda j, elo, ehi, lv: (j, 0)),
                  pl.BlockSpec((None, d, f), lo), pl.BlockSpec((None, d, f), hi),
                  pl.BlockSpec((None, d, f), lo), pl.BlockSpec((None, d, f), hi),
                  pl.BlockSpec((None, f, d), lo), pl.BlockSpec((None, f, d), hi)],
        out_specs=pl.BlockSpec((tm, d), lambda j, elo, ehi, lv: (j, 0)),
    )
    return pl.pallas_call(
        _experts_kernel,
        grid_spec=grid_spec,
        out_shape=jax.ShapeDtypeStruct((n_rows, d), F32),
        compiler_params=pltpu.CompilerParams(
            dimension_semantics=("arbitrary",), vmem_limit_bytes=VMEM_LIMIT),
        name="experts",
    )(e_lo, e_hi, live, xs, wg, wg, wu, wu, wd, wd)


def _combine_kernel(pos_ref, next_ref, ys_ref, x1_ref, mod_ref, o_ref, buf, sem):
    tm = buf.shape[1]
    step = pl.program_id(0) * pl.num_programs(1) + pl.program_id(1)
    last = pl.num_programs(0) * pl.num_programs(1) - 1

    def issue(p_ref, slot):
        for r in range(tm):
            pltpu.make_async_copy(ys_ref.at[pl.ds(p_ref[slot, r], 1), :], buf.at[slot, pl.ds(r, 1), :],
                                  sem.at[slot]).start(priority=r % 2)

    def finish(slot):
        pltpu.make_async_copy(ys_ref.at[pl.ds(0, tm), :], buf.at[slot], sem.at[slot]).wait()
        rows = pl.ds(slot * tm, tm)
        o_ref[rows, :] = x1_ref[rows, :] + mod_ref[5:6, :] * buf[slot]

    @pl.when(step == 0)
    def _():
        issue(pos_ref, 0)

    issue(pos_ref, 1)
    finish(0)

    @pl.when(step < last)
    def _():
        issue(next_ref, 0)

    finish(1)


def _combine(ys, pos, x1, mod_l):
    b, s, d = x1.shape
    tm = MOE_DMA_TILE
    nt = s // (2 * tm)
    n_steps = b * nt
    pos2 = pos.reshape(n_steps, 2, tm)
    this_step = pl.BlockSpec((None, 2, tm), lambda bb, i: (bb * nt + i, 0, 0), memory_space=pltpu.SMEM)
    next_step = pl.BlockSpec((None, 2, tm), lambda bb, i: (jnp.minimum(bb * nt + i + 1, n_steps - 1), 0, 0),
                             memory_space=pltpu.SMEM)
    return pl.pallas_call(
        _combine_kernel,
        grid=(b, nt),
        in_specs=[this_step, next_step,
                  pl.BlockSpec(memory_space=pl.ANY),
                  pl.BlockSpec((None, 2 * tm, d), lambda bb, i: (bb, i, 0)),
                  pl.BlockSpec((None, N_MOD, d), lambda bb, i: (bb, 0, 0))],
        out_specs=pl.BlockSpec((None, 2 * tm, d), lambda bb, i: (bb, i, 0)),
        out_shape=jax.ShapeDtypeStruct((b, s, d), F32),
        scratch_shapes=[pltpu.VMEM((2, tm, d), F32), pltpu.SemaphoreType.DMA((2,))],
        compiler_params=pltpu.CompilerParams(
            dimension_semantics=("arbitrary", "arbitrary"), vmem_limit_bytes=VMEM_LIMIT),
        name="combine",
    )(pos2, pos2, ys, x1, mod_l)


def _moe(h2x, info, counts, wg, wu, wd, x1, mod_l):
    b, s, d = x1.shape
    t = b * s
    tm = MOE_ROW_TILE
    n_tiles = t // tm + N_CLASSES
    counts = counts[:N_CLASSES, 0]
    padded = (counts + tm - 1) // tm * tm
    ends = jnp.cumsum(padded)
    offs = jnp.pad(ends - padded, (0, N_CLASS_PAD - N_CLASSES)).astype(I32)
    tile_cls = jnp.sum(jnp.arange(n_tiles, dtype=I32)[:, None] * tm >= ends[None, :], axis=1)
    live = (tile_cls < N_CLASSES).astype(I32)
    tile_cls = jnp.minimum(tile_cls, N_CLASSES - 1)
    group, pair = tile_cls // N_PAIRS, tile_cls % N_PAIRS
    pairs = jnp.asarray(_PAIRS, I32)
    e_lo = group * EXPERTS_PER_GROUP + pairs[pair, 0]
    e_hi = group * EXPERTS_PER_GROUP + pairs[pair, 1]

    pos = _positions(info, offs).reshape(t // MOE_DMA_TILE, 1, MOE_DMA_TILE)
    xs = _dispatch(h2x.reshape(t, d + LANES), pos, ends.astype(I32), n_tiles * tm)
    ys = _experts(xs, e_lo, e_hi, live, wg, wu, wd)
    return _combine(ys, pos, x1, mod_l)


def kernel(x, c, w_ada, b_ada, norm1_g, w_in, q_norm_g, k_norm_g, rel_bias, w_o_a, w_o_b, w_out,
           norm2_g, router_w, router_b, w_gate, w_up, w_down):
    b, s, d = x.shape
    depth = w_ada.shape[0]
    assert d % LANES == 0
    tm = min(512, s)

    mod = _adaln(c, w_ada, b_ada)
    bias = _bias_tiles(rel_bias, DSA_QUERY_BLOCK, DSA_KEY_BLOCK)
    rw = jnp.pad(router_w, ((0, 0), (0, LANES - N_EXPERTS)))
    rwh = rw.astype(BF16)
    rwl = (rw - rwh.astype(F32)).astype(BF16)
    rb = jnp.pad(router_b, (0, LANES - N_EXPERTS)).reshape(1, LANES)

    for l in range(depth):
        qa, ka, va, qbt, qit, kb, vt, ki, wit, sga, sgb = _in_proj(
            x, mod[l], norm1_g[l].reshape(1, d), _pack_w_in(w_in[l]),
            q_norm_g[l].reshape(HEAD_DIM, 1), k_norm_g[l].reshape(1, HEAD_DIM), tm)
        oa = _sb_attention(qa, ka, va)
        ob = _dsa_attention(qit, ki, wit, qbt, kb, vt, bias)
        x1, h2x, info, counts = _merge(oa, ob, sga, sgb, x, mod[l], w_o_a[l].astype(BF16),
                                       w_o_b[l].astype(BF16), w_out[l].astype(BF16),
                                       norm2_g[l].reshape(1, d), rwh, rwl, rb, tm)
        x = _moe(h2x, info, counts, w_gate[l].astype(BF16), w_up[l].astype(BF16),
                 w_down[l].astype(BF16), x1, mod[l])
    return x
```

```python
import functools
import math

import numpy as np
import jax
import jax.numpy as jnp
from jax import lax
from jax.experimental import pallas as pl
from jax.experimental.pallas import tpu as pltpu

F32 = jnp.float32
BF16 = jnp.bfloat16
I32 = jnp.int32
I16 = jnp.int16

HEAD_DIM = 64
N_HEADS = 8
N_EXPERTS = 16
N_GROUPS = 4
EXPERTS_PER_GROUP = 4
_PAIRS = ((0, 1), (0, 2), (0, 3), (1, 2), (1, 3), (2, 3))
N_PAIRS = len(_PAIRS)
N_CLASSES = N_GROUPS * N_PAIRS
N_CLASS_PAD = 32
MOE_ROW_TILE = 256
MOE_DMA_TILE = 512
N_MOD = 6
EPS = 1e-6
INDEX_TOPK_MAX = 256
REL_BUCKETS = 32
REL_MAX_DIST = 128
LANES = 128
DSA_QUERY_BLOCK = 256
DSA_KEY_BLOCK = 128
SB_QUERY_BLOCK = 256
SB_KEY_BLOCK = 256
SB_DEAD_LOG2 = 150.0
LOG2E = 1.4426950408889634
NEG_BIG = -1e30
INT_MIN = -2 ** 31
HALF_BIAS = 2 ** 15

VMEM_LIMIT = 56 * 1024 * 1024


def _nt_dot(a, b):
    return lax.dot_general(a, b, (((1,), (1,)), ((), ())), preferred_element_type=F32)


def _dot(a, b):
    return jnp.dot(a, b, preferred_element_type=F32)


def _mod_kernel(c_ref, w_ref, b_ref, o_ref):
    c = c_ref[...]
    sc = c * jax.nn.sigmoid(c)
    o_ref[...] = _dot(sc.astype(BF16), w_ref[...].astype(BF16)) + b_ref[...]


def _adaln(c, w_ada, b_ada):
    depth, d, nd = w_ada.shape
    b = c.shape[0]
    out = pl.pallas_call(
        _mod_kernel,
        grid=(depth, nd // d),
        in_specs=[pl.BlockSpec((b, d), lambda l, j: (0, 0)),
                  pl.BlockSpec((None, d, d), lambda l, j: (l, 0, j)),
                  pl.BlockSpec((None, 1, d), lambda l, j: (l, 0, j))],
        out_specs=pl.BlockSpec((None, b, d), lambda l, j: (l, 0, j)),
        out_shape=jax.ShapeDtypeStruct((depth, b, nd), F32),
        compiler_params=pltpu.CompilerParams(vmem_limit_bytes=VMEM_LIMIT),
        name="adaln",
    )(c, w_ada, b_ada.reshape(depth, 1, nd))
    return out.reshape(depth, b, N_MOD, d)


def _rms(x):
    return x * lax.rsqrt(jnp.mean(x * x, axis=-1, keepdims=True) + EPS)


def _in_kernel(x_ref, mod_ref, g1_ref, w_ref, qng_ref, kng_ref,
               qa_ref, ka_ref, va_ref, qbt_ref, qit_ref, kb_ref, vt_ref, ki_ref, wit_ref,
               sga_ref, sgb_ref):
    d = x_ref.shape[-1]
    tm = x_ref.shape[0]
    hw = N_HEADS * HEAD_DIM
    x = x_ref[...]
    h = (_rms(x) * g1_ref[...]) * (1.0 + mod_ref[1:2, :]) + mod_ref[0:1, :]
    hb = h.astype(BF16)

    def proj(lo, n):
        return _dot(hb, w_ref[:, lo:lo + n])

    def heads(r, ref):
        for hh in range(N_HEADS):
            ref[hh] = r[:, hh * HEAD_DIM:(hh + 1) * HEAD_DIM].astype(ref.dtype)

    scale = HEAD_DIM ** -0.5
    heads(proj(0, hw) * (scale * LOG2E), qa_ref)
    heads(proj(hw, hw), ka_ref)
    heads(proj(2 * hw, hw), va_ref)

    rt = proj(3 * hw, hw).T
    for hh in range(N_HEADS):
        slab = rt[hh * HEAD_DIM:(hh + 1) * HEAD_DIM, :]
        inv = lax.rsqrt(jnp.mean(slab * slab, axis=0, keepdims=True) + EPS)
        qbt_ref[hh] = (((slab * inv) * qng_ref[...]) * scale).astype(BF16)

    r = proj(4 * hw, 2 * HEAD_DIM)
    kb_ref[...] = (_rms(r[:, :HEAD_DIM]) * kng_ref[...]).astype(BF16)
    vt = r.T[HEAD_DIM:, :]
    ones_row = (lax.broadcasted_iota(I32, (HEAD_DIM, LANES), 0) == 0).astype(BF16)
    for cc in range(tm // LANES):
        vt_ref[cc, :HEAD_DIM, :] = vt[:, cc * LANES:(cc + 1) * LANES].astype(BF16)
        vt_ref[cc, HEAD_DIM:, :] = ones_row

    rt = proj(4 * hw + LANES, hw).T
    for hh in range(N_HEADS):
        qit_ref[hh] = rt[hh * HEAD_DIM:(hh + 1) * HEAD_DIM, :].astype(BF16)

    r = proj(5 * hw + LANES, LANES)
    ki_ref[...] = r[:, :HEAD_DIM].astype(BF16)
    wit_ref[...] = r.T[HEAD_DIM:HEAD_DIM + N_HEADS, :]

    off = 5 * hw + 2 * LANES
    sga_ref[...] = jax.nn.sigmoid(proj(off, d)).astype(BF16)
    sgb_ref[...] = jax.nn.sigmoid(proj(off + d, d)).astype(BF16)


def _pack_w_in(w_in_l):
    hw = N_HEADS * HEAD_DIM
    a = 4 * hw + 2 * HEAD_DIM + hw
    small = w_in_l[:, a:a + HEAD_DIM + N_HEADS]
    small = jnp.pad(small, ((0, 0), (0, LANES - small.shape[1])))
    gates = w_in_l[:, a + HEAD_DIM + N_HEADS:]
    return jnp.concatenate([w_in_l[:, :a], small, gates], axis=1).astype(BF16)


def _in_proj(x, mod_l, g1, w_packed, qng, kng, tm):
    b, s, d = x.shape
    hm = jax.ShapeDtypeStruct((b, N_HEADS, s, HEAD_DIM), BF16)
    hmt = jax.ShapeDtypeStruct((b, N_HEADS, HEAD_DIM, s), BF16)
    tok64 = jax.ShapeDtypeStruct((b, s, HEAD_DIM), BF16)
    hm_spec = pl.BlockSpec((None, N_HEADS, tm, HEAD_DIM), lambda bb, i: (bb, 0, i, 0))
    hmt_spec = pl.BlockSpec((None, N_HEADS, HEAD_DIM, tm), lambda bb, i: (bb, 0, 0, i))
    tok = lambda n: pl.BlockSpec((None, tm, n), lambda bb, i: (bb, i, 0))
    nw = w_packed.shape[1]
    return pl.pallas_call(
        _in_kernel,
        grid=(b, s // tm),
        in_specs=[tok(d),
                  pl.BlockSpec((None, N_MOD, d), lambda bb, i: (bb, 0, 0)),
                  pl.BlockSpec((1, d), lambda bb, i: (0, 0)),
                  pl.BlockSpec((d, nw), lambda bb, i: (0, 0)),
                  pl.BlockSpec((HEAD_DIM, 1), lambda bb, i: (0, 0)),
                  pl.BlockSpec((1, HEAD_DIM), lambda bb, i: (0, 0))],
        out_specs=[hm_spec] * 3 + [hmt_spec] * 2
                  + [tok(HEAD_DIM),
                     pl.BlockSpec((None, tm // LANES, 2 * HEAD_DIM, LANES), lambda bb, i: (bb, i, 0, 0)),
                     tok(HEAD_DIM),
                     pl.BlockSpec((None, N_HEADS, tm), lambda bb, i: (bb, 0, i)),
                     tok(d), tok(d)],
        out_shape=[hm] * 3 + [hmt] * 2
                  + [tok64,
                     jax.ShapeDtypeStruct((b, s // LANES, 2 * HEAD_DIM, LANES), BF16),
                     tok64,
                     jax.ShapeDtypeStruct((b, N_HEADS, s), F32),
                     jax.ShapeDtypeStruct((b, s, d), BF16),
                     jax.ShapeDtypeStruct((b, s, d), BF16)],
        compiler_params=pltpu.CompilerParams(
            dimension_semantics=("parallel", "parallel"), vmem_limit_bytes=VMEM_LIMIT),
        name="in_proj",
    )(x, mod_l, g1, w_packed, qng, kng)


def _sb_kernel(q_ref, k_ref, v_ref, o_ref, acc_scr, csum_scr, sp_scr, ls_scr):
    tq = q_ref.shape[1]
    tk = SB_KEY_BLOCK
    i = pl.program_id(1)
    nfull = (i * tq) // tk
    row = lax.broadcasted_iota(I32, (tq, tk), 0)
    col = lax.broadcasted_iota(I32, (tq, tk), 1)
    before = (col - row) < (i * tq - nfull * tk)
    krow = lax.broadcasted_iota(I32, (tk, tk), 0)
    kcol = lax.broadcasted_iota(I32, (tk, tk), 1)
    suffix = (krow > kcol).astype(BF16)

    def stage_a(hh, st, first):
        z2 = _nt_dot(q_ref[hh], k_ref[hh, pl.ds(st, tk), :])
        sp2 = jnp.maximum(jnp.log2(1.0 + jnp.exp2(jnp.minimum(z2, 126.0))), z2)
        if first:
            sp2 = jnp.where(before, sp2, 0.0)
        sp_scr[hh] = sp2.astype(BF16)
        ls_scr[hh] = z2 - sp2

    def stage_b(hh, st, first):
        sp = sp_scr[hh]
        suf = _dot(sp, suffix)
        if first:
            a = jnp.where(before, jnp.exp2(ls_scr[hh] - suf), 0.0)
            csum = jnp.zeros((tq, 1), F32)
        else:
            csum = csum_scr[hh]
            a = jnp.exp2((ls_scr[hh] - suf) - csum)
        c = _dot(a.astype(BF16), v_ref[hh, pl.ds(st, tk), :])
        if first:
            acc_scr[hh] = c
        else:
            acc_scr[hh] += c
        csum_scr[hh] = csum + (suf[:, 0:1] + sp[:, 0:1].astype(F32))

    def key_block(st, first):
        for hh in range(N_HEADS):
            stage_a(hh, st, first)
        for hh in range(N_HEADS):
            stage_b(hh, st, first)

    key_block(pl.multiple_of(nfull * tk, tk), True)

    def live():
        return jnp.min(csum_scr[...]) < SB_DEAD_LOG2

    def body(carry):
        jj, _ = carry
        key_block(pl.multiple_of((nfull - 1 - jj) * tk, tk), False)
        return jj + 1, live()

    lax.while_loop(lambda carry: (carry[0] < nfull) & carry[1], body, (jnp.int32(0), live()))
    for hh in range(N_HEADS):
        o_ref[:, hh * HEAD_DIM:(hh + 1) * HEAD_DIM] = acc_scr[hh].astype(o_ref.dtype)


def _sb_attention(qa, ka, va):
    b, nh, s, dh = qa.shape
    tq = SB_QUERY_BLOCK
    assert SB_KEY_BLOCK % tq == 0 and s % SB_KEY_BLOCK == 0
    return pl.pallas_call(
        _sb_kernel,
        grid=(b, s // tq),
        in_specs=[pl.BlockSpec((None, nh, tq, dh), lambda bb, i: (bb, 0, i, 0)),
                  pl.BlockSpec((None, nh, s, dh), lambda bb, i: (bb, 0, 0, 0)),
                  pl.BlockSpec((None, nh, s, dh), lambda bb, i: (bb, 0, 0, 0))],
        out_specs=pl.BlockSpec((None, tq, nh * dh), lambda bb, i: (bb, i, 0)),
        out_shape=jax.ShapeDtypeStruct((b, s, nh * dh), BF16),
        scratch_shapes=[pltpu.VMEM((nh, tq, dh), F32), pltpu.VMEM((nh, tq, 1), F32),
                        pltpu.VMEM((nh, tq, SB_KEY_BLOCK), BF16), pltpu.VMEM((nh, tq, SB_KEY_BLOCK), F32)],
        compiler_params=pltpu.CompilerParams(
            dimension_semantics=("parallel", "parallel"), vmem_limit_bytes=VMEM_LIMIT),
        name="stick_breaking",
    )(qa, ka, va)


def _t5_bucket_np(dist):
    max_exact = REL_BUCKETS // 2
    d_f = np.maximum(dist, 1).astype(np.float32)
    large = max_exact + (np.log(d_f / np.float32(max_exact))
                         / np.float32(math.log(REL_MAX_DIST / max_exact))
                         * np.float32(REL_BUCKETS - max_exact)).astype(np.int32)
    large = np.minimum(large, REL_BUCKETS - 1)
    return np.where(dist < max_exact, dist, large).astype(np.int32)


def _bias_tiles(rel_bias, tq, tk):
    n_near = tq // tk + 1
    key = np.arange(tk)[:, None]
    query = np.arange(tq)[None, :]
    bk = np.stack([_t5_bucket_np(np.maximum(query - key - (n - 1) * tk, 0)) for n in range(n_near)])
    far = int(_t5_bucket_np(np.array([REL_MAX_DIST]))[0])
    nh = rel_bias.shape[1]

    def body(rb_ref, bk_ref, o_ref):
        hh = pl.program_id(1)
        buckets = bk_ref[...]
        tile = jnp.zeros((tk, tq), F32)
        for bucket in range(REL_BUCKETS):
            tile = jnp.where(buckets == bucket, rb_ref[hh, bucket] - rb_ref[hh, far], tile)
        o_ref[...] = tile

    return pl.pallas_call(
        body,
        grid=(n_near, nh),
        in_specs=[pl.BlockSpec(memory_space=pltpu.SMEM),
                  pl.BlockSpec((None, tk, tq), lambda n, hh: (n, 0, 0))],
        out_specs=pl.BlockSpec((None, None, tk, tq), lambda n, hh: (n, hh, 0, 0)),
        out_shape=jax.ShapeDtypeStruct((n_near, nh, tk, tq), F32),
        name="bias_tiles",
    )(rel_bias.T, jnp.asarray(bk))


def _sort_key(x):
    bits = pltpu.bitcast(x, I32)
    return bits ^ ((bits >> 31) & 0x7FFFFFFF)


def _fold_rows(x, op):
    return functools.reduce(op, [x[g * 8:(g + 1) * 8, :] for g in range(x.shape[0] // 8)])


def _dsa_kernel(ktop, qit_ref, ki_ref, wit_ref, qbt_ref, kb_ref, vt_ref, bias_ref, o_ref,
                key_scr, hi_scr, lo_scr, m_scr, acc_scr, s_scr):
    tq = qit_ref.shape[2]
    tk = DSA_KEY_BLOCK
    per_q = tq // tk
    i = pl.program_id(1)
    first_own = per_q * i
    nchunks = per_q * (i + 1)
    key_i = lax.broadcasted_iota(I32, (tk, tq), 0)
    qry_i = lax.broadcasted_iota(I32, (tk, tq), 1)

    def causal(rel):
        return (key_i - qry_i) <= (-rel * tk)

    def for_chunks(fn):
        def far(j, carry):
            for u in range(per_q):
                fn(j * per_q + u, None)
            return carry
        lax.fori_loop(0, i - 1, far, 0)

        @pl.when(i >= 1)
        def _():
            for rel in range(-per_q, -1):
                fn(first_own + rel, None)
            fn(first_own - 1, -1)

        for rel in range(per_q):
            fn(first_own + rel, rel)

    def score_chunk(c, rel):
        kic = ki_ref[pl.ds(pl.multiple_of(c * tk, tk), tk), :]
        score = jnp.zeros((tk, tq), F32)
        for hh in range(N_HEADS):
            score = score + wit_ref[hh:hh + 1, :] * jnp.maximum(_dot(kic, qit_ref[hh]), 0.0)
        keys = _sort_key(score)
        if rel is not None and rel >= 0:
            keys = jnp.where(causal(rel), keys, INT_MIN)
        key_scr[c] = keys
        hi_scr[c] = (keys >> 16).astype(I16)
        lo_scr[c] = ((keys & 0xFFFF) - HALF_BIAS).astype(I16)

    for_chunks(score_chunk)

    def count(pred):
        def body(j, part):
            for u in range(per_q):
                part = part + _fold_rows(jnp.where(pred(key_scr[j * per_q + u]), 1, 0), jnp.add)
            return part
        part = lax.fori_loop(0, i + 1, body, jnp.zeros((8, tq), I32))
        return jnp.sum(part, axis=0, keepdims=True)

    def count16(scr, pred):
        one, zero = jnp.int16(1), jnp.int16(0)

        def block(j, part):
            for u in range(per_q):
                hit = jnp.where(pred(scr[j * per_q + u]), one, zero)
                part = part + functools.reduce(
                    jnp.add, [hit[g * 16:(g + 1) * 16, :] for g in range(tk // 16)])
            return part

        nblk = i + 1
        part = lax.fori_loop(0, nblk // 2, lambda jj, part: block(2 * jj + 1, block(2 * jj, part)),
                             jnp.zeros((16, tq), I16))
        part = lax.cond(nblk % 2 == 1, lambda p: block(nblk - 1, p), lambda p: p, part)
        return jnp.sum(part.astype(I32), axis=0, keepdims=True)

    def search16(scr, base):
        def bit_body(bi, t_u):
            cand_u = t_u | jnp.left_shift(jnp.int32(1), 15 - bi)
            cand = (cand_u - HALF_BIAS).astype(I16)
            return jnp.where(base + count16(scr, lambda k: k >= cand) >= ktop, cand_u, t_u)
        return lax.fori_loop(0, 16, bit_body, jnp.zeros((1, tq), I32))

    t_hi = search16(hi_scr, 0) - HALF_BIAS
    t_hi16 = t_hi.astype(I16)
    above = count16(hi_scr, lambda k: k > t_hi16)

    def keep_band(j, carry):
        for u in range(per_q):
            c = j * per_q + u
            lo_scr[c] = jnp.where(hi_scr[c] == t_hi16, lo_scr[c], jnp.int16(-HALF_BIAS))
        return carry

    lax.fori_loop(0, i + 1, keep_band, 0)
    thr = t_hi * (2 * HALF_BIAS) + search16(lo_scr, above)
    n_ge = count(lambda k: k >= thr)

    @pl.when(jnp.max(n_ge) > ktop)
    def _():
        need = (ktop - count(lambda k: k > thr)).astype(F32)
        lower = (lax.broadcasted_iota(I32, (tk, tk), 1)
                 < lax.broadcasted_iota(I32, (tk, tk), 0)).astype(BF16)

        def body(c, seen):
            k = key_scr[c]
            eq = k == thr
            eqf = eq.astype(BF16)
            rank = seen + _dot(lower, eqf)
            drop = eq & (rank >= need) & (n_ge > ktop)
            key_scr[c] = jnp.where(drop, thr - 1, k)
            return seen + jnp.sum(eqf.astype(F32), axis=0, keepdims=True)

        lax.fori_loop(0, nchunks, body, jnp.zeros((1, tq), F32))

    def logits(c, rel):
        sel = key_scr[c] >= thr
        if rel is not None and rel >= 0:
            sel = sel & causal(rel)
        kbc = kb_ref[pl.ds(pl.multiple_of(c * tk, tk), tk), :]

        def head(hh):
            s = _dot(kbc, qbt_ref[hh])
            if rel is not None:
                s = s + bias_ref[rel + 1, hh]
            return s
        return sel, head

    m_scr[...] = jnp.full(m_scr.shape, NEG_BIG, F32)

    def max_chunk(c, rel):
        sel, head = logits(c, rel)
        for hh in range(N_HEADS):
            s = jnp.where(sel, head(hh), NEG_BIG)
            s_scr[hh, c] = s
            m_scr[hh] = jnp.maximum(m_scr[hh], _fold_rows(s, jnp.maximum))

    for_chunks(max_chunk)
    m = [jnp.max(m_scr[hh], axis=0, keepdims=True) for hh in range(N_HEADS)]

    acc_scr[...] = jnp.zeros(acc_scr.shape, F32)

    def att_block(j, carry):
        for u in range(per_q):
            c = j * per_q + u
            vtc = vt_ref[c]
            for hh in range(N_HEADS):
                p = jnp.exp(s_scr[hh, c] - m[hh])
                acc_scr[hh] += _dot(vtc, p.astype(BF16))
        return carry

    lax.fori_loop(0, i + 1, att_block, 0)

    out_t = jnp.concatenate(
        [acc_scr[hh, :HEAD_DIM, :] / acc_scr[hh, HEAD_DIM:HEAD_DIM + 1, :] for hh in range(N_HEADS)],
        axis=0)
    o_ref[...] = out_t.T.astype(o_ref.dtype)


def _dsa_attention(qit, ki, wit, qbt, kb, vt, bias):
    b, nh, dh, s = qit.shape
    tq = DSA_QUERY_BLOCK
    tk = DSA_KEY_BLOCK
    assert tk == REL_MAX_DIST and tk == LANES and tq % tk == 0 and s % tq == 0
    ktop = min(INDEX_TOPK_MAX, s // 4)
    hmt_q = pl.BlockSpec((None, nh, dh, tq), lambda bb, i: (bb, 0, 0, i))
    full64 = pl.BlockSpec((None, s, dh), lambda bb, i: (bb, 0, 0))
    return pl.pallas_call(
        functools.partial(_dsa_kernel, ktop),
        grid=(b, s // tq),
        in_specs=[hmt_q, full64,
                  pl.BlockSpec((None, nh, tq), lambda bb, i: (bb, 0, i)),
                  hmt_q, full64,
                  pl.BlockSpec((None, s // tk, 2 * dh, tk), lambda bb, i: (bb, 0, 0, 0)),
                  pl.BlockSpec(bias.shape, lambda bb, i: (0, 0, 0, 0))],
        out_specs=pl.BlockSpec((None, tq, nh * dh), lambda bb, i: (bb, i, 0)),
        out_shape=jax.ShapeDtypeStruct((b, s, nh * dh), BF16),
        scratch_shapes=[pltpu.VMEM((s // tk, tk, tq), I32),
                        pltpu.VMEM((s // tk, tk, tq), I16),
                        pltpu.VMEM((s // tk, tk, tq), I16),
                        pltpu.VMEM((nh, 8, tq), F32),
                        pltpu.VMEM((nh, 2 * dh, tq), F32),
                        pltpu.VMEM((nh, s // tk, tk, tq), F32)],
        compiler_params=pltpu.CompilerParams(
            dimension_semantics=("parallel", "parallel"), vmem_limit_bytes=VMEM_LIMIT),
        name="sparse_attention",
    )(qit, ki, wit, qbt, kb, vt, bias)


def _first_max4(a):
    m1 = jnp.maximum(jnp.maximum(a[0], a[1]), jnp.maximum(a[2], a[3]))
    i1 = jnp.where(a[0] == m1, 0, jnp.where(a[1] == m1, 1, jnp.where(a[2] == m1, 2, 3)))
    rest = [jnp.where(i1 == j, -1.0, a[j]) for j in range(4)]
    m2 = jnp.maximum(jnp.maximum(rest[0], rest[1]), jnp.maximum(rest[2], rest[3]))
    i2 = jnp.where(rest[0] == m2, 0, jnp.where(rest[1] == m2, 1, jnp.where(rest[2] == m2, 2, 3)))
    return m1, i1, m2, i2


def _route(lt):
    lg = [lt[e:e + 1, :] for e in range(N_EXPERTS)]
    mx = functools.reduce(jnp.maximum, lg)
    ex = [jnp.exp(v - mx) for v in lg]
    tot = functools.reduce(lambda u, v: u + v, ex)
    p = [v / tot for v in ex]
    gs = []
    for g in range(N_GROUPS):
        m1, _, m2, _ = _first_max4(p[4 * g:4 * g + 4])
        gs.append(m1 + m2)
    _, gbest, _, _ = _first_max4(gs)
    chosen = [jnp.where(gbest == 0, p[j], jnp.where(gbest == 1, p[4 + j],
              jnp.where(gbest == 2, p[8 + j], p[12 + j]))) for j in range(4)]
    m1, i1, m2, i2 = _first_max4(chosen)
    den = m1 + m2
    lo = jnp.minimum(i1, i2)
    hi = jnp.maximum(i1, i2)
    pair = jnp.where(lo == 0, hi - 1, jnp.where(lo == 1, hi + 1, N_PAIRS - 1))
    first_is_lo = i1 < i2
    w1 = m1 / den
    w2 = m2 / den
    return gbest * N_PAIRS + pair, jnp.where(first_is_lo, w1, w2), jnp.where(first_is_lo, w2, w1)


def _merge_kernel(oa_ref, ob_ref, sga_ref, sgb_ref, x_ref, mod_ref, woa_ref, wob_ref, wout_ref,
                  g2_ref, rwh_ref, rwl_ref, rb_ref, x1_ref, h2x_ref, info_ref, cnt_ref, carry_scr):
    tm, d = x_ref.shape

    @pl.when((pl.program_id(0) == 0) & (pl.program_id(1) == 0))
    def _():
        carry_scr[...] = jnp.zeros(carry_scr.shape, F32)

    merged = (sga_ref[...].astype(F32) * _dot(oa_ref[...], woa_ref[...])
              + sgb_ref[...].astype(F32) * _dot(ob_ref[...], wob_ref[...]))
    x1 = x_ref[...] + mod_ref[2:3, :] * _dot(merged.astype(BF16), wout_ref[...])
    x1_ref[...] = x1
    h2 = (_rms(x1) * g2_ref[...]) * (1.0 + mod_ref[4:5, :]) + mod_ref[3:4, :]
    h2x_ref[:, :d] = h2
    hi = h2.astype(BF16)
    lo = (h2 - hi.astype(F32)).astype(BF16)
    logits = (_dot(hi, rwh_ref[...]) + _dot(lo, rwh_ref[...]) + _dot(hi, rwl_ref[...])) + rb_ref[...]
    cls, cw_lo, cw_hi = _route(logits.T)

    ncls = carry_scr.shape[0]
    onehot = (lax.broadcasted_iota(I32, (ncls, tm), 0) == cls).astype(F32)
    earlier = (lax.broadcasted_iota(I32, (tm, tm), 0)
               < lax.broadcasted_iota(I32, (tm, tm), 1)).astype(BF16)
    before = _dot(onehot.astype(BF16), earlier)
    carry = carry_scr[...]
    carry_t = jnp.concatenate([carry] * (tm // LANES), axis=1)
    rank = jnp.sum(onehot * (carry_t + before), axis=0, keepdims=True).astype(I32)
    carry = carry + jnp.sum(onehot, axis=1, keepdims=True)
    carry_scr[...] = carry
    cnt_ref[...] = carry.astype(I32)

    sub = lax.broadcasted_iota(I32, (8, tm), 0)
    info_ref[...] = jnp.where(sub == 0, cls, jnp.where(sub == 1, rank, 0))
    sub = lax.broadcasted_iota(I32, (LANES, tm), 0)
    weights_t = jnp.where(sub == 0, cw_lo, jnp.where(sub == 1, cw_hi, 0.0))
    h2x_ref[:, d:] = weights_t.T


def _merge(oa, ob, sga, sgb, x, mod_l, woa, wob, wout, g2, rwh, rwl, rb, tm):
    b, s, d = x.shape
    hw = oa.shape[-1]
    nt = s // tm
    tok = lambda n: pl.BlockSpec((None, tm, n), lambda bb, i: (bb, i, 0))
    const = lambda r, c: pl.BlockSpec((r, c), lambda bb, i: (0, 0))
    return pl.pallas_call(
        _merge_kernel,
        grid=(b, nt),
        in_specs=[tok(hw), tok(hw), tok(d), tok(d), tok(d),
                  pl.BlockSpec((None, N_MOD, d), lambda bb, i: (bb, 0, 0)),
                  const(hw, d), const(hw, d), const(d, d), const(1, d),
                  const(d, LANES), const(d, LANES), const(1, LANES)],
        out_specs=[tok(d), tok(d + LANES),
                   pl.BlockSpec((8, tm), lambda bb, i: (0, bb * nt + i)),
                   const(N_CLASS_PAD, LANES)],
        out_shape=[jax.ShapeDtypeStruct((b, s, d), F32),
                   jax.ShapeDtypeStruct((b, s, d + LANES), F32),
                   jax.ShapeDtypeStruct((8, b * s), I32),
                   jax.ShapeDtypeStruct((N_CLASS_PAD, LANES), I32)],
        scratch_shapes=[pltpu.VMEM((N_CLASS_PAD, LANES), F32)],
        compiler_params=pltpu.CompilerParams(
            dimension_semantics=("arbitrary", "arbitrary"), vmem_limit_bytes=VMEM_LIMIT),
        name="merge_route",
    )(oa, ob, sga, sgb, x, mod_l, woa, wob, wout, g2, rwh, rwl, rb)


def _positions_kernel(offs_ref, info_ref, pos_ref):
    cls = info_ref[0:1, :]
    pos = info_ref[1:2, :]
    for c in range(N_CLASSES):
        pos = pos + jnp.where(cls == c, offs_ref[c], 0)
    pos_ref[...] = jnp.broadcast_to(pos, pos_ref.shape)


def _positions(info, offs):
    return pl.pallas_call(
        _positions_kernel,
        in_specs=[pl.BlockSpec(memory_space=pltpu.SMEM), pl.BlockSpec(info.shape, lambda: (0, 0))],
        out_specs=pl.BlockSpec(info.shape, lambda: (0, 0)),
        out_shape=jax.ShapeDtypeStruct(info.shape, I32),
        name="positions",
    )(offs, info)[0]


def _dispatch_kernel(ends_ref, pos_ref, src_ref, dst_ref, zero_scr, sem, zero_sem):
    tm = pos_ref.shape[1]
    row_tile = zero_scr.shape[0]

    @pl.when(pl.program_id(0) == 0)
    def _():
        zero_scr[...] = jnp.zeros(zero_scr.shape, F32)

        def zero_tail(c):
            start = pl.multiple_of(ends_ref[c] - row_tile, row_tile)
            return pltpu.make_async_copy(zero_scr, dst_ref.at[pl.ds(start, row_tile), :], zero_sem)

        def nonempty(c):
            return ends_ref[c] > (ends_ref[c - 1] if c else 0)

        for c in range(N_CLASSES):
            @pl.when(nonempty(c))
            def _(c=c):
                zero_tail(c).start()
        for c in range(N_CLASSES):
            @pl.when(nonempty(c))
            def _(c=c):
                zero_tail(c).wait()

        def unused_tile(j):
            start = pl.multiple_of(ends_ref[N_CLASSES - 1] + j * row_tile, row_tile)
            return pltpu.make_async_copy(zero_scr, dst_ref.at[pl.ds(start, row_tile), :], zero_sem)

        n_unused = (dst_ref.shape[0] - ends_ref[N_CLASSES - 1]) // row_tile
        lax.fori_loop(0, n_unused, lambda j, carry: (unused_tile(j).start(), carry)[1], 0)
        lax.fori_loop(0, n_unused, lambda j, carry: (unused_tile(j).wait(), carry)[1], 0)

    for r in range(tm):
        pltpu.make_async_copy(src_ref.at[pl.ds(r, 1), :], dst_ref.at[pl.ds(pos_ref[0, r], 1), :],
                              sem).start(priority=r % 2)
    pltpu.make_async_copy(src_ref, dst_ref.at[pl.ds(0, tm), :], sem).wait()


def _dispatch(h2x, pos, ends, n_rows):
    t, w = h2x.shape
    tm = MOE_DMA_TILE
    smem_row = pl.BlockSpec((None, 1, tm), lambda i: (i, 0, 0), memory_space=pltpu.SMEM)
    return pl.pallas_call(
        _dispatch_kernel,
        grid=(t // tm,),
        in_specs=[pl.BlockSpec(memory_space=pltpu.SMEM), smem_row, pl.BlockSpec((tm, w), lambda i: (i, 0))],
        out_specs=pl.BlockSpec(memory_space=pl.ANY),
        out_shape=jax.ShapeDtypeStruct((n_rows, w), F32),
        scratch_shapes=[pltpu.VMEM((MOE_ROW_TILE, w), F32), pltpu.SemaphoreType.DMA, pltpu.SemaphoreType.DMA],
        compiler_params=pltpu.CompilerParams(
            dimension_semantics=("arbitrary",), vmem_limit_bytes=VMEM_LIMIT),
        name="dispatch",
    )(ends, pos, h2x)


def _experts_kernel(elo_ref, ehi_ref, live_ref, xs_ref, wg_lo, wg_hi, wu_lo, wu_hi, wd_lo, wd_hi, ys_ref):
    del elo_ref, ehi_ref
    d = ys_ref.shape[1]
    j = pl.program_id(0)

    @pl.when(live_ref[j] == 1)
    def _():
        h = xs_ref[:, :d].astype(BF16)

        def expert(wg, wu, wd, weight):
            gate = _dot(h, wg[...])
            hidden = ((gate * jax.nn.sigmoid(gate)) * _dot(h, wu[...])) * weight
            return _dot(hidden.astype(BF16), wd[...])

        ys_ref[...] = (expert(wg_lo, wu_lo, wd_lo, xs_ref[:, d:d + 1])
                       + expert(wg_hi, wu_hi, wd_hi, xs_ref[:, d + 1:d + 2]))

    @pl.when(live_ref[j] == 0)
    def _():
        ys_ref[...] = jnp.zeros(ys_ref.shape, F32)


def _experts(xs, e_lo, e_hi, live, wg, wu, wd, layer):
    n_rows, w = xs.shape
    _, ne, d, f = wg.shape
    tm = MOE_ROW_TILE
    lo = lambda j, elo, ehi, lv: (layer, elo[j], 0, 0)
    hi = lambda j, elo, ehi, lv: (layer, ehi[j], 0, 0)
    grid_spec = pltpu.PrefetchScalarGridSpec(
        num_scalar_prefetch=3,
        grid=(n_rows // tm,),
        in_specs=[pl.BlockSpec((tm, w), lambda j, elo, ehi, lv: (j, 0)),
                  pl.BlockSpec((None, None, d, f), lo), pl.BlockSpec((None, None, d, f), hi),
                  pl.BlockSpec((None, None, d, f), lo), pl.BlockSpec((None, None, d, f), hi),
                  pl.BlockSpec((None, None, f, d), lo), pl.BlockSpec((None, None, f, d), hi)],
        out_specs=pl.BlockSpec((tm, d), lambda j, elo, ehi, lv: (j, 0)),
    )
    return pl.pallas_call(
        _experts_kernel,
        grid_spec=grid_spec,
        out_shape=jax.ShapeDtypeStruct((n_rows, d), F32),
        compiler_params=pltpu.CompilerParams(
            dimension_semantics=("arbitrary",), vmem_limit_bytes=VMEM_LIMIT),
        name="experts",
    )(e_lo, e_hi, live, xs, wg, wg, wu, wu, wd, wd)


def _combine_kernel(pos_ref, next_ref, ys_ref, x1_ref, mod_ref, o_ref, buf, sem):
    tm = buf.shape[1]
    step = pl.program_id(0) * pl.num_programs(1) + pl.program_id(1)
    last = pl.num_programs(0) * pl.num_programs(1) - 1

    def issue(p_ref, slot):
        for r in range(tm):
            pltpu.make_async_copy(ys_ref.at[pl.ds(p_ref[slot, r], 1), :], buf.at[slot, pl.ds(r, 1), :],
                                  sem.at[slot]).start(priority=r % 2)

    def finish(slot):
        pltpu.make_async_copy(ys_ref.at[pl.ds(0, tm), :], buf.at[slot], sem.at[slot]).wait()
        rows = pl.ds(slot * tm, tm)
        o_ref[rows, :] = x1_ref[rows, :] + mod_ref[5:6, :] * buf[slot]

    @pl.when(step == 0)
    def _():
        issue(pos_ref, 0)

    issue(pos_ref, 1)
    finish(0)

    @pl.when(step < last)
    def _():
        issue(next_ref, 0)

    finish(1)


def _combine(ys, pos, x1, mod_l):
    b, s, d = x1.shape
    tm = MOE_DMA_TILE
    nt = s // (2 * tm)
    n_steps = b * nt
    pos2 = pos.reshape(n_steps, 2, tm)
    this_step = pl.BlockSpec((None, 2, tm), lambda bb, i: (bb * nt + i, 0, 0), memory_space=pltpu.SMEM)
    next_step = pl.BlockSpec((None, 2, tm), lambda bb, i: (jnp.minimum(bb * nt + i + 1, n_steps - 1), 0, 0),
                             memory_space=pltpu.SMEM)
    return pl.pallas_call(
        _combine_kernel,
        grid=(b, nt),
        in_specs=[this_step, next_step,
                  pl.BlockSpec(memory_space=pl.ANY),
                  pl.BlockSpec((None, 2 * tm, d), lambda bb, i: (bb, i, 0)),
                  pl.BlockSpec((None, N_MOD, d), lambda bb, i: (bb, 0, 0))],
        out_specs=pl.BlockSpec((None, 2 * tm, d), lambda bb, i: (bb, i, 0)),
        out_shape=jax.ShapeDtypeStruct((b, s, d), F32),
        scratch_shapes=[pltpu.VMEM((2, tm, d), F32), pltpu.SemaphoreType.DMA((2,))],
        compiler_params=pltpu.CompilerParams(
            dimension_semantics=("arbitrary", "arbitrary"), vmem_limit_bytes=VMEM_LIMIT),
        name="combine",
    )(pos2, pos2, ys, x1, mod_l)


def _moe(h2x, info, counts, wg, wu, wd, layer, x1, mod_l):
    b, s, d = x1.shape
    t = b * s
    tm = MOE_ROW_TILE
    n_tiles = t // tm + N_CLASSES
    counts = counts[:N_CLASSES, 0]
    padded = (counts + tm - 1) // tm * tm
    ends = jnp.cumsum(padded)
    offs = jnp.pad(ends - padded, (0, N_CLASS_PAD - N_CLASSES)).astype(I32)
    tile_cls = jnp.sum(jnp.arange(n_tiles, dtype=I32)[:, None] * tm >= ends[None, :], axis=1)
    live = (tile_cls < N_CLASSES).astype(I32)
    tile_cls = jnp.minimum(tile_cls, N_CLASSES - 1)
    group, pair = tile_cls // N_PAIRS, tile_cls % N_PAIRS
    pairs = jnp.asarray(_PAIRS, I32)
    e_lo = group * EXPERTS_PER_GROUP + pairs[pair, 0]
    e_hi = group * EXPERTS_PER_GROUP + pairs[pair, 1]

    pos = _positions(info, offs).reshape(t // MOE_DMA_TILE, 1, MOE_DMA_TILE)
    xs = _dispatch(h2x.reshape(t, d + LANES), pos, ends.astype(I32), n_tiles * tm)
    ys = _experts(xs, e_lo, e_hi, live, wg, wu, wd, layer)
    return _combine(ys, pos, x1, mod_l)


def kernel(x, c, w_ada, b_ada, norm1_g, w_in, q_norm_g, k_norm_g, rel_bias, w_o_a, w_o_b, w_out,
           norm2_g, router_w, router_b, w_gate, w_up, w_down):
    b, s, d = x.shape
    depth = w_ada.shape[0]
    assert d % LANES == 0
    tm = min(512, s)

    mod = _adaln(c, w_ada, b_ada)
    bias = _bias_tiles(rel_bias, DSA_QUERY_BLOCK, DSA_KEY_BLOCK)
    rw = jnp.pad(router_w, ((0, 0), (0, LANES - N_EXPERTS)))
    rwh = rw.astype(BF16)
    rwl = (rw - rwh.astype(F32)).astype(BF16)
    rb = jnp.pad(router_b, (0, LANES - N_EXPERTS)).reshape(1, LANES)
    wg, wu, wd = w_gate.astype(BF16), w_up.astype(BF16), w_down.astype(BF16)

    for l in range(depth):
        qa, ka, va, qbt, qit, kb, vt, ki, wit, sga, sgb = _in_proj(
            x, mod[l], norm1_g[l].reshape(1, d), _pack_w_in(w_in[l]),
            q_norm_g[l].reshape(HEAD_DIM, 1), k_norm_g[l].reshape(1, HEAD_DIM), tm)
        oa = _sb_attention(qa, ka, va)
        ob = _dsa_attention(qit, ki, wit, qbt, kb, vt, bias)
        x1, h2x, info, counts = _merge(oa, ob, sga, sgb, x, mod[l], w_o_a[l].astype(BF16),
                                       w_o_b[l].astype(BF16), w_out[l].astype(BF16),
                                       norm2_g[l].reshape(1, d), rwh, rwl, rb, tm)
        x = _moe(h2x, info, counts, wg, wu, wd, l, x1, mod[l])
    return x
```

```python
import functools
import math

import numpy as np
import jax
import jax.numpy as jnp
from jax import lax
from jax.experimental import pallas as pl
from jax.experimental.pallas import tpu as pltpu

F32 = jnp.float32
BF16 = jnp.bfloat16
I32 = jnp.int32
I16 = jnp.int16

HEAD_DIM = 64
N_HEADS = 8
N_EXPERTS = 16
N_GROUPS = 4
EXPERTS_PER_GROUP = 4
_PAIRS = ((0, 1), (0, 2), (0, 3), (1, 2), (1, 3), (2, 3))
N_PAIRS = len(_PAIRS)
N_CLASSES = N_GROUPS * N_PAIRS
N_CLASS_PAD = 32
MOE_ROW_TILE = 256
MOE_DMA_TILE = 512
N_MOD = 6
EPS = 1e-6
INDEX_TOPK_MAX = 256
REL_BUCKETS = 32
REL_MAX_DIST = 128
LANES = 128
SUBLANES = 8
PACKED_SUBLANES = 16
HALF_BITS = 16
TOKEN_TILE = 512
EXP2_MAX = 126.0
DSA_QUERY_BLOCK = 256
DSA_KEY_BLOCK = 128
SB_QUERY_BLOCK = 256
SB_KEY_BLOCK = 256
SB_DEAD_LOG2 = 150.0
LOG2E = 1.4426950408889634
NEG_BIG = -1e30
INT_MIN = -2 ** 31
INT_MAX = 2 ** 31 - 1
HALF_BIAS = 2 ** 15

VMEM_LIMIT = 56 * 1024 * 1024


def _nt_dot(a, b):
    return lax.dot_general(a, b, (((1,), (1,)), ((), ())), preferred_element_type=F32)


def _dot(a, b):
    return jnp.dot(a, b, preferred_element_type=F32)


def _mod_kernel(c_ref, w_ref, b_ref, o_ref):
    c = c_ref[...]
    sc = c * jax.nn.sigmoid(c)
    o_ref[...] = _dot(sc.astype(BF16), w_ref[...].astype(BF16)) + b_ref[...]


def _adaln(c, w_ada, b_ada):
    depth, d, nd = w_ada.shape
    b = c.shape[0]
    out = pl.pallas_call(
        _mod_kernel,
        grid=(depth, nd // d),
        in_specs=[pl.BlockSpec((b, d), lambda l, j: (0, 0)),
                  pl.BlockSpec((None, d, d), lambda l, j: (l, 0, j)),
                  pl.BlockSpec((None, 1, d), lambda l, j: (l, 0, j))],
        out_specs=pl.BlockSpec((None, b, d), lambda l, j: (l, 0, j)),
        out_shape=jax.ShapeDtypeStruct((depth, b, nd), F32),
        compiler_params=pltpu.CompilerParams(vmem_limit_bytes=VMEM_LIMIT),
        name="adaln",
    )(c, w_ada, b_ada.reshape(depth, 1, nd))
    return out.reshape(depth, b, N_MOD, d)


def _rms(x):
    return x * lax.rsqrt(jnp.mean(x * x, axis=-1, keepdims=True) + EPS)


def _in_kernel(x_ref, mod_ref, g1_ref, w_ref, qng_ref, kng_ref,
               qa_ref, ka_ref, va_ref, qbt_ref, qit_ref, kb_ref, vt_ref, ki_ref, wit_ref,
               sga_ref, sgb_ref):
    d = x_ref.shape[-1]
    tm = x_ref.shape[0]
    hw = N_HEADS * HEAD_DIM
    x = x_ref[...]
    h = (_rms(x) * g1_ref[...]) * (1.0 + mod_ref[1:2, :]) + mod_ref[0:1, :]
    hb = h.astype(BF16)

    def proj(lo, n):
        return _dot(hb, w_ref[:, lo:lo + n])

    def heads(r, ref):
        for hh in range(N_HEADS):
            ref[hh] = r[:, hh * HEAD_DIM:(hh + 1) * HEAD_DIM].astype(ref.dtype)

    scale = HEAD_DIM ** -0.5
    heads(proj(0, hw) * (scale * LOG2E), qa_ref)
    heads(proj(hw, hw), ka_ref)
    heads(proj(2 * hw, hw), va_ref)

    rt = proj(3 * hw, hw).T
    for hh in range(N_HEADS):
        slab = rt[hh * HEAD_DIM:(hh + 1) * HEAD_DIM, :]
        inv = lax.rsqrt(jnp.mean(slab * slab, axis=0, keepdims=True) + EPS)
        qbt_ref[hh] = (((slab * inv) * qng_ref[...]) * (scale * LOG2E)).astype(BF16)

    r = proj(4 * hw, 2 * HEAD_DIM)
    kb_ref[...] = (_rms(r[:, :HEAD_DIM]) * kng_ref[...]).astype(BF16)
    vt = r.T[HEAD_DIM:, :]
    kblk = vt_ref.shape[-1]
    ones_row = (lax.broadcasted_iota(I32, (HEAD_DIM, kblk), 0) == 0).astype(BF16)
    for cc in range(tm // kblk):
        vt_ref[cc, :HEAD_DIM, :] = vt[:, cc * kblk:(cc + 1) * kblk].astype(BF16)
        vt_ref[cc, HEAD_DIM:, :] = ones_row

    rt = proj(4 * hw + LANES, hw).T
    for hh in range(N_HEADS):
        qit_ref[hh] = rt[hh * HEAD_DIM:(hh + 1) * HEAD_DIM, :].astype(BF16)

    r = proj(5 * hw + LANES, LANES)
    ki_ref[...] = r[:, :HEAD_DIM].astype(BF16)
    wit_ref[...] = r.T[HEAD_DIM:HEAD_DIM + N_HEADS, :]

    off = 5 * hw + 2 * LANES
    sga_ref[...] = jax.nn.sigmoid(proj(off, d)).astype(BF16)
    sgb_ref[...] = jax.nn.sigmoid(proj(off + d, d)).astype(BF16)


def _pack_w_in(w_in_l):
    hw = N_HEADS * HEAD_DIM
    a = 4 * hw + 2 * HEAD_DIM + hw
    small = w_in_l[:, a:a + HEAD_DIM + N_HEADS]
    small = jnp.pad(small, ((0, 0), (0, LANES - small.shape[1])))
    gates = w_in_l[:, a + HEAD_DIM + N_HEADS:]
    return jnp.concatenate([w_in_l[:, :a], small, gates], axis=1).astype(BF16)


def _in_proj(x, mod_l, g1, w_packed, qng, kng, tm):
    b, s, d = x.shape
    hm = jax.ShapeDtypeStruct((b, N_HEADS, s, HEAD_DIM), BF16)
    hmt = jax.ShapeDtypeStruct((b, N_HEADS, HEAD_DIM, s), BF16)
    tok64 = jax.ShapeDtypeStruct((b, s, HEAD_DIM), BF16)
    hm_spec = pl.BlockSpec((None, N_HEADS, tm, HEAD_DIM), lambda bb, i: (bb, 0, i, 0))
    hmt_spec = pl.BlockSpec((None, N_HEADS, HEAD_DIM, tm), lambda bb, i: (bb, 0, 0, i))
    tok = lambda n: pl.BlockSpec((None, tm, n), lambda bb, i: (bb, i, 0))
    nw = w_packed.shape[1]
    return pl.pallas_call(
        _in_kernel,
        grid=(b, s // tm),
        in_specs=[tok(d),
                  pl.BlockSpec((None, N_MOD, d), lambda bb, i: (bb, 0, 0)),
                  pl.BlockSpec((1, d), lambda bb, i: (0, 0)),
                  pl.BlockSpec((d, nw), lambda bb, i: (0, 0)),
                  pl.BlockSpec((HEAD_DIM, 1), lambda bb, i: (0, 0)),
                  pl.BlockSpec((1, HEAD_DIM), lambda bb, i: (0, 0))],
        out_specs=[hm_spec] * 3 + [hmt_spec] * 2
                  + [tok(HEAD_DIM),
                     pl.BlockSpec((None, tm // DSA_QUERY_BLOCK, 2 * HEAD_DIM, DSA_QUERY_BLOCK),
                                  lambda bb, i: (bb, i, 0, 0)),
                     tok(HEAD_DIM),
                     pl.BlockSpec((None, N_HEADS, tm), lambda bb, i: (bb, 0, i)),
                     tok(d), tok(d)],
        out_shape=[hm] * 3 + [hmt] * 2
                  + [tok64,
                     jax.ShapeDtypeStruct((b, s // DSA_QUERY_BLOCK, 2 * HEAD_DIM, DSA_QUERY_BLOCK), BF16),
                     tok64,
                     jax.ShapeDtypeStruct((b, N_HEADS, s), F32),
                     jax.ShapeDtypeStruct((b, s, d), BF16),
                     jax.ShapeDtypeStruct((b, s, d), BF16)],
        compiler_params=pltpu.CompilerParams(
            dimension_semantics=("parallel", "parallel"), vmem_limit_bytes=VMEM_LIMIT),
        name="in_proj",
    )(x, mod_l, g1, w_packed, qng, kng)


def _sb_kernel(q_ref, k_ref, v_ref, o_ref, acc_scr, csum_scr, sp_scr, ls_scr):
    tq = q_ref.shape[1]
    tk = SB_KEY_BLOCK
    i = pl.program_id(1)
    nfull = (i * tq) // tk
    row = lax.broadcasted_iota(I32, (tq, tk), 0)
    col = lax.broadcasted_iota(I32, (tq, tk), 1)
    before = (col - row) < (i * tq - nfull * tk)
    krow = lax.broadcasted_iota(I32, (tk, tk), 0)
    kcol = lax.broadcasted_iota(I32, (tk, tk), 1)
    suffix = (krow > kcol).astype(BF16)

    def stage_a(hh, st, first):
        z2 = _nt_dot(q_ref[hh], k_ref[hh, pl.ds(st, tk), :])
        sp2 = jnp.maximum(jnp.log2(1.0 + jnp.exp2(jnp.minimum(z2, EXP2_MAX))), z2)
        if first:
            sp2 = jnp.where(before, sp2, 0.0)
        sp_scr[hh] = sp2.astype(BF16)
        ls_scr[hh] = z2 - sp2

    def stage_b(hh, st, first):
        sp = sp_scr[hh]
        suf = _dot(sp, suffix)
        if first:
            a = jnp.where(before, jnp.exp2(ls_scr[hh] - suf), 0.0)
            csum = jnp.zeros((tq, 1), F32)
        else:
            csum = csum_scr[hh]
            a = jnp.exp2((ls_scr[hh] - suf) - csum)
        c = _dot(a.astype(BF16), v_ref[hh, pl.ds(st, tk), :])
        if first:
            acc_scr[hh] = c
        else:
            acc_scr[hh] += c
        csum_scr[hh] = csum + (suf[:, 0:1] + sp[:, 0:1].astype(F32))

    def key_block(st, first):
        for hh in range(N_HEADS):
            stage_a(hh, st, first)
        for hh in range(N_HEADS):
            stage_b(hh, st, first)

    key_block(pl.multiple_of(nfull * tk, tk), True)

    def live():
        return jnp.min(csum_scr[...]) < SB_DEAD_LOG2

    def body(carry):
        jj, _ = carry
        key_block(pl.multiple_of((nfull - 1 - jj) * tk, tk), False)
        return jj + 1, live()

    lax.while_loop(lambda carry: (carry[0] < nfull) & carry[1], body, (jnp.int32(0), live()))
    for hh in range(N_HEADS):
        o_ref[:, hh * HEAD_DIM:(hh + 1) * HEAD_DIM] = acc_scr[hh].astype(o_ref.dtype)


def _sb_attention(qa, ka, va):
    b, nh, s, dh = qa.shape
    tq = SB_QUERY_BLOCK
    assert SB_KEY_BLOCK % tq == 0 and s % SB_KEY_BLOCK == 0
    return pl.pallas_call(
        _sb_kernel,
        grid=(b, s // tq),
        in_specs=[pl.BlockSpec((None, nh, tq, dh), lambda bb, i: (bb, 0, i, 0)),
                  pl.BlockSpec((None, nh, s, dh), lambda bb, i: (bb, 0, 0, 0)),
                  pl.BlockSpec((None, nh, s, dh), lambda bb, i: (bb, 0, 0, 0))],
        out_specs=pl.BlockSpec((None, tq, nh * dh), lambda bb, i: (bb, i, 0)),
        out_shape=jax.ShapeDtypeStruct((b, s, nh * dh), BF16),
        scratch_shapes=[pltpu.VMEM((nh, tq, dh), F32), pltpu.VMEM((nh, tq, 1), F32),
                        pltpu.VMEM((nh, tq, SB_KEY_BLOCK), BF16), pltpu.VMEM((nh, tq, SB_KEY_BLOCK), F32)],
        compiler_params=pltpu.CompilerParams(
            dimension_semantics=("parallel", "parallel"), vmem_limit_bytes=VMEM_LIMIT),
        name="stick_breaking",
    )(qa, ka, va)


def _t5_bucket_np(dist):
    max_exact = REL_BUCKETS // 2
    d_f = np.maximum(dist, 1).astype(np.float32)
    large = max_exact + (np.log(d_f / np.float32(max_exact))
                         / np.float32(math.log(REL_MAX_DIST / max_exact))
                         * np.float32(REL_BUCKETS - max_exact)).astype(np.int32)
    large = np.minimum(large, REL_BUCKETS - 1)
    return np.where(dist < max_exact, dist, large).astype(np.int32)


def _bias_tiles(rel_bias, tq, tk):
    n_near = tq // tk + 1
    key = np.arange(tk)[:, None]
    query = np.arange(tq)[None, :]
    bk = np.stack([_t5_bucket_np(np.maximum(query - key - (n - 1) * tk, 0)) for n in range(n_near)])
    far = int(_t5_bucket_np(np.array([REL_MAX_DIST]))[0])
    nh = rel_bias.shape[1]

    def body(rb_ref, bk_ref, o_ref):
        hh = pl.program_id(1)
        buckets = bk_ref[...]
        tile = jnp.zeros((tk, tq), F32)
        for bucket in range(REL_BUCKETS):
            tile = jnp.where(buckets == bucket, (rb_ref[hh, bucket] - rb_ref[hh, far]) * LOG2E, tile)
        o_ref[...] = tile

    return pl.pallas_call(
        body,
        grid=(n_near, nh),
        in_specs=[pl.BlockSpec(memory_space=pltpu.SMEM),
                  pl.BlockSpec((None, tk, tq), lambda n, hh: (n, 0, 0))],
        out_specs=pl.BlockSpec((None, None, tk, tq), lambda n, hh: (n, hh, 0, 0)),
        out_shape=jax.ShapeDtypeStruct((n_near, nh, tk, tq), F32),
        name="bias_tiles",
    )(rel_bias.T, jnp.asarray(bk))


def _sort_key(x):
    bits = pltpu.bitcast(x, I32)
    return bits ^ ((bits >> 31) & INT_MAX)


def _fold_rows(x, op):
    return functools.reduce(
        op, [x[g * SUBLANES:(g + 1) * SUBLANES, :] for g in range(x.shape[0] // SUBLANES)])


def _dsa_kernel(ktop, qit_ref, ki_ref, wit_ref, qbt_ref, kb_ref, vt_ref, bias_ref, o_ref,
                key_scr, hi_scr, lo_scr, m_scr, acc_scr, s_scr):
    tq = qit_ref.shape[2]
    tk = DSA_KEY_BLOCK
    per_q = tq // tk
    i = pl.program_id(1)
    first_own = per_q * i
    nchunks = per_q * (i + 1)
    key_i = lax.broadcasted_iota(I32, (tk, tq), 0)
    qry_i = lax.broadcasted_iota(I32, (tk, tq), 1)

    def causal(rel):
        return (key_i - qry_i) <= (-rel * tk)

    def for_chunks(fn):
        def far(j, carry):
            for u in range(per_q):
                fn(j * per_q + u, None)
            return carry
        lax.fori_loop(0, i - 1, far, 0)

        @pl.when(i >= 1)
        def _():
            for rel in range(-per_q, -1):
                fn(first_own + rel, None)
            fn(first_own - 1, -1)

        for rel in range(per_q):
            fn(first_own + rel, rel)

    def score_chunk(c, rel):
        kic = ki_ref[pl.ds(pl.multiple_of(c * tk, tk), tk), :]
        score = jnp.zeros((tk, tq), F32)
        for hh in range(N_HEADS):
            score = score + wit_ref[hh:hh + 1, :] * jnp.maximum(_dot(kic, qit_ref[hh]), 0.0)
        keys = _sort_key(score)
        if rel is not None and rel >= 0:
            keys = jnp.where(causal(rel), keys, INT_MIN)
        key_scr[c] = keys
        hi_scr[c] = (keys >> HALF_BITS).astype(I16)
        lo_scr[c] = ((keys & (2 * HALF_BIAS - 1)) - HALF_BIAS).astype(I16)

    for_chunks(score_chunk)

    def count(pred):
        def body(j, part):
            for u in range(per_q):
                part = part + _fold_rows(jnp.where(pred(key_scr[j * per_q + u]), 1, 0), jnp.add)
            return part
        part = lax.fori_loop(0, i + 1, body, jnp.zeros((SUBLANES, tq), I32))
        return jnp.sum(part, axis=0, keepdims=True)

    def count16(scr, pred):
        one, zero = jnp.int16(1), jnp.int16(0)

        def block(j, part):
            for u in range(per_q):
                hit = jnp.where(pred(scr[j * per_q + u]), one, zero)
                part = part + functools.reduce(
                    jnp.add, [hit[g * PACKED_SUBLANES:(g + 1) * PACKED_SUBLANES, :]
                              for g in range(tk // PACKED_SUBLANES)])
            return part

        nblk = i + 1
        part = lax.fori_loop(0, nblk // 2, lambda jj, part: block(2 * jj + 1, block(2 * jj, part)),
                             jnp.zeros((PACKED_SUBLANES, tq), I16))
        part = lax.cond(nblk % 2 == 1, lambda p: block(nblk - 1, p), lambda p: p, part)
        return jnp.sum(part.astype(I32), axis=0, keepdims=True)

    def search16(scr, base):
        def bit_body(bi, t_u):
            cand_u = t_u | jnp.left_shift(jnp.int32(1), HALF_BITS - 1 - bi)
            cand = (cand_u - HALF_BIAS).astype(I16)
            return jnp.where(base + count16(scr, lambda k: k >= cand) >= ktop, cand_u, t_u)
        return lax.fori_loop(0, HALF_BITS, bit_body, jnp.zeros((1, tq), I32))

    t_hi = search16(hi_scr, 0) - HALF_BIAS
    t_hi16 = t_hi.astype(I16)
    above = count16(hi_scr, lambda k: k > t_hi16)

    def keep_band(j, carry):
        for u in range(per_q):
            c = j * per_q + u
            lo_scr[c] = jnp.where(hi_scr[c] == t_hi16, lo_scr[c], jnp.int16(-HALF_BIAS))
        return carry

    lax.fori_loop(0, i + 1, keep_band, 0)
    thr = t_hi * (2 * HALF_BIAS) + search16(lo_scr, above)
    n_ge = count(lambda k: k >= thr)

    @pl.when(jnp.max(n_ge) > ktop)
    def _():
        need = (ktop - count(lambda k: k > thr)).astype(F32)
        lower = (lax.broadcasted_iota(I32, (tk, tk), 1)
                 < lax.broadcasted_iota(I32, (tk, tk), 0)).astype(BF16)

        def body(c, seen):
            k = key_scr[c]
            eq = k == thr
            eqf = eq.astype(BF16)
            rank = seen + _dot(lower, eqf)
            drop = eq & (rank >= need) & (n_ge > ktop)
            key_scr[c] = jnp.where(drop, thr - 1, k)
            return seen + jnp.sum(eqf.astype(F32), axis=0, keepdims=True)

        lax.fori_loop(0, nchunks, body, jnp.zeros((1, tq), F32))

    def logits(c, rel):
        sel = key_scr[c] >= thr
        if rel is not None and rel >= 0:
            sel = sel & causal(rel)
        kbc = kb_ref[pl.ds(pl.multiple_of(c * tk, tk), tk), :]

        def head(hh):
            s = _dot(kbc, qbt_ref[hh])
            if rel is not None:
                s = s + bias_ref[rel + 1, hh]
            return s
        return sel, head

    m_scr[...] = jnp.full(m_scr.shape, NEG_BIG, F32)

    def max_chunk(c, rel):
        sel, head = logits(c, rel)
        for hh in range(N_HEADS):
            s = jnp.where(sel, head(hh), NEG_BIG)
            s_scr[hh, pl.ds(pl.multiple_of(c * tk, tk), tk), :] = s
            m_scr[hh] = jnp.maximum(m_scr[hh], _fold_rows(s, jnp.maximum))

    for_chunks(max_chunk)
    m = [jnp.max(m_scr[hh], axis=0, keepdims=True) for hh in range(N_HEADS)]

    acc_scr[...] = jnp.zeros(acc_scr.shape, F32)

    def att_block(j, carry):
        vtb = vt_ref[j]
        rows = pl.ds(pl.multiple_of(j * tq, tq), tq)
        for hh in range(N_HEADS):
            p = jnp.exp2(s_scr[hh, rows, :] - m[hh])
            acc_scr[hh] += _dot(vtb, p.astype(BF16))
        return carry

    lax.fori_loop(0, i + 1, att_block, 0)

    out_t = jnp.concatenate(
        [acc_scr[hh, :HEAD_DIM, :] / acc_scr[hh, HEAD_DIM:HEAD_DIM + 1, :] for hh in range(N_HEADS)],
        axis=0)
    o_ref[...] = out_t.T.astype(o_ref.dtype)


def _dsa_attention(qit, ki, wit, qbt, kb, vt, bias):
    b, nh, dh, s = qit.shape
    tq = DSA_QUERY_BLOCK
    tk = DSA_KEY_BLOCK
    assert tk == REL_MAX_DIST and tk == LANES and tq % tk == 0 and s % tq == 0
    ktop = min(INDEX_TOPK_MAX, s // 4)
    hmt_q = pl.BlockSpec((None, nh, dh, tq), lambda bb, i: (bb, 0, 0, i))
    full64 = pl.BlockSpec((None, s, dh), lambda bb, i: (bb, 0, 0))
    return pl.pallas_call(
        functools.partial(_dsa_kernel, ktop),
        grid=(b, s // tq),
        in_specs=[hmt_q, full64,
                  pl.BlockSpec((None, nh, tq), lambda bb, i: (bb, 0, i)),
                  hmt_q, full64,
                  pl.BlockSpec((None, s // tq, 2 * dh, tq), lambda bb, i: (bb, 0, 0, 0)),
                  pl.BlockSpec(bias.shape, lambda bb, i: (0, 0, 0, 0))],
        out_specs=pl.BlockSpec((None, tq, nh * dh), lambda bb, i: (bb, i, 0)),
        out_shape=jax.ShapeDtypeStruct((b, s, nh * dh), BF16),
        scratch_shapes=[pltpu.VMEM((s // tk, tk, tq), I32),
                        pltpu.VMEM((s // tk, tk, tq), I16),
                        pltpu.VMEM((s // tk, tk, tq), I16),
                        pltpu.VMEM((nh, SUBLANES, tq), F32),
                        pltpu.VMEM((nh, 2 * dh, tq), F32),
                        pltpu.VMEM((nh, s, tq), F32)],
        compiler_params=pltpu.CompilerParams(
            dimension_semantics=("parallel", "parallel"), vmem_limit_bytes=VMEM_LIMIT),
        name="sparse_attention",
    )(qit, ki, wit, qbt, kb, vt, bias)


def _first_max4(a):
    m1 = jnp.maximum(jnp.maximum(a[0], a[1]), jnp.maximum(a[2], a[3]))
    i1 = jnp.where(a[0] == m1, 0, jnp.where(a[1] == m1, 1, jnp.where(a[2] == m1, 2, 3)))
    rest = [jnp.where(i1 == j, -1.0, a[j]) for j in range(4)]
    m2 = jnp.maximum(jnp.maximum(rest[0], rest[1]), jnp.maximum(rest[2], rest[3]))
    i2 = jnp.where(rest[0] == m2, 0, jnp.where(rest[1] == m2, 1, jnp.where(rest[2] == m2, 2, 3)))
    return m1, i1, m2, i2


def _route(lt):
    lg = [lt[e:e + 1, :] for e in range(N_EXPERTS)]
    mx = functools.reduce(jnp.maximum, lg)
    ex = [jnp.exp(v - mx) for v in lg]
    tot = functools.reduce(lambda u, v: u + v, ex)
    p = [v / tot for v in ex]
    gs = []
    for g in range(N_GROUPS):
        m1, _, m2, _ = _first_max4(p[4 * g:4 * g + 4])
        gs.append(m1 + m2)
    _, gbest, _, _ = _first_max4(gs)
    chosen = [jnp.where(gbest == 0, p[j], jnp.where(gbest == 1, p[4 + j],
              jnp.where(gbest == 2, p[8 + j], p[12 + j]))) for j in range(4)]
    m1, i1, m2, i2 = _first_max4(chosen)
    den = m1 + m2
    lo = jnp.minimum(i1, i2)
    hi = jnp.maximum(i1, i2)
    pair = jnp.where(lo == 0, hi - 1, jnp.where(lo == 1, hi + 1, N_PAIRS - 1))
    first_is_lo = i1 < i2
    w1 = m1 / den
    w2 = m2 / den
    return gbest * N_PAIRS + pair, jnp.where(first_is_lo, w1, w2), jnp.where(first_is_lo, w2, w1)


def _merge_kernel(oa_ref, ob_ref, sga_ref, sgb_ref, x_ref, mod_ref, woa_ref, wob_ref, wout_ref,
                  g2_ref, rwh_ref, rwl_ref, rb_ref, x1_ref, h2x_ref, info_ref, cnt_ref, carry_scr):
    tm, d = x_ref.shape

    @pl.when((pl.program_id(0) == 0) & (pl.program_id(1) == 0))
    def _():
        carry_scr[...] = jnp.zeros(carry_scr.shape, F32)

    merged = (sga_ref[...].astype(F32) * _dot(oa_ref[...], woa_ref[...])
              + sgb_ref[...].astype(F32) * _dot(ob_ref[...], wob_ref[...]))
    x1 = x_ref[...] + mod_ref[2:3, :] * _dot(merged.astype(BF16), wout_ref[...])
    x1_ref[...] = x1
    h2 = (_rms(x1) * g2_ref[...]) * (1.0 + mod_ref[4:5, :]) + mod_ref[3:4, :]
    h2x_ref[:, :d] = h2
    hi = h2.astype(BF16)
    lo = (h2 - hi.astype(F32)).astype(BF16)
    logits = (_dot(hi, rwh_ref[...]) + _dot(lo, rwh_ref[...]) + _dot(hi, rwl_ref[...])) + rb_ref[...]
    cls, cw_lo, cw_hi = _route(logits.T)

    ncls = carry_scr.shape[0]
    onehot = (lax.broadcasted_iota(I32, (ncls, tm), 0) == cls).astype(F32)
    earlier = (lax.broadcasted_iota(I32, (tm, tm), 0)
               < lax.broadcasted_iota(I32, (tm, tm), 1)).astype(BF16)
    before = _dot(onehot.astype(BF16), earlier)
    carry = carry_scr[...]
    carry_t = jnp.concatenate([carry] * (tm // LANES), axis=1)
    rank = jnp.sum(onehot * (carry_t + before), axis=0, keepdims=True).astype(I32)
    carry = carry + jnp.sum(onehot, axis=1, keepdims=True)
    carry_scr[...] = carry
    cnt_ref[...] = carry.astype(I32)

    sub = lax.broadcasted_iota(I32, (SUBLANES, tm), 0)
    info_ref[...] = jnp.where(sub == 0, cls, jnp.where(sub == 1, rank, 0))
    sub = lax.broadcasted_iota(I32, (LANES, tm), 0)
    weights_t = jnp.where(sub == 0, cw_lo, jnp.where(sub == 1, cw_hi, 0.0))
    h2x_ref[:, d:] = weights_t.T


def _merge(oa, ob, sga, sgb, x, mod_l, woa, wob, wout, g2, rwh, rwl, rb, tm):
    b, s, d = x.shape
    hw = oa.shape[-1]
    nt = s // tm
    tok = lambda n: pl.BlockSpec((None, tm, n), lambda bb, i: (bb, i, 0))
    const = lambda r, c: pl.BlockSpec((r, c), lambda bb, i: (0, 0))
    return pl.pallas_call(
        _merge_kernel,
        grid=(b, nt),
        in_specs=[tok(hw), tok(hw), tok(d), tok(d), tok(d),
                  pl.BlockSpec((None, N_MOD, d), lambda bb, i: (bb, 0, 0)),
                  const(hw, d), const(hw, d), const(d, d), const(1, d),
                  const(d, LANES), const(d, LANES), const(1, LANES)],
        out_specs=[tok(d), tok(d + LANES),
                   pl.BlockSpec((SUBLANES, tm), lambda bb, i: (0, bb * nt + i)),
                   const(N_CLASS_PAD, LANES)],
        out_shape=[jax.ShapeDtypeStruct((b, s, d), F32),
                   jax.ShapeDtypeStruct((b, s, d + LANES), F32),
                   jax.ShapeDtypeStruct((SUBLANES, b * s), I32),
                   jax.ShapeDtypeStruct((N_CLASS_PAD, LANES), I32)],
        scratch_shapes=[pltpu.VMEM((N_CLASS_PAD, LANES), F32)],
        compiler_params=pltpu.CompilerParams(
            dimension_semantics=("arbitrary", "arbitrary"), vmem_limit_bytes=VMEM_LIMIT),
        name="merge_route",
    )(oa, ob, sga, sgb, x, mod_l, woa, wob, wout, g2, rwh, rwl, rb)


def _positions_kernel(offs_ref, info_ref, pos_ref):
    cls = info_ref[0:1, :]
    pos = info_ref[1:2, :]
    for c in range(N_CLASSES):
        pos = pos + jnp.where(cls == c, offs_ref[c], 0)
    pos_ref[...] = jnp.broadcast_to(pos, pos_ref.shape)


def _positions(info, offs):
    return pl.pallas_call(
        _positions_kernel,
        in_specs=[pl.BlockSpec(memory_space=pltpu.SMEM), pl.BlockSpec(info.shape, lambda: (0, 0))],
        out_specs=pl.BlockSpec(info.shape, lambda: (0, 0)),
        out_shape=jax.ShapeDtypeStruct(info.shape, I32),
        name="positions",
    )(offs, info)[0]


def _dispatch_kernel(ends_ref, pos_ref, src_ref, dst_ref, zero_scr, sem, zero_sem):
    tm = pos_ref.shape[1]
    row_tile = zero_scr.shape[0]

    @pl.when(pl.program_id(0) == 0)
    def _():
        zero_scr[...] = jnp.zeros(zero_scr.shape, F32)

        def zero_tail(c):
            start = pl.multiple_of(ends_ref[c] - row_tile, row_tile)
            return pltpu.make_async_copy(zero_scr, dst_ref.at[pl.ds(start, row_tile), :], zero_sem)

        def nonempty(c):
            return ends_ref[c] > (ends_ref[c - 1] if c else 0)

        for c in range(N_CLASSES):
            @pl.when(nonempty(c))
            def _(c=c):
                zero_tail(c).start()
        for c in range(N_CLASSES):
            @pl.when(nonempty(c))
            def _(c=c):
                zero_tail(c).wait()

        def unused_tile(j):
            start = pl.multiple_of(ends_ref[N_CLASSES - 1] + j * row_tile, row_tile)
            return pltpu.make_async_copy(zero_scr, dst_ref.at[pl.ds(start, row_tile), :], zero_sem)

        n_unused = (dst_ref.shape[0] - ends_ref[N_CLASSES - 1]) // row_tile
        lax.fori_loop(0, n_unused, lambda j, carry: (unused_tile(j).start(), carry)[1], 0)
        lax.fori_loop(0, n_unused, lambda j, carry: (unused_tile(j).wait(), carry)[1], 0)

    for r in range(tm):
        pltpu.make_async_copy(src_ref.at[pl.ds(r, 1), :], dst_ref.at[pl.ds(pos_ref[0, r], 1), :],
                              sem).start(priority=r % 2)
    pltpu.make_async_copy(src_ref, dst_ref.at[pl.ds(0, tm), :], sem).wait()


def _dispatch(h2x, pos, ends, n_rows):
    t, w = h2x.shape
    tm = MOE_DMA_TILE
    smem_row = pl.BlockSpec((None, 1, tm), lambda i: (i, 0, 0), memory_space=pltpu.SMEM)
    return pl.pallas_call(
        _dispatch_kernel,
        grid=(t // tm,),
        in_specs=[pl.BlockSpec(memory_space=pltpu.SMEM), smem_row, pl.BlockSpec((tm, w), lambda i: (i, 0))],
        out_specs=pl.BlockSpec(memory_space=pl.ANY),
        out_shape=jax.ShapeDtypeStruct((n_rows, w), F32),
        scratch_shapes=[pltpu.VMEM((MOE_ROW_TILE, w), F32), pltpu.SemaphoreType.DMA, pltpu.SemaphoreType.DMA],
        compiler_params=pltpu.CompilerParams(
            dimension_semantics=("arbitrary",), vmem_limit_bytes=VMEM_LIMIT),
        name="dispatch",
    )(ends, pos, h2x)


def _experts_kernel(elo_ref, ehi_ref, live_ref, xs_ref, wg_lo, wg_hi, wu_lo, wu_hi, wd_lo, wd_hi, ys_ref):
    del elo_ref, ehi_ref
    d = ys_ref.shape[1]
    j = pl.program_id(0)

    @pl.when(live_ref[j] == 1)
    def _():
        h = xs_ref[:, :d].astype(BF16)

        def expert(wg, wu, wd, weight):
            gate = _dot(h, wg[...])
            hidden = ((gate * jax.nn.sigmoid(gate)) * _dot(h, wu[...])) * weight
            return _dot(hidden.astype(BF16), wd[...])

        ys_ref[...] = (expert(wg_lo, wu_lo, wd_lo, xs_ref[:, d:d + 1])
                       + expert(wg_hi, wu_hi, wd_hi, xs_ref[:, d + 1:d + 2]))

    @pl.when(live_ref[j] == 0)
    def _():
        ys_ref[...] = jnp.zeros(ys_ref.shape, F32)


def _experts(xs, e_lo, e_hi, live, wg, wu, wd, layer):
    n_rows, w = xs.shape
    _, ne, d, f = wg.shape
    tm = MOE_ROW_TILE
    lo = lambda j, elo, ehi, lv: (layer, elo[j], 0, 0)
    hi = lambda j, elo, ehi, lv: (layer, ehi[j], 0, 0)
    grid_spec = pltpu.PrefetchScalarGridSpec(
        num_scalar_prefetch=3,
        grid=(n_rows // tm,),
        in_specs=[pl.BlockSpec((tm, w), lambda j, elo, ehi, lv: (j, 0)),
                  pl.BlockSpec((None, None, d, f), lo), pl.BlockSpec((None, None, d, f), hi),
                  pl.BlockSpec((None, None, d, f), lo), pl.BlockSpec((None, None, d, f), hi),
                  pl.BlockSpec((None, None, f, d), lo), pl.BlockSpec((None, None, f, d), hi)],
        out_specs=pl.BlockSpec((tm, d), lambda j, elo, ehi, lv: (j, 0)),
    )
    return pl.pallas_call(
        _experts_kernel,
        grid_spec=grid_spec,
        out_shape=jax.ShapeDtypeStruct((n_rows, d), F32),
        compiler_params=pltpu.CompilerParams(
            dimension_semantics=("arbitrary",), vmem_limit_bytes=VMEM_LIMIT),
        name="experts",
    )(e_lo, e_hi, live, xs, wg, wg, wu, wu, wd, wd)


def _combine_kernel(pos_ref, next_ref, ys_ref, x1_ref, mod_ref, o_ref, buf, sem):
    tm = buf.shape[1]
    step = pl.program_id(0) * pl.num_programs(1) + pl.program_id(1)
    last = pl.num_programs(0) * pl.num_programs(1) - 1

    def issue(p_ref, slot):
        for r in range(tm):
            pltpu.make_async_copy(ys_ref.at[pl.ds(p_ref[slot, r], 1), :], buf.at[slot, pl.ds(r, 1), :],
                                  sem.at[slot]).start(priority=r % 2)

    def finish(slot):
        pltpu.make_async_copy(ys_ref.at[pl.ds(0, tm), :], buf.at[slot], sem.at[slot]).wait()
        rows = pl.ds(slot * tm, tm)
        o_ref[rows, :] = x1_ref[rows, :] + mod_ref[5:6, :] * buf[slot]

    @pl.when(step == 0)
    def _():
        issue(pos_ref, 0)

    issue(pos_ref, 1)
    finish(0)

    @pl.when(step < last)
    def _():
        issue(next_ref, 0)

    finish(1)


def _combine(ys, pos, x1, mod_l):
    b, s, d = x1.shape
    tm = MOE_DMA_TILE
    assert s % (2 * tm) == 0
    nt = s // (2 * tm)
    n_steps = b * nt
    pos2 = pos.reshape(n_steps, 2, tm)
    this_step = pl.BlockSpec((None, 2, tm), lambda bb, i: (bb * nt + i, 0, 0), memory_space=pltpu.SMEM)
    next_step = pl.BlockSpec((None, 2, tm), lambda bb, i: (jnp.minimum(bb * nt + i + 1, n_steps - 1), 0, 0),
                             memory_space=pltpu.SMEM)
    return pl.pallas_call(
        _combine_kernel,
        grid=(b, nt),
        in_specs=[this_step, next_step,
                  pl.BlockSpec(memory_space=pl.ANY),
                  pl.BlockSpec((None, 2 * tm, d), lambda bb, i: (bb, i, 0)),
                  pl.BlockSpec((None, N_MOD, d), lambda bb, i: (bb, 0, 0))],
        out_specs=pl.BlockSpec((None, 2 * tm, d), lambda bb, i: (bb, i, 0)),
        out_shape=jax.ShapeDtypeStruct((b, s, d), F32),
        scratch_shapes=[pltpu.VMEM((2, tm, d), F32), pltpu.SemaphoreType.DMA((2,))],
        compiler_params=pltpu.CompilerParams(
            dimension_semantics=("arbitrary", "arbitrary"), vmem_limit_bytes=VMEM_LIMIT),
        name="combine",
    )(pos2, pos2, ys, x1, mod_l)


def _moe(h2x, info, counts, wg, wu, wd, layer, x1, mod_l):
    b, s, d = x1.shape
    t = b * s
    tm = MOE_ROW_TILE
    n_tiles = t // tm + N_CLASSES
    counts = counts[:N_CLASSES, 0]
    padded = (counts + tm - 1) // tm * tm
    ends = jnp.cumsum(padded)
    offs = jnp.pad(ends - padded, (0, N_CLASS_PAD - N_CLASSES)).astype(I32)
    tile_cls = jnp.sum(jnp.arange(n_tiles, dtype=I32)[:, None] * tm >= ends[None, :], axis=1)
    live = (tile_cls < N_CLASSES).astype(I32)
    tile_cls = jnp.minimum(tile_cls, N_CLASSES - 1)
    group, pair = tile_cls // N_PAIRS, tile_cls % N_PAIRS
    pairs = jnp.asarray(_PAIRS, I32)
    e_lo = group * EXPERTS_PER_GROUP + pairs[pair, 0]
    e_hi = group * EXPERTS_PER_GROUP + pairs[pair, 1]

    pos = _positions(info, offs).reshape(t // MOE_DMA_TILE, 1, MOE_DMA_TILE)
    xs = _dispatch(h2x.reshape(t, d + LANES), pos, ends.astype(I32), n_tiles * tm)
    ys = _experts(xs, e_lo, e_hi, live, wg, wu, wd, layer)
    return _combine(ys, pos, x1, mod_l)


def kernel(x, c, w_ada, b_ada, norm1_g, w_in, q_norm_g, k_norm_g, rel_bias, w_o_a, w_o_b, w_out,
           norm2_g, router_w, router_b, w_gate, w_up, w_down):
    b, s, d = x.shape
    depth = w_ada.shape[0]
    assert d % LANES == 0
    tm = min(TOKEN_TILE, s)

    mod = _adaln(c, w_ada, b_ada)
    bias = _bias_tiles(rel_bias, DSA_QUERY_BLOCK, DSA_KEY_BLOCK)
    rw = jnp.pad(router_w, ((0, 0), (0, LANES - N_EXPERTS)))
    rwh = rw.astype(BF16)
    rwl = (rw - rwh.astype(F32)).astype(BF16)
    rb = jnp.pad(router_b, (0, LANES - N_EXPERTS)).reshape(1, LANES)
    wg, wu, wd = w_gate.astype(BF16), w_up.astype(BF16), w_down.astype(BF16)

    for l in range(depth):
        qa, ka, va, qbt, qit, kb, vt, ki, wit, sga, sgb = _in_proj(
            x, mod[l], norm1_g[l].reshape(1, d), _pack_w_in(w_in[l]),
            q_norm_g[l].reshape(HEAD_DIM, 1), k_norm_g[l].reshape(1, HEAD_DIM), tm)
        oa = _sb_attention(qa, ka, va)
        ob = _dsa_attention(qit, ki, wit, qbt, kb, vt, bias)
        x1, h2x, info, counts = _merge(oa, ob, sga, sgb, x, mod[l], w_o_a[l].astype(BF16),
                                       w_o_b[l].astype(BF16), w_out[l].astype(BF16),
                                       norm2_g[l].reshape(1, d), rwh, rwl, rb, tm)
        x = _moe(h2x, info, counts, wg, wu, wd, l, x1, mod[l])
    return x
```

```python
import functools
import math

import numpy as np
import jax
import jax.numpy as jnp
from jax import lax
from jax.experimental import pallas as pl
from jax.experimental.pallas import tpu as pltpu

F32 = jnp.float32
BF16 = jnp.bfloat16
I32 = jnp.int32
I16 = jnp.int16

HEAD_DIM = 64
N_HEADS = 8
N_EXPERTS = 16
N_GROUPS = 4
EXPERTS_PER_GROUP = 4
_PAIRS = ((0, 1), (0, 2), (0, 3), (1, 2), (1, 3), (2, 3))
N_PAIRS = len(_PAIRS)
N_CLASSES = N_GROUPS * N_PAIRS
N_CLASS_PAD = 32
MOE_ROW_TILE = 256
MOE_DMA_TILE = 512
N_MOD = 6
EPS = 1e-6
INDEX_TOPK_MAX = 256
REL_BUCKETS = 32
REL_MAX_DIST = 128
LANES = 128
SUBLANES = 8
PACKED_SUBLANES = 16
HALF_BITS = 16
TOKEN_TILE = 512
EXP2_MAX = 126.0
DSA_QUERY_BLOCK = 256
DSA_KEY_BLOCK = 128
SB_QUERY_BLOCK = 256
SB_KEY_BLOCK = 256
SB_DEAD_LOG2 = 150.0
LOG2E = 1.4426950408889634
NEG_BIG = -1e30
INT_MIN = -2 ** 31
INT_MAX = 2 ** 31 - 1
HALF_BIAS = 2 ** 15

VMEM_LIMIT = 56 * 1024 * 1024


def _nt_dot(a, b):
    return lax.dot_general(a, b, (((1,), (1,)), ((), ())), preferred_element_type=F32)


def _dot(a, b):
    return jnp.dot(a, b, preferred_element_type=F32)


def _mod_kernel(c_ref, w_ref, b_ref, o_ref):
    c = c_ref[...]
    sc = c * jax.nn.sigmoid(c)
    o_ref[...] = _dot(sc.astype(BF16), w_ref[...].astype(BF16)) + b_ref[...]


def _adaln(c, w_ada, b_ada):
    depth, d, nd = w_ada.shape
    b = c.shape[0]
    out = pl.pallas_call(
        _mod_kernel,
        grid=(depth, nd // d),
        in_specs=[pl.BlockSpec((b, d), lambda l, j: (0, 0)),
                  pl.BlockSpec((None, d, d), lambda l, j: (l, 0, j)),
                  pl.BlockSpec((None, 1, d), lambda l, j: (l, 0, j))],
        out_specs=pl.BlockSpec((None, b, d), lambda l, j: (l, 0, j)),
        out_shape=jax.ShapeDtypeStruct((depth, b, nd), F32),
        compiler_params=pltpu.CompilerParams(vmem_limit_bytes=VMEM_LIMIT),
        name="adaln",
    )(c, w_ada, b_ada.reshape(depth, 1, nd))
    return out.reshape(depth, b, N_MOD, d)


def _rms(x):
    return x * lax.rsqrt(jnp.mean(x * x, axis=-1, keepdims=True) + EPS)


def _in_kernel(x_ref, mod_ref, g1_ref, w_ref, qng_ref, kng_ref,
               qa_ref, ka_ref, va_ref, qbt_ref, qit_ref, kb_ref, vt_ref, ki_ref, wit_ref,
               sga_ref, sgb_ref):
    d = x_ref.shape[-1]
    tm = x_ref.shape[0]
    hw = N_HEADS * HEAD_DIM
    x = x_ref[...]
    h = (_rms(x) * g1_ref[...]) * (1.0 + mod_ref[1:2, :]) + mod_ref[0:1, :]
    hb = h.astype(BF16)

    def proj(lo, n):
        return _dot(hb, w_ref[:, lo:lo + n])

    def heads(r, ref):
        for hh in range(N_HEADS):
            ref[hh] = r[:, hh * HEAD_DIM:(hh + 1) * HEAD_DIM].astype(ref.dtype)

    scale = HEAD_DIM ** -0.5
    heads(proj(0, hw) * (scale * LOG2E), qa_ref)
    heads(proj(hw, hw), ka_ref)
    heads(proj(2 * hw, hw), va_ref)

    rt = proj(3 * hw, hw).T
    for hh in range(N_HEADS):
        slab = rt[hh * HEAD_DIM:(hh + 1) * HEAD_DIM, :]
        inv = lax.rsqrt(jnp.mean(slab * slab, axis=0, keepdims=True) + EPS)
        qbt_ref[hh] = (((slab * inv) * qng_ref[...]) * scale).astype(BF16)

    r = proj(4 * hw, 2 * HEAD_DIM)
    kb_ref[...] = (_rms(r[:, :HEAD_DIM]) * kng_ref[...]).astype(BF16)
    vt = r.T[HEAD_DIM:, :]
    ones_row = (lax.broadcasted_iota(I32, (HEAD_DIM, LANES), 0) == 0).astype(BF16)
    for cc in range(tm // LANES):
        vt_ref[cc, :HEAD_DIM, :] = vt[:, cc * LANES:(cc + 1) * LANES].astype(BF16)
        vt_ref[cc, HEAD_DIM:, :] = ones_row

    rt = proj(4 * hw + LANES, hw).T
    for hh in range(N_HEADS):
        qit_ref[hh] = rt[hh * HEAD_DIM:(hh + 1) * HEAD_DIM, :].astype(BF16)

    r = proj(5 * hw + LANES, LANES)
    ki_ref[...] = r[:, :HEAD_DIM].astype(BF16)
    wit_ref[...] = r.T[HEAD_DIM:HEAD_DIM + N_HEADS, :]

    off = 5 * hw + 2 * LANES
    sga_ref[...] = jax.nn.sigmoid(proj(off, d)).astype(BF16)
    sgb_ref[...] = jax.nn.sigmoid(proj(off + d, d)).astype(BF16)


def _pack_w_in(w_in_l):
    hw = N_HEADS * HEAD_DIM
    a = 4 * hw + 2 * HEAD_DIM + hw
    small = w_in_l[:, a:a + HEAD_DIM + N_HEADS]
    small = jnp.pad(small, ((0, 0), (0, LANES - small.shape[1])))
    gates = w_in_l[:, a + HEAD_DIM + N_HEADS:]
    return jnp.concatenate([w_in_l[:, :a], small, gates], axis=1).astype(BF16)


def _in_proj(x, mod_l, g1, w_packed, qng, kng, tm):
    b, s, d = x.shape
    hm = jax.ShapeDtypeStruct((b, N_HEADS, s, HEAD_DIM), BF16)
    hmt = jax.ShapeDtypeStruct((b, N_HEADS, HEAD_DIM, s), BF16)
    tok64 = jax.ShapeDtypeStruct((b, s, HEAD_DIM), BF16)
    hm_spec = pl.BlockSpec((None, N_HEADS, tm, HEAD_DIM), lambda bb, i: (bb, 0, i, 0))
    hmt_spec = pl.BlockSpec((None, N_HEADS, HEAD_DIM, tm), lambda bb, i: (bb, 0, 0, i))
    tok = lambda n: pl.BlockSpec((None, tm, n), lambda bb, i: (bb, i, 0))
    nw = w_packed.shape[1]
    return pl.pallas_call(
        _in_kernel,
        grid=(b, s // tm),
        in_specs=[tok(d),
                  pl.BlockSpec((None, N_MOD, d), lambda bb, i: (bb, 0, 0)),
                  pl.BlockSpec((1, d), lambda bb, i: (0, 0)),
                  pl.BlockSpec((d, nw), lambda bb, i: (0, 0)),
                  pl.BlockSpec((HEAD_DIM, 1), lambda bb, i: (0, 0)),
                  pl.BlockSpec((1, HEAD_DIM), lambda bb, i: (0, 0))],
        out_specs=[hm_spec] * 3 + [hmt_spec] * 2
                  + [tok(HEAD_DIM),
                     pl.BlockSpec((None, tm // LANES, 2 * HEAD_DIM, LANES), lambda bb, i: (bb, i, 0, 0)),
                     tok(HEAD_DIM),
                     pl.BlockSpec((None, N_HEADS, tm), lambda bb, i: (bb, 0, i)),
                     tok(d), tok(d)],
        out_shape=[hm] * 3 + [hmt] * 2
                  + [tok64,
                     jax.ShapeDtypeStruct((b, s // LANES, 2 * HEAD_DIM, LANES), BF16),
                     tok64,
                     jax.ShapeDtypeStruct((b, N_HEADS, s), F32),
                     jax.ShapeDtypeStruct((b, s, d), BF16),
                     jax.ShapeDtypeStruct((b, s, d), BF16)],
        compiler_params=pltpu.CompilerParams(
            dimension_semantics=("parallel", "parallel"), vmem_limit_bytes=VMEM_LIMIT),
        name="in_proj",
    )(x, mod_l, g1, w_packed, qng, kng)


def _sb_kernel(q_ref, k_ref, v_ref, o_ref, acc_scr, csum_scr, sp_scr, ls_scr):
    tq = q_ref.shape[1]
    tk = SB_KEY_BLOCK
    i = pl.program_id(1)
    nfull = (i * tq) // tk
    row = lax.broadcasted_iota(I32, (tq, tk), 0)
    col = lax.broadcasted_iota(I32, (tq, tk), 1)
    before = (col - row) < (i * tq - nfull * tk)
    krow = lax.broadcasted_iota(I32, (tk, tk), 0)
    kcol = lax.broadcasted_iota(I32, (tk, tk), 1)
    suffix = (krow > kcol).astype(BF16)

    def stage_a(hh, st, first):
        z2 = _nt_dot(q_ref[hh], k_ref[hh, pl.ds(st, tk), :])
        sp2 = jnp.maximum(jnp.log2(1.0 + jnp.exp2(jnp.minimum(z2, EXP2_MAX))), z2)
        if first:
            sp2 = jnp.where(before, sp2, 0.0)
        sp_scr[hh] = sp2.astype(BF16)
        ls_scr[hh] = z2 - sp2

    def stage_b(hh, st, first):
        sp = sp_scr[hh]
        suf = _dot(sp, suffix)
        if first:
            a = jnp.where(before, jnp.exp2(ls_scr[hh] - suf), 0.0)
            csum = jnp.zeros((tq, 1), F32)
        else:
            csum = csum_scr[hh]
            a = jnp.exp2((ls_scr[hh] - suf) - csum)
        c = _dot(a.astype(BF16), v_ref[hh, pl.ds(st, tk), :])
        if first:
            acc_scr[hh] = c
        else:
            acc_scr[hh] += c
        csum_scr[hh] = csum + (suf[:, 0:1] + sp[:, 0:1].astype(F32))

    def key_block(st, first):
        for hh in range(N_HEADS):
            stage_a(hh, st, first)
        for hh in range(N_HEADS):
            stage_b(hh, st, first)

    key_block(pl.multiple_of(nfull * tk, tk), True)

    def live():
        return jnp.min(csum_scr[...]) < SB_DEAD_LOG2

    def body(carry):
        jj, _ = carry
        key_block(pl.multiple_of((nfull - 1 - jj) * tk, tk), False)
        return jj + 1, live()

    lax.while_loop(lambda carry: (carry[0] < nfull) & carry[1], body, (jnp.int32(0), live()))
    for hh in range(N_HEADS):
        o_ref[:, hh * HEAD_DIM:(hh + 1) * HEAD_DIM] = acc_scr[hh].astype(o_ref.dtype)


def _sb_attention(qa, ka, va):
    b, nh, s, dh = qa.shape
    tq = SB_QUERY_BLOCK
    assert SB_KEY_BLOCK % tq == 0 and s % SB_KEY_BLOCK == 0
    return pl.pallas_call(
        _sb_kernel,
        grid=(b, s // tq),
        in_specs=[pl.BlockSpec((None, nh, tq, dh), lambda bb, i: (bb, 0, i, 0)),
                  pl.BlockSpec((None, nh, s, dh), lambda bb, i: (bb, 0, 0, 0)),
                  pl.BlockSpec((None, nh, s, dh), lambda bb, i: (bb, 0, 0, 0))],
        out_specs=pl.BlockSpec((None, tq, nh * dh), lambda bb, i: (bb, i, 0)),
        out_shape=jax.ShapeDtypeStruct((b, s, nh * dh), BF16),
        scratch_shapes=[pltpu.VMEM((nh, tq, dh), F32), pltpu.VMEM((nh, tq, 1), F32),
                        pltpu.VMEM((nh, tq, SB_KEY_BLOCK), BF16), pltpu.VMEM((nh, tq, SB_KEY_BLOCK), F32)],
        compiler_params=pltpu.CompilerParams(
            dimension_semantics=("parallel", "parallel"), vmem_limit_bytes=VMEM_LIMIT),
        name="stick_breaking",
    )(qa, ka, va)


def _t5_bucket_np(dist):
    max_exact = REL_BUCKETS // 2
    d_f = np.maximum(dist, 1).astype(np.float32)
    large = max_exact + (np.log(d_f / np.float32(max_exact))
                         / np.float32(math.log(REL_MAX_DIST / max_exact))
                         * np.float32(REL_BUCKETS - max_exact)).astype(np.int32)
    large = np.minimum(large, REL_BUCKETS - 1)
    return np.where(dist < max_exact, dist, large).astype(np.int32)


def _bias_tiles(rel_bias, tq, tk):
    n_near = tq // tk + 1
    key = np.arange(tk)[:, None]
    query = np.arange(tq)[None, :]
    bk = np.stack([_t5_bucket_np(np.maximum(query - key - (n - 1) * tk, 0)) for n in range(n_near)])
    far = int(_t5_bucket_np(np.array([REL_MAX_DIST]))[0])
    nh = rel_bias.shape[1]

    def body(rb_ref, bk_ref, o_ref):
        hh = pl.program_id(1)
        buckets = bk_ref[...]
        tile = jnp.zeros((tk, tq), F32)
        for bucket in range(REL_BUCKETS):
            tile = jnp.where(buckets == bucket, rb_ref[hh, bucket] - rb_ref[hh, far], tile)
        o_ref[...] = tile

    return pl.pallas_call(
        body,
        grid=(n_near, nh),
        in_specs=[pl.BlockSpec(memory_space=pltpu.SMEM),
                  pl.BlockSpec((None, tk, tq), lambda n, hh: (n, 0, 0))],
        out_specs=pl.BlockSpec((None, None, tk, tq), lambda n, hh: (n, hh, 0, 0)),
        out_shape=jax.ShapeDtypeStruct((n_near, nh, tk, tq), F32),
        name="bias_tiles",
    )(rel_bias.T, jnp.asarray(bk))


def _sort_key(x):
    bits = pltpu.bitcast(x, I32)
    return bits ^ ((bits >> 31) & INT_MAX)


def _fold_rows(x, op):
    return functools.reduce(
        op, [x[g * SUBLANES:(g + 1) * SUBLANES, :] for g in range(x.shape[0] // SUBLANES)])


def _dsa_kernel(ktop, qit_ref, ki_ref, wit_ref, qbt_ref, kb_ref, vt_ref, bias_ref, o_ref,
                key_scr, hi_scr, lo_scr, m_scr, acc_scr, s_scr):
    tq = qit_ref.shape[2]
    tk = DSA_KEY_BLOCK
    per_q = tq // tk
    i = pl.program_id(1)
    first_own = per_q * i
    nchunks = per_q * (i + 1)
    key_i = lax.broadcasted_iota(I32, (tk, tq), 0)
    qry_i = lax.broadcasted_iota(I32, (tk, tq), 1)

    def causal(rel):
        return (key_i - qry_i) <= (-rel * tk)

    def for_chunks(fn):
        def far(j, carry):
            for u in range(per_q):
                fn(j * per_q + u, None)
            return carry
        lax.fori_loop(0, i - 1, far, 0)

        @pl.when(i >= 1)
        def _():
            for rel in range(-per_q, -1):
                fn(first_own + rel, None)
            fn(first_own - 1, -1)

        for rel in range(per_q):
            fn(first_own + rel, rel)

    def score_chunk(c, rel):
        kic = ki_ref[pl.ds(pl.multiple_of(c * tk, tk), tk), :]
        score = jnp.zeros((tk, tq), F32)
        for hh in range(N_HEADS):
            score = score + wit_ref[hh:hh + 1, :] * jnp.maximum(_dot(kic, qit_ref[hh]), 0.0)
        keys = _sort_key(score)
        if rel is not None and rel >= 0:
            keys = jnp.where(causal(rel), keys, INT_MIN)
        key_scr[c] = keys
        hi_scr[c] = (keys >> HALF_BITS).astype(I16)
        lo_scr[c] = ((keys & (2 * HALF_BIAS - 1)) - HALF_BIAS).astype(I16)

    for_chunks(score_chunk)

    def count(pred):
        def body(j, part):
            for u in range(per_q):
                part = part + _fold_rows(jnp.where(pred(key_scr[j * per_q + u]), 1, 0), jnp.add)
            return part
        part = lax.fori_loop(0, i + 1, body, jnp.zeros((SUBLANES, tq), I32))
        return jnp.sum(part, axis=0, keepdims=True)

    def count16(scr, pred):
        one, zero = jnp.int16(1), jnp.int16(0)

        def block(j, part):
            for u in range(per_q):
                hit = jnp.where(pred(scr[j * per_q + u]), one, zero)
                part = part + functools.reduce(
                    jnp.add, [hit[g * PACKED_SUBLANES:(g + 1) * PACKED_SUBLANES, :]
                              for g in range(tk // PACKED_SUBLANES)])
            return part

        nblk = i + 1
        part = lax.fori_loop(0, nblk // 2, lambda jj, part: block(2 * jj + 1, block(2 * jj, part)),
                             jnp.zeros((PACKED_SUBLANES, tq), I16))
        part = lax.cond(nblk % 2 == 1, lambda p: block(nblk - 1, p), lambda p: p, part)
        return jnp.sum(part.astype(I32), axis=0, keepdims=True)

    def search16(scr, base):
        def bit_body(bi, t_u):
            cand_u = t_u | jnp.left_shift(jnp.int32(1), HALF_BITS - 1 - bi)
            cand = (cand_u - HALF_BIAS).astype(I16)
            return jnp.where(base + count16(scr, lambda k: k >= cand) >= ktop, cand_u, t_u)
        return lax.fori_loop(0, HALF_BITS, bit_body, jnp.zeros((1, tq), I32))

    t_hi = search16(hi_scr, 0) - HALF_BIAS
    t_hi16 = t_hi.astype(I16)
    above = count16(hi_scr, lambda k: k > t_hi16)

    def keep_band(j, carry):
        for u in range(per_q):
            c = j * per_q + u
            lo_scr[c] = jnp.where(hi_scr[c] == t_hi16, lo_scr[c], jnp.int16(-HALF_BIAS))
        return carry

    lax.fori_loop(0, i + 1, keep_band, 0)
    thr = t_hi * (2 * HALF_BIAS) + search16(lo_scr, above)
    n_ge = count(lambda k: k >= thr)

    @pl.when(jnp.max(n_ge) > ktop)
    def _():
        need = (ktop - count(lambda k: k > thr)).astype(F32)
        lower = (lax.broadcasted_iota(I32, (tk, tk), 1)
                 < lax.broadcasted_iota(I32, (tk, tk), 0)).astype(BF16)

        def body(c, seen):
            k = key_scr[c]
            eq = k == thr
            eqf = eq.astype(BF16)
            rank = seen + _dot(lower, eqf)
            drop = eq & (rank >= need) & (n_ge > ktop)
            key_scr[c] = jnp.where(drop, thr - 1, k)
            return seen + jnp.sum(eqf.astype(F32), axis=0, keepdims=True)

        lax.fori_loop(0, nchunks, body, jnp.zeros((1, tq), F32))

    def logits(c, rel):
        sel = key_scr[c] >= thr
        if rel is not None and rel >= 0:
            sel = sel & causal(rel)
        kbc = kb_ref[pl.ds(pl.multiple_of(c * tk, tk), tk), :]

        def head(hh):
            s = _dot(kbc, qbt_ref[hh])
            if rel is not None:
                s = s + bias_ref[rel + 1, hh]
            return s
        return sel, head

    m_scr[...] = jnp.full(m_scr.shape, NEG_BIG, F32)

    def max_chunk(c, rel):
        sel, head = logits(c, rel)
        for hh in range(N_HEADS):
            s = jnp.where(sel, head(hh), NEG_BIG)
            s_scr[hh, c] = s
            m_scr[hh] = jnp.maximum(m_scr[hh], _fold_rows(s, jnp.maximum))

    for_chunks(max_chunk)
    m = [jnp.max(m_scr[hh], axis=0, keepdims=True) for hh in range(N_HEADS)]

    acc_scr[...] = jnp.zeros(acc_scr.shape, F32)

    def att_block(j, carry):
        for u in range(per_q):
            c = j * per_q + u
            vtc = vt_ref[c]
            for hh in range(N_HEADS):
                p = jnp.exp(s_scr[hh, c] - m[hh])
                acc_scr[hh] += _dot(vtc, p.astype(BF16))
        return carry

    lax.fori_loop(0, i + 1, att_block, 0)

    out_t = jnp.concatenate(
        [acc_scr[hh, :HEAD_DIM, :] / acc_scr[hh, HEAD_DIM:HEAD_DIM + 1, :] for hh in range(N_HEADS)],
        axis=0)
    o_ref[...] = out_t.T.astype(o_ref.dtype)


def _dsa_attention(qit, ki, wit, qbt, kb, vt, bias):
    b, nh, dh, s = qit.shape
    tq = DSA_QUERY_BLOCK
    tk = DSA_KEY_BLOCK
    assert tk == REL_MAX_DIST and tk == LANES and tq % tk == 0 and s % tq == 0
    ktop = min(INDEX_TOPK_MAX, s // 4)
    hmt_q = pl.BlockSpec((None, nh, dh, tq), lambda bb, i: (bb, 0, 0, i))
    full64 = pl.BlockSpec((None, s, dh), lambda bb, i: (bb, 0, 0))
    return pl.pallas_call(
        functools.partial(_dsa_kernel, ktop),
        grid=(b, s // tq),
        in_specs=[hmt_q, full64,
                  pl.BlockSpec((None, nh, tq), lambda bb, i: (bb, 0, i)),
                  hmt_q, full64,
                  pl.BlockSpec((None, s // tk, 2 * dh, tk), lambda bb, i: (bb, 0, 0, 0)),
                  pl.BlockSpec(bias.shape, lambda bb, i: (0, 0, 0, 0))],
        out_specs=pl.BlockSpec((None, tq, nh * dh), lambda bb, i: (bb, i, 0)),
        out_shape=jax.ShapeDtypeStruct((b, s, nh * dh), BF16),
        scratch_shapes=[pltpu.VMEM((s // tk, tk, tq), I32),
                        pltpu.VMEM((s // tk, tk, tq), I16),
                        pltpu.VMEM((s // tk, tk, tq), I16),
                        pltpu.VMEM((nh, SUBLANES, tq), F32),
                        pltpu.VMEM((nh, 2 * dh, tq), F32),
                        pltpu.VMEM((nh, s // tk, tk, tq), F32)],
        compiler_params=pltpu.CompilerParams(
            dimension_semantics=("parallel", "parallel"), vmem_limit_bytes=VMEM_LIMIT),
        name="sparse_attention",
    )(qit, ki, wit, qbt, kb, vt, bias)


def _first_max4(a):
    m1 = jnp.maximum(jnp.maximum(a[0], a[1]), jnp.maximum(a[2], a[3]))
    i1 = jnp.where(a[0] == m1, 0, jnp.where(a[1] == m1, 1, jnp.where(a[2] == m1, 2, 3)))
    rest = [jnp.where(i1 == j, -1.0, a[j]) for j in range(4)]
    m2 = jnp.maximum(jnp.maximum(rest[0], rest[1]), jnp.maximum(rest[2], rest[3]))
    i2 = jnp.where(rest[0] == m2, 0, jnp.where(rest[1] == m2, 1, jnp.where(rest[2] == m2, 2, 3)))
    return m1, i1, m2, i2


def _route(lt):
    lg = [lt[e:e + 1, :] for e in range(N_EXPERTS)]
    mx = functools.reduce(jnp.maximum, lg)
    ex = [jnp.exp(v - mx) for v in lg]
    tot = functools.reduce(lambda u, v: u + v, ex)
    p = [v / tot for v in ex]
    gs = []
    for g in range(N_GROUPS):
        m1, _, m2, _ = _first_max4(p[4 * g:4 * g + 4])
        gs.append(m1 + m2)
    _, gbest, _, _ = _first_max4(gs)
    chosen = [jnp.where(gbest == 0, p[j], jnp.where(gbest == 1, p[4 + j],
              jnp.where(gbest == 2, p[8 + j], p[12 + j]))) for j in range(4)]
    m1, i1, m2, i2 = _first_max4(chosen)
    den = m1 + m2
    lo = jnp.minimum(i1, i2)
    hi = jnp.maximum(i1, i2)
    pair = jnp.where(lo == 0, hi - 1, jnp.where(lo == 1, hi + 1, N_PAIRS - 1))
    first_is_lo = i1 < i2
    w1 = m1 / den
    w2 = m2 / den
    return gbest * N_PAIRS + pair, jnp.where(first_is_lo, w1, w2), jnp.where(first_is_lo, w2, w1)


def _merge_kernel(oa_ref, ob_ref, sga_ref, sgb_ref, x_ref, mod_ref, woa_ref, wob_ref, wout_ref,
                  g2_ref, rwh_ref, rwl_ref, rb_ref, x1_ref, h2x_ref, info_ref, cnt_ref, carry_scr):
    tm, d = x_ref.shape

    @pl.when((pl.program_id(0) == 0) & (pl.program_id(1) == 0))
    def _():
        carry_scr[...] = jnp.zeros(carry_scr.shape, F32)

    merged = (sga_ref[...].astype(F32) * _dot(oa_ref[...], woa_ref[...])
              + sgb_ref[...].astype(F32) * _dot(ob_ref[...], wob_ref[...]))
    x1 = x_ref[...] + mod_ref[2:3, :] * _dot(merged.astype(BF16), wout_ref[...])
    x1_ref[...] = x1
    h2 = (_rms(x1) * g2_ref[...]) * (1.0 + mod_ref[4:5, :]) + mod_ref[3:4, :]
    h2x_ref[:, :d] = h2
    hi = h2.astype(BF16)
    lo = (h2 - hi.astype(F32)).astype(BF16)
    logits = (_dot(hi, rwh_ref[...]) + _dot(lo, rwh_ref[...]) + _dot(hi, rwl_ref[...])) + rb_ref[...]
    cls, cw_lo, cw_hi = _route(logits.T)

    ncls = carry_scr.shape[0]
    onehot = (lax.broadcasted_iota(I32, (ncls, tm), 0) == cls).astype(F32)
    earlier = (lax.broadcasted_iota(I32, (tm, tm), 0)
               < lax.broadcasted_iota(I32, (tm, tm), 1)).astype(BF16)
    before = _dot(onehot.astype(BF16), earlier)
    carry = carry_scr[...]
    carry_t = jnp.concatenate([carry] * (tm // LANES), axis=1)
    rank = jnp.sum(onehot * (carry_t + before), axis=0, keepdims=True).astype(I32)
    carry = carry + jnp.sum(onehot, axis=1, keepdims=True)
    carry_scr[...] = carry
    cnt_ref[...] = carry.astype(I32)

    sub = lax.broadcasted_iota(I32, (SUBLANES, tm), 0)
    info_ref[...] = jnp.where(sub == 0, cls, jnp.where(sub == 1, rank, 0))
    sub = lax.broadcasted_iota(I32, (LANES, tm), 0)
    weights_t = jnp.where(sub == 0, cw_lo, jnp.where(sub == 1, cw_hi, 0.0))
    h2x_ref[:, d:] = weights_t.T


def _merge(oa, ob, sga, sgb, x, mod_l, woa, wob, wout, g2, rwh, rwl, rb, tm):
    b, s, d = x.shape
    hw = oa.shape[-1]
    nt = s // tm
    tok = lambda n: pl.BlockSpec((None, tm, n), lambda bb, i: (bb, i, 0))
    const = lambda r, c: pl.BlockSpec((r, c), lambda bb, i: (0, 0))
    return pl.pallas_call(
        _merge_kernel,
        grid=(b, nt),
        in_specs=[tok(hw), tok(hw), tok(d), tok(d), tok(d),
                  pl.BlockSpec((None, N_MOD, d), lambda bb, i: (bb, 0, 0)),
                  const(hw, d), const(hw, d), const(d, d), const(1, d),
                  const(d, LANES), const(d, LANES), const(1, LANES)],
        out_specs=[tok(d), tok(d + LANES),
                   pl.BlockSpec((SUBLANES, tm), lambda bb, i: (0, bb * nt + i)),
                   const(N_CLASS_PAD, LANES)],
        out_shape=[jax.ShapeDtypeStruct((b, s, d), F32),
                   jax.ShapeDtypeStruct((b, s, d + LANES), F32),
                   jax.ShapeDtypeStruct((SUBLANES, b * s), I32),
                   jax.ShapeDtypeStruct((N_CLASS_PAD, LANES), I32)],
        scratch_shapes=[pltpu.VMEM((N_CLASS_PAD, LANES), F32)],
        compiler_params=pltpu.CompilerParams(
            dimension_semantics=("arbitrary", "arbitrary"), vmem_limit_bytes=VMEM_LIMIT),
        name="merge_route",
    )(oa, ob, sga, sgb, x, mod_l, woa, wob, wout, g2, rwh, rwl, rb)


def _positions_kernel(offs_ref, info_ref, pos_ref):
    cls = info_ref[0:1, :]
    pos = info_ref[1:2, :]
    for c in range(N_CLASSES):
        pos = pos + jnp.where(cls == c, offs_ref[c], 0)
    pos_ref[...] = jnp.broadcast_to(pos, pos_ref.shape)


def _positions(info, offs):
    return pl.pallas_call(
        _positions_kernel,
        in_specs=[pl.BlockSpec(memory_space=pltpu.SMEM), pl.BlockSpec(info.shape, lambda: (0, 0))],
        out_specs=pl.BlockSpec(info.shape, lambda: (0, 0)),
        out_shape=jax.ShapeDtypeStruct(info.shape, I32),
        name="positions",
    )(offs, info)[0]


def _dispatch_kernel(ends_ref, pos_ref, src_ref, dst_ref, zero_scr, sem, zero_sem):
    tm = pos_ref.shape[1]
    row_tile = zero_scr.shape[0]

    @pl.when(pl.program_id(0) == 0)
    def _():
        zero_scr[...] = jnp.zeros(zero_scr.shape, F32)

        def zero_tail(c):
            start = pl.multiple_of(ends_ref[c] - row_tile, row_tile)
            return pltpu.make_async_copy(zero_scr, dst_ref.at[pl.ds(start, row_tile), :], zero_sem)

        def nonempty(c):
            return ends_ref[c] > (ends_ref[c - 1] if c else 0)

        for c in range(N_CLASSES):
            @pl.when(nonempty(c))
            def _(c=c):
                zero_tail(c).start()
        for c in range(N_CLASSES):
            @pl.when(nonempty(c))
            def _(c=c):
                zero_tail(c).wait()

        def unused_tile(j):
            start = pl.multiple_of(ends_ref[N_CLASSES - 1] + j * row_tile, row_tile)
            return pltpu.make_async_copy(zero_scr, dst_ref.at[pl.ds(start, row_tile), :], zero_sem)

        n_unused = (dst_ref.shape[0] - ends_ref[N_CLASSES - 1]) // row_tile
        lax.fori_loop(0, n_unused, lambda j, carry: (unused_tile(j).start(), carry)[1], 0)
        lax.fori_loop(0, n_unused, lambda j, carry: (unused_tile(j).wait(), carry)[1], 0)

    for r in range(tm):
        pltpu.make_async_copy(src_ref.at[pl.ds(r, 1), :], dst_ref.at[pl.ds(pos_ref[0, r], 1), :],
                              sem).start(priority=r % 2)
    pltpu.make_async_copy(src_ref, dst_ref.at[pl.ds(0, tm), :], sem).wait()


def _dispatch(h2x, pos, ends, n_rows):
    t, w = h2x.shape
    tm = MOE_DMA_TILE
    smem_row = pl.BlockSpec((None, 1, tm), lambda i: (i, 0, 0), memory_space=pltpu.SMEM)
    return pl.pallas_call(
        _dispatch_kernel,
        grid=(t // tm,),
        in_specs=[pl.BlockSpec(memory_space=pltpu.SMEM), smem_row, pl.BlockSpec((tm, w), lambda i: (i, 0))],
        out_specs=pl.BlockSpec(memory_space=pl.ANY),
        out_shape=jax.ShapeDtypeStruct((n_rows, w), F32),
        scratch_shapes=[pltpu.VMEM((MOE_ROW_TILE, w), F32), pltpu.SemaphoreType.DMA, pltpu.SemaphoreType.DMA],
        compiler_params=pltpu.CompilerParams(
            dimension_semantics=("arbitrary",), vmem_limit_bytes=VMEM_LIMIT),
        name="dispatch",
    )(ends, pos, h2x)


def _experts_kernel(elo_ref, ehi_ref, live_ref, xs_ref, wg_lo, wg_hi, wu_lo, wu_hi, wd_lo, wd_hi, ys_ref):
    del elo_ref, ehi_ref
    d = ys_ref.shape[1]
    j = pl.program_id(0)

    @pl.when(live_ref[j] == 1)
    def _():
        h = xs_ref[:, :d].astype(BF16)

        def expert(wg, wu, wd, weight):
            gate = _dot(h, wg[...])
            hidden = ((gate * jax.nn.sigmoid(gate)) * _dot(h, wu[...])) * weight
            return _dot(hidden.astype(BF16), wd[...])

        ys_ref[...] = (expert(wg_lo, wu_lo, wd_lo, xs_ref[:, d:d + 1])
                       + expert(wg_hi, wu_hi, wd_hi, xs_ref[:, d + 1:d + 2]))

    @pl.when(live_ref[j] == 0)
    def _():
        ys_ref[...] = jnp.zeros(ys_ref.shape, F32)


def _experts(xs, e_lo, e_hi, live, wg, wu, wd, layer):
    n_rows, w = xs.shape
    _, ne, d, f = wg.shape
    tm = MOE_ROW_TILE
    lo = lambda j, elo, ehi, lv: (layer, elo[j], 0, 0)
    hi = lambda j, elo, ehi, lv: (layer, ehi[j], 0, 0)
    grid_spec = pltpu.PrefetchScalarGridSpec(
        num_scalar_prefetch=3,
        grid=(n_rows // tm,),
        in_specs=[pl.BlockSpec((tm, w), lambda j, elo, ehi, lv: (j, 0)),
                  pl.BlockSpec((None, None, d, f), lo), pl.BlockSpec((None, None, d, f), hi),
                  pl.BlockSpec((None, None, d, f), lo), pl.BlockSpec((None, None, d, f), hi),
                  pl.BlockSpec((None, None, f, d), lo), pl.BlockSpec((None, None, f, d), hi)],
        out_specs=pl.BlockSpec((tm, d), lambda j, elo, ehi, lv: (j, 0)),
    )
    return pl.pallas_call(
        _experts_kernel,
        grid_spec=grid_spec,
        out_shape=jax.ShapeDtypeStruct((n_rows, d), F32),
        compiler_params=pltpu.CompilerParams(
            dimension_semantics=("arbitrary",), vmem_limit_bytes=VMEM_LIMIT),
        name="experts",
    )(e_lo, e_hi, live, xs, wg, wg, wu, wu, wd, wd)


def _combine_kernel(pos_ref, next_ref, ys_ref, x1_ref, mod_ref, o_ref, buf, sem):
    tm = buf.shape[1]
    step = pl.program_id(0) * pl.num_programs(1) + pl.program_id(1)
    last = pl.num_programs(0) * pl.num_programs(1) - 1

    def issue(p_ref, slot):
        for r in range(tm):
            pltpu.make_async_copy(ys_ref.at[pl.ds(p_ref[slot, r], 1), :], buf.at[slot, pl.ds(r, 1), :],
                                  sem.at[slot]).start(priority=r % 2)

    def finish(slot):
        pltpu.make_async_copy(ys_ref.at[pl.ds(0, tm), :], buf.at[slot], sem.at[slot]).wait()
        rows = pl.ds(slot * tm, tm)
        o_ref[rows, :] = x1_ref[rows, :] + mod_ref[5:6, :] * buf[slot]

    @pl.when(step == 0)
    def _():
        issue(pos_ref, 0)

    issue(pos_ref, 1)
    finish(0)

    @pl.when(step < last)
    def _():
        issue(next_ref, 0)

    finish(1)


def _combine(ys, pos, x1, mod_l):
    b, s, d = x1.shape
    tm = MOE_DMA_TILE
    assert s % (2 * tm) == 0
    nt = s // (2 * tm)
    n_steps = b * nt
    pos2 = pos.reshape(n_steps, 2, tm)
    this_step = pl.BlockSpec((None, 2, tm), lambda bb, i: (bb * nt + i, 0, 0), memory_space=pltpu.SMEM)
    next_step = pl.BlockSpec((None, 2, tm), lambda bb, i: (jnp.minimum(bb * nt + i + 1, n_steps - 1), 0, 0),
                             memory_space=pltpu.SMEM)
    return pl.pallas_call(
        _combine_kernel,
        grid=(b, nt),
        in_specs=[this_step, next_step,
                  pl.BlockSpec(memory_space=pl.ANY),
                  pl.BlockSpec((None, 2 * tm, d), lambda bb, i: (bb, i, 0)),
                  pl.BlockSpec((None, N_MOD, d), lambda bb, i: (bb, 0, 0))],
        out_specs=pl.BlockSpec((None, 2 * tm, d), lambda bb, i: (bb, i, 0)),
        out_shape=jax.ShapeDtypeStruct((b, s, d), F32),
        scratch_shapes=[pltpu.VMEM((2, tm, d), F32), pltpu.SemaphoreType.DMA((2,))],
        compiler_params=pltpu.CompilerParams(
            dimension_semantics=("arbitrary", "arbitrary"), vmem_limit_bytes=VMEM_LIMIT),
        name="combine",
    )(pos2, pos2, ys, x1, mod_l)


def _moe(h2x, info, counts, wg, wu, wd, layer, x1, mod_l):
    b, s, d = x1.shape
    t = b * s
    tm = MOE_ROW_TILE
    n_tiles = t // tm + N_CLASSES
    counts = counts[:N_CLASSES, 0]
    padded = (counts + tm - 1) // tm * tm
    ends = jnp.cumsum(padded)
    offs = jnp.pad(ends - padded, (0, N_CLASS_PAD - N_CLASSES)).astype(I32)
    tile_cls = jnp.sum(jnp.arange(n_tiles, dtype=I32)[:, None] * tm >= ends[None, :], axis=1)
    live = (tile_cls < N_CLASSES).astype(I32)
    tile_cls = jnp.minimum(tile_cls, N_CLASSES - 1)
    group, pair = tile_cls // N_PAIRS, tile_cls % N_PAIRS
    pairs = jnp.asarray(_PAIRS, I32)
    e_lo = group * EXPERTS_PER_GROUP + pairs[pair, 0]
    e_hi = group * EXPERTS_PER_GROUP + pairs[pair, 1]

    pos = _positions(info, offs).reshape(t // MOE_DMA_TILE, 1, MOE_DMA_TILE)
    xs = _dispatch(h2x.reshape(t, d + LANES), pos, ends.astype(I32), n_tiles * tm)
    ys = _experts(xs, e_lo, e_hi, live, wg, wu, wd, layer)
    return _combine(ys, pos, x1, mod_l)


def kernel(x, c, w_ada, b_ada, norm1_g, w_in, q_norm_g, k_norm_g, rel_bias, w_o_a, w_o_b, w_out,
           norm2_g, router_w, router_b, w_gate, w_up, w_down):
    b, s, d = x.shape
    depth = w_ada.shape[0]
    assert d % LANES == 0
    tm = min(TOKEN_TILE, s)

    mod = _adaln(c, w_ada, b_ada)
    bias = _bias_tiles(rel_bias, DSA_QUERY_BLOCK, DSA_KEY_BLOCK)
    rw = jnp.pad(router_w, ((0, 0), (0, LANES - N_EXPERTS)))
    rwh = rw.astype(BF16)
    rwl = (rw - rwh.astype(F32)).astype(BF16)
    rb = jnp.pad(router_b, (0, LANES - N_EXPERTS)).reshape(1, LANES)
    wg, wu, wd = w_gate.astype(BF16), w_up.astype(BF16), w_down.astype(BF16)

    for l in range(depth):
        qa, ka, va, qbt, qit, kb, vt, ki, wit, sga, sgb = _in_proj(
            x, mod[l], norm1_g[l].reshape(1, d), _pack_w_in(w_in[l]),
            q_norm_g[l].reshape(HEAD_DIM, 1), k_norm_g[l].reshape(1, HEAD_DIM), tm)
        oa = _sb_attention(qa, ka, va)
        ob = _dsa_attention(qit, ki, wit, qbt, kb, vt, bias)
        x1, h2x, info, counts = _merge(oa, ob, sga, sgb, x, mod[l], w_o_a[l].astype(BF16),
                                       w_o_b[l].astype(BF16), w_out[l].astype(BF16),
                                       norm2_g[l].reshape(1, d), rwh, rwl, rb, tm)
        x = _moe(h2x, info, counts, wg, wu, wd, l, x1, mod[l])
    return x
```

```python
import functools
import math

import numpy as np
import jax
import jax.numpy as jnp
from jax import lax
from jax.experimental import pallas as pl
from jax.experimental.pallas import tpu as pltpu

F32 = jnp.float32
BF16 = jnp.bfloat16
I32 = jnp.int32
I16 = jnp.int16

HEAD_DIM = 64
N_HEADS = 8
N_EXPERTS = 16
N_GROUPS = 4
EXPERTS_PER_GROUP = 4
_PAIRS = ((0, 1), (0, 2), (0, 3), (1, 2), (1, 3), (2, 3))
N_PAIRS = len(_PAIRS)
N_CLASSES = N_GROUPS * N_PAIRS
N_CLASS_PAD = 32
MOE_ROW_TILE = 256
MOE_DMA_TILE = 512
DISPATCH_TILE = 1024
N_MOD = 6
EPS = 1e-6
INDEX_TOPK_MAX = 256
REL_BUCKETS = 32
REL_MAX_DIST = 128
LANES = 128
SUBLANES = 8
PACKED_SUBLANES = 16
HALF_BITS = 16
TOKEN_TILE = 512
EXP2_MAX = 126.0
DSA_QUERY_BLOCK = 256
DSA_KEY_BLOCK = 128
SB_QUERY_BLOCK = 256
SB_KEY_BLOCK = 256
SB_DEAD_LOG2 = 150.0
LOG2E = 1.4426950408889634
NEG_BIG = -1e30
INT_MIN = -2 ** 31
INT_MAX = 2 ** 31 - 1
HALF_BIAS = 2 ** 15

VMEM_LIMIT = 56 * 1024 * 1024


def _nt_dot(a, b):
    return lax.dot_general(a, b, (((1,), (1,)), ((), ())), preferred_element_type=F32)


def _dot(a, b):
    return jnp.dot(a, b, preferred_element_type=F32)


def _mod_kernel(c_ref, w_ref, b_ref, o_ref):
    c = c_ref[...]
    sc = c * jax.nn.sigmoid(c)
    o_ref[...] = _dot(sc.astype(BF16), w_ref[...].astype(BF16)) + b_ref[...]


def _adaln(c, w_ada, b_ada):
    depth, d, nd = w_ada.shape
    b = c.shape[0]
    out = pl.pallas_call(
        _mod_kernel,
        grid=(depth, nd // d),
        in_specs=[pl.BlockSpec((b, d), lambda l, j: (0, 0)),
                  pl.BlockSpec((None, d, d), lambda l, j: (l, 0, j)),
                  pl.BlockSpec((None, 1, d), lambda l, j: (l, 0, j))],
        out_specs=pl.BlockSpec((None, b, d), lambda l, j: (l, 0, j)),
        out_shape=jax.ShapeDtypeStruct((depth, b, nd), F32),
        compiler_params=pltpu.CompilerParams(vmem_limit_bytes=VMEM_LIMIT),
        name="adaln",
    )(c, w_ada, b_ada.reshape(depth, 1, nd))
    return out.reshape(depth, b, N_MOD, d)


def _rms(x):
    return x * lax.rsqrt(jnp.mean(x * x, axis=-1, keepdims=True) + EPS)


def _in_kernel(x_ref, mod_ref, g1_ref, w_ref, qng_ref, kng_ref,
               qa_ref, ka_ref, va_ref, qbt_ref, qit_ref, kb_ref, vt_ref, ki_ref, wit_ref,
               sga_ref, sgb_ref):
    d = x_ref.shape[-1]
    tm = x_ref.shape[0]
    hw = N_HEADS * HEAD_DIM
    x = x_ref[...]
    h = (_rms(x) * g1_ref[...]) * (1.0 + mod_ref[1:2, :]) + mod_ref[0:1, :]
    hb = h.astype(BF16)

    def proj(lo, n):
        return _dot(hb, w_ref[:, lo:lo + n])

    def heads(r, ref):
        for hh in range(N_HEADS):
            ref[hh] = r[:, hh * HEAD_DIM:(hh + 1) * HEAD_DIM].astype(ref.dtype)

    scale = HEAD_DIM ** -0.5
    heads(proj(0, hw) * (scale * LOG2E), qa_ref)
    heads(proj(hw, hw), ka_ref)
    heads(proj(2 * hw, hw), va_ref)

    rt = proj(3 * hw, hw).T
    for hh in range(N_HEADS):
        slab = rt[hh * HEAD_DIM:(hh + 1) * HEAD_DIM, :]
        inv = lax.rsqrt(jnp.mean(slab * slab, axis=0, keepdims=True) + EPS)
        qbt_ref[hh] = (((slab * inv) * qng_ref[...]) * scale).astype(BF16)

    r = proj(4 * hw, 2 * HEAD_DIM)
    kb_ref[...] = (_rms(r[:, :HEAD_DIM]) * kng_ref[...]).astype(BF16)
    vt = r.T[HEAD_DIM:, :]
    ones_row = (lax.broadcasted_iota(I32, (HEAD_DIM, LANES), 0) == 0).astype(BF16)
    for cc in range(tm // LANES):
        vt_ref[cc, :HEAD_DIM, :] = vt[:, cc * LANES:(cc + 1) * LANES].astype(BF16)
        vt_ref[cc, HEAD_DIM:, :] = ones_row

    rt = proj(4 * hw + LANES, hw).T
    for hh in range(N_HEADS):
        qit_ref[hh] = rt[hh * HEAD_DIM:(hh + 1) * HEAD_DIM, :].astype(BF16)

    r = proj(5 * hw + LANES, LANES)
    ki_ref[...] = r[:, :HEAD_DIM].astype(BF16)
    wit_ref[...] = r.T[HEAD_DIM:HEAD_DIM + N_HEADS, :]

    off = 5 * hw + 2 * LANES
    sga_ref[...] = jax.nn.sigmoid(proj(off, d)).astype(BF16)
    sgb_ref[...] = jax.nn.sigmoid(proj(off + d, d)).astype(BF16)


def _pack_w_in(w_in_l):
    hw = N_HEADS * HEAD_DIM
    a = 4 * hw + 2 * HEAD_DIM + hw
    small = w_in_l[:, a:a + HEAD_DIM + N_HEADS]
    small = jnp.pad(small, ((0, 0), (0, LANES - small.shape[1])))
    gates = w_in_l[:, a + HEAD_DIM + N_HEADS:]
    return jnp.concatenate([w_in_l[:, :a], small, gates], axis=1).astype(BF16)


def _in_proj(x, mod_l, g1, w_packed, qng, kng, tm):
    b, s, d = x.shape
    hm = jax.ShapeDtypeStruct((b, N_HEADS, s, HEAD_DIM), BF16)
    hmt = jax.ShapeDtypeStruct((b, N_HEADS, HEAD_DIM, s), BF16)
    tok64 = jax.ShapeDtypeStruct((b, s, HEAD_DIM), BF16)
    hm_spec = pl.BlockSpec((None, N_HEADS, tm, HEAD_DIM), lambda bb, i: (bb, 0, i, 0))
    hmt_spec = pl.BlockSpec((None, N_HEADS, HEAD_DIM, tm), lambda bb, i: (bb, 0, 0, i))
    tok = lambda n: pl.BlockSpec((None, tm, n), lambda bb, i: (bb, i, 0))
    nw = w_packed.shape[1]
    return pl.pallas_call(
        _in_kernel,
        grid=(b, s // tm),
        in_specs=[tok(d),
                  pl.BlockSpec((None, N_MOD, d), lambda bb, i: (bb, 0, 0)),
                  pl.BlockSpec((1, d), lambda bb, i: (0, 0)),
                  pl.BlockSpec((d, nw), lambda bb, i: (0, 0)),
                  pl.BlockSpec((HEAD_DIM, 1), lambda bb, i: (0, 0)),
                  pl.BlockSpec((1, HEAD_DIM), lambda bb, i: (0, 0))],
        out_specs=[hm_spec] * 3 + [hmt_spec] * 2
                  + [tok(HEAD_DIM),
                     pl.BlockSpec((None, tm // LANES, 2 * HEAD_DIM, LANES), lambda bb, i: (bb, i, 0, 0)),
                     tok(HEAD_DIM),
                     pl.BlockSpec((None, N_HEADS, tm), lambda bb, i: (bb, 0, i)),
                     tok(d), tok(d)],
        out_shape=[hm] * 3 + [hmt] * 2
                  + [tok64,
                     jax.ShapeDtypeStruct((b, s // LANES, 2 * HEAD_DIM, LANES), BF16),
                     tok64,
                     jax.ShapeDtypeStruct((b, N_HEADS, s), F32),
                     jax.ShapeDtypeStruct((b, s, d), BF16),
                     jax.ShapeDtypeStruct((b, s, d), BF16)],
        compiler_params=pltpu.CompilerParams(
            dimension_semantics=("parallel", "parallel"), vmem_limit_bytes=VMEM_LIMIT),
        name="in_proj",
    )(x, mod_l, g1, w_packed, qng, kng)


def _sb_kernel(q_ref, k_ref, v_ref, o_ref, acc_scr, csum_scr, sp_scr, ls_scr):
    tq = q_ref.shape[1]
    tk = SB_KEY_BLOCK
    i = pl.program_id(1)
    nfull = (i * tq) // tk
    row = lax.broadcasted_iota(I32, (tq, tk), 0)
    col = lax.broadcasted_iota(I32, (tq, tk), 1)
    before = (col - row) < (i * tq - nfull * tk)
    krow = lax.broadcasted_iota(I32, (tk, tk), 0)
    kcol = lax.broadcasted_iota(I32, (tk, tk), 1)
    suffix = (krow > kcol).astype(BF16)

    def stage_a(hh, st, first):
        z2 = _nt_dot(q_ref[hh], k_ref[hh, pl.ds(st, tk), :])
        sp2 = jnp.maximum(jnp.log2(1.0 + jnp.exp2(jnp.minimum(z2, EXP2_MAX))), z2)
        if first:
            sp2 = jnp.where(before, sp2, 0.0)
        sp_scr[hh] = sp2.astype(BF16)
        ls_scr[hh] = z2 - sp2

    def stage_b(hh, st, first):
        sp = sp_scr[hh]
        suf = _dot(sp, suffix)
        if first:
            a = jnp.where(before, jnp.exp2(ls_scr[hh] - suf), 0.0)
            csum = jnp.zeros((tq, 1), F32)
        else:
            csum = csum_scr[hh]
            a = jnp.exp2((ls_scr[hh] - suf) - csum)
        c = _dot(a.astype(BF16), v_ref[hh, pl.ds(st, tk), :])
        if first:
            acc_scr[hh] = c
        else:
            acc_scr[hh] += c
        csum_scr[hh] = csum + (suf[:, 0:1] + sp[:, 0:1].astype(F32))

    def key_block(st, first):
        for hh in range(N_HEADS):
            stage_a(hh, st, first)
        for hh in range(N_HEADS):
            stage_b(hh, st, first)

    key_block(pl.multiple_of(nfull * tk, tk), True)

    def live():
        return jnp.min(csum_scr[...]) < SB_DEAD_LOG2

    def body(carry):
        jj, _ = carry
        key_block(pl.multiple_of((nfull - 1 - jj) * tk, tk), False)
        return jj + 1, live()

    lax.while_loop(lambda carry: (carry[0] < nfull) & carry[1], body, (jnp.int32(0), live()))
    for hh in range(N_HEADS):
        o_ref[:, hh * HEAD_DIM:(hh + 1) * HEAD_DIM] = acc_scr[hh].astype(o_ref.dtype)


def _sb_attention(qa, ka, va):
    b, nh, s, dh = qa.shape
    tq = SB_QUERY_BLOCK
    assert SB_KEY_BLOCK % tq == 0 and s % SB_KEY_BLOCK == 0
    return pl.pallas_call(
        _sb_kernel,
        grid=(b, s // tq),
        in_specs=[pl.BlockSpec((None, nh, tq, dh), lambda bb, i: (bb, 0, i, 0)),
                  pl.BlockSpec((None, nh, s, dh), lambda bb, i: (bb, 0, 0, 0)),
                  pl.BlockSpec((None, nh, s, dh), lambda bb, i: (bb, 0, 0, 0))],
        out_specs=pl.BlockSpec((None, tq, nh * dh), lambda bb, i: (bb, i, 0)),
        out_shape=jax.ShapeDtypeStruct((b, s, nh * dh), BF16),
        scratch_shapes=[pltpu.VMEM((nh, tq, dh), F32), pltpu.VMEM((nh, tq, 1), F32),
                        pltpu.VMEM((nh, tq, SB_KEY_BLOCK), BF16), pltpu.VMEM((nh, tq, SB_KEY_BLOCK), F32)],
        compiler_params=pltpu.CompilerParams(
            dimension_semantics=("parallel", "parallel"), vmem_limit_bytes=VMEM_LIMIT),
        name="stick_breaking",
    )(qa, ka, va)


def _t5_bucket_np(dist):
    max_exact = REL_BUCKETS // 2
    d_f = np.maximum(dist, 1).astype(np.float32)
    large = max_exact + (np.log(d_f / np.float32(max_exact))
                         / np.float32(math.log(REL_MAX_DIST / max_exact))
                         * np.float32(REL_BUCKETS - max_exact)).astype(np.int32)
    large = np.minimum(large, REL_BUCKETS - 1)
    return np.where(dist < max_exact, dist, large).astype(np.int32)


def _bias_tiles(rel_bias, tq, tk):
    n_near = tq // tk + 1
    key = np.arange(tk)[:, None]
    query = np.arange(tq)[None, :]
    bk = np.stack([_t5_bucket_np(np.maximum(query - key - (n - 1) * tk, 0)) for n in range(n_near)])
    far = int(_t5_bucket_np(np.array([REL_MAX_DIST]))[0])
    nh = rel_bias.shape[1]

    def body(rb_ref, bk_ref, o_ref):
        buckets = bk_ref[...]
        for hh in range(nh):
            tile = jnp.zeros((tk, tq), F32)
            for bucket in range(REL_BUCKETS):
                tile = jnp.where(buckets == bucket, rb_ref[hh, bucket] - rb_ref[hh, far], tile)
            o_ref[hh] = tile

    return pl.pallas_call(
        body,
        grid=(n_near,),
        in_specs=[pl.BlockSpec(memory_space=pltpu.SMEM),
                  pl.BlockSpec((None, tk, tq), lambda n: (n, 0, 0))],
        out_specs=pl.BlockSpec((None, nh, tk, tq), lambda n: (n, 0, 0, 0)),
        out_shape=jax.ShapeDtypeStruct((n_near, nh, tk, tq), F32),
        name="bias_tiles",
    )(rel_bias.T, jnp.asarray(bk))


def _sort_key(x):
    bits = pltpu.bitcast(x, I32)
    return bits ^ ((bits >> 31) & INT_MAX)


def _fold_rows(x, op):
    return functools.reduce(
        op, [x[g * SUBLANES:(g + 1) * SUBLANES, :] for g in range(x.shape[0] // SUBLANES)])


def _dsa_kernel(ktop, qit_ref, ki_ref, wit_ref, qbt_ref, kb_ref, vt_ref, bias_ref, o_ref,
                key_scr, hi_scr, lo_scr, m_scr, acc_scr, s_scr):
    tq = qit_ref.shape[2]
    tk = DSA_KEY_BLOCK
    per_q = tq // tk
    i = pl.program_id(1)
    first_own = per_q * i
    nchunks = per_q * (i + 1)
    key_i = lax.broadcasted_iota(I32, (tk, tq), 0)
    qry_i = lax.broadcasted_iota(I32, (tk, tq), 1)

    def causal(rel):
        return (key_i - qry_i) <= (-rel * tk)

    def for_chunks(fn):
        def far(j, carry):
            for u in range(per_q):
                fn(j * per_q + u, None)
            return carry
        lax.fori_loop(0, i - 1, far, 0)

        @pl.when(i >= 1)
        def _():
            for rel in range(-per_q, -1):
                fn(first_own + rel, None)
            fn(first_own - 1, -1)

        for rel in range(per_q):
            fn(first_own + rel, rel)

    def score_chunk(c, rel):
        kic = ki_ref[pl.ds(pl.multiple_of(c * tk, tk), tk), :]
        score = jnp.zeros((tk, tq), F32)
        for hh in range(N_HEADS):
            score = score + wit_ref[hh:hh + 1, :] * jnp.maximum(_dot(kic, qit_ref[hh]), 0.0)
        keys = _sort_key(score)
        if rel is not None and rel >= 0:
            keys = jnp.where(causal(rel), keys, INT_MIN)
        key_scr[c] = keys
        hi_scr[c] = (keys >> HALF_BITS).astype(I16)
        lo_scr[c] = ((keys & (2 * HALF_BIAS - 1)) - HALF_BIAS).astype(I16)

    for_chunks(score_chunk)

    def count(pred):
        def body(j, part):
            for u in range(per_q):
                part = part + _fold_rows(jnp.where(pred(key_scr[j * per_q + u]), 1, 0), jnp.add)
            return part
        part = lax.fori_loop(0, i + 1, body, jnp.zeros((SUBLANES, tq), I32))
        return jnp.sum(part, axis=0, keepdims=True)

    def count16(scr, pred):
        one, zero = jnp.int16(1), jnp.int16(0)

        def block(j, part):
            for u in range(per_q):
                hit = jnp.where(pred(scr[j * per_q + u]), one, zero)
                part = part + functools.reduce(
                    jnp.add, [hit[g * PACKED_SUBLANES:(g + 1) * PACKED_SUBLANES, :]
                              for g in range(tk // PACKED_SUBLANES)])
            return part

        nblk = i + 1
        part = lax.fori_loop(0, nblk // 2, lambda jj, part: block(2 * jj + 1, block(2 * jj, part)),
                             jnp.zeros((PACKED_SUBLANES, tq), I16))
        part = lax.cond(nblk % 2 == 1, lambda p: block(nblk - 1, p), lambda p: p, part)
        return jnp.sum(part.astype(I32), axis=0, keepdims=True)

    def search16(scr, base):
        def bit_body(bi, t_u):
            cand_u = t_u | jnp.left_shift(jnp.int32(1), HALF_BITS - 1 - bi)
            cand = (cand_u - HALF_BIAS).astype(I16)
            return jnp.where(base + count16(scr, lambda k: k >= cand) >= ktop, cand_u, t_u)
        return lax.fori_loop(0, HALF_BITS, bit_body, jnp.zeros((1, tq), I32))

    t_hi = search16(hi_scr, 0) - HALF_BIAS
    t_hi16 = t_hi.astype(I16)
    above = count16(hi_scr, lambda k: k > t_hi16)

    def keep_band(j, carry):
        for u in range(per_q):
            c = j * per_q + u
            lo_scr[c] = jnp.where(hi_scr[c] == t_hi16, lo_scr[c], jnp.int16(-HALF_BIAS))
        return carry

    lax.fori_loop(0, i + 1, keep_band, 0)
    thr = t_hi * (2 * HALF_BIAS) + search16(lo_scr, above)
    n_ge = count(lambda k: k >= thr)

    @pl.when(jnp.max(n_ge) > ktop)
    def _():
        need = (ktop - count(lambda k: k > thr)).astype(F32)
        lower = (lax.broadcasted_iota(I32, (tk, tk), 1)
                 < lax.broadcasted_iota(I32, (tk, tk), 0)).astype(BF16)

        def body(c, seen):
            k = key_scr[c]
            eq = k == thr
            eqf = eq.astype(BF16)
            rank = seen + _dot(lower, eqf)
            drop = eq & (rank >= need) & (n_ge > ktop)
            key_scr[c] = jnp.where(drop, thr - 1, k)
            return seen + jnp.sum(eqf.astype(F32), axis=0, keepdims=True)

        lax.fori_loop(0, nchunks, body, jnp.zeros((1, tq), F32))

    def logits(c, rel):
        sel = key_scr[c] >= thr
        if rel is not None and rel >= 0:
            sel = sel & causal(rel)
        kbc = kb_ref[pl.ds(pl.multiple_of(c * tk, tk), tk), :]

        def head(hh):
            s = _dot(kbc, qbt_ref[hh])
            if rel is not None:
                s = s + bias_ref[rel + 1, hh]
            return s
        return sel, head

    m_scr[...] = jnp.full(m_scr.shape, NEG_BIG, F32)

    def max_chunk(c, rel):
        sel, head = logits(c, rel)
        for hh in range(N_HEADS):
            s = jnp.where(sel, head(hh), NEG_BIG)
            s_scr[hh, c] = s
            m_scr[hh] = jnp.maximum(m_scr[hh], _fold_rows(s, jnp.maximum))

    for_chunks(max_chunk)
    m = [jnp.max(m_scr[hh], axis=0, keepdims=True) for hh in range(N_HEADS)]

    acc_scr[...] = jnp.zeros(acc_scr.shape, F32)

    def att_block(j, carry):
        for u in range(per_q):
            c = j * per_q + u
            vtc = vt_ref[c]
            for hh in range(N_HEADS):
                p = jnp.exp(s_scr[hh, c] - m[hh])
                acc_scr[hh] += _dot(vtc, p.astype(BF16))
        return carry

    lax.fori_loop(0, i + 1, att_block, 0)

    out_t = jnp.concatenate(
        [acc_scr[hh, :HEAD_DIM, :] / acc_scr[hh, HEAD_DIM:HEAD_DIM + 1, :] for hh in range(N_HEADS)],
        axis=0)
    o_ref[...] = out_t.T.astype(o_ref.dtype)


def _dsa_attention(qit, ki, wit, qbt, kb, vt, bias):
    b, nh, dh, s = qit.shape
    tq = DSA_QUERY_BLOCK
    tk = DSA_KEY_BLOCK
    assert tk == REL_MAX_DIST and tk == LANES and tq % tk == 0 and s % tq == 0
    ktop = min(INDEX_TOPK_MAX, s // 4)
    hmt_q = pl.BlockSpec((None, nh, dh, tq), lambda bb, i: (bb, 0, 0, i))
    full64 = pl.BlockSpec((None, s, dh), lambda bb, i: (bb, 0, 0))
    return pl.pallas_call(
        functools.partial(_dsa_kernel, ktop),
        grid=(b, s // tq),
        in_specs=[hmt_q, full64,
                  pl.BlockSpec((None, nh, tq), lambda bb, i: (bb, 0, i)),
                  hmt_q, full64,
                  pl.BlockSpec((None, s // tk, 2 * dh, tk), lambda bb, i: (bb, 0, 0, 0)),
                  pl.BlockSpec(bias.shape, lambda bb, i: (0, 0, 0, 0))],
        out_specs=pl.BlockSpec((None, tq, nh * dh), lambda bb, i: (bb, i, 0)),
        out_shape=jax.ShapeDtypeStruct((b, s, nh * dh), BF16),
        scratch_shapes=[pltpu.VMEM((s // tk, tk, tq), I32),
                        pltpu.VMEM((s // tk, tk, tq), I16),
                        pltpu.VMEM((s // tk, tk, tq), I16),
                        pltpu.VMEM((nh, SUBLANES, tq), F32),
                        pltpu.VMEM((nh, 2 * dh, tq), F32),
                        pltpu.VMEM((nh, s // tk, tk, tq), F32)],
        compiler_params=pltpu.CompilerParams(
            dimension_semantics=("parallel", "parallel"), vmem_limit_bytes=VMEM_LIMIT),
        name="sparse_attention",
    )(qit, ki, wit, qbt, kb, vt, bias)


def _first_max4(a):
    m1 = jnp.maximum(jnp.maximum(a[0], a[1]), jnp.maximum(a[2], a[3]))
    i1 = jnp.where(a[0] == m1, 0, jnp.where(a[1] == m1, 1, jnp.where(a[2] == m1, 2, 3)))
    rest = [jnp.where(i1 == j, -1.0, a[j]) for j in range(4)]
    m2 = jnp.maximum(jnp.maximum(rest[0], rest[1]), jnp.maximum(rest[2], rest[3]))
    i2 = jnp.where(rest[0] == m2, 0, jnp.where(rest[1] == m2, 1, jnp.where(rest[2] == m2, 2, 3)))
    return m1, i1, m2, i2


def _route(lt):
    lg = [lt[e:e + 1, :] for e in range(N_EXPERTS)]
    mx = functools.reduce(jnp.maximum, lg)
    ex = [jnp.exp(v - mx) for v in lg]
    tot = functools.reduce(lambda u, v: u + v, ex)
    p = [v / tot for v in ex]
    gs = []
    for g in range(N_GROUPS):
        m1, _, m2, _ = _first_max4(p[4 * g:4 * g + 4])
        gs.append(m1 + m2)
    _, gbest, _, _ = _first_max4(gs)
    chosen = [jnp.where(gbest == 0, p[j], jnp.where(gbest == 1, p[4 + j],
              jnp.where(gbest == 2, p[8 + j], p[12 + j]))) for j in range(4)]
    m1, i1, m2, i2 = _first_max4(chosen)
    den = m1 + m2
    lo = jnp.minimum(i1, i2)
    hi = jnp.maximum(i1, i2)
    pair = jnp.where(lo == 0, hi - 1, jnp.where(lo == 1, hi + 1, N_PAIRS - 1))
    first_is_lo = i1 < i2
    w1 = m1 / den
    w2 = m2 / den
    return gbest * N_PAIRS + pair, jnp.where(first_is_lo, w1, w2), jnp.where(first_is_lo, w2, w1)


def _merge_kernel(oa_ref, ob_ref, sga_ref, sgb_ref, x_ref, mod_ref, woa_ref, wob_ref, wout_ref,
                  g2_ref, rwh_ref, rwl_ref, rb_ref, x1_ref, h2x_ref, info_ref, cnt_ref, carry_scr):
    tm, d = x_ref.shape

    @pl.when((pl.program_id(0) == 0) & (pl.program_id(1) == 0))
    def _():
        carry_scr[...] = jnp.zeros(carry_scr.shape, F32)

    merged = (sga_ref[...].astype(F32) * _dot(oa_ref[...], woa_ref[...])
              + sgb_ref[...].astype(F32) * _dot(ob_ref[...], wob_ref[...]))
    x1 = x_ref[...] + mod_ref[2:3, :] * _dot(merged.astype(BF16), wout_ref[...])
    x1_ref[...] = x1
    h2 = (_rms(x1) * g2_ref[...]) * (1.0 + mod_ref[4:5, :]) + mod_ref[3:4, :]
    h2x_ref[:, :d] = h2
    hi = h2.astype(BF16)
    lo = (h2 - hi.astype(F32)).astype(BF16)
    logits = (_dot(hi, rwh_ref[...]) + _dot(lo, rwh_ref[...]) + _dot(hi, rwl_ref[...])) + rb_ref[...]
    cls, cw_lo, cw_hi = _route(logits.T)

    ncls = carry_scr.shape[0]
    onehot = (lax.broadcasted_iota(I32, (ncls, tm), 0) == cls).astype(F32)
    earlier = (lax.broadcasted_iota(I32, (tm, tm), 0)
               < lax.broadcasted_iota(I32, (tm, tm), 1)).astype(BF16)
    before = _dot(onehot.astype(BF16), earlier)
    carry = carry_scr[...]
    carry_t = jnp.concatenate([carry] * (tm // LANES), axis=1)
    rank = jnp.sum(onehot * (carry_t + before), axis=0, keepdims=True).astype(I32)
    carry = carry + jnp.sum(onehot, axis=1, keepdims=True)
    carry_scr[...] = carry
    cnt_ref[...] = carry.astype(I32)

    sub = lax.broadcasted_iota(I32, (SUBLANES, tm), 0)
    info_ref[...] = jnp.where(sub == 0, cls, jnp.where(sub == 1, rank, 0))
    sub = lax.broadcasted_iota(I32, (LANES, tm), 0)
    weights_t = jnp.where(sub == 0, cw_lo, jnp.where(sub == 1, cw_hi, 0.0))
    h2x_ref[:, d:] = weights_t.T


def _merge(oa, ob, sga, sgb, x, mod_l, woa, wob, wout, g2, rwh, rwl, rb, tm):
    b, s, d = x.shape
    hw = oa.shape[-1]
    nt = s // tm
    tok = lambda n: pl.BlockSpec((None, tm, n), lambda bb, i: (bb, i, 0))
    const = lambda r, c: pl.BlockSpec((r, c), lambda bb, i: (0, 0))
    return pl.pallas_call(
        _merge_kernel,
        grid=(b, nt),
        in_specs=[tok(hw), tok(hw), tok(d), tok(d), tok(d),
                  pl.BlockSpec((None, N_MOD, d), lambda bb, i: (bb, 0, 0)),
                  const(hw, d), const(hw, d), const(d, d), const(1, d),
                  const(d, LANES), const(d, LANES), const(1, LANES)],
        out_specs=[tok(d), tok(d + LANES),
                   pl.BlockSpec((SUBLANES, tm), lambda bb, i: (0, bb * nt + i)),
                   const(N_CLASS_PAD, LANES)],
        out_shape=[jax.ShapeDtypeStruct((b, s, d), F32),
                   jax.ShapeDtypeStruct((b, s, d + LANES), F32),
                   jax.ShapeDtypeStruct((SUBLANES, b * s), I32),
                   jax.ShapeDtypeStruct((N_CLASS_PAD, LANES), I32)],
        scratch_shapes=[pltpu.VMEM((N_CLASS_PAD, LANES), F32)],
        compiler_params=pltpu.CompilerParams(
            dimension_semantics=("arbitrary", "arbitrary"), vmem_limit_bytes=VMEM_LIMIT),
        name="merge_route",
    )(oa, ob, sga, sgb, x, mod_l, woa, wob, wout, g2, rwh, rwl, rb)


def _positions_kernel(offs_ref, info_ref, pos_ref):
    cls = info_ref[0:1, :]
    pos = info_ref[1:2, :]
    for c in range(N_CLASSES):
        pos = pos + jnp.where(cls == c, offs_ref[c], 0)
    pos_ref[...] = jnp.broadcast_to(pos, pos_ref.shape)


def _positions(info, offs):
    return pl.pallas_call(
        _positions_kernel,
        in_specs=[pl.BlockSpec(memory_space=pltpu.SMEM), pl.BlockSpec(info.shape, lambda: (0, 0))],
        out_specs=pl.BlockSpec(info.shape, lambda: (0, 0)),
        out_shape=jax.ShapeDtypeStruct(info.shape, I32),
        name="positions",
    )(offs, info)[0]


def _dispatch_kernel(ends_ref, pos_ref, src_ref, dst_ref, zero_scr, sem, zero_sem):
    tm = pos_ref.shape[1]
    row_tile = zero_scr.shape[0]

    @pl.when(pl.program_id(0) == 0)
    def _():
        zero_scr[...] = jnp.zeros(zero_scr.shape, F32)

        def zero_tail(c):
            start = pl.multiple_of(ends_ref[c] - row_tile, row_tile)
            return pltpu.make_async_copy(zero_scr, dst_ref.at[pl.ds(start, row_tile), :], zero_sem)

        def nonempty(c):
            return ends_ref[c] > (ends_ref[c - 1] if c else 0)

        for c in range(N_CLASSES):
            @pl.when(nonempty(c))
            def _(c=c):
                zero_tail(c).start()
        for c in range(N_CLASSES):
            @pl.when(nonempty(c))
            def _(c=c):
                zero_tail(c).wait()

        def unused_tile(j):
            start = pl.multiple_of(ends_ref[N_CLASSES - 1] + j * row_tile, row_tile)
            return pltpu.make_async_copy(zero_scr, dst_ref.at[pl.ds(start, row_tile), :], zero_sem)

        n_unused = (dst_ref.shape[0] - ends_ref[N_CLASSES - 1]) // row_tile
        lax.fori_loop(0, n_unused, lambda j, carry: (unused_tile(j).start(), carry)[1], 0)
        lax.fori_loop(0, n_unused, lambda j, carry: (unused_tile(j).wait(), carry)[1], 0)

    for r in range(tm):
        pltpu.make_async_copy(src_ref.at[pl.ds(r, 1), :], dst_ref.at[pl.ds(pos_ref[0, r], 1), :],
                              sem).start(priority=r % 2)
    pltpu.make_async_copy(src_ref, dst_ref.at[pl.ds(0, tm), :], sem).wait()


def _dispatch(h2x, pos, ends, n_rows):
    t, w = h2x.shape
    tm = DISPATCH_TILE
    pos = pos.reshape(t // tm, 1, tm)
    smem_row = pl.BlockSpec((None, 1, tm), lambda i: (i, 0, 0), memory_space=pltpu.SMEM)
    return pl.pallas_call(
        _dispatch_kernel,
        grid=(t // tm,),
        in_specs=[pl.BlockSpec(memory_space=pltpu.SMEM), smem_row, pl.BlockSpec((tm, w), lambda i: (i, 0))],
        out_specs=pl.BlockSpec(memory_space=pl.ANY),
        out_shape=jax.ShapeDtypeStruct((n_rows, w), F32),
        scratch_shapes=[pltpu.VMEM((MOE_ROW_TILE, w), F32), pltpu.SemaphoreType.DMA, pltpu.SemaphoreType.DMA],
        compiler_params=pltpu.CompilerParams(
            dimension_semantics=("arbitrary",), vmem_limit_bytes=VMEM_LIMIT),
        name="dispatch",
    )(ends, pos, h2x)


def _experts_kernel(elo_ref, ehi_ref, live_ref, xs_ref, wg_lo, wg_hi, wu_lo, wu_hi, wd_lo, wd_hi, ys_ref):
    del elo_ref, ehi_ref
    d = ys_ref.shape[1]
    j = pl.program_id(0)

    @pl.when(live_ref[j] == 1)
    def _():
        h = xs_ref[:, :d].astype(BF16)

        def expert(wg, wu, wd, weight):
            gate = _dot(h, wg[...])
            hidden = ((gate * jax.nn.sigmoid(gate)) * _dot(h, wu[...])) * weight
            return _dot(hidden.astype(BF16), wd[...])

        ys_ref[...] = (expert(wg_lo, wu_lo, wd_lo, xs_ref[:, d:d + 1])
                       + expert(wg_hi, wu_hi, wd_hi, xs_ref[:, d + 1:d + 2]))

    @pl.when(live_ref[j] == 0)
    def _():
        ys_ref[...] = jnp.zeros(ys_ref.shape, F32)


def _experts(xs, e_lo, e_hi, live, wg, wu, wd, layer):
    n_rows, w = xs.shape
    _, ne, d, f = wg.shape
    tm = MOE_ROW_TILE
    lo = lambda j, elo, ehi, lv: (layer, elo[j], 0, 0)
    hi = lambda j, elo, ehi, lv: (layer, ehi[j], 0, 0)
    grid_spec = pltpu.PrefetchScalarGridSpec(
        num_scalar_prefetch=3,
        grid=(n_rows // tm,),
        in_specs=[pl.BlockSpec((tm, w), lambda j, elo, ehi, lv: (j, 0)),
                  pl.BlockSpec((None, None, d, f), lo), pl.BlockSpec((None, None, d, f), hi),
                  pl.BlockSpec((None, None, d, f), lo), pl.BlockSpec((None, None, d, f), hi),
                  pl.BlockSpec((None, None, f, d), lo), pl.BlockSpec((None, None, f, d), hi)],
        out_specs=pl.BlockSpec((tm, d), lambda j, elo, ehi, lv: (j, 0)),
    )
    return pl.pallas_call(
        _experts_kernel,
        grid_spec=grid_spec,
        out_shape=jax.ShapeDtypeStruct((n_rows, d), F32),
        compiler_params=pltpu.CompilerParams(
            dimension_semantics=("arbitrary",), vmem_limit_bytes=VMEM_LIMIT),
        name="experts",
    )(e_lo, e_hi, live, xs, wg, wg, wu, wu, wd, wd)


def _combine_kernel(pos_ref, next_ref, ys_ref, x1_ref, mod_ref, o_ref, buf, sem):
    tm = buf.shape[1]
    step = pl.program_id(0) * pl.num_programs(1) + pl.program_id(1)
    last = pl.num_programs(0) * pl.num_programs(1) - 1

    def issue(p_ref, slot):
        for r in range(tm):
            pltpu.make_async_copy(ys_ref.at[pl.ds(p_ref[slot, r], 1), :], buf.at[slot, pl.ds(r, 1), :],
                                  sem.at[slot]).start(priority=r % 2)

    def finish(slot):
        pltpu.make_async_copy(ys_ref.at[pl.ds(0, tm), :], buf.at[slot], sem.at[slot]).wait()
        rows = pl.ds(slot * tm, tm)
        o_ref[rows, :] = x1_ref[rows, :] + mod_ref[5:6, :] * buf[slot]

    @pl.when(step == 0)
    def _():
        issue(pos_ref, 0)

    issue(pos_ref, 1)
    finish(0)

    @pl.when(step < last)
    def _():
        issue(next_ref, 0)

    finish(1)


def _combine(ys, pos, x1, mod_l):
    b, s, d = x1.shape
    tm = MOE_DMA_TILE
    assert s % (2 * tm) == 0
    nt = s // (2 * tm)
    n_steps = b * nt
    pos2 = pos.reshape(n_steps, 2, tm)
    this_step = pl.BlockSpec((None, 2, tm), lambda bb, i: (bb * nt + i, 0, 0), memory_space=pltpu.SMEM)
    next_step = pl.BlockSpec((None, 2, tm), lambda bb, i: (jnp.minimum(bb * nt + i + 1, n_steps - 1), 0, 0),
                             memory_space=pltpu.SMEM)
    return pl.pallas_call(
        _combine_kernel,
        grid=(b, nt),
        in_specs=[this_step, next_step,
                  pl.BlockSpec(memory_space=pl.ANY),
                  pl.BlockSpec((None, 2 * tm, d), lambda bb, i: (bb, i, 0)),
                  pl.BlockSpec((None, N_MOD, d), lambda bb, i: (bb, 0, 0))],
        out_specs=pl.BlockSpec((None, 2 * tm, d), lambda bb, i: (bb, i, 0)),
        out_shape=jax.ShapeDtypeStruct((b, s, d), F32),
        scratch_shapes=[pltpu.VMEM((2, tm, d), F32), pltpu.SemaphoreType.DMA((2,))],
        compiler_params=pltpu.CompilerParams(
            dimension_semantics=("arbitrary", "arbitrary"), vmem_limit_bytes=VMEM_LIMIT),
        name="combine",
    )(pos2, pos2, ys, x1, mod_l)


def _moe(h2x, info, counts, wg, wu, wd, layer, x1, mod_l):
    b, s, d = x1.shape
    t = b * s
    tm = MOE_ROW_TILE
    n_tiles = t // tm + N_CLASSES
    counts = counts[:N_CLASSES, 0]
    padded = (counts + tm - 1) // tm * tm
    ends = jnp.cumsum(padded)
    offs = jnp.pad(ends - padded, (0, N_CLASS_PAD - N_CLASSES)).astype(I32)
    tile_cls = jnp.sum(jnp.arange(n_tiles, dtype=I32)[:, None] * tm >= ends[None, :], axis=1)
    live = (tile_cls < N_CLASSES).astype(I32)
    tile_cls = jnp.minimum(tile_cls, N_CLASSES - 1)
    group, pair = tile_cls // N_PAIRS, tile_cls % N_PAIRS
    pairs = jnp.asarray(_PAIRS, I32)
    e_lo = group * EXPERTS_PER_GROUP + pairs[pair, 0]
    e_hi = group * EXPERTS_PER_GROUP + pairs[pair, 1]

    pos = _positions(info, offs).reshape(t // MOE_DMA_TILE, 1, MOE_DMA_TILE)
    xs = _dispatch(h2x.reshape(t, d + LANES), pos, ends.astype(I32), n_tiles * tm)
    ys = _experts(xs, e_lo, e_hi, live, wg, wu, wd, layer)
    return _combine(ys, pos, x1, mod_l)


def kernel(x, c, w_ada, b_ada, norm1_g, w_in, q_norm_g, k_norm_g, rel_bias, w_o_a, w_o_b, w_out,
           norm2_g, router_w, router_b, w_gate, w_up, w_down):
    b, s, d = x.shape
    depth = w_ada.shape[0]
    assert d % LANES == 0
    tm = min(TOKEN_TILE, s)

    mod = _adaln(c, w_ada, b_ada)
    bias = _bias_tiles(rel_bias, DSA_QUERY_BLOCK, DSA_KEY_BLOCK)
    rw = jnp.pad(router_w, ((0, 0), (0, LANES - N_EXPERTS)))
    rwh = rw.astype(BF16)
    rwl = (rw - rwh.astype(F32)).astype(BF16)
    rb = jnp.pad(router_b, (0, LANES - N_EXPERTS)).reshape(1, LANES)
    wg, wu, wd = w_gate.astype(BF16), w_up.astype(BF16), w_down.astype(BF16)

    for l in range(depth):
        qa, ka, va, qbt, qit, kb, vt, ki, wit, sga, sgb = _in_proj(
            x, mod[l], norm1_g[l].reshape(1, d), _pack_w_in(w_in[l]),
            q_norm_g[l].reshape(HEAD_DIM, 1), k_norm_g[l].reshape(1, HEAD_DIM), tm)
        oa = _sb_attention(qa, ka, va)
        ob = _dsa_attention(qit, ki, wit, qbt, kb, vt, bias)
        x1, h2x, info, counts = _merge(oa, ob, sga, sgb, x, mod[l], w_o_a[l].astype(BF16),
                                       w_o_b[l].astype(BF16), w_out[l].astype(BF16),
                                       norm2_g[l].reshape(1, d), rwh, rwl, rb, tm)
        x = _moe(h2x, info, counts, wg, wu, wd, l, x1, mod[l])
    return x
```

```python
import functools
import math

import numpy as np
import jax
import jax.numpy as jnp
from jax import lax
from jax.experimental import pallas as pl
from jax.experimental.pallas import tpu as pltpu

F32 = jnp.float32
BF16 = jnp.bfloat16
I32 = jnp.int32
I16 = jnp.int16

HEAD_DIM = 64
N_HEADS = 8
N_EXPERTS = 16
N_GROUPS = 4
EXPERTS_PER_GROUP = 4
_PAIRS = ((0, 1), (0, 2), (0, 3), (1, 2), (1, 3), (2, 3))
N_PAIRS = len(_PAIRS)
N_CLASSES = N_GROUPS * N_PAIRS
N_CLASS_PAD = 32
MOE_ROW_TILE = 256
MOE_DMA_TILE = 512
DISPATCH_TILE = 1024
N_MOD = 6
EPS = 1e-6
INDEX_TOPK_MAX = 256
REL_BUCKETS = 32
REL_MAX_DIST = 128
LANES = 128
SUBLANES = 8
PACKED_SUBLANES = 16
HALF_BITS = 16
TOKEN_TILE = 512
EXP2_MAX = 126.0
DSA_QUERY_BLOCK = 256
DSA_KEY_BLOCK = 128
SB_QUERY_BLOCK = 256
SB_KEY_BLOCK = 256
SB_DEAD_LOG2 = 150.0
LOG2E = 1.4426950408889634
NEG_BIG = -1e30
INT_MIN = -2 ** 31
INT_MAX = 2 ** 31 - 1
HALF_BIAS = 2 ** 15

VMEM_LIMIT = 56 * 1024 * 1024


def _nt_dot(a, b):
    return lax.dot_general(a, b, (((1,), (1,)), ((), ())), preferred_element_type=F32)


def _dot(a, b):
    return jnp.dot(a, b, preferred_element_type=F32)


def _mod_kernel(c_ref, w_ref, b_ref, o_ref):
    c = c_ref[...]
    sc = c * jax.nn.sigmoid(c)
    o_ref[...] = _dot(sc.astype(BF16), w_ref[...].astype(BF16)) + b_ref[...]


def _adaln(c, w_ada, b_ada):
    depth, d, nd = w_ada.shape
    b = c.shape[0]
    out = pl.pallas_call(
        _mod_kernel,
        grid=(depth, nd // d),
        in_specs=[pl.BlockSpec((b, d), lambda l, j: (0, 0)),
                  pl.BlockSpec((None, d, d), lambda l, j: (l, 0, j)),
                  pl.BlockSpec((None, 1, d), lambda l, j: (l, 0, j))],
        out_specs=pl.BlockSpec((None, b, d), lambda l, j: (l, 0, j)),
        out_shape=jax.ShapeDtypeStruct((depth, b, nd), F32),
        compiler_params=pltpu.CompilerParams(vmem_limit_bytes=VMEM_LIMIT),
        name="adaln",
    )(c, w_ada, b_ada.reshape(depth, 1, nd))
    return out.reshape(depth, b, N_MOD, d)


def _rms(x):
    return x * lax.rsqrt(jnp.mean(x * x, axis=-1, keepdims=True) + EPS)


def _in_kernel(x_ref, mod_ref, g1_ref, w_ref, qng_ref, kng_ref,
               qa_ref, ka_ref, va_ref, qbt_ref, qit_ref, kb_ref, vt_ref, ki_ref, wit_ref,
               sga_ref, sgb_ref):
    d = x_ref.shape[-1]
    tm = x_ref.shape[0]
    hw = N_HEADS * HEAD_DIM
    x = x_ref[...]
    h = (_rms(x) * g1_ref[...]) * (1.0 + mod_ref[1:2, :]) + mod_ref[0:1, :]
    hb = h.astype(BF16)

    def proj(lo, n):
        return _dot(hb, w_ref[:, lo:lo + n])

    def heads(r, ref):
        for hh in range(N_HEADS):
            ref[hh] = r[:, hh * HEAD_DIM:(hh + 1) * HEAD_DIM].astype(ref.dtype)

    scale = HEAD_DIM ** -0.5
    heads(proj(0, hw) * (scale * LOG2E), qa_ref)
    heads(proj(hw, hw), ka_ref)
    heads(proj(2 * hw, hw), va_ref)

    rt = proj(3 * hw, hw).T
    for hh in range(N_HEADS):
        slab = rt[hh * HEAD_DIM:(hh + 1) * HEAD_DIM, :]
        inv = lax.rsqrt(jnp.mean(slab * slab, axis=0, keepdims=True) + EPS)
        qbt_ref[hh] = (((slab * inv) * qng_ref[...]) * scale).astype(BF16)

    r = proj(4 * hw, 2 * HEAD_DIM)
    kb_ref[...] = (_rms(r[:, :HEAD_DIM]) * kng_ref[...]).astype(BF16)
    vt = r.T[HEAD_DIM:, :]
    ones_row = (lax.broadcasted_iota(I32, (HEAD_DIM, LANES), 0) == 0).astype(BF16)
    for cc in range(tm // LANES):
        vt_ref[cc, :HEAD_DIM, :] = vt[:, cc * LANES:(cc + 1) * LANES].astype(BF16)
        vt_ref[cc, HEAD_DIM:, :] = ones_row

    rt = proj(4 * hw + LANES, hw).T
    for hh in range(N_HEADS):
        qit_ref[hh] = rt[hh * HEAD_DIM:(hh + 1) * HEAD_DIM, :].astype(BF16)

    r = proj(5 * hw + LANES, LANES)
    ki_ref[...] = r[:, :HEAD_DIM].astype(BF16)
    wit_ref[...] = r.T[HEAD_DIM:HEAD_DIM + N_HEADS, :]

    off = 5 * hw + 2 * LANES
    sga_ref[...] = jax.nn.sigmoid(proj(off, d)).astype(BF16)
    sgb_ref[...] = jax.nn.sigmoid(proj(off + d, d)).astype(BF16)


def _pack_w_in(w_in_l):
    hw = N_HEADS * HEAD_DIM
    a = 4 * hw + 2 * HEAD_DIM + hw
    small = w_in_l[:, a:a + HEAD_DIM + N_HEADS]
    small = jnp.pad(small, ((0, 0), (0, LANES - small.shape[1])))
    gates = w_in_l[:, a + HEAD_DIM + N_HEADS:]
    return jnp.concatenate([w_in_l[:, :a], small, gates], axis=1).astype(BF16)


def _in_proj(x, mod_l, g1, w_packed, qng, kng, tm):
    b, s, d = x.shape
    hm = jax.ShapeDtypeStruct((b, N_HEADS, s, HEAD_DIM), BF16)
    hmt = jax.ShapeDtypeStruct((b, N_HEADS, HEAD_DIM, s), BF16)
    tok64 = jax.ShapeDtypeStruct((b, s, HEAD_DIM), BF16)
    hm_spec = pl.BlockSpec((None, N_HEADS, tm, HEAD_DIM), lambda bb, i: (bb, 0, i, 0))
    hmt_spec = pl.BlockSpec((None, N_HEADS, HEAD_DIM, tm), lambda bb, i: (bb, 0, 0, i))
    tok = lambda n: pl.BlockSpec((None, tm, n), lambda bb, i: (bb, i, 0))
    nw = w_packed.shape[1]
    return pl.pallas_call(
        _in_kernel,
        grid=(b, s // tm),
        in_specs=[tok(d),
                  pl.BlockSpec((None, N_MOD, d), lambda bb, i: (bb, 0, 0)),
                  pl.BlockSpec((1, d), lambda bb, i: (0, 0)),
                  pl.BlockSpec((d, nw), lambda bb, i: (0, 0)),
                  pl.BlockSpec((HEAD_DIM, 1), lambda bb, i: (0, 0)),
                  pl.BlockSpec((1, HEAD_DIM), lambda bb, i: (0, 0))],
        out_specs=[hm_spec] * 3 + [hmt_spec] * 2
                  + [tok(HEAD_DIM),
                     pl.BlockSpec((None, tm // LANES, 2 * HEAD_DIM, LANES), lambda bb, i: (bb, i, 0, 0)),
                     tok(HEAD_DIM),
                     pl.BlockSpec((None, N_HEADS, tm), lambda bb, i: (bb, 0, i)),
                     tok(d), tok(d)],
        out_shape=[hm] * 3 + [hmt] * 2
                  + [tok64,
                     jax.ShapeDtypeStruct((b, s // LANES, 2 * HEAD_DIM, LANES), BF16),
                     tok64,
                     jax.ShapeDtypeStruct((b, N_HEADS, s), F32),
                     jax.ShapeDtypeStruct((b, s, d), BF16),
                     jax.ShapeDtypeStruct((b, s, d), BF16)],
        compiler_params=pltpu.CompilerParams(
            dimension_semantics=("parallel", "parallel"), vmem_limit_bytes=VMEM_LIMIT),
        name="in_proj",
    )(x, mod_l, g1, w_packed, qng, kng)


def _sb_kernel(q_ref, k_ref, v_ref, o_ref, acc_scr, csum_scr, sp_scr, ls_scr):
    tq = q_ref.shape[1]
    tk = SB_KEY_BLOCK
    i = pl.program_id(1)
    nfull = (i * tq) // tk
    row = lax.broadcasted_iota(I32, (tq, tk), 0)
    col = lax.broadcasted_iota(I32, (tq, tk), 1)
    before = (col - row) < (i * tq - nfull * tk)
    krow = lax.broadcasted_iota(I32, (tk, tk), 0)
    kcol = lax.broadcasted_iota(I32, (tk, tk), 1)
    suffix = (krow > kcol).astype(BF16)

    def stage_a(hh, st, first):
        z2 = _nt_dot(q_ref[hh], k_ref[hh, pl.ds(st, tk), :])
        sp2 = jnp.maximum(jnp.log2(1.0 + jnp.exp2(jnp.minimum(z2, EXP2_MAX))), z2)
        if first:
            sp2 = jnp.where(before, sp2, 0.0)
        sp_scr[hh] = sp2.astype(BF16)
        ls_scr[hh] = z2 - sp2

    def stage_b(hh, st, first):
        sp = sp_scr[hh]
        suf = _dot(sp, suffix)
        if first:
            a = jnp.where(before, jnp.exp2(ls_scr[hh] - suf), 0.0)
            csum = jnp.zeros((tq, 1), F32)
        else:
            csum = csum_scr[hh]
            a = jnp.exp2((ls_scr[hh] - suf) - csum)
        c = _dot(a.astype(BF16), v_ref[hh, pl.ds(st, tk), :])
        if first:
            acc_scr[hh] = c
        else:
            acc_scr[hh] += c
        csum_scr[hh] = csum + (suf[:, 0:1] + sp[:, 0:1].astype(F32))

    def key_block(st, first):
        for hh in range(N_HEADS):
            stage_a(hh, st, first)
        for hh in range(N_HEADS):
            stage_b(hh, st, first)

    key_block(pl.multiple_of(nfull * tk, tk), True)

    def live():
        return jnp.min(csum_scr[...]) < SB_DEAD_LOG2

    def body(carry):
        jj, _ = carry
        key_block(pl.multiple_of((nfull - 1 - jj) * tk, tk), False)
        return jj + 1, live()

    lax.while_loop(lambda carry: (carry[0] < nfull) & carry[1], body, (jnp.int32(0), live()))
    for hh in range(N_HEADS):
        o_ref[:, hh * HEAD_DIM:(hh + 1) * HEAD_DIM] = acc_scr[hh].astype(o_ref.dtype)


def _sb_attention(qa, ka, va):
    b, nh, s, dh = qa.shape
    tq = SB_QUERY_BLOCK
    assert SB_KEY_BLOCK % tq == 0 and s % SB_KEY_BLOCK == 0
    return pl.pallas_call(
        _sb_kernel,
        grid=(b, s // tq),
        in_specs=[pl.BlockSpec((None, nh, tq, dh), lambda bb, i: (bb, 0, i, 0)),
                  pl.BlockSpec((None, nh, s, dh), lambda bb, i: (bb, 0, 0, 0)),
                  pl.BlockSpec((None, nh, s, dh), lambda bb, i: (bb, 0, 0, 0))],
        out_specs=pl.BlockSpec((None, tq, nh * dh), lambda bb, i: (bb, i, 0)),
        out_shape=jax.ShapeDtypeStruct((b, s, nh * dh), BF16),
        scratch_shapes=[pltpu.VMEM((nh, tq, dh), F32), pltpu.VMEM((nh, tq, 1), F32),
                        pltpu.VMEM((nh, tq, SB_KEY_BLOCK), BF16), pltpu.VMEM((nh, tq, SB_KEY_BLOCK), F32)],
        compiler_params=pltpu.CompilerParams(
            dimension_semantics=("parallel", "parallel"), vmem_limit_bytes=VMEM_LIMIT),
        name="stick_breaking",
    )(qa, ka, va)


def _t5_bucket_np(dist):
    max_exact = REL_BUCKETS // 2
    d_f = np.maximum(dist, 1).astype(np.float32)
    large = max_exact + (np.log(d_f / np.float32(max_exact))
                         / np.float32(math.log(REL_MAX_DIST / max_exact))
                         * np.float32(REL_BUCKETS - max_exact)).astype(np.int32)
    large = np.minimum(large, REL_BUCKETS - 1)
    return np.where(dist < max_exact, dist, large).astype(np.int32)


def _bias_tiles(rel_bias, tq, tk):
    n_near = tq // tk + 1
    key = np.arange(tk)[:, None]
    query = np.arange(tq)[None, :]
    bk = np.stack([_t5_bucket_np(np.maximum(query - key - (n - 1) * tk, 0)) for n in range(n_near)])
    far = int(_t5_bucket_np(np.array([REL_MAX_DIST]))[0])
    nh = rel_bias.shape[1]

    def body(rb_ref, bk_ref, o_ref):
        buckets = bk_ref[...]
        for hh in range(nh):
            tile = jnp.zeros((tk, tq), F32)
            for bucket in range(REL_BUCKETS):
                tile = jnp.where(buckets == bucket, rb_ref[hh, bucket] - rb_ref[hh, far], tile)
            o_ref[hh] = tile

    return pl.pallas_call(
        body,
        grid=(n_near,),
        in_specs=[pl.BlockSpec(memory_space=pltpu.SMEM),
                  pl.BlockSpec((None, tk, tq), lambda n: (n, 0, 0))],
        out_specs=pl.BlockSpec((None, nh, tk, tq), lambda n: (n, 0, 0, 0)),
        out_shape=jax.ShapeDtypeStruct((n_near, nh, tk, tq), F32),
        name="bias_tiles",
    )(rel_bias.T, jnp.asarray(bk))


def _sort_key(x):
    bits = pltpu.bitcast(x, I32)
    return bits ^ ((bits >> 31) & INT_MAX)


def _fold_rows(x, op):
    return functools.reduce(
        op, [x[g * SUBLANES:(g + 1) * SUBLANES, :] for g in range(x.shape[0] // SUBLANES)])


def _dsa_kernel(ktop, qit_ref, ki_ref, wit_ref, qbt_ref, kb_ref, vt_ref, bias_ref, o_ref,
                key_scr, hi_scr, lo_scr, m_scr, acc_scr, s_scr):
    tq = qit_ref.shape[2]
    tk = DSA_KEY_BLOCK
    per_q = tq // tk
    i = pl.program_id(1)
    first_own = per_q * i
    nchunks = per_q * (i + 1)
    key_i = lax.broadcasted_iota(I32, (tk, tq), 0)
    qry_i = lax.broadcasted_iota(I32, (tk, tq), 1)

    def causal(rel):
        return (key_i - qry_i) <= (-rel * tk)

    def for_chunks(fn):
        def far(j, carry):
            for u in range(per_q):
                fn(j * per_q + u, None)
            return carry
        lax.fori_loop(0, i - 1, far, 0)

        @pl.when(i >= 1)
        def _():
            for rel in range(-per_q, -1):
                fn(first_own + rel, None)
            fn(first_own - 1, -1)

        for rel in range(per_q):
            fn(first_own + rel, rel)

    def score_chunk(c, rel):
        kic = ki_ref[pl.ds(pl.multiple_of(c * tk, tk), tk), :]
        score = jnp.zeros((tk, tq), F32)
        for hh in range(N_HEADS):
            score = score + wit_ref[hh:hh + 1, :] * jnp.maximum(_dot(kic, qit_ref[hh]), 0.0)
        keys = _sort_key(score)
        if rel is not None and rel >= 0:
            keys = jnp.where(causal(rel), keys, INT_MIN)
        key_scr[c] = keys
        hi_scr[c] = (keys >> HALF_BITS).astype(I16)
        lo_scr[c] = ((keys & (2 * HALF_BIAS - 1)) - HALF_BIAS).astype(I16)

    for_chunks(score_chunk)

    def count(pred):
        def body(j, part):
            for u in range(per_q):
                part = part + _fold_rows(jnp.where(pred(key_scr[j * per_q + u]), 1, 0), jnp.add)
            return part
        part = lax.fori_loop(0, i + 1, body, jnp.zeros((SUBLANES, tq), I32))
        return jnp.sum(part, axis=0, keepdims=True)

    def count16(scr, pred):
        one, zero = jnp.int16(1), jnp.int16(0)

        def block(j, part):
            for u in range(per_q):
                hit = jnp.where(pred(scr[j * per_q + u]), one, zero)
                part = part + functools.reduce(
                    jnp.add, [hit[g * PACKED_SUBLANES:(g + 1) * PACKED_SUBLANES, :]
                              for g in range(tk // PACKED_SUBLANES)])
            return part

        nblk = i + 1
        part = lax.fori_loop(0, nblk // 2, lambda jj, part: block(2 * jj + 1, block(2 * jj, part)),
                             jnp.zeros((PACKED_SUBLANES, tq), I16))
        part = lax.cond(nblk % 2 == 1, lambda p: block(nblk - 1, p), lambda p: p, part)
        return jnp.sum(part.astype(I32), axis=0, keepdims=True)

    def search16(scr, base):
        def bit_body(bi, t_u):
            cand_u = t_u | jnp.left_shift(jnp.int32(1), HALF_BITS - 1 - bi)
            cand = (cand_u - HALF_BIAS).astype(I16)
            return jnp.where(base + count16(scr, lambda k: k >= cand) >= ktop, cand_u, t_u)
        return lax.fori_loop(0, HALF_BITS, bit_body, jnp.zeros((1, tq), I32))

    def find_threshold():
        t_hi = search16(hi_scr, 0) - HALF_BIAS
        t_hi16 = t_hi.astype(I16)
        above = count16(hi_scr, lambda k: k > t_hi16)

        def keep_band(j, carry):
            for u in range(per_q):
                c = j * per_q + u
                lo_scr[c] = jnp.where(hi_scr[c] == t_hi16, lo_scr[c], jnp.int16(-HALF_BIAS))
            return carry

        lax.fori_loop(0, i + 1, keep_band, 0)
        return t_hi * (2 * HALF_BIAS) + search16(lo_scr, above)

    thr = lax.cond((i + 1) * tq <= ktop, lambda: jnp.full((1, tq), INT_MIN, I32), find_threshold)
    n_ge = count(lambda k: k >= thr)

    @pl.when(jnp.max(n_ge) > ktop)
    def _():
        need = (ktop - count(lambda k: k > thr)).astype(F32)
        lower = (lax.broadcasted_iota(I32, (tk, tk), 1)
                 < lax.broadcasted_iota(I32, (tk, tk), 0)).astype(BF16)

        def body(c, seen):
            k = key_scr[c]
            eq = k == thr
            eqf = eq.astype(BF16)
            rank = seen + _dot(lower, eqf)
            drop = eq & (rank >= need) & (n_ge > ktop)
            key_scr[c] = jnp.where(drop, thr - 1, k)
            return seen + jnp.sum(eqf.astype(F32), axis=0, keepdims=True)

        lax.fori_loop(0, nchunks, body, jnp.zeros((1, tq), F32))

    def logits(c, rel):
        sel = key_scr[c] >= thr
        if rel is not None and rel >= 0:
            sel = sel & causal(rel)
        kbc = kb_ref[pl.ds(pl.multiple_of(c * tk, tk), tk), :]

        def head(hh):
            s = _dot(kbc, qbt_ref[hh])
            if rel is not None:
                s = s + bias_ref[rel + 1, hh]
            return s
        return sel, head

    m_scr[...] = jnp.full(m_scr.shape, NEG_BIG, F32)

    def max_chunk(c, rel):
        sel, head = logits(c, rel)
        for hh in range(N_HEADS):
            s = jnp.where(sel, head(hh), NEG_BIG)
            s_scr[hh, c] = s
            m_scr[hh] = jnp.maximum(m_scr[hh], _fold_rows(s, jnp.maximum))

    for_chunks(max_chunk)
    m = [jnp.max(m_scr[hh], axis=0, keepdims=True) for hh in range(N_HEADS)]

    acc_scr[...] = jnp.zeros(acc_scr.shape, F32)

    def att_block(j, carry):
        for u in range(per_q):
            c = j * per_q + u
            vtc = vt_ref[c]
            for hh in range(N_HEADS):
                p = jnp.exp(s_scr[hh, c] - m[hh])
                acc_scr[hh] += _dot(vtc, p.astype(BF16))
        return carry

    lax.fori_loop(0, i + 1, att_block, 0)

    out_t = jnp.concatenate(
        [acc_scr[hh, :HEAD_DIM, :] / acc_scr[hh, HEAD_DIM:HEAD_DIM + 1, :] for hh in range(N_HEADS)],
        axis=0)
    o_ref[...] = out_t.T.astype(o_ref.dtype)


def _dsa_attention(qit, ki, wit, qbt, kb, vt, bias):
    b, nh, dh, s = qit.shape
    tq = DSA_QUERY_BLOCK
    tk = DSA_KEY_BLOCK
    assert tk == REL_MAX_DIST and tk == LANES and tq % tk == 0 and s % tq == 0
    ktop = min(INDEX_TOPK_MAX, s // 4)
    hmt_q = pl.BlockSpec((None, nh, dh, tq), lambda bb, i: (bb, 0, 0, i))
    full64 = pl.BlockSpec((None, s, dh), lambda bb, i: (bb, 0, 0))
    return pl.pallas_call(
        functools.partial(_dsa_kernel, ktop),
        grid=(b, s // tq),
        in_specs=[hmt_q, full64,
                  pl.BlockSpec((None, nh, tq), lambda bb, i: (bb, 0, i)),
                  hmt_q, full64,
                  pl.BlockSpec((None, s // tk, 2 * dh, tk), lambda bb, i: (bb, 0, 0, 0)),
                  pl.BlockSpec(bias.shape, lambda bb, i: (0, 0, 0, 0))],
        out_specs=pl.BlockSpec((None, tq, nh * dh), lambda bb, i: (bb, i, 0)),
        out_shape=jax.ShapeDtypeStruct((b, s, nh * dh), BF16),
        scratch_shapes=[pltpu.VMEM((s // tk, tk, tq), I32),
                        pltpu.VMEM((s // tk, tk, tq), I16),
                        pltpu.VMEM((s // tk, tk, tq), I16),
                        pltpu.VMEM((nh, SUBLANES, tq), F32),
                        pltpu.VMEM((nh, 2 * dh, tq), F32),
                        pltpu.VMEM((nh, s // tk, tk, tq), F32)],
        compiler_params=pltpu.CompilerParams(
            dimension_semantics=("parallel", "parallel"), vmem_limit_bytes=VMEM_LIMIT),
        name="sparse_attention",
    )(qit, ki, wit, qbt, kb, vt, bias)


def _first_max4(a):
    m1 = jnp.maximum(jnp.maximum(a[0], a[1]), jnp.maximum(a[2], a[3]))
    i1 = jnp.where(a[0] == m1, 0, jnp.where(a[1] == m1, 1, jnp.where(a[2] == m1, 2, 3)))
    rest = [jnp.where(i1 == j, -1.0, a[j]) for j in range(4)]
    m2 = jnp.maximum(jnp.maximum(rest[0], rest[1]), jnp.maximum(rest[2], rest[3]))
    i2 = jnp.where(rest[0] == m2, 0, jnp.where(rest[1] == m2, 1, jnp.where(rest[2] == m2, 2, 3)))
    return m1, i1, m2, i2


def _route(lt):
    lg = [lt[e:e + 1, :] for e in range(N_EXPERTS)]
    mx = functools.reduce(jnp.maximum, lg)
    ex = [jnp.exp(v - mx) for v in lg]
    tot = functools.reduce(lambda u, v: u + v, ex)
    p = [v / tot for v in ex]
    gs = []
    for g in range(N_GROUPS):
        m1, _, m2, _ = _first_max4(p[4 * g:4 * g + 4])
        gs.append(m1 + m2)
    _, gbest, _, _ = _first_max4(gs)
    chosen = [jnp.where(gbest == 0, p[j], jnp.where(gbest == 1, p[4 + j],
              jnp.where(gbest == 2, p[8 + j], p[12 + j]))) for j in range(4)]
    m1, i1, m2, i2 = _first_max4(chosen)
    den = m1 + m2
    lo = jnp.minimum(i1, i2)
    hi = jnp.maximum(i1, i2)
    pair = jnp.where(lo == 0, hi - 1, jnp.where(lo == 1, hi + 1, N_PAIRS - 1))
    first_is_lo = i1 < i2
    w1 = m1 / den
    w2 = m2 / den
    return gbest * N_PAIRS + pair, jnp.where(first_is_lo, w1, w2), jnp.where(first_is_lo, w2, w1)


def _merge_kernel(oa_ref, ob_ref, sga_ref, sgb_ref, x_ref, mod_ref, woa_ref, wob_ref, wout_ref,
                  g2_ref, rwh_ref, rwl_ref, rb_ref, x1_ref, h2x_ref, info_ref, cnt_ref, carry_scr):
    tm, d = x_ref.shape

    @pl.when((pl.program_id(0) == 0) & (pl.program_id(1) == 0))
    def _():
        carry_scr[...] = jnp.zeros(carry_scr.shape, F32)

    merged = (sga_ref[...].astype(F32) * _dot(oa_ref[...], woa_ref[...])
              + sgb_ref[...].astype(F32) * _dot(ob_ref[...], wob_ref[...]))
    x1 = x_ref[...] + mod_ref[2:3, :] * _dot(merged.astype(BF16), wout_ref[...])
    x1_ref[...] = x1
    h2 = (_rms(x1) * g2_ref[...]) * (1.0 + mod_ref[4:5, :]) + mod_ref[3:4, :]
    h2x_ref[:, :d] = h2
    hi = h2.astype(BF16)
    lo = (h2 - hi.astype(F32)).astype(BF16)
    logits = (_dot(hi, rwh_ref[...]) + _dot(lo, rwh_ref[...]) + _dot(hi, rwl_ref[...])) + rb_ref[...]
    cls, cw_lo, cw_hi = _route(logits.T)

    ncls = carry_scr.shape[0]
    onehot = (lax.broadcasted_iota(I32, (ncls, tm), 0) == cls).astype(F32)
    earlier = (lax.broadcasted_iota(I32, (tm, tm), 0)
               < lax.broadcasted_iota(I32, (tm, tm), 1)).astype(BF16)
    before = _dot(onehot.astype(BF16), earlier)
    carry = carry_scr[...]
    carry_t = jnp.concatenate([carry] * (tm // LANES), axis=1)
    rank = jnp.sum(onehot * (carry_t + before), axis=0, keepdims=True).astype(I32)
    carry = carry + jnp.sum(onehot, axis=1, keepdims=True)
    carry_scr[...] = carry
    cnt_ref[...] = carry.astype(I32)

    sub = lax.broadcasted_iota(I32, (SUBLANES, tm), 0)
    info_ref[...] = jnp.where(sub == 0, cls, jnp.where(sub == 1, rank, 0))
    sub = lax.broadcasted_iota(I32, (LANES, tm), 0)
    weights_t = jnp.where(sub == 0, cw_lo, jnp.where(sub == 1, cw_hi, 0.0))
    h2x_ref[:, d:] = weights_t.T


def _merge(oa, ob, sga, sgb, x, mod_l, woa, wob, wout, g2, rwh, rwl, rb, tm):
    b, s, d = x.shape
    hw = oa.shape[-1]
    nt = s // tm
    tok = lambda n: pl.BlockSpec((None, tm, n), lambda bb, i: (bb, i, 0))
    const = lambda r, c: pl.BlockSpec((r, c), lambda bb, i: (0, 0))
    return pl.pallas_call(
        _merge_kernel,
        grid=(b, nt),
        in_specs=[tok(hw), tok(hw), tok(d), tok(d), tok(d),
                  pl.BlockSpec((None, N_MOD, d), lambda bb, i: (bb, 0, 0)),
                  const(hw, d), const(hw, d), const(d, d), const(1, d),
                  const(d, LANES), const(d, LANES), const(1, LANES)],
        out_specs=[tok(d), tok(d + LANES),
                   pl.BlockSpec((SUBLANES, tm), lambda bb, i: (0, bb * nt + i)),
                   const(N_CLASS_PAD, LANES)],
        out_shape=[jax.ShapeDtypeStruct((b, s, d), F32),
                   jax.ShapeDtypeStruct((b, s, d + LANES), F32),
                   jax.ShapeDtypeStruct((SUBLANES, b * s), I32),
                   jax.ShapeDtypeStruct((N_CLASS_PAD, LANES), I32)],
        scratch_shapes=[pltpu.VMEM((N_CLASS_PAD, LANES), F32)],
        compiler_params=pltpu.CompilerParams(
            dimension_semantics=("arbitrary", "arbitrary"), vmem_limit_bytes=VMEM_LIMIT),
        name="merge_route",
    )(oa, ob, sga, sgb, x, mod_l, woa, wob, wout, g2, rwh, rwl, rb)


def _positions_kernel(offs_ref, info_ref, pos_ref):
    cls = info_ref[0:1, :]
    pos = info_ref[1:2, :]
    for c in range(N_CLASSES):
        pos = pos + jnp.where(cls == c, offs_ref[c], 0)
    pos_ref[...] = jnp.broadcast_to(pos, pos_ref.shape)


def _positions(info, offs):
    return pl.pallas_call(
        _positions_kernel,
        in_specs=[pl.BlockSpec(memory_space=pltpu.SMEM), pl.BlockSpec(info.shape, lambda: (0, 0))],
        out_specs=pl.BlockSpec(info.shape, lambda: (0, 0)),
        out_shape=jax.ShapeDtypeStruct(info.shape, I32),
        name="positions",
    )(offs, info)[0]


def _dispatch_kernel(ends_ref, pos_ref, src_ref, dst_ref, zero_scr, sem, zero_sem):
    tm = pos_ref.shape[1]
    row_tile = zero_scr.shape[0]

    @pl.when(pl.program_id(0) == 0)
    def _():
        zero_scr[...] = jnp.zeros(zero_scr.shape, F32)

        def zero_tail(c):
            start = pl.multiple_of(ends_ref[c] - row_tile, row_tile)
            return pltpu.make_async_copy(zero_scr, dst_ref.at[pl.ds(start, row_tile), :], zero_sem)

        def nonempty(c):
            return ends_ref[c] > (ends_ref[c - 1] if c else 0)

        for c in range(N_CLASSES):
            @pl.when(nonempty(c))
            def _(c=c):
                zero_tail(c).start()
        for c in range(N_CLASSES):
            @pl.when(nonempty(c))
            def _(c=c):
                zero_tail(c).wait()

        def unused_tile(j):
            start = pl.multiple_of(ends_ref[N_CLASSES - 1] + j * row_tile, row_tile)
            return pltpu.make_async_copy(zero_scr, dst_ref.at[pl.ds(start, row_tile), :], zero_sem)

        n_unused = (dst_ref.shape[0] - ends_ref[N_CLASSES - 1]) // row_tile
        lax.fori_loop(0, n_unused, lambda j, carry: (unused_tile(j).start(), carry)[1], 0)
        lax.fori_loop(0, n_unused, lambda j, carry: (unused_tile(j).wait(), carry)[1], 0)

    for r in range(tm):
        pltpu.make_async_copy(src_ref.at[pl.ds(r, 1), :], dst_ref.at[pl.ds(pos_ref[0, r], 1), :],
                              sem).start(priority=r % 2)
    pltpu.make_async_copy(src_ref, dst_ref.at[pl.ds(0, tm), :], sem).wait()


def _dispatch(h2x, pos, ends, n_rows):
    t, w = h2x.shape
    tm = DISPATCH_TILE
    pos = pos.reshape(t // tm, 1, tm)
    smem_row = pl.BlockSpec((None, 1, tm), lambda i: (i, 0, 0), memory_space=pltpu.SMEM)
    return pl.pallas_call(
        _dispatch_kernel,
        grid=(t // tm,),
        in_specs=[pl.BlockSpec(memory_space=pltpu.SMEM), smem_row, pl.BlockSpec((tm, w), lambda i: (i, 0))],
        out_specs=pl.BlockSpec(memory_space=pl.ANY),
        out_shape=jax.ShapeDtypeStruct((n_rows, w), F32),
        scratch_shapes=[pltpu.VMEM((MOE_ROW_TILE, w), F32), pltpu.SemaphoreType.DMA, pltpu.SemaphoreType.DMA],
        compiler_params=pltpu.CompilerParams(
            dimension_semantics=("arbitrary",), vmem_limit_bytes=VMEM_LIMIT),
        name="dispatch",
    )(ends, pos, h2x)


def _experts_kernel(elo_ref, ehi_ref, live_ref, xs_ref, wg_lo, wg_hi, wu_lo, wu_hi, wd_lo, wd_hi, ys_ref):
    del elo_ref, ehi_ref
    d = ys_ref.shape[1]
    j = pl.program_id(0)

    @pl.when(live_ref[j] == 1)
    def _():
        h = xs_ref[:, :d].astype(BF16)

        def expert(wg, wu, wd, weight):
            gate = _dot(h, wg[...])
            hidden = ((gate * jax.nn.sigmoid(gate)) * _dot(h, wu[...])) * weight
            return _dot(hidden.astype(BF16), wd[...])

        ys_ref[...] = (expert(wg_lo, wu_lo, wd_lo, xs_ref[:, d:d + 1])
                       + expert(wg_hi, wu_hi, wd_hi, xs_ref[:, d + 1:d + 2]))

    @pl.when(live_ref[j] == 0)
    def _():
        ys_ref[...] = jnp.zeros(ys_ref.shape, F32)


def _experts(xs, e_lo, e_hi, live, wg, wu, wd, layer):
    n_rows, w = xs.shape
    _, ne, d, f = wg.shape
    tm = MOE_ROW_TILE
    lo = lambda j, elo, ehi, lv: (layer, elo[j], 0, 0)
    hi = lambda j, elo, ehi, lv: (layer, ehi[j], 0, 0)
    grid_spec = pltpu.PrefetchScalarGridSpec(
        num_scalar_prefetch=3,
        grid=(n_rows // tm,),
        in_specs=[pl.BlockSpec((tm, w), lambda j, elo, ehi, lv: (j, 0)),
                  pl.BlockSpec((None, None, d, f), lo), pl.BlockSpec((None, None, d, f), hi),
                  pl.BlockSpec((None, None, d, f), lo), pl.BlockSpec((None, None, d, f), hi),
                  pl.BlockSpec((None, None, f, d), lo), pl.BlockSpec((None, None, f, d), hi)],
        out_specs=pl.BlockSpec((tm, d), lambda j, elo, ehi, lv: (j, 0)),
    )
    return pl.pallas_call(
        _experts_kernel,
        grid_spec=grid_spec,
        out_shape=jax.ShapeDtypeStruct((n_rows, d), F32),
        compiler_params=pltpu.CompilerParams(
            dimension_semantics=("arbitrary",), vmem_limit_bytes=VMEM_LIMIT),
        name="experts",
    )(e_lo, e_hi, live, xs, wg, wg, wu, wu, wd, wd)


def _combine_kernel(pos_ref, next_ref, ys_ref, x1_ref, mod_ref, o_ref, buf, sem):
    tm = buf.shape[1]
    step = pl.program_id(0) * pl.num_programs(1) + pl.program_id(1)
    last = pl.num_programs(0) * pl.num_programs(1) - 1

    def issue(p_ref, slot):
        for r in range(tm):
            pltpu.make_async_copy(ys_ref.at[pl.ds(p_ref[slot, r], 1), :], buf.at[slot, pl.ds(r, 1), :],
                                  sem.at[slot]).start(priority=r % 2)

    def finish(slot):
        pltpu.make_async_copy(ys_ref.at[pl.ds(0, tm), :], buf.at[slot], sem.at[slot]).wait()
        rows = pl.ds(slot * tm, tm)
        o_ref[rows, :] = x1_ref[rows, :] + mod_ref[5:6, :] * buf[slot]

    @pl.when(step == 0)
    def _():
        issue(pos_ref, 0)

    issue(pos_ref, 1)
    finish(0)

    @pl.when(step < last)
    def _():
        issue(next_ref, 0)

    finish(1)


def _combine(ys, pos, x1, mod_l):
    b, s, d = x1.shape
    tm = MOE_DMA_TILE
    assert s % (2 * tm) == 0
    nt = s // (2 * tm)
    n_steps = b * nt
    pos2 = pos.reshape(n_steps, 2, tm)
    this_step = pl.BlockSpec((None, 2, tm), lambda bb, i: (bb * nt + i, 0, 0), memory_space=pltpu.SMEM)
    next_step = pl.BlockSpec((None, 2, tm), lambda bb, i: (jnp.minimum(bb * nt + i + 1, n_steps - 1), 0, 0),
                             memory_space=pltpu.SMEM)
    return pl.pallas_call(
        _combine_kernel,
        grid=(b, nt),
        in_specs=[this_step, next_step,
                  pl.BlockSpec(memory_space=pl.ANY),
                  pl.BlockSpec((None, 2 * tm, d), lambda bb, i: (bb, i, 0)),
                  pl.BlockSpec((None, N_MOD, d), lambda bb, i: (bb, 0, 0))],
        out_specs=pl.BlockSpec((None, 2 * tm, d), lambda bb, i: (bb, i, 0)),
        out_shape=jax.ShapeDtypeStruct((b, s, d), F32),
        scratch_shapes=[pltpu.VMEM((2, tm, d), F32), pltpu.SemaphoreType.DMA((2,))],
        compiler_params=pltpu.CompilerParams(
            dimension_semantics=("arbitrary", "arbitrary"), vmem_limit_bytes=VMEM_LIMIT),
        name="combine",
    )(pos2, pos2, ys, x1, mod_l)


def _moe(h2x, info, counts, wg, wu, wd, layer, x1, mod_l):
    b, s, d = x1.shape
    t = b * s
    tm = MOE_ROW_TILE
    n_tiles = t // tm + N_CLASSES
    counts = counts[:N_CLASSES, 0]
    padded = (counts + tm - 1) // tm * tm
    ends = jnp.cumsum(padded)
    offs = jnp.pad(ends - padded, (0, N_CLASS_PAD - N_CLASSES)).astype(I32)
    tile_cls = jnp.sum(jnp.arange(n_tiles, dtype=I32)[:, None] * tm >= ends[None, :], axis=1)
    live = (tile_cls < N_CLASSES).astype(I32)
    tile_cls = jnp.minimum(tile_cls, N_CLASSES - 1)
    group, pair = tile_cls // N_PAIRS, tile_cls % N_PAIRS
    pairs = jnp.asarray(_PAIRS, I32)
    e_lo = group * EXPERTS_PER_GROUP + pairs[pair, 0]
    e_hi = group * EXPERTS_PER_GROUP + pairs[pair, 1]

    pos = _positions(info, offs).reshape(t // MOE_DMA_TILE, 1, MOE_DMA_TILE)
    xs = _dispatch(h2x.reshape(t, d + LANES), pos, ends.astype(I32), n_tiles * tm)
    ys = _experts(xs, e_lo, e_hi, live, wg, wu, wd, layer)
    return _combine(ys, pos, x1, mod_l)


def kernel(x, c, w_ada, b_ada, norm1_g, w_in, q_norm_g, k_norm_g, rel_bias, w_o_a, w_o_b, w_out,
           norm2_g, router_w, router_b, w_gate, w_up, w_down):
    b, s, d = x.shape
    depth = w_ada.shape[0]
    assert d % LANES == 0
    tm = min(TOKEN_TILE, s)

    mod = _adaln(c, w_ada, b_ada)
    bias = _bias_tiles(rel_bias, DSA_QUERY_BLOCK, DSA_KEY_BLOCK)
    rw = jnp.pad(router_w, ((0, 0), (0, LANES - N_EXPERTS)))
    rwh = rw.astype(BF16)
    rwl = (rw - rwh.astype(F32)).astype(BF16)
    rb = jnp.pad(router_b, (0, LANES - N_EXPERTS)).reshape(1, LANES)
    wg, wu, wd = w_gate.astype(BF16), w_up.astype(BF16), w_down.astype(BF16)

    for l in range(depth):
        qa, ka, va, qbt, qit, kb, vt, ki, wit, sga, sgb = _in_proj(
            x, mod[l], norm1_g[l].reshape(1, d), _pack_w_in(w_in[l]),
            q_norm_g[l].reshape(HEAD_DIM, 1), k_norm_g[l].reshape(1, HEAD_DIM), tm)
        oa = _sb_attention(qa, ka, va)
        ob = _dsa_attention(qit, ki, wit, qbt, kb, vt, bias)
        x1, h2x, info, counts = _merge(oa, ob, sga, sgb, x, mod[l], w_o_a[l].astype(BF16),
                                       w_o_b[l].astype(BF16), w_out[l].astype(BF16),
                                       norm2_g[l].reshape(1, d), rwh, rwl, rb, tm)
        x = _moe(h2x, info, counts, wg, wu, wd, l, x1, mod[l])
    return x
```

```python
import functools
import math

import numpy as np
import jax
import jax.numpy as jnp
from jax import lax
from jax.experimental import pallas as pl
from jax.experimental.pallas import tpu as pltpu

F32 = jnp.float32
BF16 = jnp.bfloat16
I32 = jnp.int32
I16 = jnp.int16

HEAD_DIM = 64
N_HEADS = 8
N_EXPERTS = 16
N_GROUPS = 4
EXPERTS_PER_GROUP = 4
_PAIRS = ((0, 1), (0, 2), (0, 3), (1, 2), (1, 3), (2, 3))
N_PAIRS = len(_PAIRS)
N_CLASSES = N_GROUPS * N_PAIRS
N_CLASS_PAD = 32
MOE_ROW_TILE = 256
MOE_DMA_TILE = 512
DISPATCH_TILE = 1024
N_MOD = 6
EPS = 1e-6
INDEX_TOPK_MAX = 256
REL_BUCKETS = 32
REL_MAX_DIST = 128
LANES = 128
SUBLANES = 8
PACKED_SUBLANES = 16
HALF_BITS = 16
TOKEN_TILE = 512
EXP2_MAX = 126.0
DSA_QUERY_BLOCK = 256
DSA_KEY_BLOCK = 128
SB_QUERY_BLOCK = 256
SB_KEY_BLOCK = 256
SB_DEAD_LOG2 = 150.0
LOG2E = 1.4426950408889634
NEG_BIG = -1e30
INT_MIN = -2 ** 31
INT_MAX = 2 ** 31 - 1
HALF_BIAS = 2 ** 15

VMEM_LIMIT = 56 * 1024 * 1024


def _nt_dot(a, b):
    return lax.dot_general(a, b, (((1,), (1,)), ((), ())), preferred_element_type=F32)


def _dot(a, b):
    return jnp.dot(a, b, preferred_element_type=F32)


def _mod_kernel(c_ref, w_ref, b_ref, o_ref):
    c = c_ref[...]
    sc = c * jax.nn.sigmoid(c)
    o_ref[...] = _dot(sc.astype(BF16), w_ref[...].astype(BF16)) + b_ref[...]


def _adaln(c, w_ada, b_ada):
    depth, d, nd = w_ada.shape
    b = c.shape[0]
    out = pl.pallas_call(
        _mod_kernel,
        grid=(depth, nd // (2 * d)),
        in_specs=[pl.BlockSpec((b, d), lambda l, j: (0, 0)),
                  pl.BlockSpec((None, d, 2 * d), lambda l, j: (l, 0, j)),
                  pl.BlockSpec((None, 1, 2 * d), lambda l, j: (l, 0, j))],
        out_specs=pl.BlockSpec((None, b, 2 * d), lambda l, j: (l, 0, j)),
        out_shape=jax.ShapeDtypeStruct((depth, b, nd), F32),
        compiler_params=pltpu.CompilerParams(vmem_limit_bytes=VMEM_LIMIT),
        name="adaln",
    )(c, w_ada, b_ada.reshape(depth, 1, nd))
    return out.reshape(depth, b, N_MOD, d)


def _rms(x):
    return x * lax.rsqrt(jnp.mean(x * x, axis=-1, keepdims=True) + EPS)


def _in_kernel(x_ref, mod_ref, g1_ref, w_ref, qng_ref, kng_ref,
               qa_ref, ka_ref, va_ref, qbt_ref, qit_ref, kb_ref, vt_ref, ki_ref, wit_ref,
               sga_ref, sgb_ref):
    d = x_ref.shape[-1]
    tm = x_ref.shape[0]
    hw = N_HEADS * HEAD_DIM
    x = x_ref[...]
    h = (_rms(x) * g1_ref[...]) * (1.0 + mod_ref[1:2, :]) + mod_ref[0:1, :]
    hb = h.astype(BF16)

    def proj(lo, n):
        return _dot(hb, w_ref[:, lo:lo + n])

    def heads(r, ref):
        for hh in range(N_HEADS):
            ref[hh] = r[:, hh * HEAD_DIM:(hh + 1) * HEAD_DIM].astype(ref.dtype)

    scale = HEAD_DIM ** -0.5
    heads(proj(0, hw) * (scale * LOG2E), qa_ref)
    heads(proj(hw, hw), ka_ref)
    heads(proj(2 * hw, hw), va_ref)

    rt = proj(3 * hw, hw).T
    for hh in range(N_HEADS):
        slab = rt[hh * HEAD_DIM:(hh + 1) * HEAD_DIM, :]
        inv = lax.rsqrt(jnp.mean(slab * slab, axis=0, keepdims=True) + EPS)
        qbt_ref[hh] = (((slab * inv) * qng_ref[...]) * scale).astype(BF16)

    r = proj(4 * hw, 2 * HEAD_DIM)
    kb_ref[...] = (_rms(r[:, :HEAD_DIM]) * kng_ref[...]).astype(BF16)
    vt = r.T[HEAD_DIM:, :]
    ones_row = (lax.broadcasted_iota(I32, (HEAD_DIM, LANES), 0) == 0).astype(BF16)
    for cc in range(tm // LANES):
        vt_ref[cc, :HEAD_DIM, :] = vt[:, cc * LANES:(cc + 1) * LANES].astype(BF16)
        vt_ref[cc, HEAD_DIM:, :] = ones_row

    rt = proj(4 * hw + LANES, hw).T
    for hh in range(N_HEADS):
        qit_ref[hh] = rt[hh * HEAD_DIM:(hh + 1) * HEAD_DIM, :].astype(BF16)

    r = proj(5 * hw + LANES, LANES)
    ki_ref[...] = r[:, :HEAD_DIM].astype(BF16)
    wit_ref[...] = r.T[HEAD_DIM:HEAD_DIM + N_HEADS, :]

    off = 5 * hw + 2 * LANES
    sga_ref[...] = jax.nn.sigmoid(proj(off, d)).astype(BF16)
    sgb_ref[...] = jax.nn.sigmoid(proj(off + d, d)).astype(BF16)


def _pack_w_in(w_in_l):
    hw = N_HEADS * HEAD_DIM
    a = 4 * hw + 2 * HEAD_DIM + hw
    small = w_in_l[:, a:a + HEAD_DIM + N_HEADS]
    small = jnp.pad(small, ((0, 0), (0, LANES - small.shape[1])))
    gates = w_in_l[:, a + HEAD_DIM + N_HEADS:]
    return jnp.concatenate([w_in_l[:, :a], small, gates], axis=1).astype(BF16)


def _in_proj(x, mod_l, g1, w_packed, qng, kng, tm):
    b, s, d = x.shape
    hm = jax.ShapeDtypeStruct((b, N_HEADS, s, HEAD_DIM), BF16)
    hmt = jax.ShapeDtypeStruct((b, N_HEADS, HEAD_DIM, s), BF16)
    tok64 = jax.ShapeDtypeStruct((b, s, HEAD_DIM), BF16)
    hm_spec = pl.BlockSpec((None, N_HEADS, tm, HEAD_DIM), lambda bb, i: (bb, 0, i, 0))
    hmt_spec = pl.BlockSpec((None, N_HEADS, HEAD_DIM, tm), lambda bb, i: (bb, 0, 0, i))
    tok = lambda n: pl.BlockSpec((None, tm, n), lambda bb, i: (bb, i, 0))
    nw = w_packed.shape[1]
    return pl.pallas_call(
        _in_kernel,
        grid=(b, s // tm),
        in_specs=[tok(d),
                  pl.BlockSpec((None, N_MOD, d), lambda bb, i: (bb, 0, 0)),
                  pl.BlockSpec((1, d), lambda bb, i: (0, 0)),
                  pl.BlockSpec((d, nw), lambda bb, i: (0, 0)),
                  pl.BlockSpec((HEAD_DIM, 1), lambda bb, i: (0, 0)),
                  pl.BlockSpec((1, HEAD_DIM), lambda bb, i: (0, 0))],
        out_specs=[hm_spec] * 3 + [hmt_spec] * 2
                  + [tok(HEAD_DIM),
                     pl.BlockSpec((None, tm // LANES, 2 * HEAD_DIM, LANES), lambda bb, i: (bb, i, 0, 0)),
                     tok(HEAD_DIM),
                     pl.BlockSpec((None, N_HEADS, tm), lambda bb, i: (bb, 0, i)),
                     tok(d), tok(d)],
        out_shape=[hm] * 3 + [hmt] * 2
                  + [tok64,
                     jax.ShapeDtypeStruct((b, s // LANES, 2 * HEAD_DIM, LANES), BF16),
                     tok64,
                     jax.ShapeDtypeStruct((b, N_HEADS, s), F32),
                     jax.ShapeDtypeStruct((b, s, d), BF16),
                     jax.ShapeDtypeStruct((b, s, d), BF16)],
        compiler_params=pltpu.CompilerParams(
            dimension_semantics=("parallel", "parallel"), vmem_limit_bytes=VMEM_LIMIT),
        name="in_proj",
    )(x, mod_l, g1, w_packed, qng, kng)


def _sb_kernel(q_ref, k_ref, v_ref, o_ref, acc_scr, csum_scr, sp_scr, ls_scr):
    tq = q_ref.shape[1]
    tk = SB_KEY_BLOCK
    i = pl.program_id(1)
    nfull = (i * tq) // tk
    row = lax.broadcasted_iota(I32, (tq, tk), 0)
    col = lax.broadcasted_iota(I32, (tq, tk), 1)
    before = (col - row) < (i * tq - nfull * tk)
    krow = lax.broadcasted_iota(I32, (tk, tk), 0)
    kcol = lax.broadcasted_iota(I32, (tk, tk), 1)
    suffix = (krow > kcol).astype(BF16)

    def stage_a(hh, st, first):
        z2 = _nt_dot(q_ref[hh], k_ref[hh, pl.ds(st, tk), :])
        sp2 = jnp.maximum(jnp.log2(1.0 + jnp.exp2(jnp.minimum(z2, EXP2_MAX))), z2)
        if first:
            sp2 = jnp.where(before, sp2, 0.0)
        sp_scr[hh] = sp2.astype(BF16)
        ls_scr[hh] = z2 - sp2

    def stage_b(hh, st, first):
        sp = sp_scr[hh]
        suf = _dot(sp, suffix)
        if first:
            a = jnp.where(before, jnp.exp2(ls_scr[hh] - suf), 0.0)
            csum = jnp.zeros((tq, 1), F32)
        else:
            csum = csum_scr[hh]
            a = jnp.exp2((ls_scr[hh] - suf) - csum)
        c = _dot(a.astype(BF16), v_ref[hh, pl.ds(st, tk), :])
        if first:
            acc_scr[hh] = c
        else:
            acc_scr[hh] += c
        csum_scr[hh] = csum + (suf[:, 0:1] + sp[:, 0:1].astype(F32))

    def key_block(st, first):
        for hh in range(N_HEADS):
            stage_a(hh, st, first)
        for hh in range(N_HEADS):
            stage_b(hh, st, first)

    key_block(pl.multiple_of(nfull * tk, tk), True)

    def live():
        return jnp.min(csum_scr[...]) < SB_DEAD_LOG2

    def body(carry):
        jj, _ = carry
        key_block(pl.multiple_of((nfull - 1 - jj) * tk, tk), False)
        return jj + 1, live()

    lax.while_loop(lambda carry: (carry[0] < nfull) & carry[1], body, (jnp.int32(0), live()))
    for hh in range(N_HEADS):
        o_ref[:, hh * HEAD_DIM:(hh + 1) * HEAD_DIM] = acc_scr[hh].astype(o_ref.dtype)


def _sb_attention(qa, ka, va):
    b, nh, s, dh = qa.shape
    tq = SB_QUERY_BLOCK
    assert SB_KEY_BLOCK % tq == 0 and s % SB_KEY_BLOCK == 0
    return pl.pallas_call(
        _sb_kernel,
        grid=(b, s // tq),
        in_specs=[pl.BlockSpec((None, nh, tq, dh), lambda bb, i: (bb, 0, i, 0)),
                  pl.BlockSpec((None, nh, s, dh), lambda bb, i: (bb, 0, 0, 0)),
                  pl.BlockSpec((None, nh, s, dh), lambda bb, i: (bb, 0, 0, 0))],
        out_specs=pl.BlockSpec((None, tq, nh * dh), lambda bb, i: (bb, i, 0)),
        out_shape=jax.ShapeDtypeStruct((b, s, nh * dh), BF16),
        scratch_shapes=[pltpu.VMEM((nh, tq, dh), F32), pltpu.VMEM((nh, tq, 1), F32),
                        pltpu.VMEM((nh, tq, SB_KEY_BLOCK), BF16), pltpu.VMEM((nh, tq, SB_KEY_BLOCK), F32)],
        compiler_params=pltpu.CompilerParams(
            dimension_semantics=("parallel", "parallel"), vmem_limit_bytes=VMEM_LIMIT),
        name="stick_breaking",
    )(qa, ka, va)


def _t5_bucket_np(dist):
    max_exact = REL_BUCKETS // 2
    d_f = np.maximum(dist, 1).astype(np.float32)
    large = max_exact + (np.log(d_f / np.float32(max_exact))
                         / np.float32(math.log(REL_MAX_DIST / max_exact))
                         * np.float32(REL_BUCKETS - max_exact)).astype(np.int32)
    large = np.minimum(large, REL_BUCKETS - 1)
    return np.where(dist < max_exact, dist, large).astype(np.int32)


def _bias_tiles(rel_bias, tq, tk):
    n_near = tq // tk + 1
    key = np.arange(tk)[:, None]
    query = np.arange(tq)[None, :]
    bk = np.stack([_t5_bucket_np(np.maximum(query - key - (n - 1) * tk, 0)) for n in range(n_near)])
    far = int(_t5_bucket_np(np.array([REL_MAX_DIST]))[0])
    nh = rel_bias.shape[1]

    def body(rb_ref, bk_ref, o_ref):
        buckets = bk_ref[...]
        for hh in range(nh):
            tile = jnp.zeros((tk, tq), F32)
            for bucket in range(REL_BUCKETS):
                tile = jnp.where(buckets == bucket, rb_ref[hh, bucket] - rb_ref[hh, far], tile)
            o_ref[hh] = tile

    return pl.pallas_call(
        body,
        grid=(n_near,),
        in_specs=[pl.BlockSpec(memory_space=pltpu.SMEM),
                  pl.BlockSpec((None, tk, tq), lambda n: (n, 0, 0))],
        out_specs=pl.BlockSpec((None, nh, tk, tq), lambda n: (n, 0, 0, 0)),
        out_shape=jax.ShapeDtypeStruct((n_near, nh, tk, tq), F32),
        name="bias_tiles",
    )(rel_bias.T, jnp.asarray(bk))


def _sort_key(x):
    bits = pltpu.bitcast(x, I32)
    return bits ^ ((bits >> 31) & INT_MAX)


def _fold_rows(x, op):
    return functools.reduce(
        op, [x[g * SUBLANES:(g + 1) * SUBLANES, :] for g in range(x.shape[0] // SUBLANES)])


def _dsa_kernel(ktop, qit_ref, ki_ref, wit_ref, qbt_ref, kb_ref, vt_ref, bias_ref, o_ref,
                key_scr, hi_scr, lo_scr, m_scr, acc_scr, s_scr):
    tq = qit_ref.shape[2]
    tk = DSA_KEY_BLOCK
    per_q = tq // tk
    i = pl.program_id(1)
    first_own = per_q * i
    nchunks = per_q * (i + 1)
    key_i = lax.broadcasted_iota(I32, (tk, tq), 0)
    qry_i = lax.broadcasted_iota(I32, (tk, tq), 1)

    def causal(rel):
        return (key_i - qry_i) <= (-rel * tk)

    def for_chunks(fn):
        def far(j, carry):
            for u in range(per_q):
                fn(j * per_q + u, None)
            return carry
        lax.fori_loop(0, i - 1, far, 0)

        @pl.when(i >= 1)
        def _():
            for rel in range(-per_q, -1):
                fn(first_own + rel, None)
            fn(first_own - 1, -1)

        for rel in range(per_q):
            fn(first_own + rel, rel)

    def score_chunk(c, rel):
        kic = ki_ref[pl.ds(pl.multiple_of(c * tk, tk), tk), :]
        score = jnp.zeros((tk, tq), F32)
        for hh in range(N_HEADS):
            score = score + wit_ref[hh:hh + 1, :] * jnp.maximum(_dot(kic, qit_ref[hh]), 0.0)
        keys = _sort_key(score)
        if rel is not None and rel >= 0:
            keys = jnp.where(causal(rel), keys, INT_MIN)
        key_scr[c] = keys
        hi_scr[c] = (keys >> HALF_BITS).astype(I16)
        lo_scr[c] = ((keys & (2 * HALF_BIAS - 1)) - HALF_BIAS).astype(I16)

    for_chunks(score_chunk)

    def count(pred):
        def body(j, part):
            for u in range(per_q):
                part = part + _fold_rows(jnp.where(pred(key_scr[j * per_q + u]), 1, 0), jnp.add)
            return part
        part = lax.fori_loop(0, i + 1, body, jnp.zeros((SUBLANES, tq), I32))
        return jnp.sum(part, axis=0, keepdims=True)

    def count16(scr, pred):
        one, zero = jnp.int16(1), jnp.int16(0)

        def block(j, part):
            for u in range(per_q):
                hit = jnp.where(pred(scr[j * per_q + u]), one, zero)
                part = part + functools.reduce(
                    jnp.add, [hit[g * PACKED_SUBLANES:(g + 1) * PACKED_SUBLANES, :]
                              for g in range(tk // PACKED_SUBLANES)])
            return part

        nblk = i + 1
        part = lax.fori_loop(0, nblk // 2, lambda jj, part: block(2 * jj + 1, block(2 * jj, part)),
                             jnp.zeros((PACKED_SUBLANES, tq), I16))
        part = lax.cond(nblk % 2 == 1, lambda p: block(nblk - 1, p), lambda p: p, part)
        return jnp.sum(part.astype(I32), axis=0, keepdims=True)

    def search16(scr, base):
        def bit_body(bi, t_u):
            cand_u = t_u | jnp.left_shift(jnp.int32(1), HALF_BITS - 1 - bi)
            cand = (cand_u - HALF_BIAS).astype(I16)
            return jnp.where(base + count16(scr, lambda k: k >= cand) >= ktop, cand_u, t_u)
        return lax.fori_loop(0, HALF_BITS, bit_body, jnp.zeros((1, tq), I32))

    def find_threshold():
        t_hi = search16(hi_scr, 0) - HALF_BIAS
        t_hi16 = t_hi.astype(I16)
        above = count16(hi_scr, lambda k: k > t_hi16)

        def keep_band(j, carry):
            for u in range(per_q):
                c = j * per_q + u
                lo_scr[c] = jnp.where(hi_scr[c] == t_hi16, lo_scr[c], jnp.int16(-HALF_BIAS))
            return carry

        lax.fori_loop(0, i + 1, keep_band, 0)
        return t_hi * (2 * HALF_BIAS) + search16(lo_scr, above)

    thr = lax.cond((i + 1) * tq <= ktop, lambda: jnp.full((1, tq), INT_MIN, I32), find_threshold)
    n_ge = count(lambda k: k >= thr)

    @pl.when(jnp.max(n_ge) > ktop)
    def _():
        need = (ktop - count(lambda k: k > thr)).astype(F32)
        lower = (lax.broadcasted_iota(I32, (tk, tk), 1)
                 < lax.broadcasted_iota(I32, (tk, tk), 0)).astype(BF16)

        def body(c, seen):
            k = key_scr[c]
            eq = k == thr
            eqf = eq.astype(BF16)
            rank = seen + _dot(lower, eqf)
            drop = eq & (rank >= need) & (n_ge > ktop)
            key_scr[c] = jnp.where(drop, thr - 1, k)
            return seen + jnp.sum(eqf.astype(F32), axis=0, keepdims=True)

        lax.fori_loop(0, nchunks, body, jnp.zeros((1, tq), F32))

    def logits(c, rel):
        sel = key_scr[c] >= thr
        if rel is not None and rel >= 0:
            sel = sel & causal(rel)
        kbc = kb_ref[pl.ds(pl.multiple_of(c * tk, tk), tk), :]

        def head(hh):
            s = _dot(kbc, qbt_ref[hh])
            if rel is not None:
                s = s + bias_ref[rel + 1, hh]
            return s
        return sel, head

    m_scr[...] = jnp.full(m_scr.shape, NEG_BIG, F32)

    def max_chunk(c, rel):
        sel, head = logits(c, rel)
        for hh in range(N_HEADS):
            s = jnp.where(sel, head(hh), NEG_BIG)
            s_scr[hh, c] = s
            m_scr[hh] = jnp.maximum(m_scr[hh], _fold_rows(s, jnp.maximum))

    for_chunks(max_chunk)
    m = [jnp.max(m_scr[hh], axis=0, keepdims=True) for hh in range(N_HEADS)]

    acc_scr[...] = jnp.zeros(acc_scr.shape, F32)

    def att_block(j, carry):
        for u in range(per_q):
            c = j * per_q + u
            vtc = vt_ref[c]
            for hh in range(N_HEADS):
                p = jnp.exp(s_scr[hh, c] - m[hh])
                acc_scr[hh] += _dot(vtc, p.astype(BF16))
        return carry

    lax.fori_loop(0, i + 1, att_block, 0)

    out_t = jnp.concatenate(
        [acc_scr[hh, :HEAD_DIM, :] / acc_scr[hh, HEAD_DIM:HEAD_DIM + 1, :] for hh in range(N_HEADS)],
        axis=0)
    o_ref[...] = out_t.T.astype(o_ref.dtype)


def _dsa_attention(qit, ki, wit, qbt, kb, vt, bias):
    b, nh, dh, s = qit.shape
    tq = DSA_QUERY_BLOCK
    tk = DSA_KEY_BLOCK
    assert tk == REL_MAX_DIST and tk == LANES and tq % tk == 0 and s % tq == 0
    ktop = min(INDEX_TOPK_MAX, s // 4)
    hmt_q = pl.BlockSpec((None, nh, dh, tq), lambda bb, i: (bb, 0, 0, i))
    full64 = pl.BlockSpec((None, s, dh), lambda bb, i: (bb, 0, 0))
    return pl.pallas_call(
        functools.partial(_dsa_kernel, ktop),
        grid=(b, s // tq),
        in_specs=[hmt_q, full64,
                  pl.BlockSpec((None, nh, tq), lambda bb, i: (bb, 0, i)),
                  hmt_q, full64,
                  pl.BlockSpec((None, s // tk, 2 * dh, tk), lambda bb, i: (bb, 0, 0, 0)),
                  pl.BlockSpec(bias.shape, lambda bb, i: (0, 0, 0, 0))],
        out_specs=pl.BlockSpec((None, tq, nh * dh), lambda bb, i: (bb, i, 0)),
        out_shape=jax.ShapeDtypeStruct((b, s, nh * dh), BF16),
        scratch_shapes=[pltpu.VMEM((s // tk, tk, tq), I32),
                        pltpu.VMEM((s // tk, tk, tq), I16),
                        pltpu.VMEM((s // tk, tk, tq), I16),
                        pltpu.VMEM((nh, SUBLANES, tq), F32),
                        pltpu.VMEM((nh, 2 * dh, tq), F32),
                        pltpu.VMEM((nh, s // tk, tk, tq), F32)],
        compiler_params=pltpu.CompilerParams(
            dimension_semantics=("parallel", "parallel"), vmem_limit_bytes=VMEM_LIMIT),
        name="sparse_attention",
    )(qit, ki, wit, qbt, kb, vt, bias)


def _first_max4(a):
    m1 = jnp.maximum(jnp.maximum(a[0], a[1]), jnp.maximum(a[2], a[3]))
    i1 = jnp.where(a[0] == m1, 0, jnp.where(a[1] == m1, 1, jnp.where(a[2] == m1, 2, 3)))
    rest = [jnp.where(i1 == j, -1.0, a[j]) for j in range(4)]
    m2 = jnp.maximum(jnp.maximum(rest[0], rest[1]), jnp.maximum(rest[2], rest[3]))
    i2 = jnp.where(rest[0] == m2, 0, jnp.where(rest[1] == m2, 1, jnp.where(rest[2] == m2, 2, 3)))
    return m1, i1, m2, i2


def _route(lt):
    lg = [lt[e:e + 1, :] for e in range(N_EXPERTS)]
    mx = functools.reduce(jnp.maximum, lg)
    ex = [jnp.exp(v - mx) for v in lg]
    tot = functools.reduce(lambda u, v: u + v, ex)
    p = [v / tot for v in ex]
    gs = []
    for g in range(N_GROUPS):
        m1, _, m2, _ = _first_max4(p[4 * g:4 * g + 4])
        gs.append(m1 + m2)
    _, gbest, _, _ = _first_max4(gs)
    chosen = [jnp.where(gbest == 0, p[j], jnp.where(gbest == 1, p[4 + j],
              jnp.where(gbest == 2, p[8 + j], p[12 + j]))) for j in range(4)]
    m1, i1, m2, i2 = _first_max4(chosen)
    den = m1 + m2
    lo = jnp.minimum(i1, i2)
    hi = jnp.maximum(i1, i2)
    pair = jnp.where(lo == 0, hi - 1, jnp.where(lo == 1, hi + 1, N_PAIRS - 1))
    first_is_lo = i1 < i2
    w1 = m1 / den
    w2 = m2 / den
    return gbest * N_PAIRS + pair, jnp.where(first_is_lo, w1, w2), jnp.where(first_is_lo, w2, w1)


def _merge_kernel(oa_ref, ob_ref, sga_ref, sgb_ref, x_ref, mod_ref, woa_ref, wob_ref, wout_ref,
                  g2_ref, rw2_ref, rb_ref, x1_ref, h2x_ref, info_ref, cnt_ref, carry_scr):
    tm, d = x_ref.shape

    @pl.when((pl.program_id(0) == 0) & (pl.program_id(1) == 0))
    def _():
        carry_scr[...] = jnp.zeros(carry_scr.shape, F32)

    merged = (sga_ref[...].astype(F32) * _dot(oa_ref[...], woa_ref[...])
              + sgb_ref[...].astype(F32) * _dot(ob_ref[...], wob_ref[...]))
    x1 = x_ref[...] + mod_ref[2:3, :] * _dot(merged.astype(BF16), wout_ref[...])
    x1_ref[...] = x1
    h2 = (_rms(x1) * g2_ref[...]) * (1.0 + mod_ref[4:5, :]) + mod_ref[3:4, :]
    h2x_ref[:, :d] = h2
    hi = h2.astype(BF16)
    lo = (h2 - hi.astype(F32)).astype(BF16)
    both = _dot(hi, rw2_ref[...])
    logits = (both[:, :LANES] + _dot(lo, rw2_ref[:, :LANES]) + both[:, LANES:]) + rb_ref[...]
    cls, cw_lo, cw_hi = _route(logits.T)

    ncls = carry_scr.shape[0]
    onehot = (lax.broadcasted_iota(I32, (ncls, tm), 0) == cls).astype(F32)
    earlier = (lax.broadcasted_iota(I32, (tm, tm), 0)
               < lax.broadcasted_iota(I32, (tm, tm), 1)).astype(BF16)
    before = _dot(onehot.astype(BF16), earlier)
    carry = carry_scr[...]
    carry_t = jnp.concatenate([carry] * (tm // LANES), axis=1)
    rank = jnp.sum(onehot * (carry_t + before), axis=0, keepdims=True).astype(I32)
    carry = carry + jnp.sum(onehot, axis=1, keepdims=True)
    carry_scr[...] = carry
    cnt_ref[...] = carry.astype(I32)

    sub = lax.broadcasted_iota(I32, (SUBLANES, tm), 0)
    info_ref[...] = jnp.where(sub == 0, cls, jnp.where(sub == 1, rank, 0))
    sub = lax.broadcasted_iota(I32, (LANES, tm), 0)
    weights_t = jnp.where(sub == 0, cw_lo, jnp.where(sub == 1, cw_hi, 0.0))
    h2x_ref[:, d:] = weights_t.T


def _merge(oa, ob, sga, sgb, x, mod_l, woa, wob, wout, g2, rw2, rb, tm):
    b, s, d = x.shape
    hw = oa.shape[-1]
    nt = s // tm
    tok = lambda n: pl.BlockSpec((None, tm, n), lambda bb, i: (bb, i, 0))
    const = lambda r, c: pl.BlockSpec((r, c), lambda bb, i: (0, 0))
    return pl.pallas_call(
        _merge_kernel,
        grid=(b, nt),
        in_specs=[tok(hw), tok(hw), tok(d), tok(d), tok(d),
                  pl.BlockSpec((None, N_MOD, d), lambda bb, i: (bb, 0, 0)),
                  const(hw, d), const(hw, d), const(d, d), const(1, d),
                  const(d, 2 * LANES), const(1, LANES)],
        out_specs=[tok(d), tok(d + LANES),
                   pl.BlockSpec((SUBLANES, tm), lambda bb, i: (0, bb * nt + i)),
                   const(N_CLASS_PAD, LANES)],
        out_shape=[jax.ShapeDtypeStruct((b, s, d), F32),
                   jax.ShapeDtypeStruct((b, s, d + LANES), F32),
                   jax.ShapeDtypeStruct((SUBLANES, b * s), I32),
                   jax.ShapeDtypeStruct((N_CLASS_PAD, LANES), I32)],
        scratch_shapes=[pltpu.VMEM((N_CLASS_PAD, LANES), F32)],
        compiler_params=pltpu.CompilerParams(
            dimension_semantics=("arbitrary", "arbitrary"), vmem_limit_bytes=VMEM_LIMIT),
        name="merge_route",
    )(oa, ob, sga, sgb, x, mod_l, woa, wob, wout, g2, rw2, rb)


def _positions_kernel(offs_ref, info_ref, pos_ref):
    cls = info_ref[0:1, :]
    pos = info_ref[1:2, :]
    for c in range(N_CLASSES):
        pos = pos + jnp.where(cls == c, offs_ref[c], 0)
    pos_ref[...] = jnp.broadcast_to(pos, pos_ref.shape)


def _positions(info, offs):
    return pl.pallas_call(
        _positions_kernel,
        in_specs=[pl.BlockSpec(memory_space=pltpu.SMEM), pl.BlockSpec(info.shape, lambda: (0, 0))],
        out_specs=pl.BlockSpec(info.shape, lambda: (0, 0)),
        out_shape=jax.ShapeDtypeStruct(info.shape, I32),
        name="positions",
    )(offs, info)[0]


def _dispatch_kernel(ends_ref, pos_ref, src_ref, dst_ref, zero_scr, sem, zero_sem):
    tm = pos_ref.shape[1]
    row_tile = zero_scr.shape[0]

    @pl.when(pl.program_id(0) == 0)
    def _():
        zero_scr[...] = jnp.zeros(zero_scr.shape, F32)

        def zero_tail(c):
            start = pl.multiple_of(ends_ref[c] - row_tile, row_tile)
            return pltpu.make_async_copy(zero_scr, dst_ref.at[pl.ds(start, row_tile), :], zero_sem)

        def nonempty(c):
            return ends_ref[c] > (ends_ref[c - 1] if c else 0)

        for c in range(N_CLASSES):
            @pl.when(nonempty(c))
            def _(c=c):
                zero_tail(c).start()
        for c in range(N_CLASSES):
            @pl.when(nonempty(c))
            def _(c=c):
                zero_tail(c).wait()

        def unused_tile(j):
            start = pl.multiple_of(ends_ref[N_CLASSES - 1] + j * row_tile, row_tile)
            return pltpu.make_async_copy(zero_scr, dst_ref.at[pl.ds(start, row_tile), :], zero_sem)

        n_unused = (dst_ref.shape[0] - ends_ref[N_CLASSES - 1]) // row_tile
        lax.fori_loop(0, n_unused, lambda j, carry: (unused_tile(j).start(), carry)[1], 0)
        lax.fori_loop(0, n_unused, lambda j, carry: (unused_tile(j).wait(), carry)[1], 0)

    for r in range(tm):
        pltpu.make_async_copy(src_ref.at[pl.ds(r, 1), :], dst_ref.at[pl.ds(pos_ref[0, r], 1), :],
                              sem).start(priority=r % 2)
    pltpu.make_async_copy(src_ref, dst_ref.at[pl.ds(0, tm), :], sem).wait()


def _dispatch(h2x, pos, ends, n_rows):
    t, w = h2x.shape
    tm = DISPATCH_TILE
    pos = pos.reshape(t // tm, 1, tm)
    smem_row = pl.BlockSpec((None, 1, tm), lambda i: (i, 0, 0), memory_space=pltpu.SMEM)
    return pl.pallas_call(
        _dispatch_kernel,
        grid=(t // tm,),
        in_specs=[pl.BlockSpec(memory_space=pltpu.SMEM), smem_row, pl.BlockSpec((tm, w), lambda i: (i, 0))],
        out_specs=pl.BlockSpec(memory_space=pl.ANY),
        out_shape=jax.ShapeDtypeStruct((n_rows, w), F32),
        scratch_shapes=[pltpu.VMEM((MOE_ROW_TILE, w), F32), pltpu.SemaphoreType.DMA, pltpu.SemaphoreType.DMA],
        compiler_params=pltpu.CompilerParams(
            dimension_semantics=("arbitrary",), vmem_limit_bytes=VMEM_LIMIT),
        name="dispatch",
    )(ends, pos, h2x)


def _experts_kernel(elo_ref, ehi_ref, live_ref, xs_ref, wg_lo, wg_hi, wu_lo, wu_hi, wd_lo, wd_hi, ys_ref):
    del elo_ref, ehi_ref
    d = ys_ref.shape[1]
    j = pl.program_id(0)

    @pl.when(live_ref[j] == 1)
    def _():
        h = xs_ref[:, :d].astype(BF16)

        def expert(wg, wu, wd, weight):
            gate = _dot(h, wg[...])
            hidden = ((gate * jax.nn.sigmoid(gate)) * _dot(h, wu[...])) * weight
            return _dot(hidden.astype(BF16), wd[...])

        ys_ref[...] = (expert(wg_lo, wu_lo, wd_lo, xs_ref[:, d:d + 1])
                       + expert(wg_hi, wu_hi, wd_hi, xs_ref[:, d + 1:d + 2]))

    @pl.when(live_ref[j] == 0)
    def _():
        ys_ref[...] = jnp.zeros(ys_ref.shape, F32)


def _experts(xs, e_lo, e_hi, live, wg, wu, wd, layer):
    n_rows, w = xs.shape
    _, ne, d, f = wg.shape
    tm = MOE_ROW_TILE
    lo = lambda j, elo, ehi, lv: (layer, elo[j], 0, 0)
    hi = lambda j, elo, ehi, lv: (layer, ehi[j], 0, 0)
    grid_spec = pltpu.PrefetchScalarGridSpec(
        num_scalar_prefetch=3,
        grid=(n_rows // tm,),
        in_specs=[pl.BlockSpec((tm, w), lambda j, elo, ehi, lv: (j, 0)),
                  pl.BlockSpec((None, None, d, f), lo), pl.BlockSpec((None, None, d, f), hi),
                  pl.BlockSpec((None, None, d, f), lo), pl.BlockSpec((None, None, d, f), hi),
                  pl.BlockSpec((None, None, f, d), lo), pl.BlockSpec((None, None, f, d), hi)],
        out_specs=pl.BlockSpec((tm, d), lambda j, elo, ehi, lv: (j, 0)),
    )
    return pl.pallas_call(
        _experts_kernel,
        grid_spec=grid_spec,
        out_shape=jax.ShapeDtypeStruct((n_rows, d), F32),
        compiler_params=pltpu.CompilerParams(
            dimension_semantics=("arbitrary",), vmem_limit_bytes=VMEM_LIMIT),
        name="experts",
    )(e_lo, e_hi, live, xs, wg, wg, wu, wu, wd, wd)


def _combine_kernel(pos_ref, next_ref, ys_ref, x1_ref, mod_ref, o_ref, buf, sem):
    tm = buf.shape[1]
    step = pl.program_id(0) * pl.num_programs(1) + pl.program_id(1)
    last = pl.num_programs(0) * pl.num_programs(1) - 1

    def issue(p_ref, slot):
        for r in range(tm):
            pltpu.make_async_copy(ys_ref.at[pl.ds(p_ref[slot, r], 1), :], buf.at[slot, pl.ds(r, 1), :],
                                  sem.at[slot]).start(priority=r % 2)

    def finish(slot):
        pltpu.make_async_copy(ys_ref.at[pl.ds(0, tm), :], buf.at[slot], sem.at[slot]).wait()
        rows = pl.ds(slot * tm, tm)
        o_ref[rows, :] = x1_ref[rows, :] + mod_ref[5:6, :] * buf[slot]

    @pl.when(step == 0)
    def _():
        issue(pos_ref, 0)

    issue(pos_ref, 1)
    finish(0)

    @pl.when(step < last)
    def _():
        issue(next_ref, 0)

    finish(1)


def _combine(ys, pos, x1, mod_l):
    b, s, d = x1.shape
    tm = MOE_DMA_TILE
    assert s % (2 * tm) == 0
    nt = s // (2 * tm)
    n_steps = b * nt
    pos2 = pos.reshape(n_steps, 2, tm)
    this_step = pl.BlockSpec((None, 2, tm), lambda bb, i: (bb * nt + i, 0, 0), memory_space=pltpu.SMEM)
    next_step = pl.BlockSpec((None, 2, tm), lambda bb, i: (jnp.minimum(bb * nt + i + 1, n_steps - 1), 0, 0),
                             memory_space=pltpu.SMEM)
    return pl.pallas_call(
        _combine_kernel,
        grid=(b, nt),
        in_specs=[this_step, next_step,
                  pl.BlockSpec(memory_space=pl.ANY),
                  pl.BlockSpec((None, 2 * tm, d), lambda bb, i: (bb, i, 0)),
                  pl.BlockSpec((None, N_MOD, d), lambda bb, i: (bb, 0, 0))],
        out_specs=pl.BlockSpec((None, 2 * tm, d), lambda bb, i: (bb, i, 0)),
        out_shape=jax.ShapeDtypeStruct((b, s, d), F32),
        scratch_shapes=[pltpu.VMEM((2, tm, d), F32), pltpu.SemaphoreType.DMA((2,))],
        compiler_params=pltpu.CompilerParams(
            dimension_semantics=("arbitrary", "arbitrary"), vmem_limit_bytes=VMEM_LIMIT),
        name="combine",
    )(pos2, pos2, ys, x1, mod_l)


def _moe(h2x, info, counts, wg, wu, wd, layer, x1, mod_l):
    b, s, d = x1.shape
    t = b * s
    tm = MOE_ROW_TILE
    n_tiles = t // tm + N_CLASSES
    counts = counts[:N_CLASSES, 0]
    padded = (counts + tm - 1) // tm * tm
    ends = jnp.cumsum(padded)
    offs = jnp.pad(ends - padded, (0, N_CLASS_PAD - N_CLASSES)).astype(I32)
    tile_cls = jnp.sum(jnp.arange(n_tiles, dtype=I32)[:, None] * tm >= ends[None, :], axis=1)
    live = (tile_cls < N_CLASSES).astype(I32)
    tile_cls = jnp.minimum(tile_cls, N_CLASSES - 1)
    group, pair = tile_cls // N_PAIRS, tile_cls % N_PAIRS
    pairs = jnp.asarray(_PAIRS, I32)
    e_lo = group * EXPERTS_PER_GROUP + pairs[pair, 0]
    e_hi = group * EXPERTS_PER_GROUP + pairs[pair, 1]

    pos = _positions(info, offs).reshape(t // MOE_DMA_TILE, 1, MOE_DMA_TILE)
    xs = _dispatch(h2x.reshape(t, d + LANES), pos, ends.astype(I32), n_tiles * tm)
    ys = _experts(xs, e_lo, e_hi, live, wg, wu, wd, layer)
    return _combine(ys, pos, x1, mod_l)


def kernel(x, c, w_ada, b_ada, norm1_g, w_in, q_norm_g, k_norm_g, rel_bias, w_o_a, w_o_b, w_out,
           norm2_g, router_w, router_b, w_gate, w_up, w_down):
    b, s, d = x.shape
    depth = w_ada.shape[0]
    assert d % LANES == 0
    tm = min(TOKEN_TILE, s)

    mod = _adaln(c, w_ada, b_ada)
    bias = _bias_tiles(rel_bias, DSA_QUERY_BLOCK, DSA_KEY_BLOCK)
    rw = jnp.pad(router_w, ((0, 0), (0, LANES - N_EXPERTS)))
    rwh = rw.astype(BF16)
    rw2 = jnp.concatenate([rwh, (rw - rwh.astype(F32)).astype(BF16)], axis=1)
    rb = jnp.pad(router_b, (0, LANES - N_EXPERTS)).reshape(1, LANES)
    wg, wu, wd = w_gate.astype(BF16), w_up.astype(BF16), w_down.astype(BF16)

    for l in range(depth):
        qa, ka, va, qbt, qit, kb, vt, ki, wit, sga, sgb = _in_proj(
            x, mod[l], norm1_g[l].reshape(1, d), _pack_w_in(w_in[l]),
            q_norm_g[l].reshape(HEAD_DIM, 1), k_norm_g[l].reshape(1, HEAD_DIM), tm)
        oa = _sb_attention(qa, ka, va)
        ob = _dsa_attention(qit, ki, wit, qbt, kb, vt, bias)
        x1, h2x, info, counts = _merge(oa, ob, sga, sgb, x, mod[l], w_o_a[l].astype(BF16),
                                       w_o_b[l].astype(BF16), w_out[l].astype(BF16),
                                       norm2_g[l].reshape(1, d), rw2, rb, tm)
        x = _moe(h2x, info, counts, wg, wu, wd, l, x1, mod[l])
    return x
```

```python
import functools
import math

import numpy as np
import jax
import jax.numpy as jnp
from jax import lax
from jax.experimental import pallas as pl
from jax.experimental.pallas import tpu as pltpu

F32 = jnp.float32
BF16 = jnp.bfloat16
I32 = jnp.int32
I16 = jnp.int16

HEAD_DIM = 64
N_HEADS = 8
N_EXPERTS = 16
N_GROUPS = 4
EXPERTS_PER_GROUP = 4
_PAIRS = ((0, 1), (0, 2), (0, 3), (1, 2), (1, 3), (2, 3))
N_PAIRS = len(_PAIRS)
N_CLASSES = N_GROUPS * N_PAIRS
N_CLASS_PAD = 32
MOE_ROW_TILE = 256
EXPERT_TILES_PER_STEP = 2
MOE_DMA_TILE = 512
DISPATCH_TILE = 1024
N_MOD = 6
EPS = 1e-6
INDEX_TOPK_MAX = 256
REL_BUCKETS = 32
REL_MAX_DIST = 128
LANES = 128
SUBLANES = 8
PACKED_SUBLANES = 16
HALF_BITS = 16
TOKEN_TILE = 512
EXP2_MAX = 126.0
DSA_QUERY_BLOCK = 256
DSA_KEY_BLOCK = 128
SB_QUERY_BLOCK = 256
SB_KEY_BLOCK = 256
SB_DEAD_LOG2 = 150.0
LOG2E = 1.4426950408889634
NEG_BIG = -1e30
INT_MIN = -2 ** 31
INT_MAX = 2 ** 31 - 1
HALF_BIAS = 2 ** 15

VMEM_LIMIT = 56 * 1024 * 1024


def _nt_dot(a, b):
    return lax.dot_general(a, b, (((1,), (1,)), ((), ())), preferred_element_type=F32)


def _dot(a, b):
    return jnp.dot(a, b, preferred_element_type=F32)


def _mod_kernel(c_ref, w_ref, b_ref, o_ref):
    c = c_ref[...]
    sc = c * jax.nn.sigmoid(c)
    o_ref[...] = _dot(sc.astype(BF16), w_ref[...].astype(BF16)) + b_ref[...]


def _adaln(c, w_ada, b_ada):
    depth, d, nd = w_ada.shape
    b = c.shape[0]
    out = pl.pallas_call(
        _mod_kernel,
        grid=(depth, nd // (2 * d)),
        in_specs=[pl.BlockSpec((b, d), lambda l, j: (0, 0)),
                  pl.BlockSpec((None, d, 2 * d), lambda l, j: (l, 0, j)),
                  pl.BlockSpec((None, 1, 2 * d), lambda l, j: (l, 0, j))],
        out_specs=pl.BlockSpec((None, b, 2 * d), lambda l, j: (l, 0, j)),
        out_shape=jax.ShapeDtypeStruct((depth, b, nd), F32),
        compiler_params=pltpu.CompilerParams(vmem_limit_bytes=VMEM_LIMIT),
        name="adaln",
    )(c, w_ada, b_ada.reshape(depth, 1, nd))
    return out.reshape(depth, b, N_MOD, d)


def _rms(x):
    return x * lax.rsqrt(jnp.mean(x * x, axis=-1, keepdims=True) + EPS)


def _in_kernel(x_ref, mod_ref, g1_ref, w_ref, qng_ref, kng_ref,
               qa_ref, ka_ref, va_ref, qbt_ref, qit_ref, kb_ref, vt_ref, ki_ref, wit_ref,
               sga_ref, sgb_ref):
    d = x_ref.shape[-1]
    tm = x_ref.shape[0]
    hw = N_HEADS * HEAD_DIM
    x = x_ref[...]
    h = (_rms(x) * g1_ref[...]) * (1.0 + mod_ref[1:2, :]) + mod_ref[0:1, :]
    hb = h.astype(BF16)

    def proj(lo, n):
        return _dot(hb, w_ref[:, lo:lo + n])

    def heads(r, ref):
        for hh in range(N_HEADS):
            ref[hh] = r[:, hh * HEAD_DIM:(hh + 1) * HEAD_DIM].astype(ref.dtype)

    scale = HEAD_DIM ** -0.5
    heads(proj(0, hw) * (scale * LOG2E), qa_ref)
    heads(proj(hw, hw), ka_ref)
    heads(proj(2 * hw, hw), va_ref)

    rt = proj(3 * hw, hw).T
    for hh in range(N_HEADS):
        slab = rt[hh * HEAD_DIM:(hh + 1) * HEAD_DIM, :]
        inv = lax.rsqrt(jnp.mean(slab * slab, axis=0, keepdims=True) + EPS)
        qbt_ref[hh] = (((slab * inv) * qng_ref[...]) * scale).astype(BF16)

    r = proj(4 * hw, 2 * HEAD_DIM)
    kb_ref[...] = (_rms(r[:, :HEAD_DIM]) * kng_ref[...]).astype(BF16)
    vt = r.T[HEAD_DIM:, :]
    ones_row = (lax.broadcasted_iota(I32, (HEAD_DIM, LANES), 0) == 0).astype(BF16)
    for cc in range(tm // LANES):
        vt_ref[cc, :HEAD_DIM, :] = vt[:, cc * LANES:(cc + 1) * LANES].astype(BF16)
        vt_ref[cc, HEAD_DIM:, :] = ones_row

    rt = proj(4 * hw + LANES, hw).T
    for hh in range(N_HEADS):
        qit_ref[hh] = rt[hh * HEAD_DIM:(hh + 1) * HEAD_DIM, :].astype(BF16)

    r = proj(5 * hw + LANES, LANES)
    ki_ref[...] = r[:, :HEAD_DIM].astype(BF16)
    wit_ref[...] = r.T[HEAD_DIM:HEAD_DIM + N_HEADS, :]

    off = 5 * hw + 2 * LANES
    sga_ref[...] = jax.nn.sigmoid(proj(off, d)).astype(BF16)
    sgb_ref[...] = jax.nn.sigmoid(proj(off + d, d)).astype(BF16)


def _pack_w_in(w_in_l):
    hw = N_HEADS * HEAD_DIM
    a = 4 * hw + 2 * HEAD_DIM + hw
    small = w_in_l[:, a:a + HEAD_DIM + N_HEADS]
    small = jnp.pad(small, ((0, 0), (0, LANES - small.shape[1])))
    gates = w_in_l[:, a + HEAD_DIM + N_HEADS:]
    return jnp.concatenate([w_in_l[:, :a], small, gates], axis=1).astype(BF16)


def _in_proj(x, mod_l, g1, w_packed, qng, kng, tm):
    b, s, d = x.shape
    hm = jax.ShapeDtypeStruct((b, N_HEADS, s, HEAD_DIM), BF16)
    hmt = jax.ShapeDtypeStruct((b, N_HEADS, HEAD_DIM, s), BF16)
    tok64 = jax.ShapeDtypeStruct((b, s, HEAD_DIM), BF16)
    hm_spec = pl.BlockSpec((None, N_HEADS, tm, HEAD_DIM), lambda bb, i: (bb, 0, i, 0))
    hmt_spec = pl.BlockSpec((None, N_HEADS, HEAD_DIM, tm), lambda bb, i: (bb, 0, 0, i))
    tok = lambda n: pl.BlockSpec((None, tm, n), lambda bb, i: (bb, i, 0))
    nw = w_packed.shape[1]
    return pl.pallas_call(
        _in_kernel,
        grid=(b, s // tm),
        in_specs=[tok(d),
                  pl.BlockSpec((None, N_MOD, d), lambda bb, i: (bb, 0, 0)),
                  pl.BlockSpec((1, d), lambda bb, i: (0, 0)),
                  pl.BlockSpec((d, nw), lambda bb, i: (0, 0)),
                  pl.BlockSpec((HEAD_DIM, 1), lambda bb, i: (0, 0)),
                  pl.BlockSpec((1, HEAD_DIM), lambda bb, i: (0, 0))],
        out_specs=[hm_spec] * 3 + [hmt_spec] * 2
                  + [tok(HEAD_DIM),
                     pl.BlockSpec((None, tm // LANES, 2 * HEAD_DIM, LANES), lambda bb, i: (bb, i, 0, 0)),
                     tok(HEAD_DIM),
                     pl.BlockSpec((None, N_HEADS, tm), lambda bb, i: (bb, 0, i)),
                     tok(d), tok(d)],
        out_shape=[hm] * 3 + [hmt] * 2
                  + [tok64,
                     jax.ShapeDtypeStruct((b, s // LANES, 2 * HEAD_DIM, LANES), BF16),
                     tok64,
                     jax.ShapeDtypeStruct((b, N_HEADS, s), F32),
                     jax.ShapeDtypeStruct((b, s, d), BF16),
                     jax.ShapeDtypeStruct((b, s, d), BF16)],
        compiler_params=pltpu.CompilerParams(
            dimension_semantics=("parallel", "parallel"), vmem_limit_bytes=VMEM_LIMIT),
        name="in_proj",
    )(x, mod_l, g1, w_packed, qng, kng)


def _sb_kernel(q_ref, k_ref, v_ref, o_ref, acc_scr, csum_scr, sp_scr, ls_scr):
    tq = q_ref.shape[1]
    tk = SB_KEY_BLOCK
    i = pl.program_id(1)
    nfull = (i * tq) // tk
    row = lax.broadcasted_iota(I32, (tq, tk), 0)
    col = lax.broadcasted_iota(I32, (tq, tk), 1)
    before = (col - row) < (i * tq - nfull * tk)
    krow = lax.broadcasted_iota(I32, (tk, tk), 0)
    kcol = lax.broadcasted_iota(I32, (tk, tk), 1)
    suffix = (krow > kcol).astype(BF16)

    def stage_a(hh, st, first):
        z2 = _nt_dot(q_ref[hh], k_ref[hh, pl.ds(st, tk), :])
        sp2 = jnp.maximum(jnp.log2(1.0 + jnp.exp2(jnp.minimum(z2, EXP2_MAX))), z2)
        if first:
            sp2 = jnp.where(before, sp2, 0.0)
        sp_scr[hh] = sp2.astype(BF16)
        ls_scr[hh] = z2 - sp2

    def stage_b(hh, st, first):
        sp = sp_scr[hh]
        suf = _dot(sp, suffix)
        if first:
            a = jnp.where(before, jnp.exp2(ls_scr[hh] - suf), 0.0)
            csum = jnp.zeros((tq, 1), F32)
        else:
            csum = csum_scr[hh]
            a = jnp.exp2((ls_scr[hh] - suf) - csum)
        c = _dot(a.astype(BF16), v_ref[hh, pl.ds(st, tk), :])
        if first:
            acc_scr[hh] = c
        else:
            acc_scr[hh] += c
        csum_scr[hh] = csum + (suf[:, 0:1] + sp[:, 0:1].astype(F32))

    def key_block(st, first):
        for hh in range(N_HEADS):
            stage_a(hh, st, first)
        for hh in range(N_HEADS):
            stage_b(hh, st, first)

    key_block(pl.multiple_of(nfull * tk, tk), True)

    def live():
        return jnp.min(csum_scr[...]) < SB_DEAD_LOG2

    def body(carry):
        jj, _ = carry
        key_block(pl.multiple_of((nfull - 1 - jj) * tk, tk), False)
        return jj + 1, live()

    lax.while_loop(lambda carry: (carry[0] < nfull) & carry[1], body, (jnp.int32(0), live()))
    for hh in range(N_HEADS):
        o_ref[:, hh * HEAD_DIM:(hh + 1) * HEAD_DIM] = acc_scr[hh].astype(o_ref.dtype)


def _sb_attention(qa, ka, va):
    b, nh, s, dh = qa.shape
    tq = SB_QUERY_BLOCK
    assert SB_KEY_BLOCK % tq == 0 and s % SB_KEY_BLOCK == 0
    return pl.pallas_call(
        _sb_kernel,
        grid=(b, s // tq),
        in_specs=[pl.BlockSpec((None, nh, tq, dh), lambda bb, i: (bb, 0, i, 0)),
                  pl.BlockSpec((None, nh, s, dh), lambda bb, i: (bb, 0, 0, 0)),
                  pl.BlockSpec((None, nh, s, dh), lambda bb, i: (bb, 0, 0, 0))],
        out_specs=pl.BlockSpec((None, tq, nh * dh), lambda bb, i: (bb, i, 0)),
        out_shape=jax.ShapeDtypeStruct((b, s, nh * dh), BF16),
        scratch_shapes=[pltpu.VMEM((nh, tq, dh), F32), pltpu.VMEM((nh, tq, 1), F32),
                        pltpu.VMEM((nh, tq, SB_KEY_BLOCK), BF16), pltpu.VMEM((nh, tq, SB_KEY_BLOCK), F32)],
        compiler_params=pltpu.CompilerParams(
            dimension_semantics=("parallel", "parallel"), vmem_limit_bytes=VMEM_LIMIT),
        name="stick_breaking",
    )(qa, ka, va)


def _t5_bucket_np(dist):
    max_exact = REL_BUCKETS // 2
    d_f = np.maximum(dist, 1).astype(np.float32)
    large = max_exact + (np.log(d_f / np.float32(max_exact))
                         / np.float32(math.log(REL_MAX_DIST / max_exact))
                         * np.float32(REL_BUCKETS - max_exact)).astype(np.int32)
    large = np.minimum(large, REL_BUCKETS - 1)
    return np.where(dist < max_exact, dist, large).astype(np.int32)


def _bias_tiles(rel_bias, tq, tk):
    n_near = tq // tk + 1
    key = np.arange(tk)[:, None]
    query = np.arange(tq)[None, :]
    bk = np.stack([_t5_bucket_np(np.maximum(query - key - (n - 1) * tk, 0)) for n in range(n_near)])
    far = int(_t5_bucket_np(np.array([REL_MAX_DIST]))[0])
    nh = rel_bias.shape[1]

    def body(rb_ref, bk_ref, o_ref):
        buckets = bk_ref[...]
        for hh in range(nh):
            tile = jnp.zeros((tk, tq), F32)
            for bucket in range(REL_BUCKETS):
                tile = jnp.where(buckets == bucket, rb_ref[hh, bucket] - rb_ref[hh, far], tile)
            o_ref[hh] = tile

    return pl.pallas_call(
        body,
        grid=(n_near,),
        in_specs=[pl.BlockSpec(memory_space=pltpu.SMEM),
                  pl.BlockSpec((None, tk, tq), lambda n: (n, 0, 0))],
        out_specs=pl.BlockSpec((None, nh, tk, tq), lambda n: (n, 0, 0, 0)),
        out_shape=jax.ShapeDtypeStruct((n_near, nh, tk, tq), F32),
        name="bias_tiles",
    )(rel_bias.T, jnp.asarray(bk))


def _sort_key(x):
    bits = pltpu.bitcast(x, I32)
    return bits ^ ((bits >> 31) & INT_MAX)


def _fold_rows(x, op):
    return functools.reduce(
        op, [x[g * SUBLANES:(g + 1) * SUBLANES, :] for g in range(x.shape[0] // SUBLANES)])


def _dsa_kernel(ktop, qit_ref, ki_ref, wit_ref, qbt_ref, kb_ref, vt_ref, bias_ref, o_ref,
                key_scr, hi_scr, lo_scr, m_scr, acc_scr, s_scr):
    tq = qit_ref.shape[2]
    tk = DSA_KEY_BLOCK
    per_q = tq // tk
    i = pl.program_id(1)
    first_own = per_q * i
    nchunks = per_q * (i + 1)
    key_i = lax.broadcasted_iota(I32, (tk, tq), 0)
    qry_i = lax.broadcasted_iota(I32, (tk, tq), 1)

    def causal(rel):
        return (key_i - qry_i) <= (-rel * tk)

    def for_chunks(fn):
        def far(j, carry):
            for u in range(per_q):
                fn(j * per_q + u, None)
            return carry
        lax.fori_loop(0, i - 1, far, 0)

        @pl.when(i >= 1)
        def _():
            for rel in range(-per_q, -1):
                fn(first_own + rel, None)
            fn(first_own - 1, -1)

        for rel in range(per_q):
            fn(first_own + rel, rel)

    def score_chunk(c, rel):
        kic = ki_ref[pl.ds(pl.multiple_of(c * tk, tk), tk), :]
        score = jnp.zeros((tk, tq), F32)
        for hh in range(N_HEADS):
            score = score + wit_ref[hh:hh + 1, :] * jnp.maximum(_dot(kic, qit_ref[hh]), 0.0)
        keys = _sort_key(score)
        if rel is not None and rel >= 0:
            keys = jnp.where(causal(rel), keys, INT_MIN)
        key_scr[c] = keys
        hi_scr[c] = (keys >> HALF_BITS).astype(I16)
        lo_scr[c] = ((keys & (2 * HALF_BIAS - 1)) - HALF_BIAS).astype(I16)

    for_chunks(score_chunk)

    def count(pred):
        def body(j, part):
            for u in range(per_q):
                part = part + _fold_rows(jnp.where(pred(key_scr[j * per_q + u]), 1, 0), jnp.add)
            return part
        part = lax.fori_loop(0, i + 1, body, jnp.zeros((SUBLANES, tq), I32))
        return jnp.sum(part, axis=0, keepdims=True)

    def count16(scr, pred):
        one, zero = jnp.int16(1), jnp.int16(0)

        def block(j, part):
            for u in range(per_q):
                hit = jnp.where(pred(scr[j * per_q + u]), one, zero)
                part = part + functools.reduce(
                    jnp.add, [hit[g * PACKED_SUBLANES:(g + 1) * PACKED_SUBLANES, :]
                              for g in range(tk // PACKED_SUBLANES)])
            return part

        nblk = i + 1
        part = lax.fori_loop(0, nblk // 2, lambda jj, part: block(2 * jj + 1, block(2 * jj, part)),
                             jnp.zeros((PACKED_SUBLANES, tq), I16))
        part = lax.cond(nblk % 2 == 1, lambda p: block(nblk - 1, p), lambda p: p, part)
        return jnp.sum(part.astype(I32), axis=0, keepdims=True)

    def search16(scr, base):
        def bit_body(bi, t_u):
            cand_u = t_u | jnp.left_shift(jnp.int32(1), HALF_BITS - 1 - bi)
            cand = (cand_u - HALF_BIAS).astype(I16)
            return jnp.where(base + count16(scr, lambda k: k >= cand) >= ktop, cand_u, t_u)
        return lax.fori_loop(0, HALF_BITS, bit_body, jnp.zeros((1, tq), I32))

    def find_threshold():
        t_hi = search16(hi_scr, 0) - HALF_BIAS
        t_hi16 = t_hi.astype(I16)
        above = count16(hi_scr, lambda k: k > t_hi16)

        def keep_band(j, carry):
            for u in range(per_q):
                c = j * per_q + u
                lo_scr[c] = jnp.where(hi_scr[c] == t_hi16, lo_scr[c], jnp.int16(-HALF_BIAS))
            return carry

        lax.fori_loop(0, i + 1, keep_band, 0)
        return t_hi * (2 * HALF_BIAS) + search16(lo_scr, above)

    thr = lax.cond((i + 1) * tq <= ktop, lambda: jnp.full((1, tq), INT_MIN, I32), find_threshold)
    n_ge = count(lambda k: k >= thr)

    @pl.when(jnp.max(n_ge) > ktop)
    def _():
        need = (ktop - count(lambda k: k > thr)).astype(F32)
        lower = (lax.broadcasted_iota(I32, (tk, tk), 1)
                 < lax.broadcasted_iota(I32, (tk, tk), 0)).astype(BF16)

        def body(c, seen):
            k = key_scr[c]
            eq = k == thr
            eqf = eq.astype(BF16)
            rank = seen + _dot(lower, eqf)
            drop = eq & (rank >= need) & (n_ge > ktop)
            key_scr[c] = jnp.where(drop, thr - 1, k)
            return seen + jnp.sum(eqf.astype(F32), axis=0, keepdims=True)

        lax.fori_loop(0, nchunks, body, jnp.zeros((1, tq), F32))

    def logits(c, rel):
        sel = key_scr[c] >= thr
        if rel is not None and rel >= 0:
            sel = sel & causal(rel)
        kbc = kb_ref[pl.ds(pl.multiple_of(c * tk, tk), tk), :]

        def head(hh):
            s = _dot(kbc, qbt_ref[hh])
            if rel is not None:
                s = s + bias_ref[rel + 1, hh]
            return s
        return sel, head

    m_scr[...] = jnp.full(m_scr.shape, NEG_BIG, F32)

    def max_chunk(c, rel):
        sel, head = logits(c, rel)
        for hh in range(N_HEADS):
            s = jnp.where(sel, head(hh), NEG_BIG)
            s_scr[hh, c] = s
            m_scr[hh] = jnp.maximum(m_scr[hh], _fold_rows(s, jnp.maximum))

    for_chunks(max_chunk)
    m = [jnp.max(m_scr[hh], axis=0, keepdims=True) for hh in range(N_HEADS)]

    acc_scr[...] = jnp.zeros(acc_scr.shape, F32)

    def att_block(j, carry):
        for u in range(per_q):
            c = j * per_q + u
            vtc = vt_ref[c]
            for hh in range(N_HEADS):
                p = jnp.exp(s_scr[hh, c] - m[hh])
                acc_scr[hh] += _dot(vtc, p.astype(BF16))
        return carry

    lax.fori_loop(0, i + 1, att_block, 0)

    out_t = jnp.concatenate(
        [acc_scr[hh, :HEAD_DIM, :] / acc_scr[hh, HEAD_DIM:HEAD_DIM + 1, :] for hh in range(N_HEADS)],
        axis=0)
    o_ref[...] = out_t.T.astype(o_ref.dtype)


def _dsa_attention(qit, ki, wit, qbt, kb, vt, bias):
    b, nh, dh, s = qit.shape
    tq = DSA_QUERY_BLOCK
    tk = DSA_KEY_BLOCK
    assert tk == REL_MAX_DIST and tk == LANES and tq % tk == 0 and s % tq == 0
    ktop = min(INDEX_TOPK_MAX, s // 4)
    hmt_q = pl.BlockSpec((None, nh, dh, tq), lambda bb, i: (bb, 0, 0, i))
    full64 = pl.BlockSpec((None, s, dh), lambda bb, i: (bb, 0, 0))
    return pl.pallas_call(
        functools.partial(_dsa_kernel, ktop),
        grid=(b, s // tq),
        in_specs=[hmt_q, full64,
                  pl.BlockSpec((None, nh, tq), lambda bb, i: (bb, 0, i)),
                  hmt_q, full64,
                  pl.BlockSpec((None, s // tk, 2 * dh, tk), lambda bb, i: (bb, 0, 0, 0)),
                  pl.BlockSpec(bias.shape, lambda bb, i: (0, 0, 0, 0))],
        out_specs=pl.BlockSpec((None, tq, nh * dh), lambda bb, i: (bb, i, 0)),
        out_shape=jax.ShapeDtypeStruct((b, s, nh * dh), BF16),
        scratch_shapes=[pltpu.VMEM((s // tk, tk, tq), I32),
                        pltpu.VMEM((s // tk, tk, tq), I16),
                        pltpu.VMEM((s // tk, tk, tq), I16),
                        pltpu.VMEM((nh, SUBLANES, tq), F32),
                        pltpu.VMEM((nh, 2 * dh, tq), F32),
                        pltpu.VMEM((nh, s // tk, tk, tq), F32)],
        compiler_params=pltpu.CompilerParams(
            dimension_semantics=("parallel", "parallel"), vmem_limit_bytes=VMEM_LIMIT),
        name="sparse_attention",
    )(qit, ki, wit, qbt, kb, vt, bias)


def _first_max4(a):
    m1 = jnp.maximum(jnp.maximum(a[0], a[1]), jnp.maximum(a[2], a[3]))
    i1 = jnp.where(a[0] == m1, 0, jnp.where(a[1] == m1, 1, jnp.where(a[2] == m1, 2, 3)))
    rest = [jnp.where(i1 == j, -1.0, a[j]) for j in range(4)]
    m2 = jnp.maximum(jnp.maximum(rest[0], rest[1]), jnp.maximum(rest[2], rest[3]))
    i2 = jnp.where(rest[0] == m2, 0, jnp.where(rest[1] == m2, 1, jnp.where(rest[2] == m2, 2, 3)))
    return m1, i1, m2, i2


def _route(lt):
    lg = [lt[e:e + 1, :] for e in range(N_EXPERTS)]
    mx = functools.reduce(jnp.maximum, lg)
    ex = [jnp.exp(v - mx) for v in lg]
    tot = functools.reduce(lambda u, v: u + v, ex)
    p = [v / tot for v in ex]
    gs = []
    for g in range(N_GROUPS):
        m1, _, m2, _ = _first_max4(p[4 * g:4 * g + 4])
        gs.append(m1 + m2)
    _, gbest, _, _ = _first_max4(gs)
    chosen = [jnp.where(gbest == 0, p[j], jnp.where(gbest == 1, p[4 + j],
              jnp.where(gbest == 2, p[8 + j], p[12 + j]))) for j in range(4)]
    m1, i1, m2, i2 = _first_max4(chosen)
    den = m1 + m2
    lo = jnp.minimum(i1, i2)
    hi = jnp.maximum(i1, i2)
    pair = jnp.where(lo == 0, hi - 1, jnp.where(lo == 1, hi + 1, N_PAIRS - 1))
    first_is_lo = i1 < i2
    w1 = m1 / den
    w2 = m2 / den
    return gbest * N_PAIRS + pair, jnp.where(first_is_lo, w1, w2), jnp.where(first_is_lo, w2, w1)


def _merge_kernel(oa_ref, ob_ref, sga_ref, sgb_ref, x_ref, mod_ref, woa_ref, wob_ref, wout_ref,
                  g2_ref, rw2_ref, rb_ref, x1_ref, h2x_ref, info_ref, cnt_ref, carry_scr):
    tm, d = x_ref.shape

    @pl.when((pl.program_id(0) == 0) & (pl.program_id(1) == 0))
    def _():
        carry_scr[...] = jnp.zeros(carry_scr.shape, F32)

    merged = (sga_ref[...].astype(F32) * _dot(oa_ref[...], woa_ref[...])
              + sgb_ref[...].astype(F32) * _dot(ob_ref[...], wob_ref[...]))
    x1 = x_ref[...] + mod_ref[2:3, :] * _dot(merged.astype(BF16), wout_ref[...])
    x1_ref[...] = x1
    h2 = (_rms(x1) * g2_ref[...]) * (1.0 + mod_ref[4:5, :]) + mod_ref[3:4, :]
    h2x_ref[:, :d] = h2
    hi = h2.astype(BF16)
    lo = (h2 - hi.astype(F32)).astype(BF16)
    both = _dot(hi, rw2_ref[...])
    logits = (both[:, :LANES] + _dot(lo, rw2_ref[:, :LANES]) + both[:, LANES:]) + rb_ref[...]
    cls, cw_lo, cw_hi = _route(logits.T)

    ncls = carry_scr.shape[0]
    onehot = (lax.broadcasted_iota(I32, (ncls, tm), 0) == cls).astype(F32)
    earlier = (lax.broadcasted_iota(I32, (tm, tm), 0)
               < lax.broadcasted_iota(I32, (tm, tm), 1)).astype(BF16)
    before = _dot(onehot.astype(BF16), earlier)
    carry = carry_scr[...]
    carry_t = jnp.concatenate([carry] * (tm // LANES), axis=1)
    rank = jnp.sum(onehot * (carry_t + before), axis=0, keepdims=True).astype(I32)
    carry = carry + jnp.sum(onehot, axis=1, keepdims=True)
    carry_scr[...] = carry
    cnt_ref[...] = carry.astype(I32)

    sub = lax.broadcasted_iota(I32, (SUBLANES, tm), 0)
    info_ref[...] = jnp.where(sub == 0, cls, jnp.where(sub == 1, rank, 0))
    sub = lax.broadcasted_iota(I32, (LANES, tm), 0)
    weights_t = jnp.where(sub == 0, cw_lo, jnp.where(sub == 1, cw_hi, 0.0))
    h2x_ref[:, d:] = weights_t.T


def _merge(oa, ob, sga, sgb, x, mod_l, woa, wob, wout, g2, rw2, rb, tm):
    b, s, d = x.shape
    hw = oa.shape[-1]
    nt = s // tm
    tok = lambda n: pl.BlockSpec((None, tm, n), lambda bb, i: (bb, i, 0))
    const = lambda r, c: pl.BlockSpec((r, c), lambda bb, i: (0, 0))
    return pl.pallas_call(
        _merge_kernel,
        grid=(b, nt),
        in_specs=[tok(hw), tok(hw), tok(d), tok(d), tok(d),
                  pl.BlockSpec((None, N_MOD, d), lambda bb, i: (bb, 0, 0)),
                  const(hw, d), const(hw, d), const(d, d), const(1, d),
                  const(d, 2 * LANES), const(1, LANES)],
        out_specs=[tok(d), tok(d + LANES),
                   pl.BlockSpec((SUBLANES, tm), lambda bb, i: (0, bb * nt + i)),
                   const(N_CLASS_PAD, LANES)],
        out_shape=[jax.ShapeDtypeStruct((b, s, d), F32),
                   jax.ShapeDtypeStruct((b, s, d + LANES), F32),
                   jax.ShapeDtypeStruct((SUBLANES, b * s), I32),
                   jax.ShapeDtypeStruct((N_CLASS_PAD, LANES), I32)],
        scratch_shapes=[pltpu.VMEM((N_CLASS_PAD, LANES), F32)],
        compiler_params=pltpu.CompilerParams(
            dimension_semantics=("arbitrary", "arbitrary"), vmem_limit_bytes=VMEM_LIMIT),
        name="merge_route",
    )(oa, ob, sga, sgb, x, mod_l, woa, wob, wout, g2, rw2, rb)


def _positions_kernel(offs_ref, info_ref, pos_ref):
    cls = info_ref[0:1, :]
    pos = info_ref[1:2, :]
    for c in range(N_CLASSES):
        pos = pos + jnp.where(cls == c, offs_ref[c], 0)
    pos_ref[...] = jnp.broadcast_to(pos, pos_ref.shape)


def _positions(info, offs):
    return pl.pallas_call(
        _positions_kernel,
        in_specs=[pl.BlockSpec(memory_space=pltpu.SMEM), pl.BlockSpec(info.shape, lambda: (0, 0))],
        out_specs=pl.BlockSpec(info.shape, lambda: (0, 0)),
        out_shape=jax.ShapeDtypeStruct(info.shape, I32),
        name="positions",
    )(offs, info)[0]


def _dispatch_kernel(ends_ref, pos_ref, src_ref, dst_ref, zero_scr, sem, zero_sem):
    tm = pos_ref.shape[1]
    row_tile = zero_scr.shape[0]

    @pl.when(pl.program_id(0) == 0)
    def _():
        zero_scr[...] = jnp.zeros(zero_scr.shape, F32)

        def zero_tail(c):
            start = pl.multiple_of(ends_ref[c] - row_tile, row_tile)
            return pltpu.make_async_copy(zero_scr, dst_ref.at[pl.ds(start, row_tile), :], zero_sem)

        def nonempty(c):
            return ends_ref[c] > (ends_ref[c - 1] if c else 0)

        for c in range(N_CLASSES):
            @pl.when(nonempty(c))
            def _(c=c):
                zero_tail(c).start()
        for c in range(N_CLASSES):
            @pl.when(nonempty(c))
            def _(c=c):
                zero_tail(c).wait()

        def unused_tile(j):
            start = pl.multiple_of(ends_ref[N_CLASSES - 1] + j * row_tile, row_tile)
            return pltpu.make_async_copy(zero_scr, dst_ref.at[pl.ds(start, row_tile), :], zero_sem)

        n_unused = (dst_ref.shape[0] - ends_ref[N_CLASSES - 1]) // row_tile
        lax.fori_loop(0, n_unused, lambda j, carry: (unused_tile(j).start(), carry)[1], 0)
        lax.fori_loop(0, n_unused, lambda j, carry: (unused_tile(j).wait(), carry)[1], 0)

    for r in range(tm):
        pltpu.make_async_copy(src_ref.at[pl.ds(r, 1), :], dst_ref.at[pl.ds(pos_ref[0, r], 1), :],
                              sem).start(priority=r % 2)
    pltpu.make_async_copy(src_ref, dst_ref.at[pl.ds(0, tm), :], sem).wait()


def _dispatch(h2x, pos, ends, n_rows):
    t, w = h2x.shape
    tm = DISPATCH_TILE
    pos = pos.reshape(t // tm, 1, tm)
    smem_row = pl.BlockSpec((None, 1, tm), lambda i: (i, 0, 0), memory_space=pltpu.SMEM)
    return pl.pallas_call(
        _dispatch_kernel,
        grid=(t // tm,),
        in_specs=[pl.BlockSpec(memory_space=pltpu.SMEM), smem_row, pl.BlockSpec((tm, w), lambda i: (i, 0))],
        out_specs=pl.BlockSpec(memory_space=pl.ANY),
        out_shape=jax.ShapeDtypeStruct((n_rows, w), F32),
        scratch_shapes=[pltpu.VMEM((MOE_ROW_TILE, w), F32), pltpu.SemaphoreType.DMA, pltpu.SemaphoreType.DMA],
        compiler_params=pltpu.CompilerParams(
            dimension_semantics=("arbitrary",), vmem_limit_bytes=VMEM_LIMIT),
        name="dispatch",
    )(ends, pos, h2x)


def _experts_kernel(elo_ref, ehi_ref, live_ref, xs_ref, *refs):
    del elo_ref, ehi_ref
    ys_ref = refs[-1]
    d = ys_ref.shape[1]
    tm = MOE_ROW_TILE
    step = pl.program_id(0)

    for u in range(EXPERT_TILES_PER_STEP):
        wg_lo, wg_hi, wu_lo, wu_hi, wd_lo, wd_hi = refs[6 * u:6 * u + 6]
        rows = pl.ds(u * tm, tm)

        @pl.when(live_ref[step * EXPERT_TILES_PER_STEP + u] == 1)
        def _():
            h = xs_ref[rows, :d].astype(BF16)

            def expert(wg, wu, wd, weight):
                gate = _dot(h, wg[...])
                hidden = ((gate * jax.nn.sigmoid(gate)) * _dot(h, wu[...])) * weight
                return _dot(hidden.astype(BF16), wd[...])

            ys_ref[rows, :] = (expert(wg_lo, wu_lo, wd_lo, xs_ref[rows, d:d + 1])
                               + expert(wg_hi, wu_hi, wd_hi, xs_ref[rows, d + 1:d + 2]))

        @pl.when(live_ref[step * EXPERT_TILES_PER_STEP + u] == 0)
        def _():
            ys_ref[rows, :] = jnp.zeros((tm, d), F32)


def _experts(xs, e_lo, e_hi, live, wg, wu, wd, layer):
    n_rows, w = xs.shape
    _, ne, d, f = wg.shape
    per = EXPERT_TILES_PER_STEP
    tm = MOE_ROW_TILE
    assert n_rows % (per * tm) == 0

    def wspec(shape, use_hi, u):
        def index(step, elo, ehi, lv):
            return layer, (ehi if use_hi else elo)[step * per + u], 0, 0
        return pl.BlockSpec((None, None) + shape, index)

    wspecs, weights = [], []
    for u in range(per):
        wspecs += [wspec((d, f), 0, u), wspec((d, f), 1, u), wspec((d, f), 0, u), wspec((d, f), 1, u),
                   wspec((f, d), 0, u), wspec((f, d), 1, u)]
        weights += [wg, wg, wu, wu, wd, wd]
    grid_spec = pltpu.PrefetchScalarGridSpec(
        num_scalar_prefetch=3,
        grid=(n_rows // (per * tm),),
        in_specs=[pl.BlockSpec((per * tm, w), lambda step, elo, ehi, lv: (step, 0))] + wspecs,
        out_specs=pl.BlockSpec((per * tm, d), lambda step, elo, ehi, lv: (step, 0)),
    )
    return pl.pallas_call(
        _experts_kernel,
        grid_spec=grid_spec,
        out_shape=jax.ShapeDtypeStruct((n_rows, d), F32),
        compiler_params=pltpu.CompilerParams(
            dimension_semantics=("arbitrary",), vmem_limit_bytes=VMEM_LIMIT),
        name="experts",
    )(e_lo, e_hi, live, xs, *weights)


def _combine_kernel(pos_ref, next_ref, ys_ref, x1_ref, mod_ref, o_ref, buf, sem):
    tm = buf.shape[1]
    step = pl.program_id(0) * pl.num_programs(1) + pl.program_id(1)
    last = pl.num_programs(0) * pl.num_programs(1) - 1

    def issue(p_ref, slot):
        for r in range(tm):
            pltpu.make_async_copy(ys_ref.at[pl.ds(p_ref[slot, r], 1), :], buf.at[slot, pl.ds(r, 1), :],
                                  sem.at[slot]).start(priority=r % 2)

    def finish(slot):
        pltpu.make_async_copy(ys_ref.at[pl.ds(0, tm), :], buf.at[slot], sem.at[slot]).wait()
        rows = pl.ds(slot * tm, tm)
        o_ref[rows, :] = x1_ref[rows, :] + mod_ref[5:6, :] * buf[slot]

    @pl.when(step == 0)
    def _():
        issue(pos_ref, 0)

    issue(pos_ref, 1)
    finish(0)

    @pl.when(step < last)
    def _():
        issue(next_ref, 0)

    finish(1)


def _combine(ys, pos, x1, mod_l):
    b, s, d = x1.shape
    tm = MOE_DMA_TILE
    assert s % (2 * tm) == 0
    nt = s // (2 * tm)
    n_steps = b * nt
    pos2 = pos.reshape(n_steps, 2, tm)
    this_step = pl.BlockSpec((None, 2, tm), lambda bb, i: (bb * nt + i, 0, 0), memory_space=pltpu.SMEM)
    next_step = pl.BlockSpec((None, 2, tm), lambda bb, i: (jnp.minimum(bb * nt + i + 1, n_steps - 1), 0, 0),
                             memory_space=pltpu.SMEM)
    return pl.pallas_call(
        _combine_kernel,
        grid=(b, nt),
        in_specs=[this_step, next_step,
                  pl.BlockSpec(memory_space=pl.ANY),
                  pl.BlockSpec((None, 2 * tm, d), lambda bb, i: (bb, i, 0)),
                  pl.BlockSpec((None, N_MOD, d), lambda bb, i: (bb, 0, 0))],
        out_specs=pl.BlockSpec((None, 2 * tm, d), lambda bb, i: (bb, i, 0)),
        out_shape=jax.ShapeDtypeStruct((b, s, d), F32),
        scratch_shapes=[pltpu.VMEM((2, tm, d), F32), pltpu.SemaphoreType.DMA((2,))],
        compiler_params=pltpu.CompilerParams(
            dimension_semantics=("arbitrary", "arbitrary"), vmem_limit_bytes=VMEM_LIMIT),
        name="combine",
    )(pos2, pos2, ys, x1, mod_l)


def _moe(h2x, info, counts, wg, wu, wd, layer, x1, mod_l):
    b, s, d = x1.shape
    t = b * s
    tm = MOE_ROW_TILE
    n_tiles = t // tm + N_CLASSES
    n_tiles += -n_tiles % EXPERT_TILES_PER_STEP
    counts = counts[:N_CLASSES, 0]
    padded = (counts + tm - 1) // tm * tm
    ends = jnp.cumsum(padded)
    offs = jnp.pad(ends - padded, (0, N_CLASS_PAD - N_CLASSES)).astype(I32)
    tile_cls = jnp.sum(jnp.arange(n_tiles, dtype=I32)[:, None] * tm >= ends[None, :], axis=1)
    live = (tile_cls < N_CLASSES).astype(I32)
    tile_cls = jnp.minimum(tile_cls, N_CLASSES - 1)
    group, pair = tile_cls // N_PAIRS, tile_cls % N_PAIRS
    pairs = jnp.asarray(_PAIRS, I32)
    e_lo = group * EXPERTS_PER_GROUP + pairs[pair, 0]
    e_hi = group * EXPERTS_PER_GROUP + pairs[pair, 1]

    pos = _positions(info, offs).reshape(t // MOE_DMA_TILE, 1, MOE_DMA_TILE)
    xs = _dispatch(h2x.reshape(t, d + LANES), pos, ends.astype(I32), n_tiles * tm)
    ys = _experts(xs, e_lo, e_hi, live, wg, wu, wd, layer)
    return _combine(ys, pos, x1, mod_l)


def kernel(x, c, w_ada, b_ada, norm1_g, w_in, q_norm_g, k_norm_g, rel_bias, w_o_a, w_o_b, w_out,
           norm2_g, router_w, router_b, w_gate, w_up, w_down):
    b, s, d = x.shape
    depth = w_ada.shape[0]
    assert d % LANES == 0
    tm = min(TOKEN_TILE, s)

    mod = _adaln(c, w_ada, b_ada)
    bias = _bias_tiles(rel_bias, DSA_QUERY_BLOCK, DSA_KEY_BLOCK)
    rw = jnp.pad(router_w, ((0, 0), (0, LANES - N_EXPERTS)))
    rwh = rw.astype(BF16)
    rw2 = jnp.concatenate([rwh, (rw - rwh.astype(F32)).astype(BF16)], axis=1)
    rb = jnp.pad(router_b, (0, LANES - N_EXPERTS)).reshape(1, LANES)
    wg, wu, wd = w_gate.astype(BF16), w_up.astype(BF16), w_down.astype(BF16)

    for l in range(depth):
        qa, ka, va, qbt, qit, kb, vt, ki, wit, sga, sgb = _in_proj(
            x, mod[l], norm1_g[l].reshape(1, d), _pack_w_in(w_in[l]),
            q_norm_g[l].reshape(HEAD_DIM, 1), k_norm_g[l].reshape(1, HEAD_DIM), tm)
        oa = _sb_attention(qa, ka, va)
        ob = _dsa_attention(qit, ki, wit, qbt, kb, vt, bias)
        x1, h2x, info, counts = _merge(oa, ob, sga, sgb, x, mod[l], w_o_a[l].astype(BF16),
                                       w_o_b[l].astype(BF16), w_out[l].astype(BF16),
                                       norm2_g[l].reshape(1, d), rw2, rb, tm)
        x = _moe(h2x, info, counts, wg, wu, wd, l, x1, mod[l])
    return x
```

```python
import functools
import math

import numpy as np
import jax
import jax.numpy as jnp
from jax import lax
from jax.experimental import pallas as pl
from jax.experimental.pallas import tpu as pltpu

F32 = jnp.float32
BF16 = jnp.bfloat16
I32 = jnp.int32
I16 = jnp.int16

HEAD_DIM = 64
N_HEADS = 8
N_EXPERTS = 16
N_GROUPS = 4
EXPERTS_PER_GROUP = 4
_PAIRS = ((0, 1), (0, 2), (0, 3), (1, 2), (1, 3), (2, 3))
N_PAIRS = len(_PAIRS)
N_CLASSES = N_GROUPS * N_PAIRS
N_CLASS_PAD = 32
MOE_ROW_TILE = 256
EXPERT_TILES_PER_STEP = 2
MOE_DMA_TILE = 512
DISPATCH_TILE = 1024
N_MOD = 6
EPS = 1e-6
INDEX_TOPK_MAX = 256
REL_BUCKETS = 32
REL_MAX_DIST = 128
LANES = 128
SUBLANES = 8
PACKED_SUBLANES = 16
HALF_BITS = 16
TOKEN_TILE = 512
EXP2_MAX = 126.0
DSA_QUERY_BLOCK = 256
DSA_KEY_BLOCK = 128
SB_QUERY_BLOCK = 256
SB_KEY_BLOCK = 256
SB_DEAD_LOG2 = 150.0
LOG2E = 1.4426950408889634
NEG_BIG = -1e30
INT_MIN = -2 ** 31
INT_MAX = 2 ** 31 - 1
HALF_BIAS = 2 ** 15

VMEM_LIMIT = 56 * 1024 * 1024


def _nt_dot(a, b):
    return lax.dot_general(a, b, (((1,), (1,)), ((), ())), preferred_element_type=F32)


def _dot(a, b):
    return jnp.dot(a, b, preferred_element_type=F32)


def _mod_kernel(c_ref, w_ref, b_ref, o_ref):
    c = c_ref[...]
    sc = c * jax.nn.sigmoid(c)
    o_ref[...] = _dot(sc.astype(BF16), w_ref[...].astype(BF16)) + b_ref[...]


def _adaln(c, w_ada, b_ada):
    depth, d, nd = w_ada.shape
    b = c.shape[0]
    out = pl.pallas_call(
        _mod_kernel,
        grid=(depth, nd // (2 * d)),
        in_specs=[pl.BlockSpec((b, d), lambda l, j: (0, 0)),
                  pl.BlockSpec((None, d, 2 * d), lambda l, j: (l, 0, j)),
                  pl.BlockSpec((None, 1, 2 * d), lambda l, j: (l, 0, j))],
        out_specs=pl.BlockSpec((None, b, 2 * d), lambda l, j: (l, 0, j)),
        out_shape=jax.ShapeDtypeStruct((depth, b, nd), F32),
        compiler_params=pltpu.CompilerParams(vmem_limit_bytes=VMEM_LIMIT),
        name="adaln",
    )(c, w_ada, b_ada.reshape(depth, 1, nd))
    return out.reshape(depth, b, N_MOD, d)


def _rms(x):
    return x * lax.rsqrt(jnp.mean(x * x, axis=-1, keepdims=True) + EPS)


def _in_kernel(x_ref, mod_ref, g1_ref, w_ref, qng_ref, kng_ref,
               qa_ref, ka_ref, va_ref, qbt_ref, qit_ref, kb_ref, vt_ref, ki_ref, wit_ref,
               sga_ref, sgb_ref):
    d = x_ref.shape[-1]
    tm = x_ref.shape[0]
    hw = N_HEADS * HEAD_DIM
    x = x_ref[...]
    h = (_rms(x) * g1_ref[...]) * (1.0 + mod_ref[1:2, :]) + mod_ref[0:1, :]
    hb = h.astype(BF16)

    def proj(lo, n):
        return _dot(hb, w_ref[:, lo:lo + n])

    def heads(r, ref):
        for pp in range(N_HEADS // 2):
            ref[pp] = r[:, pp * LANES:(pp + 1) * LANES].astype(ref.dtype)

    scale = HEAD_DIM ** -0.5
    heads(proj(0, hw) * (scale * LOG2E), qa_ref)
    heads(proj(hw, hw), ka_ref)
    heads(proj(2 * hw, hw), va_ref)

    rt = proj(3 * hw, hw).T
    for hh in range(N_HEADS):
        slab = rt[hh * HEAD_DIM:(hh + 1) * HEAD_DIM, :]
        inv = lax.rsqrt(jnp.mean(slab * slab, axis=0, keepdims=True) + EPS)
        qbt_ref[hh] = (((slab * inv) * qng_ref[...]) * scale).astype(BF16)

    r = proj(4 * hw, 2 * HEAD_DIM)
    kb_ref[...] = (_rms(r[:, :HEAD_DIM]) * kng_ref[...]).astype(BF16)
    vt = r.T[HEAD_DIM:, :]
    ones_row = (lax.broadcasted_iota(I32, (HEAD_DIM, LANES), 0) == 0).astype(BF16)
    for cc in range(tm // LANES):
        vt_ref[cc, :HEAD_DIM, :] = vt[:, cc * LANES:(cc + 1) * LANES].astype(BF16)
        vt_ref[cc, HEAD_DIM:, :] = ones_row

    rt = proj(4 * hw + LANES, hw).T
    for hh in range(N_HEADS):
        qit_ref[hh] = rt[hh * HEAD_DIM:(hh + 1) * HEAD_DIM, :].astype(BF16)

    r = proj(5 * hw + LANES, LANES)
    ki_ref[...] = r[:, :HEAD_DIM].astype(BF16)
    wit_ref[...] = r.T[HEAD_DIM:HEAD_DIM + N_HEADS, :]

    off = 5 * hw + 2 * LANES
    sga_ref[...] = jax.nn.sigmoid(proj(off, d)).astype(BF16)
    sgb_ref[...] = jax.nn.sigmoid(proj(off + d, d)).astype(BF16)


def _pack_w_in(w_in_l):
    hw = N_HEADS * HEAD_DIM
    a = 4 * hw + 2 * HEAD_DIM + hw
    small = w_in_l[:, a:a + HEAD_DIM + N_HEADS]
    small = jnp.pad(small, ((0, 0), (0, LANES - small.shape[1])))
    gates = w_in_l[:, a + HEAD_DIM + N_HEADS:]
    return jnp.concatenate([w_in_l[:, :a], small, gates], axis=1).astype(BF16)


def _in_proj(x, mod_l, g1, w_packed, qng, kng, tm):
    b, s, d = x.shape
    hm = jax.ShapeDtypeStruct((b, N_HEADS // 2, s, LANES), BF16)
    hmt = jax.ShapeDtypeStruct((b, N_HEADS, HEAD_DIM, s), BF16)
    tok64 = jax.ShapeDtypeStruct((b, s, HEAD_DIM), BF16)
    hm_spec = pl.BlockSpec((None, N_HEADS // 2, tm, LANES), lambda bb, i: (bb, 0, i, 0))
    hmt_spec = pl.BlockSpec((None, N_HEADS, HEAD_DIM, tm), lambda bb, i: (bb, 0, 0, i))
    tok = lambda n: pl.BlockSpec((None, tm, n), lambda bb, i: (bb, i, 0))
    nw = w_packed.shape[1]
    return pl.pallas_call(
        _in_kernel,
        grid=(b, s // tm),
        in_specs=[tok(d),
                  pl.BlockSpec((None, N_MOD, d), lambda bb, i: (bb, 0, 0)),
                  pl.BlockSpec((1, d), lambda bb, i: (0, 0)),
                  pl.BlockSpec((d, nw), lambda bb, i: (0, 0)),
                  pl.BlockSpec((HEAD_DIM, 1), lambda bb, i: (0, 0)),
                  pl.BlockSpec((1, HEAD_DIM), lambda bb, i: (0, 0))],
        out_specs=[hm_spec] * 3 + [hmt_spec] * 2
                  + [tok(HEAD_DIM),
                     pl.BlockSpec((None, tm // LANES, 2 * HEAD_DIM, LANES), lambda bb, i: (bb, i, 0, 0)),
                     tok(HEAD_DIM),
                     pl.BlockSpec((None, N_HEADS, tm), lambda bb, i: (bb, 0, i)),
                     tok(d), tok(d)],
        out_shape=[hm] * 3 + [hmt] * 2
                  + [tok64,
                     jax.ShapeDtypeStruct((b, s // LANES, 2 * HEAD_DIM, LANES), BF16),
                     tok64,
                     jax.ShapeDtypeStruct((b, N_HEADS, s), F32),
                     jax.ShapeDtypeStruct((b, s, d), BF16),
                     jax.ShapeDtypeStruct((b, s, d), BF16)],
        compiler_params=pltpu.CompilerParams(
            dimension_semantics=("parallel", "parallel"), vmem_limit_bytes=VMEM_LIMIT),
        name="in_proj",
    )(x, mod_l, g1, w_packed, qng, kng)


def _sb_kernel(q_ref, k_ref, v_ref, o_ref, acc_scr, csum_scr, sp_scr, ls_scr, qm_scr):
    tq = q_ref.shape[1]
    tk = SB_KEY_BLOCK
    i = pl.program_id(1)
    nfull = (i * tq) // tk
    row = lax.broadcasted_iota(I32, (tq, tk), 0)
    col = lax.broadcasted_iota(I32, (tq, tk), 1)
    before = (col - row) < (i * tq - nfull * tk)
    krow = lax.broadcasted_iota(I32, (tk, tk), 0)
    kcol = lax.broadcasted_iota(I32, (tk, tk), 1)
    suffix = (krow > kcol).astype(BF16)
    low_half = lax.broadcasted_iota(I32, (tq, LANES), 1) < HEAD_DIM
    for hh in range(N_HEADS):
        slab = q_ref[hh // 2]
        qm_scr[hh] = jnp.where(low_half == (hh % 2 == 0), slab, jnp.zeros_like(slab))

    def stage_a(hh, st, first):
        z2 = _nt_dot(qm_scr[hh], k_ref[hh // 2, pl.ds(st, tk), :])
        sp2 = jnp.maximum(jnp.log2(1.0 + jnp.exp2(jnp.minimum(z2, EXP2_MAX))), z2)
        if first:
            sp2 = jnp.where(before, sp2, 0.0)
        sp_scr[hh] = sp2.astype(BF16)
        ls_scr[hh] = z2 - sp2

    def stage_b(hh, st, first):
        sp = sp_scr[hh]
        suf = _dot(sp, suffix)
        if first:
            a = jnp.where(before, jnp.exp2(ls_scr[hh] - suf), 0.0)
            csum = jnp.zeros((tq, 1), F32)
        else:
            csum = csum_scr[hh]
            a = jnp.exp2((ls_scr[hh] - suf) - csum)
        c = _dot(a.astype(BF16), v_ref[hh // 2, pl.ds(st, tk), :])
        if first:
            acc_scr[hh] = c
        else:
            acc_scr[hh] += c
        csum_scr[hh] = csum + (suf[:, 0:1] + sp[:, 0:1].astype(F32))

    def key_block(st, first):
        for hh in range(N_HEADS):
            stage_a(hh, st, first)
        for hh in range(N_HEADS):
            stage_b(hh, st, first)

    key_block(pl.multiple_of(nfull * tk, tk), True)

    def live():
        return jnp.min(csum_scr[...]) < SB_DEAD_LOG2

    def body(carry):
        jj, _ = carry
        key_block(pl.multiple_of((nfull - 1 - jj) * tk, tk), False)
        return jj + 1, live()

    lax.while_loop(lambda carry: (carry[0] < nfull) & carry[1], body, (jnp.int32(0), live()))
    for pp in range(N_HEADS // 2):
        o_ref[:, pp * LANES:(pp + 1) * LANES] = jnp.where(
            low_half, acc_scr[2 * pp], acc_scr[2 * pp + 1]).astype(o_ref.dtype)


def _sb_attention(qa, ka, va):
    b, npair, s, _ = qa.shape
    nh, dh = 2 * npair, HEAD_DIM
    tq = SB_QUERY_BLOCK
    assert SB_KEY_BLOCK % tq == 0 and s % SB_KEY_BLOCK == 0
    return pl.pallas_call(
        _sb_kernel,
        grid=(b, s // tq),
        in_specs=[pl.BlockSpec((None, npair, tq, LANES), lambda bb, i: (bb, 0, i, 0)),
                  pl.BlockSpec((None, npair, s, LANES), lambda bb, i: (bb, 0, 0, 0)),
                  pl.BlockSpec((None, npair, s, LANES), lambda bb, i: (bb, 0, 0, 0))],
        out_specs=pl.BlockSpec((None, tq, nh * dh), lambda bb, i: (bb, i, 0)),
        out_shape=jax.ShapeDtypeStruct((b, s, nh * dh), BF16),
        scratch_shapes=[pltpu.VMEM((nh, tq, LANES), F32), pltpu.VMEM((nh, tq, 1), F32),
                        pltpu.VMEM((nh, tq, SB_KEY_BLOCK), BF16), pltpu.VMEM((nh, tq, SB_KEY_BLOCK), F32),
                        pltpu.VMEM((nh, tq, LANES), BF16)],
        compiler_params=pltpu.CompilerParams(
            dimension_semantics=("parallel", "parallel"), vmem_limit_bytes=VMEM_LIMIT),
        name="stick_breaking",
    )(qa, ka, va)


def _t5_bucket_np(dist):
    max_exact = REL_BUCKETS // 2
    d_f = np.maximum(dist, 1).astype(np.float32)
    large = max_exact + (np.log(d_f / np.float32(max_exact))
                         / np.float32(math.log(REL_MAX_DIST / max_exact))
                         * np.float32(REL_BUCKETS - max_exact)).astype(np.int32)
    large = np.minimum(large, REL_BUCKETS - 1)
    return np.where(dist < max_exact, dist, large).astype(np.int32)


def _bias_tiles(rel_bias, tq, tk):
    n_near = tq // tk + 1
    key = np.arange(tk)[:, None]
    query = np.arange(tq)[None, :]
    bk = np.stack([_t5_bucket_np(np.maximum(query - key - (n - 1) * tk, 0)) for n in range(n_near)])
    far = int(_t5_bucket_np(np.array([REL_MAX_DIST]))[0])
    nh = rel_bias.shape[1]

    def body(rb_ref, bk_ref, o_ref):
        buckets = bk_ref[...]
        for hh in range(nh):
            tile = jnp.zeros((tk, tq), F32)
            for bucket in range(REL_BUCKETS):
                tile = jnp.where(buckets == bucket, rb_ref[hh, bucket] - rb_ref[hh, far], tile)
            o_ref[hh] = tile

    return pl.pallas_call(
        body,
        grid=(n_near,),
        in_specs=[pl.BlockSpec(memory_space=pltpu.SMEM),
                  pl.BlockSpec((None, tk, tq), lambda n: (n, 0, 0))],
        out_specs=pl.BlockSpec((None, nh, tk, tq), lambda n: (n, 0, 0, 0)),
        out_shape=jax.ShapeDtypeStruct((n_near, nh, tk, tq), F32),
        name="bias_tiles",
    )(rel_bias.T, jnp.asarray(bk))


def _sort_key(x):
    bits = pltpu.bitcast(x, I32)
    return bits ^ ((bits >> 31) & INT_MAX)


def _fold_rows(x, op):
    return functools.reduce(
        op, [x[g * SUBLANES:(g + 1) * SUBLANES, :] for g in range(x.shape[0] // SUBLANES)])


def _dsa_kernel(ktop, qit_ref, ki_ref, wit_ref, qbt_ref, kb_ref, vt_ref, bias_ref, o_ref,
                key_scr, hi_scr, lo_scr, m_scr, acc_scr, s_scr):
    tq = qit_ref.shape[2]
    tk = DSA_KEY_BLOCK
    per_q = tq // tk
    i = pl.program_id(1)
    first_own = per_q * i
    nchunks = per_q * (i + 1)
    key_i = lax.broadcasted_iota(I32, (tk, tq), 0)
    qry_i = lax.broadcasted_iota(I32, (tk, tq), 1)

    def causal(rel):
        return (key_i - qry_i) <= (-rel * tk)

    def for_chunks(fn):
        def far(j, carry):
            for u in range(per_q):
                fn(j * per_q + u, None)
            return carry
        lax.fori_loop(0, i - 1, far, 0)

        @pl.when(i >= 1)
        def _():
            for rel in range(-per_q, -1):
                fn(first_own + rel, None)
            fn(first_own - 1, -1)

        for rel in range(per_q):
            fn(first_own + rel, rel)

    def score_chunk(c, rel):
        kic = ki_ref[pl.ds(pl.multiple_of(c * tk, tk), tk), :]
        score = jnp.zeros((tk, tq), F32)
        for hh in range(N_HEADS):
            score = score + wit_ref[hh:hh + 1, :] * jnp.maximum(_dot(kic, qit_ref[hh]), 0.0)
        keys = _sort_key(score)
        if rel is not None and rel >= 0:
            keys = jnp.where(causal(rel), keys, INT_MIN)
        key_scr[c] = keys
        hi_scr[c] = (keys >> HALF_BITS).astype(I16)
        lo_scr[c] = ((keys & (2 * HALF_BIAS - 1)) - HALF_BIAS).astype(I16)

    for_chunks(score_chunk)

    def count(pred):
        def body(j, part):
            for u in range(per_q):
                part = part + _fold_rows(jnp.where(pred(key_scr[j * per_q + u]), 1, 0), jnp.add)
            return part
        part = lax.fori_loop(0, i + 1, body, jnp.zeros((SUBLANES, tq), I32))
        return jnp.sum(part, axis=0, keepdims=True)

    def count16(scr, pred):
        one, zero = jnp.int16(1), jnp.int16(0)

        def block(j, part):
            for u in range(per_q):
                hit = jnp.where(pred(scr[j * per_q + u]), one, zero)
                part = part + functools.reduce(
                    jnp.add, [hit[g * PACKED_SUBLANES:(g + 1) * PACKED_SUBLANES, :]
                              for g in range(tk // PACKED_SUBLANES)])
            return part

        nblk = i + 1
        part = lax.fori_loop(0, nblk // 2, lambda jj, part: block(2 * jj + 1, block(2 * jj, part)),
                             jnp.zeros((PACKED_SUBLANES, tq), I16))
        part = lax.cond(nblk % 2 == 1, lambda p: block(nblk - 1, p), lambda p: p, part)
        return jnp.sum(part.astype(I32), axis=0, keepdims=True)

    def search16(scr, base):
        def bit_body(bi, t_u):
            cand_u = t_u | jnp.left_shift(jnp.int32(1), HALF_BITS - 1 - bi)
            cand = (cand_u - HALF_BIAS).astype(I16)
            return jnp.where(base + count16(scr, lambda k: k >= cand) >= ktop, cand_u, t_u)
        return lax.fori_loop(0, HALF_BITS, bit_body, jnp.zeros((1, tq), I32))

    def find_threshold():
        t_hi = search16(hi_scr, 0) - HALF_BIAS
        t_hi16 = t_hi.astype(I16)
        above = count16(hi_scr, lambda k: k > t_hi16)

        def keep_band(j, carry):
            for u in range(per_q):
                c = j * per_q + u
                lo_scr[c] = jnp.where(hi_scr[c] == t_hi16, lo_scr[c], jnp.int16(-HALF_BIAS))
            return carry

        lax.fori_loop(0, i + 1, keep_band, 0)
        return t_hi * (2 * HALF_BIAS) + search16(lo_scr, above)

    thr = lax.cond((i + 1) * tq <= ktop, lambda: jnp.full((1, tq), INT_MIN, I32), find_threshold)
    n_ge = count(lambda k: k >= thr)

    @pl.when(jnp.max(n_ge) > ktop)
    def _():
        need = (ktop - count(lambda k: k > thr)).astype(F32)
        lower = (lax.broadcasted_iota(I32, (tk, tk), 1)
                 < lax.broadcasted_iota(I32, (tk, tk), 0)).astype(BF16)

        def body(c, seen):
            k = key_scr[c]
            eq = k == thr
            eqf = eq.astype(BF16)
            rank = seen + _dot(lower, eqf)
            drop = eq & (rank >= need) & (n_ge > ktop)
            key_scr[c] = jnp.where(drop, thr - 1, k)
            return seen + jnp.sum(eqf.astype(F32), axis=0, keepdims=True)

        lax.fori_loop(0, nchunks, body, jnp.zeros((1, tq), F32))

    def logits(c, rel):
        sel = key_scr[c] >= thr
        if rel is not None and rel >= 0:
            sel = sel & causal(rel)
        kbc = kb_ref[pl.ds(pl.multiple_of(c * tk, tk), tk), :]

        def head(hh):
            s = _dot(kbc, qbt_ref[hh])
            if rel is not None:
                s = s + bias_ref[rel + 1, hh]
            return s
        return sel, head

    m_scr[...] = jnp.full(m_scr.shape, NEG_BIG, F32)

    def max_chunk(c, rel):
        sel, head = logits(c, rel)
        for hh in range(N_HEADS):
            s = jnp.where(sel, head(hh), NEG_BIG)
            s_scr[hh, c] = s
            m_scr[hh] = jnp.maximum(m_scr[hh], _fold_rows(s, jnp.maximum))

    for_chunks(max_chunk)
    m = [jnp.max(m_scr[hh], axis=0, keepdims=True) for hh in range(N_HEADS)]

    acc_scr[...] = jnp.zeros(acc_scr.shape, F32)

    def att_block(j, carry):
        for u in range(per_q):
            c = j * per_q + u
            vtc = vt_ref[c]
            for hh in range(N_HEADS):
                p = jnp.exp(s_scr[hh, c] - m[hh])
                acc_scr[hh] += _dot(vtc, p.astype(BF16))
        return carry

    lax.fori_loop(0, i + 1, att_block, 0)

    out_t = jnp.concatenate(
        [acc_scr[hh, :HEAD_DIM, :] / acc_scr[hh, HEAD_DIM:HEAD_DIM + 1, :] for hh in range(N_HEADS)],
        axis=0)
    o_ref[...] = out_t.T.astype(o_ref.dtype)


def _dsa_attention(qit, ki, wit, qbt, kb, vt, bias):
    b, nh, dh, s = qit.shape
    tq = DSA_QUERY_BLOCK
    tk = DSA_KEY_BLOCK
    assert tk == REL_MAX_DIST and tk == LANES and tq % tk == 0 and s % tq == 0
    ktop = min(INDEX_TOPK_MAX, s // 4)
    hmt_q = pl.BlockSpec((None, nh, dh, tq), lambda bb, i: (bb, 0, 0, i))
    full64 = pl.BlockSpec((None, s, dh), lambda bb, i: (bb, 0, 0))
    return pl.pallas_call(
        functools.partial(_dsa_kernel, ktop),
        grid=(b, s // tq),
        in_specs=[hmt_q, full64,
                  pl.BlockSpec((None, nh, tq), lambda bb, i: (bb, 0, i)),
                  hmt_q, full64,
                  pl.BlockSpec((None, s // tk, 2 * dh, tk), lambda bb, i: (bb, 0, 0, 0)),
                  pl.BlockSpec(bias.shape, lambda bb, i: (0, 0, 0, 0))],
        out_specs=pl.BlockSpec((None, tq, nh * dh), lambda bb, i: (bb, i, 0)),
        out_shape=jax.ShapeDtypeStruct((b, s, nh * dh), BF16),
        scratch_shapes=[pltpu.VMEM((s // tk, tk, tq), I32),
                        pltpu.VMEM((s // tk, tk, tq), I16),
                        pltpu.VMEM((s // tk, tk, tq), I16),
                        pltpu.VMEM((nh, SUBLANES, tq), F32),
                        pltpu.VMEM((nh, 2 * dh, tq), F32),
                        pltpu.VMEM((nh, s // tk, tk, tq), F32)],
        compiler_params=pltpu.CompilerParams(
            dimension_semantics=("parallel", "parallel"), vmem_limit_bytes=VMEM_LIMIT),
        name="sparse_attention",
    )(qit, ki, wit, qbt, kb, vt, bias)


def _first_max4(a):
    m1 = jnp.maximum(jnp.maximum(a[0], a[1]), jnp.maximum(a[2], a[3]))
    i1 = jnp.where(a[0] == m1, 0, jnp.where(a[1] == m1, 1, jnp.where(a[2] == m1, 2, 3)))
    rest = [jnp.where(i1 == j, -1.0, a[j]) for j in range(4)]
    m2 = jnp.maximum(jnp.maximum(rest[0], rest[1]), jnp.maximum(rest[2], rest[3]))
    i2 = jnp.where(rest[0] == m2, 0, jnp.where(rest[1] == m2, 1, jnp.where(rest[2] == m2, 2, 3)))
    return m1, i1, m2, i2


def _route(lt):
    lg = [lt[e:e + 1, :] for e in range(N_EXPERTS)]
    mx = functools.reduce(jnp.maximum, lg)
    ex = [jnp.exp(v - mx) for v in lg]
    tot = functools.reduce(lambda u, v: u + v, ex)
    p = [v / tot for v in ex]
    gs = []
    for g in range(N_GROUPS):
        m1, _, m2, _ = _first_max4(p[4 * g:4 * g + 4])
        gs.append(m1 + m2)
    _, gbest, _, _ = _first_max4(gs)
    chosen = [jnp.where(gbest == 0, p[j], jnp.where(gbest == 1, p[4 + j],
              jnp.where(gbest == 2, p[8 + j], p[12 + j]))) for j in range(4)]
    m1, i1, m2, i2 = _first_max4(chosen)
    den = m1 + m2
    lo = jnp.minimum(i1, i2)
    hi = jnp.maximum(i1, i2)
    pair = jnp.where(lo == 0, hi - 1, jnp.where(lo == 1, hi + 1, N_PAIRS - 1))
    first_is_lo = i1 < i2
    w1 = m1 / den
    w2 = m2 / den
    return gbest * N_PAIRS + pair, jnp.where(first_is_lo, w1, w2), jnp.where(first_is_lo, w2, w1)


def _merge_kernel(oa_ref, ob_ref, sga_ref, sgb_ref, x_ref, mod_ref, woa_ref, wob_ref, wout_ref,
                  g2_ref, rw2_ref, rb_ref, x1_ref, h2x_ref, info_ref, cnt_ref, carry_scr):
    tm, d = x_ref.shape

    @pl.when((pl.program_id(0) == 0) & (pl.program_id(1) == 0))
    def _():
        carry_scr[...] = jnp.zeros(carry_scr.shape, F32)

    merged = (sga_ref[...].astype(F32) * _dot(oa_ref[...], woa_ref[...])
              + sgb_ref[...].astype(F32) * _dot(ob_ref[...], wob_ref[...]))
    x1 = x_ref[...] + mod_ref[2:3, :] * _dot(merged.astype(BF16), wout_ref[...])
    x1_ref[...] = x1
    h2 = (_rms(x1) * g2_ref[...]) * (1.0 + mod_ref[4:5, :]) + mod_ref[3:4, :]
    h2x_ref[:, :d] = h2
    hi = h2.astype(BF16)
    lo = (h2 - hi.astype(F32)).astype(BF16)
    both = _dot(hi, rw2_ref[...])
    logits = (both[:, :LANES] + _dot(lo, rw2_ref[:, :LANES]) + both[:, LANES:]) + rb_ref[...]
    cls, cw_lo, cw_hi = _route(logits.T)

    ncls = carry_scr.shape[0]
    onehot = (lax.broadcasted_iota(I32, (ncls, tm), 0) == cls).astype(F32)
    earlier = (lax.broadcasted_iota(I32, (tm, tm), 0)
               < lax.broadcasted_iota(I32, (tm, tm), 1)).astype(BF16)
    before = _dot(onehot.astype(BF16), earlier)
    carry = carry_scr[...]
    carry_t = jnp.concatenate([carry] * (tm // LANES), axis=1)
    rank = jnp.sum(onehot * (carry_t + before), axis=0, keepdims=True).astype(I32)
    carry = carry + jnp.sum(onehot, axis=1, keepdims=True)
    carry_scr[...] = carry
    cnt_ref[...] = carry.astype(I32)

    sub = lax.broadcasted_iota(I32, (SUBLANES, tm), 0)
    info_ref[...] = jnp.where(sub == 0, cls, jnp.where(sub == 1, rank, 0))
    sub = lax.broadcasted_iota(I32, (LANES, tm), 0)
    weights_t = jnp.where(sub == 0, cw_lo, jnp.where(sub == 1, cw_hi, 0.0))
    h2x_ref[:, d:] = weights_t.T


def _merge(oa, ob, sga, sgb, x, mod_l, woa, wob, wout, g2, rw2, rb, tm):
    b, s, d = x.shape
    hw = oa.shape[-1]
    nt = s // tm
    tok = lambda n: pl.BlockSpec((None, tm, n), lambda bb, i: (bb, i, 0))
    const = lambda r, c: pl.BlockSpec((r, c), lambda bb, i: (0, 0))
    return pl.pallas_call(
        _merge_kernel,
        grid=(b, nt),
        in_specs=[tok(hw), tok(hw), tok(d), tok(d), tok(d),
                  pl.BlockSpec((None, N_MOD, d), lambda bb, i: (bb, 0, 0)),
                  const(hw, d), const(hw, d), const(d, d), const(1, d),
                  const(d, 2 * LANES), const(1, LANES)],
        out_specs=[tok(d), tok(d + LANES),
                   pl.BlockSpec((SUBLANES, tm), lambda bb, i: (0, bb * nt + i)),
                   const(N_CLASS_PAD, LANES)],
        out_shape=[jax.ShapeDtypeStruct((b, s, d), F32),
                   jax.ShapeDtypeStruct((b, s, d + LANES), F32),
                   jax.ShapeDtypeStruct((SUBLANES, b * s), I32),
                   jax.ShapeDtypeStruct((N_CLASS_PAD, LANES), I32)],
        scratch_shapes=[pltpu.VMEM((N_CLASS_PAD, LANES), F32)],
        compiler_params=pltpu.CompilerParams(
            dimension_semantics=("arbitrary", "arbitrary"), vmem_limit_bytes=VMEM_LIMIT),
        name="merge_route",
    )(oa, ob, sga, sgb, x, mod_l, woa, wob, wout, g2, rw2, rb)


def _positions_kernel(offs_ref, info_ref, pos_ref):
    cls = info_ref[0:1, :]
    pos = info_ref[1:2, :]
    for c in range(N_CLASSES):
        pos = pos + jnp.where(cls == c, offs_ref[c], 0)
    pos_ref[...] = jnp.broadcast_to(pos, pos_ref.shape)


def _positions(info, offs):
    return pl.pallas_call(
        _positions_kernel,
        in_specs=[pl.BlockSpec(memory_space=pltpu.SMEM), pl.BlockSpec(info.shape, lambda: (0, 0))],
        out_specs=pl.BlockSpec(info.shape, lambda: (0, 0)),
        out_shape=jax.ShapeDtypeStruct(info.shape, I32),
        name="positions",
    )(offs, info)[0]


def _dispatch_kernel(ends_ref, pos_ref, src_ref, dst_ref, zero_scr, sem, zero_sem):
    tm = pos_ref.shape[1]
    row_tile = zero_scr.shape[0]

    @pl.when(pl.program_id(0) == 0)
    def _():
        zero_scr[...] = jnp.zeros(zero_scr.shape, F32)

        def zero_tail(c):
            start = pl.multiple_of(ends_ref[c] - row_tile, row_tile)
            return pltpu.make_async_copy(zero_scr, dst_ref.at[pl.ds(start, row_tile), :], zero_sem)

        def nonempty(c):
            return ends_ref[c] > (ends_ref[c - 1] if c else 0)

        for c in range(N_CLASSES):
            @pl.when(nonempty(c))
            def _(c=c):
                zero_tail(c).start()
        for c in range(N_CLASSES):
            @pl.when(nonempty(c))
            def _(c=c):
                zero_tail(c).wait()

        def unused_tile(j):
            start = pl.multiple_of(ends_ref[N_CLASSES - 1] + j * row_tile, row_tile)
            return pltpu.make_async_copy(zero_scr, dst_ref.at[pl.ds(start, row_tile), :], zero_sem)

        n_unused = (dst_ref.shape[0] - ends_ref[N_CLASSES - 1]) // row_tile
        lax.fori_loop(0, n_unused, lambda j, carry: (unused_tile(j).start(), carry)[1], 0)
        lax.fori_loop(0, n_unused, lambda j, carry: (unused_tile(j).wait(), carry)[1], 0)

    for r in range(tm):
        pltpu.make_async_copy(src_ref.at[pl.ds(r, 1), :], dst_ref.at[pl.ds(pos_ref[0, r], 1), :],
                              sem).start(priority=r % 2)
    pltpu.make_async_copy(src_ref, dst_ref.at[pl.ds(0, tm), :], sem).wait()


def _dispatch(h2x, pos, ends, n_rows):
    t, w = h2x.shape
    tm = DISPATCH_TILE
    pos = pos.reshape(t // tm, 1, tm)
    smem_row = pl.BlockSpec((None, 1, tm), lambda i: (i, 0, 0), memory_space=pltpu.SMEM)
    return pl.pallas_call(
        _dispatch_kernel,
        grid=(t // tm,),
        in_specs=[pl.BlockSpec(memory_space=pltpu.SMEM), smem_row, pl.BlockSpec((tm, w), lambda i: (i, 0))],
        out_specs=pl.BlockSpec(memory_space=pl.ANY),
        out_shape=jax.ShapeDtypeStruct((n_rows, w), F32),
        scratch_shapes=[pltpu.VMEM((MOE_ROW_TILE, w), F32), pltpu.SemaphoreType.DMA, pltpu.SemaphoreType.DMA],
        compiler_params=pltpu.CompilerParams(
            dimension_semantics=("arbitrary",), vmem_limit_bytes=VMEM_LIMIT),
        name="dispatch",
    )(ends, pos, h2x)


def _experts_kernel(elo_ref, ehi_ref, live_ref, xs_ref, *refs):
    del elo_ref, ehi_ref
    ys_ref = refs[-1]
    d = ys_ref.shape[1]
    tm = MOE_ROW_TILE
    step = pl.program_id(0)

    for u in range(EXPERT_TILES_PER_STEP):
        wg_lo, wg_hi, wu_lo, wu_hi, wd_lo, wd_hi = refs[6 * u:6 * u + 6]
        rows = pl.ds(u * tm, tm)

        @pl.when(live_ref[step * EXPERT_TILES_PER_STEP + u] == 1)
        def _():
            h = xs_ref[rows, :d].astype(BF16)

            def expert(wg, wu, wd, weight):
                gate = _dot(h, wg[...])
                hidden = ((gate * jax.nn.sigmoid(gate)) * _dot(h, wu[...])) * weight
                return _dot(hidden.astype(BF16), wd[...])

            ys_ref[rows, :] = (expert(wg_lo, wu_lo, wd_lo, xs_ref[rows, d:d + 1])
                               + expert(wg_hi, wu_hi, wd_hi, xs_ref[rows, d + 1:d + 2]))

        @pl.when(live_ref[step * EXPERT_TILES_PER_STEP + u] == 0)
        def _():
            ys_ref[rows, :] = jnp.zeros((tm, d), F32)


def _experts(xs, e_lo, e_hi, live, wg, wu, wd, layer):
    n_rows, w = xs.shape
    _, ne, d, f = wg.shape
    per = EXPERT_TILES_PER_STEP
    tm = MOE_ROW_TILE
    assert n_rows % (per * tm) == 0

    def wspec(shape, use_hi, u):
        def index(step, elo, ehi, lv):
            return layer, (ehi if use_hi else elo)[step * per + u], 0, 0
        return pl.BlockSpec((None, None) + shape, index)

    wspecs, weights = [], []
    for u in range(per):
        wspecs += [wspec((d, f), 0, u), wspec((d, f), 1, u), wspec((d, f), 0, u), wspec((d, f), 1, u),
                   wspec((f, d), 0, u), wspec((f, d), 1, u)]
        weights += [wg, wg, wu, wu, wd, wd]
    grid_spec = pltpu.PrefetchScalarGridSpec(
        num_scalar_prefetch=3,
        grid=(n_rows // (per * tm),),
        in_specs=[pl.BlockSpec((per * tm, w), lambda step, elo, ehi, lv: (step, 0))] + wspecs,
        out_specs=pl.BlockSpec((per * tm, d), lambda step, elo, ehi, lv: (step, 0)),
    )
    return pl.pallas_call(
        _experts_kernel,
        grid_spec=grid_spec,
        out_shape=jax.ShapeDtypeStruct((n_rows, d), F32),
        compiler_params=pltpu.CompilerParams(
            dimension_semantics=("arbitrary",), vmem_limit_bytes=VMEM_LIMIT),
        name="experts",
    )(e_lo, e_hi, live, xs, *weights)


def _combine_kernel(pos_ref, next_ref, ys_ref, x1_ref, mod_ref, o_ref, buf, sem):
    tm = buf.shape[1]
    step = pl.program_id(0) * pl.num_programs(1) + pl.program_id(1)
    last = pl.num_programs(0) * pl.num_programs(1) - 1

    def issue(p_ref, slot):
        for r in range(tm):
            pltpu.make_async_copy(ys_ref.at[pl.ds(p_ref[slot, r], 1), :], buf.at[slot, pl.ds(r, 1), :],
                                  sem.at[slot]).start(priority=r % 2)

    def finish(slot):
        pltpu.make_async_copy(ys_ref.at[pl.ds(0, tm), :], buf.at[slot], sem.at[slot]).wait()
        rows = pl.ds(slot * tm, tm)
        o_ref[rows, :] = x1_ref[rows, :] + mod_ref[5:6, :] * buf[slot]

    @pl.when(step == 0)
    def _():
        issue(pos_ref, 0)

    issue(pos_ref, 1)
    finish(0)

    @pl.when(step < last)
    def _():
        issue(next_ref, 0)

    finish(1)


def _combine(ys, pos, x1, mod_l):
    b, s, d = x1.shape
    tm = MOE_DMA_TILE
    assert s % (2 * tm) == 0
    nt = s // (2 * tm)
    n_steps = b * nt
    pos2 = pos.reshape(n_steps, 2, tm)
    this_step = pl.BlockSpec((None, 2, tm), lambda bb, i: (bb * nt + i, 0, 0), memory_space=pltpu.SMEM)
    next_step = pl.BlockSpec((None, 2, tm), lambda bb, i: (jnp.minimum(bb * nt + i + 1, n_steps - 1), 0, 0),
                             memory_space=pltpu.SMEM)
    return pl.pallas_call(
        _combine_kernel,
        grid=(b, nt),
        in_specs=[this_step, next_step,
                  pl.BlockSpec(memory_space=pl.ANY),
                  pl.BlockSpec((None, 2 * tm, d), lambda bb, i: (bb, i, 0)),
                  pl.BlockSpec((None, N_MOD, d), lambda bb, i: (bb, 0, 0))],
        out_specs=pl.BlockSpec((None, 2 * tm, d), lambda bb, i: (bb, i, 0)),
        out_shape=jax.ShapeDtypeStruct((b, s, d), F32),
        scratch_shapes=[pltpu.VMEM((2, tm, d), F32), pltpu.SemaphoreType.DMA((2,))],
        compiler_params=pltpu.CompilerParams(
            dimension_semantics=("arbitrary", "arbitrary"), vmem_limit_bytes=VMEM_LIMIT),
        name="combine",
    )(pos2, pos2, ys, x1, mod_l)


def _moe(h2x, info, counts, wg, wu, wd, layer, x1, mod_l):
    b, s, d = x1.shape
    t = b * s
    tm = MOE_ROW_TILE
    n_tiles = t // tm + N_CLASSES
    n_tiles += -n_tiles % EXPERT_TILES_PER_STEP
    counts = counts[:N_CLASSES, 0]
    padded = (counts + tm - 1) // tm * tm
    ends = jnp.cumsum(padded)
    offs = jnp.pad(ends - padded, (0, N_CLASS_PAD - N_CLASSES)).astype(I32)
    tile_cls = jnp.sum(jnp.arange(n_tiles, dtype=I32)[:, None] * tm >= ends[None, :], axis=1)
    live = (tile_cls < N_CLASSES).astype(I32)
    tile_cls = jnp.minimum(tile_cls, N_CLASSES - 1)
    group, pair = tile_cls // N_PAIRS, tile_cls % N_PAIRS
    pairs = jnp.asarray(_PAIRS, I32)
    e_lo = group * EXPERTS_PER_GROUP + pairs[pair, 0]
    e_hi = group * EXPERTS_PER_GROUP + pairs[pair, 1]

    pos = _positions(info, offs).reshape(t // MOE_DMA_TILE, 1, MOE_DMA_TILE)
    xs = _dispatch(h2x.reshape(t, d + LANES), pos, ends.astype(I32), n_tiles * tm)
    ys = _experts(xs, e_lo, e_hi, live, wg, wu, wd, layer)
    return _combine(ys, pos, x1, mod_l)


def kernel(x, c, w_ada, b_ada, norm1_g, w_in, q_norm_g, k_norm_g, rel_bias, w_o_a, w_o_b, w_out,
           norm2_g, router_w, router_b, w_gate, w_up, w_down):
    b, s, d = x.shape
    depth = w_ada.shape[0]
    assert d % LANES == 0
    tm = min(TOKEN_TILE, s)

    mod = _adaln(c, w_ada, b_ada)
    bias = _bias_tiles(rel_bias, DSA_QUERY_BLOCK, DSA_KEY_BLOCK)
    rw = jnp.pad(router_w, ((0, 0), (0, LANES - N_EXPERTS)))
    rwh = rw.astype(BF16)
    rw2 = jnp.concatenate([rwh, (rw - rwh.astype(F32)).astype(BF16)], axis=1)
    rb = jnp.pad(router_b, (0, LANES - N_EXPERTS)).reshape(1, LANES)
    wg, wu, wd = w_gate.astype(BF16), w_up.astype(BF16), w_down.astype(BF16)

    for l in range(depth):
        qa, ka, va, qbt, qit, kb, vt, ki, wit, sga, sgb = _in_proj(
            x, mod[l], norm1_g[l].reshape(1, d), _pack_w_in(w_in[l]),
            q_norm_g[l].reshape(HEAD_DIM, 1), k_norm_g[l].reshape(1, HEAD_DIM), tm)
        oa = _sb_attention(qa, ka, va)
        ob = _dsa_attention(qit, ki, wit, qbt, kb, vt, bias)
        x1, h2x, info, counts = _merge(oa, ob, sga, sgb, x, mod[l], w_o_a[l].astype(BF16),
                                       w_o_b[l].astype(BF16), w_out[l].astype(BF16),
                                       norm2_g[l].reshape(1, d), rw2, rb, tm)
        x = _moe(h2x, info, counts, wg, wu, wd, l, x1, mod[l])
    return x
```

```python
import functools
import math

import numpy as np
import jax
import jax.numpy as jnp
from jax import lax
from jax.experimental import pallas as pl
from jax.experimental.pallas import tpu as pltpu

F32 = jnp.float32
BF16 = jnp.bfloat16
I32 = jnp.int32
I16 = jnp.int16

HEAD_DIM = 64
N_HEADS = 8
N_EXPERTS = 16
N_GROUPS = 4
EXPERTS_PER_GROUP = 4
_PAIRS = ((0, 1), (0, 2), (0, 3), (1, 2), (1, 3), (2, 3))
N_PAIRS = len(_PAIRS)
N_CLASSES = N_GROUPS * N_PAIRS
N_CLASS_PAD = 32
MOE_ROW_TILE = 256
EXPERT_TILES_PER_STEP = 2
MOE_DMA_TILE = 512
DISPATCH_TILE = 1024
N_MOD = 6
EPS = 1e-6
INDEX_TOPK_MAX = 256
REL_BUCKETS = 32
REL_MAX_DIST = 128
LANES = 128
SUBLANES = 8
PACKED_SUBLANES = 16
HALF_BITS = 16
TOKEN_TILE = 512
EXP2_MAX = 126.0
DSA_QUERY_BLOCK = 256
DSA_KEY_BLOCK = 128
SB_QUERY_BLOCK = 256
SB_KEY_BLOCK = 256
SB_DEAD_LOG2 = 150.0
LOG2E = 1.4426950408889634
NEG_BIG = -1e30
INT_MIN = -2 ** 31
INT_MAX = 2 ** 31 - 1
HALF_BIAS = 2 ** 15

VMEM_LIMIT = 56 * 1024 * 1024


def _nt_dot(a, b):
    return lax.dot_general(a, b, (((1,), (1,)), ((), ())), preferred_element_type=F32)


def _dot(a, b):
    return jnp.dot(a, b, preferred_element_type=F32)


def _mod_kernel(c_ref, w_ref, b_ref, o_ref):
    c = c_ref[...]
    sc = c * jax.nn.sigmoid(c)
    o_ref[...] = _dot(sc.astype(BF16), w_ref[...].astype(BF16)) + b_ref[...]


def _adaln(c, w_ada, b_ada):
    depth, d, nd = w_ada.shape
    b = c.shape[0]
    out = pl.pallas_call(
        _mod_kernel,
        grid=(depth, nd // (2 * d)),
        in_specs=[pl.BlockSpec((b, d), lambda l, j: (0, 0)),
                  pl.BlockSpec((None, d, 2 * d), lambda l, j: (l, 0, j)),
                  pl.BlockSpec((None, 1, 2 * d), lambda l, j: (l, 0, j))],
        out_specs=pl.BlockSpec((None, b, 2 * d), lambda l, j: (l, 0, j)),
        out_shape=jax.ShapeDtypeStruct((depth, b, nd), F32),
        compiler_params=pltpu.CompilerParams(vmem_limit_bytes=VMEM_LIMIT),
        name="adaln",
    )(c, w_ada, b_ada.reshape(depth, 1, nd))
    return out.reshape(depth, b, N_MOD, d)


def _rms(x):
    return x * lax.rsqrt(jnp.mean(x * x, axis=-1, keepdims=True) + EPS)


def _in_kernel(x_ref, mod_ref, g1_ref, w_ref, qng_ref, kng_ref,
               qa_ref, ka_ref, va_ref, qbt_ref, qit_ref, kb_ref, vt_ref, ki_ref, wit_ref,
               sga_ref, sgb_ref):
    d = x_ref.shape[-1]
    tm = x_ref.shape[0]
    hw = N_HEADS * HEAD_DIM
    x = x_ref[...]
    h = (_rms(x) * g1_ref[...]) * (1.0 + mod_ref[1:2, :]) + mod_ref[0:1, :]
    hb = h.astype(BF16)

    def proj(lo, n):
        return _dot(hb, w_ref[:, lo:lo + n])

    def heads(r, ref):
        for pp in range(N_HEADS // 2):
            ref[pp] = r[:, pp * LANES:(pp + 1) * LANES].astype(ref.dtype)

    scale = HEAD_DIM ** -0.5
    heads(proj(0, hw) * (scale * LOG2E), qa_ref)
    heads(proj(hw, hw), ka_ref)
    heads(proj(2 * hw, hw), va_ref)

    rt = proj(3 * hw, hw).T
    for hh in range(N_HEADS):
        slab = rt[hh * HEAD_DIM:(hh + 1) * HEAD_DIM, :]
        inv = lax.rsqrt(jnp.mean(slab * slab, axis=0, keepdims=True) + EPS)
        qbt_ref[hh] = (((slab * inv) * qng_ref[...]) * scale).astype(BF16)

    r = proj(4 * hw, 2 * HEAD_DIM)
    kb_ref[...] = (_rms(r[:, :HEAD_DIM]) * kng_ref[...]).astype(BF16)
    vt = r.T[HEAD_DIM:, :]
    ones_row = (lax.broadcasted_iota(I32, (HEAD_DIM, LANES), 0) == 0).astype(BF16)
    for cc in range(tm // LANES):
        vt_ref[cc, :HEAD_DIM, :] = vt[:, cc * LANES:(cc + 1) * LANES].astype(BF16)
        vt_ref[cc, HEAD_DIM:, :] = ones_row

    rt = proj(4 * hw + LANES, hw).T
    for hh in range(N_HEADS):
        qit_ref[hh] = rt[hh * HEAD_DIM:(hh + 1) * HEAD_DIM, :].astype(BF16)

    r = proj(5 * hw + LANES, LANES)
    ki_ref[...] = r[:, :HEAD_DIM].astype(BF16)
    wit_ref[...] = r.T[HEAD_DIM:HEAD_DIM + N_HEADS, :]

    off = 5 * hw + 2 * LANES
    sga_ref[...] = jax.nn.sigmoid(proj(off, d)).astype(BF16)
    sgb_ref[...] = jax.nn.sigmoid(proj(off + d, d)).astype(BF16)


def _pack_w_in(w_in):
    hw = N_HEADS * HEAD_DIM
    a = 4 * hw + 2 * HEAD_DIM + hw
    small = w_in[..., a:a + HEAD_DIM + N_HEADS]
    small = jnp.pad(small, ((0, 0), (0, 0), (0, LANES - small.shape[-1])))
    gates = w_in[..., a + HEAD_DIM + N_HEADS:]
    return jnp.concatenate([w_in[..., :a], small, gates], axis=-1).astype(BF16)


def _in_proj(x, mod_l, g1, w_packed, layer, qng, kng, tm):
    b, s, d = x.shape
    hm = jax.ShapeDtypeStruct((b, N_HEADS // 2, s, LANES), BF16)
    hmt = jax.ShapeDtypeStruct((b, N_HEADS, HEAD_DIM, s), BF16)
    tok64 = jax.ShapeDtypeStruct((b, s, HEAD_DIM), BF16)
    hm_spec = pl.BlockSpec((None, N_HEADS // 2, tm, LANES), lambda bb, i: (bb, 0, i, 0))
    hmt_spec = pl.BlockSpec((None, N_HEADS, HEAD_DIM, tm), lambda bb, i: (bb, 0, 0, i))
    tok = lambda n: pl.BlockSpec((None, tm, n), lambda bb, i: (bb, i, 0))
    nw = w_packed.shape[-1]
    return pl.pallas_call(
        _in_kernel,
        grid=(b, s // tm),
        in_specs=[tok(d),
                  pl.BlockSpec((None, N_MOD, d), lambda bb, i: (bb, 0, 0)),
                  pl.BlockSpec((1, d), lambda bb, i: (0, 0)),
                  pl.BlockSpec((None, d, nw), lambda bb, i: (layer, 0, 0)),
                  pl.BlockSpec((HEAD_DIM, 1), lambda bb, i: (0, 0)),
                  pl.BlockSpec((1, HEAD_DIM), lambda bb, i: (0, 0))],
        out_specs=[hm_spec] * 3 + [hmt_spec] * 2
                  + [tok(HEAD_DIM),
                     pl.BlockSpec((None, tm // LANES, 2 * HEAD_DIM, LANES), lambda bb, i: (bb, i, 0, 0)),
                     tok(HEAD_DIM),
                     pl.BlockSpec((None, N_HEADS, tm), lambda bb, i: (bb, 0, i)),
                     tok(d), tok(d)],
        out_shape=[hm] * 3 + [hmt] * 2
                  + [tok64,
                     jax.ShapeDtypeStruct((b, s // LANES, 2 * HEAD_DIM, LANES), BF16),
                     tok64,
                     jax.ShapeDtypeStruct((b, N_HEADS, s), F32),
                     jax.ShapeDtypeStruct((b, s, d), BF16),
                     jax.ShapeDtypeStruct((b, s, d), BF16)],
        compiler_params=pltpu.CompilerParams(
            dimension_semantics=("parallel", "parallel"), vmem_limit_bytes=VMEM_LIMIT),
        name="in_proj",
    )(x, mod_l, g1, w_packed, qng, kng)


def _sb_kernel(q_ref, k_ref, v_ref, o_ref, acc_scr, csum_scr, sp_scr, ls_scr, qm_scr):
    tq = q_ref.shape[1]
    tk = SB_KEY_BLOCK
    i = pl.program_id(1)
    nfull = (i * tq) // tk
    row = lax.broadcasted_iota(I32, (tq, tk), 0)
    col = lax.broadcasted_iota(I32, (tq, tk), 1)
    before = (col - row) < (i * tq - nfull * tk)
    krow = lax.broadcasted_iota(I32, (tk, tk), 0)
    kcol = lax.broadcasted_iota(I32, (tk, tk), 1)
    suffix = (krow > kcol).astype(BF16)
    low_half = lax.broadcasted_iota(I32, (tq, LANES), 1) < HEAD_DIM
    for hh in range(N_HEADS):
        slab = q_ref[hh // 2]
        qm_scr[hh] = jnp.where(low_half == (hh % 2 == 0), slab, jnp.zeros_like(slab))

    def stage_a(hh, st, first):
        z2 = _nt_dot(qm_scr[hh], k_ref[hh // 2, pl.ds(st, tk), :])
        sp2 = jnp.maximum(jnp.log2(1.0 + jnp.exp2(jnp.minimum(z2, EXP2_MAX))), z2)
        if first:
            sp2 = jnp.where(before, sp2, 0.0)
        sp_scr[hh] = sp2.astype(BF16)
        ls_scr[hh] = z2 - sp2

    def stage_b(hh, st, first):
        sp = sp_scr[hh]
        suf = _dot(sp, suffix)
        if first:
            a = jnp.where(before, jnp.exp2(ls_scr[hh] - suf), 0.0)
            csum = jnp.zeros((tq, 1), F32)
        else:
            csum = csum_scr[hh]
            a = jnp.exp2((ls_scr[hh] - suf) - csum)
        c = _dot(a.astype(BF16), v_ref[hh // 2, pl.ds(st, tk), :])
        if first:
            acc_scr[hh] = c
        else:
            acc_scr[hh] += c
        csum_scr[hh] = csum + (suf[:, 0:1] + sp[:, 0:1].astype(F32))

    def key_block(st, first):
        for hh in range(N_HEADS):
            stage_a(hh, st, first)
        for hh in range(N_HEADS):
            stage_b(hh, st, first)

    key_block(pl.multiple_of(nfull * tk, tk), True)

    def live():
        return jnp.min(csum_scr[...]) < SB_DEAD_LOG2

    def body(carry):
        jj, _ = carry
        key_block(pl.multiple_of((nfull - 1 - jj) * tk, tk), False)
        return jj + 1, live()

    lax.while_loop(lambda carry: (carry[0] < nfull) & carry[1], body, (jnp.int32(0), live()))
    for pp in range(N_HEADS // 2):
        o_ref[:, pp * LANES:(pp + 1) * LANES] = jnp.where(
            low_half, acc_scr[2 * pp], acc_scr[2 * pp + 1]).astype(o_ref.dtype)


def _sb_attention(qa, ka, va):
    b, npair, s, _ = qa.shape
    nh, dh = 2 * npair, HEAD_DIM
    tq = SB_QUERY_BLOCK
    assert SB_KEY_BLOCK % tq == 0 and s % SB_KEY_BLOCK == 0
    return pl.pallas_call(
        _sb_kernel,
        grid=(b, s // tq),
        in_specs=[pl.BlockSpec((None, npair, tq, LANES), lambda bb, i: (bb, 0, i, 0)),
                  pl.BlockSpec((None, npair, s, LANES), lambda bb, i: (bb, 0, 0, 0)),
                  pl.BlockSpec((None, npair, s, LANES), lambda bb, i: (bb, 0, 0, 0))],
        out_specs=pl.BlockSpec((None, tq, nh * dh), lambda bb, i: (bb, i, 0)),
        out_shape=jax.ShapeDtypeStruct((b, s, nh * dh), BF16),
        scratch_shapes=[pltpu.VMEM((nh, tq, LANES), F32), pltpu.VMEM((nh, tq, 1), F32),
                        pltpu.VMEM((nh, tq, SB_KEY_BLOCK), BF16), pltpu.VMEM((nh, tq, SB_KEY_BLOCK), F32),
                        pltpu.VMEM((nh, tq, LANES), BF16)],
        compiler_params=pltpu.CompilerParams(
            dimension_semantics=("parallel", "parallel"), vmem_limit_bytes=VMEM_LIMIT),
        name="stick_breaking",
    )(qa, ka, va)


def _t5_bucket_np(dist):
    max_exact = REL_BUCKETS // 2
    d_f = np.maximum(dist, 1).astype(np.float32)
    large = max_exact + (np.log(d_f / np.float32(max_exact))
                         / np.float32(math.log(REL_MAX_DIST / max_exact))
                         * np.float32(REL_BUCKETS - max_exact)).astype(np.int32)
    large = np.minimum(large, REL_BUCKETS - 1)
    return np.where(dist < max_exact, dist, large).astype(np.int32)


def _bias_tiles(rel_bias, tq, tk):
    n_near = tq // tk + 1
    key = np.arange(tk)[:, None]
    query = np.arange(tq)[None, :]
    bk = np.stack([_t5_bucket_np(np.maximum(query - key - (n - 1) * tk, 0)) for n in range(n_near)])
    far = int(_t5_bucket_np(np.array([REL_MAX_DIST]))[0])
    nh = rel_bias.shape[1]

    def body(rb_ref, bk_ref, o_ref):
        buckets = bk_ref[...]
        for hh in range(nh):
            tile = jnp.zeros((tk, tq), F32)
            for bucket in range(REL_BUCKETS):
                tile = jnp.where(buckets == bucket, rb_ref[hh, bucket] - rb_ref[hh, far], tile)
            o_ref[hh] = tile

    return pl.pallas_call(
        body,
        grid=(n_near,),
        in_specs=[pl.BlockSpec(memory_space=pltpu.SMEM),
                  pl.BlockSpec((None, tk, tq), lambda n: (n, 0, 0))],
        out_specs=pl.BlockSpec((None, nh, tk, tq), lambda n: (n, 0, 0, 0)),
        out_shape=jax.ShapeDtypeStruct((n_near, nh, tk, tq), F32),
        name="bias_tiles",
    )(rel_bias.T, jnp.asarray(bk))


def _sort_key(x):
    bits = pltpu.bitcast(x, I32)
    return bits ^ ((bits >> 31) & INT_MAX)


def _fold_rows(x, op):
    return functools.reduce(
        op, [x[g * SUBLANES:(g + 1) * SUBLANES, :] for g in range(x.shape[0] // SUBLANES)])


def _dsa_kernel(ktop, qit_ref, ki_ref, wit_ref, qbt_ref, kb_ref, vt_ref, bias_ref, o_ref,
                key_scr, hi_scr, lo_scr, m_scr, acc_scr, s_scr):
    tq = qit_ref.shape[2]
    tk = DSA_KEY_BLOCK
    per_q = tq // tk
    i = pl.program_id(1)
    first_own = per_q * i
    nchunks = per_q * (i + 1)
    key_i = lax.broadcasted_iota(I32, (tk, tq), 0)
    qry_i = lax.broadcasted_iota(I32, (tk, tq), 1)

    def causal(rel):
        return (key_i - qry_i) <= (-rel * tk)

    def for_chunks(fn):
        def far(j, carry):
            for u in range(per_q):
                fn(j * per_q + u, None)
            return carry
        lax.fori_loop(0, i - 1, far, 0)

        @pl.when(i >= 1)
        def _():
            for rel in range(-per_q, -1):
                fn(first_own + rel, None)
            fn(first_own - 1, -1)

        for rel in range(per_q):
            fn(first_own + rel, rel)

    def score_chunk(c, rel):
        kic = ki_ref[pl.ds(pl.multiple_of(c * tk, tk), tk), :]
        score = jnp.zeros((tk, tq), F32)
        for hh in range(N_HEADS):
            score = score + wit_ref[hh:hh + 1, :] * jnp.maximum(_dot(kic, qit_ref[hh]), 0.0)
        keys = _sort_key(score)
        if rel is not None and rel >= 0:
            keys = jnp.where(causal(rel), keys, INT_MIN)
        key_scr[c] = keys
        hi_scr[c] = (keys >> HALF_BITS).astype(I16)
        lo_scr[c] = ((keys & (2 * HALF_BIAS - 1)) - HALF_BIAS).astype(I16)

    for_chunks(score_chunk)

    def count(pred):
        def body(j, part):
            for u in range(per_q):
                part = part + _fold_rows(jnp.where(pred(key_scr[j * per_q + u]), 1, 0), jnp.add)
            return part
        part = lax.fori_loop(0, i + 1, body, jnp.zeros((SUBLANES, tq), I32))
        return jnp.sum(part, axis=0, keepdims=True)

    def count16(scr, pred):
        one, zero = jnp.int16(1), jnp.int16(0)

        def block(j, part):
            for u in range(per_q):
                hit = jnp.where(pred(scr[j * per_q + u]), one, zero)
                part = part + functools.reduce(
                    jnp.add, [hit[g * PACKED_SUBLANES:(g + 1) * PACKED_SUBLANES, :]
                              for g in range(tk // PACKED_SUBLANES)])
            return part

        nblk = i + 1
        part = lax.fori_loop(0, nblk // 2, lambda jj, part: block(2 * jj + 1, block(2 * jj, part)),
                             jnp.zeros((PACKED_SUBLANES, tq), I16))
        part = lax.cond(nblk % 2 == 1, lambda p: block(nblk - 1, p), lambda p: p, part)
        return jnp.sum(part.astype(I32), axis=0, keepdims=True)

    def search16(scr, base):
        def bit_body(bi, t_u):
            cand_u = t_u | jnp.left_shift(jnp.int32(1), HALF_BITS - 1 - bi)
            cand = (cand_u - HALF_BIAS).astype(I16)
            return jnp.where(base + count16(scr, lambda k: k >= cand) >= ktop, cand_u, t_u)
        return lax.fori_loop(0, HALF_BITS, bit_body, jnp.zeros((1, tq), I32))

    def find_threshold():
        t_hi = search16(hi_scr, 0) - HALF_BIAS
        t_hi16 = t_hi.astype(I16)
        above = count16(hi_scr, lambda k: k > t_hi16)

        def keep_band(j, carry):
            for u in range(per_q):
                c = j * per_q + u
                lo_scr[c] = jnp.where(hi_scr[c] == t_hi16, lo_scr[c], jnp.int16(-HALF_BIAS))
            return carry

        lax.fori_loop(0, i + 1, keep_band, 0)
        return t_hi * (2 * HALF_BIAS) + search16(lo_scr, above)

    thr = lax.cond((i + 1) * tq <= ktop, lambda: jnp.full((1, tq), INT_MIN, I32), find_threshold)
    n_ge = count(lambda k: k >= thr)

    @pl.when(jnp.max(n_ge) > ktop)
    def _():
        need = (ktop - count(lambda k: k > thr)).astype(F32)
        lower = (lax.broadcasted_iota(I32, (tk, tk), 1)
                 < lax.broadcasted_iota(I32, (tk, tk), 0)).astype(BF16)

        def body(c, seen):
            k = key_scr[c]
            eq = k == thr
            eqf = eq.astype(BF16)
            rank = seen + _dot(lower, eqf)
            drop = eq & (rank >= need) & (n_ge > ktop)
            key_scr[c] = jnp.where(drop, thr - 1, k)
            return seen + jnp.sum(eqf.astype(F32), axis=0, keepdims=True)

        lax.fori_loop(0, nchunks, body, jnp.zeros((1, tq), F32))

    def logits(c, rel):
        sel = key_scr[c] >= thr
        if rel is not None and rel >= 0:
            sel = sel & causal(rel)
        kbc = kb_ref[pl.ds(pl.multiple_of(c * tk, tk), tk), :]

        def head(hh):
            s = _dot(kbc, qbt_ref[hh])
            if rel is not None:
                s = s + bias_ref[rel + 1, hh]
            return s
        return sel, head

    m_scr[...] = jnp.full(m_scr.shape, NEG_BIG, F32)

    def max_chunk(c, rel):
        sel, head = logits(c, rel)
        for hh in range(N_HEADS):
            s = jnp.where(sel, head(hh), NEG_BIG)
            s_scr[hh, c] = s
            m_scr[hh] = jnp.maximum(m_scr[hh], _fold_rows(s, jnp.maximum))

    for_chunks(max_chunk)
    m = [jnp.max(m_scr[hh], axis=0, keepdims=True) for hh in range(N_HEADS)]

    acc_scr[...] = jnp.zeros(acc_scr.shape, F32)

    def att_block(j, carry):
        for u in range(per_q):
            c = j * per_q + u
            vtc = vt_ref[c]
            for hh in range(N_HEADS):
                p = jnp.exp(s_scr[hh, c] - m[hh])
                acc_scr[hh] += _dot(vtc, p.astype(BF16))
        return carry

    lax.fori_loop(0, i + 1, att_block, 0)

    out_t = jnp.concatenate(
        [acc_scr[hh, :HEAD_DIM, :] / acc_scr[hh, HEAD_DIM:HEAD_DIM + 1, :] for hh in range(N_HEADS)],
        axis=0)
    o_ref[...] = out_t.T.astype(o_ref.dtype)


def _dsa_attention(qit, ki, wit, qbt, kb, vt, bias):
    b, nh, dh, s = qit.shape
    tq = DSA_QUERY_BLOCK
    tk = DSA_KEY_BLOCK
    assert tk == REL_MAX_DIST and tk == LANES and tq % tk == 0 and s % tq == 0
    ktop = min(INDEX_TOPK_MAX, s // 4)
    hmt_q = pl.BlockSpec((None, nh, dh, tq), lambda bb, i: (bb, 0, 0, i))
    full64 = pl.BlockSpec((None, s, dh), lambda bb, i: (bb, 0, 0))
    return pl.pallas_call(
        functools.partial(_dsa_kernel, ktop),
        grid=(b, s // tq),
        in_specs=[hmt_q, full64,
                  pl.BlockSpec((None, nh, tq), lambda bb, i: (bb, 0, i)),
                  hmt_q, full64,
                  pl.BlockSpec((None, s // tk, 2 * dh, tk), lambda bb, i: (bb, 0, 0, 0)),
                  pl.BlockSpec(bias.shape, lambda bb, i: (0, 0, 0, 0))],
        out_specs=pl.BlockSpec((None, tq, nh * dh), lambda bb, i: (bb, i, 0)),
        out_shape=jax.ShapeDtypeStruct((b, s, nh * dh), BF16),
        scratch_shapes=[pltpu.VMEM((s // tk, tk, tq), I32),
                        pltpu.VMEM((s // tk, tk, tq), I16),
                        pltpu.VMEM((s // tk, tk, tq), I16),
                        pltpu.VMEM((nh, SUBLANES, tq), F32),
                        pltpu.VMEM((nh, 2 * dh, tq), F32),
                        pltpu.VMEM((nh, s // tk, tk, tq), F32)],
        compiler_params=pltpu.CompilerParams(
            dimension_semantics=("parallel", "parallel"), vmem_limit_bytes=VMEM_LIMIT),
        name="sparse_attention",
    )(qit, ki, wit, qbt, kb, vt, bias)


def _first_max4(a):
    m1 = jnp.maximum(jnp.maximum(a[0], a[1]), jnp.maximum(a[2], a[3]))
    i1 = jnp.where(a[0] == m1, 0, jnp.where(a[1] == m1, 1, jnp.where(a[2] == m1, 2, 3)))
    rest = [jnp.where(i1 == j, -1.0, a[j]) for j in range(4)]
    m2 = jnp.maximum(jnp.maximum(rest[0], rest[1]), jnp.maximum(rest[2], rest[3]))
    i2 = jnp.where(rest[0] == m2, 0, jnp.where(rest[1] == m2, 1, jnp.where(rest[2] == m2, 2, 3)))
    return m1, i1, m2, i2


def _route(lt):
    lg = [lt[e:e + 1, :] for e in range(N_EXPERTS)]
    mx = functools.reduce(jnp.maximum, lg)
    ex = [jnp.exp(v - mx) for v in lg]
    tot = functools.reduce(lambda u, v: u + v, ex)
    p = [v / tot for v in ex]
    gs = []
    for g in range(N_GROUPS):
        m1, _, m2, _ = _first_max4(p[4 * g:4 * g + 4])
        gs.append(m1 + m2)
    _, gbest, _, _ = _first_max4(gs)
    chosen = [jnp.where(gbest == 0, p[j], jnp.where(gbest == 1, p[4 + j],
              jnp.where(gbest == 2, p[8 + j], p[12 + j]))) for j in range(4)]
    m1, i1, m2, i2 = _first_max4(chosen)
    den = m1 + m2
    lo = jnp.minimum(i1, i2)
    hi = jnp.maximum(i1, i2)
    pair = jnp.where(lo == 0, hi - 1, jnp.where(lo == 1, hi + 1, N_PAIRS - 1))
    first_is_lo = i1 < i2
    w1 = m1 / den
    w2 = m2 / den
    return gbest * N_PAIRS + pair, jnp.where(first_is_lo, w1, w2), jnp.where(first_is_lo, w2, w1)


def _merge_kernel(oa_ref, ob_ref, sga_ref, sgb_ref, x_ref, mod_ref, woa_ref, wob_ref, wout_ref,
                  g2_ref, rw2_ref, rb_ref, x1_ref, h2x_ref, info_ref, cnt_ref, carry_scr):
    tm, d = x_ref.shape

    @pl.when((pl.program_id(0) == 0) & (pl.program_id(1) == 0))
    def _():
        carry_scr[...] = jnp.zeros(carry_scr.shape, F32)

    merged = (sga_ref[...].astype(F32) * _dot(oa_ref[...], woa_ref[...])
              + sgb_ref[...].astype(F32) * _dot(ob_ref[...], wob_ref[...]))
    x1 = x_ref[...] + mod_ref[2:3, :] * _dot(merged.astype(BF16), wout_ref[...])
    x1_ref[...] = x1
    h2 = (_rms(x1) * g2_ref[...]) * (1.0 + mod_ref[4:5, :]) + mod_ref[3:4, :]
    h2x_ref[:, :d] = h2
    hi = h2.astype(BF16)
    lo = (h2 - hi.astype(F32)).astype(BF16)
    both = _dot(hi, rw2_ref[...])
    logits = (both[:, :LANES] + _dot(lo, rw2_ref[:, :LANES]) + both[:, LANES:]) + rb_ref[...]
    cls, cw_lo, cw_hi = _route(logits.T)

    ncls = carry_scr.shape[0]
    onehot = (lax.broadcasted_iota(I32, (ncls, tm), 0) == cls).astype(F32)
    earlier = (lax.broadcasted_iota(I32, (tm, tm), 0)
               < lax.broadcasted_iota(I32, (tm, tm), 1)).astype(BF16)
    before = _dot(onehot.astype(BF16), earlier)
    carry = carry_scr[...]
    carry_t = jnp.concatenate([carry] * (tm // LANES), axis=1)
    rank = jnp.sum(onehot * (carry_t + before), axis=0, keepdims=True).astype(I32)
    carry = carry + jnp.sum(onehot, axis=1, keepdims=True)
    carry_scr[...] = carry
    cnt_ref[...] = carry.astype(I32)

    sub = lax.broadcasted_iota(I32, (SUBLANES, tm), 0)
    info_ref[...] = jnp.where(sub == 0, cls, jnp.where(sub == 1, rank, 0))
    sub = lax.broadcasted_iota(I32, (LANES, tm), 0)
    weights_t = jnp.where(sub == 0, cw_lo, jnp.where(sub == 1, cw_hi, 0.0))
    h2x_ref[:, d:] = weights_t.T


def _merge(oa, ob, sga, sgb, x, mod_l, woa, wob, wout, layer, g2, rw2, rb, tm):
    b, s, d = x.shape
    hw = oa.shape[-1]
    nt = s // tm
    tok = lambda n: pl.BlockSpec((None, tm, n), lambda bb, i: (bb, i, 0))
    const = lambda r, c: pl.BlockSpec((r, c), lambda bb, i: (0, 0))
    layered = lambda r, c: pl.BlockSpec((None, r, c), lambda bb, i: (layer, 0, 0))
    return pl.pallas_call(
        _merge_kernel,
        grid=(b, nt),
        in_specs=[tok(hw), tok(hw), tok(d), tok(d), tok(d),
                  pl.BlockSpec((None, N_MOD, d), lambda bb, i: (bb, 0, 0)),
                  layered(hw, d), layered(hw, d), layered(d, d), const(1, d),
                  const(d, 2 * LANES), const(1, LANES)],
        out_specs=[tok(d), tok(d + LANES),
                   pl.BlockSpec((SUBLANES, tm), lambda bb, i: (0, bb * nt + i)),
                   const(N_CLASS_PAD, LANES)],
        out_shape=[jax.ShapeDtypeStruct((b, s, d), F32),
                   jax.ShapeDtypeStruct((b, s, d + LANES), F32),
                   jax.ShapeDtypeStruct((SUBLANES, b * s), I32),
                   jax.ShapeDtypeStruct((N_CLASS_PAD, LANES), I32)],
        scratch_shapes=[pltpu.VMEM((N_CLASS_PAD, LANES), F32)],
        compiler_params=pltpu.CompilerParams(
            dimension_semantics=("arbitrary", "arbitrary"), vmem_limit_bytes=VMEM_LIMIT),
        name="merge_route",
    )(oa, ob, sga, sgb, x, mod_l, woa, wob, wout, g2, rw2, rb)


def _positions_kernel(offs_ref, info_ref, pos_ref):
    cls = info_ref[0:1, :]
    pos = info_ref[1:2, :]
    for c in range(N_CLASSES):
        pos = pos + jnp.where(cls == c, offs_ref[c], 0)
    pos_ref[...] = jnp.broadcast_to(pos, pos_ref.shape)


def _positions(info, offs):
    return pl.pallas_call(
        _positions_kernel,
        in_specs=[pl.BlockSpec(memory_space=pltpu.SMEM), pl.BlockSpec(info.shape, lambda: (0, 0))],
        out_specs=pl.BlockSpec(info.shape, lambda: (0, 0)),
        out_shape=jax.ShapeDtypeStruct(info.shape, I32),
        name="positions",
    )(offs, info)[0]


def _dispatch_kernel(ends_ref, pos_ref, src_ref, dst_ref, zero_scr, sem, zero_sem):
    tm = pos_ref.shape[1]
    row_tile = zero_scr.shape[0]

    @pl.when(pl.program_id(0) == 0)
    def _():
        zero_scr[...] = jnp.zeros(zero_scr.shape, F32)

        def zero_tail(c):
            start = pl.multiple_of(ends_ref[c] - row_tile, row_tile)
            return pltpu.make_async_copy(zero_scr, dst_ref.at[pl.ds(start, row_tile), :], zero_sem)

        def nonempty(c):
            return ends_ref[c] > (ends_ref[c - 1] if c else 0)

        for c in range(N_CLASSES):
            @pl.when(nonempty(c))
            def _(c=c):
                zero_tail(c).start()
        for c in range(N_CLASSES):
            @pl.when(nonempty(c))
            def _(c=c):
                zero_tail(c).wait()

        def unused_tile(j):
            start = pl.multiple_of(ends_ref[N_CLASSES - 1] + j * row_tile, row_tile)
            return pltpu.make_async_copy(zero_scr, dst_ref.at[pl.ds(start, row_tile), :], zero_sem)

        n_unused = (dst_ref.shape[0] - ends_ref[N_CLASSES - 1]) // row_tile
        lax.fori_loop(0, n_unused, lambda j, carry: (unused_tile(j).start(), carry)[1], 0)
        lax.fori_loop(0, n_unused, lambda j, carry: (unused_tile(j).wait(), carry)[1], 0)

    for r in range(tm):
        pltpu.make_async_copy(src_ref.at[pl.ds(r, 1), :], dst_ref.at[pl.ds(pos_ref[0, r], 1), :],
                              sem).start(priority=r % 2)
    pltpu.make_async_copy(src_ref, dst_ref.at[pl.ds(0, tm), :], sem).wait()


def _dispatch(h2x, pos, ends, n_rows):
    t, w = h2x.shape
    tm = DISPATCH_TILE
    pos = pos.reshape(t // tm, 1, tm)
    smem_row = pl.BlockSpec((None, 1, tm), lambda i: (i, 0, 0), memory_space=pltpu.SMEM)
    return pl.pallas_call(
        _dispatch_kernel,
        grid=(t // tm,),
        in_specs=[pl.BlockSpec(memory_space=pltpu.SMEM), smem_row, pl.BlockSpec((tm, w), lambda i: (i, 0))],
        out_specs=pl.BlockSpec(memory_space=pl.ANY),
        out_shape=jax.ShapeDtypeStruct((n_rows, w), F32),
        scratch_shapes=[pltpu.VMEM((MOE_ROW_TILE, w), F32), pltpu.SemaphoreType.DMA, pltpu.SemaphoreType.DMA],
        compiler_params=pltpu.CompilerParams(
            dimension_semantics=("arbitrary",), vmem_limit_bytes=VMEM_LIMIT),
        name="dispatch",
    )(ends, pos, h2x)


def _experts_kernel(elo_ref, ehi_ref, live_ref, xs_ref, *refs):
    del elo_ref, ehi_ref
    ys_ref = refs[-1]
    d = ys_ref.shape[1]
    tm = MOE_ROW_TILE
    step = pl.program_id(0)

    for u in range(EXPERT_TILES_PER_STEP):
        wg_lo, wg_hi, wu_lo, wu_hi, wd_lo, wd_hi = refs[6 * u:6 * u + 6]
        rows = pl.ds(u * tm, tm)

        @pl.when(live_ref[step * EXPERT_TILES_PER_STEP + u] == 1)
        def _():
            h = xs_ref[rows, :d].astype(BF16)

            def expert(wg, wu, wd, weight):
                gate = _dot(h, wg[...])
                hidden = ((gate * jax.nn.sigmoid(gate)) * _dot(h, wu[...])) * weight
                return _dot(hidden.astype(BF16), wd[...])

            ys_ref[rows, :] = (expert(wg_lo, wu_lo, wd_lo, xs_ref[rows, d:d + 1])
                               + expert(wg_hi, wu_hi, wd_hi, xs_ref[rows, d + 1:d + 2]))

        @pl.when(live_ref[step * EXPERT_TILES_PER_STEP + u] == 0)
        def _():
            ys_ref[rows, :] = jnp.zeros((tm, d), F32)


def _experts(xs, e_lo, e_hi, live, wg, wu, wd, layer):
    n_rows, w = xs.shape
    _, ne, d, f = wg.shape
    per = EXPERT_TILES_PER_STEP
    tm = MOE_ROW_TILE
    assert n_rows % (per * tm) == 0

    def wspec(shape, use_hi, u):
        def index(step, elo, ehi, lv):
            return layer, (ehi if use_hi else elo)[step * per + u], 0, 0
        return pl.BlockSpec((None, None) + shape, index)

    wspecs, weights = [], []
    for u in range(per):
        wspecs += [wspec((d, f), 0, u), wspec((d, f), 1, u), wspec((d, f), 0, u), wspec((d, f), 1, u),
                   wspec((f, d), 0, u), wspec((f, d), 1, u)]
        weights += [wg, wg, wu, wu, wd, wd]
    grid_spec = pltpu.PrefetchScalarGridSpec(
        num_scalar_prefetch=3,
        grid=(n_rows // (per * tm),),
        in_specs=[pl.BlockSpec((per * tm, w), lambda step, elo, ehi, lv: (step, 0))] + wspecs,
        out_specs=pl.BlockSpec((per * tm, d), lambda step, elo, ehi, lv: (step, 0)),
    )
    return pl.pallas_call(
        _experts_kernel,
        grid_spec=grid_spec,
        out_shape=jax.ShapeDtypeStruct((n_rows, d), F32),
        compiler_params=pltpu.CompilerParams(
            dimension_semantics=("arbitrary",), vmem_limit_bytes=VMEM_LIMIT),
        name="experts",
    )(e_lo, e_hi, live, xs, *weights)


def _combine_kernel(pos_ref, next_ref, ys_ref, x1_ref, mod_ref, o_ref, buf, sem):
    tm = buf.shape[1]
    step = pl.program_id(0) * pl.num_programs(1) + pl.program_id(1)
    last = pl.num_programs(0) * pl.num_programs(1) - 1

    def issue(p_ref, slot):
        for r in range(tm):
            pltpu.make_async_copy(ys_ref.at[pl.ds(p_ref[slot, r], 1), :], buf.at[slot, pl.ds(r, 1), :],
                                  sem.at[slot]).start(priority=r % 2)

    def finish(slot):
        pltpu.make_async_copy(ys_ref.at[pl.ds(0, tm), :], buf.at[slot], sem.at[slot]).wait()
        rows = pl.ds(slot * tm, tm)
        o_ref[rows, :] = x1_ref[rows, :] + mod_ref[5:6, :] * buf[slot]

    @pl.when(step == 0)
    def _():
        issue(pos_ref, 0)

    issue(pos_ref, 1)
    finish(0)

    @pl.when(step < last)
    def _():
        issue(next_ref, 0)

    finish(1)


def _combine(ys, pos, x1, mod_l):
    b, s, d = x1.shape
    tm = MOE_DMA_TILE
    assert s % (2 * tm) == 0
    nt = s // (2 * tm)
    n_steps = b * nt
    pos2 = pos.reshape(n_steps, 2, tm)
    this_step = pl.BlockSpec((None, 2, tm), lambda bb, i: (bb * nt + i, 0, 0), memory_space=pltpu.SMEM)
    next_step = pl.BlockSpec((None, 2, tm), lambda bb, i: (jnp.minimum(bb * nt + i + 1, n_steps - 1), 0, 0),
                             memory_space=pltpu.SMEM)
    return pl.pallas_call(
        _combine_kernel,
        grid=(b, nt),
        in_specs=[this_step, next_step,
                  pl.BlockSpec(memory_space=pl.ANY),
                  pl.BlockSpec((None, 2 * tm, d), lambda bb, i: (bb, i, 0)),
                  pl.BlockSpec((None, N_MOD, d), lambda bb, i: (bb, 0, 0))],
        out_specs=pl.BlockSpec((None, 2 * tm, d), lambda bb, i: (bb, i, 0)),
        out_shape=jax.ShapeDtypeStruct((b, s, d), F32),
        scratch_shapes=[pltpu.VMEM((2, tm, d), F32), pltpu.SemaphoreType.DMA((2,))],
        compiler_params=pltpu.CompilerParams(
            dimension_semantics=("arbitrary", "arbitrary"), vmem_limit_bytes=VMEM_LIMIT),
        name="combine",
    )(pos2, pos2, ys, x1, mod_l)


def _moe(h2x, info, counts, wg, wu, wd, layer, x1, mod_l):
    b, s, d = x1.shape
    t = b * s
    tm = MOE_ROW_TILE
    n_tiles = t // tm + N_CLASSES
    n_tiles += -n_tiles % EXPERT_TILES_PER_STEP
    counts = counts[:N_CLASSES, 0]
    padded = (counts + tm - 1) // tm * tm
    ends = jnp.cumsum(padded)
    offs = jnp.pad(ends - padded, (0, N_CLASS_PAD - N_CLASSES)).astype(I32)
    tile_cls = jnp.sum(jnp.arange(n_tiles, dtype=I32)[:, None] * tm >= ends[None, :], axis=1)
    live = (tile_cls < N_CLASSES).astype(I32)
    tile_cls = jnp.minimum(tile_cls, N_CLASSES - 1)
    group, pair = tile_cls // N_PAIRS, tile_cls % N_PAIRS
    pairs = jnp.asarray(_PAIRS, I32)
    e_lo = group * EXPERTS_PER_GROUP + pairs[pair, 0]
    e_hi = group * EXPERTS_PER_GROUP + pairs[pair, 1]

    pos = _positions(info, offs).reshape(t // MOE_DMA_TILE, 1, MOE_DMA_TILE)
    xs = _dispatch(h2x.reshape(t, d + LANES), pos, ends.astype(I32), n_tiles * tm)
    ys = _experts(xs, e_lo, e_hi, live, wg, wu, wd, layer)
    return _combine(ys, pos, x1, mod_l)


def kernel(x, c, w_ada, b_ada, norm1_g, w_in, q_norm_g, k_norm_g, rel_bias, w_o_a, w_o_b, w_out,
           norm2_g, router_w, router_b, w_gate, w_up, w_down):
    b, s, d = x.shape
    depth = w_ada.shape[0]
    assert d % LANES == 0
    tm = min(TOKEN_TILE, s)

    mod = _adaln(c, w_ada, b_ada)
    bias = _bias_tiles(rel_bias, DSA_QUERY_BLOCK, DSA_KEY_BLOCK)
    rw = jnp.pad(router_w, ((0, 0), (0, LANES - N_EXPERTS)))
    rwh = rw.astype(BF16)
    rw2 = jnp.concatenate([rwh, (rw - rwh.astype(F32)).astype(BF16)], axis=1)
    rb = jnp.pad(router_b, (0, LANES - N_EXPERTS)).reshape(1, LANES)
    wg, wu, wd = w_gate.astype(BF16), w_up.astype(BF16), w_down.astype(BF16)
    woa, wob, wout = w_o_a.astype(BF16), w_o_b.astype(BF16), w_out.astype(BF16)
    w_packed = _pack_w_in(w_in)

    for l in range(depth):
        qa, ka, va, qbt, qit, kb, vt, ki, wit, sga, sgb = _in_proj(
            x, mod[l], norm1_g[l].reshape(1, d), w_packed, l,
            q_norm_g[l].reshape(HEAD_DIM, 1), k_norm_g[l].reshape(1, HEAD_DIM), tm)
        oa = _sb_attention(qa, ka, va)
        ob = _dsa_attention(qit, ki, wit, qbt, kb, vt, bias)
        x1, h2x, info, counts = _merge(oa, ob, sga, sgb, x, mod[l], woa, wob, wout, l,
                                       norm2_g[l].reshape(1, d), rw2, rb, tm)
        x = _moe(h2x, info, counts, wg, wu, wd, l, x1, mod[l])
    return x
```

```python
import functools
import math

import numpy as np
import jax
import jax.numpy as jnp
from jax import lax
from jax.experimental import pallas as pl
from jax.experimental.pallas import tpu as pltpu

F32 = jnp.float32
BF16 = jnp.bfloat16
I32 = jnp.int32
I16 = jnp.int16

HEAD_DIM = 64
N_HEADS = 8
N_EXPERTS = 16
N_GROUPS = 4
EXPERTS_PER_GROUP = 4
_PAIRS = ((0, 1), (0, 2), (0, 3), (1, 2), (1, 3), (2, 3))
N_PAIRS = len(_PAIRS)
N_CLASSES = N_GROUPS * N_PAIRS
N_CLASS_PAD = 32
MOE_ROW_TILE = 256
EXPERT_TILES_PER_STEP = 2
MOE_DMA_TILE = 512
DISPATCH_TILE = 1024
N_MOD = 6
EPS = 1e-6
INDEX_TOPK_MAX = 256
REL_BUCKETS = 32
REL_MAX_DIST = 128
LANES = 128
SUBLANES = 8
PACKED_SUBLANES = 16
HALF_BITS = 16
TOKEN_TILE = 512
EXP2_MAX = 126.0
DSA_QUERY_BLOCK = 256
DSA_KEY_BLOCK = 128
SB_QUERY_BLOCK = 256
SB_KEY_BLOCK = 256
SB_DEAD_LOG2 = 150.0
LOG2E = 1.4426950408889634
NEG_BIG = -1e30
INT_MIN = -2 ** 31
INT_MAX = 2 ** 31 - 1
HALF_BIAS = 2 ** 15

VMEM_LIMIT = 56 * 1024 * 1024


def _nt_dot(a, b):
    return lax.dot_general(a, b, (((1,), (1,)), ((), ())), preferred_element_type=F32)


def _dot(a, b):
    return jnp.dot(a, b, preferred_element_type=F32)


def _mod_kernel(c_ref, w_ref, b_ref, o_ref):
    c = c_ref[...]
    sc = c * jax.nn.sigmoid(c)
    o_ref[...] = _dot(sc.astype(BF16), w_ref[...].astype(BF16)) + b_ref[...]


def _adaln(c, w_ada, b_ada):
    depth, d, nd = w_ada.shape
    b = c.shape[0]
    out = pl.pallas_call(
        _mod_kernel,
        grid=(depth, nd // (2 * d)),
        in_specs=[pl.BlockSpec((b, d), lambda l, j: (0, 0)),
                  pl.BlockSpec((None, d, 2 * d), lambda l, j: (l, 0, j)),
                  pl.BlockSpec((None, 1, 2 * d), lambda l, j: (l, 0, j))],
        out_specs=pl.BlockSpec((None, b, 2 * d), lambda l, j: (l, 0, j)),
        out_shape=jax.ShapeDtypeStruct((depth, b, nd), F32),
        compiler_params=pltpu.CompilerParams(vmem_limit_bytes=VMEM_LIMIT),
        name="adaln",
    )(c, w_ada, b_ada.reshape(depth, 1, nd))
    return out.reshape(depth, b, N_MOD, d)


def _rms(x):
    return x * lax.rsqrt(jnp.mean(x * x, axis=-1, keepdims=True) + EPS)


def _in_kernel(x_ref, mod_ref, g1_ref, w_ref, qng_ref, kng_ref,
               qa_ref, ka_ref, va_ref, qbt_ref, qit_ref, kb_ref, vt_ref, ki_ref, wit_ref,
               sga_ref, sgb_ref):
    d = x_ref.shape[-1]
    tm = x_ref.shape[0]
    hw = N_HEADS * HEAD_DIM
    x = x_ref[...]
    h = (_rms(x) * g1_ref[...]) * (1.0 + mod_ref[1:2, :]) + mod_ref[0:1, :]
    hb = h.astype(BF16)

    def proj(lo, n):
        return _dot(hb, w_ref[:, lo:lo + n])

    def heads(r, ref):
        for pp in range(N_HEADS // 2):
            ref[pp] = r[:, pp * LANES:(pp + 1) * LANES].astype(ref.dtype)

    scale = HEAD_DIM ** -0.5
    heads(proj(0, hw) * (scale * LOG2E), qa_ref)
    heads(proj(hw, hw), ka_ref)
    heads(proj(2 * hw, hw), va_ref)

    rt = proj(3 * hw, hw).T
    for hh in range(N_HEADS):
        slab = rt[hh * HEAD_DIM:(hh + 1) * HEAD_DIM, :]
        inv = lax.rsqrt(jnp.mean(slab * slab, axis=0, keepdims=True) + EPS)
        qbt_ref[hh] = (((slab * inv) * qng_ref[...]) * scale).astype(BF16)

    r = proj(4 * hw, 2 * HEAD_DIM)
    kb_ref[...] = (_rms(r[:, :HEAD_DIM]) * kng_ref[...]).astype(BF16)
    vt = r.T[HEAD_DIM:, :]
    ones_row = (lax.broadcasted_iota(I32, (HEAD_DIM, LANES), 0) == 0).astype(BF16)
    for cc in range(tm // LANES):
        vt_ref[cc, :HEAD_DIM, :] = vt[:, cc * LANES:(cc + 1) * LANES].astype(BF16)
        vt_ref[cc, HEAD_DIM:, :] = ones_row

    rt = proj(4 * hw + LANES, hw).T
    for hh in range(N_HEADS):
        qit_ref[hh] = rt[hh * HEAD_DIM:(hh + 1) * HEAD_DIM, :].astype(BF16)

    r = proj(5 * hw + LANES, LANES)
    ki_ref[...] = r[:, :HEAD_DIM].astype(BF16)
    wit_ref[...] = r.T[HEAD_DIM:HEAD_DIM + N_HEADS, :]

    off = 5 * hw + 2 * LANES
    sga_ref[...] = jax.nn.sigmoid(proj(off, d)).astype(BF16)
    sgb_ref[...] = jax.nn.sigmoid(proj(off + d, d)).astype(BF16)


def _pack_w_in(w_in):
    hw = N_HEADS * HEAD_DIM
    a = 4 * hw + 2 * HEAD_DIM + hw
    small = w_in[..., a:a + HEAD_DIM + N_HEADS]
    small = jnp.pad(small, ((0, 0), (0, 0), (0, LANES - small.shape[-1])))
    gates = w_in[..., a + HEAD_DIM + N_HEADS:]
    return jnp.concatenate([w_in[..., :a], small, gates], axis=-1).astype(BF16)


def _in_proj(x, mod, g1, w_packed, layer, qng, kng, tm):
    b, s, d = x.shape
    hm = jax.ShapeDtypeStruct((b, N_HEADS // 2, s, LANES), BF16)
    hmt = jax.ShapeDtypeStruct((b, N_HEADS, HEAD_DIM, s), BF16)
    tok64 = jax.ShapeDtypeStruct((b, s, HEAD_DIM), BF16)
    hm_spec = pl.BlockSpec((None, N_HEADS // 2, tm, LANES), lambda bb, i: (bb, 0, i, 0))
    hmt_spec = pl.BlockSpec((None, N_HEADS, HEAD_DIM, tm), lambda bb, i: (bb, 0, 0, i))
    tok = lambda n: pl.BlockSpec((None, tm, n), lambda bb, i: (bb, i, 0))
    nw = w_packed.shape[-1]
    return pl.pallas_call(
        _in_kernel,
        grid=(b, s // tm),
        in_specs=[tok(d),
                  pl.BlockSpec((None, None, N_MOD, d), lambda bb, i: (layer, bb, 0, 0)),
                  pl.BlockSpec((None, 1, d), lambda bb, i: (layer, 0, 0)),
                  pl.BlockSpec((None, d, nw), lambda bb, i: (layer, 0, 0)),
                  pl.BlockSpec((None, HEAD_DIM, 1), lambda bb, i: (layer, 0, 0)),
                  pl.BlockSpec((None, 1, HEAD_DIM), lambda bb, i: (layer, 0, 0))],
        out_specs=[hm_spec] * 3 + [hmt_spec] * 2
                  + [tok(HEAD_DIM),
                     pl.BlockSpec((None, tm // LANES, 2 * HEAD_DIM, LANES), lambda bb, i: (bb, i, 0, 0)),
                     tok(HEAD_DIM),
                     pl.BlockSpec((None, N_HEADS, tm), lambda bb, i: (bb, 0, i)),
                     tok(d), tok(d)],
        out_shape=[hm] * 3 + [hmt] * 2
                  + [tok64,
                     jax.ShapeDtypeStruct((b, s // LANES, 2 * HEAD_DIM, LANES), BF16),
                     tok64,
                     jax.ShapeDtypeStruct((b, N_HEADS, s), F32),
                     jax.ShapeDtypeStruct((b, s, d), BF16),
                     jax.ShapeDtypeStruct((b, s, d), BF16)],
        compiler_params=pltpu.CompilerParams(
            dimension_semantics=("parallel", "parallel"), vmem_limit_bytes=VMEM_LIMIT),
        name="in_proj",
    )(x, mod, g1, w_packed, qng, kng)


def _sb_kernel(q_ref, k_ref, v_ref, o_ref, acc_scr, csum_scr, sp_scr, ls_scr, qm_scr):
    tq = q_ref.shape[1]
    tk = SB_KEY_BLOCK
    i = pl.program_id(1)
    nfull = (i * tq) // tk
    row = lax.broadcasted_iota(I32, (tq, tk), 0)
    col = lax.broadcasted_iota(I32, (tq, tk), 1)
    before = (col - row) < (i * tq - nfull * tk)
    krow = lax.broadcasted_iota(I32, (tk, tk), 0)
    kcol = lax.broadcasted_iota(I32, (tk, tk), 1)
    suffix = (krow > kcol).astype(BF16)
    low_half = lax.broadcasted_iota(I32, (tq, LANES), 1) < HEAD_DIM
    for hh in range(N_HEADS):
        slab = q_ref[hh // 2]
        qm_scr[hh] = jnp.where(low_half == (hh % 2 == 0), slab, jnp.zeros_like(slab))

    def stage_a(hh, st, first):
        z2 = _nt_dot(qm_scr[hh], k_ref[hh // 2, pl.ds(st, tk), :])
        sp2 = jnp.maximum(jnp.log2(1.0 + jnp.exp2(jnp.minimum(z2, EXP2_MAX))), z2)
        if first:
            sp2 = jnp.where(before, sp2, 0.0)
        sp_scr[hh] = sp2.astype(BF16)
        ls_scr[hh] = z2 - sp2

    def stage_b(hh, st, first):
        sp = sp_scr[hh]
        suf = _dot(sp, suffix)
        if first:
            a = jnp.where(before, jnp.exp2(ls_scr[hh] - suf), 0.0)
            csum = jnp.zeros((tq, 1), F32)
        else:
            csum = csum_scr[hh]
            a = jnp.exp2((ls_scr[hh] - suf) - csum)
        c = _dot(a.astype(BF16), v_ref[hh // 2, pl.ds(st, tk), :])
        if first:
            acc_scr[hh] = c
        else:
            acc_scr[hh] += c
        csum_scr[hh] = csum + (suf[:, 0:1] + sp[:, 0:1].astype(F32))

    def key_block(st, first):
        for hh in range(N_HEADS):
            stage_a(hh, st, first)
        for hh in range(N_HEADS):
            stage_b(hh, st, first)

    key_block(pl.multiple_of(nfull * tk, tk), True)

    def live():
        return jnp.min(csum_scr[...]) < SB_DEAD_LOG2

    def body(carry):
        jj, _ = carry
        key_block(pl.multiple_of((nfull - 1 - jj) * tk, tk), False)
        return jj + 1, live()

    lax.while_loop(lambda carry: (carry[0] < nfull) & carry[1], body, (jnp.int32(0), live()))
    for pp in range(N_HEADS // 2):
        o_ref[:, pp * LANES:(pp + 1) * LANES] = jnp.where(
            low_half, acc_scr[2 * pp], acc_scr[2 * pp + 1]).astype(o_ref.dtype)


def _sb_attention(qa, ka, va):
    b, npair, s, _ = qa.shape
    nh, dh = 2 * npair, HEAD_DIM
    tq = SB_QUERY_BLOCK
    assert SB_KEY_BLOCK % tq == 0 and s % SB_KEY_BLOCK == 0
    return pl.pallas_call(
        _sb_kernel,
        grid=(b, s // tq),
        in_specs=[pl.BlockSpec((None, npair, tq, LANES), lambda bb, i: (bb, 0, i, 0)),
                  pl.BlockSpec((None, npair, s, LANES), lambda bb, i: (bb, 0, 0, 0)),
                  pl.BlockSpec((None, npair, s, LANES), lambda bb, i: (bb, 0, 0, 0))],
        out_specs=pl.BlockSpec((None, tq, nh * dh), lambda bb, i: (bb, i, 0)),
        out_shape=jax.ShapeDtypeStruct((b, s, nh * dh), BF16),
        scratch_shapes=[pltpu.VMEM((nh, tq, LANES), F32), pltpu.VMEM((nh, tq, 1), F32),
                        pltpu.VMEM((nh, tq, SB_KEY_BLOCK), BF16), pltpu.VMEM((nh, tq, SB_KEY_BLOCK), F32),
                        pltpu.VMEM((nh, tq, LANES), BF16)],
        compiler_params=pltpu.CompilerParams(
            dimension_semantics=("parallel", "parallel"), vmem_limit_bytes=VMEM_LIMIT),
        name="stick_breaking",
    )(qa, ka, va)


def _t5_bucket_np(dist):
    max_exact = REL_BUCKETS // 2
    d_f = np.maximum(dist, 1).astype(np.float32)
    large = max_exact + (np.log(d_f / np.float32(max_exact))
                         / np.float32(math.log(REL_MAX_DIST / max_exact))
                         * np.float32(REL_BUCKETS - max_exact)).astype(np.int32)
    large = np.minimum(large, REL_BUCKETS - 1)
    return np.where(dist < max_exact, dist, large).astype(np.int32)


def _bias_tiles(rel_bias, tq, tk):
    n_near = tq // tk + 1
    key = np.arange(tk)[:, None]
    query = np.arange(tq)[None, :]
    bk = np.stack([_t5_bucket_np(np.maximum(query - key - (n - 1) * tk, 0)) for n in range(n_near)])
    far = int(_t5_bucket_np(np.array([REL_MAX_DIST]))[0])
    nh = rel_bias.shape[1]

    def body(rb_ref, bk_ref, o_ref):
        buckets = bk_ref[...]
        for hh in range(nh):
            tile = jnp.zeros((tk, tq), F32)
            for bucket in range(REL_BUCKETS):
                tile = jnp.where(buckets == bucket, rb_ref[hh, bucket] - rb_ref[hh, far], tile)
            o_ref[hh] = tile

    return pl.pallas_call(
        body,
        grid=(n_near,),
        in_specs=[pl.BlockSpec(memory_space=pltpu.SMEM),
                  pl.BlockSpec((None, tk, tq), lambda n: (n, 0, 0))],
        out_specs=pl.BlockSpec((None, nh, tk, tq), lambda n: (n, 0, 0, 0)),
        out_shape=jax.ShapeDtypeStruct((n_near, nh, tk, tq), F32),
        name="bias_tiles",
    )(rel_bias.T, jnp.asarray(bk))


def _sort_key(x):
    bits = pltpu.bitcast(x, I32)
    return bits ^ ((bits >> 31) & INT_MAX)


def _fold_rows(x, op):
    return functools.reduce(
        op, [x[g * SUBLANES:(g + 1) * SUBLANES, :] for g in range(x.shape[0] // SUBLANES)])


def _dsa_kernel(ktop, qit_ref, ki_ref, wit_ref, qbt_ref, kb_ref, vt_ref, bias_ref, o_ref,
                key_scr, hi_scr, lo_scr, m_scr, acc_scr, s_scr):
    tq = qit_ref.shape[2]
    tk = DSA_KEY_BLOCK
    per_q = tq // tk
    i = pl.program_id(1)
    first_own = per_q * i
    nchunks = per_q * (i + 1)
    key_i = lax.broadcasted_iota(I32, (tk, tq), 0)
    qry_i = lax.broadcasted_iota(I32, (tk, tq), 1)

    def causal(rel):
        return (key_i - qry_i) <= (-rel * tk)

    def for_chunks(fn):
        def far(j, carry):
            for u in range(per_q):
                fn(j * per_q + u, None)
            return carry
        lax.fori_loop(0, i - 1, far, 0)

        @pl.when(i >= 1)
        def _():
            for rel in range(-per_q, -1):
                fn(first_own + rel, None)
            fn(first_own - 1, -1)

        for rel in range(per_q):
            fn(first_own + rel, rel)

    def score_chunk(c, rel):
        kic = ki_ref[pl.ds(pl.multiple_of(c * tk, tk), tk), :]
        score = jnp.zeros((tk, tq), F32)
        for hh in range(N_HEADS):
            score = score + wit_ref[hh:hh + 1, :] * jnp.maximum(_dot(kic, qit_ref[hh]), 0.0)
        keys = _sort_key(score)
        if rel is not None and rel >= 0:
            keys = jnp.where(causal(rel), keys, INT_MIN)
        key_scr[c] = keys
        hi_scr[c] = (keys >> HALF_BITS).astype(I16)
        lo_scr[c] = ((keys & (2 * HALF_BIAS - 1)) - HALF_BIAS).astype(I16)

    for_chunks(score_chunk)

    def count(pred):
        def body(j, part):
            for u in range(per_q):
                part = part + _fold_rows(jnp.where(pred(key_scr[j * per_q + u]), 1, 0), jnp.add)
            return part
        part = lax.fori_loop(0, i + 1, body, jnp.zeros((SUBLANES, tq), I32))
        return jnp.sum(part, axis=0, keepdims=True)

    def count16(scr, pred):
        one, zero = jnp.int16(1), jnp.int16(0)

        def block(j, part):
            for u in range(per_q):
                hit = jnp.where(pred(scr[j * per_q + u]), one, zero)
                part = part + functools.reduce(
                    jnp.add, [hit[g * PACKED_SUBLANES:(g + 1) * PACKED_SUBLANES, :]
                              for g in range(tk // PACKED_SUBLANES)])
            return part

        nblk = i + 1
        part = lax.fori_loop(0, nblk // 2, lambda jj, part: block(2 * jj + 1, block(2 * jj, part)),
                             jnp.zeros((PACKED_SUBLANES, tq), I16))
        part = lax.cond(nblk % 2 == 1, lambda p: block(nblk - 1, p), lambda p: p, part)
        return jnp.sum(part.astype(I32), axis=0, keepdims=True)

    def search16(scr, base):
        def bit_body(bi, t_u):
            cand_u = t_u | jnp.left_shift(jnp.int32(1), HALF_BITS - 1 - bi)
            cand = (cand_u - HALF_BIAS).astype(I16)
            return jnp.where(base + count16(scr, lambda k: k >= cand) >= ktop, cand_u, t_u)
        return lax.fori_loop(0, HALF_BITS, bit_body, jnp.zeros((1, tq), I32))

    def find_threshold():
        t_hi = search16(hi_scr, 0) - HALF_BIAS
        t_hi16 = t_hi.astype(I16)
        above = count16(hi_scr, lambda k: k > t_hi16)

        def keep_band(j, carry):
            for u in range(per_q):
                c = j * per_q + u
                lo_scr[c] = jnp.where(hi_scr[c] == t_hi16, lo_scr[c], jnp.int16(-HALF_BIAS))
            return carry

        lax.fori_loop(0, i + 1, keep_band, 0)
        return t_hi * (2 * HALF_BIAS) + search16(lo_scr, above)

    thr = lax.cond((i + 1) * tq <= ktop, lambda: jnp.full((1, tq), INT_MIN, I32), find_threshold)
    n_ge = count(lambda k: k >= thr)

    @pl.when(jnp.max(n_ge) > ktop)
    def _():
        need = (ktop - count(lambda k: k > thr)).astype(F32)
        lower = (lax.broadcasted_iota(I32, (tk, tk), 1)
                 < lax.broadcasted_iota(I32, (tk, tk), 0)).astype(BF16)

        def body(c, seen):
            k = key_scr[c]
            eq = k == thr
            eqf = eq.astype(BF16)
            rank = seen + _dot(lower, eqf)
            drop = eq & (rank >= need) & (n_ge > ktop)
            key_scr[c] = jnp.where(drop, thr - 1, k)
            return seen + jnp.sum(eqf.astype(F32), axis=0, keepdims=True)

        lax.fori_loop(0, nchunks, body, jnp.zeros((1, tq), F32))

    def logits(c, rel):
        sel = key_scr[c] >= thr
        if rel is not None and rel >= 0:
            sel = sel & causal(rel)
        kbc = kb_ref[pl.ds(pl.multiple_of(c * tk, tk), tk), :]

        def head(hh):
            s = _dot(kbc, qbt_ref[hh])
            if rel is not None:
                s = s + bias_ref[rel + 1, hh]
            return s
        return sel, head

    m_scr[...] = jnp.full(m_scr.shape, NEG_BIG, F32)

    def max_chunk(c, rel):
        sel, head = logits(c, rel)
        for hh in range(N_HEADS):
            s = jnp.where(sel, head(hh), NEG_BIG)
            s_scr[hh, c] = s
            m_scr[hh] = jnp.maximum(m_scr[hh], _fold_rows(s, jnp.maximum))

    for_chunks(max_chunk)
    m = [jnp.max(m_scr[hh], axis=0, keepdims=True) for hh in range(N_HEADS)]

    acc_scr[...] = jnp.zeros(acc_scr.shape, F32)

    def att_block(j, carry):
        for u in range(per_q):
            c = j * per_q + u
            vtc = vt_ref[c]
            for hh in range(N_HEADS):
                p = jnp.exp(s_scr[hh, c] - m[hh])
                acc_scr[hh] += _dot(vtc, p.astype(BF16))
        return carry

    lax.fori_loop(0, i + 1, att_block, 0)

    out_t = jnp.concatenate(
        [acc_scr[hh, :HEAD_DIM, :] / acc_scr[hh, HEAD_DIM:HEAD_DIM + 1, :] for hh in range(N_HEADS)],
        axis=0)
    o_ref[...] = out_t.T.astype(o_ref.dtype)


def _dsa_attention(qit, ki, wit, qbt, kb, vt, bias):
    b, nh, dh, s = qit.shape
    tq = DSA_QUERY_BLOCK
    tk = DSA_KEY_BLOCK
    assert tk == REL_MAX_DIST and tk == LANES and tq % tk == 0 and s % tq == 0
    ktop = min(INDEX_TOPK_MAX, s // 4)
    hmt_q = pl.BlockSpec((None, nh, dh, tq), lambda bb, i: (bb, 0, 0, i))
    full64 = pl.BlockSpec((None, s, dh), lambda bb, i: (bb, 0, 0))
    return pl.pallas_call(
        functools.partial(_dsa_kernel, ktop),
        grid=(b, s // tq),
        in_specs=[hmt_q, full64,
                  pl.BlockSpec((None, nh, tq), lambda bb, i: (bb, 0, i)),
                  hmt_q, full64,
                  pl.BlockSpec((None, s // tk, 2 * dh, tk), lambda bb, i: (bb, 0, 0, 0)),
                  pl.BlockSpec(bias.shape, lambda bb, i: (0, 0, 0, 0))],
        out_specs=pl.BlockSpec((None, tq, nh * dh), lambda bb, i: (bb, i, 0)),
        out_shape=jax.ShapeDtypeStruct((b, s, nh * dh), BF16),
        scratch_shapes=[pltpu.VMEM((s // tk, tk, tq), I32),
                        pltpu.VMEM((s // tk, tk, tq), I16),
                        pltpu.VMEM((s // tk, tk, tq), I16),
                        pltpu.VMEM((nh, SUBLANES, tq), F32),
                        pltpu.VMEM((nh, 2 * dh, tq), F32),
                        pltpu.VMEM((nh, s // tk, tk, tq), F32)],
        compiler_params=pltpu.CompilerParams(
            dimension_semantics=("parallel", "parallel"), vmem_limit_bytes=VMEM_LIMIT),
        name="sparse_attention",
    )(qit, ki, wit, qbt, kb, vt, bias)


def _first_max4(a):
    m1 = jnp.maximum(jnp.maximum(a[0], a[1]), jnp.maximum(a[2], a[3]))
    i1 = jnp.where(a[0] == m1, 0, jnp.where(a[1] == m1, 1, jnp.where(a[2] == m1, 2, 3)))
    rest = [jnp.where(i1 == j, -1.0, a[j]) for j in range(4)]
    m2 = jnp.maximum(jnp.maximum(rest[0], rest[1]), jnp.maximum(rest[2], rest[3]))
    i2 = jnp.where(rest[0] == m2, 0, jnp.where(rest[1] == m2, 1, jnp.where(rest[2] == m2, 2, 3)))
    return m1, i1, m2, i2


def _route(lt):
    lg = [lt[e:e + 1, :] for e in range(N_EXPERTS)]
    mx = functools.reduce(jnp.maximum, lg)
    ex = [jnp.exp(v - mx) for v in lg]
    tot = functools.reduce(lambda u, v: u + v, ex)
    p = [v / tot for v in ex]
    gs = []
    for g in range(N_GROUPS):
        m1, _, m2, _ = _first_max4(p[4 * g:4 * g + 4])
        gs.append(m1 + m2)
    _, gbest, _, _ = _first_max4(gs)
    chosen = [jnp.where(gbest == 0, p[j], jnp.where(gbest == 1, p[4 + j],
              jnp.where(gbest == 2, p[8 + j], p[12 + j]))) for j in range(4)]
    m1, i1, m2, i2 = _first_max4(chosen)
    den = m1 + m2
    lo = jnp.minimum(i1, i2)
    hi = jnp.maximum(i1, i2)
    pair = jnp.where(lo == 0, hi - 1, jnp.where(lo == 1, hi + 1, N_PAIRS - 1))
    first_is_lo = i1 < i2
    w1 = m1 / den
    w2 = m2 / den
    return gbest * N_PAIRS + pair, jnp.where(first_is_lo, w1, w2), jnp.where(first_is_lo, w2, w1)


def _merge_kernel(oa_ref, ob_ref, sga_ref, sgb_ref, x_ref, mod_ref, woa_ref, wob_ref, wout_ref,
                  g2_ref, rw2_ref, rb_ref, x1_ref, h2x_ref, info_ref, cnt_ref, carry_scr):
    tm, d = x_ref.shape

    @pl.when((pl.program_id(0) == 0) & (pl.program_id(1) == 0))
    def _():
        carry_scr[...] = jnp.zeros(carry_scr.shape, F32)

    merged = (sga_ref[...].astype(F32) * _dot(oa_ref[...], woa_ref[...])
              + sgb_ref[...].astype(F32) * _dot(ob_ref[...], wob_ref[...]))
    x1 = x_ref[...] + mod_ref[2:3, :] * _dot(merged.astype(BF16), wout_ref[...])
    x1_ref[...] = x1
    h2 = (_rms(x1) * g2_ref[...]) * (1.0 + mod_ref[4:5, :]) + mod_ref[3:4, :]
    h2x_ref[:, :d] = h2
    hi = h2.astype(BF16)
    lo = (h2 - hi.astype(F32)).astype(BF16)
    both = _dot(hi, rw2_ref[...])
    logits = (both[:, :LANES] + _dot(lo, rw2_ref[:, :LANES]) + both[:, LANES:]) + rb_ref[...]
    cls, cw_lo, cw_hi = _route(logits.T)

    ncls = carry_scr.shape[0]
    onehot = (lax.broadcasted_iota(I32, (ncls, tm), 0) == cls).astype(F32)
    earlier = (lax.broadcasted_iota(I32, (tm, tm), 0)
               < lax.broadcasted_iota(I32, (tm, tm), 1)).astype(BF16)
    before = _dot(onehot.astype(BF16), earlier)
    carry = carry_scr[...]
    carry_t = jnp.concatenate([carry] * (tm // LANES), axis=1)
    rank = jnp.sum(onehot * (carry_t + before), axis=0, keepdims=True).astype(I32)
    carry = carry + jnp.sum(onehot, axis=1, keepdims=True)
    carry_scr[...] = carry
    cnt_ref[...] = carry.astype(I32)

    sub = lax.broadcasted_iota(I32, (SUBLANES, tm), 0)
    info_ref[...] = jnp.where(sub == 0, cls, jnp.where(sub == 1, rank, 0))
    sub = lax.broadcasted_iota(I32, (LANES, tm), 0)
    weights_t = jnp.where(sub == 0, cw_lo, jnp.where(sub == 1, cw_hi, 0.0))
    h2x_ref[:, d:] = weights_t.T


def _merge(oa, ob, sga, sgb, x, mod, woa, wob, wout, layer, g2, rw2, rb, tm):
    b, s, d = x.shape
    hw = oa.shape[-1]
    nt = s // tm
    tok = lambda n: pl.BlockSpec((None, tm, n), lambda bb, i: (bb, i, 0))
    const = lambda r, c: pl.BlockSpec((r, c), lambda bb, i: (0, 0))
    layered = lambda r, c: pl.BlockSpec((None, r, c), lambda bb, i: (layer, 0, 0))
    return pl.pallas_call(
        _merge_kernel,
        grid=(b, nt),
        in_specs=[tok(hw), tok(hw), tok(d), tok(d), tok(d),
                  pl.BlockSpec((None, None, N_MOD, d), lambda bb, i: (layer, bb, 0, 0)),
                  layered(hw, d), layered(hw, d), layered(d, d), layered(1, d),
                  const(d, 2 * LANES), const(1, LANES)],
        out_specs=[tok(d), tok(d + LANES),
                   pl.BlockSpec((SUBLANES, tm), lambda bb, i: (0, bb * nt + i)),
                   const(N_CLASS_PAD, LANES)],
        out_shape=[jax.ShapeDtypeStruct((b, s, d), F32),
                   jax.ShapeDtypeStruct((b, s, d + LANES), F32),
                   jax.ShapeDtypeStruct((SUBLANES, b * s), I32),
                   jax.ShapeDtypeStruct((N_CLASS_PAD, LANES), I32)],
        scratch_shapes=[pltpu.VMEM((N_CLASS_PAD, LANES), F32)],
        compiler_params=pltpu.CompilerParams(
            dimension_semantics=("arbitrary", "arbitrary"), vmem_limit_bytes=VMEM_LIMIT),
        name="merge_route",
    )(oa, ob, sga, sgb, x, mod, woa, wob, wout, g2, rw2, rb)


def _positions_kernel(offs_ref, info_ref, pos_ref):
    cls = info_ref[0:1, :]
    pos = info_ref[1:2, :]
    for c in range(N_CLASSES):
        pos = pos + jnp.where(cls == c, offs_ref[c], 0)
    pos_ref[...] = jnp.broadcast_to(pos, pos_ref.shape)


def _positions(info, offs):
    return pl.pallas_call(
        _positions_kernel,
        in_specs=[pl.BlockSpec(memory_space=pltpu.SMEM), pl.BlockSpec(info.shape, lambda: (0, 0))],
        out_specs=pl.BlockSpec(info.shape, lambda: (0, 0)),
        out_shape=jax.ShapeDtypeStruct(info.shape, I32),
        name="positions",
    )(offs, info)[0]


def _dispatch_kernel(ends_ref, pos_ref, src_ref, dst_ref, zero_scr, sem, zero_sem):
    tm = pos_ref.shape[1]
    row_tile = zero_scr.shape[0]

    @pl.when(pl.program_id(0) == 0)
    def _():
        zero_scr[...] = jnp.zeros(zero_scr.shape, F32)

        def zero_tail(c):
            start = pl.multiple_of(ends_ref[c] - row_tile, row_tile)
            return pltpu.make_async_copy(zero_scr, dst_ref.at[pl.ds(start, row_tile), :], zero_sem)

        def nonempty(c):
            return ends_ref[c] > (ends_ref[c - 1] if c else 0)

        for c in range(N_CLASSES):
            @pl.when(nonempty(c))
            def _(c=c):
                zero_tail(c).start()
        for c in range(N_CLASSES):
            @pl.when(nonempty(c))
            def _(c=c):
                zero_tail(c).wait()

        def unused_tile(j):
            start = pl.multiple_of(ends_ref[N_CLASSES - 1] + j * row_tile, row_tile)
            return pltpu.make_async_copy(zero_scr, dst_ref.at[pl.ds(start, row_tile), :], zero_sem)

        n_unused = (dst_ref.shape[0] - ends_ref[N_CLASSES - 1]) // row_tile
        lax.fori_loop(0, n_unused, lambda j, carry: (unused_tile(j).start(), carry)[1], 0)
        lax.fori_loop(0, n_unused, lambda j, carry: (unused_tile(j).wait(), carry)[1], 0)

    for r in range(tm):
        pltpu.make_async_copy(src_ref.at[pl.ds(r, 1), :], dst_ref.at[pl.ds(pos_ref[0, r], 1), :],
                              sem).start(priority=r % 2)
    pltpu.make_async_copy(src_ref, dst_ref.at[pl.ds(0, tm), :], sem).wait()


def _dispatch(h2x, pos, ends, n_rows):
    t, w = h2x.shape
    tm = DISPATCH_TILE
    pos = pos.reshape(t // tm, 1, tm)
    smem_row = pl.BlockSpec((None, 1, tm), lambda i: (i, 0, 0), memory_space=pltpu.SMEM)
    return pl.pallas_call(
        _dispatch_kernel,
        grid=(t // tm,),
        in_specs=[pl.BlockSpec(memory_space=pltpu.SMEM), smem_row, pl.BlockSpec((tm, w), lambda i: (i, 0))],
        out_specs=pl.BlockSpec(memory_space=pl.ANY),
        out_shape=jax.ShapeDtypeStruct((n_rows, w), F32),
        scratch_shapes=[pltpu.VMEM((MOE_ROW_TILE, w), F32), pltpu.SemaphoreType.DMA, pltpu.SemaphoreType.DMA],
        compiler_params=pltpu.CompilerParams(
            dimension_semantics=("arbitrary",), vmem_limit_bytes=VMEM_LIMIT),
        name="dispatch",
    )(ends, pos, h2x)


def _experts_kernel(elo_ref, ehi_ref, live_ref, xs_ref, *refs):
    del elo_ref, ehi_ref
    ys_ref = refs[-1]
    d = ys_ref.shape[1]
    tm = MOE_ROW_TILE
    step = pl.program_id(0)

    for u in range(EXPERT_TILES_PER_STEP):
        wg_lo, wg_hi, wu_lo, wu_hi, wd_lo, wd_hi = refs[6 * u:6 * u + 6]
        rows = pl.ds(u * tm, tm)

        @pl.when(live_ref[step * EXPERT_TILES_PER_STEP + u] == 1)
        def _():
            h = xs_ref[rows, :d].astype(BF16)

            def expert(wg, wu, wd, weight):
                gate = _dot(h, wg[...])
                hidden = ((gate * jax.nn.sigmoid(gate)) * _dot(h, wu[...])) * weight
                return _dot(hidden.astype(BF16), wd[...])

            ys_ref[rows, :] = (expert(wg_lo, wu_lo, wd_lo, xs_ref[rows, d:d + 1])
                               + expert(wg_hi, wu_hi, wd_hi, xs_ref[rows, d + 1:d + 2]))

        @pl.when(live_ref[step * EXPERT_TILES_PER_STEP + u] == 0)
        def _():
            ys_ref[rows, :] = jnp.zeros((tm, d), F32)


def _experts(xs, e_lo, e_hi, live, wg, wu, wd, layer):
    n_rows, w = xs.shape
    _, ne, d, f = wg.shape
    per = EXPERT_TILES_PER_STEP
    tm = MOE_ROW_TILE
    assert n_rows % (per * tm) == 0

    def wspec(shape, use_hi, u):
        def index(step, elo, ehi, lv):
            return layer, (ehi if use_hi else elo)[step * per + u], 0, 0
        return pl.BlockSpec((None, None) + shape, index)

    wspecs, weights = [], []
    for u in range(per):
        wspecs += [wspec((d, f), 0, u), wspec((d, f), 1, u), wspec((d, f), 0, u), wspec((d, f), 1, u),
                   wspec((f, d), 0, u), wspec((f, d), 1, u)]
        weights += [wg, wg, wu, wu, wd, wd]
    grid_spec = pltpu.PrefetchScalarGridSpec(
        num_scalar_prefetch=3,
        grid=(n_rows // (per * tm),),
        in_specs=[pl.BlockSpec((per * tm, w), lambda step, elo, ehi, lv: (step, 0))] + wspecs,
        out_specs=pl.BlockSpec((per * tm, d), lambda step, elo, ehi, lv: (step, 0)),
    )
    return pl.pallas_call(
        _experts_kernel,
        grid_spec=grid_spec,
        out_shape=jax.ShapeDtypeStruct((n_rows, d), F32),
        compiler_params=pltpu.CompilerParams(
            dimension_semantics=("arbitrary",), vmem_limit_bytes=VMEM_LIMIT),
        name="experts",
    )(e_lo, e_hi, live, xs, *weights)


def _combine_kernel(pos_ref, next_ref, ys_ref, x1_ref, mod_ref, o_ref, buf, sem):
    tm = buf.shape[1]
    step = pl.program_id(0) * pl.num_programs(1) + pl.program_id(1)
    last = pl.num_programs(0) * pl.num_programs(1) - 1

    def issue(p_ref, slot):
        for r in range(tm):
            pltpu.make_async_copy(ys_ref.at[pl.ds(p_ref[slot, r], 1), :], buf.at[slot, pl.ds(r, 1), :],
                                  sem.at[slot]).start(priority=r % 2)

    def finish(slot):
        pltpu.make_async_copy(ys_ref.at[pl.ds(0, tm), :], buf.at[slot], sem.at[slot]).wait()
        rows = pl.ds(slot * tm, tm)
        o_ref[rows, :] = x1_ref[rows, :] + mod_ref[5:6, :] * buf[slot]

    @pl.when(step == 0)
    def _():
        issue(pos_ref, 0)

    issue(pos_ref, 1)
    finish(0)

    @pl.when(step < last)
    def _():
        issue(next_ref, 0)

    finish(1)


def _combine(ys, pos, x1, mod, layer):
    b, s, d = x1.shape
    tm = MOE_DMA_TILE
    assert s % (2 * tm) == 0
    nt = s // (2 * tm)
    n_steps = b * nt
    pos2 = pos.reshape(n_steps, 2, tm)
    this_step = pl.BlockSpec((None, 2, tm), lambda bb, i: (bb * nt + i, 0, 0), memory_space=pltpu.SMEM)
    next_step = pl.BlockSpec((None, 2, tm), lambda bb, i: (jnp.minimum(bb * nt + i + 1, n_steps - 1), 0, 0),
                             memory_space=pltpu.SMEM)
    return pl.pallas_call(
        _combine_kernel,
        grid=(b, nt),
        in_specs=[this_step, next_step,
                  pl.BlockSpec(memory_space=pl.ANY),
                  pl.BlockSpec((None, 2 * tm, d), lambda bb, i: (bb, i, 0)),
                  pl.BlockSpec((None, None, N_MOD, d), lambda bb, i: (layer, bb, 0, 0))],
        out_specs=pl.BlockSpec((None, 2 * tm, d), lambda bb, i: (bb, i, 0)),
        out_shape=jax.ShapeDtypeStruct((b, s, d), F32),
        scratch_shapes=[pltpu.VMEM((2, tm, d), F32), pltpu.SemaphoreType.DMA((2,))],
        compiler_params=pltpu.CompilerParams(
            dimension_semantics=("arbitrary", "arbitrary"), vmem_limit_bytes=VMEM_LIMIT),
        name="combine",
    )(pos2, pos2, ys, x1, mod)


def _moe(h2x, info, counts, wg, wu, wd, layer, x1, mod):
    b, s, d = x1.shape
    t = b * s
    tm = MOE_ROW_TILE
    n_tiles = t // tm + N_CLASSES
    n_tiles += -n_tiles % EXPERT_TILES_PER_STEP
    counts = counts[:N_CLASSES, 0]
    padded = (counts + tm - 1) // tm * tm
    ends = jnp.cumsum(padded)
    offs = jnp.pad(ends - padded, (0, N_CLASS_PAD - N_CLASSES)).astype(I32)
    tile_cls = jnp.sum(jnp.arange(n_tiles, dtype=I32)[:, None] * tm >= ends[None, :], axis=1)
    live = (tile_cls < N_CLASSES).astype(I32)
    tile_cls = jnp.minimum(tile_cls, N_CLASSES - 1)
    group, pair = tile_cls // N_PAIRS, tile_cls % N_PAIRS
    pairs = jnp.asarray(_PAIRS, I32)
    e_lo = group * EXPERTS_PER_GROUP + pairs[pair, 0]
    e_hi = group * EXPERTS_PER_GROUP + pairs[pair, 1]

    pos = _positions(info, offs).reshape(t // MOE_DMA_TILE, 1, MOE_DMA_TILE)
    xs = _dispatch(h2x.reshape(t, d + LANES), pos, ends.astype(I32), n_tiles * tm)
    ys = _experts(xs, e_lo, e_hi, live, wg, wu, wd, layer)
    return _combine(ys, pos, x1, mod, layer)


def kernel(x, c, w_ada, b_ada, norm1_g, w_in, q_norm_g, k_norm_g, rel_bias, w_o_a, w_o_b, w_out,
           norm2_g, router_w, router_b, w_gate, w_up, w_down):
    b, s, d = x.shape
    depth = w_ada.shape[0]
    assert d % LANES == 0
    tm = min(TOKEN_TILE, s)

    mod = _adaln(c, w_ada, b_ada)
    bias = _bias_tiles(rel_bias, DSA_QUERY_BLOCK, DSA_KEY_BLOCK)
    rw = jnp.pad(router_w, ((0, 0), (0, LANES - N_EXPERTS)))
    rwh = rw.astype(BF16)
    rw2 = jnp.concatenate([rwh, (rw - rwh.astype(F32)).astype(BF16)], axis=1)
    rb = jnp.pad(router_b, (0, LANES - N_EXPERTS)).reshape(1, LANES)
    wg, wu, wd = w_gate.astype(BF16), w_up.astype(BF16), w_down.astype(BF16)
    woa, wob, wout = w_o_a.astype(BF16), w_o_b.astype(BF16), w_out.astype(BF16)
    w_packed = _pack_w_in(w_in)
    g1, g2 = norm1_g.reshape(depth, 1, d), norm2_g.reshape(depth, 1, d)
    qng, kng = q_norm_g.reshape(depth, HEAD_DIM, 1), k_norm_g.reshape(depth, 1, HEAD_DIM)

    for l in range(depth):
        qa, ka, va, qbt, qit, kb, vt, ki, wit, sga, sgb = _in_proj(
            x, mod, g1, w_packed, l, qng, kng, tm)
        oa = _sb_attention(qa, ka, va)
        ob = _dsa_attention(qit, ki, wit, qbt, kb, vt, bias)
        x1, h2x, info, counts = _merge(oa, ob, sga, sgb, x, mod, woa, wob, wout, l, g2, rw2, rb, tm)
        x = _moe(h2x, info, counts, wg, wu, wd, l, x1, mod)
    return x
```

```python
import functools
import math

import numpy as np
import jax
import jax.numpy as jnp
from jax import lax
from jax.experimental import pallas as pl
from jax.experimental.pallas import tpu as pltpu

F32 = jnp.float32
BF16 = jnp.bfloat16
I32 = jnp.int32
I16 = jnp.int16

HEAD_DIM = 64
N_HEADS = 8
N_EXPERTS = 16
N_GROUPS = 4
EXPERTS_PER_GROUP = 4
_PAIRS = ((0, 1), (0, 2), (0, 3), (1, 2), (1, 3), (2, 3))
N_PAIRS = len(_PAIRS)
N_CLASSES = N_GROUPS * N_PAIRS
N_CLASS_PAD = 32
MOE_ROW_TILE = 256
EXPERT_TILES_PER_STEP = 2
MOE_DMA_TILE = 512
DISPATCH_TILE = 1024
N_MOD = 6
EPS = 1e-6
INDEX_TOPK_MAX = 256
REL_BUCKETS = 32
REL_MAX_DIST = 128
LANES = 128
SUBLANES = 8
PACKED_SUBLANES = 16
HALF_BITS = 16
TOKEN_TILE = 512
EXP2_MAX = 126.0
DSA_QUERY_BLOCK = 256
DSA_KEY_BLOCK = 128
SB_QUERY_BLOCK = 256
SB_KEY_BLOCK = 256
SB_DEAD_LOG2 = 150.0
LOG2E = 1.4426950408889634
NEG_BIG = -1e30
INT_MIN = -2 ** 31
INT_MAX = 2 ** 31 - 1
HALF_BIAS = 2 ** 15

VMEM_LIMIT = 56 * 1024 * 1024


def _nt_dot(a, b):
    return lax.dot_general(a, b, (((1,), (1,)), ((), ())), preferred_element_type=F32)


def _dot(a, b):
    return jnp.dot(a, b, preferred_element_type=F32)


def _mod_kernel(c_ref, w_ref, b_ref, o_ref):
    c = c_ref[...]
    sc = c * jax.nn.sigmoid(c)
    o_ref[...] = _dot(sc.astype(BF16), w_ref[...].astype(BF16)) + b_ref[...]


def _adaln(c, w_ada, b_ada):
    depth, d, nd = w_ada.shape
    b = c.shape[0]
    out = pl.pallas_call(
        _mod_kernel,
        grid=(depth, nd // (2 * d)),
        in_specs=[pl.BlockSpec((b, d), lambda l, j: (0, 0)),
                  pl.BlockSpec((None, d, 2 * d), lambda l, j: (l, 0, j)),
                  pl.BlockSpec((None, 1, 2 * d), lambda l, j: (l, 0, j))],
        out_specs=pl.BlockSpec((None, b, 2 * d), lambda l, j: (l, 0, j)),
        out_shape=jax.ShapeDtypeStruct((depth, b, nd), F32),
        compiler_params=pltpu.CompilerParams(vmem_limit_bytes=VMEM_LIMIT),
        name="adaln",
    )(c, w_ada, b_ada.reshape(depth, 1, nd))
    return out.reshape(depth, b, N_MOD, d)


def _rms(x):
    return x * lax.rsqrt(jnp.mean(x * x, axis=-1, keepdims=True) + EPS)


def _in_kernel(x_ref, mod_ref, g1_ref, w_ref, qng_ref, kng_ref,
               qa_ref, ka_ref, va_ref, qbt_ref, qit_ref, kb_ref, vt_ref, ki_ref, wit_ref,
               sga_ref, sgb_ref):
    d = x_ref.shape[-1]
    tm = x_ref.shape[0]
    hw = N_HEADS * HEAD_DIM
    x = x_ref[...]
    h = (_rms(x) * g1_ref[...]) * (1.0 + mod_ref[1:2, :]) + mod_ref[0:1, :]
    hb = h.astype(BF16)

    def proj(lo, n):
        return _dot(hb, w_ref[:, lo:lo + n])

    def heads(r, ref):
        for pp in range(N_HEADS // 2):
            ref[pp] = r[:, pp * LANES:(pp + 1) * LANES].astype(ref.dtype)

    scale = HEAD_DIM ** -0.5
    heads(proj(0, hw) * (scale * LOG2E), qa_ref)
    heads(proj(hw, hw), ka_ref)
    heads(proj(2 * hw, hw), va_ref)

    rt = proj(3 * hw, hw).T
    for hh in range(N_HEADS):
        slab = rt[hh * HEAD_DIM:(hh + 1) * HEAD_DIM, :]
        inv = lax.rsqrt(jnp.mean(slab * slab, axis=0, keepdims=True) + EPS)
        qbt_ref[hh] = (((slab * inv) * qng_ref[...]) * scale).astype(BF16)

    r = proj(4 * hw, 2 * HEAD_DIM)
    kb_ref[...] = (_rms(r[:, :HEAD_DIM]) * kng_ref[...]).astype(BF16)
    vt = r.T[HEAD_DIM:, :]
    ones_row = (lax.broadcasted_iota(I32, (HEAD_DIM, LANES), 0) == 0).astype(BF16)
    for cc in range(tm // LANES):
        vt_ref[cc, :HEAD_DIM, :] = vt[:, cc * LANES:(cc + 1) * LANES].astype(BF16)
        vt_ref[cc, HEAD_DIM:, :] = ones_row

    rt = proj(4 * hw + LANES, hw).T
    for hh in range(N_HEADS):
        qit_ref[hh] = rt[hh * HEAD_DIM:(hh + 1) * HEAD_DIM, :].astype(BF16)

    r = proj(5 * hw + LANES, LANES)
    ki_ref[...] = r[:, :HEAD_DIM].astype(BF16)
    wit_ref[...] = r.T[HEAD_DIM:HEAD_DIM + N_HEADS, :]

    off = 5 * hw + 2 * LANES
    sga_ref[...] = jax.nn.sigmoid(proj(off, d)).astype(BF16)
    sgb_ref[...] = jax.nn.sigmoid(proj(off + d, d)).astype(BF16)


def _pack_w_in(w_in):
    hw = N_HEADS * HEAD_DIM
    a = 4 * hw + 2 * HEAD_DIM + hw
    small = w_in[..., a:a + HEAD_DIM + N_HEADS]
    small = jnp.pad(small, ((0, 0), (0, 0), (0, LANES - small.shape[-1])))
    gates = w_in[..., a + HEAD_DIM + N_HEADS:]
    return jnp.concatenate([w_in[..., :a], small, gates], axis=-1).astype(BF16)


def _in_proj(x, mod_l, g1, w_packed, layer, qng, kng, tm):
    b, s, d = x.shape
    hm = jax.ShapeDtypeStruct((b, N_HEADS // 2, s, LANES), BF16)
    hmt = jax.ShapeDtypeStruct((b, N_HEADS, HEAD_DIM, s), BF16)
    tok64 = jax.ShapeDtypeStruct((b, s, HEAD_DIM), BF16)
    hm_spec = pl.BlockSpec((None, N_HEADS // 2, tm, LANES), lambda bb, i: (bb, 0, i, 0))
    hmt_spec = pl.BlockSpec((None, N_HEADS, HEAD_DIM, tm), lambda bb, i: (bb, 0, 0, i))
    tok = lambda n: pl.BlockSpec((None, tm, n), lambda bb, i: (bb, i, 0))
    nw = w_packed.shape[-1]
    return pl.pallas_call(
        _in_kernel,
        grid=(b, s // tm),
        in_specs=[tok(d),
                  pl.BlockSpec((None, N_MOD, d), lambda bb, i: (bb, 0, 0)),
                  pl.BlockSpec((1, d), lambda bb, i: (0, 0)),
                  pl.BlockSpec((None, d, nw), lambda bb, i: (layer, 0, 0)),
                  pl.BlockSpec((HEAD_DIM, 1), lambda bb, i: (0, 0)),
                  pl.BlockSpec((1, HEAD_DIM), lambda bb, i: (0, 0))],
        out_specs=[hm_spec] * 3 + [hmt_spec] * 2
                  + [tok(HEAD_DIM),
                     pl.BlockSpec((None, tm // LANES, 2 * HEAD_DIM, LANES), lambda bb, i: (bb, i, 0, 0)),
                     tok(HEAD_DIM),
                     pl.BlockSpec((None, N_HEADS, tm), lambda bb, i: (bb, 0, i)),
                     tok(d), tok(d)],
        out_shape=[hm] * 3 + [hmt] * 2
                  + [tok64,
                     jax.ShapeDtypeStruct((b, s // LANES, 2 * HEAD_DIM, LANES), BF16),
                     tok64,
                     jax.ShapeDtypeStruct((b, N_HEADS, s), F32),
                     jax.ShapeDtypeStruct((b, s, d), BF16),
                     jax.ShapeDtypeStruct((b, s, d), BF16)],
        compiler_params=pltpu.CompilerParams(
            dimension_semantics=("parallel", "parallel"), vmem_limit_bytes=VMEM_LIMIT),
        name="in_proj",
    )(x, mod_l, g1, w_packed, qng, kng)


def _sb_kernel(q_ref, k_ref, v_ref, suffix_ref, o_ref, acc_scr, csum_scr, sp_scr, ls_scr, qm_scr):
    tq = q_ref.shape[1]
    tk = SB_KEY_BLOCK
    i = pl.program_id(1)
    nfull = (i * tq) // tk
    row = lax.broadcasted_iota(I32, (tq, tk), 0)
    col = lax.broadcasted_iota(I32, (tq, tk), 1)
    before = (col - row) < (i * tq - nfull * tk)
    suffix = suffix_ref[...]
    low_half = lax.broadcasted_iota(I32, (tq, LANES), 1) < HEAD_DIM
    for hh in range(N_HEADS):
        slab = q_ref[hh // 2]
        qm_scr[hh] = jnp.where(low_half == (hh % 2 == 0), slab, jnp.zeros_like(slab))

    def stage_a(hh, st, first):
        z2 = _nt_dot(qm_scr[hh], k_ref[hh // 2, pl.ds(st, tk), :])
        sp2 = jnp.maximum(jnp.log2(1.0 + jnp.exp2(jnp.minimum(z2, EXP2_MAX))), z2)
        if first:
            sp2 = jnp.where(before, sp2, 0.0)
        sp_scr[hh] = sp2.astype(BF16)
        ls_scr[hh] = z2 - sp2

    def stage_b(hh, st, first):
        sp = sp_scr[hh]
        suf = _dot(sp, suffix)
        if first:
            a = jnp.where(before, jnp.exp2(ls_scr[hh] - suf), 0.0)
            csum = jnp.zeros((tq, 1), F32)
        else:
            csum = csum_scr[hh]
            a = jnp.exp2((ls_scr[hh] - suf) - csum)
        c = _dot(a.astype(BF16), v_ref[hh // 2, pl.ds(st, tk), :])
        if first:
            acc_scr[hh] = c
        else:
            acc_scr[hh] += c
        csum_scr[hh] = csum + (suf[:, 0:1] + sp[:, 0:1].astype(F32))

    def key_block(st, first):
        for hh in range(N_HEADS):
            stage_a(hh, st, first)
        for hh in range(N_HEADS):
            stage_b(hh, st, first)

    key_block(pl.multiple_of(nfull * tk, tk), True)

    def live():
        return jnp.min(csum_scr[...]) < SB_DEAD_LOG2

    def body(carry):
        jj, _ = carry
        key_block(pl.multiple_of((nfull - 1 - jj) * tk, tk), False)
        return jj + 1, live()

    lax.while_loop(lambda carry: (carry[0] < nfull) & carry[1], body, (jnp.int32(0), live()))
    for pp in range(N_HEADS // 2):
        o_ref[:, pp * LANES:(pp + 1) * LANES] = jnp.where(
            low_half, acc_scr[2 * pp], acc_scr[2 * pp + 1]).astype(o_ref.dtype)


def _sb_attention(qa, ka, va):
    b, npair, s, _ = qa.shape
    nh, dh = 2 * npair, HEAD_DIM
    tq = SB_QUERY_BLOCK
    assert SB_KEY_BLOCK % tq == 0 and s % SB_KEY_BLOCK == 0
    return pl.pallas_call(
        _sb_kernel,
        grid=(b, s // tq),
        in_specs=[pl.BlockSpec((None, npair, tq, LANES), lambda bb, i: (bb, 0, i, 0)),
                  pl.BlockSpec((None, npair, s, LANES), lambda bb, i: (bb, 0, 0, 0)),
                  pl.BlockSpec((None, npair, s, LANES), lambda bb, i: (bb, 0, 0, 0)),
                  pl.BlockSpec((SB_KEY_BLOCK, SB_KEY_BLOCK), lambda bb, i: (0, 0))],
        out_specs=pl.BlockSpec((None, tq, nh * dh), lambda bb, i: (bb, i, 0)),
        out_shape=jax.ShapeDtypeStruct((b, s, nh * dh), BF16),
        scratch_shapes=[pltpu.VMEM((nh, tq, LANES), F32), pltpu.VMEM((nh, tq, 1), F32),
                        pltpu.VMEM((nh, tq, SB_KEY_BLOCK), BF16), pltpu.VMEM((nh, tq, SB_KEY_BLOCK), F32),
                        pltpu.VMEM((nh, tq, LANES), BF16)],
        compiler_params=pltpu.CompilerParams(
            dimension_semantics=("parallel", "parallel"), vmem_limit_bytes=VMEM_LIMIT),
        name="stick_breaking",
    )(qa, ka, va, jnp.asarray(np.tril(np.ones((SB_KEY_BLOCK, SB_KEY_BLOCK), np.float32), -1), BF16))


def _t5_bucket_np(dist):
    max_exact = REL_BUCKETS // 2
    d_f = np.maximum(dist, 1).astype(np.float32)
    large = max_exact + (np.log(d_f / np.float32(max_exact))
                         / np.float32(math.log(REL_MAX_DIST / max_exact))
                         * np.float32(REL_BUCKETS - max_exact)).astype(np.int32)
    large = np.minimum(large, REL_BUCKETS - 1)
    return np.where(dist < max_exact, dist, large).astype(np.int32)


def _bias_tiles(rel_bias, tq, tk):
    n_near = tq // tk + 1
    key = np.arange(tk)[:, None]
    query = np.arange(tq)[None, :]
    bk = np.stack([_t5_bucket_np(np.maximum(query - key - (n - 1) * tk, 0)) for n in range(n_near)])
    far = int(_t5_bucket_np(np.array([REL_MAX_DIST]))[0])
    nh = rel_bias.shape[1]

    def body(rb_ref, bk_ref, o_ref):
        buckets = bk_ref[...]
        for hh in range(nh):
            tile = jnp.zeros((tk, tq), F32)
            for bucket in range(REL_BUCKETS):
                tile = jnp.where(buckets == bucket, rb_ref[hh, bucket] - rb_ref[hh, far], tile)
            o_ref[hh] = tile

    return pl.pallas_call(
        body,
        grid=(n_near,),
        in_specs=[pl.BlockSpec(memory_space=pltpu.SMEM),
                  pl.BlockSpec((None, tk, tq), lambda n: (n, 0, 0))],
        out_specs=pl.BlockSpec((None, nh, tk, tq), lambda n: (n, 0, 0, 0)),
        out_shape=jax.ShapeDtypeStruct((n_near, nh, tk, tq), F32),
        name="bias_tiles",
    )(rel_bias.T, jnp.asarray(bk))


def _sort_key(x):
    bits = pltpu.bitcast(x, I32)
    return bits ^ ((bits >> 31) & INT_MAX)


def _fold_rows(x, op):
    return functools.reduce(
        op, [x[g * SUBLANES:(g + 1) * SUBLANES, :] for g in range(x.shape[0] // SUBLANES)])


def _dsa_kernel(ktop, qit_ref, ki_ref, wit_ref, qbt_ref, kb_ref, vt_ref, bias_ref, o_ref,
                key_scr, hi_scr, lo_scr, m_scr, acc_scr, s_scr):
    tq = qit_ref.shape[2]
    tk = DSA_KEY_BLOCK
    per_q = tq // tk
    i = pl.program_id(1)
    first_own = per_q * i
    nchunks = per_q * (i + 1)
    key_i = lax.broadcasted_iota(I32, (tk, tq), 0)
    qry_i = lax.broadcasted_iota(I32, (tk, tq), 1)

    def causal(rel):
        return (key_i - qry_i) <= (-rel * tk)

    def for_chunks(fn):
        def far(j, carry):
            for u in range(per_q):
                fn(j * per_q + u, None)
            return carry
        lax.fori_loop(0, i - 1, far, 0)

        @pl.when(i >= 1)
        def _():
            for rel in range(-per_q, -1):
                fn(first_own + rel, None)
            fn(first_own - 1, -1)

        for rel in range(per_q):
            fn(first_own + rel, rel)

    def score_chunk(c, rel):
        kic = ki_ref[pl.ds(pl.multiple_of(c * tk, tk), tk), :]
        score = jnp.zeros((tk, tq), F32)
        for hh in range(N_HEADS):
            score = score + wit_ref[hh:hh + 1, :] * jnp.maximum(_dot(kic, qit_ref[hh]), 0.0)
        keys = _sort_key(score)
        if rel is not None and rel >= 0:
            keys = jnp.where(causal(rel), keys, INT_MIN)
        key_scr[c] = keys
        hi_scr[c] = (keys >> HALF_BITS).astype(I16)
        lo_scr[c] = ((keys & (2 * HALF_BIAS - 1)) - HALF_BIAS).astype(I16)

    for_chunks(score_chunk)

    def count(pred):
        def body(j, part):
            for u in range(per_q):
                part = part + _fold_rows(jnp.where(pred(key_scr[j * per_q + u]), 1, 0), jnp.add)
            return part
        part = lax.fori_loop(0, i + 1, body, jnp.zeros((SUBLANES, tq), I32))
        return jnp.sum(part, axis=0, keepdims=True)

    def count16(scr, pred):
        one, zero = jnp.int16(1), jnp.int16(0)

        def block(j, part):
            for u in range(per_q):
                hit = jnp.where(pred(scr[j * per_q + u]), one, zero)
                part = part + functools.reduce(
                    jnp.add, [hit[g * PACKED_SUBLANES:(g + 1) * PACKED_SUBLANES, :]
                              for g in range(tk // PACKED_SUBLANES)])
            return part

        nblk = i + 1
        part = lax.fori_loop(0, nblk // 2, lambda jj, part: block(2 * jj + 1, block(2 * jj, part)),
                             jnp.zeros((PACKED_SUBLANES, tq), I16))
        part = lax.cond(nblk % 2 == 1, lambda p: block(nblk - 1, p), lambda p: p, part)
        return jnp.sum(part.astype(I32), axis=0, keepdims=True)

    def search16(scr, base):
        def bit_body(bi, t_u):
            cand_u = t_u | jnp.left_shift(jnp.int32(1), HALF_BITS - 1 - bi)
            cand = (cand_u - HALF_BIAS).astype(I16)
            return jnp.where(base + count16(scr, lambda k: k >= cand) >= ktop, cand_u, t_u)
        return lax.fori_loop(0, HALF_BITS, bit_body, jnp.zeros((1, tq), I32))

    def find_threshold():
        t_hi = search16(hi_scr, 0) - HALF_BIAS
        t_hi16 = t_hi.astype(I16)
        above = count16(hi_scr, lambda k: k > t_hi16)

        def keep_band(j, carry):
            for u in range(per_q):
                c = j * per_q + u
                lo_scr[c] = jnp.where(hi_scr[c] == t_hi16, lo_scr[c], jnp.int16(-HALF_BIAS))
            return carry

        lax.fori_loop(0, i + 1, keep_band, 0)
        return t_hi * (2 * HALF_BIAS) + search16(lo_scr, above)

    thr = lax.cond((i + 1) * tq <= ktop, lambda: jnp.full((1, tq), INT_MIN, I32), find_threshold)
    n_ge = count(lambda k: k >= thr)

    @pl.when(jnp.max(n_ge) > ktop)
    def _():
        need = (ktop - count(lambda k: k > thr)).astype(F32)
        lower = (lax.broadcasted_iota(I32, (tk, tk), 1)
                 < lax.broadcasted_iota(I32, (tk, tk), 0)).astype(BF16)

        def body(c, seen):
            k = key_scr[c]
            eq = k == thr
            eqf = eq.astype(BF16)
            rank = seen + _dot(lower, eqf)
            drop = eq & (rank >= need) & (n_ge > ktop)
            key_scr[c] = jnp.where(drop, thr - 1, k)
            return seen + jnp.sum(eqf.astype(F32), axis=0, keepdims=True)

        lax.fori_loop(0, nchunks, body, jnp.zeros((1, tq), F32))

    def logits(c, rel):
        sel = key_scr[c] >= thr
        if rel is not None and rel >= 0:
            sel = sel & causal(rel)
        kbc = kb_ref[pl.ds(pl.multiple_of(c * tk, tk), tk), :]

        def head(hh):
            s = _dot(kbc, qbt_ref[hh])
            if rel is not None:
                s = s + bias_ref[rel + 1, hh]
            return s
        return sel, head

    m_scr[...] = jnp.full(m_scr.shape, NEG_BIG, F32)

    def max_chunk(c, rel):
        sel, head = logits(c, rel)
        for hh in range(N_HEADS):
            s = jnp.where(sel, head(hh), NEG_BIG)
            s_scr[hh, c] = s
            m_scr[hh] = jnp.maximum(m_scr[hh], _fold_rows(s, jnp.maximum))

    for_chunks(max_chunk)
    m = [jnp.max(m_scr[hh], axis=0, keepdims=True) for hh in range(N_HEADS)]

    acc_scr[...] = jnp.zeros(acc_scr.shape, F32)

    def att_block(j, carry):
        for u in range(per_q):
            c = j * per_q + u
            vtc = vt_ref[c]
            for hh in range(N_HEADS):
                p = jnp.exp(s_scr[hh, c] - m[hh])
                acc_scr[hh] += _dot(vtc, p.astype(BF16))
        return carry

    lax.fori_loop(0, i + 1, att_block, 0)

    out_t = jnp.concatenate(
        [acc_scr[hh, :HEAD_DIM, :] / acc_scr[hh, HEAD_DIM:HEAD_DIM + 1, :] for hh in range(N_HEADS)],
        axis=0)
    o_ref[...] = out_t.T.astype(o_ref.dtype)


def _dsa_attention(qit, ki, wit, qbt, kb, vt, bias):
    b, nh, dh, s = qit.shape
    tq = DSA_QUERY_BLOCK
    tk = DSA_KEY_BLOCK
    assert tk == REL_MAX_DIST and tk == LANES and tq % tk == 0 and s % tq == 0
    ktop = min(INDEX_TOPK_MAX, s // 4)
    hmt_q = pl.BlockSpec((None, nh, dh, tq), lambda bb, i: (bb, 0, 0, i))
    full64 = pl.BlockSpec((None, s, dh), lambda bb, i: (bb, 0, 0))
    return pl.pallas_call(
        functools.partial(_dsa_kernel, ktop),
        grid=(b, s // tq),
        in_specs=[hmt_q, full64,
                  pl.BlockSpec((None, nh, tq), lambda bb, i: (bb, 0, i)),
                  hmt_q, full64,
                  pl.BlockSpec((None, s // tk, 2 * dh, tk), lambda bb, i: (bb, 0, 0, 0)),
                  pl.BlockSpec(bias.shape, lambda bb, i: (0, 0, 0, 0))],
        out_specs=pl.BlockSpec((None, tq, nh * dh), lambda bb, i: (bb, i, 0)),
        out_shape=jax.ShapeDtypeStruct((b, s, nh * dh), BF16),
        scratch_shapes=[pltpu.VMEM((s // tk, tk, tq), I32),
                        pltpu.VMEM((s // tk, tk, tq), I16),
                        pltpu.VMEM((s // tk, tk, tq), I16),
                        pltpu.VMEM((nh, SUBLANES, tq), F32),
                        pltpu.VMEM((nh, 2 * dh, tq), F32),
                        pltpu.VMEM((nh, s // tk, tk, tq), F32)],
        compiler_params=pltpu.CompilerParams(
            dimension_semantics=("parallel", "parallel"), vmem_limit_bytes=VMEM_LIMIT),
        name="sparse_attention",
    )(qit, ki, wit, qbt, kb, vt, bias)


def _first_max4(a):
    m1 = jnp.maximum(jnp.maximum(a[0], a[1]), jnp.maximum(a[2], a[3]))
    i1 = jnp.where(a[0] == m1, 0, jnp.where(a[1] == m1, 1, jnp.where(a[2] == m1, 2, 3)))
    rest = [jnp.where(i1 == j, -1.0, a[j]) for j in range(4)]
    m2 = jnp.maximum(jnp.maximum(rest[0], rest[1]), jnp.maximum(rest[2], rest[3]))
    i2 = jnp.where(rest[0] == m2, 0, jnp.where(rest[1] == m2, 1, jnp.where(rest[2] == m2, 2, 3)))
    return m1, i1, m2, i2


def _route(lt):
    lg = [lt[e:e + 1, :] for e in range(N_EXPERTS)]
    mx = functools.reduce(jnp.maximum, lg)
    ex = [jnp.exp(v - mx) for v in lg]
    tot = functools.reduce(lambda u, v: u + v, ex)
    p = [v / tot for v in ex]
    gs = []
    for g in range(N_GROUPS):
        m1, _, m2, _ = _first_max4(p[4 * g:4 * g + 4])
        gs.append(m1 + m2)
    _, gbest, _, _ = _first_max4(gs)
    chosen = [jnp.where(gbest == 0, p[j], jnp.where(gbest == 1, p[4 + j],
              jnp.where(gbest == 2, p[8 + j], p[12 + j]))) for j in range(4)]
    m1, i1, m2, i2 = _first_max4(chosen)
    den = m1 + m2
    lo = jnp.minimum(i1, i2)
    hi = jnp.maximum(i1, i2)
    pair = jnp.where(lo == 0, hi - 1, jnp.where(lo == 1, hi + 1, N_PAIRS - 1))
    first_is_lo = i1 < i2
    w1 = m1 / den
    w2 = m2 / den
    return gbest * N_PAIRS + pair, jnp.where(first_is_lo, w1, w2), jnp.where(first_is_lo, w2, w1)


def _merge_kernel(oa_ref, ob_ref, sga_ref, sgb_ref, x_ref, mod_ref, woa_ref, wob_ref, wout_ref,
                  g2_ref, rw2_ref, rb_ref, earlier_ref, x1_ref, h2x_ref, info_ref, cnt_ref, carry_scr):
    tm, d = x_ref.shape

    @pl.when((pl.program_id(0) == 0) & (pl.program_id(1) == 0))
    def _():
        carry_scr[...] = jnp.zeros(carry_scr.shape, F32)

    merged = (sga_ref[...].astype(F32) * _dot(oa_ref[...], woa_ref[...])
              + sgb_ref[...].astype(F32) * _dot(ob_ref[...], wob_ref[...]))
    x1 = x_ref[...] + mod_ref[2:3, :] * _dot(merged.astype(BF16), wout_ref[...])
    x1_ref[...] = x1
    h2 = (_rms(x1) * g2_ref[...]) * (1.0 + mod_ref[4:5, :]) + mod_ref[3:4, :]
    h2x_ref[:, :d] = h2
    hi = h2.astype(BF16)
    lo = (h2 - hi.astype(F32)).astype(BF16)
    both = _dot(hi, rw2_ref[...])
    logits = (both[:, :LANES] + _dot(lo, rw2_ref[:, :LANES]) + both[:, LANES:]) + rb_ref[...]
    cls, cw_lo, cw_hi = _route(logits.T)

    ncls = carry_scr.shape[0]
    onehot = (lax.broadcasted_iota(I32, (ncls, tm), 0) == cls).astype(F32)
    earlier = earlier_ref[...]
    before = _dot(onehot.astype(BF16), earlier)
    carry = carry_scr[...]
    carry_t = jnp.concatenate([carry] * (tm // LANES), axis=1)
    rank = jnp.sum(onehot * (carry_t + before), axis=0, keepdims=True).astype(I32)
    carry = carry + jnp.sum(onehot, axis=1, keepdims=True)
    carry_scr[...] = carry
    cnt_ref[...] = carry.astype(I32)

    sub = lax.broadcasted_iota(I32, (SUBLANES, tm), 0)
    info_ref[...] = jnp.where(sub == 0, cls, jnp.where(sub == 1, rank, 0))
    sub = lax.broadcasted_iota(I32, (LANES, tm), 0)
    weights_t = jnp.where(sub == 0, cw_lo, jnp.where(sub == 1, cw_hi, 0.0))
    h2x_ref[:, d:] = weights_t.T


def _merge(oa, ob, sga, sgb, x, mod_l, woa, wob, wout, layer, g2, rw2, rb, tm):
    b, s, d = x.shape
    hw = oa.shape[-1]
    nt = s // tm
    tok = lambda n: pl.BlockSpec((None, tm, n), lambda bb, i: (bb, i, 0))
    const = lambda r, c: pl.BlockSpec((r, c), lambda bb, i: (0, 0))
    layered = lambda r, c: pl.BlockSpec((None, r, c), lambda bb, i: (layer, 0, 0))
    return pl.pallas_call(
        _merge_kernel,
        grid=(b, nt),
        in_specs=[tok(hw), tok(hw), tok(d), tok(d), tok(d),
                  pl.BlockSpec((None, N_MOD, d), lambda bb, i: (bb, 0, 0)),
                  layered(hw, d), layered(hw, d), layered(d, d), const(1, d),
                  const(d, 2 * LANES), const(1, LANES), const(tm, tm)],
        out_specs=[tok(d), tok(d + LANES),
                   pl.BlockSpec((SUBLANES, tm), lambda bb, i: (0, bb * nt + i)),
                   const(N_CLASS_PAD, LANES)],
        out_shape=[jax.ShapeDtypeStruct((b, s, d), F32),
                   jax.ShapeDtypeStruct((b, s, d + LANES), F32),
                   jax.ShapeDtypeStruct((SUBLANES, b * s), I32),
                   jax.ShapeDtypeStruct((N_CLASS_PAD, LANES), I32)],
        scratch_shapes=[pltpu.VMEM((N_CLASS_PAD, LANES), F32)],
        compiler_params=pltpu.CompilerParams(
            dimension_semantics=("arbitrary", "arbitrary"), vmem_limit_bytes=VMEM_LIMIT),
        name="merge_route",
    )(oa, ob, sga, sgb, x, mod_l, woa, wob, wout, g2, rw2, rb,
      jnp.asarray(np.triu(np.ones((tm, tm), np.float32), 1), BF16))


def _positions_kernel(offs_ref, info_ref, pos_ref):
    cls = info_ref[0:1, :]
    pos = info_ref[1:2, :]
    for c in range(N_CLASSES):
        pos = pos + jnp.where(cls == c, offs_ref[c], 0)
    pos_ref[...] = jnp.broadcast_to(pos, pos_ref.shape)


def _positions(info, offs):
    return pl.pallas_call(
        _positions_kernel,
        in_specs=[pl.BlockSpec(memory_space=pltpu.SMEM), pl.BlockSpec(info.shape, lambda: (0, 0))],
        out_specs=pl.BlockSpec(info.shape, lambda: (0, 0)),
        out_shape=jax.ShapeDtypeStruct(info.shape, I32),
        name="positions",
    )(offs, info)[0]


def _dispatch_kernel(ends_ref, pos_ref, src_ref, dst_ref, zero_scr, sem, zero_sem):
    tm = pos_ref.shape[1]
    row_tile = zero_scr.shape[0]

    @pl.when(pl.program_id(0) == 0)
    def _():
        zero_scr[...] = jnp.zeros(zero_scr.shape, F32)

        def zero_tail(c):
            start = pl.multiple_of(ends_ref[c] - row_tile, row_tile)
            return pltpu.make_async_copy(zero_scr, dst_ref.at[pl.ds(start, row_tile), :], zero_sem)

        def nonempty(c):
            return ends_ref[c] > (ends_ref[c - 1] if c else 0)

        for c in range(N_CLASSES):
            @pl.when(nonempty(c))
            def _(c=c):
                zero_tail(c).start()
        for c in range(N_CLASSES):
            @pl.when(nonempty(c))
            def _(c=c):
                zero_tail(c).wait()

        def unused_tile(j):
            start = pl.multiple_of(ends_ref[N_CLASSES - 1] + j * row_tile, row_tile)
            return pltpu.make_async_copy(zero_scr, dst_ref.at[pl.ds(start, row_tile), :], zero_sem)

        n_unused = (dst_ref.shape[0] - ends_ref[N_CLASSES - 1]) // row_tile
        lax.fori_loop(0, n_unused, lambda j, carry: (unused_tile(j).start(), carry)[1], 0)
        lax.fori_loop(0, n_unused, lambda j, carry: (unused_tile(j).wait(), carry)[1], 0)

    for r in range(tm):
        pltpu.make_async_copy(src_ref.at[pl.ds(r, 1), :], dst_ref.at[pl.ds(pos_ref[0, r], 1), :],
                              sem).start(priority=r % 2)
    pltpu.make_async_copy(src_ref, dst_ref.at[pl.ds(0, tm), :], sem).wait()


def _dispatch(h2x, pos, ends, n_rows):
    t, w = h2x.shape
    tm = DISPATCH_TILE
    pos = pos.reshape(t // tm, 1, tm)
    smem_row = pl.BlockSpec((None, 1, tm), lambda i: (i, 0, 0), memory_space=pltpu.SMEM)
    return pl.pallas_call(
        _dispatch_kernel,
        grid=(t // tm,),
        in_specs=[pl.BlockSpec(memory_space=pltpu.SMEM), smem_row, pl.BlockSpec((tm, w), lambda i: (i, 0))],
        out_specs=pl.BlockSpec(memory_space=pl.ANY),
        out_shape=jax.ShapeDtypeStruct((n_rows, w), F32),
        scratch_shapes=[pltpu.VMEM((MOE_ROW_TILE, w), F32), pltpu.SemaphoreType.DMA, pltpu.SemaphoreType.DMA],
        compiler_params=pltpu.CompilerParams(
            dimension_semantics=("arbitrary",), vmem_limit_bytes=VMEM_LIMIT),
        name="dispatch",
    )(ends, pos, h2x)


def _experts_kernel(elo_ref, ehi_ref, live_ref, xs_ref, *refs):
    del elo_ref, ehi_ref
    ys_ref = refs[-1]
    d = ys_ref.shape[1]
    tm = MOE_ROW_TILE
    step = pl.program_id(0)

    for u in range(EXPERT_TILES_PER_STEP):
        wg_lo, wg_hi, wu_lo, wu_hi, wd_lo, wd_hi = refs[6 * u:6 * u + 6]
        rows = pl.ds(u * tm, tm)

        @pl.when(live_ref[step * EXPERT_TILES_PER_STEP + u] == 1)
        def _():
            h = xs_ref[rows, :d].astype(BF16)

            def expert(wg, wu, wd, weight):
                gate = _dot(h, wg[...])
                hidden = ((gate * jax.nn.sigmoid(gate)) * _dot(h, wu[...])) * weight
                return _dot(hidden.astype(BF16), wd[...])

            ys_ref[rows, :] = (expert(wg_lo, wu_lo, wd_lo, xs_ref[rows, d:d + 1])
                               + expert(wg_hi, wu_hi, wd_hi, xs_ref[rows, d + 1:d + 2]))

        @pl.when(live_ref[step * EXPERT_TILES_PER_STEP + u] == 0)
        def _():
            ys_ref[rows, :] = jnp.zeros((tm, d), F32)


def _experts(xs, e_lo, e_hi, live, wg, wu, wd, layer):
    n_rows, w = xs.shape
    _, ne, d, f = wg.shape
    per = EXPERT_TILES_PER_STEP
    tm = MOE_ROW_TILE
    assert n_rows % (per * tm) == 0

    def wspec(shape, use_hi, u):
        def index(step, elo, ehi, lv):
            return layer, (ehi if use_hi else elo)[step * per + u], 0, 0
        return pl.BlockSpec((None, None) + shape, index)

    wspecs, weights = [], []
    for u in range(per):
        wspecs += [wspec((d, f), 0, u), wspec((d, f), 1, u), wspec((d, f), 0, u), wspec((d, f), 1, u),
                   wspec((f, d), 0, u), wspec((f, d), 1, u)]
        weights += [wg, wg, wu, wu, wd, wd]
    grid_spec = pltpu.PrefetchScalarGridSpec(
        num_scalar_prefetch=3,
        grid=(n_rows // (per * tm),),
        in_specs=[pl.BlockSpec((per * tm, w), lambda step, elo, ehi, lv: (step, 0))] + wspecs,
        out_specs=pl.BlockSpec((per * tm, d), lambda step, elo, ehi, lv: (step, 0)),
    )
    return pl.pallas_call(
        _experts_kernel,
        grid_spec=grid_spec,
        out_shape=jax.ShapeDtypeStruct((n_rows, d), F32),
        compiler_params=pltpu.CompilerParams(
            dimension_semantics=("arbitrary",), vmem_limit_bytes=VMEM_LIMIT),
        name="experts",
    )(e_lo, e_hi, live, xs, *weights)


def _combine_kernel(pos_ref, next_ref, ys_ref, x1_ref, mod_ref, o_ref, buf, sem):
    tm = buf.shape[1]
    step = pl.program_id(0) * pl.num_programs(1) + pl.program_id(1)
    last = pl.num_programs(0) * pl.num_programs(1) - 1

    def issue(p_ref, slot):
        for r in range(tm):
            pltpu.make_async_copy(ys_ref.at[pl.ds(p_ref[slot, r], 1), :], buf.at[slot, pl.ds(r, 1), :],
                                  sem.at[slot]).start(priority=r % 2)

    def finish(slot):
        pltpu.make_async_copy(ys_ref.at[pl.ds(0, tm), :], buf.at[slot], sem.at[slot]).wait()
        rows = pl.ds(slot * tm, tm)
        o_ref[rows, :] = x1_ref[rows, :] + mod_ref[5:6, :] * buf[slot]

    @pl.when(step == 0)
    def _():
        issue(pos_ref, 0)

    issue(pos_ref, 1)
    finish(0)

    @pl.when(step < last)
    def _():
        issue(next_ref, 0)

    finish(1)


def _combine(ys, pos, x1, mod_l):
    b, s, d = x1.shape
    tm = MOE_DMA_TILE
    assert s % (2 * tm) == 0
    nt = s // (2 * tm)
    n_steps = b * nt
    pos2 = pos.reshape(n_steps, 2, tm)
    this_step = pl.BlockSpec((None, 2, tm), lambda bb, i: (bb * nt + i, 0, 0), memory_space=pltpu.SMEM)
    next_step = pl.BlockSpec((None, 2, tm), lambda bb, i: (jnp.minimum(bb * nt + i + 1, n_steps - 1), 0, 0),
                             memory_space=pltpu.SMEM)
    return pl.pallas_call(
        _combine_kernel,
        grid=(b, nt),
        in_specs=[this_step, next_step,
                  pl.BlockSpec(memory_space=pl.ANY),
                  pl.BlockSpec((None, 2 * tm, d), lambda bb, i: (bb, i, 0)),
                  pl.BlockSpec((None, N_MOD, d), lambda bb, i: (bb, 0, 0))],
        out_specs=pl.BlockSpec((None, 2 * tm, d), lambda bb, i: (bb, i, 0)),
        out_shape=jax.ShapeDtypeStruct((b, s, d), F32),
        scratch_shapes=[pltpu.VMEM((2, tm, d), F32), pltpu.SemaphoreType.DMA((2,))],
        compiler_params=pltpu.CompilerParams(
            dimension_semantics=("arbitrary", "arbitrary"), vmem_limit_bytes=VMEM_LIMIT),
        name="combine",
    )(pos2, pos2, ys, x1, mod_l)


def _moe(h2x, info, counts, wg, wu, wd, layer, x1, mod_l):
    b, s, d = x1.shape
    t = b * s
    tm = MOE_ROW_TILE
    n_tiles = t // tm + N_CLASSES
    n_tiles += -n_tiles % EXPERT_TILES_PER_STEP
    counts = counts[:N_CLASSES, 0]
    padded = (counts + tm - 1) // tm * tm
    ends = jnp.cumsum(padded)
    offs = jnp.pad(ends - padded, (0, N_CLASS_PAD - N_CLASSES)).astype(I32)
    tile_cls = jnp.sum(jnp.arange(n_tiles, dtype=I32)[:, None] * tm >= ends[None, :], axis=1)
    live = (tile_cls < N_CLASSES).astype(I32)
    tile_cls = jnp.minimum(tile_cls, N_CLASSES - 1)
    group, pair = tile_cls // N_PAIRS, tile_cls % N_PAIRS
    pairs = jnp.asarray(_PAIRS, I32)
    e_lo = group * EXPERTS_PER_GROUP + pairs[pair, 0]
    e_hi = group * EXPERTS_PER_GROUP + pairs[pair, 1]

    pos = _positions(info, offs).reshape(t // MOE_DMA_TILE, 1, MOE_DMA_TILE)
    xs = _dispatch(h2x.reshape(t, d + LANES), pos, ends.astype(I32), n_tiles * tm)
    ys = _experts(xs, e_lo, e_hi, live, wg, wu, wd, layer)
    return _combine(ys, pos, x1, mod_l)


def kernel(x, c, w_ada, b_ada, norm1_g, w_in, q_norm_g, k_norm_g, rel_bias, w_o_a, w_o_b, w_out,
           norm2_g, router_w, router_b, w_gate, w_up, w_down):
    b, s, d = x.shape
    depth = w_ada.shape[0]
    assert d % LANES == 0
    tm = min(TOKEN_TILE, s)

    mod = _adaln(c, w_ada, b_ada)
    bias = _bias_tiles(rel_bias, DSA_QUERY_BLOCK, DSA_KEY_BLOCK)
    rw = jnp.pad(router_w, ((0, 0), (0, LANES - N_EXPERTS)))
    rwh = rw.astype(BF16)
    rw2 = jnp.concatenate([rwh, (rw - rwh.astype(F32)).astype(BF16)], axis=1)
    rb = jnp.pad(router_b, (0, LANES - N_EXPERTS)).reshape(1, LANES)
    wg, wu, wd = w_gate.astype(BF16), w_up.astype(BF16), w_down.astype(BF16)
    woa, wob, wout = w_o_a.astype(BF16), w_o_b.astype(BF16), w_out.astype(BF16)
    w_packed = _pack_w_in(w_in)

    for l in range(depth):
        qa, ka, va, qbt, qit, kb, vt, ki, wit, sga, sgb = _in_proj(
            x, mod[l], norm1_g[l].reshape(1, d), w_packed, l,
            q_norm_g[l].reshape(HEAD_DIM, 1), k_norm_g[l].reshape(1, HEAD_DIM), tm)
        oa = _sb_attention(qa, ka, va)
        ob = _dsa_attention(qit, ki, wit, qbt, kb, vt, bias)
        x1, h2x, info, counts = _merge(oa, ob, sga, sgb, x, mod[l], woa, wob, wout, l,
                                       norm2_g[l].reshape(1, d), rw2, rb, tm)
        x = _moe(h2x, info, counts, wg, wu, wd, l, x1, mod[l])
    return x
```
